```python
import jax
import jax.numpy as jnp
from jax import lax
import numpy as np

D_MODEL = 1024
BATCH = 32
SEQ = 256
DEPTH = 1
DEC_BATCH = 4
DEC_SEQ = 2048
PAST_LEN = 256

GRID_W = 64
MIX_W = D_MODEL
GLA_WIDTH = 512
GLA_HEADS = 4
GLA_DK = 64
GLA_DV = 128
GLA_LORA = 16
GLA_GATE_NORM = 16.0
GLA_CHUNK = 64
SWA_WIDTH = 512
SWA_HEAD_DIM = 64
SWA_HEADS = 8
SWA_KV_HEADS = 2
SWA_GROUP = 4
WINDOW = 128
ATTN_BLOCK = 128
ROPE_BASE = 10000.0
N_EXPERTS = 64
TOP_K = 8
N_EXPERT_GROUPS = 8
TOPK_GROUPS = 4
EXPERT_FF = 128
SHARED_FF = 256
ROUTED_SCALE = 2.5
EPS = 1e-6
IN_SIZES = (256, 256, 512, 512, 32, 512, 128, 128)
IN_COLS = 2336

kernel_name = "hybrid_gla_swa_moe_diffusion_step"


def rms_norm(x, g):
    xf = x.astype(jnp.float32)
    y = xf * lax.rsqrt(jnp.mean(xf * xf, axis=-1, keepdims=True) + EPS)
    return (y * g.astype(jnp.float32)).astype(x.dtype)


def adaln_params(cond, w_ada, b_ada):
    mod = jax.nn.silu(cond) @ w_ada + b_ada
    return [m[:, None, :] for m in jnp.split(mod, 6, axis=-1)]


def modulate(xn, shift, scale):
    return xn * (1 + scale) + shift


def split_cols(h):
    offs = np.cumsum(IN_SIZES)[:-1].tolist()
    return jnp.split(h, offs, axis=-1)


def axial_rope(x):
    t = x.shape[1]
    rows = t // GRID_W
    row = jnp.repeat(jnp.arange(rows), GRID_W)
    col = jnp.tile(jnp.arange(GRID_W), rows)
    half = SWA_HEAD_DIM // 2
    quarter = half // 2
    inv_freq = ROPE_BASE ** (-jnp.arange(quarter, dtype=jnp.float32) / quarter)

    def rot(xa, pos):
        ang = pos.astype(jnp.float32)[:, None] * inv_freq[None, :]
        cos = jnp.cos(ang)[None, :, None, :]
        sin = jnp.sin(ang)[None, :, None, :]
        x1, x2 = xa[..., :quarter], xa[..., quarter:]
        return jnp.concatenate([x1 * cos - x2 * sin, x1 * sin + x2 * cos], axis=-1)

    xf = x.astype(jnp.float32)
    out = jnp.concatenate([rot(xf[..., :half], row), rot(xf[..., half:], col)], axis=-1)
    return out.astype(x.dtype)


def sink_softmax(scores, sink, mask=None):
    if mask is not None:
        scores = jnp.where(mask, scores, -jnp.inf)
    m = jnp.maximum(jnp.max(scores, axis=-1, keepdims=True), sink)
    p = jnp.exp(scores - m)
    return p / (jnp.sum(p, axis=-1, keepdims=True) + jnp.exp(sink - m))


def gla_chunked(q, k, v, log_a, s0):
    b, t, h, dk = q.shape
    dv = v.shape[-1]
    n = t // GLA_CHUNK

    def to_chunks(z):
        return z.astype(jnp.float32).reshape(b, n, GLA_CHUNK, h, z.shape[-1]).transpose(1, 0, 3, 2, 4)

    causal = jnp.tril(jnp.ones((GLA_CHUNK, GLA_CHUNK), dtype=bool))

    def step(s, inp):
        qc, kc, vc, ac = inp
        cum = jnp.cumsum(ac, axis=2)
        o_inter = jnp.einsum('bhtk,bhkv->bhtv', qc * jnp.exp(cum), s)
        diff = cum[:, :, :, None, :] - cum[:, :, None, :, :]
        decay = jnp.exp(jnp.where(causal[None, None, :, :, None], diff, -jnp.inf))
        att = jnp.einsum('bhtk,bhsk,bhtsk->bhts', qc, kc, decay)
        o_intra = jnp.einsum('bhts,bhsv->bhtv', att, vc)
        last = cum[:, :, -1:, :]
        s_new = jnp.exp(last[:, :, 0, :, None]) * s + jnp.einsum(
            'bhsk,bhsv->bhkv', kc * jnp.exp(last - cum), vc)
        return s_new, o_inter + o_intra

    s_fin, o = lax.scan(step, s0.astype(jnp.float32),
                        (to_chunks(q), to_chunks(k), to_chunks(v), to_chunks(log_a)))
    o = o.transpose(1, 0, 3, 2, 4).reshape(b, t, h, dv)
    return o, s_fin


def gla_branch(q, k, v, g, lora, s0_f, s0_b, wa_f, ba_f, wa_b, ba_b, norm_g):
    b, t, _ = q.shape
    q = q.reshape(b, t, GLA_HEADS, GLA_DK) * (GLA_DK ** -0.5)
    k = k.reshape(b, t, GLA_HEADS, GLA_DK)
    v = v.reshape(b, t, GLA_HEADS, GLA_DV)
    lora_f, lora_b = jnp.split(lora, 2, axis=-1)

    def log_gate(z, wa, ba):
        pre = (z @ wa + ba).astype(jnp.float32)
        return (jax.nn.log_sigmoid(pre) / GLA_GATE_NORM).reshape(b, t, GLA_HEADS, GLA_DK)

    la_f = log_gate(lora_f, wa_f, ba_f)
    la_b = log_gate(lora_b, wa_b, ba_b)
    o_f, s_f = gla_chunked(q, k, v, la_f, s0_f)
    flip = lambda z: jnp.flip(z, axis=1)
    o_b, s_b = gla_chunked(flip(q), flip(k), flip(v), flip(la_b), s0_b)
    o = rms_norm(o_f + flip(o_b), norm_g)
    gate = jax.nn.silu(g.reshape(b, t, GLA_HEADS, GLA_DV).astype(jnp.float32))
    out = (o * gate).reshape(b, t, GLA_WIDTH).astype(g.dtype)
    return out, s_f.astype(v.dtype), s_b.astype(v.dtype)


def context_attention(q, k, v, sink):
    b, t = q.shape[:2]
    nb = t // ATTN_BLOCK
    qb = q.reshape(b, nb, ATTN_BLOCK, SWA_KV_HEADS, SWA_GROUP, SWA_HEAD_DIM).transpose(1, 0, 2, 3, 4, 5)
    sink_b = sink.astype(jnp.float32).reshape(SWA_KV_HEADS, SWA_GROUP)[None, :, :, None, None]
    scale = SWA_HEAD_DIM ** -0.5

    def one_block(qi):
        s = jnp.einsum('bqkgd,bskd->bkgqs', qi, k, preferred_element_type=jnp.float32) * scale
        p = sink_softmax(s, sink_b)
        return jnp.einsum('bkgqs,bskd->bqkgd', p.astype(v.dtype), v)

    o = lax.map(one_block, qb)
    return o.transpose(1, 0, 2, 3, 4, 5).reshape(b, t, SWA_WIDTH)


def latent_attention(q, k, v, k_ctx, v_ctx, sink):
    b, t = q.shape[:2]
    nb = t // ATTN_BLOCK
    lc = k_ctx.shape[1]
    qb = q.reshape(b, nb, ATTN_BLOCK, SWA_KV_HEADS, SWA_GROUP, SWA_HEAD_DIM)
    pad = ((0, 0), (ATTN_BLOCK, ATTN_BLOCK), (0, 0), (0, 0))
    kb = jnp.pad(k, pad).reshape(b, nb + 2, ATTN_BLOCK, SWA_KV_HEADS, SWA_HEAD_DIM)
    vb = jnp.pad(v, pad).reshape(b, nb + 2, ATTN_BLOCK, SWA_KV_HEADS, SWA_HEAD_DIM)
    kwin = jnp.concatenate([kb[:, :-2], kb[:, 1:-1], kb[:, 2:]], axis=2)
    vwin = jnp.concatenate([vb[:, :-2], vb[:, 1:-1], vb[:, 2:]], axis=2)
    blk = jnp.arange(nb)
    t_abs = blk[:, None] * ATTN_BLOCK + jnp.arange(ATTN_BLOCK)[None, :]
    s_abs = blk[:, None] * ATTN_BLOCK - ATTN_BLOCK + jnp.arange(3 * ATTN_BLOCK)[None, :]
    band = ((jnp.abs(t_abs[:, :, None] - s_abs[:, None, :]) <= WINDOW)
            & (s_abs[:, None, :] >= 0) & (s_abs[:, None, :] < t))
    mask = jnp.concatenate([jnp.ones((nb, ATTN_BLOCK, lc), dtype=bool), band], axis=-1)
    mask = mask[None, :, None, None, :, :]
    scale = SWA_HEAD_DIM ** -0.5
    s_ctx = jnp.einsum('bnqkgd,bskd->bnkgqs', qb, k_ctx, preferred_element_type=jnp.float32)
    s_band = jnp.einsum('bnqkgd,bnskd->bnkgqs', qb, kwin, preferred_element_type=jnp.float32)
    scores = jnp.concatenate([s_ctx, s_band], axis=-1) * scale
    sink_b = sink.astype(jnp.float32).reshape(SWA_KV_HEADS, SWA_GROUP)[None, None, :, :, None, None]
    p = sink_softmax(scores, sink_b, mask).astype(v.dtype)
    o = (jnp.einsum('bnkgqs,bskd->bnqkgd', p[..., :lc], v_ctx)
         + jnp.einsum('bnkgqs,bnskd->bnqkgd', p[..., lc:], vwin))
    return o.reshape(b, t, SWA_WIDTH)


def moe(x, router_w, router_bias, wg, wu, wd, swg, swu, swd):
    b, t, d = x.shape
    xt = x.reshape(b * t, d)
    scores = jax.nn.sigmoid((xt @ router_w).astype(jnp.float32))
    sel = scores + router_bias.astype(jnp.float32)
    grouped = sel.reshape(b * t, N_EXPERT_GROUPS, N_EXPERTS // N_EXPERT_GROUPS)
    group_score = jnp.sum(lax.top_k(grouped, 2)[0], axis=-1)
    _, top_groups = lax.top_k(group_score, TOPK_GROUPS)
    gmask = jnp.any(top_groups[..., None] == jnp.arange(N_EXPERT_GROUPS), axis=-2)
    emask = jnp.repeat(gmask, N_EXPERTS // N_EXPERT_GROUPS, axis=-1)
    _, idx = lax.top_k(jnp.where(emask, sel, -jnp.inf), TOP_K)
    w = jnp.take_along_axis(scores, idx, axis=-1)
    w = w / jnp.sum(w, axis=-1, keepdims=True) * ROUTED_SCALE
    gates = jnp.sum((idx[..., None] == jnp.arange(N_EXPERTS)).astype(jnp.float32) * w[..., None], axis=-2)
    h = jax.nn.silu(jnp.einsum('td,edf->tef', xt, wg)) * jnp.einsum('td,edf->tef', xt, wu)
    routed = jnp.einsum('tef,efd->td', h * gates[..., None].astype(h.dtype), wd)
    shared = (jax.nn.silu(xt @ swg) * (xt @ swu)) @ swd
    return (routed + shared).reshape(b, t, d)


def setup_inputs(seed: int = 0) -> dict:
    key = jax.random.key(seed)
    ks = jax.random.split(key, 32)
    L, D = DEPTH, D_MODEL

    def nrm(k, shape, scale):
        return jax.random.normal(k, shape, jnp.float32) * scale

    return {
        "x_prompt": nrm(ks[0], (BATCH, SEQ, D), 1.0),
        "x_sample": nrm(ks[1], (DEC_BATCH, DEC_SEQ, D), 1.0),
        "c": nrm(ks[2], (DEC_BATCH, D), 1.0),
        "cache_swa_k": nrm(ks[3], (DEC_BATCH, L, PAST_LEN, SWA_KV_HEADS, SWA_HEAD_DIM), 1.0),
        "cache_swa_v": nrm(ks[4], (DEC_BATCH, L, PAST_LEN, SWA_KV_HEADS, SWA_HEAD_DIM), 1.0),
        "state_gla_fwd": nrm(ks[5], (DEC_BATCH, L, GLA_HEADS, GLA_DK, GLA_DV), 0.5),
        "state_gla_bwd": nrm(ks[6], (DEC_BATCH, L, GLA_HEADS, GLA_DK, GLA_DV), 0.5),
        "c_ctx": nrm(ks[7], (D,), 1.0),
        "w_ada": nrm(ks[8], (L, D, 6 * D), 0.5 * D ** -0.5),
        "b_ada": nrm(ks[9], (L, 6 * D), 0.02),
        "norm_attn_g": 1.0 + nrm(ks[10], (L, D), 0.05),
        "norm_ffn_g": 1.0 + nrm(ks[11], (L, D), 0.05),
        "w_in": nrm(ks[12], (L, D, IN_COLS), D ** -0.5),
        "gla_wa_f": nrm(ks[13], (L, GLA_LORA, GLA_HEADS * GLA_DK), GLA_LORA ** -0.5),
        "gla_ba_f": nrm(ks[14], (L, GLA_HEADS * GLA_DK), 0.1),
        "gla_wa_b": nrm(ks[15], (L, GLA_LORA, GLA_HEADS * GLA_DK), GLA_LORA ** -0.5),
        "gla_ba_b": nrm(ks[16], (L, GLA_HEADS * GLA_DK), 0.1),
        "gla_norm_g": 1.0 + nrm(ks[17], (L, GLA_DV), 0.05),
        "swa_sink": nrm(ks[18], (L, SWA_HEADS), 0.5),
        "w_out": nrm(ks[19], (L, MIX_W, D), MIX_W ** -0.5),
        "router_w": nrm(ks[20], (L, D, N_EXPERTS), D ** -0.5),
        "router_bias": nrm(ks[21], (L, N_EXPERTS), 0.01),
        "exp_w_gate": nrm(ks[22], (L, N_EXPERTS, D, EXPERT_FF), D ** -0.5),
        "exp_w_up": nrm(ks[23], (L, N_EXPERTS, D, EXPERT_FF), D ** -0.5),
        "exp_w_down": nrm(ks[24], (L, N_EXPERTS, EXPERT_FF, D), EXPERT_FF ** -0.5),
        "sh_w_gate": nrm(ks[25], (L, D, SHARED_FF), D ** -0.5),
        "sh_w_up": nrm(ks[26], (L, D, SHARED_FF), D ** -0.5),
        "sh_w_down": nrm(ks[27], (L, SHARED_FF, D), SHARED_FF ** -0.5),
        "final_norm_g": 1.0 + nrm(ks[28], (D,), 0.05),
    }


def reference(x_prompt, x_sample, c, cache_swa_k, cache_swa_v, state_gla_fwd, state_gla_bwd,
              c_ctx, w_ada, b_ada, norm_attn_g, norm_ffn_g, w_in, gla_wa_f, gla_ba_f,
              gla_wa_b, gla_ba_b, gla_norm_g, swa_sink, w_out, router_w, router_bias,
              exp_w_gate, exp_w_up, exp_w_down, sh_w_gate, sh_w_up, sh_w_down, final_norm_g):
    xc = x_prompt
    xl = x_sample
    new_k, new_v, new_sf, new_sb = [], [], [], []
    for l in range(DEPTH):
        gla_p = (gla_wa_f[l], gla_ba_f[l], gla_wa_b[l], gla_ba_b[l], gla_norm_g[l])
        moe_p = (router_w[l], router_bias[l], exp_w_gate[l], exp_w_up[l], exp_w_down[l],
                 sh_w_gate[l], sh_w_up[l], sh_w_down[l])

        b, t, _ = xc.shape
        sh1, sc1, g1, sh2, sc2, g2 = adaln_params(c_ctx[None, :], w_ada[l], b_ada[l])
        h = modulate(rms_norm(xc, norm_attn_g[l]), sh1, sc1)
        q_g, k_g, v_g, g_g, lora, q_s, k_s, v_s = split_cols(h @ w_in[l])
        s_zero = jnp.zeros((b, GLA_HEADS, GLA_DK, GLA_DV), jnp.float32)
        gla_out, s_f, s_b = gla_branch(q_g, k_g, v_g, g_g, lora, s_zero, s_zero, *gla_p)
        kc = k_s.reshape(b, t, SWA_KV_HEADS, SWA_HEAD_DIM)
        vc = v_s.reshape(b, t, SWA_KV_HEADS, SWA_HEAD_DIM)
        attn_out = context_attention(q_s.reshape(b, t, SWA_HEADS, SWA_HEAD_DIM), kc, vc, swa_sink[l])
        xc = xc + g1 * (jnp.concatenate([gla_out, attn_out], axis=-1) @ w_out[l])
        xc = xc + g2 * moe(modulate(rms_norm(xc, norm_ffn_g[l]), sh2, sc2), *moe_p)
        new_k.append(kc)
        new_v.append(vc)
        new_sf.append(s_f)
        new_sb.append(s_b)

        b, t, _ = xl.shape
        sh1, sc1, g1, sh2, sc2, g2 = adaln_params(c, w_ada[l], b_ada[l])
        h = modulate(rms_norm(xl, norm_attn_g[l]), sh1, sc1)
        q_g, k_g, v_g, g_g, lora, q_s, k_s, v_s = split_cols(h @ w_in[l])
        gla_out, _, _ = gla_branch(q_g, k_g, v_g, g_g, lora,
                                   state_gla_fwd[:, l], state_gla_bwd[:, l], *gla_p)
        ql = axial_rope(q_s.reshape(b, t, SWA_HEADS, SWA_HEAD_DIM))
        kl = axial_rope(k_s.reshape(b, t, SWA_KV_HEADS, SWA_HEAD_DIM))
        vl = v_s.reshape(b, t, SWA_KV_HEADS, SWA_HEAD_DIM)
        attn_out = latent_attention(ql, kl, vl, cache_swa_k[:, l], cache_swa_v[:, l], swa_sink[l])
        xl = xl + g1 * (jnp.concatenate([gla_out, attn_out], axis=-1) @ w_out[l])
        xl = xl + g2 * moe(modulate(rms_norm(xl, norm_ffn_g[l]), sh2, sc2), *moe_p)

    y_prompt = rms_norm(xc, final_norm_g)
    y_sample = rms_norm(xl, final_norm_g)
    new_cache_swa_k = jnp.stack(new_k, axis=1)
    new_cache_swa_v = jnp.stack(new_v, axis=1)
    new_state_gla_fwd = jnp.stack(new_sf, axis=1)
    new_state_gla_bwd = jnp.stack(new_sb, axis=1)
    return (y_prompt, y_sample, new_cache_swa_k, new_cache_swa_v, new_state_gla_fwd, new_state_gla_bwd)
```

```python
import functools

import jax
import jax.numpy as jnp
from jax import lax
from jax.experimental import pallas as pl
from jax.experimental.pallas import tpu as pltpu

F32 = jnp.float32
BF16 = jnp.bfloat16

D_MODEL = 1024
GLA_HEADS = 4
GLA_DK = 64
GLA_DV = 128
GLA_LORA = 16
GLA_GATE_NORM = 16.0
GLA_CHUNK = 64
GLA_QK = GLA_HEADS * GLA_DK
GLA_V = GLA_HEADS * GLA_DV
SWA_HEAD_DIM = 64
SWA_HEADS = 8
SWA_KV_HEADS = 2
SWA_Q = SWA_HEADS * SWA_HEAD_DIM
SWA_KV = SWA_KV_HEADS * SWA_HEAD_DIM
ATTN_BLOCK = 128
GRID_W = 64
ROPE_BASE = 10000.0
N_EXPERTS = 64
TOP_K = 8
N_EXPERT_GROUPS = 8
TOPK_GROUPS = 4
EXPERT_FF = 128
SHARED_FF = 256
ROUTED_SCALE = 2.5
EPS = 1e-6

LANES = 128
VMEM_LIMIT = 56 * 1024 * 1024

NEG_INF = float("-inf")


def _dot(a, b):
    return jnp.dot(a, b, preferred_element_type=F32)


def _dot_nt(a, b):
    return lax.dot_general(a, b, (((1,), (1,)), ((), ())), preferred_element_type=F32)


def _split_hi_lo(x):
    hi = x.astype(BF16)
    lo = (x - hi.astype(F32)).astype(BF16)
    return hi, lo


def _sigmoid(x):
    return 1.0 / (1.0 + jnp.exp(-x))


def _silu(x):
    return x * _sigmoid(x)


def _rms_norm(x, g):
    ms = jnp.mean(x * x, axis=-1, keepdims=True)
    return x * lax.rsqrt(ms + EPS) * g


def _adaln_kernel(c_ref, w_ref, b_ref, o_ref):
    a_hi, a_lo = _split_hi_lo(_silu(c_ref[...]))
    w_hi, w_lo = _split_hi_lo(w_ref[...])
    o_ref[...] = _dot(a_hi, w_hi) + _dot(a_lo, w_hi) + _dot(a_hi, w_lo) + b_ref[...]


def _adaln(cond8, w_ada, b_ada):
    n = w_ada.shape[1]
    tn = 1536
    return pl.pallas_call(
        _adaln_kernel,
        out_shape=jax.ShapeDtypeStruct((8, n), F32),
        grid=(n // tn,),
        in_specs=[pl.BlockSpec((8, D_MODEL), lambda j: (0, 0)),
                  pl.BlockSpec((D_MODEL, tn), lambda j: (0, j)),
                  pl.BlockSpec((1, tn), lambda j: (0, j))],
        out_specs=pl.BlockSpec((8, tn), lambda j: (0, j)),
        compiler_params=pltpu.CompilerParams(dimension_semantics=("arbitrary",),
                                             vmem_limit_bytes=VMEM_LIMIT),
        name="adaln",
    )(cond8, w_ada, b_ada)


def _inproj_kernel(x_ref, g_ref, sh_ref, sc_ref, wg_ref, wl_ref, ws_ref,
                   gla_ref, lora_ref, q_ref, k_ref, v_ref):
    h = _rms_norm(x_ref[...], g_ref[...]) * (1.0 + sc_ref[...]) + sh_ref[...]
    hb = h.astype(BF16)
    gla_ref[...] = _dot(hb, wg_ref[...])
    lora_ref[...] = _dot(hb, wl_ref[...])
    s = _dot(hb, ws_ref[...])
    q_ref[...] = s[:, :SWA_Q]
    k_ref[...] = s[:, SWA_Q:SWA_Q + SWA_KV]
    v_ref[...] = s[:, SWA_Q + SWA_KV:]


def _inproj(x, g, sh, sc, w_gla, w_lora, w_swa, *, tm):
    b, t, d = x.shape
    nmod = sh.shape[0]
    mod_map = (lambda i, j: (i, 0, 0)) if nmod > 1 else (lambda i, j: (0, 0, 0))
    row = lambda i, j: (i, j, 0)
    full = lambda i, j: (0, 0)
    n_gla = w_gla.shape[1]
    n_lora = w_lora.shape[1]
    return pl.pallas_call(
        _inproj_kernel,
        out_shape=(jax.ShapeDtypeStruct((b, t, n_gla), F32),
                   jax.ShapeDtypeStruct((b, t, n_lora), F32),
                   jax.ShapeDtypeStruct((b, t, SWA_Q), F32),
                   jax.ShapeDtypeStruct((b, t, SWA_KV), F32),
                   jax.ShapeDtypeStruct((b, t, SWA_KV), F32)),
        grid=(b, t // tm),
        in_specs=[pl.BlockSpec((None, tm, d), row),
                  pl.BlockSpec((1, d), full),
                  pl.BlockSpec((None, 1, d), mod_map),
                  pl.BlockSpec((None, 1, d), mod_map),
                  pl.BlockSpec((d, n_gla), full),
                  pl.BlockSpec((d, n_lora), full),
                  pl.BlockSpec((d, w_swa.shape[1]), full)],
        out_specs=(pl.BlockSpec((None, tm, n_gla), row),
                   pl.BlockSpec((None, tm, n_lora), row),
                   pl.BlockSpec((None, tm, SWA_Q), row),
                   pl.BlockSpec((None, tm, SWA_KV), row),
                   pl.BlockSpec((None, tm, SWA_KV), row)),
        compiler_params=pltpu.CompilerParams(dimension_semantics=("arbitrary", "arbitrary"),
                                             vmem_limit_bytes=VMEM_LIMIT),
        name="inproj",
    )(x, g, sh, sc, w_gla, w_lora, w_swa)


def _log_sigmoid(x):
    return -(jnp.maximum(-x, 0.0) + jnp.log1p(jnp.exp(-jnp.abs(x))))


def _heads_to_rows(x):
    return jnp.concatenate([x[:, h * LANES:(h + 1) * LANES] for h in range(GLA_HEADS)], axis=0)


def _rows_to_heads(x, c):
    return jnp.concatenate([x[h * c:(h + 1) * c, :] for h in range(GLA_HEADS)], axis=1)


def _gla_kernel(has_init, q_ref, k_ref, v_ref, g_ref, lora_ref, waf_ref, baf_ref, wab_ref, bab_ref,
                ng_ref, *rest):
    if has_init:
        s0f_ref, s0b_ref, out_ref, sf_ref, sb_ref, laf_ref, lab_ref, o_ref, stf_ref, stb_ref = rest
    else:
        out_ref, sf_ref, sb_ref, laf_ref, lab_ref, o_ref, stf_ref, stb_ref = rest
    t = q_ref.shape[0]
    c = GLA_CHUNK
    n = t // c
    hc = GLA_HEADS * c

    lora = lora_ref[...].astype(BF16)
    laf_ref[...] = _log_sigmoid(_dot(lora, waf_ref[...]) + baf_ref[...]) * (1.0 / GLA_GATE_NORM)
    lab_ref[...] = _log_sigmoid(_dot(lora, wab_ref[...]) + bab_ref[...]) * (1.0 / GLA_GATE_NORM)

    if has_init:
        stf_ref[...] = s0f_ref[...].T
        stb_ref[...] = s0b_ref[...].T
    else:
        stf_ref[...] = jnp.zeros_like(stf_ref)
        stb_ref[...] = jnp.zeros_like(stb_ref)

    r64 = lax.broadcasted_iota(jnp.int32, (c, c), 0)
    c64 = lax.broadcasted_iota(jnp.int32, (c, c), 1)
    tri_f = jnp.where(c64 <= r64, 1.0, 0.0).astype(BF16)
    tri_b = jnp.where(c64 >= r64, 1.0, 0.0).astype(BF16)
    rr = lax.broadcasted_iota(jnp.int32, (hc, hc), 0)
    cc = lax.broadcasted_iota(jnp.int32, (hc, hc), 1)
    same_head = (rr >> 6) == (cc >> 6)
    keep_f = same_head & ((rr & (c - 1)) >= (cc & (c - 1)))
    keep_b = same_head & ((rr & (c - 1)) <= (cc & (c - 1)))
    norm_g = ng_ref[...]

    def tile_heads(x):
        x4 = jnp.concatenate([x] * GLA_HEADS, axis=0)
        return jnp.where(same_head, x4, 0.0).astype(BF16)

    def direction(ci, la_ref, st_ref, tri, keep, last_row):
        sl = pl.ds(pl.multiple_of(ci * c, c), c)
        la = la_ref[sl, :]
        la_hi, la_lo = _split_hi_lo(la)
        cum = _dot(tri, la_hi) + _dot(tri, la_lo)
        tot = cum[last_row:last_row + 1, :]
        qc = q_ref[sl, :]
        kc = k_ref[sl, :]
        q4 = tile_heads(qc * (GLA_DK ** -0.5) * jnp.exp(cum))
        k4 = tile_heads(kc * jnp.exp(-cum))
        kd4 = tile_heads(kc * jnp.exp(tot - cum))
        v_rows = _heads_to_rows(v_ref[sl, :])
        att = jnp.where(keep, _dot_nt(q4, k4), 0.0).astype(BF16)
        st = st_ref[...]
        o = _dot(att, v_rows.astype(BF16)) + _dot_nt(q4, st.astype(BF16))
        st_ref[...] = jnp.exp(tot) * st + _dot(v_rows.T.astype(BF16), kd4)
        return o

    def finalize(ci, o):
        sl = pl.ds(pl.multiple_of(ci * c, c), c)
        on = _rms_norm(o, norm_g)
        gate = _silu(_heads_to_rows(g_ref[sl, :]))
        out_ref[sl, :] = _rows_to_heads(on * gate, c)

    half = n // 2

    def first_half(i, carry):
        j = n - 1 - i
        o_ref[i] = direction(i, laf_ref, stf_ref, tri_f, keep_f, c - 1)
        o_ref[j] = direction(j, lab_ref, stb_ref, tri_b, keep_b, 0)
        return carry

    def second_half(i, carry):
        j = n - 1 - i
        finalize(i, direction(i, laf_ref, stf_ref, tri_f, keep_f, c - 1) + o_ref[i])
        finalize(j, direction(j, lab_ref, stb_ref, tri_b, keep_b, 0) + o_ref[j])
        return carry

    lax.fori_loop(0, half, first_half, 0)
    lax.fori_loop(half, n, second_half, 0)
    sf_ref[...] = stf_ref[...].T
    sb_ref[...] = stb_ref[...].T


def _gla(gla_in, lora, waf, baf, wab, bab, norm_g, s0f=None, s0b=None):
    b, t, _ = gla_in.shape
    has_init = s0f is not None
    n = t // GLA_CHUNK
    bmap = lambda i: (i, 0, 0)
    full = lambda i: (0, 0)
    in_specs = [pl.BlockSpec((None, t, GLA_QK), lambda i: (i, 0, 0)),
                pl.BlockSpec((None, t, GLA_QK), lambda i: (i, 0, 1)),
                pl.BlockSpec((None, t, GLA_V), lambda i: (i, 0, 1)),
                pl.BlockSpec((None, t, GLA_V), lambda i: (i, 0, 2)),
                pl.BlockSpec((None, t, 2 * GLA_LORA), bmap),
                pl.BlockSpec((2 * GLA_LORA, GLA_QK), full),
                pl.BlockSpec((1, GLA_QK), full),
                pl.BlockSpec((2 * GLA_LORA, GLA_QK), full),
                pl.BlockSpec((1, GLA_QK), full),
                pl.BlockSpec((1, GLA_DV), full)]
    args = [gla_in, gla_in, gla_in, gla_in, lora, waf, baf, wab, bab, norm_g]
    if has_init:
        in_specs += [pl.BlockSpec((None, GLA_QK, GLA_DV), bmap)] * 2
        args += [s0f, s0b]
    return pl.pallas_call(
        functools.partial(_gla_kernel, has_init),
        out_shape=(jax.ShapeDtypeStruct((b, t, GLA_V), F32),
                   jax.ShapeDtypeStruct((b, GLA_QK, GLA_DV), F32),
                   jax.ShapeDtypeStruct((b, GLA_QK, GLA_DV), F32)),
        grid=(b,),
        in_specs=in_specs,
        out_specs=(pl.BlockSpec((None, t, GLA_V), bmap),
                   pl.BlockSpec((None, GLA_QK, GLA_DV), bmap),
                   pl.BlockSpec((None, GLA_QK, GLA_DV), bmap)),
        scratch_shapes=[pltpu.VMEM((t, GLA_QK), F32),
                        pltpu.VMEM((t, GLA_QK), F32),
                        pltpu.VMEM((n, GLA_HEADS * GLA_CHUNK, GLA_DV), F32),
                        pltpu.VMEM((GLA_DV, GLA_QK), F32),
                        pltpu.VMEM((GLA_DV, GLA_QK), F32)],
        compiler_params=pltpu.CompilerParams(dimension_semantics=("arbitrary",),
                                             vmem_limit_bytes=VMEM_LIMIT),
        name="gla",
    )(*args)


def _dup_groups(x):
    lo = lax.broadcasted_iota(jnp.int32, x.shape, 1) < SWA_HEAD_DIM
    xr = pltpu.roll(x, SWA_HEAD_DIM, axis=1)
    return jnp.where(lo, x, xr).astype(BF16), jnp.where(lo, xr, x).astype(BF16)


def _split_pair(qp):
    lo = lax.broadcasted_iota(jnp.int32, qp.shape, 1) < SWA_HEAD_DIM
    return jnp.where(lo, qp, 0.0).astype(BF16), jnp.where(lo, 0.0, qp).astype(BF16)


def _merge_pair(o_even, o_odd):
    lo = lax.broadcasted_iota(jnp.int32, o_even.shape, 1) < SWA_HEAD_DIM
    return jnp.where(lo, o_even, o_odd)


def _attn_ctx_kernel(sink_ref, q_ref, k_ref, v_ref, o_ref):
    kd = _dup_groups(k_ref[...])
    vd = _dup_groups(v_ref[...])
    scale = SWA_HEAD_DIM ** -0.5
    for pair in range(SWA_HEADS // 2):
        grp = pair // 2
        qs = _split_pair(q_ref[:, pair * LANES:(pair + 1) * LANES] * scale)
        outs = []
        for par in range(2):
            sink = sink_ref[2 * pair + par]
            s = _dot_nt(qs[par], kd[grp])
            m = jnp.maximum(jnp.max(s, axis=-1, keepdims=True), sink)
            p = jnp.exp(s - m)
            denom = jnp.sum(p, axis=-1, keepdims=True) + jnp.exp(sink - m)
            outs.append(_dot(p.astype(BF16), vd[grp]) / denom)
        o_ref[:, pair * LANES:(pair + 1) * LANES] = _merge_pair(outs[0], outs[1])


def _attn_ctx(sink, q, k, v):
    b, t, _ = q.shape
    bmap = lambda i: (i, 0, 0)
    return pl.pallas_call(
        _attn_ctx_kernel,
        out_shape=jax.ShapeDtypeStruct((b, t, SWA_Q), F32),
        grid=(b,),
        in_specs=[pl.BlockSpec(memory_space=pltpu.SMEM),
                  pl.BlockSpec((None, t, SWA_Q), bmap),
                  pl.BlockSpec((None, t, SWA_KV), bmap),
                  pl.BlockSpec((None, t, SWA_KV), bmap)],
        out_specs=pl.BlockSpec((None, t, SWA_Q), bmap),
        compiler_params=pltpu.CompilerParams(dimension_semantics=("arbitrary",),
                                             vmem_limit_bytes=VMEM_LIMIT),
        name="attn_ctx",
    )(sink, q, k, v)


def _rope(x, cos, sin_lo, sin_hi):
    return x * cos + pltpu.roll(x, LANES - 16, axis=1) * sin_lo + pltpu.roll(x, 16, axis=1) * sin_hi


def _attn_lat_kernel(sink_ref, q_ref, k_ref, v_ref, kc_ref, vc_ref, cos_ref, sl_ref, sh_ref,
                     o_ref, kw_ref, vw_ref):
    t = q_ref.shape[0]
    ab = ATTN_BLOCK
    nb = t // ab
    scale = SWA_HEAD_DIM ** -0.5

    k_rot = _dup_groups(_rope(k_ref[...], cos_ref[...], sl_ref[...], sh_ref[...]))
    v_dup = _dup_groups(v_ref[...])
    zeros = jnp.zeros((ab, LANES), BF16)
    for grp in range(SWA_KV_HEADS):
        kw_ref[grp, 0:ab, :] = zeros
        kw_ref[grp, ab:ab + t, :] = k_rot[grp]
        kw_ref[grp, ab + t:, :] = zeros
        vw_ref[grp, 0:ab, :] = zeros
        vw_ref[grp, ab:ab + t, :] = v_dup[grp]
        vw_ref[grp, ab + t:, :] = zeros
    kc = _dup_groups(kc_ref[...])
    vc = _dup_groups(vc_ref[...])

    tq = lax.broadcasted_iota(jnp.int32, (4 * ab, 3 * ab), 0) & (ab - 1)
    sk = lax.broadcasted_iota(jnp.int32, (4 * ab, 3 * ab), 1)
    band = jnp.abs(tq + ab - sk) <= ab

    def block(nq, carry):
        row0 = pl.multiple_of(nq * ab, ab)
        s_abs = sk + (nq - 1) * ab
        mask = band & (s_abs >= 0) & (s_abs < t)
        cos = cos_ref[pl.ds(row0, ab), :]
        s_lo = sl_ref[pl.ds(row0, ab), :]
        s_hi = sh_ref[pl.ds(row0, ab), :]
        for grp in range(SWA_KV_HEADS):
            qs = []
            sinks = []
            for pp in range(2):
                pair = 2 * grp + pp
                qp = _rope(q_ref[pl.ds(row0, ab), pair * LANES:(pair + 1) * LANES], cos, s_lo, s_hi)
                qs.extend(_split_pair(qp * scale))
                sinks += [jnp.full((ab, 1), sink_ref[2 * pair], F32),
                          jnp.full((ab, 1), sink_ref[2 * pair + 1], F32)]
            q4 = jnp.concatenate(qs, axis=0)
            sink = jnp.concatenate(sinks, axis=0)
            kwin = kw_ref[grp, pl.ds(row0, 3 * ab), :]
            vwin = vw_ref[grp, pl.ds(row0, 3 * ab), :]
            s_ctx = _dot_nt(q4, kc[grp])
            s_win = jnp.where(mask, _dot_nt(q4, kwin), NEG_INF)
            m = jnp.maximum(jnp.maximum(jnp.max(s_ctx, axis=-1, keepdims=True),
                                        jnp.max(s_win, axis=-1, keepdims=True)), sink)
            p_ctx = jnp.exp(s_ctx - m)
            p_win = jnp.exp(s_win - m)
            denom = (jnp.sum(p_ctx, axis=-1, keepdims=True) + jnp.sum(p_win, axis=-1, keepdims=True)
                     + jnp.exp(sink - m))
            o = (_dot(p_ctx.astype(BF16), vc[grp]) + _dot(p_win.astype(BF16), vwin)) / denom
            for pp in range(2):
                pair = 2 * grp + pp
                o_ref[pl.ds(row0, ab), pair * LANES:(pair + 1) * LANES] = _merge_pair(
                    o[(2 * pp) * ab:(2 * pp + 1) * ab], o[(2 * pp + 1) * ab:(2 * pp + 2) * ab])
        return carry

    lax.fori_loop(0, nb, block, 0)


def _attn_lat(sink, q, k, v, kc, vc, cos, sin_lo, sin_hi):
    b, t, _ = q.shape
    lc = kc.shape[1]
    bmap = lambda i: (i, 0, 0)
    full = lambda i: (0, 0)
    return pl.pallas_call(
        _attn_lat_kernel,
        out_shape=jax.ShapeDtypeStruct((b, t, SWA_Q), F32),
        grid=(b,),
        in_specs=[pl.BlockSpec(memory_space=pltpu.SMEM),
                  pl.BlockSpec((None, t, SWA_Q), bmap),
                  pl.BlockSpec((None, t, SWA_KV), bmap),
                  pl.BlockSpec((None, t, SWA_KV), bmap),
                  pl.BlockSpec((None, lc, SWA_KV), bmap),
                  pl.BlockSpec((None, lc, SWA_KV), bmap),
                  pl.BlockSpec((t, LANES), full),
                  pl.BlockSpec((t, LANES), full),
                  pl.BlockSpec((t, LANES), full)],
        out_specs=pl.BlockSpec((None, t, SWA_Q), bmap),
        scratch_shapes=[pltpu.VMEM((SWA_KV_HEADS, t + 2 * ATTN_BLOCK, LANES), BF16),
                        pltpu.VMEM((SWA_KV_HEADS, t + 2 * ATTN_BLOCK, LANES), BF16)],
        compiler_params=pltpu.CompilerParams(dimension_semantics=("arbitrary",),
                                             vmem_limit_bytes=VMEM_LIMIT),
        name="attn_lat",
    )(sink, q, k, v, kc, vc, cos, sin_lo, sin_hi)


def _rope_tables(t):
    half = SWA_HEAD_DIM // 2
    quarter = half // 2
    pos = jnp.arange(t)
    row = (pos // GRID_W).astype(F32)
    col = (pos % GRID_W).astype(F32)
    inv_freq = ROPE_BASE ** (-jnp.arange(quarter, dtype=F32) / quarter)
    lane = jnp.arange(LANES)
    d = lane % SWA_HEAD_DIM
    freq = inv_freq[d % quarter]
    use_row = (d < half)
    ang = jnp.where(use_row[None, :], row[:, None], col[:, None]) * freq[None, :]
    cos = jnp.cos(ang)
    sin = jnp.sin(ang)
    lower = (d % half) < quarter
    return cos, jnp.where(lower[None, :], -sin, 0.0), jnp.where(lower[None, :], 0.0, sin)


def _route(sel, scores):
    n = sel.shape[1]
    gsz = N_EXPERTS // N_EXPERT_GROUPS

    def first_max(x, idx, size):
        m = jnp.max(x, axis=0, keepdims=True)
        first = jnp.min(jnp.where(x == m, idx, float(size)), axis=0, keepdims=True)
        return m, idx == first

    i8 = lax.broadcasted_iota(jnp.int32, (gsz, n), 0).astype(F32)
    rows = []
    for g in range(N_EXPERT_GROUPS):
        slab = sel[g * gsz:(g + 1) * gsz, :]
        m1, hit = first_max(slab, i8, gsz)
        m2 = jnp.max(jnp.where(hit, NEG_INF, slab), axis=0, keepdims=True)
        rows.append(m1 + m2)
    gscore = jnp.concatenate(rows, axis=0)
    gsel = jnp.zeros((N_EXPERT_GROUPS, n), F32)
    for _ in range(TOPK_GROUPS):
        _, hit = first_max(gscore, i8, N_EXPERT_GROUPS)
        gsel = jnp.where(hit, 1.0, gsel)
        gscore = jnp.where(hit, NEG_INF, gscore)
    emask = jnp.concatenate(
        [jnp.broadcast_to(gsel[g:g + 1, :], (gsz, n)) for g in range(N_EXPERT_GROUPS)], axis=0)
    cand = jnp.where(emask > 0.5, sel, NEG_INF)
    ie = lax.broadcasted_iota(jnp.int32, (N_EXPERTS, n), 0).astype(F32)
    w = jnp.zeros((N_EXPERTS, n), F32)
    for _ in range(TOP_K):
        _, hit = first_max(cand, ie, N_EXPERTS)
        w = jnp.where(hit, scores, w)
        cand = jnp.where(hit, NEG_INF, cand)
    return w / jnp.sum(w, axis=0, keepdims=True) * ROUTED_SCALE


def _outproj_kernel(gla_ref, att_ref, x_ref, wo_ref, g1_ref, sh_ref, sc_ref, ng_ref, rw_ref, rwh_ref,
                    rb_ref, x1_ref, xm_ref, gates_ref):
    y = (_dot(gla_ref[...].astype(BF16), wo_ref[0:GLA_V, :])
         + _dot(att_ref[...].astype(BF16), wo_ref[GLA_V:, :]))
    x1 = x_ref[...] + g1_ref[...] * y
    x1_ref[...] = x1
    xm = _rms_norm(x1, ng_ref[...]) * (1.0 + sc_ref[...]) + sh_ref[...]
    xm_hi, xm_lo = _split_hi_lo(xm)
    xm_ref[...] = xm_hi
    lg = _dot(xm_hi, rw_ref[...])
    logits = lg[:, :N_EXPERTS] + lg[:, N_EXPERTS:] + _dot(xm_lo, rwh_ref[...])
    tm = logits.shape[0]
    lt = jnp.concatenate([logits, jnp.zeros((tm, LANES - N_EXPERTS), F32)], axis=1).T[:N_EXPERTS, :]
    scores = _sigmoid(lt)
    gates_t = _route(scores + rb_ref[...], scores)
    gates_full = jnp.concatenate([gates_t, jnp.zeros((LANES - N_EXPERTS, tm), F32)], axis=0).T
    gates_ref[...] = gates_full[:, :N_EXPERTS]


def _outproj(gla_out, att_out, x, w_out, g1, sh2, sc2, norm_g, rw_cat, rw_hi, rbias, *, tm):
    b, t, d = x.shape
    nmod = g1.shape[0]
    mod_map = (lambda i, j: (i, 0, 0)) if nmod > 1 else (lambda i, j: (0, 0, 0))
    row = lambda i, j: (i, j, 0)
    full = lambda i, j: (0, 0)
    return pl.pallas_call(
        _outproj_kernel,
        out_shape=(jax.ShapeDtypeStruct((b, t, d), F32),
                   jax.ShapeDtypeStruct((b, t, d), BF16),
                   jax.ShapeDtypeStruct((b, t, N_EXPERTS), F32)),
        grid=(b, t // tm),
        in_specs=[pl.BlockSpec((None, tm, GLA_V), row),
                  pl.BlockSpec((None, tm, SWA_Q), row),
                  pl.BlockSpec((None, tm, d), row),
                  pl.BlockSpec((d, d), full),
                  pl.BlockSpec((None, 1, d), mod_map),
                  pl.BlockSpec((None, 1, d), mod_map),
                  pl.BlockSpec((None, 1, d), mod_map),
                  pl.BlockSpec((1, d), full),
                  pl.BlockSpec((d, 2 * N_EXPERTS), full),
                  pl.BlockSpec((d, N_EXPERTS), full),
                  pl.BlockSpec((N_EXPERTS, 1), full)],
        out_specs=(pl.BlockSpec((None, tm, d), row),
                   pl.BlockSpec((None, tm, d), row),
                   pl.BlockSpec((None, tm, N_EXPERTS), row)),
        compiler_params=pltpu.CompilerParams(dimension_semantics=("arbitrary", "arbitrary"),
                                             vmem_limit_bytes=VMEM_LIMIT),
        name="outproj",
    )(gla_out, att_out, x, w_out, g1, sh2, sc2, norm_g, rw_cat, rw_hi, rbias)


MOE_EB = 4
MOE_TM = 1024


def _moe_kernel(xm_ref, gt_ref, x1_ref, g2_ref, fg_ref, wg_ref, wu_ref, wd_ref, swg_ref, swu_ref, swd_ref,
                o_ref, acc_ref):
    e = pl.program_id(1)
    x = xm_ref[...]

    @pl.when(e == 0)
    def _():
        hs = _silu(_dot(x, swg_ref[...])) * _dot(x, swu_ref[...])
        acc_ref[...] = _dot(hs.astype(BF16), swd_ref[...])

    gcols = gt_ref[...]
    hs = []
    for j in range(MOE_EB):
        wgu = jnp.concatenate([wg_ref[j].astype(BF16), wu_ref[j].astype(BF16)], axis=1)
        ab = _dot(x, wgu)
        h = _silu(ab[:, :EXPERT_FF]) * ab[:, EXPERT_FF:]
        hs.append((h * gcols[:, j:j + 1]).astype(BF16))
    hcat = jnp.concatenate(hs, axis=1)
    wd = wd_ref[...].reshape(MOE_EB * EXPERT_FF, D_MODEL).astype(BF16)
    acc_ref[...] += _dot(hcat, wd)

    @pl.when(e == pl.num_programs(1) - 1)
    def _():
        y = x1_ref[...] + g2_ref[...] * acc_ref[...]
        o_ref[...] = _rms_norm(y, fg_ref[...])


def _moe(xm, gates_t, x1, g2, final_g, wg, wu, wd, swg, swu, swd, *, tiles_per_mod):
    n, d = xm.shape
    tm = MOE_TM
    row = lambda i, e: (i, 0)
    full = lambda i, e: (0, 0)
    mod_map = lambda i, e: (i // tiles_per_mod, 0, 0)
    return pl.pallas_call(
        _moe_kernel,
        out_shape=jax.ShapeDtypeStruct((n, d), F32),
        grid=(n // tm, N_EXPERTS // MOE_EB),
        in_specs=[pl.BlockSpec((tm, d), row),
                  pl.BlockSpec((None, tm, MOE_EB), lambda i, e: (e, i, 0)),
                  pl.BlockSpec((tm, d), row),
                  pl.BlockSpec((None, 1, d), mod_map),
                  pl.BlockSpec((1, d), full),
                  pl.BlockSpec((MOE_EB, d, EXPERT_FF), lambda i, e: (e, 0, 0)),
                  pl.BlockSpec((MOE_EB, d, EXPERT_FF), lambda i, e: (e, 0, 0)),
                  pl.BlockSpec((MOE_EB, EXPERT_FF, d), lambda i, e: (e, 0, 0)),
                  pl.BlockSpec((d, SHARED_FF), full),
                  pl.BlockSpec((d, SHARED_FF), full),
                  pl.BlockSpec((SHARED_FF, d), full)],
        out_specs=pl.BlockSpec((tm, d), row),
        scratch_shapes=[pltpu.VMEM((tm, d), F32)],
        compiler_params=pltpu.CompilerParams(dimension_semantics=("arbitrary", "arbitrary"),
                                             vmem_limit_bytes=VMEM_LIMIT),
        name="moe",
    )(xm, gates_t, x1, g2, final_g, wg, wu, wd, swg, swu, swd)


def _stream(x, mods, p, attn_fn, s0=None):
    b, t, d = x.shape
    sh1, sc1, g1, sh2, sc2, g2 = mods
    gla_in, lora, q_s, k_s, v_s = _inproj(x, p["norm_attn_g"], sh1, sc1, p["w_gla"], p["w_lora"],
                                          p["w_swa"], tm=256)
    if s0 is None:
        gla_out, s_f, s_b = _gla(gla_in, lora, p["waf"], p["baf"], p["wab"], p["bab"], p["gla_norm_g"])
    else:
        gla_out, s_f, s_b = _gla(gla_in, lora, p["waf"], p["baf"], p["wab"], p["bab"], p["gla_norm_g"],
                                 s0[0], s0[1])
    att_out = attn_fn(q_s, k_s, v_s)
    x1, xm, gates = _outproj(gla_out, att_out, x, p["w_out"], g1, sh2, sc2, p["norm_ffn_g"],
                             p["rw_cat"], p["rw_hi"], p["rbias"], tm=256)
    n = b * t
    gates_t = gates.reshape(n, N_EXPERTS // MOE_EB, MOE_EB).transpose(1, 0, 2)
    tiles_per_mod = max(t // MOE_TM, 1) if g2.shape[0] > 1 else (n // MOE_TM)
    y = _moe(xm.reshape(n, d), gates_t, x1.reshape(n, d), g2, p["final_norm_g"],
             p["wg"], p["wu"], p["wd"], p["swg"], p["swu"], p["swd"], tiles_per_mod=tiles_per_mod)
    return y.reshape(b, t, d), k_s, v_s, s_f, s_b


def kernel(x_prompt, x_sample, c, cache_swa_k, cache_swa_v, state_gla_fwd, state_gla_bwd, c_ctx, w_ada, b_ada, norm_attn_g, norm_ffn_g, w_in, gla_wa_f, gla_ba_f, gla_wa_b, gla_ba_b, gla_norm_g, swa_sink, w_out, router_w, router_bias, exp_w_gate, exp_w_up, exp_w_down, sh_w_gate, sh_w_up, sh_w_down, final_norm_g):
    l = 0
    d = D_MODEL
    nb_ctx, t_ctx, _ = x_prompt.shape
    nb_lat, t_lat, _ = x_sample.shape

    pad = jnp.zeros((8 - 1 - nb_lat, d), F32)
    cond8 = jnp.concatenate([c_ctx[None, :], c, pad], axis=0)
    mod = _adaln(cond8, w_ada[l], b_ada[l][None, :])
    mods_ctx = [mod[0:1, i * d:(i + 1) * d][:, None, :] for i in range(6)]
    mods_lat = [mod[1:1 + nb_lat, i * d:(i + 1) * d][:, None, :] for i in range(6)]

    zeros_lora = jnp.zeros((GLA_LORA, GLA_QK), F32)
    rw = router_w[l]
    rw_hi = rw.astype(BF16)
    rw_lo = (rw - rw_hi.astype(F32)).astype(BF16)
    w_in_b = w_in[l].astype(BF16)
    p = {
        "norm_attn_g": norm_attn_g[l][None, :],
        "norm_ffn_g": norm_ffn_g[l][None, :],
        "final_norm_g": final_norm_g[None, :],
        "w_gla": w_in_b[:, :2 * GLA_QK + 2 * GLA_V],
        "w_lora": w_in_b[:, 2 * GLA_QK + 2 * GLA_V:2 * GLA_QK + 2 * GLA_V + 2 * GLA_LORA],
        "w_swa": w_in_b[:, 2 * GLA_QK + 2 * GLA_V + 2 * GLA_LORA:],
        "waf": jnp.concatenate([gla_wa_f[l], zeros_lora], axis=0).astype(BF16),
        "wab": jnp.concatenate([zeros_lora, gla_wa_b[l]], axis=0).astype(BF16),
        "baf": gla_ba_f[l][None, :],
        "bab": gla_ba_b[l][None, :],
        "gla_norm_g": gla_norm_g[l][None, :],
        "w_out": w_out[l].astype(BF16),
        "rw_cat": jnp.concatenate([rw_hi, rw_lo], axis=1),
        "rw_hi": rw_hi,
        "rbias": router_bias[l][:, None],
        "wg": exp_w_gate[l], "wu": exp_w_up[l], "wd": exp_w_down[l],
        "swg": sh_w_gate[l].astype(BF16), "swu": sh_w_up[l].astype(BF16),
        "swd": sh_w_down[l].astype(BF16),
    }
    sink = swa_sink[l]

    y_prompt, k_c, v_c, s_f, s_b = _stream(x_prompt, mods_ctx, p, functools.partial(_attn_ctx, sink))

    cos, sin_lo, sin_hi = _rope_tables(t_lat)
    kc = cache_swa_k[:, l].reshape(nb_lat, -1, SWA_KV)
    vc = cache_swa_v[:, l].reshape(nb_lat, -1, SWA_KV)
    lat_attn = lambda q, k, v: _attn_lat(sink, q, k, v, kc, vc, cos, sin_lo, sin_hi)
    s0 = (state_gla_fwd[:, l].reshape(nb_lat, GLA_QK, GLA_DV),
          state_gla_bwd[:, l].reshape(nb_lat, GLA_QK, GLA_DV))
    y_sample, _, _, _, _ = _stream(x_sample, mods_lat, p, lat_attn, s0)

    new_k = k_c.reshape(nb_ctx, 1, t_ctx, SWA_KV_HEADS, SWA_HEAD_DIM)
    new_v = v_c.reshape(nb_ctx, 1, t_ctx, SWA_KV_HEADS, SWA_HEAD_DIM)
    new_sf = s_f.reshape(nb_ctx, 1, GLA_HEADS, GLA_DK, GLA_DV)
    new_sb = s_b.reshape(nb_ctx, 1, GLA_HEADS, GLA_DK, GLA_DV)
    return (y_prompt, y_sample, new_k, new_v, new_sf, new_sb)
```

```python
import functools

import jax
import jax.numpy as jnp
from jax import lax
from jax.experimental import pallas as pl
from jax.experimental.pallas import tpu as pltpu

F32 = jnp.float32
BF16 = jnp.bfloat16

D_MODEL = 1024
GLA_HEADS = 4
GLA_DK = 64
GLA_DV = 128
GLA_LORA = 16
GLA_GATE_NORM = 16.0
GLA_CHUNK = 64
GLA_QK = GLA_HEADS * GLA_DK
GLA_V = GLA_HEADS * GLA_DV
SWA_HEAD_DIM = 64
SWA_HEADS = 8
SWA_KV_HEADS = 2
SWA_Q = SWA_HEADS * SWA_HEAD_DIM
SWA_KV = SWA_KV_HEADS * SWA_HEAD_DIM
ATTN_BLOCK = 128
GRID_W = 64
ROPE_BASE = 10000.0
N_EXPERTS = 64
TOP_K = 8
N_EXPERT_GROUPS = 8
TOPK_GROUPS = 4
EXPERT_FF = 128
SHARED_FF = 256
ROUTED_SCALE = 2.5
EPS = 1e-6

LANES = 128
VMEM_LIMIT = 56 * 1024 * 1024

NEG_INF = float("-inf")


def _dot(a, b):
    return jnp.dot(a, b, preferred_element_type=F32)


def _dot_nt(a, b):
    return lax.dot_general(a, b, (((1,), (1,)), ((), ())), preferred_element_type=F32)


def _split_hi_lo(x):
    hi = x.astype(BF16)
    lo = (x - hi.astype(F32)).astype(BF16)
    return hi, lo


def _sigmoid(x):
    return 1.0 / (1.0 + jnp.exp(-x))


def _silu(x):
    return x * _sigmoid(x)


def _rms_norm(x, g):
    ms = jnp.mean(x * x, axis=-1, keepdims=True)
    return x * lax.rsqrt(ms + EPS) * g


def _adaln_kernel(c_ref, w_ref, b_ref, o_ref):
    a_hi, a_lo = _split_hi_lo(_silu(c_ref[...]))
    w_hi, w_lo = _split_hi_lo(w_ref[...])
    o_ref[...] = _dot(a_hi, w_hi) + _dot(a_lo, w_hi) + _dot(a_hi, w_lo) + b_ref[...]


def _adaln(cond8, w_ada, b_ada):
    n = w_ada.shape[1]
    tn = 1536
    return pl.pallas_call(
        _adaln_kernel,
        out_shape=jax.ShapeDtypeStruct((8, n), F32),
        grid=(n // tn,),
        in_specs=[pl.BlockSpec((8, D_MODEL), lambda j: (0, 0)),
                  pl.BlockSpec((D_MODEL, tn), lambda j: (0, j)),
                  pl.BlockSpec((1, tn), lambda j: (0, j))],
        out_specs=pl.BlockSpec((8, tn), lambda j: (0, j)),
        compiler_params=pltpu.CompilerParams(dimension_semantics=("arbitrary",),
                                             vmem_limit_bytes=VMEM_LIMIT),
        name="adaln",
    )(cond8, w_ada, b_ada)


def _inproj_kernel(x_ref, g_ref, sh_ref, sc_ref, wg_ref, wl_ref, ws_ref,
                   gla_ref, lora_ref, q_ref, k_ref, v_ref):
    h = _rms_norm(x_ref[...], g_ref[...]) * (1.0 + sc_ref[...]) + sh_ref[...]
    hb = h.astype(BF16)
    gla_ref[...] = _dot(hb, wg_ref[...])
    lora_ref[...] = _dot(hb, wl_ref[...])
    s = _dot(hb, ws_ref[...])
    q_ref[...] = s[:, :SWA_Q]
    k_ref[...] = s[:, SWA_Q:SWA_Q + SWA_KV]
    v_ref[...] = s[:, SWA_Q + SWA_KV:]


def _inproj(x, g, sh, sc, w_gla, w_lora, w_swa, *, tm):
    b, t, d = x.shape
    nmod = sh.shape[0]
    mod_map = (lambda i, j: (i, 0, 0)) if nmod > 1 else (lambda i, j: (0, 0, 0))
    row = lambda i, j: (i, j, 0)
    full = lambda i, j: (0, 0)
    n_gla = w_gla.shape[1]
    n_lora = w_lora.shape[1]
    return pl.pallas_call(
        _inproj_kernel,
        out_shape=(jax.ShapeDtypeStruct((b, t, n_gla), F32),
                   jax.ShapeDtypeStruct((b, t, n_lora), F32),
                   jax.ShapeDtypeStruct((b, t, SWA_Q), F32),
                   jax.ShapeDtypeStruct((b, t, SWA_KV), F32),
                   jax.ShapeDtypeStruct((b, t, SWA_KV), F32)),
        grid=(b, t // tm),
        in_specs=[pl.BlockSpec((None, tm, d), row),
                  pl.BlockSpec((1, d), full),
                  pl.BlockSpec((None, 1, d), mod_map),
                  pl.BlockSpec((None, 1, d), mod_map),
                  pl.BlockSpec((d, n_gla), full),
                  pl.BlockSpec((d, n_lora), full),
                  pl.BlockSpec((d, w_swa.shape[1]), full)],
        out_specs=(pl.BlockSpec((None, tm, n_gla), row),
                   pl.BlockSpec((None, tm, n_lora), row),
                   pl.BlockSpec((None, tm, SWA_Q), row),
                   pl.BlockSpec((None, tm, SWA_KV), row),
                   pl.BlockSpec((None, tm, SWA_KV), row)),
        compiler_params=pltpu.CompilerParams(dimension_semantics=("arbitrary", "arbitrary"),
                                             vmem_limit_bytes=VMEM_LIMIT),
        name="inproj",
    )(x, g, sh, sc, w_gla, w_lora, w_swa)


def _log_sigmoid(x):
    return -(jnp.maximum(-x, 0.0) + jnp.log1p(jnp.exp(-jnp.abs(x))))


def _heads_to_rows(x):
    return jnp.concatenate([x[:, h * LANES:(h + 1) * LANES] for h in range(GLA_HEADS)], axis=0)


def _rows_to_heads(x, c):
    return jnp.concatenate([x[h * c:(h + 1) * c, :] for h in range(GLA_HEADS)], axis=1)


def _gla_kernel(has_init, q_ref, k_ref, v_ref, g_ref, lora_ref, waf_ref, baf_ref, wab_ref, bab_ref,
                ng_ref, *rest):
    if has_init:
        s0f_ref, s0b_ref, out_ref, sf_ref, sb_ref, laf_ref, lab_ref, o_ref, stf_ref, stb_ref = rest
    else:
        out_ref, sf_ref, sb_ref, laf_ref, lab_ref, o_ref, stf_ref, stb_ref = rest
    t = q_ref.shape[0]
    c = GLA_CHUNK
    n = t // c
    hc = GLA_HEADS * c

    lora = lora_ref[...].astype(BF16)
    laf_ref[...] = _log_sigmoid(_dot(lora, waf_ref[...]) + baf_ref[...]) * (1.0 / GLA_GATE_NORM)
    lab_ref[...] = _log_sigmoid(_dot(lora, wab_ref[...]) + bab_ref[...]) * (1.0 / GLA_GATE_NORM)

    if has_init:
        stf_ref[...] = s0f_ref[...].T
        stb_ref[...] = s0b_ref[...].T
    else:
        stf_ref[...] = jnp.zeros_like(stf_ref)
        stb_ref[...] = jnp.zeros_like(stb_ref)

    r64 = lax.broadcasted_iota(jnp.int32, (c, c), 0)
    c64 = lax.broadcasted_iota(jnp.int32, (c, c), 1)
    tri_f = jnp.where(c64 <= r64, 1.0, 0.0).astype(BF16)
    tri_b = jnp.where(c64 >= r64, 1.0, 0.0).astype(BF16)
    rr = lax.broadcasted_iota(jnp.int32, (hc, hc), 0)
    cc = lax.broadcasted_iota(jnp.int32, (hc, hc), 1)
    same_head = (rr >> 6) == (cc >> 6)
    keep_f = same_head & ((rr & (c - 1)) >= (cc & (c - 1)))
    keep_b = same_head & ((rr & (c - 1)) <= (cc & (c - 1)))
    norm_g = ng_ref[...]

    def tile_heads(x):
        x4 = jnp.concatenate([x] * GLA_HEADS, axis=0)
        return jnp.where(same_head, x4, 0.0).astype(BF16)

    def direction(ci, la_ref, st_ref, tri, keep, last_row):
        sl = pl.ds(pl.multiple_of(ci * c, c), c)
        la = la_ref[sl, :]
        la_hi, la_lo = _split_hi_lo(la)
        cum = _dot(tri, la_hi) + _dot(tri, la_lo)
        tot = cum[last_row:last_row + 1, :]
        qc = q_ref[sl, :]
        kc = k_ref[sl, :]
        q4 = tile_heads(qc * (GLA_DK ** -0.5) * jnp.exp(cum))
        k4 = tile_heads(kc * jnp.exp(-cum))
        kd4 = tile_heads(kc * jnp.exp(tot - cum))
        v_rows = _heads_to_rows(v_ref[sl, :])
        att = jnp.where(keep, _dot_nt(q4, k4), 0.0).astype(BF16)
        st = st_ref[...]
        o = _dot(att, v_rows.astype(BF16)) + _dot_nt(q4, st.astype(BF16))
        st_ref[...] = jnp.exp(tot) * st + _dot(v_rows.T.astype(BF16), kd4)
        return o

    def finalize(ci, o):
        sl = pl.ds(pl.multiple_of(ci * c, c), c)
        on = _rms_norm(o, norm_g)
        gate = _silu(_heads_to_rows(g_ref[sl, :]))
        out_ref[sl, :] = _rows_to_heads(on * gate, c)

    half = n // 2

    def first_half(i, carry):
        j = n - 1 - i
        o_ref[i] = direction(i, laf_ref, stf_ref, tri_f, keep_f, c - 1)
        o_ref[j] = direction(j, lab_ref, stb_ref, tri_b, keep_b, 0)
        return carry

    def second_half(i, carry):
        j = n - 1 - i
        finalize(i, direction(i, laf_ref, stf_ref, tri_f, keep_f, c - 1) + o_ref[i])
        finalize(j, direction(j, lab_ref, stb_ref, tri_b, keep_b, 0) + o_ref[j])
        return carry

    lax.fori_loop(0, half, first_half, 0)
    lax.fori_loop(half, n, second_half, 0)
    sf_ref[...] = stf_ref[...].T
    sb_ref[...] = stb_ref[...].T


def _gla(gla_in, lora, waf, baf, wab, bab, norm_g, s0f=None, s0b=None):
    b, t, _ = gla_in.shape
    has_init = s0f is not None
    n = t // GLA_CHUNK
    bmap = lambda i: (i, 0, 0)
    full = lambda i: (0, 0)
    in_specs = [pl.BlockSpec((None, t, GLA_QK), lambda i: (i, 0, 0)),
                pl.BlockSpec((None, t, GLA_QK), lambda i: (i, 0, 1)),
                pl.BlockSpec((None, t, GLA_V), lambda i: (i, 0, 1)),
                pl.BlockSpec((None, t, GLA_V), lambda i: (i, 0, 2)),
                pl.BlockSpec((None, t, 2 * GLA_LORA), bmap),
                pl.BlockSpec((2 * GLA_LORA, GLA_QK), full),
                pl.BlockSpec((1, GLA_QK), full),
                pl.BlockSpec((2 * GLA_LORA, GLA_QK), full),
                pl.BlockSpec((1, GLA_QK), full),
                pl.BlockSpec((1, GLA_DV), full)]
    args = [gla_in, gla_in, gla_in, gla_in, lora, waf, baf, wab, bab, norm_g]
    if has_init:
        in_specs += [pl.BlockSpec((None, GLA_QK, GLA_DV), bmap)] * 2
        args += [s0f, s0b]
    return pl.pallas_call(
        functools.partial(_gla_kernel, has_init),
        out_shape=(jax.ShapeDtypeStruct((b, t, GLA_V), F32),
                   jax.ShapeDtypeStruct((b, GLA_QK, GLA_DV), F32),
                   jax.ShapeDtypeStruct((b, GLA_QK, GLA_DV), F32)),
        grid=(b,),
        in_specs=in_specs,
        out_specs=(pl.BlockSpec((None, t, GLA_V), bmap),
                   pl.BlockSpec((None, GLA_QK, GLA_DV), bmap),
                   pl.BlockSpec((None, GLA_QK, GLA_DV), bmap)),
        scratch_shapes=[pltpu.VMEM((t, GLA_QK), F32),
                        pltpu.VMEM((t, GLA_QK), F32),
                        pltpu.VMEM((n, GLA_HEADS * GLA_CHUNK, GLA_DV), F32),
                        pltpu.VMEM((GLA_DV, GLA_QK), F32),
                        pltpu.VMEM((GLA_DV, GLA_QK), F32)],
        compiler_params=pltpu.CompilerParams(dimension_semantics=("arbitrary",),
                                             vmem_limit_bytes=VMEM_LIMIT),
        name="gla",
    )(*args)


def _dup_groups(x):
    lo = lax.broadcasted_iota(jnp.int32, x.shape, 1) < SWA_HEAD_DIM
    xr = pltpu.roll(x, SWA_HEAD_DIM, axis=1)
    return jnp.where(lo, x, xr).astype(BF16), jnp.where(lo, xr, x).astype(BF16)


def _split_pair(qp):
    lo = lax.broadcasted_iota(jnp.int32, qp.shape, 1) < SWA_HEAD_DIM
    return jnp.where(lo, qp, 0.0).astype(BF16), jnp.where(lo, 0.0, qp).astype(BF16)


def _merge_pair(o_even, o_odd):
    lo = lax.broadcasted_iota(jnp.int32, o_even.shape, 1) < SWA_HEAD_DIM
    return jnp.where(lo, o_even, o_odd)


def _attn_ctx_kernel(sink_ref, q_ref, k_ref, v_ref, o_ref):
    kd = _dup_groups(k_ref[...])
    vd = _dup_groups(v_ref[...])
    scale = SWA_HEAD_DIM ** -0.5
    for pair in range(SWA_HEADS // 2):
        grp = pair // 2
        qs = _split_pair(q_ref[:, pair * LANES:(pair + 1) * LANES] * scale)
        outs = []
        for par in range(2):
            sink = sink_ref[2 * pair + par]
            s = _dot_nt(qs[par], kd[grp])
            m = jnp.maximum(jnp.max(s, axis=-1, keepdims=True), sink)
            p = jnp.exp(s - m)
            denom = jnp.sum(p, axis=-1, keepdims=True) + jnp.exp(sink - m)
            outs.append(_dot(p.astype(BF16), vd[grp]) / denom)
        o_ref[:, pair * LANES:(pair + 1) * LANES] = _merge_pair(outs[0], outs[1])


def _attn_ctx(sink, q, k, v):
    b, t, _ = q.shape
    bmap = lambda i: (i, 0, 0)
    return pl.pallas_call(
        _attn_ctx_kernel,
        out_shape=jax.ShapeDtypeStruct((b, t, SWA_Q), F32),
        grid=(b,),
        in_specs=[pl.BlockSpec(memory_space=pltpu.SMEM),
                  pl.BlockSpec((None, t, SWA_Q), bmap),
                  pl.BlockSpec((None, t, SWA_KV), bmap),
                  pl.BlockSpec((None, t, SWA_KV), bmap)],
        out_specs=pl.BlockSpec((None, t, SWA_Q), bmap),
        compiler_params=pltpu.CompilerParams(dimension_semantics=("arbitrary",),
                                             vmem_limit_bytes=VMEM_LIMIT),
        name="attn_ctx",
    )(sink, q, k, v)


def _rope(x, cos, sin_lo, sin_hi):
    return x * cos + pltpu.roll(x, LANES - 16, axis=1) * sin_lo + pltpu.roll(x, 16, axis=1) * sin_hi


def _attn_lat_kernel(sink_ref, q_ref, k_ref, v_ref, kc_ref, vc_ref, cos_ref, sl_ref, sh_ref,
                     o_ref, kw_ref, vw_ref):
    t = q_ref.shape[0]
    ab = ATTN_BLOCK
    nb = t // ab
    scale = SWA_HEAD_DIM ** -0.5

    k_rot = _dup_groups(_rope(k_ref[...], cos_ref[...], sl_ref[...], sh_ref[...]))
    v_dup = _dup_groups(v_ref[...])
    zeros = jnp.zeros((ab, LANES), BF16)
    for grp in range(SWA_KV_HEADS):
        kw_ref[grp, 0:ab, :] = zeros
        kw_ref[grp, ab:ab + t, :] = k_rot[grp]
        kw_ref[grp, ab + t:, :] = zeros
        vw_ref[grp, 0:ab, :] = zeros
        vw_ref[grp, ab:ab + t, :] = v_dup[grp]
        vw_ref[grp, ab + t:, :] = zeros
    kc = _dup_groups(kc_ref[...])
    vc = _dup_groups(vc_ref[...])

    tq = lax.broadcasted_iota(jnp.int32, (4 * ab, 3 * ab), 0) & (ab - 1)
    sk = lax.broadcasted_iota(jnp.int32, (4 * ab, 3 * ab), 1)
    band = jnp.abs(tq + ab - sk) <= ab

    def block(nq, carry):
        row0 = pl.multiple_of(nq * ab, ab)
        s_abs = sk + (nq - 1) * ab
        mask = band & (s_abs >= 0) & (s_abs < t)
        cos = cos_ref[pl.ds(row0, ab), :]
        s_lo = sl_ref[pl.ds(row0, ab), :]
        s_hi = sh_ref[pl.ds(row0, ab), :]
        for grp in range(SWA_KV_HEADS):
            qs = []
            sinks = []
            for pp in range(2):
                pair = 2 * grp + pp
                qp = _rope(q_ref[pl.ds(row0, ab), pair * LANES:(pair + 1) * LANES], cos, s_lo, s_hi)
                qs.extend(_split_pair(qp * scale))
                sinks += [jnp.full((ab, 1), sink_ref[2 * pair], F32),
                          jnp.full((ab, 1), sink_ref[2 * pair + 1], F32)]
            q4 = jnp.concatenate(qs, axis=0)
            sink = jnp.concatenate(sinks, axis=0)
            kwin = kw_ref[grp, pl.ds(row0, 3 * ab), :]
            vwin = vw_ref[grp, pl.ds(row0, 3 * ab), :]
            s_ctx = _dot_nt(q4, kc[grp])
            s_win = jnp.where(mask, _dot_nt(q4, kwin), NEG_INF)
            m = jnp.maximum(jnp.maximum(jnp.max(s_ctx, axis=-1, keepdims=True),
                                        jnp.max(s_win, axis=-1, keepdims=True)), sink)
            p_ctx = jnp.exp(s_ctx - m)
            p_win = jnp.exp(s_win - m)
            denom = (jnp.sum(p_ctx, axis=-1, keepdims=True) + jnp.sum(p_win, axis=-1, keepdims=True)
                     + jnp.exp(sink - m))
            o = (_dot(p_ctx.astype(BF16), vc[grp]) + _dot(p_win.astype(BF16), vwin)) / denom
            for pp in range(2):
                pair = 2 * grp + pp
                o_ref[pl.ds(row0, ab), pair * LANES:(pair + 1) * LANES] = _merge_pair(
                    o[(2 * pp) * ab:(2 * pp + 1) * ab], o[(2 * pp + 1) * ab:(2 * pp + 2) * ab])
        return carry

    lax.fori_loop(0, nb, block, 0)


def _attn_lat(sink, q, k, v, kc, vc, cos, sin_lo, sin_hi):
    b, t, _ = q.shape
    lc = kc.shape[1]
    bmap = lambda i: (i, 0, 0)
    full = lambda i: (0, 0)
    return pl.pallas_call(
        _attn_lat_kernel,
        out_shape=jax.ShapeDtypeStruct((b, t, SWA_Q), F32),
        grid=(b,),
        in_specs=[pl.BlockSpec(memory_space=pltpu.SMEM),
                  pl.BlockSpec((None, t, SWA_Q), bmap),
                  pl.BlockSpec((None, t, SWA_KV), bmap),
                  pl.BlockSpec((None, t, SWA_KV), bmap),
                  pl.BlockSpec((None, lc, SWA_KV), bmap),
                  pl.BlockSpec((None, lc, SWA_KV), bmap),
                  pl.BlockSpec((t, LANES), full),
                  pl.BlockSpec((t, LANES), full),
                  pl.BlockSpec((t, LANES), full)],
        out_specs=pl.BlockSpec((None, t, SWA_Q), bmap),
        scratch_shapes=[pltpu.VMEM((SWA_KV_HEADS, t + 2 * ATTN_BLOCK, LANES), BF16),
                        pltpu.VMEM((SWA_KV_HEADS, t + 2 * ATTN_BLOCK, LANES), BF16)],
        compiler_params=pltpu.CompilerParams(dimension_semantics=("arbitrary",),
                                             vmem_limit_bytes=VMEM_LIMIT),
        name="attn_lat",
    )(sink, q, k, v, kc, vc, cos, sin_lo, sin_hi)


def _rope_tables(t):
    half = SWA_HEAD_DIM // 2
    quarter = half // 2
    pos = jnp.arange(t)
    row = (pos // GRID_W).astype(F32)
    col = (pos % GRID_W).astype(F32)
    inv_freq = ROPE_BASE ** (-jnp.arange(quarter, dtype=F32) / quarter)
    lane = jnp.arange(LANES)
    d = lane % SWA_HEAD_DIM
    freq = inv_freq[d % quarter]
    use_row = (d < half)
    ang = jnp.where(use_row[None, :], row[:, None], col[:, None]) * freq[None, :]
    cos = jnp.cos(ang)
    sin = jnp.sin(ang)
    lower = (d % half) < quarter
    return cos, jnp.where(lower[None, :], -sin, 0.0), jnp.where(lower[None, :], 0.0, sin)


def _route(sel, scores):
    n = sel.shape[1]
    gsz = N_EXPERTS // N_EXPERT_GROUPS

    def first_max(x, idx, size):
        m = jnp.max(x, axis=0, keepdims=True)
        first = jnp.min(jnp.where(x == m, idx, float(size)), axis=0, keepdims=True)
        return m, idx == first

    i8 = lax.broadcasted_iota(jnp.int32, (gsz, n), 0).astype(F32)
    rows = []
    for g in range(N_EXPERT_GROUPS):
        slab = sel[g * gsz:(g + 1) * gsz, :]
        m1, hit = first_max(slab, i8, gsz)
        m2 = jnp.max(jnp.where(hit, NEG_INF, slab), axis=0, keepdims=True)
        rows.append(m1 + m2)
    gscore = jnp.concatenate(rows, axis=0)
    gsel = jnp.zeros((N_EXPERT_GROUPS, n), F32)
    for _ in range(TOPK_GROUPS):
        _, hit = first_max(gscore, i8, N_EXPERT_GROUPS)
        gsel = jnp.where(hit, 1.0, gsel)
        gscore = jnp.where(hit, NEG_INF, gscore)
    emask = jnp.concatenate(
        [jnp.broadcast_to(gsel[g:g + 1, :], (gsz, n)) for g in range(N_EXPERT_GROUPS)], axis=0)
    cand = jnp.where(emask > 0.5, sel, NEG_INF)
    ie = lax.broadcasted_iota(jnp.int32, (N_EXPERTS, n), 0).astype(F32)
    w = jnp.zeros((N_EXPERTS, n), F32)
    for _ in range(TOP_K):
        _, hit = first_max(cand, ie, N_EXPERTS)
        w = jnp.where(hit, scores, w)
        cand = jnp.where(hit, NEG_INF, cand)
    return w / jnp.sum(w, axis=0, keepdims=True) * ROUTED_SCALE


def _outproj_kernel(gla_ref, att_ref, x_ref, wo_ref, g1_ref, sh_ref, sc_ref, ng_ref, rw_ref, rwh_ref,
                    rb_ref, x1_ref, xm_ref, gates_ref):
    y = (_dot(gla_ref[...].astype(BF16), wo_ref[0:GLA_V, :])
         + _dot(att_ref[...].astype(BF16), wo_ref[GLA_V:, :]))
    x1 = x_ref[...] + g1_ref[...] * y
    x1_ref[...] = x1
    xm = _rms_norm(x1, ng_ref[...]) * (1.0 + sc_ref[...]) + sh_ref[...]
    xm_hi, xm_lo = _split_hi_lo(xm)
    xm_ref[...] = xm_hi
    lg = _dot(xm_hi, rw_ref[...])
    logits = lg[:, :N_EXPERTS] + lg[:, N_EXPERTS:] + _dot(xm_lo, rwh_ref[...])
    tm = logits.shape[0]
    lt = jnp.concatenate([logits, jnp.zeros((tm, LANES - N_EXPERTS), F32)], axis=1).T[:N_EXPERTS, :]
    scores = _sigmoid(lt)
    gates_t = _route(scores + rb_ref[...], scores)
    gates_ref[...] = jnp.concatenate([gates_t, jnp.zeros((LANES - N_EXPERTS, tm), F32)], axis=0).T


def _outproj(gla_out, att_out, x, w_out, g1, sh2, sc2, norm_g, rw_cat, rw_hi, rbias, *, tm):
    b, t, d = x.shape
    nmod = g1.shape[0]
    mod_map = (lambda i, j: (i, 0, 0)) if nmod > 1 else (lambda i, j: (0, 0, 0))
    row = lambda i, j: (i, j, 0)
    full = lambda i, j: (0, 0)
    return pl.pallas_call(
        _outproj_kernel,
        out_shape=(jax.ShapeDtypeStruct((b, t, d), F32),
                   jax.ShapeDtypeStruct((b, t, d), BF16),
                   jax.ShapeDtypeStruct((b, t, LANES), F32)),
        grid=(b, t // tm),
        in_specs=[pl.BlockSpec((None, tm, GLA_V), row),
                  pl.BlockSpec((None, tm, SWA_Q), row),
                  pl.BlockSpec((None, tm, d), row),
                  pl.BlockSpec((d, d), full),
                  pl.BlockSpec((None, 1, d), mod_map),
                  pl.BlockSpec((None, 1, d), mod_map),
                  pl.BlockSpec((None, 1, d), mod_map),
                  pl.BlockSpec((1, d), full),
                  pl.BlockSpec((d, 2 * N_EXPERTS), full),
                  pl.BlockSpec((d, N_EXPERTS), full),
                  pl.BlockSpec((N_EXPERTS, 1), full)],
        out_specs=(pl.BlockSpec((None, tm, d), row),
                   pl.BlockSpec((None, tm, d), row),
                   pl.BlockSpec((None, tm, LANES), row)),
        compiler_params=pltpu.CompilerParams(dimension_semantics=("arbitrary", "arbitrary"),
                                             vmem_limit_bytes=VMEM_LIMIT),
        name="outproj",
    )(gla_out, att_out, x, w_out, g1, sh2, sc2, norm_g, rw_cat, rw_hi, rbias)


MOE_EB = 8
MOE_TM = 1024
PREP_EB = 4


def _expert_prep_kernel(wg_ref, wu_ref, wd_ref, wgu_ref, wdb_ref):
    wgu_ref[...] = jnp.concatenate([wg_ref[...].astype(BF16), wu_ref[...].astype(BF16)], axis=2)
    wdb_ref[...] = wd_ref[...].astype(BF16)


def _expert_prep(wg, wu, wd):
    ne, d, f = wg.shape
    blk = lambda e: (e, 0, 0)
    return pl.pallas_call(
        _expert_prep_kernel,
        out_shape=(jax.ShapeDtypeStruct((ne, d, 2 * f), BF16),
                   jax.ShapeDtypeStruct((ne, f, d), BF16)),
        grid=(ne // PREP_EB,),
        in_specs=[pl.BlockSpec((PREP_EB, d, f), blk),
                  pl.BlockSpec((PREP_EB, d, f), blk),
                  pl.BlockSpec((PREP_EB, f, d), blk)],
        out_specs=(pl.BlockSpec((PREP_EB, d, 2 * f), blk),
                   pl.BlockSpec((PREP_EB, f, d), blk)),
        compiler_params=pltpu.CompilerParams(dimension_semantics=("arbitrary",),
                                             vmem_limit_bytes=VMEM_LIMIT),
        name="expert_prep",
    )(wg, wu, wd)


def _moe_kernel(xm_ref, gt_ref, x1_ref, g2_ref, fg_ref, wgu_ref, wd_ref, swg_ref, swu_ref, swd_ref, o_ref):
    e = pl.program_id(1)
    x = xm_ref[...]

    @pl.when(e == 0)
    def _():
        hs = _silu(_dot(x, swg_ref[...])) * _dot(x, swu_ref[...])
        o_ref[...] = _dot(hs.astype(BF16), swd_ref[...])

    shift = jnp.where(e == 0, 0, LANES - e * MOE_EB)
    gcols = pltpu.roll(gt_ref[...], shift, axis=1)
    hs = []
    for j in range(MOE_EB):
        ab = _dot(x, wgu_ref[j])
        h = _silu(ab[:, :EXPERT_FF]) * ab[:, EXPERT_FF:]
        hs.append((h * gcols[:, j:j + 1]).astype(BF16))
    hcat = jnp.concatenate(hs, axis=1)
    o_ref[...] += _dot(hcat, wd_ref[...].reshape(MOE_EB * EXPERT_FF, D_MODEL))

    @pl.when(e == pl.num_programs(1) - 1)
    def _():
        y = x1_ref[...] + g2_ref[...] * o_ref[...]
        o_ref[...] = _rms_norm(y, fg_ref[...])


def _moe(xm, gates, x1, g2, final_g, wgu, wd, swg, swu, swd, *, tiles_per_mod):
    n, d = xm.shape
    tm = MOE_TM
    row = lambda i, e: (i, 0)
    full = lambda i, e: (0, 0)
    mod_map = lambda i, e: (i // tiles_per_mod, 0, 0)
    blk = lambda i, e: (e, 0, 0)
    return pl.pallas_call(
        _moe_kernel,
        out_shape=jax.ShapeDtypeStruct((n, d), F32),
        grid=(n // tm, N_EXPERTS // MOE_EB),
        in_specs=[pl.BlockSpec((tm, d), row),
                  pl.BlockSpec((tm, LANES), row),
                  pl.BlockSpec((tm, d), row),
                  pl.BlockSpec((None, 1, d), mod_map),
                  pl.BlockSpec((1, d), full),
                  pl.BlockSpec((MOE_EB, d, 2 * EXPERT_FF), blk),
                  pl.BlockSpec((MOE_EB, EXPERT_FF, d), blk),
                  pl.BlockSpec((d, SHARED_FF), full),
                  pl.BlockSpec((d, SHARED_FF), full),
                  pl.BlockSpec((SHARED_FF, d), full)],
        out_specs=pl.BlockSpec((tm, d), row),
        compiler_params=pltpu.CompilerParams(dimension_semantics=("arbitrary", "arbitrary"),
                                             vmem_limit_bytes=VMEM_LIMIT),
        name="moe",
    )(xm, gates, x1, g2, final_g, wgu, wd, swg, swu, swd)


def _stream(x, mods, p, attn_fn, s0=None):
    b, t, d = x.shape
    sh1, sc1, g1, sh2, sc2, g2 = mods
    gla_in, lora, q_s, k_s, v_s = _inproj(x, p["norm_attn_g"], sh1, sc1, p["w_gla"], p["w_lora"],
                                          p["w_swa"], tm=256)
    if s0 is None:
        gla_out, s_f, s_b = _gla(gla_in, lora, p["waf"], p["baf"], p["wab"], p["bab"], p["gla_norm_g"])
    else:
        gla_out, s_f, s_b = _gla(gla_in, lora, p["waf"], p["baf"], p["wab"], p["bab"], p["gla_norm_g"],
                                 s0[0], s0[1])
    att_out = attn_fn(q_s, k_s, v_s)
    x1, xm, gates = _outproj(gla_out, att_out, x, p["w_out"], g1, sh2, sc2, p["norm_ffn_g"],
                             p["rw_cat"], p["rw_hi"], p["rbias"], tm=256)
    n = b * t
    tiles_per_mod = max(t // MOE_TM, 1) if g2.shape[0] > 1 else (n // MOE_TM)
    y = _moe(xm.reshape(n, d), gates.reshape(n, LANES), x1.reshape(n, d), g2, p["final_norm_g"],
             p["wgu"], p["wd"], p["swg"], p["swu"], p["swd"], tiles_per_mod=tiles_per_mod)
    return y.reshape(b, t, d), k_s, v_s, s_f, s_b


def kernel(x_prompt, x_sample, c, cache_swa_k, cache_swa_v, state_gla_fwd, state_gla_bwd, c_ctx, w_ada, b_ada, norm_attn_g, norm_ffn_g, w_in, gla_wa_f, gla_ba_f, gla_wa_b, gla_ba_b, gla_norm_g, swa_sink, w_out, router_w, router_bias, exp_w_gate, exp_w_up, exp_w_down, sh_w_gate, sh_w_up, sh_w_down, final_norm_g):
    l = 0
    d = D_MODEL
    nb_ctx, t_ctx, _ = x_prompt.shape
    nb_lat, t_lat, _ = x_sample.shape

    pad = jnp.zeros((8 - 1 - nb_lat, d), F32)
    cond8 = jnp.concatenate([c_ctx[None, :], c, pad], axis=0)
    mod = _adaln(cond8, w_ada[l], b_ada[l][None, :])
    mods_ctx = [mod[0:1, i * d:(i + 1) * d][:, None, :] for i in range(6)]
    mods_lat = [mod[1:1 + nb_lat, i * d:(i + 1) * d][:, None, :] for i in range(6)]

    wgu, wdb = _expert_prep(exp_w_gate[l], exp_w_up[l], exp_w_down[l])
    zeros_lora = jnp.zeros((GLA_LORA, GLA_QK), F32)
    rw = router_w[l]
    rw_hi = rw.astype(BF16)
    rw_lo = (rw - rw_hi.astype(F32)).astype(BF16)
    w_in_b = w_in[l].astype(BF16)
    p = {
        "norm_attn_g": norm_attn_g[l][None, :],
        "norm_ffn_g": norm_ffn_g[l][None, :],
        "final_norm_g": final_norm_g[None, :],
        "w_gla": w_in_b[:, :2 * GLA_QK + 2 * GLA_V],
        "w_lora": w_in_b[:, 2 * GLA_QK + 2 * GLA_V:2 * GLA_QK + 2 * GLA_V + 2 * GLA_LORA],
        "w_swa": w_in_b[:, 2 * GLA_QK + 2 * GLA_V + 2 * GLA_LORA:],
        "waf": jnp.concatenate([gla_wa_f[l], zeros_lora], axis=0).astype(BF16),
        "wab": jnp.concatenate([zeros_lora, gla_wa_b[l]], axis=0).astype(BF16),
        "baf": gla_ba_f[l][None, :],
        "bab": gla_ba_b[l][None, :],
        "gla_norm_g": gla_norm_g[l][None, :],
        "w_out": w_out[l].astype(BF16),
        "rw_cat": jnp.concatenate([rw_hi, rw_lo], axis=1),
        "rw_hi": rw_hi,
        "rbias": router_bias[l][:, None],
        "wgu": wgu, "wd": wdb,
        "swg": sh_w_gate[l].astype(BF16), "swu": sh_w_up[l].astype(BF16),
        "swd": sh_w_down[l].astype(BF16),
    }
    sink = swa_sink[l]

    y_prompt, k_c, v_c, s_f, s_b = _stream(x_prompt, mods_ctx, p, functools.partial(_attn_ctx, sink))

    cos, sin_lo, sin_hi = _rope_tables(t_lat)
    kc = cache_swa_k[:, l].reshape(nb_lat, -1, SWA_KV)
    vc = cache_swa_v[:, l].reshape(nb_lat, -1, SWA_KV)
    lat_attn = lambda q, k, v: _attn_lat(sink, q, k, v, kc, vc, cos, sin_lo, sin_hi)
    s0 = (state_gla_fwd[:, l].reshape(nb_lat, GLA_QK, GLA_DV),
          state_gla_bwd[:, l].reshape(nb_lat, GLA_QK, GLA_DV))
    y_sample, _, _, _, _ = _stream(x_sample, mods_lat, p, lat_attn, s0)

    new_k = k_c.reshape(nb_ctx, 1, t_ctx, SWA_KV_HEADS, SWA_HEAD_DIM)
    new_v = v_c.reshape(nb_ctx, 1, t_ctx, SWA_KV_HEADS, SWA_HEAD_DIM)
    new_sf = s_f.reshape(nb_ctx, 1, GLA_HEADS, GLA_DK, GLA_DV)
    new_sb = s_b.reshape(nb_ctx, 1, GLA_HEADS, GLA_DK, GLA_DV)
    return (y_prompt, y_sample, new_k, new_v, new_sf, new_sb)
```

```python
import functools

import jax
import jax.numpy as jnp
from jax import lax
from jax.experimental import pallas as pl
from jax.experimental.pallas import tpu as pltpu

F32 = jnp.float32
BF16 = jnp.bfloat16

D_MODEL = 1024
GLA_HEADS = 4
GLA_DK = 64
GLA_DV = 128
GLA_LORA = 16
GLA_GATE_NORM = 16.0
GLA_CHUNK = 64
GLA_QK = GLA_HEADS * GLA_DK
GLA_V = GLA_HEADS * GLA_DV
SWA_HEAD_DIM = 64
SWA_HEADS = 8
SWA_KV_HEADS = 2
SWA_Q = SWA_HEADS * SWA_HEAD_DIM
SWA_KV = SWA_KV_HEADS * SWA_HEAD_DIM
ATTN_BLOCK = 128
GRID_W = 64
ROPE_BASE = 10000.0
N_EXPERTS = 64
TOP_K = 8
N_EXPERT_GROUPS = 8
TOPK_GROUPS = 4
EXPERT_FF = 128
SHARED_FF = 256
ROUTED_SCALE = 2.5
EPS = 1e-6

LANES = 128
VMEM_LIMIT = 56 * 1024 * 1024

NEG_INF = float("-inf")


def _dot(a, b):
    return jnp.dot(a, b, preferred_element_type=F32)


def _dot_nt(a, b):
    return lax.dot_general(a, b, (((1,), (1,)), ((), ())), preferred_element_type=F32)


def _split_hi_lo(x):
    hi = x.astype(BF16)
    lo = (x - hi.astype(F32)).astype(BF16)
    return hi, lo


def _sigmoid(x):
    return 1.0 / (1.0 + jnp.exp(-x))


def _silu(x):
    return x * _sigmoid(x)


def _rms_norm(x, g):
    ms = jnp.mean(x * x, axis=-1, keepdims=True)
    return x * lax.rsqrt(ms + EPS) * g


def _adaln_kernel(c_ref, w_ref, b_ref, o_ref):
    a_hi, a_lo = _split_hi_lo(_silu(c_ref[...]))
    w_hi, w_lo = _split_hi_lo(w_ref[...])
    o_ref[...] = _dot(a_hi, w_hi) + _dot(a_lo, w_hi) + _dot(a_hi, w_lo) + b_ref[...]


def _adaln(cond8, w_ada, b_ada):
    n = w_ada.shape[1]
    tn = 1536
    return pl.pallas_call(
        _adaln_kernel,
        out_shape=jax.ShapeDtypeStruct((8, n), F32),
        grid=(n // tn,),
        in_specs=[pl.BlockSpec((8, D_MODEL), lambda j: (0, 0)),
                  pl.BlockSpec((D_MODEL, tn), lambda j: (0, j)),
                  pl.BlockSpec((1, tn), lambda j: (0, j))],
        out_specs=pl.BlockSpec((8, tn), lambda j: (0, j)),
        compiler_params=pltpu.CompilerParams(dimension_semantics=("arbitrary",),
                                             vmem_limit_bytes=VMEM_LIMIT),
        name="adaln",
    )(cond8, w_ada, b_ada)


def _inproj_kernel(x_ref, g_ref, sh_ref, sc_ref, wg_ref, wl_ref, ws_ref,
                   gla_ref, lora_ref, q_ref, k_ref, v_ref):
    h = _rms_norm(x_ref[...], g_ref[...]) * (1.0 + sc_ref[...]) + sh_ref[...]
    hb = h.astype(BF16)
    gla_ref[...] = _dot(hb, wg_ref[...])
    lora_ref[...] = _dot(hb, wl_ref[...])
    s = _dot(hb, ws_ref[...])
    q_ref[...] = s[:, :SWA_Q]
    k_ref[...] = s[:, SWA_Q:SWA_Q + SWA_KV]
    v_ref[...] = s[:, SWA_Q + SWA_KV:]


def _inproj(x, g, sh, sc, w_gla, w_lora, w_swa, *, tm):
    b, t, d = x.shape
    nmod = sh.shape[0]
    mod_map = (lambda i, j: (i, 0, 0)) if nmod > 1 else (lambda i, j: (0, 0, 0))
    row = lambda i, j: (i, j, 0)
    full = lambda i, j: (0, 0)
    n_gla = w_gla.shape[1]
    n_lora = w_lora.shape[1]
    return pl.pallas_call(
        _inproj_kernel,
        out_shape=(jax.ShapeDtypeStruct((b, t, n_gla), F32),
                   jax.ShapeDtypeStruct((b, t, n_lora), F32),
                   jax.ShapeDtypeStruct((b, t, SWA_Q), F32),
                   jax.ShapeDtypeStruct((b, t, SWA_KV), F32),
                   jax.ShapeDtypeStruct((b, t, SWA_KV), F32)),
        grid=(b, t // tm),
        in_specs=[pl.BlockSpec((None, tm, d), row),
                  pl.BlockSpec((1, d), full),
                  pl.BlockSpec((None, 1, d), mod_map),
                  pl.BlockSpec((None, 1, d), mod_map),
                  pl.BlockSpec((d, n_gla), full),
                  pl.BlockSpec((d, n_lora), full),
                  pl.BlockSpec((d, w_swa.shape[1]), full)],
        out_specs=(pl.BlockSpec((None, tm, n_gla), row),
                   pl.BlockSpec((None, tm, n_lora), row),
                   pl.BlockSpec((None, tm, SWA_Q), row),
                   pl.BlockSpec((None, tm, SWA_KV), row),
                   pl.BlockSpec((None, tm, SWA_KV), row)),
        compiler_params=pltpu.CompilerParams(dimension_semantics=("arbitrary", "arbitrary"),
                                             vmem_limit_bytes=VMEM_LIMIT),
        name="inproj",
    )(x, g, sh, sc, w_gla, w_lora, w_swa)


def _log_sigmoid(x):
    return -(jnp.maximum(-x, 0.0) + jnp.log1p(jnp.exp(-jnp.abs(x))))


def _heads_to_rows(x):
    return jnp.concatenate([x[:, h * LANES:(h + 1) * LANES] for h in range(GLA_HEADS)], axis=0)


def _rows_to_heads(x, c):
    return jnp.concatenate([x[h * c:(h + 1) * c, :] for h in range(GLA_HEADS)], axis=1)


def _gla_kernel(has_init, q_ref, k_ref, v_ref, g_ref, lora_ref, waf_ref, baf_ref, wab_ref, bab_ref,
                ng_ref, *rest):
    if has_init:
        s0f_ref, s0b_ref, out_ref, sf_ref, sb_ref, laf_ref, lab_ref, o_ref, stf_ref, stb_ref = rest
    else:
        out_ref, sf_ref, sb_ref, laf_ref, lab_ref, o_ref, stf_ref, stb_ref = rest
    t = q_ref.shape[0]
    c = GLA_CHUNK
    n = t // c
    hc = GLA_HEADS * c

    lora = lora_ref[...].astype(BF16)
    laf_ref[...] = _log_sigmoid(_dot(lora, waf_ref[...]) + baf_ref[...]) * (1.0 / GLA_GATE_NORM)
    lab_ref[...] = _log_sigmoid(_dot(lora, wab_ref[...]) + bab_ref[...]) * (1.0 / GLA_GATE_NORM)

    if has_init:
        stf_ref[...] = s0f_ref[...].T
        stb_ref[...] = s0b_ref[...].T
    else:
        stf_ref[...] = jnp.zeros_like(stf_ref)
        stb_ref[...] = jnp.zeros_like(stb_ref)

    r64 = lax.broadcasted_iota(jnp.int32, (c, c), 0)
    c64 = lax.broadcasted_iota(jnp.int32, (c, c), 1)
    tri_f = jnp.where(c64 <= r64, 1.0, 0.0).astype(BF16)
    tri_b = jnp.where(c64 >= r64, 1.0, 0.0).astype(BF16)
    rr = lax.broadcasted_iota(jnp.int32, (hc, hc), 0)
    cc = lax.broadcasted_iota(jnp.int32, (hc, hc), 1)
    same_head = (rr >> 6) == (cc >> 6)
    keep_f = same_head & ((rr & (c - 1)) >= (cc & (c - 1)))
    keep_b = same_head & ((rr & (c - 1)) <= (cc & (c - 1)))
    norm_g = ng_ref[...]

    def tile_heads(x):
        x4 = jnp.concatenate([x] * GLA_HEADS, axis=0)
        return jnp.where(same_head, x4, 0.0).astype(BF16)

    def direction(ci, la_ref, st_ref, tri, keep, last_row):
        sl = pl.ds(pl.multiple_of(ci * c, c), c)
        la = la_ref[sl, :]
        la_hi, la_lo = _split_hi_lo(la)
        cum = _dot(tri, la_hi) + _dot(tri, la_lo)
        tot = cum[last_row:last_row + 1, :]
        qc = q_ref[sl, :]
        kc = k_ref[sl, :]
        q4 = tile_heads(qc * (GLA_DK ** -0.5) * jnp.exp(cum))
        k4 = tile_heads(kc * jnp.exp(-cum))
        kd4 = tile_heads(kc * jnp.exp(tot - cum))
        v_rows = _heads_to_rows(v_ref[sl, :])
        att = jnp.where(keep, _dot_nt(q4, k4), 0.0).astype(BF16)
        st = st_ref[...]
        o = _dot(att, v_rows.astype(BF16)) + _dot_nt(q4, st.astype(BF16))
        st_ref[...] = jnp.exp(tot) * st + _dot(v_rows.T.astype(BF16), kd4)
        return o

    def finalize(ci, o):
        sl = pl.ds(pl.multiple_of(ci * c, c), c)
        on = _rms_norm(o, norm_g)
        gate = _silu(_heads_to_rows(g_ref[sl, :]))
        out_ref[sl, :] = _rows_to_heads(on * gate, c)

    half = n // 2

    def first_half(i, carry):
        j = n - 1 - i
        o_ref[i] = direction(i, laf_ref, stf_ref, tri_f, keep_f, c - 1)
        o_ref[j] = direction(j, lab_ref, stb_ref, tri_b, keep_b, 0)
        return carry

    def second_half(i, carry):
        j = n - 1 - i
        finalize(i, direction(i, laf_ref, stf_ref, tri_f, keep_f, c - 1) + o_ref[i])
        finalize(j, direction(j, lab_ref, stb_ref, tri_b, keep_b, 0) + o_ref[j])
        return carry

    lax.fori_loop(0, half, first_half, 0)
    lax.fori_loop(half, n, second_half, 0)
    sf_ref[...] = stf_ref[...].T
    sb_ref[...] = stb_ref[...].T


def _gla(gla_in, lora, waf, baf, wab, bab, norm_g, s0f=None, s0b=None):
    b, t, _ = gla_in.shape
    has_init = s0f is not None
    n = t // GLA_CHUNK
    bmap = lambda i: (i, 0, 0)
    full = lambda i: (0, 0)
    in_specs = [pl.BlockSpec((None, t, GLA_QK), lambda i: (i, 0, 0)),
                pl.BlockSpec((None, t, GLA_QK), lambda i: (i, 0, 1)),
                pl.BlockSpec((None, t, GLA_V), lambda i: (i, 0, 1)),
                pl.BlockSpec((None, t, GLA_V), lambda i: (i, 0, 2)),
                pl.BlockSpec((None, t, 2 * GLA_LORA), bmap),
                pl.BlockSpec((2 * GLA_LORA, GLA_QK), full),
                pl.BlockSpec((1, GLA_QK), full),
                pl.BlockSpec((2 * GLA_LORA, GLA_QK), full),
                pl.BlockSpec((1, GLA_QK), full),
                pl.BlockSpec((1, GLA_DV), full)]
    args = [gla_in, gla_in, gla_in, gla_in, lora, waf, baf, wab, bab, norm_g]
    if has_init:
        in_specs += [pl.BlockSpec((None, GLA_QK, GLA_DV), bmap)] * 2
        args += [s0f, s0b]
    return pl.pallas_call(
        functools.partial(_gla_kernel, has_init),
        out_shape=(jax.ShapeDtypeStruct((b, t, GLA_V), F32),
                   jax.ShapeDtypeStruct((b, GLA_QK, GLA_DV), F32),
                   jax.ShapeDtypeStruct((b, GLA_QK, GLA_DV), F32)),
        grid=(b,),
        in_specs=in_specs,
        out_specs=(pl.BlockSpec((None, t, GLA_V), bmap),
                   pl.BlockSpec((None, GLA_QK, GLA_DV), bmap),
                   pl.BlockSpec((None, GLA_QK, GLA_DV), bmap)),
        scratch_shapes=[pltpu.VMEM((t, GLA_QK), F32),
                        pltpu.VMEM((t, GLA_QK), F32),
                        pltpu.VMEM((n, GLA_HEADS * GLA_CHUNK, GLA_DV), F32),
                        pltpu.VMEM((GLA_DV, GLA_QK), F32),
                        pltpu.VMEM((GLA_DV, GLA_QK), F32)],
        compiler_params=pltpu.CompilerParams(dimension_semantics=("arbitrary",),
                                             vmem_limit_bytes=VMEM_LIMIT),
        name="gla",
    )(*args)


def _dup_groups(x):
    lo = lax.broadcasted_iota(jnp.int32, x.shape, 1) < SWA_HEAD_DIM
    xr = pltpu.roll(x, SWA_HEAD_DIM, axis=1)
    return jnp.where(lo, x, xr), jnp.where(lo, xr, x)


def _pairs_attention(qps, sinks, k_dups, vt_dups, mask):
    nq = qps[0].shape[0]
    lo = lax.broadcasted_iota(jnp.int32, (nq, LANES), 1) < SWA_HEAD_DIM
    even = lax.broadcasted_iota(jnp.int32, (1, 2 * nq), 1) < nq
    scores = []
    for qp, k_dup in zip(qps, k_dups):
        q2 = jnp.concatenate([jnp.where(lo, qp, 0.0), jnp.where(lo, 0.0, qp)], axis=0).astype(BF16)
        scores.append(_dot_nt(k_dup, q2))
    probs = []
    for s, (sink_even, sink_odd) in zip(scores, sinks):
        if mask is not None:
            s = jnp.where(mask, s, NEG_INF)
        sink = jnp.where(even, sink_even, sink_odd)
        m = jnp.maximum(jnp.max(s, axis=0, keepdims=True), sink)
        p = jnp.exp(s - m)
        denom = jnp.sum(p, axis=0, keepdims=True) + jnp.exp(sink - m)
        probs.append((p.astype(BF16), 1.0 / denom))
    outs = []
    for (p, rdenom), vt_dup in zip(probs, vt_dups):
        o = _dot(vt_dup, p) * rdenom
        outs.append(jnp.concatenate([o[:SWA_HEAD_DIM, :nq], o[SWA_HEAD_DIM:, nq:]], axis=0).T)
    return outs


def _attn_ctx_kernel(sink_ref, q_ref, k_ref, v_ref, o_ref):
    kd = [x.astype(BF16) for x in _dup_groups(k_ref[...])]
    vt = [x.T.astype(BF16) for x in _dup_groups(v_ref[...])]
    scale = SWA_HEAD_DIM ** -0.5
    pairs = range(SWA_HEADS // 2)
    outs = _pairs_attention([q_ref[:, pr * LANES:(pr + 1) * LANES] * scale for pr in pairs],
                            [(sink_ref[2 * pr], sink_ref[2 * pr + 1]) for pr in pairs],
                            [kd[pr // 2] for pr in pairs], [vt[pr // 2] for pr in pairs], None)
    for pr in pairs:
        o_ref[:, pr * LANES:(pr + 1) * LANES] = outs[pr]


def _attn_ctx(sink, q, k, v):
    b, t, _ = q.shape
    bmap = lambda i: (i, 0, 0)
    return pl.pallas_call(
        _attn_ctx_kernel,
        out_shape=jax.ShapeDtypeStruct((b, t, SWA_Q), F32),
        grid=(b,),
        in_specs=[pl.BlockSpec(memory_space=pltpu.SMEM),
                  pl.BlockSpec((None, t, SWA_Q), bmap),
                  pl.BlockSpec((None, t, SWA_KV), bmap),
                  pl.BlockSpec((None, t, SWA_KV), bmap)],
        out_specs=pl.BlockSpec((None, t, SWA_Q), bmap),
        compiler_params=pltpu.CompilerParams(dimension_semantics=("arbitrary",),
                                             vmem_limit_bytes=VMEM_LIMIT),
        name="attn_ctx",
    )(sink, q, k, v)


def _rope(x, cos, sin_lo, sin_hi):
    return x * cos + pltpu.roll(x, LANES - 16, axis=1) * sin_lo + pltpu.roll(x, 16, axis=1) * sin_hi


def _attn_lat_kernel(sink_ref, q_ref, k_ref, v_ref, kc_ref, vc_ref, cos_ref, sl_ref, sh_ref,
                     o_ref, kw_ref, vw_ref):
    t = q_ref.shape[0]
    ab = ATTN_BLOCK
    nb = t // ab
    scale = SWA_HEAD_DIM ** -0.5

    k_rot = _dup_groups(_rope(k_ref[...], cos_ref[...], sl_ref[...], sh_ref[...]))
    v_dup = _dup_groups(v_ref[...])
    zeros = jnp.zeros((ab, LANES), BF16)
    for grp in range(SWA_KV_HEADS):
        kw_ref[grp, 0:ab, :] = zeros
        kw_ref[grp, ab:ab + t, :] = k_rot[grp].astype(BF16)
        kw_ref[grp, ab + t:, :] = zeros
        vw_ref[grp, 0] = zeros
        for blk in range(nb):
            vw_ref[grp, blk + 1] = v_dup[grp][blk * ab:(blk + 1) * ab, :].T.astype(BF16)
        vw_ref[grp, nb + 1] = zeros
    kc = [x.astype(BF16) for x in _dup_groups(kc_ref[...])]
    vct = [x.T.astype(BF16) for x in _dup_groups(vc_ref[...])]
    lc = kc_ref.shape[0]

    key = lax.broadcasted_iota(jnp.int32, (lc + 3 * ab, 2 * ab), 0) - lc
    tq = lax.broadcasted_iota(jnp.int32, (lc + 3 * ab, 2 * ab), 1) & (ab - 1)
    band = (key < 0) | (jnp.abs(tq + ab - key) <= ab)

    def block(nq, carry):
        row0 = pl.multiple_of(nq * ab, ab)
        s_abs = key + (nq - 1) * ab
        mask = band & ((key < 0) | ((s_abs >= 0) & (s_abs < t)))
        cos = cos_ref[pl.ds(row0, ab), :]
        s_lo = sl_ref[pl.ds(row0, ab), :]
        s_hi = sh_ref[pl.ds(row0, ab), :]
        k_all = [jnp.concatenate([kc[grp], kw_ref[grp, pl.ds(row0, 3 * ab), :]], axis=0)
                 for grp in range(SWA_KV_HEADS)]
        vt_all = [jnp.concatenate([vct[grp], vw_ref[grp, nq], vw_ref[grp, nq + 1], vw_ref[grp, nq + 2]],
                                  axis=1) for grp in range(SWA_KV_HEADS)]
        pairs = range(SWA_HEADS // 2)
        qps = [_rope(q_ref[pl.ds(row0, ab), pr * LANES:(pr + 1) * LANES], cos, s_lo, s_hi) * scale
               for pr in pairs]
        outs = _pairs_attention(qps, [(sink_ref[2 * pr], sink_ref[2 * pr + 1]) for pr in pairs],
                                [k_all[pr // 2] for pr in pairs], [vt_all[pr // 2] for pr in pairs], mask)
        for pr in pairs:
            o_ref[pl.ds(row0, ab), pr * LANES:(pr + 1) * LANES] = outs[pr]
        return carry

    lax.fori_loop(0, nb, block, 0)


def _attn_lat(sink, q, k, v, kc, vc, cos, sin_lo, sin_hi):
    b, t, _ = q.shape
    lc = kc.shape[1]
    bmap = lambda i: (i, 0, 0)
    full = lambda i: (0, 0)
    return pl.pallas_call(
        _attn_lat_kernel,
        out_shape=jax.ShapeDtypeStruct((b, t, SWA_Q), F32),
        grid=(b,),
        in_specs=[pl.BlockSpec(memory_space=pltpu.SMEM),
                  pl.BlockSpec((None, t, SWA_Q), bmap),
                  pl.BlockSpec((None, t, SWA_KV), bmap),
                  pl.BlockSpec((None, t, SWA_KV), bmap),
                  pl.BlockSpec((None, lc, SWA_KV), bmap),
                  pl.BlockSpec((None, lc, SWA_KV), bmap),
                  pl.BlockSpec((t, LANES), full),
                  pl.BlockSpec((t, LANES), full),
                  pl.BlockSpec((t, LANES), full)],
        out_specs=pl.BlockSpec((None, t, SWA_Q), bmap),
        scratch_shapes=[pltpu.VMEM((SWA_KV_HEADS, t + 2 * ATTN_BLOCK, LANES), BF16),
                        pltpu.VMEM((SWA_KV_HEADS, t // ATTN_BLOCK + 2, LANES, ATTN_BLOCK), BF16)],
        compiler_params=pltpu.CompilerParams(dimension_semantics=("arbitrary",),
                                             vmem_limit_bytes=VMEM_LIMIT),
        name="attn_lat",
    )(sink, q, k, v, kc, vc, cos, sin_lo, sin_hi)


def _rope_tables(t):
    half = SWA_HEAD_DIM // 2
    quarter = half // 2
    pos = jnp.arange(t)
    row = (pos // GRID_W).astype(F32)
    col = (pos % GRID_W).astype(F32)
    inv_freq = ROPE_BASE ** (-jnp.arange(quarter, dtype=F32) / quarter)
    lane = jnp.arange(LANES)
    d = lane % SWA_HEAD_DIM
    freq = inv_freq[d % quarter]
    use_row = (d < half)
    ang = jnp.where(use_row[None, :], row[:, None], col[:, None]) * freq[None, :]
    cos = jnp.cos(ang)
    sin = jnp.sin(ang)
    lower = (d % half) < quarter
    return cos, jnp.where(lower[None, :], -sin, 0.0), jnp.where(lower[None, :], 0.0, sin)


def _route(sel, scores):
    n = sel.shape[1]
    gsz = N_EXPERTS // N_EXPERT_GROUPS

    def first_max(x, idx, size):
        m = jnp.max(x, axis=0, keepdims=True)
        first = jnp.min(jnp.where(x == m, idx, float(size)), axis=0, keepdims=True)
        return m, idx == first

    i8 = lax.broadcasted_iota(jnp.int32, (gsz, n), 0).astype(F32)
    rows = []
    for g in range(N_EXPERT_GROUPS):
        slab = sel[g * gsz:(g + 1) * gsz, :]
        m1, hit = first_max(slab, i8, gsz)
        m2 = jnp.max(jnp.where(hit, NEG_INF, slab), axis=0, keepdims=True)
        rows.append(m1 + m2)
    gscore = jnp.concatenate(rows, axis=0)
    gsel = jnp.zeros((N_EXPERT_GROUPS, n), F32)
    for _ in range(TOPK_GROUPS):
        _, hit = first_max(gscore, i8, N_EXPERT_GROUPS)
        gsel = jnp.where(hit, 1.0, gsel)
        gscore = jnp.where(hit, NEG_INF, gscore)
    emask = jnp.concatenate(
        [jnp.broadcast_to(gsel[g:g + 1, :], (gsz, n)) for g in range(N_EXPERT_GROUPS)], axis=0)
    cand = jnp.where(emask > 0.5, sel, NEG_INF)
    ie = lax.broadcasted_iota(jnp.int32, (N_EXPERTS, n), 0).astype(F32)
    w = jnp.zeros((N_EXPERTS, n), F32)
    for _ in range(TOP_K):
        _, hit = first_max(cand, ie, N_EXPERTS)
        w = jnp.where(hit, scores, w)
        cand = jnp.where(hit, NEG_INF, cand)
    return w / jnp.sum(w, axis=0, keepdims=True) * ROUTED_SCALE


def _outproj_kernel(gla_ref, att_ref, x_ref, wo_ref, g1_ref, sh_ref, sc_ref, ng_ref, rw_ref, rwh_ref,
                    rb_ref, x1_ref, xm_ref, gates_ref):
    y = (_dot(gla_ref[...].astype(BF16), wo_ref[0:GLA_V, :])
         + _dot(att_ref[...].astype(BF16), wo_ref[GLA_V:, :]))
    x1 = x_ref[...] + g1_ref[...] * y
    x1_ref[...] = x1
    xm = _rms_norm(x1, ng_ref[...]) * (1.0 + sc_ref[...]) + sh_ref[...]
    xm_hi, xm_lo = _split_hi_lo(xm)
    xm_ref[...] = xm_hi
    lg = _dot(xm_hi, rw_ref[...])
    logits = lg[:, :N_EXPERTS] + lg[:, N_EXPERTS:] + _dot(xm_lo, rwh_ref[...])
    tm = logits.shape[0]
    lt = jnp.concatenate([logits, jnp.zeros((tm, LANES - N_EXPERTS), F32)], axis=1).T[:N_EXPERTS, :]
    scores = _sigmoid(lt)
    gates_t = _route(scores + rb_ref[...], scores)
    gates_ref[...] = jnp.concatenate([gates_t, jnp.zeros((LANES - N_EXPERTS, tm), F32)], axis=0).T


def _outproj(gla_out, att_out, x, w_out, g1, sh2, sc2, norm_g, rw_cat, rw_hi, rbias, *, tm):
    b, t, d = x.shape
    nmod = g1.shape[0]
    mod_map = (lambda i, j: (i, 0, 0)) if nmod > 1 else (lambda i, j: (0, 0, 0))
    row = lambda i, j: (i, j, 0)
    full = lambda i, j: (0, 0)
    return pl.pallas_call(
        _outproj_kernel,
        out_shape=(jax.ShapeDtypeStruct((b, t, d), F32),
                   jax.ShapeDtypeStruct((b, t, d), BF16),
                   jax.ShapeDtypeStruct((b, t, LANES), F32)),
        grid=(b, t // tm),
        in_specs=[pl.BlockSpec((None, tm, GLA_V), row),
                  pl.BlockSpec((None, tm, SWA_Q), row),
                  pl.BlockSpec((None, tm, d), row),
                  pl.BlockSpec((d, d), full),
                  pl.BlockSpec((None, 1, d), mod_map),
                  pl.BlockSpec((None, 1, d), mod_map),
                  pl.BlockSpec((None, 1, d), mod_map),
                  pl.BlockSpec((1, d), full),
                  pl.BlockSpec((d, 2 * N_EXPERTS), full),
                  pl.BlockSpec((d, N_EXPERTS), full),
                  pl.BlockSpec((N_EXPERTS, 1), full)],
        out_specs=(pl.BlockSpec((None, tm, d), row),
                   pl.BlockSpec((None, tm, d), row),
                   pl.BlockSpec((None, tm, LANES), row)),
        compiler_params=pltpu.CompilerParams(dimension_semantics=("arbitrary", "arbitrary"),
                                             vmem_limit_bytes=VMEM_LIMIT),
        name="outproj",
    )(gla_out, att_out, x, w_out, g1, sh2, sc2, norm_g, rw_cat, rw_hi, rbias)


MOE_EB = 8
MOE_TM = 1024
PREP_EB = 4


def _expert_prep_kernel(wg_ref, wu_ref, wd_ref, wgu_ref, wdb_ref):
    wgu_ref[...] = jnp.concatenate([wg_ref[...].astype(BF16), wu_ref[...].astype(BF16)], axis=2)
    wdb_ref[...] = wd_ref[...].astype(BF16)


def _expert_prep(wg, wu, wd):
    ne, d, f = wg.shape
    blk = lambda e: (e, 0, 0)
    return pl.pallas_call(
        _expert_prep_kernel,
        out_shape=(jax.ShapeDtypeStruct((ne, d, 2 * f), BF16),
                   jax.ShapeDtypeStruct((ne, f, d), BF16)),
        grid=(ne // PREP_EB,),
        in_specs=[pl.BlockSpec((PREP_EB, d, f), blk),
                  pl.BlockSpec((PREP_EB, d, f), blk),
                  pl.BlockSpec((PREP_EB, f, d), blk)],
        out_specs=(pl.BlockSpec((PREP_EB, d, 2 * f), blk),
                   pl.BlockSpec((PREP_EB, f, d), blk)),
        compiler_params=pltpu.CompilerParams(dimension_semantics=("arbitrary",),
                                             vmem_limit_bytes=VMEM_LIMIT),
        name="expert_prep",
    )(wg, wu, wd)


def _moe_kernel(xm_ref, gt_ref, x1_ref, g2_ref, fg_ref, wgu_ref, wd_ref, swg_ref, swu_ref, swd_ref, o_ref):
    e = pl.program_id(1)
    x = xm_ref[...]

    @pl.when(e == 0)
    def _():
        hs = _silu(_dot(x, swg_ref[...])) * _dot(x, swu_ref[...])
        o_ref[...] = _dot(hs.astype(BF16), swd_ref[...])

    shift = jnp.where(e == 0, 0, LANES - e * MOE_EB)
    gcols = pltpu.roll(gt_ref[...], shift, axis=1)
    hs = []
    for j in range(MOE_EB):
        ab = _dot(x, wgu_ref[j])
        h = _silu(ab[:, :EXPERT_FF]) * ab[:, EXPERT_FF:]
        hs.append((h * gcols[:, j:j + 1]).astype(BF16))
    hcat = jnp.concatenate(hs, axis=1)
    o_ref[...] += _dot(hcat, wd_ref[...].reshape(MOE_EB * EXPERT_FF, D_MODEL))

    @pl.when(e == pl.num_programs(1) - 1)
    def _():
        y = x1_ref[...] + g2_ref[...] * o_ref[...]
        o_ref[...] = _rms_norm(y, fg_ref[...])


def _moe(xm, gates, x1, g2, final_g, wgu, wd, swg, swu, swd, *, tiles_per_mod):
    n, d = xm.shape
    tm = MOE_TM
    row = lambda i, e: (i, 0)
    full = lambda i, e: (0, 0)
    mod_map = lambda i, e: (i // tiles_per_mod, 0, 0)
    blk = lambda i, e: (e, 0, 0)
    return pl.pallas_call(
        _moe_kernel,
        out_shape=jax.ShapeDtypeStruct((n, d), F32),
        grid=(n // tm, N_EXPERTS // MOE_EB),
        in_specs=[pl.BlockSpec((tm, d), row),
                  pl.BlockSpec((tm, LANES), row),
                  pl.BlockSpec((tm, d), row),
                  pl.BlockSpec((None, 1, d), mod_map),
                  pl.BlockSpec((1, d), full),
                  pl.BlockSpec((MOE_EB, d, 2 * EXPERT_FF), blk),
                  pl.BlockSpec((MOE_EB, EXPERT_FF, d), blk),
                  pl.BlockSpec((d, SHARED_FF), full),
                  pl.BlockSpec((d, SHARED_FF), full),
                  pl.BlockSpec((SHARED_FF, d), full)],
        out_specs=pl.BlockSpec((tm, d), row),
        compiler_params=pltpu.CompilerParams(dimension_semantics=("arbitrary", "arbitrary"),
                                             vmem_limit_bytes=VMEM_LIMIT),
        name="moe",
    )(xm, gates, x1, g2, final_g, wgu, wd, swg, swu, swd)


def _stream(x, mods, p, attn_fn, s0=None):
    b, t, d = x.shape
    sh1, sc1, g1, sh2, sc2, g2 = mods
    gla_in, lora, q_s, k_s, v_s = _inproj(x, p["norm_attn_g"], sh1, sc1, p["w_gla"], p["w_lora"],
                                          p["w_swa"], tm=256)
    if s0 is None:
        gla_out, s_f, s_b = _gla(gla_in, lora, p["waf"], p["baf"], p["wab"], p["bab"], p["gla_norm_g"])
    else:
        gla_out, s_f, s_b = _gla(gla_in, lora, p["waf"], p["baf"], p["wab"], p["bab"], p["gla_norm_g"],
                                 s0[0], s0[1])
    att_out = attn_fn(q_s, k_s, v_s)
    x1, xm, gates = _outproj(gla_out, att_out, x, p["w_out"], g1, sh2, sc2, p["norm_ffn_g"],
                             p["rw_cat"], p["rw_hi"], p["rbias"], tm=256)
    n = b * t
    tiles_per_mod = max(t // MOE_TM, 1) if g2.shape[0] > 1 else (n // MOE_TM)
    y = _moe(xm.reshape(n, d), gates.reshape(n, LANES), x1.reshape(n, d), g2, p["final_norm_g"],
             p["wgu"], p["wd"], p["swg"], p["swu"], p["swd"], tiles_per_mod=tiles_per_mod)
    return y.reshape(b, t, d), k_s, v_s, s_f, s_b


def kernel(x_prompt, x_sample, c, cache_swa_k, cache_swa_v, state_gla_fwd, state_gla_bwd, c_ctx, w_ada, b_ada, norm_attn_g, norm_ffn_g, w_in, gla_wa_f, gla_ba_f, gla_wa_b, gla_ba_b, gla_norm_g, swa_sink, w_out, router_w, router_bias, exp_w_gate, exp_w_up, exp_w_down, sh_w_gate, sh_w_up, sh_w_down, final_norm_g):
    l = 0
    d = D_MODEL
    nb_ctx, t_ctx, _ = x_prompt.shape
    nb_lat, t_lat, _ = x_sample.shape

    pad = jnp.zeros((8 - 1 - nb_lat, d), F32)
    cond8 = jnp.concatenate([c_ctx[None, :], c, pad], axis=0)
    mod = _adaln(cond8, w_ada[l], b_ada[l][None, :])
    mods_ctx = [mod[0:1, i * d:(i + 1) * d][:, None, :] for i in range(6)]
    mods_lat = [mod[1:1 + nb_lat, i * d:(i + 1) * d][:, None, :] for i in range(6)]

    wgu, wdb = _expert_prep(exp_w_gate[l], exp_w_up[l], exp_w_down[l])
    zeros_lora = jnp.zeros((GLA_LORA, GLA_QK), F32)
    rw = router_w[l]
    rw_hi = rw.astype(BF16)
    rw_lo = (rw - rw_hi.astype(F32)).astype(BF16)
    w_in_b = w_in[l].astype(BF16)
    p = {
        "norm_attn_g": norm_attn_g[l][None, :],
        "norm_ffn_g": norm_ffn_g[l][None, :],
        "final_norm_g": final_norm_g[None, :],
        "w_gla": w_in_b[:, :2 * GLA_QK + 2 * GLA_V],
        "w_lora": w_in_b[:, 2 * GLA_QK + 2 * GLA_V:2 * GLA_QK + 2 * GLA_V + 2 * GLA_LORA],
        "w_swa": w_in_b[:, 2 * GLA_QK + 2 * GLA_V + 2 * GLA_LORA:],
        "waf": jnp.concatenate([gla_wa_f[l], zeros_lora], axis=0).astype(BF16),
        "wab": jnp.concatenate([zeros_lora, gla_wa_b[l]], axis=0).astype(BF16),
        "baf": gla_ba_f[l][None, :],
        "bab": gla_ba_b[l][None, :],
        "gla_norm_g": gla_norm_g[l][None, :],
        "w_out": w_out[l].astype(BF16),
        "rw_cat": jnp.concatenate([rw_hi, rw_lo], axis=1),
        "rw_hi": rw_hi,
        "rbias": router_bias[l][:, None],
        "wgu": wgu, "wd": wdb,
        "swg": sh_w_gate[l].astype(BF16), "swu": sh_w_up[l].astype(BF16),
        "swd": sh_w_down[l].astype(BF16),
    }
    sink = swa_sink[l]

    y_prompt, k_c, v_c, s_f, s_b = _stream(x_prompt, mods_ctx, p, functools.partial(_attn_ctx, sink))

    cos, sin_lo, sin_hi = _rope_tables(t_lat)
    kc = cache_swa_k[:, l].reshape(nb_lat, -1, SWA_KV)
    vc = cache_swa_v[:, l].reshape(nb_lat, -1, SWA_KV)
    lat_attn = lambda q, k, v: _attn_lat(sink, q, k, v, kc, vc, cos, sin_lo, sin_hi)
    s0 = (state_gla_fwd[:, l].reshape(nb_lat, GLA_QK, GLA_DV),
          state_gla_bwd[:, l].reshape(nb_lat, GLA_QK, GLA_DV))
    y_sample, _, _, _, _ = _stream(x_sample, mods_lat, p, lat_attn, s0)

    new_k = k_c.reshape(nb_ctx, 1, t_ctx, SWA_KV_HEADS, SWA_HEAD_DIM)
    new_v = v_c.reshape(nb_ctx, 1, t_ctx, SWA_KV_HEADS, SWA_HEAD_DIM)
    new_sf = s_f.reshape(nb_ctx, 1, GLA_HEADS, GLA_DK, GLA_DV)
    new_sb = s_b.reshape(nb_ctx, 1, GLA_HEADS, GLA_DK, GLA_DV)
    return (y_prompt, y_sample, new_k, new_v, new_sf, new_sb)
```

```python
import functools

import jax
import jax.numpy as jnp
from jax import lax
from jax.experimental import pallas as pl
from jax.experimental.pallas import tpu as pltpu

F32 = jnp.float32
BF16 = jnp.bfloat16

D_MODEL = 1024
GLA_HEADS = 4
GLA_DK = 64
GLA_DV = 128
GLA_LORA = 16
GLA_GATE_NORM = 16.0
GLA_CHUNK = 64
GLA_QK = GLA_HEADS * GLA_DK
GLA_V = GLA_HEADS * GLA_DV
SWA_HEAD_DIM = 64
SWA_HEADS = 8
SWA_KV_HEADS = 2
SWA_Q = SWA_HEADS * SWA_HEAD_DIM
SWA_KV = SWA_KV_HEADS * SWA_HEAD_DIM
ATTN_BLOCK = 128
GRID_W = 64
ROPE_BASE = 10000.0
N_EXPERTS = 64
TOP_K = 8
N_EXPERT_GROUPS = 8
TOPK_GROUPS = 4
EXPERT_FF = 128
SHARED_FF = 256
ROUTED_SCALE = 2.5
EPS = 1e-6

LANES = 128
VMEM_LIMIT = 56 * 1024 * 1024

NEG_INF = float("-inf")


def _dot(a, b):
    return jnp.dot(a, b, preferred_element_type=F32)


def _dot_nt(a, b):
    return lax.dot_general(a, b, (((1,), (1,)), ((), ())), preferred_element_type=F32)


def _split_hi_lo(x):
    hi = x.astype(BF16)
    lo = (x - hi.astype(F32)).astype(BF16)
    return hi, lo


def _sigmoid(x):
    return 1.0 / (1.0 + jnp.exp(-x))


def _silu(x):
    return x * _sigmoid(x)


def _rms_norm(x, g):
    ms = jnp.mean(x * x, axis=-1, keepdims=True)
    return x * lax.rsqrt(ms + EPS) * g


def _adaln_kernel(c_ref, w_ref, b_ref, o_ref):
    a_hi, a_lo = _split_hi_lo(_silu(c_ref[...]))
    w_hi, w_lo = _split_hi_lo(w_ref[...])
    o_ref[...] = _dot(a_hi, w_hi) + _dot(a_lo, w_hi) + _dot(a_hi, w_lo) + b_ref[...]


def _adaln(cond8, w_ada, b_ada):
    n = w_ada.shape[1]
    tn = 1536
    return pl.pallas_call(
        _adaln_kernel,
        out_shape=jax.ShapeDtypeStruct((8, n), F32),
        grid=(n // tn,),
        in_specs=[pl.BlockSpec((8, D_MODEL), lambda j: (0, 0)),
                  pl.BlockSpec((D_MODEL, tn), lambda j: (0, j)),
                  pl.BlockSpec((1, tn), lambda j: (0, j))],
        out_specs=pl.BlockSpec((8, tn), lambda j: (0, j)),
        compiler_params=pltpu.CompilerParams(dimension_semantics=("arbitrary",),
                                             vmem_limit_bytes=VMEM_LIMIT),
        name="adaln",
    )(cond8, w_ada, b_ada)


def _inproj_kernel(x_ref, g_ref, sh_ref, sc_ref, wg_ref, wl_ref, ws_ref,
                   gla_ref, lora_ref, q_ref, k_ref, v_ref):
    h = _rms_norm(x_ref[...], g_ref[...]) * (1.0 + sc_ref[...]) + sh_ref[...]
    hb = h.astype(BF16)
    gla_ref[...] = _dot(hb, wg_ref[...])
    lora_ref[...] = _dot(hb, wl_ref[...])
    s = _dot(hb, ws_ref[...])
    q_ref[...] = s[:, :SWA_Q]
    k_ref[...] = s[:, SWA_Q:SWA_Q + SWA_KV]
    v_ref[...] = s[:, SWA_Q + SWA_KV:]


def _inproj(x, g, sh, sc, w_gla, w_lora, w_swa, *, tm):
    b, t, d = x.shape
    nmod = sh.shape[0]
    mod_map = (lambda i, j: (i, 0, 0)) if nmod > 1 else (lambda i, j: (0, 0, 0))
    row = lambda i, j: (i, j, 0)
    full = lambda i, j: (0, 0)
    n_gla = w_gla.shape[1]
    n_lora = w_lora.shape[1]
    return pl.pallas_call(
        _inproj_kernel,
        out_shape=(jax.ShapeDtypeStruct((b, t, n_gla), F32),
                   jax.ShapeDtypeStruct((b, t, n_lora), F32),
                   jax.ShapeDtypeStruct((b, t, SWA_Q), F32),
                   jax.ShapeDtypeStruct((b, t, SWA_KV), F32),
                   jax.ShapeDtypeStruct((b, t, SWA_KV), F32)),
        grid=(b, t // tm),
        in_specs=[pl.BlockSpec((None, tm, d), row),
                  pl.BlockSpec((1, d), full),
                  pl.BlockSpec((None, 1, d), mod_map),
                  pl.BlockSpec((None, 1, d), mod_map),
                  pl.BlockSpec((d, n_gla), full),
                  pl.BlockSpec((d, n_lora), full),
                  pl.BlockSpec((d, w_swa.shape[1]), full)],
        out_specs=(pl.BlockSpec((None, tm, n_gla), row),
                   pl.BlockSpec((None, tm, n_lora), row),
                   pl.BlockSpec((None, tm, SWA_Q), row),
                   pl.BlockSpec((None, tm, SWA_KV), row),
                   pl.BlockSpec((None, tm, SWA_KV), row)),
        compiler_params=pltpu.CompilerParams(dimension_semantics=("arbitrary", "arbitrary"),
                                             vmem_limit_bytes=VMEM_LIMIT),
        name="inproj",
    )(x, g, sh, sc, w_gla, w_lora, w_swa)


SCAN_UNROLL = 2
OUT_UNROLL = 4


def _log_sigmoid(x):
    return jnp.minimum(x, 0.0) - jnp.log(1.0 + jnp.exp(-jnp.abs(x)))


def _heads_to_rows(x):
    return jnp.concatenate([x[:, h * LANES:(h + 1) * LANES] for h in range(GLA_HEADS)], axis=0)


def _rows_to_heads(x, c):
    return jnp.concatenate([x[h * c:(h + 1) * c, :] for h in range(GLA_HEADS)], axis=1)


def _gla_kernel(has_init, q_ref, k_ref, v_ref, g_ref, lora_ref, waf_ref, baf_ref, wab_ref, bab_ref,
                ng_ref, *rest):
    if has_init:
        s0f_ref, s0b_ref, *rest = rest
    (out_ref, sf_ref, sb_ref, laf_ref, lab_ref, oacc_ref, qtf_ref, qtb_ref, saf_ref, sab_ref,
     stf_ref, stb_ref) = rest
    t = q_ref.shape[0]
    c = GLA_CHUNK
    n = t // c
    hc = GLA_HEADS * c

    lora = lora_ref[...].astype(BF16)
    laf_ref[...] = _log_sigmoid(_dot(lora, waf_ref[...]) + baf_ref[...]) * (1.0 / GLA_GATE_NORM)
    lab_ref[...] = _log_sigmoid(_dot(lora, wab_ref[...]) + bab_ref[...]) * (1.0 / GLA_GATE_NORM)

    if has_init:
        stf_ref[...] = s0f_ref[...].T
        stb_ref[...] = s0b_ref[...].T
    else:
        stf_ref[...] = jnp.zeros_like(stf_ref)
        stb_ref[...] = jnp.zeros_like(stb_ref)
    oacc_ref[...] = jnp.zeros_like(oacc_ref)

    r64 = lax.broadcasted_iota(jnp.int32, (c, c), 0)
    c64 = lax.broadcasted_iota(jnp.int32, (c, c), 1)
    tri_f = jnp.where(c64 <= r64, 1.0, 0.0).astype(BF16)
    tri_b = jnp.where(c64 >= r64, 1.0, 0.0).astype(BF16)
    rr = lax.broadcasted_iota(jnp.int32, (hc, hc), 0)
    cc = lax.broadcasted_iota(jnp.int32, (hc, hc), 1)
    same_head = (rr >> 6) == (cc >> 6)
    keep_f = same_head & ((rr & (c - 1)) >= (cc & (c - 1)))
    keep_b = same_head & ((rr & (c - 1)) <= (cc & (c - 1)))
    head_mask = jnp.where(same_head, 1.0, 0.0).astype(BF16)
    norm_g = ng_ref[...]

    def chunk_rows(ci):
        return pl.ds(pl.multiple_of(ci * c, c), c)

    def tile_heads(x):
        x4 = jnp.concatenate([x] * GLA_HEADS, axis=0)
        return jnp.where(same_head, x4, 0.0).astype(BF16)

    def scan_step(i, carry):
        dirs = []
        for u in range(SCAN_UNROLL):
            dirs += [(SCAN_UNROLL * i + u, laf_ref, tri_f, keep_f, c - 1, stf_ref, saf_ref, qtf_ref),
                     (n - 1 - SCAN_UNROLL * i - u, lab_ref, tri_b, keep_b, 0, stb_ref, sab_ref, qtb_ref)]
        cums = []
        for ci, la_ref, tri, _, _, _, _, _ in dirs:
            la_hi, la_lo = _split_hi_lo(la_ref[chunk_rows(ci), :])
            cums.append(_dot(tri, la_hi) + _dot(tri, la_lo))
        ops = []
        for (ci, _, _, _, last_row, _, _, qt_ref), cum in zip(dirs, cums):
            sl = chunk_rows(ci)
            tot = cum[last_row:last_row + 1, :]
            kc = k_ref[sl, :]
            qt = q_ref[sl, :] * (GLA_DK ** -0.5) * jnp.exp(cum)
            qt_ref[sl, :] = qt.astype(BF16)
            v_rows = _heads_to_rows(v_ref[sl, :])
            ops.append((tot, tile_heads(qt), tile_heads(kc * jnp.exp(-cum)),
                        tile_heads(kc * jnp.exp(tot - cum)), v_rows))
        atts = [_dot_nt(q4, k4) for _, q4, k4, _, _ in ops]
        incs = []
        for (_, _, _, keep, _, _, _, _), (_, _, _, kd4, v_rows), att in zip(dirs, ops, atts):
            att = jnp.where(keep, att, 0.0).astype(BF16)
            incs.append((_dot(att, v_rows.astype(BF16)), _dot(v_rows.T.astype(BF16), kd4)))
        for (ci, _, _, _, _, st_ref, snap_ref, _), (tot, _, _, _, _), (o_intra, st_inc) in zip(dirs, ops, incs):
            oacc_ref[ci] += o_intra
            st = st_ref[...]
            snap_ref[ci] = st.astype(BF16)
            st_ref[...] = jnp.exp(tot) * st + st_inc
        return carry

    def tile_heads_bf16(x):
        return jnp.concatenate([x] * GLA_HEADS, axis=0) * head_mask

    def out_step(i, carry):
        chunks = [OUT_UNROLL * i + u for u in range(OUT_UNROLL)]
        inter = []
        for ci in chunks:
            sl = chunk_rows(ci)
            q4 = jnp.concatenate([tile_heads_bf16(qtf_ref[sl, :]), tile_heads_bf16(qtb_ref[sl, :])], axis=1)
            st = jnp.concatenate([saf_ref[ci], sab_ref[ci]], axis=1)
            inter.append(_dot_nt(q4, st))
        for ci, o_inter in zip(chunks, inter):
            sl = chunk_rows(ci)
            on = _rms_norm(oacc_ref[ci] + o_inter, norm_g)
            gate = _silu(_heads_to_rows(g_ref[sl, :]))
            out_ref[sl, :] = _rows_to_heads(on * gate, c)
        return carry

    lax.fori_loop(0, n // SCAN_UNROLL, scan_step, 0)
    lax.fori_loop(0, n // OUT_UNROLL, out_step, 0)
    sf_ref[...] = stf_ref[...].T
    sb_ref[...] = stb_ref[...].T


def _gla(gla_in, lora, waf, baf, wab, bab, norm_g, s0f=None, s0b=None):
    b, t, _ = gla_in.shape
    has_init = s0f is not None
    n = t // GLA_CHUNK
    bmap = lambda i: (i, 0, 0)
    full = lambda i: (0, 0)
    in_specs = [pl.BlockSpec((None, t, GLA_QK), lambda i: (i, 0, 0)),
                pl.BlockSpec((None, t, GLA_QK), lambda i: (i, 0, 1)),
                pl.BlockSpec((None, t, GLA_V), lambda i: (i, 0, 1)),
                pl.BlockSpec((None, t, GLA_V), lambda i: (i, 0, 2)),
                pl.BlockSpec((None, t, 2 * GLA_LORA), bmap),
                pl.BlockSpec((2 * GLA_LORA, GLA_QK), full),
                pl.BlockSpec((1, GLA_QK), full),
                pl.BlockSpec((2 * GLA_LORA, GLA_QK), full),
                pl.BlockSpec((1, GLA_QK), full),
                pl.BlockSpec((1, GLA_DV), full)]
    args = [gla_in, gla_in, gla_in, gla_in, lora, waf, baf, wab, bab, norm_g]
    if has_init:
        in_specs += [pl.BlockSpec((None, GLA_QK, GLA_DV), bmap)] * 2
        args += [s0f, s0b]
    return pl.pallas_call(
        functools.partial(_gla_kernel, has_init),
        out_shape=(jax.ShapeDtypeStruct((b, t, GLA_V), F32),
                   jax.ShapeDtypeStruct((b, GLA_QK, GLA_DV), F32),
                   jax.ShapeDtypeStruct((b, GLA_QK, GLA_DV), F32)),
        grid=(b,),
        in_specs=in_specs,
        out_specs=(pl.BlockSpec((None, t, GLA_V), bmap),
                   pl.BlockSpec((None, GLA_QK, GLA_DV), bmap),
                   pl.BlockSpec((None, GLA_QK, GLA_DV), bmap)),
        scratch_shapes=[pltpu.VMEM((t, GLA_QK), F32),
                        pltpu.VMEM((t, GLA_QK), F32),
                        pltpu.VMEM((n, GLA_HEADS * GLA_CHUNK, GLA_DV), F32),
                        pltpu.VMEM((t, GLA_QK), BF16),
                        pltpu.VMEM((t, GLA_QK), BF16),
                        pltpu.VMEM((n, GLA_DV, GLA_QK), BF16),
                        pltpu.VMEM((n, GLA_DV, GLA_QK), BF16),
                        pltpu.VMEM((GLA_DV, GLA_QK), F32),
                        pltpu.VMEM((GLA_DV, GLA_QK), F32)],
        compiler_params=pltpu.CompilerParams(dimension_semantics=("arbitrary",),
                                             vmem_limit_bytes=VMEM_LIMIT),
        name="gla",
    )(*args)


def _dup_groups(x):
    lo = lax.broadcasted_iota(jnp.int32, x.shape, 1) < SWA_HEAD_DIM
    xr = pltpu.roll(x, SWA_HEAD_DIM, axis=1)
    return jnp.where(lo, x, xr), jnp.where(lo, xr, x)


def _pairs_attention(qps, sinks, k_dups, vt_dups, mask):
    nq = qps[0].shape[0]
    lo = lax.broadcasted_iota(jnp.int32, (nq, LANES), 1) < SWA_HEAD_DIM
    even = lax.broadcasted_iota(jnp.int32, (1, 2 * nq), 1) < nq
    scores = []
    for qp, k_dup in zip(qps, k_dups):
        q2 = jnp.concatenate([jnp.where(lo, qp, 0.0), jnp.where(lo, 0.0, qp)], axis=0).astype(BF16)
        scores.append(_dot_nt(k_dup, q2))
    probs = []
    for s, (sink_even, sink_odd) in zip(scores, sinks):
        if mask is not None:
            s = jnp.where(mask, s, NEG_INF)
        sink = jnp.where(even, sink_even, sink_odd)
        m = jnp.maximum(jnp.max(s, axis=0, keepdims=True), sink)
        p = jnp.exp(s - m)
        denom = jnp.sum(p, axis=0, keepdims=True) + jnp.exp(sink - m)
        probs.append((p.astype(BF16), 1.0 / denom))
    outs = []
    for (p, rdenom), vt_dup in zip(probs, vt_dups):
        o = _dot(vt_dup, p) * rdenom
        outs.append(jnp.concatenate([o[:SWA_HEAD_DIM, :nq], o[SWA_HEAD_DIM:, nq:]], axis=0).T)
    return outs


def _attn_ctx_kernel(sink_ref, q_ref, k_ref, v_ref, o_ref):
    kd = [x.astype(BF16) for x in _dup_groups(k_ref[...])]
    vt = [x.T.astype(BF16) for x in _dup_groups(v_ref[...])]
    scale = SWA_HEAD_DIM ** -0.5
    pairs = range(SWA_HEADS // 2)
    outs = _pairs_attention([q_ref[:, pr * LANES:(pr + 1) * LANES] * scale for pr in pairs],
                            [(sink_ref[2 * pr], sink_ref[2 * pr + 1]) for pr in pairs],
                            [kd[pr // 2] for pr in pairs], [vt[pr // 2] for pr in pairs], None)
    for pr in pairs:
        o_ref[:, pr * LANES:(pr + 1) * LANES] = outs[pr]


def _attn_ctx(sink, q, k, v):
    b, t, _ = q.shape
    bmap = lambda i: (i, 0, 0)
    return pl.pallas_call(
        _attn_ctx_kernel,
        out_shape=jax.ShapeDtypeStruct((b, t, SWA_Q), F32),
        grid=(b,),
        in_specs=[pl.BlockSpec(memory_space=pltpu.SMEM),
                  pl.BlockSpec((None, t, SWA_Q), bmap),
                  pl.BlockSpec((None, t, SWA_KV), bmap),
                  pl.BlockSpec((None, t, SWA_KV), bmap)],
        out_specs=pl.BlockSpec((None, t, SWA_Q), bmap),
        compiler_params=pltpu.CompilerParams(dimension_semantics=("arbitrary",),
                                             vmem_limit_bytes=VMEM_LIMIT),
        name="attn_ctx",
    )(sink, q, k, v)


def _rope(x, cos, sin_lo, sin_hi):
    return x * cos + pltpu.roll(x, LANES - 16, axis=1) * sin_lo + pltpu.roll(x, 16, axis=1) * sin_hi


def _attn_lat_kernel(sink_ref, q_ref, k_ref, v_ref, kc_ref, vc_ref, cos_ref, sl_ref, sh_ref,
                     o_ref, kw_ref, vw_ref):
    t = q_ref.shape[0]
    ab = ATTN_BLOCK
    nb = t // ab
    scale = SWA_HEAD_DIM ** -0.5

    k_rot = _dup_groups(_rope(k_ref[...], cos_ref[...], sl_ref[...], sh_ref[...]))
    v_dup = _dup_groups(v_ref[...])
    zeros = jnp.zeros((ab, LANES), BF16)
    for grp in range(SWA_KV_HEADS):
        kw_ref[grp, 0:ab, :] = zeros
        kw_ref[grp, ab:ab + t, :] = k_rot[grp].astype(BF16)
        kw_ref[grp, ab + t:, :] = zeros
        vw_ref[grp, 0] = zeros
        for blk in range(nb):
            vw_ref[grp, blk + 1] = v_dup[grp][blk * ab:(blk + 1) * ab, :].T.astype(BF16)
        vw_ref[grp, nb + 1] = zeros
    kc = [x.astype(BF16) for x in _dup_groups(kc_ref[...])]
    vct = [x.T.astype(BF16) for x in _dup_groups(vc_ref[...])]
    lc = kc_ref.shape[0]

    key = lax.broadcasted_iota(jnp.int32, (lc + 3 * ab, 2 * ab), 0) - lc
    tq = lax.broadcasted_iota(jnp.int32, (lc + 3 * ab, 2 * ab), 1) & (ab - 1)
    band = (key < 0) | (jnp.abs(tq + ab - key) <= ab)

    def block(nq, carry):
        row0 = pl.multiple_of(nq * ab, ab)
        s_abs = key + (nq - 1) * ab
        mask = band & ((key < 0) | ((s_abs >= 0) & (s_abs < t)))
        cos = cos_ref[pl.ds(row0, ab), :]
        s_lo = sl_ref[pl.ds(row0, ab), :]
        s_hi = sh_ref[pl.ds(row0, ab), :]
        k_all = [jnp.concatenate([kc[grp], kw_ref[grp, pl.ds(row0, 3 * ab), :]], axis=0)
                 for grp in range(SWA_KV_HEADS)]
        vt_all = [jnp.concatenate([vct[grp], vw_ref[grp, nq], vw_ref[grp, nq + 1], vw_ref[grp, nq + 2]],
                                  axis=1) for grp in range(SWA_KV_HEADS)]
        pairs = range(SWA_HEADS // 2)
        qps = [_rope(q_ref[pl.ds(row0, ab), pr * LANES:(pr + 1) * LANES], cos, s_lo, s_hi) * scale
               for pr in pairs]
        outs = _pairs_attention(qps, [(sink_ref[2 * pr], sink_ref[2 * pr + 1]) for pr in pairs],
                                [k_all[pr // 2] for pr in pairs], [vt_all[pr // 2] for pr in pairs], mask)
        for pr in pairs:
            o_ref[pl.ds(row0, ab), pr * LANES:(pr + 1) * LANES] = outs[pr]
        return carry

    lax.fori_loop(0, nb, block, 0)


def _attn_lat(sink, q, k, v, kc, vc, cos, sin_lo, sin_hi):
    b, t, _ = q.shape
    lc = kc.shape[1]
    bmap = lambda i: (i, 0, 0)
    full = lambda i: (0, 0)
    return pl.pallas_call(
        _attn_lat_kernel,
        out_shape=jax.ShapeDtypeStruct((b, t, SWA_Q), F32),
        grid=(b,),
        in_specs=[pl.BlockSpec(memory_space=pltpu.SMEM),
                  pl.BlockSpec((None, t, SWA_Q), bmap),
                  pl.BlockSpec((None, t, SWA_KV), bmap),
                  pl.BlockSpec((None, t, SWA_KV), bmap),
                  pl.BlockSpec((None, lc, SWA_KV), bmap),
                  pl.BlockSpec((None, lc, SWA_KV), bmap),
                  pl.BlockSpec((t, LANES), full),
                  pl.BlockSpec((t, LANES), full),
                  pl.BlockSpec((t, LANES), full)],
        out_specs=pl.BlockSpec((None, t, SWA_Q), bmap),
        scratch_shapes=[pltpu.VMEM((SWA_KV_HEADS, t + 2 * ATTN_BLOCK, LANES), BF16),
                        pltpu.VMEM((SWA_KV_HEADS, t // ATTN_BLOCK + 2, LANES, ATTN_BLOCK), BF16)],
        compiler_params=pltpu.CompilerParams(dimension_semantics=("arbitrary",),
                                             vmem_limit_bytes=VMEM_LIMIT),
        name="attn_lat",
    )(sink, q, k, v, kc, vc, cos, sin_lo, sin_hi)


def _rope_tables(t):
    half = SWA_HEAD_DIM // 2
    quarter = half // 2
    pos = jnp.arange(t)
    row = (pos // GRID_W).astype(F32)
    col = (pos % GRID_W).astype(F32)
    inv_freq = ROPE_BASE ** (-jnp.arange(quarter, dtype=F32) / quarter)
    lane = jnp.arange(LANES)
    d = lane % SWA_HEAD_DIM
    freq = inv_freq[d % quarter]
    use_row = (d < half)
    ang = jnp.where(use_row[None, :], row[:, None], col[:, None]) * freq[None, :]
    cos = jnp.cos(ang)
    sin = jnp.sin(ang)
    lower = (d % half) < quarter
    return cos, jnp.where(lower[None, :], -sin, 0.0), jnp.where(lower[None, :], 0.0, sin)


def _route(sel, scores):
    n = sel.shape[1]
    gsz = N_EXPERTS // N_EXPERT_GROUPS

    def first_max(x, idx, size):
        m = jnp.max(x, axis=0, keepdims=True)
        first = jnp.min(jnp.where(x == m, idx, float(size)), axis=0, keepdims=True)
        return m, idx == first

    i8 = lax.broadcasted_iota(jnp.int32, (gsz, n), 0).astype(F32)
    rows = []
    for g in range(N_EXPERT_GROUPS):
        slab = sel[g * gsz:(g + 1) * gsz, :]
        m1, hit = first_max(slab, i8, gsz)
        m2 = jnp.max(jnp.where(hit, NEG_INF, slab), axis=0, keepdims=True)
        rows.append(m1 + m2)
    gscore = jnp.concatenate(rows, axis=0)
    gsel = jnp.zeros((N_EXPERT_GROUPS, n), F32)
    for _ in range(TOPK_GROUPS):
        _, hit = first_max(gscore, i8, N_EXPERT_GROUPS)
        gsel = jnp.where(hit, 1.0, gsel)
        gscore = jnp.where(hit, NEG_INF, gscore)
    emask = jnp.concatenate(
        [jnp.broadcast_to(gsel[g:g + 1, :], (gsz, n)) for g in range(N_EXPERT_GROUPS)], axis=0)
    cand = jnp.where(emask > 0.5, sel, NEG_INF)
    ie = lax.broadcasted_iota(jnp.int32, (N_EXPERTS, n), 0).astype(F32)
    w = jnp.zeros((N_EXPERTS, n), F32)
    for _ in range(TOP_K):
        _, hit = first_max(cand, ie, N_EXPERTS)
        w = jnp.where(hit, scores, w)
        cand = jnp.where(hit, NEG_INF, cand)
    return w / jnp.sum(w, axis=0, keepdims=True) * ROUTED_SCALE


def _outproj_kernel(gla_ref, att_ref, x_ref, wo_ref, g1_ref, sh_ref, sc_ref, ng_ref, rw_ref, rwh_ref,
                    rb_ref, x1_ref, xm_ref, gates_ref):
    y = (_dot(gla_ref[...].astype(BF16), wo_ref[0:GLA_V, :])
         + _dot(att_ref[...].astype(BF16), wo_ref[GLA_V:, :]))
    x1 = x_ref[...] + g1_ref[...] * y
    x1_ref[...] = x1
    xm = _rms_norm(x1, ng_ref[...]) * (1.0 + sc_ref[...]) + sh_ref[...]
    xm_hi, xm_lo = _split_hi_lo(xm)
    xm_ref[...] = xm_hi
    lg = _dot(xm_hi, rw_ref[...])
    logits = lg[:, :N_EXPERTS] + lg[:, N_EXPERTS:] + _dot(xm_lo, rwh_ref[...])
    tm = logits.shape[0]
    lt = jnp.concatenate([logits, jnp.zeros((tm, LANES - N_EXPERTS), F32)], axis=1).T[:N_EXPERTS, :]
    scores = _sigmoid(lt)
    gates_t = _route(scores + rb_ref[...], scores)
    gates_ref[...] = jnp.concatenate([gates_t, jnp.zeros((LANES - N_EXPERTS, tm), F32)], axis=0).T


def _outproj(gla_out, att_out, x, w_out, g1, sh2, sc2, norm_g, rw_cat, rw_hi, rbias, *, tm):
    b, t, d = x.shape
    nmod = g1.shape[0]
    mod_map = (lambda i, j: (i, 0, 0)) if nmod > 1 else (lambda i, j: (0, 0, 0))
    row = lambda i, j: (i, j, 0)
    full = lambda i, j: (0, 0)
    return pl.pallas_call(
        _outproj_kernel,
        out_shape=(jax.ShapeDtypeStruct((b, t, d), F32),
                   jax.ShapeDtypeStruct((b, t, d), BF16),
                   jax.ShapeDtypeStruct((b, t, LANES), F32)),
        grid=(b, t // tm),
        in_specs=[pl.BlockSpec((None, tm, GLA_V), row),
                  pl.BlockSpec((None, tm, SWA_Q), row),
                  pl.BlockSpec((None, tm, d), row),
                  pl.BlockSpec((d, d), full),
                  pl.BlockSpec((None, 1, d), mod_map),
                  pl.BlockSpec((None, 1, d), mod_map),
                  pl.BlockSpec((None, 1, d), mod_map),
                  pl.BlockSpec((1, d), full),
                  pl.BlockSpec((d, 2 * N_EXPERTS), full),
                  pl.BlockSpec((d, N_EXPERTS), full),
                  pl.BlockSpec((N_EXPERTS, 1), full)],
        out_specs=(pl.BlockSpec((None, tm, d), row),
                   pl.BlockSpec((None, tm, d), row),
                   pl.BlockSpec((None, tm, LANES), row)),
        compiler_params=pltpu.CompilerParams(dimension_semantics=("arbitrary", "arbitrary"),
                                             vmem_limit_bytes=VMEM_LIMIT),
        name="outproj",
    )(gla_out, att_out, x, w_out, g1, sh2, sc2, norm_g, rw_cat, rw_hi, rbias)


MOE_EB = 8
MOE_TM = 1024
PREP_EB = 4


def _expert_prep_kernel(wg_ref, wu_ref, wd_ref, wgu_ref, wdb_ref):
    wgu_ref[...] = jnp.concatenate([wg_ref[...].astype(BF16), wu_ref[...].astype(BF16)], axis=2)
    wdb_ref[...] = wd_ref[...].astype(BF16)


def _expert_prep(wg, wu, wd):
    ne, d, f = wg.shape
    blk = lambda e: (e, 0, 0)
    return pl.pallas_call(
        _expert_prep_kernel,
        out_shape=(jax.ShapeDtypeStruct((ne, d, 2 * f), BF16),
                   jax.ShapeDtypeStruct((ne, f, d), BF16)),
        grid=(ne // PREP_EB,),
        in_specs=[pl.BlockSpec((PREP_EB, d, f), blk),
                  pl.BlockSpec((PREP_EB, d, f), blk),
                  pl.BlockSpec((PREP_EB, f, d), blk)],
        out_specs=(pl.BlockSpec((PREP_EB, d, 2 * f), blk),
                   pl.BlockSpec((PREP_EB, f, d), blk)),
        compiler_params=pltpu.CompilerParams(dimension_semantics=("arbitrary",),
                                             vmem_limit_bytes=VMEM_LIMIT),
        name="expert_prep",
    )(wg, wu, wd)


def _moe_kernel(xm_ref, gt_ref, x1_ref, g2_ref, fg_ref, wgu_ref, wd_ref, swg_ref, swu_ref, swd_ref, o_ref):
    e = pl.program_id(1)
    x = xm_ref[...]

    @pl.when(e == 0)
    def _():
        hs = _silu(_dot(x, swg_ref[...])) * _dot(x, swu_ref[...])
        o_ref[...] = _dot(hs.astype(BF16), swd_ref[...])

    shift = jnp.where(e == 0, 0, LANES - e * MOE_EB)
    gcols = pltpu.roll(gt_ref[...], shift, axis=1)
    hs = []
    for j in range(MOE_EB):
        ab = _dot(x, wgu_ref[j])
        h = _silu(ab[:, :EXPERT_FF]) * ab[:, EXPERT_FF:]
        hs.append((h * gcols[:, j:j + 1]).astype(BF16))
    hcat = jnp.concatenate(hs, axis=1)
    o_ref[...] += _dot(hcat, wd_ref[...].reshape(MOE_EB * EXPERT_FF, D_MODEL))

    @pl.when(e == pl.num_programs(1) - 1)
    def _():
        y = x1_ref[...] + g2_ref[...] * o_ref[...]
        o_ref[...] = _rms_norm(y, fg_ref[...])


def _moe(xm, gates, x1, g2, final_g, wgu, wd, swg, swu, swd, *, tiles_per_mod):
    n, d = xm.shape
    tm = MOE_TM
    row = lambda i, e: (i, 0)
    full = lambda i, e: (0, 0)
    mod_map = lambda i, e: (i // tiles_per_mod, 0, 0)
    blk = lambda i, e: (e, 0, 0)
    return pl.pallas_call(
        _moe_kernel,
        out_shape=jax.ShapeDtypeStruct((n, d), F32),
        grid=(n // tm, N_EXPERTS // MOE_EB),
        in_specs=[pl.BlockSpec((tm, d), row),
                  pl.BlockSpec((tm, LANES), row),
                  pl.BlockSpec((tm, d), row),
                  pl.BlockSpec((None, 1, d), mod_map),
                  pl.BlockSpec((1, d), full),
                  pl.BlockSpec((MOE_EB, d, 2 * EXPERT_FF), blk),
                  pl.BlockSpec((MOE_EB, EXPERT_FF, d), blk),
                  pl.BlockSpec((d, SHARED_FF), full),
                  pl.BlockSpec((d, SHARED_FF), full),
                  pl.BlockSpec((SHARED_FF, d), full)],
        out_specs=pl.BlockSpec((tm, d), row),
        compiler_params=pltpu.CompilerParams(dimension_semantics=("arbitrary", "arbitrary"),
                                             vmem_limit_bytes=VMEM_LIMIT),
        name="moe",
    )(xm, gates, x1, g2, final_g, wgu, wd, swg, swu, swd)


def _stream(x, mods, p, attn_fn, s0=None):
    b, t, d = x.shape
    sh1, sc1, g1, sh2, sc2, g2 = mods
    gla_in, lora, q_s, k_s, v_s = _inproj(x, p["norm_attn_g"], sh1, sc1, p["w_gla"], p["w_lora"],
                                          p["w_swa"], tm=256)
    if s0 is None:
        gla_out, s_f, s_b = _gla(gla_in, lora, p["waf"], p["baf"], p["wab"], p["bab"], p["gla_norm_g"])
    else:
        gla_out, s_f, s_b = _gla(gla_in, lora, p["waf"], p["baf"], p["wab"], p["bab"], p["gla_norm_g"],
                                 s0[0], s0[1])
    att_out = attn_fn(q_s, k_s, v_s)
    x1, xm, gates = _outproj(gla_out, att_out, x, p["w_out"], g1, sh2, sc2, p["norm_ffn_g"],
                             p["rw_cat"], p["rw_hi"], p["rbias"], tm=256)
    n = b * t
    tiles_per_mod = max(t // MOE_TM, 1) if g2.shape[0] > 1 else (n // MOE_TM)
    y = _moe(xm.reshape(n, d), gates.reshape(n, LANES), x1.reshape(n, d), g2, p["final_norm_g"],
             p["wgu"], p["wd"], p["swg"], p["swu"], p["swd"], tiles_per_mod=tiles_per_mod)
    return y.reshape(b, t, d), k_s, v_s, s_f, s_b


def kernel(x_prompt, x_sample, c, cache_swa_k, cache_swa_v, state_gla_fwd, state_gla_bwd, c_ctx, w_ada, b_ada, norm_attn_g, norm_ffn_g, w_in, gla_wa_f, gla_ba_f, gla_wa_b, gla_ba_b, gla_norm_g, swa_sink, w_out, router_w, router_bias, exp_w_gate, exp_w_up, exp_w_down, sh_w_gate, sh_w_up, sh_w_down, final_norm_g):
    l = 0
    d = D_MODEL
    nb_ctx, t_ctx, _ = x_prompt.shape
    nb_lat, t_lat, _ = x_sample.shape

    pad = jnp.zeros((8 - 1 - nb_lat, d), F32)
    cond8 = jnp.concatenate([c_ctx[None, :], c, pad], axis=0)
    mod = _adaln(cond8, w_ada[l], b_ada[l][None, :])
    mods_ctx = [mod[0:1, i * d:(i + 1) * d][:, None, :] for i in range(6)]
    mods_lat = [mod[1:1 + nb_lat, i * d:(i + 1) * d][:, None, :] for i in range(6)]

    wgu, wdb = _expert_prep(exp_w_gate[l], exp_w_up[l], exp_w_down[l])
    zeros_lora = jnp.zeros((GLA_LORA, GLA_QK), F32)
    rw = router_w[l]
    rw_hi = rw.astype(BF16)
    rw_lo = (rw - rw_hi.astype(F32)).astype(BF16)
    w_in_b = w_in[l].astype(BF16)
    p = {
        "norm_attn_g": norm_attn_g[l][None, :],
        "norm_ffn_g": norm_ffn_g[l][None, :],
        "final_norm_g": final_norm_g[None, :],
        "w_gla": w_in_b[:, :2 * GLA_QK + 2 * GLA_V],
        "w_lora": w_in_b[:, 2 * GLA_QK + 2 * GLA_V:2 * GLA_QK + 2 * GLA_V + 2 * GLA_LORA],
        "w_swa": w_in_b[:, 2 * GLA_QK + 2 * GLA_V + 2 * GLA_LORA:],
        "waf": jnp.concatenate([gla_wa_f[l], zeros_lora], axis=0).astype(BF16),
        "wab": jnp.concatenate([zeros_lora, gla_wa_b[l]], axis=0).astype(BF16),
        "baf": gla_ba_f[l][None, :],
        "bab": gla_ba_b[l][None, :],
        "gla_norm_g": gla_norm_g[l][None, :],
        "w_out": w_out[l].astype(BF16),
        "rw_cat": jnp.concatenate([rw_hi, rw_lo], axis=1),
        "rw_hi": rw_hi,
        "rbias": router_bias[l][:, None],
        "wgu": wgu, "wd": wdb,
        "swg": sh_w_gate[l].astype(BF16), "swu": sh_w_up[l].astype(BF16),
        "swd": sh_w_down[l].astype(BF16),
    }
    sink = swa_sink[l]

    y_prompt, k_c, v_c, s_f, s_b = _stream(x_prompt, mods_ctx, p, functools.partial(_attn_ctx, sink))

    cos, sin_lo, sin_hi = _rope_tables(t_lat)
    kc = cache_swa_k[:, l].reshape(nb_lat, -1, SWA_KV)
    vc = cache_swa_v[:, l].reshape(nb_lat, -1, SWA_KV)
    lat_attn = lambda q, k, v: _attn_lat(sink, q, k, v, kc, vc, cos, sin_lo, sin_hi)
    s0 = (state_gla_fwd[:, l].reshape(nb_lat, GLA_QK, GLA_DV),
          state_gla_bwd[:, l].reshape(nb_lat, GLA_QK, GLA_DV))
    y_sample, _, _, _, _ = _stream(x_sample, mods_lat, p, lat_attn, s0)

    new_k = k_c.reshape(nb_ctx, 1, t_ctx, SWA_KV_HEADS, SWA_HEAD_DIM)
    new_v = v_c.reshape(nb_ctx, 1, t_ctx, SWA_KV_HEADS, SWA_HEAD_DIM)
    new_sf = s_f.reshape(nb_ctx, 1, GLA_HEADS, GLA_DK, GLA_DV)
    new_sb = s_b.reshape(nb_ctx, 1, GLA_HEADS, GLA_DK, GLA_DV)
    return (y_prompt, y_sample, new_k, new_v, new_sf, new_sb)
```

```python
import functools

import jax
import jax.numpy as jnp
from jax import lax
from jax.experimental import pallas as pl
from jax.experimental.pallas import tpu as pltpu

F32 = jnp.float32
BF16 = jnp.bfloat16

D_MODEL = 1024
GLA_HEADS = 4
GLA_DK = 64
GLA_DV = 128
GLA_LORA = 16
GLA_GATE_NORM = 16.0
GLA_CHUNK = 64
GLA_QK = GLA_HEADS * GLA_DK
GLA_V = GLA_HEADS * GLA_DV
SWA_HEAD_DIM = 64
SWA_HEADS = 8
SWA_KV_HEADS = 2
SWA_Q = SWA_HEADS * SWA_HEAD_DIM
SWA_KV = SWA_KV_HEADS * SWA_HEAD_DIM
ATTN_BLOCK = 128
GRID_W = 64
ROPE_BASE = 10000.0
N_EXPERTS = 64
TOP_K = 8
N_EXPERT_GROUPS = 8
TOPK_GROUPS = 4
EXPERT_FF = 128
SHARED_FF = 256
ROUTED_SCALE = 2.5
EPS = 1e-6

LANES = 128
VMEM_LIMIT = 56 * 1024 * 1024

NEG_INF = float("-inf")


def _dot(a, b):
    return jnp.dot(a, b, preferred_element_type=F32)


def _dot_nt(a, b):
    return lax.dot_general(a, b, (((1,), (1,)), ((), ())), preferred_element_type=F32)


def _split_hi_lo(x):
    hi = x.astype(BF16)
    lo = (x - hi.astype(F32)).astype(BF16)
    return hi, lo


def _sigmoid(x):
    return 1.0 / (1.0 + jnp.exp(-x))


def _silu(x):
    return x * _sigmoid(x)


def _rms_norm(x, g):
    ms = jnp.mean(x * x, axis=-1, keepdims=True)
    return x * lax.rsqrt(ms + EPS) * g


def _adaln_kernel(c_ref, w_ref, b_ref, o_ref):
    a_hi, a_lo = _split_hi_lo(_silu(c_ref[...]))
    w_hi, w_lo = _split_hi_lo(w_ref[...])
    o_ref[...] = _dot(a_hi, w_hi) + _dot(a_lo, w_hi) + _dot(a_hi, w_lo) + b_ref[...]


def _adaln(cond8, w_ada, b_ada):
    n = w_ada.shape[1]
    tn = 1536
    return pl.pallas_call(
        _adaln_kernel,
        out_shape=jax.ShapeDtypeStruct((8, n), F32),
        grid=(n // tn,),
        in_specs=[pl.BlockSpec((8, D_MODEL), lambda j: (0, 0)),
                  pl.BlockSpec((D_MODEL, tn), lambda j: (0, j)),
                  pl.BlockSpec((1, tn), lambda j: (0, j))],
        out_specs=pl.BlockSpec((8, tn), lambda j: (0, j)),
        compiler_params=pltpu.CompilerParams(dimension_semantics=("arbitrary",),
                                             vmem_limit_bytes=VMEM_LIMIT),
        name="adaln",
    )(cond8, w_ada, b_ada)


def _inproj_kernel(x_ref, g_ref, sh_ref, sc_ref, wg_ref, wl_ref, ws_ref,
                   gla_ref, lora_ref, q_ref, k_ref, v_ref):
    h = _rms_norm(x_ref[...], g_ref[...]) * (1.0 + sc_ref[...]) + sh_ref[...]
    hb = h.astype(BF16)
    gla_ref[...] = _dot(hb, wg_ref[...])
    lora_ref[...] = _dot(hb, wl_ref[...])
    s = _dot(hb, ws_ref[...])
    q_ref[...] = s[:, :SWA_Q]
    k_ref[...] = s[:, SWA_Q:SWA_Q + SWA_KV]
    v_ref[...] = s[:, SWA_Q + SWA_KV:]


def _inproj(x, g, sh, sc, w_gla, w_lora, w_swa, *, tm):
    b, t, d = x.shape
    nmod = sh.shape[0]
    mod_map = (lambda i, j: (i, 0, 0)) if nmod > 1 else (lambda i, j: (0, 0, 0))
    row = lambda i, j: (i, j, 0)
    full = lambda i, j: (0, 0)
    n_gla = w_gla.shape[1]
    n_lora = w_lora.shape[1]
    return pl.pallas_call(
        _inproj_kernel,
        out_shape=(jax.ShapeDtypeStruct((b, t, n_gla), F32),
                   jax.ShapeDtypeStruct((b, t, n_lora), F32),
                   jax.ShapeDtypeStruct((b, t, SWA_Q), F32),
                   jax.ShapeDtypeStruct((b, t, SWA_KV), F32),
                   jax.ShapeDtypeStruct((b, t, SWA_KV), F32)),
        grid=(b, t // tm),
        in_specs=[pl.BlockSpec((None, tm, d), row),
                  pl.BlockSpec((1, d), full),
                  pl.BlockSpec((None, 1, d), mod_map),
                  pl.BlockSpec((None, 1, d), mod_map),
                  pl.BlockSpec((d, n_gla), full),
                  pl.BlockSpec((d, n_lora), full),
                  pl.BlockSpec((d, w_swa.shape[1]), full)],
        out_specs=(pl.BlockSpec((None, tm, n_gla), row),
                   pl.BlockSpec((None, tm, n_lora), row),
                   pl.BlockSpec((None, tm, SWA_Q), row),
                   pl.BlockSpec((None, tm, SWA_KV), row),
                   pl.BlockSpec((None, tm, SWA_KV), row)),
        compiler_params=pltpu.CompilerParams(dimension_semantics=("arbitrary", "arbitrary"),
                                             vmem_limit_bytes=VMEM_LIMIT),
        name="inproj",
    )(x, g, sh, sc, w_gla, w_lora, w_swa)


SCAN_UNROLL = 2
OUT_UNROLL = 4


def _log_sigmoid(x):
    return jnp.minimum(x, 0.0) - jnp.log(1.0 + jnp.exp(-jnp.abs(x)))


def _heads_to_rows(x):
    return jnp.concatenate([x[:, h * LANES:(h + 1) * LANES] for h in range(GLA_HEADS)], axis=0)


def _rows_to_heads(x, c):
    return jnp.concatenate([x[h * c:(h + 1) * c, :] for h in range(GLA_HEADS)], axis=1)


def _gla_kernel(has_init, q_ref, k_ref, v_ref, g_ref, lora_ref, waf_ref, baf_ref, wab_ref, bab_ref,
                ng_ref, *rest):
    if has_init:
        s0f_ref, s0b_ref, *rest = rest
    (out_ref, sf_ref, sb_ref, laf_ref, lab_ref, oacc_ref, qtf_ref, qtb_ref, saf_ref, sab_ref,
     stf_ref, stb_ref) = rest
    t = q_ref.shape[0]
    c = GLA_CHUNK
    n = t // c
    hc = GLA_HEADS * c

    lora = lora_ref[...].astype(BF16)
    laf_ref[...] = _log_sigmoid(_dot(lora, waf_ref[...]) + baf_ref[...]) * (1.0 / GLA_GATE_NORM)
    lab_ref[...] = _log_sigmoid(_dot(lora, wab_ref[...]) + bab_ref[...]) * (1.0 / GLA_GATE_NORM)

    if has_init:
        stf_ref[...] = s0f_ref[...].T
        stb_ref[...] = s0b_ref[...].T
    else:
        stf_ref[...] = jnp.zeros_like(stf_ref)
        stb_ref[...] = jnp.zeros_like(stb_ref)
    oacc_ref[...] = jnp.zeros_like(oacc_ref)

    r64 = lax.broadcasted_iota(jnp.int32, (c, c), 0)
    c64 = lax.broadcasted_iota(jnp.int32, (c, c), 1)
    tri_f = jnp.where(c64 <= r64, 1.0, 0.0).astype(BF16)
    tri_b = jnp.where(c64 >= r64, 1.0, 0.0).astype(BF16)
    rr = lax.broadcasted_iota(jnp.int32, (hc, hc), 0)
    cc = lax.broadcasted_iota(jnp.int32, (hc, hc), 1)
    same_head = (rr >> 6) == (cc >> 6)
    keep_f = same_head & ((rr & (c - 1)) >= (cc & (c - 1)))
    keep_b = same_head & ((rr & (c - 1)) <= (cc & (c - 1)))
    head_mask = jnp.where(same_head, 1.0, 0.0).astype(BF16)
    norm_g = ng_ref[...]

    def chunk_rows(ci):
        return pl.ds(pl.multiple_of(ci * c, c), c)

    def tile_heads(x):
        x4 = jnp.concatenate([x] * GLA_HEADS, axis=0)
        return jnp.where(same_head, x4, 0.0).astype(BF16)

    def scan_step(i, carry):
        dirs = []
        for u in range(SCAN_UNROLL):
            dirs += [(SCAN_UNROLL * i + u, laf_ref, tri_f, keep_f, c - 1, stf_ref, saf_ref, qtf_ref),
                     (n - 1 - SCAN_UNROLL * i - u, lab_ref, tri_b, keep_b, 0, stb_ref, sab_ref, qtb_ref)]
        cums = []
        for ci, la_ref, tri, _, _, _, _, _ in dirs:
            la_hi, la_lo = _split_hi_lo(la_ref[chunk_rows(ci), :])
            cums.append(_dot(tri, la_hi) + _dot(tri, la_lo))
        ops = []
        for (ci, _, _, _, last_row, _, _, qt_ref), cum in zip(dirs, cums):
            sl = chunk_rows(ci)
            tot = cum[last_row:last_row + 1, :]
            kc = k_ref[sl, :]
            qt = q_ref[sl, :] * (GLA_DK ** -0.5) * jnp.exp(cum)
            qt_ref[sl, :] = qt.astype(BF16)
            v_rows = _heads_to_rows(v_ref[sl, :])
            ops.append((tot, tile_heads(qt), tile_heads(kc * jnp.exp(-cum)),
                        tile_heads(kc * jnp.exp(tot - cum)), v_rows))
        atts = [_dot_nt(q4, k4) for _, q4, k4, _, _ in ops]
        incs = []
        for (_, _, _, keep, _, _, _, _), (_, _, _, kd4, v_rows), att in zip(dirs, ops, atts):
            att = jnp.where(keep, att, 0.0).astype(BF16)
            incs.append((_dot(att, v_rows.astype(BF16)), _dot(v_rows.T.astype(BF16), kd4)))
        for (ci, _, _, _, _, st_ref, snap_ref, _), (tot, _, _, _, _), (o_intra, st_inc) in zip(dirs, ops, incs):
            oacc_ref[ci] += o_intra
            st = st_ref[...]
            snap_ref[ci] = st.astype(BF16)
            st_ref[...] = jnp.exp(tot) * st + st_inc
        return carry

    def tile_heads_bf16(x):
        return jnp.concatenate([x] * GLA_HEADS, axis=0) * head_mask

    def out_step(i, carry):
        chunks = [OUT_UNROLL * i + u for u in range(OUT_UNROLL)]
        inter = []
        for ci in chunks:
            sl = chunk_rows(ci)
            q4 = jnp.concatenate([tile_heads_bf16(qtf_ref[sl, :]), tile_heads_bf16(qtb_ref[sl, :])], axis=1)
            st = jnp.concatenate([saf_ref[ci], sab_ref[ci]], axis=1)
            inter.append(_dot_nt(q4, st))
        for ci, o_inter in zip(chunks, inter):
            sl = chunk_rows(ci)
            on = _rms_norm(oacc_ref[ci] + o_inter, norm_g)
            gate = _silu(_heads_to_rows(g_ref[sl, :]))
            out_ref[sl, :] = _rows_to_heads(on * gate, c)
        return carry

    lax.fori_loop(0, n // SCAN_UNROLL, scan_step, 0)
    lax.fori_loop(0, n // OUT_UNROLL, out_step, 0)
    sf_ref[...] = stf_ref[...].T
    sb_ref[...] = stb_ref[...].T


def _gla(gla_in, lora, waf, baf, wab, bab, norm_g, s0f=None, s0b=None):
    b, t, _ = gla_in.shape
    has_init = s0f is not None
    n = t // GLA_CHUNK
    bmap = lambda i: (i, 0, 0)
    full = lambda i: (0, 0)
    in_specs = [pl.BlockSpec((None, t, GLA_QK), lambda i: (i, 0, 0)),
                pl.BlockSpec((None, t, GLA_QK), lambda i: (i, 0, 1)),
                pl.BlockSpec((None, t, GLA_V), lambda i: (i, 0, 1)),
                pl.BlockSpec((None, t, GLA_V), lambda i: (i, 0, 2)),
                pl.BlockSpec((None, t, 2 * GLA_LORA), bmap),
                pl.BlockSpec((2 * GLA_LORA, GLA_QK), full),
                pl.BlockSpec((1, GLA_QK), full),
                pl.BlockSpec((2 * GLA_LORA, GLA_QK), full),
                pl.BlockSpec((1, GLA_QK), full),
                pl.BlockSpec((1, GLA_DV), full)]
    args = [gla_in, gla_in, gla_in, gla_in, lora, waf, baf, wab, bab, norm_g]
    if has_init:
        in_specs += [pl.BlockSpec((None, GLA_QK, GLA_DV), bmap)] * 2
        args += [s0f, s0b]
    return pl.pallas_call(
        functools.partial(_gla_kernel, has_init),
        out_shape=(jax.ShapeDtypeStruct((b, t, GLA_V), F32),
                   jax.ShapeDtypeStruct((b, GLA_QK, GLA_DV), F32),
                   jax.ShapeDtypeStruct((b, GLA_QK, GLA_DV), F32)),
        grid=(b,),
        in_specs=in_specs,
        out_specs=(pl.BlockSpec((None, t, GLA_V), bmap),
                   pl.BlockSpec((None, GLA_QK, GLA_DV), bmap),
                   pl.BlockSpec((None, GLA_QK, GLA_DV), bmap)),
        scratch_shapes=[pltpu.VMEM((t, GLA_QK), F32),
                        pltpu.VMEM((t, GLA_QK), F32),
                        pltpu.VMEM((n, GLA_HEADS * GLA_CHUNK, GLA_DV), F32),
                        pltpu.VMEM((t, GLA_QK), BF16),
                        pltpu.VMEM((t, GLA_QK), BF16),
                        pltpu.VMEM((n, GLA_DV, GLA_QK), BF16),
                        pltpu.VMEM((n, GLA_DV, GLA_QK), BF16),
                        pltpu.VMEM((GLA_DV, GLA_QK), F32),
                        pltpu.VMEM((GLA_DV, GLA_QK), F32)],
        compiler_params=pltpu.CompilerParams(dimension_semantics=("arbitrary",),
                                             vmem_limit_bytes=VMEM_LIMIT),
        name="gla",
    )(*args)


def _dup_groups(x):
    lo = lax.broadcasted_iota(jnp.int32, x.shape, 1) < SWA_HEAD_DIM
    xr = pltpu.roll(x, SWA_HEAD_DIM, axis=1)
    return jnp.where(lo, x, xr), jnp.where(lo, xr, x)


def _pairs_attention(qps, sinks, k_dups, vt_dups, mask):
    nq = qps[0].shape[0]
    lo = lax.broadcasted_iota(jnp.int32, (nq, LANES), 1) < SWA_HEAD_DIM
    even = lax.broadcasted_iota(jnp.int32, (1, 2 * nq), 1) < nq
    scores = []
    for qp, k_dup in zip(qps, k_dups):
        q2 = jnp.concatenate([jnp.where(lo, qp, 0.0), jnp.where(lo, 0.0, qp)], axis=0).astype(BF16)
        scores.append(_dot_nt(k_dup, q2))
    probs = []
    for s, (sink_even, sink_odd) in zip(scores, sinks):
        if mask is not None:
            s = jnp.where(mask, s, NEG_INF)
        sink = jnp.where(even, sink_even, sink_odd)
        m = jnp.maximum(jnp.max(s, axis=0, keepdims=True), sink)
        p = jnp.exp(s - m)
        denom = jnp.sum(p, axis=0, keepdims=True) + jnp.exp(sink - m)
        probs.append((p.astype(BF16), 1.0 / denom))
    outs = []
    for (p, rdenom), vt_dup in zip(probs, vt_dups):
        o = _dot(vt_dup, p) * rdenom
        outs.append(jnp.concatenate([o[:SWA_HEAD_DIM, :nq], o[SWA_HEAD_DIM:, nq:]], axis=0).T)
    return outs


def _attn_ctx_kernel(sink_ref, q_ref, k_ref, v_ref, o_ref):
    kd = [x.astype(BF16) for x in _dup_groups(k_ref[...])]
    vt = [x.T.astype(BF16) for x in _dup_groups(v_ref[...])]
    scale = SWA_HEAD_DIM ** -0.5
    pairs = range(SWA_HEADS // 2)
    outs = _pairs_attention([q_ref[:, pr * LANES:(pr + 1) * LANES] * scale for pr in pairs],
                            [(sink_ref[2 * pr], sink_ref[2 * pr + 1]) for pr in pairs],
                            [kd[pr // 2] for pr in pairs], [vt[pr // 2] for pr in pairs], None)
    for pr in pairs:
        o_ref[:, pr * LANES:(pr + 1) * LANES] = outs[pr]


def _attn_ctx(sink, q, k, v):
    b, t, _ = q.shape
    bmap = lambda i: (i, 0, 0)
    return pl.pallas_call(
        _attn_ctx_kernel,
        out_shape=jax.ShapeDtypeStruct((b, t, SWA_Q), F32),
        grid=(b,),
        in_specs=[pl.BlockSpec(memory_space=pltpu.SMEM),
                  pl.BlockSpec((None, t, SWA_Q), bmap),
                  pl.BlockSpec((None, t, SWA_KV), bmap),
                  pl.BlockSpec((None, t, SWA_KV), bmap)],
        out_specs=pl.BlockSpec((None, t, SWA_Q), bmap),
        compiler_params=pltpu.CompilerParams(dimension_semantics=("arbitrary",),
                                             vmem_limit_bytes=VMEM_LIMIT),
        name="attn_ctx",
    )(sink, q, k, v)


def _rope(x, cos, sin_lo, sin_hi):
    return x * cos + pltpu.roll(x, LANES - 16, axis=1) * sin_lo + pltpu.roll(x, 16, axis=1) * sin_hi


def _attn_lat_kernel(sink_ref, q_ref, k_ref, v_ref, kc_ref, vc_ref, cos_ref, sl_ref, sh_ref,
                     o_ref, kw_ref, vw_ref):
    t = q_ref.shape[0]
    ab = ATTN_BLOCK
    nb = t // ab
    scale = SWA_HEAD_DIM ** -0.5

    k_rot = _dup_groups(_rope(k_ref[...], cos_ref[...], sl_ref[...], sh_ref[...]))
    v_dup = _dup_groups(v_ref[...])
    zeros = jnp.zeros((ab, LANES), BF16)
    for grp in range(SWA_KV_HEADS):
        kw_ref[grp, 0:ab, :] = zeros
        kw_ref[grp, ab:ab + t, :] = k_rot[grp].astype(BF16)
        kw_ref[grp, ab + t:, :] = zeros
        vw_ref[grp, 0] = zeros
        for blk in range(nb):
            vw_ref[grp, blk + 1] = v_dup[grp][blk * ab:(blk + 1) * ab, :].T.astype(BF16)
        vw_ref[grp, nb + 1] = zeros
    kc = [x.astype(BF16) for x in _dup_groups(kc_ref[...])]
    vct = [x.T.astype(BF16) for x in _dup_groups(vc_ref[...])]
    lc = kc_ref.shape[0]

    key = lax.broadcasted_iota(jnp.int32, (lc + 3 * ab, 2 * ab), 0) - lc
    tq = lax.broadcasted_iota(jnp.int32, (lc + 3 * ab, 2 * ab), 1) & (ab - 1)
    band = (key < 0) | (jnp.abs(tq + ab - key) <= ab)

    def block(nq, carry):
        row0 = pl.multiple_of(nq * ab, ab)
        s_abs = key + (nq - 1) * ab
        mask = band & ((key < 0) | ((s_abs >= 0) & (s_abs < t)))
        cos = cos_ref[pl.ds(row0, ab), :]
        s_lo = sl_ref[pl.ds(row0, ab), :]
        s_hi = sh_ref[pl.ds(row0, ab), :]
        k_all = [jnp.concatenate([kc[grp], kw_ref[grp, pl.ds(row0, 3 * ab), :]], axis=0)
                 for grp in range(SWA_KV_HEADS)]
        vt_all = [jnp.concatenate([vct[grp], vw_ref[grp, nq], vw_ref[grp, nq + 1], vw_ref[grp, nq + 2]],
                                  axis=1) for grp in range(SWA_KV_HEADS)]
        pairs = range(SWA_HEADS // 2)
        qps = [_rope(q_ref[pl.ds(row0, ab), pr * LANES:(pr + 1) * LANES], cos, s_lo, s_hi) * scale
               for pr in pairs]
        outs = _pairs_attention(qps, [(sink_ref[2 * pr], sink_ref[2 * pr + 1]) for pr in pairs],
                                [k_all[pr // 2] for pr in pairs], [vt_all[pr // 2] for pr in pairs], mask)
        for pr in pairs:
            o_ref[pl.ds(row0, ab), pr * LANES:(pr + 1) * LANES] = outs[pr]
        return carry

    lax.fori_loop(0, nb, block, 0)


def _attn_lat(sink, q, k, v, kc, vc, cos, sin_lo, sin_hi):
    b, t, _ = q.shape
    lc = kc.shape[1]
    bmap = lambda i: (i, 0, 0)
    full = lambda i: (0, 0)
    return pl.pallas_call(
        _attn_lat_kernel,
        out_shape=jax.ShapeDtypeStruct((b, t, SWA_Q), F32),
        grid=(b,),
        in_specs=[pl.BlockSpec(memory_space=pltpu.SMEM),
                  pl.BlockSpec((None, t, SWA_Q), bmap),
                  pl.BlockSpec((None, t, SWA_KV), bmap),
                  pl.BlockSpec((None, t, SWA_KV), bmap),
                  pl.BlockSpec((None, lc, SWA_KV), bmap),
                  pl.BlockSpec((None, lc, SWA_KV), bmap),
                  pl.BlockSpec((t, LANES), full),
                  pl.BlockSpec((t, LANES), full),
                  pl.BlockSpec((t, LANES), full)],
        out_specs=pl.BlockSpec((None, t, SWA_Q), bmap),
        scratch_shapes=[pltpu.VMEM((SWA_KV_HEADS, t + 2 * ATTN_BLOCK, LANES), BF16),
                        pltpu.VMEM((SWA_KV_HEADS, t // ATTN_BLOCK + 2, LANES, ATTN_BLOCK), BF16)],
        compiler_params=pltpu.CompilerParams(dimension_semantics=("arbitrary",),
                                             vmem_limit_bytes=VMEM_LIMIT),
        name="attn_lat",
    )(sink, q, k, v, kc, vc, cos, sin_lo, sin_hi)


def _rope_tables(t):
    half = SWA_HEAD_DIM // 2
    quarter = half // 2
    pos = jnp.arange(t)
    row = (pos // GRID_W).astype(F32)
    col = (pos % GRID_W).astype(F32)
    inv_freq = ROPE_BASE ** (-jnp.arange(quarter, dtype=F32) / quarter)
    lane = jnp.arange(LANES)
    d = lane % SWA_HEAD_DIM
    freq = inv_freq[d % quarter]
    use_row = (d < half)
    ang = jnp.where(use_row[None, :], row[:, None], col[:, None]) * freq[None, :]
    cos = jnp.cos(ang)
    sin = jnp.sin(ang)
    lower = (d % half) < quarter
    return cos, jnp.where(lower[None, :], -sin, 0.0), jnp.where(lower[None, :], 0.0, sin)


def _route(sel, scores):
    n = sel.shape[1]
    gsz = N_EXPERTS // N_EXPERT_GROUPS

    def first_max(x, idx, size):
        m = jnp.max(x, axis=0, keepdims=True)
        first = jnp.min(jnp.where(x == m, idx, float(size)), axis=0, keepdims=True)
        return m, idx == first

    i8 = lax.broadcasted_iota(jnp.int32, (gsz, n), 0).astype(F32)
    rows = []
    for g in range(N_EXPERT_GROUPS):
        slab = sel[g * gsz:(g + 1) * gsz, :]
        m1, hit = first_max(slab, i8, gsz)
        m2 = jnp.max(jnp.where(hit, NEG_INF, slab), axis=0, keepdims=True)
        rows.append(m1 + m2)
    gscore = jnp.concatenate(rows, axis=0)
    gsel = jnp.zeros((N_EXPERT_GROUPS, n), F32)
    for _ in range(TOPK_GROUPS):
        _, hit = first_max(gscore, i8, N_EXPERT_GROUPS)
        gsel = jnp.where(hit, 1.0, gsel)
        gscore = jnp.where(hit, NEG_INF, gscore)
    emask = jnp.concatenate(
        [jnp.broadcast_to(gsel[g:g + 1, :], (gsz, n)) for g in range(N_EXPERT_GROUPS)], axis=0)
    cand = jnp.where(emask > 0.5, sel, NEG_INF)
    ie = lax.broadcasted_iota(jnp.int32, (N_EXPERTS, n), 0).astype(F32)
    w = jnp.zeros((N_EXPERTS, n), F32)
    chosen = jnp.zeros((N_EXPERTS, n), F32)
    hits = []
    for _ in range(TOP_K):
        _, hit = first_max(cand, ie, N_EXPERTS)
        hits.append(hit)
        w = jnp.where(hit, scores, w)
        chosen = jnp.where(hit, 1.0, chosen)
        cand = jnp.where(hit, NEG_INF, cand)
    gates = w / jnp.sum(w, axis=0, keepdims=True) * ROUTED_SCALE

    s_idx = lax.broadcasted_iota(jnp.int32, (n, n), 0)
    t_idx = lax.broadcasted_iota(jnp.int32, (n, n), 1)
    before = jnp.where(s_idx < t_idx, 1.0, 0.0).astype(BF16)
    rank = _dot(chosen.astype(BF16), before)
    count = jnp.sum(chosen, axis=1, keepdims=True)
    padded = jnp.floor((count + (SORT_ALIGN - 1)) * (1.0 / SORT_ALIGN)) * SORT_ALIGN
    padded = jnp.broadcast_to(padded, (N_EXPERTS, LANES))
    e_row = lax.broadcasted_iota(jnp.int32, (N_EXPERTS, N_EXPERTS), 0)
    e_col = lax.broadcasted_iota(jnp.int32, (N_EXPERTS, N_EXPERTS), 1)
    below = jnp.where(e_col < e_row, 1.0, 0.0).astype(BF16)
    start = _dot(below, padded.astype(BF16))
    row = start[:, 0:1] + rank
    pos = jnp.concatenate([jnp.sum(jnp.where(h, row, 0.0), axis=0, keepdims=True) for h in hits], axis=0)
    wts = jnp.concatenate([jnp.sum(jnp.where(h, gates, 0.0), axis=0, keepdims=True) for h in hits], axis=0)
    return pos, wts, padded, start


def _outproj_kernel(gla_ref, att_ref, x_ref, wo_ref, g1_ref, sh_ref, sc_ref, ng_ref, rw_ref, rwh_ref,
                    rb_ref, x1_ref, xm_ref, pos_ref, wts_ref, cnt_ref, start_ref):
    y = (_dot(gla_ref[...].astype(BF16), wo_ref[0:GLA_V, :])
         + _dot(att_ref[...].astype(BF16), wo_ref[GLA_V:, :]))
    x1 = x_ref[...] + g1_ref[...] * y
    x1_ref[...] = x1
    xm = _rms_norm(x1, ng_ref[...]) * (1.0 + sc_ref[...]) + sh_ref[...]
    xm_hi, xm_lo = _split_hi_lo(xm)
    xm_ref[...] = xm_hi
    lg = _dot(xm_hi, rw_ref[...])
    logits = lg[:, :N_EXPERTS] + lg[:, N_EXPERTS:] + _dot(xm_lo, rwh_ref[...])
    tm = logits.shape[0]
    lt = jnp.concatenate([logits, jnp.zeros((tm, LANES - N_EXPERTS), F32)], axis=1).T[:N_EXPERTS, :]
    scores = _sigmoid(lt)
    pos_ref[...], wts_ref[...], cnt_ref[...], start_ref[...] = _route(scores + rb_ref[...], scores)


def _outproj(gla_out, att_out, x, w_out, g1, sh2, sc2, norm_g, rw_cat, rw_hi, rbias, *, tm):
    b, t, d = x.shape
    nmod = g1.shape[0]
    mod_map = (lambda i, j: (i, 0, 0)) if nmod > 1 else (lambda i, j: (0, 0, 0))
    row = lambda i, j: (i, j, 0)
    full = lambda i, j: (0, 0)
    tile = lambda i, j: (i, j, 0, 0)
    nt = t // tm
    return pl.pallas_call(
        _outproj_kernel,
        out_shape=(jax.ShapeDtypeStruct((b, t, d), F32),
                   jax.ShapeDtypeStruct((b, t, d), BF16),
                   jax.ShapeDtypeStruct((b, nt, TOP_K, tm), F32),
                   jax.ShapeDtypeStruct((b, nt, TOP_K, tm), F32),
                   jax.ShapeDtypeStruct((b, nt, N_EXPERTS, LANES), F32),
                   jax.ShapeDtypeStruct((b, nt, N_EXPERTS, LANES), F32)),
        grid=(b, t // tm),
        in_specs=[pl.BlockSpec((None, tm, GLA_V), row),
                  pl.BlockSpec((None, tm, SWA_Q), row),
                  pl.BlockSpec((None, tm, d), row),
                  pl.BlockSpec((d, d), full),
                  pl.BlockSpec((None, 1, d), mod_map),
                  pl.BlockSpec((None, 1, d), mod_map),
                  pl.BlockSpec((None, 1, d), mod_map),
                  pl.BlockSpec((1, d), full),
                  pl.BlockSpec((d, 2 * N_EXPERTS), full),
                  pl.BlockSpec((d, N_EXPERTS), full),
                  pl.BlockSpec((N_EXPERTS, 1), full)],
        out_specs=(pl.BlockSpec((None, tm, d), row),
                   pl.BlockSpec((None, tm, d), row),
                   pl.BlockSpec((None, None, TOP_K, tm), tile),
                   pl.BlockSpec((None, None, TOP_K, tm), tile),
                   pl.BlockSpec((None, None, N_EXPERTS, LANES), tile),
                   pl.BlockSpec((None, None, N_EXPERTS, LANES), tile)),
        compiler_params=pltpu.CompilerParams(dimension_semantics=("arbitrary", "arbitrary"),
                                             vmem_limit_bytes=VMEM_LIMIT),
        name="outproj",
    )(gla_out, att_out, x, w_out, g1, sh2, sc2, norm_g, rw_cat, rw_hi, rbias)


MOE_TILE = 256
SORT_ALIGN = 16
SORT_ROWS = 3072
ROW_TILE = 512
PREP_EB = 4


def _expert_prep_kernel(wg_ref, wu_ref, wd_ref, wgu_ref, wdb_ref):
    wgu_ref[...] = jnp.concatenate([wg_ref[...].astype(BF16), wu_ref[...].astype(BF16)], axis=2)
    wdb_ref[...] = wd_ref[...].astype(BF16)


def _expert_prep(wg, wu, wd):
    ne, d, f = wg.shape
    blk = lambda e: (e, 0, 0)
    return pl.pallas_call(
        _expert_prep_kernel,
        out_shape=(jax.ShapeDtypeStruct((ne, d, 2 * f), BF16),
                   jax.ShapeDtypeStruct((ne, f, d), BF16)),
        grid=(ne // PREP_EB,),
        in_specs=[pl.BlockSpec((PREP_EB, d, f), blk),
                  pl.BlockSpec((PREP_EB, d, f), blk),
                  pl.BlockSpec((PREP_EB, f, d), blk)],
        out_specs=(pl.BlockSpec((PREP_EB, d, 2 * f), blk),
                   pl.BlockSpec((PREP_EB, f, d), blk)),
        compiler_params=pltpu.CompilerParams(dimension_semantics=("arbitrary",),
                                             vmem_limit_bytes=VMEM_LIMIT),
        name="expert_prep",
    )(wg, wu, wd)


def _moe_sort_kernel(xm_ref, pos_ref, *rest):
    xs_ref = rest[-1]
    x = xm_ref[...]
    pos = pos_ref[...]
    tm = x.shape[0]
    for blk in range(SORT_ROWS // tm):
        rows = (lax.broadcasted_iota(jnp.int32, (tm, tm), 0) + blk * tm).astype(F32)
        onehot = jnp.zeros((tm, tm), F32)
        for k in range(TOP_K):
            onehot = jnp.where(rows == pos[k:k + 1, :], 1.0, onehot)
        out = _dot(onehot.astype(BF16), x).astype(BF16)
        g0 = blk * tm // SORT_ALIGN
        xs_ref[g0:g0 + tm // SORT_ALIGN] = out.reshape(tm // SORT_ALIGN, SORT_ALIGN, D_MODEL)


def _moe_sort(xm, pos, tile0, n_tiles_total, xs=None):
    n, d = xm.shape
    tm = pos.shape[-1]
    nt = n // tm
    gpt = SORT_ROWS // SORT_ALIGN
    in_specs = [pl.BlockSpec((tm, d), lambda i: (i, 0)),
                pl.BlockSpec((None, TOP_K, tm), lambda i: (i, 0, 0))]
    args = [xm, pos.reshape(nt, TOP_K, tm)]
    aliases = {}
    if xs is not None:
        in_specs.append(pl.BlockSpec(memory_space=pl.ANY))
        args.append(xs)
        aliases = {2: 0}
    return pl.pallas_call(
        _moe_sort_kernel,
        out_shape=jax.ShapeDtypeStruct((n_tiles_total * gpt, SORT_ALIGN, d), BF16),
        grid=(nt,),
        in_specs=in_specs,
        out_specs=pl.BlockSpec((gpt, SORT_ALIGN, d), lambda i: (tile0 + i, 0, 0)),
        input_output_aliases=aliases,
        compiler_params=pltpu.CompilerParams(dimension_semantics=("arbitrary",),
                                             vmem_limit_bytes=VMEM_LIMIT),
        name="moe_sort",
    )(*args)


def _moe_row_tiles(n_tokens):
    rows = n_tokens * TOP_K + (n_tokens // MOE_TILE) * N_EXPERTS * (SORT_ALIGN - 1) + N_EXPERTS * (ROW_TILE - 1)
    return -(-rows // ROW_TILE)


def _moe_plan(cnt, start):
    nt, ne = cnt.shape
    row_tiles = _moe_row_tiles(nt * MOE_TILE)
    gpt = SORT_ROWS // SORT_ALIGN
    gpr = ROW_TILE // SORT_ALIGN
    gc = cnt // SORT_ALIGN
    tot = jnp.sum(gc, axis=0)
    ptot = (tot + gpr - 1) // gpr * gpr
    gend = jnp.cumsum(ptot)
    gstart = gend - ptot
    n_used = gend[-1] // gpr
    cum_t = jnp.cumsum(gc, axis=0)
    cum_excl = cum_t - gc
    lstart = start // SORT_ALIGN
    r = jnp.minimum(jnp.arange(row_tiles, dtype=jnp.int32), n_used - 1)
    tile_expert = jnp.sum((gend[None, :] <= (r * gpr)[:, None]).astype(jnp.int32), axis=1)
    g = jnp.arange(row_tiles * gpr, dtype=jnp.int32)
    eg = tile_expert[g // gpr]
    u = g - gstart[eg]
    t_of = jnp.sum((cum_t[:, eg] <= u[None, :]).astype(jnp.int32), axis=0)
    valid = u < tot[eg]
    tc = jnp.minimum(t_of, nt - 1)
    src = tc * gpt + lstart[tc, eg] + (u - cum_excl[tc, eg])
    src = jnp.where(valid, src, gpt - 1)
    lg = jnp.arange(gpt, dtype=jnp.int32)
    lend = lstart + gc
    e_of = jnp.sum((lend[:, :, None] <= lg[None, None, :]).astype(jnp.int32), axis=1)
    used = e_of < ne
    ec = jnp.minimum(e_of, ne - 1)
    tt = jnp.arange(nt, dtype=jnp.int32)[:, None]
    back = gstart[ec] + cum_excl[tt, ec] + (lg[None, :] - lstart[tt, ec])
    back = jnp.where(used, back, 0)
    return n_used.reshape(1).astype(jnp.int32), tile_expert, src, back


def _moe_experts_kernel(nu_ref, te_ref, src_ref, xs_hbm, wgu_ref, wd_ref, ys_ref, xbuf, sem):
    r = pl.program_id(0)
    n_used = nu_ref[0]
    gpr = ROW_TILE // SORT_ALIGN
    slot = lax.rem(r, 2)

    def gather(tile, to_slot):
        for j in range(gpr):
            pltpu.make_async_copy(xs_hbm.at[src_ref[tile * gpr + j]], xbuf.at[to_slot, j],
                                  sem.at[to_slot]).start()

    @pl.when(r == 0)
    def _():
        gather(0, 0)

    @pl.when(r + 1 < n_used)
    def _():
        gather(r + 1, 1 - slot)

    @pl.when(r < n_used)
    def _():
        for j in range(gpr):
            pltpu.make_async_copy(xs_hbm.at[0], xbuf.at[slot, j], sem.at[slot]).wait()
        x = xbuf[slot].reshape(ROW_TILE, D_MODEL)
        ab = _dot(x, wgu_ref[...])
        h = (_silu(ab[:, :EXPERT_FF]) * ab[:, EXPERT_FF:]).astype(BF16)
        y = _dot(h, wd_ref[...]).astype(BF16)
        ys_ref[...] = y.reshape(gpr, SORT_ALIGN, D_MODEL)


def _moe_experts(n_used, tile_expert, src, xs, wgu, wd):
    d = xs.shape[-1]
    gpr = ROW_TILE // SORT_ALIGN
    row_tiles = tile_expert.shape[0]
    out_map = lambda r, nu, te, sr: (jnp.minimum(r, nu[0] - 1), 0, 0)
    w_map = lambda r, nu, te, sr: (te[r], 0, 0)
    grid_spec = pltpu.PrefetchScalarGridSpec(
        num_scalar_prefetch=3,
        grid=(row_tiles,),
        in_specs=[pl.BlockSpec(memory_space=pl.ANY),
                  pl.BlockSpec((None, d, 2 * EXPERT_FF), w_map),
                  pl.BlockSpec((None, EXPERT_FF, d), w_map)],
        out_specs=pl.BlockSpec((gpr, SORT_ALIGN, d), out_map),
        scratch_shapes=[pltpu.VMEM((2, gpr, SORT_ALIGN, d), BF16),
                        pltpu.SemaphoreType.DMA((2,))])
    return pl.pallas_call(
        _moe_experts_kernel,
        out_shape=jax.ShapeDtypeStruct((row_tiles * gpr, SORT_ALIGN, d), BF16),
        grid_spec=grid_spec,
        compiler_params=pltpu.CompilerParams(dimension_semantics=("arbitrary",),
                                             vmem_limit_bytes=VMEM_LIMIT),
        name="moe_experts",
    )(n_used, tile_expert, src, xs, wgu, wd)


def _moe_combine_kernel(back_ref, ys_hbm, pos_ref, wts_ref, xm_ref, x1_ref, g2_ref, fg_ref,
                        swg_ref, swu_ref, swd_ref, o_ref, buf, sem):
    i = pl.program_id(0)
    gpt = SORT_ROWS // SORT_ALIGN
    slot = lax.rem(i, 2)

    def gather(tile, to_slot):
        for g in range(gpt):
            pltpu.make_async_copy(ys_hbm.at[back_ref[tile * gpt + g]], buf.at[to_slot, g],
                                  sem.at[to_slot]).start()

    @pl.when(i == 0)
    def _():
        gather(0, 0)

    @pl.when(i + 1 < pl.num_programs(0))
    def _():
        gather(i + 1, 1 - slot)

    x = xm_ref[...]
    tm = x.shape[0]
    shared = _dot((_silu(_dot(x, swg_ref[...])) * _dot(x, swu_ref[...])).astype(BF16), swd_ref[...])
    pad = jnp.zeros((LANES - TOP_K, tm), F32)
    pos_t = jnp.concatenate([pos_ref[...], pad], axis=0).T
    wts_t = jnp.concatenate([wts_ref[...], pad], axis=0).T
    rows = lax.broadcasted_iota(jnp.int32, (tm, SORT_ROWS), 1).astype(F32)
    comb = jnp.zeros((tm, SORT_ROWS), F32)
    for k in range(TOP_K):
        comb = jnp.where(rows == pos_t[:, k:k + 1], wts_t[:, k:k + 1], comb)
    for g in range(gpt):
        pltpu.make_async_copy(ys_hbm.at[0], buf.at[slot, g], sem.at[slot]).wait()
    routed = _dot(comb.astype(BF16), buf[slot].reshape(SORT_ROWS, D_MODEL))
    y = x1_ref[...] + g2_ref[...] * (routed + shared)
    o_ref[...] = _rms_norm(y, fg_ref[...])


def _moe_combine(back, ys, pos, wts, xm, x1, g2, final_g, swg, swu, swd, *, tiles_per_mod):
    n, d = xm.shape
    tm = pos.shape[-1]
    nt = n // tm
    gpt = SORT_ROWS // SORT_ALIGN
    row = lambda i, bk: (i, 0)
    full = lambda i, bk: (0, 0)
    tile = lambda i, bk: (i, 0, 0)
    mod_map = lambda i, bk: (i // tiles_per_mod, 0, 0)
    grid_spec = pltpu.PrefetchScalarGridSpec(
        num_scalar_prefetch=1,
        grid=(nt,),
        in_specs=[pl.BlockSpec(memory_space=pl.ANY),
                  pl.BlockSpec((None, TOP_K, tm), tile),
                  pl.BlockSpec((None, TOP_K, tm), tile),
                  pl.BlockSpec((tm, d), row),
                  pl.BlockSpec((tm, d), row),
                  pl.BlockSpec((None, 1, d), mod_map),
                  pl.BlockSpec((1, d), full),
                  pl.BlockSpec((d, SHARED_FF), full),
                  pl.BlockSpec((d, SHARED_FF), full),
                  pl.BlockSpec((SHARED_FF, d), full)],
        out_specs=pl.BlockSpec((tm, d), row),
        scratch_shapes=[pltpu.VMEM((2, gpt, SORT_ALIGN, d), BF16),
                        pltpu.SemaphoreType.DMA((2,))])
    return pl.pallas_call(
        _moe_combine_kernel,
        out_shape=jax.ShapeDtypeStruct((n, d), F32),
        grid_spec=grid_spec,
        compiler_params=pltpu.CompilerParams(dimension_semantics=("arbitrary",),
                                             vmem_limit_bytes=VMEM_LIMIT),
        name="moe_combine",
    )(back, ys, pos.reshape(nt, TOP_K, tm), wts.reshape(nt, TOP_K, tm), xm, x1, g2, final_g, swg, swu, swd)


def _mix(x, mods, p, attn_fn, s0=None):
    sh1, sc1, g1, sh2, sc2, _ = mods
    gla_in, lora, q_s, k_s, v_s = _inproj(x, p["norm_attn_g"], sh1, sc1, p["w_gla"], p["w_lora"],
                                          p["w_swa"], tm=256)
    if s0 is None:
        gla_out, s_f, s_b = _gla(gla_in, lora, p["waf"], p["baf"], p["wab"], p["bab"], p["gla_norm_g"])
    else:
        gla_out, s_f, s_b = _gla(gla_in, lora, p["waf"], p["baf"], p["wab"], p["bab"], p["gla_norm_g"],
                                 s0[0], s0[1])
    att_out = attn_fn(q_s, k_s, v_s)
    routed = _outproj(gla_out, att_out, x, p["w_out"], g1, sh2, sc2, p["norm_ffn_g"],
                      p["rw_cat"], p["rw_hi"], p["rbias"], tm=MOE_TILE)
    return routed, k_s, v_s, s_f, s_b


def _moe(streams, p):
    d = D_MODEL
    n_tiles = [r[1].shape[0] * r[1].shape[1] // MOE_TILE for r, _ in streams]
    total = sum(n_tiles)
    xs = None
    tile0 = 0
    for (x1, xm, pos, wts, cnt, start), _ in streams:
        xs = _moe_sort(xm.reshape(-1, d), pos, tile0, total, xs)
        tile0 += xm.shape[0] * xm.shape[1] // MOE_TILE
    cnt_all = jnp.concatenate([r[4][..., 0].reshape(-1, N_EXPERTS) for r, _ in streams], axis=0)
    start_all = jnp.concatenate([r[5][..., 0].reshape(-1, N_EXPERTS) for r, _ in streams], axis=0)
    n_used, tile_expert, src, back = _moe_plan(cnt_all.astype(jnp.int32), start_all.astype(jnp.int32))
    ys = _moe_experts(n_used, tile_expert, src, xs, p["wgu"], p["wd"])
    outs = []
    tile0 = 0
    for ((x1, xm, pos, wts, cnt, start), g2), nt in zip(streams, n_tiles):
        b, t, _ = x1.shape
        tiles_per_mod = (t // MOE_TILE) if g2.shape[0] > 1 else nt
        y = _moe_combine(back[tile0:tile0 + nt].reshape(-1), ys, pos, wts, xm.reshape(-1, d),
                         x1.reshape(-1, d), g2, p["final_norm_g"], p["swg"], p["swu"], p["swd"],
                         tiles_per_mod=tiles_per_mod)
        outs.append(y.reshape(b, t, d))
        tile0 += nt
    return outs


def kernel(x_prompt, x_sample, c, cache_swa_k, cache_swa_v, state_gla_fwd, state_gla_bwd, c_ctx, w_ada, b_ada, norm_attn_g, norm_ffn_g, w_in, gla_wa_f, gla_ba_f, gla_wa_b, gla_ba_b, gla_norm_g, swa_sink, w_out, router_w, router_bias, exp_w_gate, exp_w_up, exp_w_down, sh_w_gate, sh_w_up, sh_w_down, final_norm_g):
    l = 0
    d = D_MODEL
    nb_ctx, t_ctx, _ = x_prompt.shape
    nb_lat, t_lat, _ = x_sample.shape

    pad = jnp.zeros((8 - 1 - nb_lat, d), F32)
    cond8 = jnp.concatenate([c_ctx[None, :], c, pad], axis=0)
    mod = _adaln(cond8, w_ada[l], b_ada[l][None, :])
    mods_ctx = [mod[0:1, i * d:(i + 1) * d][:, None, :] for i in range(6)]
    mods_lat = [mod[1:1 + nb_lat, i * d:(i + 1) * d][:, None, :] for i in range(6)]

    wgu, wdb = _expert_prep(exp_w_gate[l], exp_w_up[l], exp_w_down[l])
    zeros_lora = jnp.zeros((GLA_LORA, GLA_QK), F32)
    rw = router_w[l]
    rw_hi = rw.astype(BF16)
    rw_lo = (rw - rw_hi.astype(F32)).astype(BF16)
    w_in_b = w_in[l].astype(BF16)
    p = {
        "norm_attn_g": norm_attn_g[l][None, :],
        "norm_ffn_g": norm_ffn_g[l][None, :],
        "final_norm_g": final_norm_g[None, :],
        "w_gla": w_in_b[:, :2 * GLA_QK + 2 * GLA_V],
        "w_lora": w_in_b[:, 2 * GLA_QK + 2 * GLA_V:2 * GLA_QK + 2 * GLA_V + 2 * GLA_LORA],
        "w_swa": w_in_b[:, 2 * GLA_QK + 2 * GLA_V + 2 * GLA_LORA:],
        "waf": jnp.concatenate([gla_wa_f[l], zeros_lora], axis=0).astype(BF16),
        "wab": jnp.concatenate([zeros_lora, gla_wa_b[l]], axis=0).astype(BF16),
        "baf": gla_ba_f[l][None, :],
        "bab": gla_ba_b[l][None, :],
        "gla_norm_g": gla_norm_g[l][None, :],
        "w_out": w_out[l].astype(BF16),
        "rw_cat": jnp.concatenate([rw_hi, rw_lo], axis=1),
        "rw_hi": rw_hi,
        "rbias": router_bias[l][:, None],
        "wgu": wgu, "wd": wdb,
        "swg": sh_w_gate[l].astype(BF16), "swu": sh_w_up[l].astype(BF16),
        "swd": sh_w_down[l].astype(BF16),
    }
    sink = swa_sink[l]

    routed_ctx, k_c, v_c, s_f, s_b = _mix(x_prompt, mods_ctx, p, functools.partial(_attn_ctx, sink))

    cos, sin_lo, sin_hi = _rope_tables(t_lat)
    kc = cache_swa_k[:, l].reshape(nb_lat, -1, SWA_KV)
    vc = cache_swa_v[:, l].reshape(nb_lat, -1, SWA_KV)
    lat_attn = lambda q, k, v: _attn_lat(sink, q, k, v, kc, vc, cos, sin_lo, sin_hi)
    s0 = (state_gla_fwd[:, l].reshape(nb_lat, GLA_QK, GLA_DV),
          state_gla_bwd[:, l].reshape(nb_lat, GLA_QK, GLA_DV))
    routed_lat, _, _, _, _ = _mix(x_sample, mods_lat, p, lat_attn, s0)
    y_prompt, y_sample = _moe([(routed_ctx, mods_ctx[5]), (routed_lat, mods_lat[5])], p)

    new_k = k_c.reshape(nb_ctx, 1, t_ctx, SWA_KV_HEADS, SWA_HEAD_DIM)
    new_v = v_c.reshape(nb_ctx, 1, t_ctx, SWA_KV_HEADS, SWA_HEAD_DIM)
    new_sf = s_f.reshape(nb_ctx, 1, GLA_HEADS, GLA_DK, GLA_DV)
    new_sb = s_b.reshape(nb_ctx, 1, GLA_HEADS, GLA_DK, GLA_DV)
    return (y_prompt, y_sample, new_k, new_v, new_sf, new_sb)
```

```python
import functools

import jax
import jax.numpy as jnp
from jax import lax
from jax.experimental import pallas as pl
from jax.experimental.pallas import tpu as pltpu

F32 = jnp.float32
BF16 = jnp.bfloat16

D_MODEL = 1024
GLA_HEADS = 4
GLA_DK = 64
GLA_DV = 128
GLA_LORA = 16
GLA_GATE_NORM = 16.0
GLA_CHUNK = 64
GLA_QK = GLA_HEADS * GLA_DK
GLA_V = GLA_HEADS * GLA_DV
SWA_HEAD_DIM = 64
SWA_HEADS = 8
SWA_KV_HEADS = 2
SWA_Q = SWA_HEADS * SWA_HEAD_DIM
SWA_KV = SWA_KV_HEADS * SWA_HEAD_DIM
ATTN_BLOCK = 128
GRID_W = 64
ROPE_BASE = 10000.0
N_EXPERTS = 64
TOP_K = 8
N_EXPERT_GROUPS = 8
TOPK_GROUPS = 4
EXPERT_FF = 128
SHARED_FF = 256
ROUTED_SCALE = 2.5
EPS = 1e-6

LANES = 128
VMEM_LIMIT = 56 * 1024 * 1024

NEG_INF = float("-inf")


def _dot(a, b):
    return jnp.dot(a, b, preferred_element_type=F32)


def _dot_nt(a, b):
    return lax.dot_general(a, b, (((1,), (1,)), ((), ())), preferred_element_type=F32)


def _split_hi_lo(x):
    hi = x.astype(BF16)
    lo = (x - hi.astype(F32)).astype(BF16)
    return hi, lo


def _sigmoid(x):
    return 1.0 / (1.0 + jnp.exp(-x))


def _silu(x):
    return x * _sigmoid(x)


def _rms_norm(x, g):
    ms = jnp.mean(x * x, axis=-1, keepdims=True)
    return x * lax.rsqrt(ms + EPS) * g


def _adaln_kernel(c_ref, w_ref, b_ref, o_ref):
    a_hi, a_lo = _split_hi_lo(_silu(c_ref[...]))
    w_hi, w_lo = _split_hi_lo(w_ref[...])
    o_ref[...] = _dot(a_hi, w_hi) + _dot(a_lo, w_hi) + _dot(a_hi, w_lo) + b_ref[...]


def _adaln(cond8, w_ada, b_ada):
    n = w_ada.shape[1]
    tn = 1536
    return pl.pallas_call(
        _adaln_kernel,
        out_shape=jax.ShapeDtypeStruct((8, n), F32),
        grid=(n // tn,),
        in_specs=[pl.BlockSpec((8, D_MODEL), lambda j: (0, 0)),
                  pl.BlockSpec((D_MODEL, tn), lambda j: (0, j)),
                  pl.BlockSpec((1, tn), lambda j: (0, j))],
        out_specs=pl.BlockSpec((8, tn), lambda j: (0, j)),
        compiler_params=pltpu.CompilerParams(dimension_semantics=("arbitrary",),
                                             vmem_limit_bytes=VMEM_LIMIT),
        name="adaln",
    )(cond8, w_ada, b_ada)


def _inproj_kernel(x_ref, g_ref, sh_ref, sc_ref, wg_ref, wl_ref, ws_ref,
                   gla_ref, lora_ref, q_ref, k_ref, v_ref):
    h = _rms_norm(x_ref[...], g_ref[...]) * (1.0 + sc_ref[...]) + sh_ref[...]
    hb = h.astype(BF16)
    gla_ref[...] = _dot(hb, wg_ref[...])
    lora_ref[...] = _dot(hb, wl_ref[...])
    s = _dot(hb, ws_ref[...])
    q_ref[...] = s[:, :SWA_Q]
    k_ref[...] = s[:, SWA_Q:SWA_Q + SWA_KV]
    v_ref[...] = s[:, SWA_Q + SWA_KV:]


def _inproj(x, g, sh, sc, w_gla, w_lora, w_swa, *, tm):
    b, t, d = x.shape
    nmod = sh.shape[0]
    mod_map = (lambda i, j: (i, 0, 0)) if nmod > 1 else (lambda i, j: (0, 0, 0))
    row = lambda i, j: (i, j, 0)
    full = lambda i, j: (0, 0)
    n_gla = w_gla.shape[1]
    n_lora = w_lora.shape[1]
    return pl.pallas_call(
        _inproj_kernel,
        out_shape=(jax.ShapeDtypeStruct((b, t, n_gla), F32),
                   jax.ShapeDtypeStruct((b, t, n_lora), F32),
                   jax.ShapeDtypeStruct((b, t, SWA_Q), F32),
                   jax.ShapeDtypeStruct((b, t, SWA_KV), F32),
                   jax.ShapeDtypeStruct((b, t, SWA_KV), F32)),
        grid=(b, t // tm),
        in_specs=[pl.BlockSpec((None, tm, d), row),
                  pl.BlockSpec((1, d), full),
                  pl.BlockSpec((None, 1, d), mod_map),
                  pl.BlockSpec((None, 1, d), mod_map),
                  pl.BlockSpec((d, n_gla), full),
                  pl.BlockSpec((d, n_lora), full),
                  pl.BlockSpec((d, w_swa.shape[1]), full)],
        out_specs=(pl.BlockSpec((None, tm, n_gla), row),
                   pl.BlockSpec((None, tm, n_lora), row),
                   pl.BlockSpec((None, tm, SWA_Q), row),
                   pl.BlockSpec((None, tm, SWA_KV), row),
                   pl.BlockSpec((None, tm, SWA_KV), row)),
        compiler_params=pltpu.CompilerParams(dimension_semantics=("arbitrary", "arbitrary"),
                                             vmem_limit_bytes=VMEM_LIMIT),
        name="inproj",
    )(x, g, sh, sc, w_gla, w_lora, w_swa)


SCAN_UNROLL = 2
OUT_UNROLL = 4


def _log_sigmoid(x):
    return jnp.minimum(x, 0.0) - jnp.log(1.0 + jnp.exp(-jnp.abs(x)))


def _heads_to_rows(x):
    return jnp.concatenate([x[:, h * LANES:(h + 1) * LANES] for h in range(GLA_HEADS)], axis=0)


def _rows_to_heads(x, c):
    return jnp.concatenate([x[h * c:(h + 1) * c, :] for h in range(GLA_HEADS)], axis=1)


def _gla_kernel(has_init, q_ref, k_ref, v_ref, g_ref, lora_ref, waf_ref, baf_ref, wab_ref, bab_ref,
                ng_ref, *rest):
    if has_init:
        s0f_ref, s0b_ref, *rest = rest
    (out_ref, sf_ref, sb_ref, laf_ref, lab_ref, oacc_ref, qtf_ref, qtb_ref, saf_ref, sab_ref,
     stf_ref, stb_ref) = rest
    t = q_ref.shape[0]
    c = GLA_CHUNK
    n = t // c
    hc = GLA_HEADS * c

    lora = lora_ref[...].astype(BF16)
    laf_ref[...] = _log_sigmoid(_dot(lora, waf_ref[...]) + baf_ref[...]) * (1.0 / GLA_GATE_NORM)
    lab_ref[...] = _log_sigmoid(_dot(lora, wab_ref[...]) + bab_ref[...]) * (1.0 / GLA_GATE_NORM)

    if has_init:
        stf_ref[...] = s0f_ref[...].T
        stb_ref[...] = s0b_ref[...].T
    else:
        stf_ref[...] = jnp.zeros_like(stf_ref)
        stb_ref[...] = jnp.zeros_like(stb_ref)
    oacc_ref[...] = jnp.zeros_like(oacc_ref)

    r64 = lax.broadcasted_iota(jnp.int32, (c, c), 0)
    c64 = lax.broadcasted_iota(jnp.int32, (c, c), 1)
    tri_f = jnp.where(c64 <= r64, 1.0, 0.0).astype(BF16)
    tri_b = jnp.where(c64 >= r64, 1.0, 0.0).astype(BF16)
    rr = lax.broadcasted_iota(jnp.int32, (hc, hc), 0)
    cc = lax.broadcasted_iota(jnp.int32, (hc, hc), 1)
    same_head = (rr >> 6) == (cc >> 6)
    keep_f = same_head & ((rr & (c - 1)) >= (cc & (c - 1)))
    keep_b = same_head & ((rr & (c - 1)) <= (cc & (c - 1)))
    head_mask = jnp.where(same_head, 1.0, 0.0).astype(BF16)
    norm_g = ng_ref[...]

    def chunk_rows(ci):
        return pl.ds(pl.multiple_of(ci * c, c), c)

    def tile_heads(x):
        x4 = jnp.concatenate([x] * GLA_HEADS, axis=0)
        return jnp.where(same_head, x4, 0.0).astype(BF16)

    def scan_step(i, carry):
        dirs = []
        for u in range(SCAN_UNROLL):
            dirs += [(SCAN_UNROLL * i + u, laf_ref, tri_f, keep_f, c - 1, stf_ref, saf_ref, qtf_ref),
                     (n - 1 - SCAN_UNROLL * i - u, lab_ref, tri_b, keep_b, 0, stb_ref, sab_ref, qtb_ref)]
        cums = []
        for ci, la_ref, tri, _, _, _, _, _ in dirs:
            la_hi, la_lo = _split_hi_lo(la_ref[chunk_rows(ci), :])
            cums.append(_dot(tri, la_hi) + _dot(tri, la_lo))
        ops = []
        for (ci, _, _, _, last_row, _, _, qt_ref), cum in zip(dirs, cums):
            sl = chunk_rows(ci)
            tot = cum[last_row:last_row + 1, :]
            kc = k_ref[sl, :]
            qt = q_ref[sl, :] * (GLA_DK ** -0.5) * jnp.exp(cum)
            qt_ref[sl, :] = qt.astype(BF16)
            v_rows = _heads_to_rows(v_ref[sl, :])
            ops.append((tot, tile_heads(qt), tile_heads(kc * jnp.exp(-cum)),
                        tile_heads(kc * jnp.exp(tot - cum)), v_rows))
        atts = [_dot_nt(q4, k4) for _, q4, k4, _, _ in ops]
        incs = []
        for (_, _, _, keep, _, _, _, _), (_, _, _, kd4, v_rows), att in zip(dirs, ops, atts):
            att = jnp.where(keep, att, 0.0).astype(BF16)
            incs.append((_dot(att, v_rows.astype(BF16)), _dot(v_rows.T.astype(BF16), kd4)))
        for (ci, _, _, _, _, st_ref, snap_ref, _), (tot, _, _, _, _), (o_intra, st_inc) in zip(dirs, ops, incs):
            oacc_ref[ci] += o_intra
            st = st_ref[...]
            snap_ref[ci] = st.astype(BF16)
            st_ref[...] = jnp.exp(tot) * st + st_inc
        return carry

    def tile_heads_bf16(x):
        return jnp.concatenate([x] * GLA_HEADS, axis=0) * head_mask

    def out_step(i, carry):
        chunks = [OUT_UNROLL * i + u for u in range(OUT_UNROLL)]
        inter = []
        for ci in chunks:
            sl = chunk_rows(ci)
            q4 = jnp.concatenate([tile_heads_bf16(qtf_ref[sl, :]), tile_heads_bf16(qtb_ref[sl, :])], axis=1)
            st = jnp.concatenate([saf_ref[ci], sab_ref[ci]], axis=1)
            inter.append(_dot_nt(q4, st))
        for ci, o_inter in zip(chunks, inter):
            sl = chunk_rows(ci)
            on = _rms_norm(oacc_ref[ci] + o_inter, norm_g)
            gate = _silu(_heads_to_rows(g_ref[sl, :]))
            out_ref[sl, :] = _rows_to_heads(on * gate, c)
        return carry

    lax.fori_loop(0, n // SCAN_UNROLL, scan_step, 0)
    lax.fori_loop(0, n // OUT_UNROLL, out_step, 0)
    sf_ref[...] = stf_ref[...].T
    sb_ref[...] = stb_ref[...].T


def _gla(gla_in, lora, waf, baf, wab, bab, norm_g, s0f=None, s0b=None):
    b, t, _ = gla_in.shape
    has_init = s0f is not None
    n = t // GLA_CHUNK
    bmap = lambda i: (i, 0, 0)
    full = lambda i: (0, 0)
    in_specs = [pl.BlockSpec((None, t, GLA_QK), lambda i: (i, 0, 0)),
                pl.BlockSpec((None, t, GLA_QK), lambda i: (i, 0, 1)),
                pl.BlockSpec((None, t, GLA_V), lambda i: (i, 0, 1)),
                pl.BlockSpec((None, t, GLA_V), lambda i: (i, 0, 2)),
                pl.BlockSpec((None, t, 2 * GLA_LORA), bmap),
                pl.BlockSpec((2 * GLA_LORA, GLA_QK), full),
                pl.BlockSpec((1, GLA_QK), full),
                pl.BlockSpec((2 * GLA_LORA, GLA_QK), full),
                pl.BlockSpec((1, GLA_QK), full),
                pl.BlockSpec((1, GLA_DV), full)]
    args = [gla_in, gla_in, gla_in, gla_in, lora, waf, baf, wab, bab, norm_g]
    if has_init:
        in_specs += [pl.BlockSpec((None, GLA_QK, GLA_DV), bmap)] * 2
        args += [s0f, s0b]
    return pl.pallas_call(
        functools.partial(_gla_kernel, has_init),
        out_shape=(jax.ShapeDtypeStruct((b, t, GLA_V), F32),
                   jax.ShapeDtypeStruct((b, GLA_QK, GLA_DV), F32),
                   jax.ShapeDtypeStruct((b, GLA_QK, GLA_DV), F32)),
        grid=(b,),
        in_specs=in_specs,
        out_specs=(pl.BlockSpec((None, t, GLA_V), bmap),
                   pl.BlockSpec((None, GLA_QK, GLA_DV), bmap),
                   pl.BlockSpec((None, GLA_QK, GLA_DV), bmap)),
        scratch_shapes=[pltpu.VMEM((t, GLA_QK), F32),
                        pltpu.VMEM((t, GLA_QK), F32),
                        pltpu.VMEM((n, GLA_HEADS * GLA_CHUNK, GLA_DV), F32),
                        pltpu.VMEM((t, GLA_QK), BF16),
                        pltpu.VMEM((t, GLA_QK), BF16),
                        pltpu.VMEM((n, GLA_DV, GLA_QK), BF16),
                        pltpu.VMEM((n, GLA_DV, GLA_QK), BF16),
                        pltpu.VMEM((GLA_DV, GLA_QK), F32),
                        pltpu.VMEM((GLA_DV, GLA_QK), F32)],
        compiler_params=pltpu.CompilerParams(dimension_semantics=("arbitrary",),
                                             vmem_limit_bytes=VMEM_LIMIT),
        name="gla",
    )(*args)


def _dup_groups(x):
    lo = lax.broadcasted_iota(jnp.int32, x.shape, 1) < SWA_HEAD_DIM
    xr = pltpu.roll(x, SWA_HEAD_DIM, axis=1)
    return jnp.where(lo, x, xr), jnp.where(lo, xr, x)


def _pairs_attention(qps, sinks, k_dups, vt_dups, mask):
    nq = qps[0].shape[0]
    lo = lax.broadcasted_iota(jnp.int32, (nq, LANES), 1) < SWA_HEAD_DIM
    even = lax.broadcasted_iota(jnp.int32, (1, 2 * nq), 1) < nq
    scores = []
    for qp, k_dup in zip(qps, k_dups):
        q2 = jnp.concatenate([jnp.where(lo, qp, 0.0), jnp.where(lo, 0.0, qp)], axis=0).astype(BF16)
        scores.append(_dot_nt(k_dup, q2))
    probs = []
    for s, (sink_even, sink_odd) in zip(scores, sinks):
        if mask is not None:
            s = jnp.where(mask, s, NEG_INF)
        sink = jnp.where(even, sink_even, sink_odd)
        m = jnp.maximum(jnp.max(s, axis=0, keepdims=True), sink)
        p = jnp.exp(s - m)
        denom = jnp.sum(p, axis=0, keepdims=True) + jnp.exp(sink - m)
        probs.append((p.astype(BF16), 1.0 / denom))
    outs = []
    for (p, rdenom), vt_dup in zip(probs, vt_dups):
        o = _dot(vt_dup, p) * rdenom
        outs.append(jnp.concatenate([o[:SWA_HEAD_DIM, :nq], o[SWA_HEAD_DIM:, nq:]], axis=0).T)
    return outs


def _attn_ctx_kernel(sink_ref, q_ref, k_ref, v_ref, o_ref):
    kd = [x.astype(BF16) for x in _dup_groups(k_ref[...])]
    vt = [x.T.astype(BF16) for x in _dup_groups(v_ref[...])]
    scale = SWA_HEAD_DIM ** -0.5
    pairs = range(SWA_HEADS // 2)
    outs = _pairs_attention([q_ref[:, pr * LANES:(pr + 1) * LANES] * scale for pr in pairs],
                            [(sink_ref[2 * pr], sink_ref[2 * pr + 1]) for pr in pairs],
                            [kd[pr // 2] for pr in pairs], [vt[pr // 2] for pr in pairs], None)
    for pr in pairs:
        o_ref[:, pr * LANES:(pr + 1) * LANES] = outs[pr]


def _attn_ctx(sink, q, k, v):
    b, t, _ = q.shape
    bmap = lambda i: (i, 0, 0)
    return pl.pallas_call(
        _attn_ctx_kernel,
        out_shape=jax.ShapeDtypeStruct((b, t, SWA_Q), F32),
        grid=(b,),
        in_specs=[pl.BlockSpec(memory_space=pltpu.SMEM),
                  pl.BlockSpec((None, t, SWA_Q), bmap),
                  pl.BlockSpec((None, t, SWA_KV), bmap),
                  pl.BlockSpec((None, t, SWA_KV), bmap)],
        out_specs=pl.BlockSpec((None, t, SWA_Q), bmap),
        compiler_params=pltpu.CompilerParams(dimension_semantics=("arbitrary",),
                                             vmem_limit_bytes=VMEM_LIMIT),
        name="attn_ctx",
    )(sink, q, k, v)


def _rope(x, cos, sin_lo, sin_hi):
    return x * cos + pltpu.roll(x, LANES - 16, axis=1) * sin_lo + pltpu.roll(x, 16, axis=1) * sin_hi


def _attn_lat_kernel(sink_ref, q_ref, k_ref, v_ref, kc_ref, vc_ref, cos_ref, sl_ref, sh_ref,
                     o_ref, kw_ref, vw_ref):
    t = q_ref.shape[0]
    ab = ATTN_BLOCK
    nb = t // ab
    scale = SWA_HEAD_DIM ** -0.5

    k_rot = _dup_groups(_rope(k_ref[...], cos_ref[...], sl_ref[...], sh_ref[...]))
    v_dup = _dup_groups(v_ref[...])
    zeros = jnp.zeros((ab, LANES), BF16)
    for grp in range(SWA_KV_HEADS):
        kw_ref[grp, 0:ab, :] = zeros
        kw_ref[grp, ab:ab + t, :] = k_rot[grp].astype(BF16)
        kw_ref[grp, ab + t:, :] = zeros
        vw_ref[grp, 0] = zeros
        for blk in range(nb):
            vw_ref[grp, blk + 1] = v_dup[grp][blk * ab:(blk + 1) * ab, :].T.astype(BF16)
        vw_ref[grp, nb + 1] = zeros
    kc = [x.astype(BF16) for x in _dup_groups(kc_ref[...])]
    vct = [x.T.astype(BF16) for x in _dup_groups(vc_ref[...])]
    lc = kc_ref.shape[0]

    key = lax.broadcasted_iota(jnp.int32, (lc + 3 * ab, 2 * ab), 0) - lc
    tq = lax.broadcasted_iota(jnp.int32, (lc + 3 * ab, 2 * ab), 1) & (ab - 1)
    band = (key < 0) | (jnp.abs(tq + ab - key) <= ab)

    def block(nq, carry):
        row0 = pl.multiple_of(nq * ab, ab)
        s_abs = key + (nq - 1) * ab
        mask = band & ((key < 0) | ((s_abs >= 0) & (s_abs < t)))
        cos = cos_ref[pl.ds(row0, ab), :]
        s_lo = sl_ref[pl.ds(row0, ab), :]
        s_hi = sh_ref[pl.ds(row0, ab), :]
        k_all = [jnp.concatenate([kc[grp], kw_ref[grp, pl.ds(row0, 3 * ab), :]], axis=0)
                 for grp in range(SWA_KV_HEADS)]
        vt_all = [jnp.concatenate([vct[grp], vw_ref[grp, nq], vw_ref[grp, nq + 1], vw_ref[grp, nq + 2]],
                                  axis=1) for grp in range(SWA_KV_HEADS)]
        pairs = range(SWA_HEADS // 2)
        qps = [_rope(q_ref[pl.ds(row0, ab), pr * LANES:(pr + 1) * LANES], cos, s_lo, s_hi) * scale
               for pr in pairs]
        outs = _pairs_attention(qps, [(sink_ref[2 * pr], sink_ref[2 * pr + 1]) for pr in pairs],
                                [k_all[pr // 2] for pr in pairs], [vt_all[pr // 2] for pr in pairs], mask)
        for pr in pairs:
            o_ref[pl.ds(row0, ab), pr * LANES:(pr + 1) * LANES] = outs[pr]
        return carry

    lax.fori_loop(0, nb, block, 0)


def _attn_lat(sink, q, k, v, kc, vc, cos, sin_lo, sin_hi):
    b, t, _ = q.shape
    lc = kc.shape[1]
    bmap = lambda i: (i, 0, 0)
    full = lambda i: (0, 0)
    return pl.pallas_call(
        _attn_lat_kernel,
        out_shape=jax.ShapeDtypeStruct((b, t, SWA_Q), F32),
        grid=(b,),
        in_specs=[pl.BlockSpec(memory_space=pltpu.SMEM),
                  pl.BlockSpec((None, t, SWA_Q), bmap),
                  pl.BlockSpec((None, t, SWA_KV), bmap),
                  pl.BlockSpec((None, t, SWA_KV), bmap),
                  pl.BlockSpec((None, lc, SWA_KV), bmap),
                  pl.BlockSpec((None, lc, SWA_KV), bmap),
                  pl.BlockSpec((t, LANES), full),
                  pl.BlockSpec((t, LANES), full),
                  pl.BlockSpec((t, LANES), full)],
        out_specs=pl.BlockSpec((None, t, SWA_Q), bmap),
        scratch_shapes=[pltpu.VMEM((SWA_KV_HEADS, t + 2 * ATTN_BLOCK, LANES), BF16),
                        pltpu.VMEM((SWA_KV_HEADS, t // ATTN_BLOCK + 2, LANES, ATTN_BLOCK), BF16)],
        compiler_params=pltpu.CompilerParams(dimension_semantics=("arbitrary",),
                                             vmem_limit_bytes=VMEM_LIMIT),
        name="attn_lat",
    )(sink, q, k, v, kc, vc, cos, sin_lo, sin_hi)


def _rope_tables(t):
    half = SWA_HEAD_DIM // 2
    quarter = half // 2
    pos = jnp.arange(t)
    row = (pos // GRID_W).astype(F32)
    col = (pos % GRID_W).astype(F32)
    inv_freq = ROPE_BASE ** (-jnp.arange(quarter, dtype=F32) / quarter)
    lane = jnp.arange(LANES)
    d = lane % SWA_HEAD_DIM
    freq = inv_freq[d % quarter]
    use_row = (d < half)
    ang = jnp.where(use_row[None, :], row[:, None], col[:, None]) * freq[None, :]
    cos = jnp.cos(ang)
    sin = jnp.sin(ang)
    lower = (d % half) < quarter
    return cos, jnp.where(lower[None, :], -sin, 0.0), jnp.where(lower[None, :], 0.0, sin)


def _route(sel, scores):
    n = sel.shape[1]
    gsz = N_EXPERTS // N_EXPERT_GROUPS

    def first_max(x, idx, size):
        m = jnp.max(x, axis=0, keepdims=True)
        first = jnp.min(jnp.where(x == m, idx, float(size)), axis=0, keepdims=True)
        return m, idx == first

    i8 = lax.broadcasted_iota(jnp.int32, (gsz, n), 0).astype(F32)
    rows = []
    for g in range(N_EXPERT_GROUPS):
        slab = sel[g * gsz:(g + 1) * gsz, :]
        m1, hit = first_max(slab, i8, gsz)
        m2 = jnp.max(jnp.where(hit, NEG_INF, slab), axis=0, keepdims=True)
        rows.append(m1 + m2)
    gscore = jnp.concatenate(rows, axis=0)
    gsel = jnp.zeros((N_EXPERT_GROUPS, n), F32)
    for _ in range(TOPK_GROUPS):
        _, hit = first_max(gscore, i8, N_EXPERT_GROUPS)
        gsel = jnp.where(hit, 1.0, gsel)
        gscore = jnp.where(hit, NEG_INF, gscore)
    emask = jnp.concatenate(
        [jnp.broadcast_to(gsel[g:g + 1, :], (gsz, n)) for g in range(N_EXPERT_GROUPS)], axis=0)
    cand = jnp.where(emask > 0.5, sel, NEG_INF)
    ie = lax.broadcasted_iota(jnp.int32, (N_EXPERTS, n), 0).astype(F32)
    w = jnp.zeros((N_EXPERTS, n), F32)
    chosen = jnp.zeros((N_EXPERTS, n), F32)
    hits = []
    for _ in range(TOP_K):
        _, hit = first_max(cand, ie, N_EXPERTS)
        hits.append(hit)
        w = jnp.where(hit, scores, w)
        chosen = jnp.where(hit, 1.0, chosen)
        cand = jnp.where(hit, NEG_INF, cand)
    gates = w / jnp.sum(w, axis=0, keepdims=True) * ROUTED_SCALE

    s_idx = lax.broadcasted_iota(jnp.int32, (n, n), 0)
    t_idx = lax.broadcasted_iota(jnp.int32, (n, n), 1)
    before = jnp.where(s_idx < t_idx, 1.0, 0.0).astype(BF16)
    rank = _dot(chosen.astype(BF16), before)
    count = jnp.sum(chosen, axis=1, keepdims=True)
    padded = jnp.floor((count + (SORT_ALIGN - 1)) * (1.0 / SORT_ALIGN)) * SORT_ALIGN
    padded = jnp.broadcast_to(padded, (N_EXPERTS, LANES))
    e_row = lax.broadcasted_iota(jnp.int32, (N_EXPERTS, N_EXPERTS), 0)
    e_col = lax.broadcasted_iota(jnp.int32, (N_EXPERTS, N_EXPERTS), 1)
    below = jnp.where(e_col < e_row, 1.0, 0.0).astype(BF16)
    start = _dot(below, padded.astype(BF16))
    row = start[:, 0:1] + rank
    pos = jnp.concatenate([jnp.sum(jnp.where(h, row, 0.0), axis=0, keepdims=True) for h in hits], axis=0)
    wts = jnp.concatenate([jnp.sum(jnp.where(h, gates, 0.0), axis=0, keepdims=True) for h in hits], axis=0)
    return pos, wts, padded, start


def _outproj_kernel(gla_ref, att_ref, x_ref, wo_ref, g1_ref, sh_ref, sc_ref, ng_ref, rw_ref, rwh_ref,
                    rb_ref, x1_ref, xm_ref, pos_ref, wts_ref, cnt_ref, start_ref):
    y = (_dot(gla_ref[...].astype(BF16), wo_ref[0:GLA_V, :])
         + _dot(att_ref[...].astype(BF16), wo_ref[GLA_V:, :]))
    x1 = x_ref[...] + g1_ref[...] * y
    x1_ref[...] = x1
    xm = _rms_norm(x1, ng_ref[...]) * (1.0 + sc_ref[...]) + sh_ref[...]
    xm_hi, xm_lo = _split_hi_lo(xm)
    xm_ref[...] = xm_hi
    lg = _dot(xm_hi, rw_ref[...])
    logits = lg[:, :N_EXPERTS] + lg[:, N_EXPERTS:] + _dot(xm_lo, rwh_ref[...])
    tm = logits.shape[0]
    lt = jnp.concatenate([logits, jnp.zeros((tm, LANES - N_EXPERTS), F32)], axis=1).T[:N_EXPERTS, :]
    scores = _sigmoid(lt)
    pos_ref[...], wts_ref[...], cnt_ref[...], start_ref[...] = _route(scores + rb_ref[...], scores)


def _outproj(gla_out, att_out, x, w_out, g1, sh2, sc2, norm_g, rw_cat, rw_hi, rbias, *, tm):
    b, t, d = x.shape
    nmod = g1.shape[0]
    mod_map = (lambda i, j: (i, 0, 0)) if nmod > 1 else (lambda i, j: (0, 0, 0))
    row = lambda i, j: (i, j, 0)
    full = lambda i, j: (0, 0)
    tile = lambda i, j: (i, j, 0, 0)
    nt = t // tm
    return pl.pallas_call(
        _outproj_kernel,
        out_shape=(jax.ShapeDtypeStruct((b, t, d), F32),
                   jax.ShapeDtypeStruct((b, t, d), BF16),
                   jax.ShapeDtypeStruct((b, nt, TOP_K, tm), F32),
                   jax.ShapeDtypeStruct((b, nt, TOP_K, tm), F32),
                   jax.ShapeDtypeStruct((b, nt, N_EXPERTS, LANES), F32),
                   jax.ShapeDtypeStruct((b, nt, N_EXPERTS, LANES), F32)),
        grid=(b, t // tm),
        in_specs=[pl.BlockSpec((None, tm, GLA_V), row),
                  pl.BlockSpec((None, tm, SWA_Q), row),
                  pl.BlockSpec((None, tm, d), row),
                  pl.BlockSpec((d, d), full),
                  pl.BlockSpec((None, 1, d), mod_map),
                  pl.BlockSpec((None, 1, d), mod_map),
                  pl.BlockSpec((None, 1, d), mod_map),
                  pl.BlockSpec((1, d), full),
                  pl.BlockSpec((d, 2 * N_EXPERTS), full),
                  pl.BlockSpec((d, N_EXPERTS), full),
                  pl.BlockSpec((N_EXPERTS, 1), full)],
        out_specs=(pl.BlockSpec((None, tm, d), row),
                   pl.BlockSpec((None, tm, d), row),
                   pl.BlockSpec((None, None, TOP_K, tm), tile),
                   pl.BlockSpec((None, None, TOP_K, tm), tile),
                   pl.BlockSpec((None, None, N_EXPERTS, LANES), tile),
                   pl.BlockSpec((None, None, N_EXPERTS, LANES), tile)),
        compiler_params=pltpu.CompilerParams(dimension_semantics=("arbitrary", "arbitrary"),
                                             vmem_limit_bytes=VMEM_LIMIT),
        name="outproj",
    )(gla_out, att_out, x, w_out, g1, sh2, sc2, norm_g, rw_cat, rw_hi, rbias)


MOE_TILE = 256
SORT_ALIGN = 16
SORT_ROWS = 3072
ROW_TILE = 512
GATHER_SLOTS = 3
FFN_CHAINS = 4
COMBINE_CHUNK = 1024
PREP_EB = 4


def _expert_prep_kernel(wg_ref, wu_ref, wd_ref, wgu_ref, wdb_ref):
    wgu_ref[...] = jnp.concatenate([wg_ref[...].astype(BF16), wu_ref[...].astype(BF16)], axis=2)
    wdb_ref[...] = wd_ref[...].astype(BF16)


def _expert_prep(wg, wu, wd):
    ne, d, f = wg.shape
    blk = lambda e: (e, 0, 0)
    return pl.pallas_call(
        _expert_prep_kernel,
        out_shape=(jax.ShapeDtypeStruct((ne, d, 2 * f), BF16),
                   jax.ShapeDtypeStruct((ne, f, d), BF16)),
        grid=(ne // PREP_EB,),
        in_specs=[pl.BlockSpec((PREP_EB, d, f), blk),
                  pl.BlockSpec((PREP_EB, d, f), blk),
                  pl.BlockSpec((PREP_EB, f, d), blk)],
        out_specs=(pl.BlockSpec((PREP_EB, d, 2 * f), blk),
                   pl.BlockSpec((PREP_EB, f, d), blk)),
        compiler_params=pltpu.CompilerParams(dimension_semantics=("arbitrary",),
                                             vmem_limit_bytes=VMEM_LIMIT),
        name="expert_prep",
    )(wg, wu, wd)


def _moe_sort_kernel(xm_ref, pos_ref, *rest):
    xs_ref = rest[-1]
    x = xm_ref[...]
    pos = pos_ref[...]
    tm = x.shape[0]
    for blk in range(SORT_ROWS // tm):
        rows = (lax.broadcasted_iota(jnp.int32, (tm, tm), 0) + blk * tm).astype(F32)
        onehot = jnp.zeros((tm, tm), F32)
        for k in range(TOP_K):
            onehot = jnp.where(rows == pos[k:k + 1, :], 1.0, onehot)
        out = _dot(onehot.astype(BF16), x).astype(BF16)
        xs_ref[blk * tm:(blk + 1) * tm, :] = out


def _moe_sort(xm, pos, tile0, n_tiles_total, xs=None):
    n, d = xm.shape
    tm = pos.shape[-1]
    nt = n // tm
    gpt = SORT_ROWS // SORT_ALIGN
    in_specs = [pl.BlockSpec((tm, d), lambda i: (i, 0)),
                pl.BlockSpec((None, TOP_K, tm), lambda i: (i, 0, 0))]
    args = [xm, pos.reshape(nt, TOP_K, tm)]
    aliases = {}
    if xs is not None:
        in_specs.append(pl.BlockSpec(memory_space=pl.ANY))
        args.append(xs)
        aliases = {2: 0}
    return pl.pallas_call(
        _moe_sort_kernel,
        out_shape=jax.ShapeDtypeStruct((n_tiles_total * SORT_ROWS, d), BF16),
        grid=(nt,),
        in_specs=in_specs,
        out_specs=pl.BlockSpec((SORT_ROWS, d), lambda i: (tile0 + i, 0)),
        input_output_aliases=aliases,
        compiler_params=pltpu.CompilerParams(dimension_semantics=("arbitrary",),
                                             vmem_limit_bytes=VMEM_LIMIT),
        name="moe_sort",
    )(*args)


def _moe_row_tiles(n_tokens):
    rows = n_tokens * TOP_K + (n_tokens // MOE_TILE) * N_EXPERTS * (SORT_ALIGN - 1) + N_EXPERTS * (ROW_TILE - 1)
    return -(-rows // ROW_TILE) + GATHER_SLOTS - 1


PLAN_CHUNK = 1280


def _int_dot_r(a, onehot):
    hi = jnp.floor(a * (1.0 / 256.0))
    return _dot(hi.astype(BF16), onehot) * 256.0 + _dot((a - hi * 256.0).astype(BF16), onehot)


def _int_dot_l(onehot, b):
    hi = jnp.floor(b * (1.0 / 256.0))
    return _dot(onehot, hi.astype(BF16)) * 256.0 + _dot(onehot, (b - hi * 256.0).astype(BF16))


def _moe_plan_kernel(cnt_ref, start_ref, src_ref, te_ref, nu_ref, back_ref):
    nt, ne = cnt_ref.shape
    gpt = SORT_ROWS // SORT_ALIGN
    gpr = ROW_TILE // SORT_ALIGN
    gc = cnt_ref[...] * (1.0 / SORT_ALIGN)
    ls = start_ref[...] * (1.0 / SORT_ALIGN)

    def transpose(x):
        x = jnp.concatenate([x, jnp.zeros((nt, LANES - ne), F32)], axis=1)
        x = jnp.concatenate([x, jnp.zeros((LANES - nt, LANES), F32)], axis=0)
        return x.T[:ne, :nt]

    def tri(n, keep):
        return jnp.where(keep(lax.broadcasted_iota(jnp.int32, (n, n), 0),
                              lax.broadcasted_iota(jnp.int32, (n, n), 1)), 1.0, 0.0).astype(BF16)

    gc_t = transpose(gc)
    ls_t = transpose(ls)
    tot_c = jnp.broadcast_to(jnp.sum(gc_t, axis=1, keepdims=True), (ne, LANES))
    ptot_c = jnp.floor((tot_c + (gpr - 1)) * (1.0 / gpr)) * gpr
    gend_c = _int_dot_l(tri(ne, lambda r, c: c <= r), ptot_c)
    gstart_c = gend_c - ptot_c
    n_used = gend_c[ne - 1:ne, :] * (1.0 / gpr)
    nu_ref[...] = n_used.astype(jnp.int32)
    tot_r = jnp.sum(gc, axis=0, keepdims=True)
    ptot_r = jnp.floor((tot_r + (gpr - 1)) * (1.0 / gpr)) * gpr
    gstart_r = _int_dot_r(jnp.broadcast_to(ptot_r, (8, ne)), tri(ne, lambda r, c: r < c))
    cumex = _dot(tri(nt, lambda r, c: c < r), gc.astype(BF16))
    cumex_t = _dot(gc_t.astype(BF16), tri(nt, lambda r, c: r < c))
    tile_base = lax.broadcasted_iota(jnp.int32, (nt, ne), 0).astype(F32) * gpt + ls
    table = jnp.concatenate([cumex + gc, cumex, tile_base, gstart_r, jnp.broadcast_to(tot_r, (8, ne))], axis=0)

    e_iota = lax.broadcasted_iota(jnp.int32, (ne, PLAN_CHUNK), 0).astype(F32)
    for ch in range(src_ref.shape[1] // PLAN_CHUNK):
        g = (lax.broadcasted_iota(jnp.int32, (1, PLAN_CHUNK), 1) + ch * PLAN_CHUNK).astype(F32)
        eg = jnp.sum(jnp.where(gend_c[:, 0:1] <= g, 1.0, 0.0), axis=0, keepdims=True)
        picked = _int_dot_r(table, jnp.where(e_iota == eg, 1.0, 0.0).astype(BF16))
        cum_g, cumex_g, base_g = picked[0:nt], picked[nt:2 * nt], picked[2 * nt:3 * nt]
        u = g - picked[3 * nt:3 * nt + 1]
        in_tile = (cumex_g <= u) & (u < cum_g)
        src = jnp.sum(jnp.where(in_tile, base_g - cumex_g, 0.0), axis=0, keepdims=True) + u
        src = jnp.where(u < picked[3 * nt + 8:3 * nt + 9], src, gpt - 1.0)
        src_ref[:, ch * PLAN_CHUNK:(ch + 1) * PLAN_CHUNK] = src.astype(jnp.int32)

    r = lax.broadcasted_iota(jnp.int32, (1, te_ref.shape[1]), 1).astype(F32)
    r = jnp.minimum(r, n_used[:, 0:1] - 1.0)
    te_ref[...] = jnp.sum(jnp.where(gend_c[:, 0:1] <= r * gpr, 1.0, 0.0), axis=0,
                          keepdims=True).astype(jnp.int32)

    lg = lax.broadcasted_iota(jnp.int32, (ne, back_ref.shape[1]), 1).astype(F32)
    for t in range(nt):
        first = ls_t[:, t:t + 1]
        inside = (first <= lg) & (lg < first + gc_t[:, t:t + 1])
        shift = gstart_c[:, 0:1] + cumex_t[:, t:t + 1] - first
        val = jnp.sum(jnp.where(inside, shift + lg, 0.0), axis=0, keepdims=True)
        back_ref[t:t + 1, :] = val.astype(jnp.int32)


def _moe_plan(cnt, start):
    nt, ne = cnt.shape
    row_tiles = _moe_row_tiles(nt * MOE_TILE)
    gpt = SORT_ROWS // SORT_ALIGN
    gpr = ROW_TILE // SORT_ALIGN
    n_src = -(-(row_tiles * gpr) // PLAN_CHUNK) * PLAN_CHUNK
    n_te = -(-row_tiles // LANES) * LANES
    n_back = -(-gpt // LANES) * LANES
    src, te, nu, back = pl.pallas_call(
        _moe_plan_kernel,
        out_shape=(jax.ShapeDtypeStruct((1, n_src), jnp.int32),
                   jax.ShapeDtypeStruct((1, n_te), jnp.int32),
                   jax.ShapeDtypeStruct((1, LANES), jnp.int32),
                   jax.ShapeDtypeStruct((nt, n_back), jnp.int32)),
        compiler_params=pltpu.CompilerParams(vmem_limit_bytes=VMEM_LIMIT),
        name="moe_plan",
    )(cnt, start)
    return nu[0, :1], te[0, :row_tiles], src[0, :row_tiles * gpr], back[:, :gpt]


def _moe_experts_kernel(nu_ref, te_ref, src_ref, xs_hbm, wgu_ref, wd_ref, ys_ref, xbuf, sem):
    r = pl.program_id(0)
    n_used = nu_ref[0]
    gpr = ROW_TILE // SORT_ALIGN
    slot = lax.rem(r, GATHER_SLOTS)

    def gather(tile, to_slot):
        for j in range(gpr):
            row = pl.multiple_of(src_ref[tile * gpr + j] * SORT_ALIGN, SORT_ALIGN)
            pltpu.make_async_copy(xs_hbm.at[pl.ds(row, SORT_ALIGN), :],
                                  xbuf.at[to_slot, j * SORT_ALIGN:(j + 1) * SORT_ALIGN, :], sem.at[to_slot]).start()

    def drain(of_slot):
        for j in range(gpr):
            pltpu.make_async_copy(xs_hbm.at[0:SORT_ALIGN, :],
                                  xbuf.at[of_slot, j * SORT_ALIGN:(j + 1) * SORT_ALIGN, :], sem.at[of_slot]).wait()

    @pl.when(r == 0)
    def _():
        gather(0, 0)
        gather(1, 1)

    @pl.when((r >= n_used) & (r < n_used + 2))
    def _():
        drain(slot)

    @pl.when(r < n_used)
    def _():
        drain(slot)
        gather(r + 2, lax.rem(r + 2, GATHER_SLOTS))
        part = ROW_TILE // FFN_CHAINS
        xs = [xbuf[slot, c * part:(c + 1) * part, :] for c in range(FFN_CHAINS)]
        abs_ = [_dot(x, wgu_ref[...]) for x in xs]
        hs = [(_silu(ab[:, :EXPERT_FF]) * ab[:, EXPERT_FF:]).astype(BF16) for ab in abs_]
        ys = [_dot(h, wd_ref[...]).astype(BF16) for h in hs]
        for c in range(FFN_CHAINS):
            ys_ref[c * part:(c + 1) * part, :] = ys[c]


def _moe_experts(n_used, tile_expert, src, xs, wgu, wd):
    d = xs.shape[-1]
    gpr = ROW_TILE // SORT_ALIGN
    row_tiles = tile_expert.shape[0]
    out_map = lambda r, nu, te, sr: (jnp.minimum(r, nu[0] - 1), 0)
    w_map = lambda r, nu, te, sr: (te[r], 0, 0)
    grid_spec = pltpu.PrefetchScalarGridSpec(
        num_scalar_prefetch=3,
        grid=(row_tiles,),
        in_specs=[pl.BlockSpec(memory_space=pl.ANY),
                  pl.BlockSpec((None, d, 2 * EXPERT_FF), w_map),
                  pl.BlockSpec((None, EXPERT_FF, d), w_map)],
        out_specs=pl.BlockSpec((ROW_TILE, d), out_map),
        scratch_shapes=[pltpu.VMEM((GATHER_SLOTS, ROW_TILE, d), BF16),
                        pltpu.SemaphoreType.DMA((GATHER_SLOTS,))])
    return pl.pallas_call(
        _moe_experts_kernel,
        out_shape=jax.ShapeDtypeStruct((row_tiles * ROW_TILE, d), BF16),
        grid_spec=grid_spec,
        compiler_params=pltpu.CompilerParams(dimension_semantics=("arbitrary",),
                                             vmem_limit_bytes=VMEM_LIMIT),
        name="moe_experts",
    )(n_used, tile_expert, src, xs, wgu, wd)


def _moe_combine_kernel(back_ref, ys_hbm, pos_ref, wts_ref, xm_ref, x1_ref, g2_ref, fg_ref,
                        swg_ref, swu_ref, swd_ref, o_ref, buf, sem):
    i = pl.program_id(0)
    gpt = SORT_ROWS // SORT_ALIGN
    slot = lax.rem(i, 2)

    def gather(tile, to_slot):
        for g in range(gpt):
            row = pl.multiple_of(back_ref[tile * gpt + g] * SORT_ALIGN, SORT_ALIGN)
            pltpu.make_async_copy(ys_hbm.at[pl.ds(row, SORT_ALIGN), :],
                                  buf.at[to_slot, g * SORT_ALIGN:(g + 1) * SORT_ALIGN, :], sem.at[to_slot]).start()

    @pl.when(i == 0)
    def _():
        gather(0, 0)

    @pl.when(i + 1 < pl.num_programs(0))
    def _():
        gather(i + 1, 1 - slot)

    x = xm_ref[...]
    tm = x.shape[0]
    shared = _dot((_silu(_dot(x, swg_ref[...])) * _dot(x, swu_ref[...])).astype(BF16), swd_ref[...])
    pad = jnp.zeros((LANES - TOP_K, tm), F32)
    pos_t = jnp.concatenate([pos_ref[...], pad], axis=0).T
    wts_t = jnp.concatenate([wts_ref[...], pad], axis=0).T
    for g in range(gpt):
        pltpu.make_async_copy(ys_hbm.at[0:SORT_ALIGN, :], buf.at[slot, g * SORT_ALIGN:(g + 1) * SORT_ALIGN, :],
                              sem.at[slot]).wait()
    routed = shared
    for c0 in range(0, SORT_ROWS, COMBINE_CHUNK):
        rows = (lax.broadcasted_iota(jnp.int32, (tm, COMBINE_CHUNK), 1) + c0).astype(F32)
        comb = jnp.zeros((tm, COMBINE_CHUNK), F32)
        for k in range(TOP_K):
            comb = jnp.where(rows == pos_t[:, k:k + 1], wts_t[:, k:k + 1], comb)
        routed = routed + _dot(comb.astype(BF16), buf[slot, c0:c0 + COMBINE_CHUNK, :])
    y = x1_ref[...] + g2_ref[...] * routed
    o_ref[...] = _rms_norm(y, fg_ref[...])


def _moe_combine(back, ys, pos, wts, xm, x1, g2, final_g, swg, swu, swd, *, tiles_per_mod):
    n, d = xm.shape
    tm = pos.shape[-1]
    nt = n // tm
    gpt = SORT_ROWS // SORT_ALIGN
    row = lambda i, bk: (i, 0)
    full = lambda i, bk: (0, 0)
    tile = lambda i, bk: (i, 0, 0)
    mod_map = lambda i, bk: (i // tiles_per_mod, 0, 0)
    grid_spec = pltpu.PrefetchScalarGridSpec(
        num_scalar_prefetch=1,
        grid=(nt,),
        in_specs=[pl.BlockSpec(memory_space=pl.ANY),
                  pl.BlockSpec((None, TOP_K, tm), tile),
                  pl.BlockSpec((None, TOP_K, tm), tile),
                  pl.BlockSpec((tm, d), row),
                  pl.BlockSpec((tm, d), row),
                  pl.BlockSpec((None, 1, d), mod_map),
                  pl.BlockSpec((1, d), full),
                  pl.BlockSpec((d, SHARED_FF), full),
                  pl.BlockSpec((d, SHARED_FF), full),
                  pl.BlockSpec((SHARED_FF, d), full)],
        out_specs=pl.BlockSpec((tm, d), row),
        scratch_shapes=[pltpu.VMEM((2, SORT_ROWS, d), BF16),
                        pltpu.SemaphoreType.DMA((2,))])
    return pl.pallas_call(
        _moe_combine_kernel,
        out_shape=jax.ShapeDtypeStruct((n, d), F32),
        grid_spec=grid_spec,
        compiler_params=pltpu.CompilerParams(dimension_semantics=("arbitrary",),
                                             vmem_limit_bytes=VMEM_LIMIT),
        name="moe_combine",
    )(back, ys, pos.reshape(nt, TOP_K, tm), wts.reshape(nt, TOP_K, tm), xm, x1, g2, final_g, swg, swu, swd)


def _mix(x, mods, p, attn_fn, s0=None):
    sh1, sc1, g1, sh2, sc2, _ = mods
    gla_in, lora, q_s, k_s, v_s = _inproj(x, p["norm_attn_g"], sh1, sc1, p["w_gla"], p["w_lora"],
                                          p["w_swa"], tm=256)
    if s0 is None:
        gla_out, s_f, s_b = _gla(gla_in, lora, p["waf"], p["baf"], p["wab"], p["bab"], p["gla_norm_g"])
    else:
        gla_out, s_f, s_b = _gla(gla_in, lora, p["waf"], p["baf"], p["wab"], p["bab"], p["gla_norm_g"],
                                 s0[0], s0[1])
    att_out = attn_fn(q_s, k_s, v_s)
    routed = _outproj(gla_out, att_out, x, p["w_out"], g1, sh2, sc2, p["norm_ffn_g"],
                      p["rw_cat"], p["rw_hi"], p["rbias"], tm=MOE_TILE)
    return routed, k_s, v_s, s_f, s_b


def _moe(streams, p):
    d = D_MODEL
    n_tiles = [r[1].shape[0] * r[1].shape[1] // MOE_TILE for r, _ in streams]
    total = sum(n_tiles)
    xs = None
    tile0 = 0
    for (x1, xm, pos, wts, cnt, start), _ in streams:
        xs = _moe_sort(xm.reshape(-1, d), pos, tile0, total, xs)
        tile0 += xm.shape[0] * xm.shape[1] // MOE_TILE
    cnt_all = jnp.concatenate([r[4][..., 0].reshape(-1, N_EXPERTS) for r, _ in streams], axis=0)
    start_all = jnp.concatenate([r[5][..., 0].reshape(-1, N_EXPERTS) for r, _ in streams], axis=0)
    n_used, tile_expert, src, back = _moe_plan(cnt_all, start_all)
    ys = _moe_experts(n_used, tile_expert, src, xs, p["wgu"], p["wd"])
    outs = []
    tile0 = 0
    for ((x1, xm, pos, wts, cnt, start), g2), nt in zip(streams, n_tiles):
        b, t, _ = x1.shape
        tiles_per_mod = (t // MOE_TILE) if g2.shape[0] > 1 else nt
        y = _moe_combine(back[tile0:tile0 + nt].reshape(-1), ys, pos, wts, xm.reshape(-1, d),
                         x1.reshape(-1, d), g2, p["final_norm_g"], p["swg"], p["swu"], p["swd"],
                         tiles_per_mod=tiles_per_mod)
        outs.append(y.reshape(b, t, d))
        tile0 += nt
    return outs


def kernel(x_prompt, x_sample, c, cache_swa_k, cache_swa_v, state_gla_fwd, state_gla_bwd, c_ctx, w_ada, b_ada, norm_attn_g, norm_ffn_g, w_in, gla_wa_f, gla_ba_f, gla_wa_b, gla_ba_b, gla_norm_g, swa_sink, w_out, router_w, router_bias, exp_w_gate, exp_w_up, exp_w_down, sh_w_gate, sh_w_up, sh_w_down, final_norm_g):
    l = 0
    d = D_MODEL
    nb_ctx, t_ctx, _ = x_prompt.shape
    nb_lat, t_lat, _ = x_sample.shape

    pad = jnp.zeros((8 - 1 - nb_lat, d), F32)
    cond8 = jnp.concatenate([c_ctx[None, :], c, pad], axis=0)
    mod = _adaln(cond8, w_ada[l], b_ada[l][None, :])
    mods_ctx = [mod[0:1, i * d:(i + 1) * d][:, None, :] for i in range(6)]
    mods_lat = [mod[1:1 + nb_lat, i * d:(i + 1) * d][:, None, :] for i in range(6)]

    wgu, wdb = _expert_prep(exp_w_gate[l], exp_w_up[l], exp_w_down[l])
    zeros_lora = jnp.zeros((GLA_LORA, GLA_QK), F32)
    rw = router_w[l]
    rw_hi = rw.astype(BF16)
    rw_lo = (rw - rw_hi.astype(F32)).astype(BF16)
    w_in_b = w_in[l].astype(BF16)
    p = {
        "norm_attn_g": norm_attn_g[l][None, :],
        "norm_ffn_g": norm_ffn_g[l][None, :],
        "final_norm_g": final_norm_g[None, :],
        "w_gla": w_in_b[:, :2 * GLA_QK + 2 * GLA_V],
        "w_lora": w_in_b[:, 2 * GLA_QK + 2 * GLA_V:2 * GLA_QK + 2 * GLA_V + 2 * GLA_LORA],
        "w_swa": w_in_b[:, 2 * GLA_QK + 2 * GLA_V + 2 * GLA_LORA:],
        "waf": jnp.concatenate([gla_wa_f[l], zeros_lora], axis=0).astype(BF16),
        "wab": jnp.concatenate([zeros_lora, gla_wa_b[l]], axis=0).astype(BF16),
        "baf": gla_ba_f[l][None, :],
        "bab": gla_ba_b[l][None, :],
        "gla_norm_g": gla_norm_g[l][None, :],
        "w_out": w_out[l].astype(BF16),
        "rw_cat": jnp.concatenate([rw_hi, rw_lo], axis=1),
        "rw_hi": rw_hi,
        "rbias": router_bias[l][:, None],
        "wgu": wgu, "wd": wdb,
        "swg": sh_w_gate[l].astype(BF16), "swu": sh_w_up[l].astype(BF16),
        "swd": sh_w_down[l].astype(BF16),
    }
    sink = swa_sink[l]

    routed_ctx, k_c, v_c, s_f, s_b = _mix(x_prompt, mods_ctx, p, functools.partial(_attn_ctx, sink))

    cos, sin_lo, sin_hi = _rope_tables(t_lat)
    kc = cache_swa_k[:, l].reshape(nb_lat, -1, SWA_KV)
    vc = cache_swa_v[:, l].reshape(nb_lat, -1, SWA_KV)
    lat_attn = lambda q, k, v: _attn_lat(sink, q, k, v, kc, vc, cos, sin_lo, sin_hi)
    s0 = (state_gla_fwd[:, l].reshape(nb_lat, GLA_QK, GLA_DV),
          state_gla_bwd[:, l].reshape(nb_lat, GLA_QK, GLA_DV))
    routed_lat, _, _, _, _ = _mix(x_sample, mods_lat, p, lat_attn, s0)
    y_prompt, y_sample = _moe([(routed_ctx, mods_ctx[5]), (routed_lat, mods_lat[5])], p)

    new_k = k_c.reshape(nb_ctx, 1, t_ctx, SWA_KV_HEADS, SWA_HEAD_DIM)
    new_v = v_c.reshape(nb_ctx, 1, t_ctx, SWA_KV_HEADS, SWA_HEAD_DIM)
    new_sf = s_f.reshape(nb_ctx, 1, GLA_HEADS, GLA_DK, GLA_DV)
    new_sb = s_b.reshape(nb_ctx, 1, GLA_HEADS, GLA_DK, GLA_DV)
    return (y_prompt, y_sample, new_k, new_v, new_sf, new_sb)
```

```python
import functools

import jax
import jax.numpy as jnp
from jax import lax
from jax.experimental import pallas as pl
from jax.experimental.pallas import tpu as pltpu

F32 = jnp.float32
BF16 = jnp.bfloat16

D_MODEL = 1024
GLA_HEADS = 4
GLA_DK = 64
GLA_DV = 128
GLA_LORA = 16
GLA_GATE_NORM = 16.0
GLA_CHUNK = 64
GLA_QK = GLA_HEADS * GLA_DK
GLA_V = GLA_HEADS * GLA_DV
SWA_HEAD_DIM = 64
SWA_HEADS = 8
SWA_KV_HEADS = 2
SWA_Q = SWA_HEADS * SWA_HEAD_DIM
SWA_KV = SWA_KV_HEADS * SWA_HEAD_DIM
ATTN_BLOCK = 128
GRID_W = 64
ROPE_BASE = 10000.0
N_EXPERTS = 64
TOP_K = 8
N_EXPERT_GROUPS = 8
TOPK_GROUPS = 4
EXPERT_FF = 128
SHARED_FF = 256
ROUTED_SCALE = 2.5
EPS = 1e-6

LANES = 128
VMEM_LIMIT = 56 * 1024 * 1024

NEG_INF = float("-inf")


def _dot(a, b):
    return jnp.dot(a, b, preferred_element_type=F32)


def _dot_nt(a, b):
    return lax.dot_general(a, b, (((1,), (1,)), ((), ())), preferred_element_type=F32)


def _split_hi_lo(x):
    hi = x.astype(BF16)
    lo = (x - hi.astype(F32)).astype(BF16)
    return hi, lo


def _sigmoid(x):
    return 1.0 / (1.0 + jnp.exp(-x))


def _silu(x):
    return x * _sigmoid(x)


def _rms_norm(x, g):
    ms = jnp.mean(x * x, axis=-1, keepdims=True)
    return x * lax.rsqrt(ms + EPS) * g


def _adaln_kernel(c_ref, w_ref, b_ref, o_ref):
    a_hi, a_lo = _split_hi_lo(_silu(c_ref[...]))
    w_hi, w_lo = _split_hi_lo(w_ref[...])
    o_ref[...] = _dot(a_hi, w_hi) + _dot(a_lo, w_hi) + _dot(a_hi, w_lo) + b_ref[...]


def _adaln(cond8, w_ada, b_ada):
    n = w_ada.shape[1]
    tn = 1536
    return pl.pallas_call(
        _adaln_kernel,
        out_shape=jax.ShapeDtypeStruct((8, n), F32),
        grid=(n // tn,),
        in_specs=[pl.BlockSpec((8, D_MODEL), lambda j: (0, 0)),
                  pl.BlockSpec((D_MODEL, tn), lambda j: (0, j)),
                  pl.BlockSpec((1, tn), lambda j: (0, j))],
        out_specs=pl.BlockSpec((8, tn), lambda j: (0, j)),
        compiler_params=pltpu.CompilerParams(dimension_semantics=("arbitrary",),
                                             vmem_limit_bytes=VMEM_LIMIT),
        name="adaln",
    )(cond8, w_ada, b_ada)


def _inproj_kernel(x_ref, g_ref, sh_ref, sc_ref, wg_ref, wl_ref, ws_ref,
                   gla_ref, lora_ref, q_ref, k_ref, v_ref):
    h = _rms_norm(x_ref[...], g_ref[...]) * (1.0 + sc_ref[...]) + sh_ref[...]
    hb = h.astype(BF16)
    gla_ref[...] = _dot(hb, wg_ref[...])
    lora_ref[...] = _dot(hb, wl_ref[...])
    s = _dot(hb, ws_ref[...])
    q_ref[...] = s[:, :SWA_Q]
    k_ref[...] = s[:, SWA_Q:SWA_Q + SWA_KV]
    v_ref[...] = s[:, SWA_Q + SWA_KV:]


def _inproj(x, g, sh, sc, w_gla, w_lora, w_swa, *, tm):
    b, t, d = x.shape
    nmod = sh.shape[0]
    mod_map = (lambda i, j: (i, 0, 0)) if nmod > 1 else (lambda i, j: (0, 0, 0))
    row = lambda i, j: (i, j, 0)
    full = lambda i, j: (0, 0)
    n_gla = w_gla.shape[1]
    n_lora = w_lora.shape[1]
    return pl.pallas_call(
        _inproj_kernel,
        out_shape=(jax.ShapeDtypeStruct((b, t, n_gla), F32),
                   jax.ShapeDtypeStruct((b, t, n_lora), F32),
                   jax.ShapeDtypeStruct((b, t, SWA_Q), F32),
                   jax.ShapeDtypeStruct((b, t, SWA_KV), F32),
                   jax.ShapeDtypeStruct((b, t, SWA_KV), F32)),
        grid=(b, t // tm),
        in_specs=[pl.BlockSpec((None, tm, d), row),
                  pl.BlockSpec((1, d), full),
                  pl.BlockSpec((None, 1, d), mod_map),
                  pl.BlockSpec((None, 1, d), mod_map),
                  pl.BlockSpec((d, n_gla), full),
                  pl.BlockSpec((d, n_lora), full),
                  pl.BlockSpec((d, w_swa.shape[1]), full)],
        out_specs=(pl.BlockSpec((None, tm, n_gla), row),
                   pl.BlockSpec((None, tm, n_lora), row),
                   pl.BlockSpec((None, tm, SWA_Q), row),
                   pl.BlockSpec((None, tm, SWA_KV), row),
                   pl.BlockSpec((None, tm, SWA_KV), row)),
        compiler_params=pltpu.CompilerParams(dimension_semantics=("arbitrary", "arbitrary"),
                                             vmem_limit_bytes=VMEM_LIMIT),
        name="inproj",
    )(x, g, sh, sc, w_gla, w_lora, w_swa)


SCAN_UNROLL = 2
OUT_UNROLL = 4


def _log_sigmoid(x):
    return jnp.minimum(x, 0.0) - jnp.log(1.0 + jnp.exp(-jnp.abs(x)))


def _heads_to_rows(x):
    return jnp.concatenate([x[:, h * LANES:(h + 1) * LANES] for h in range(GLA_HEADS)], axis=0)


def _rows_to_heads(x, c):
    return jnp.concatenate([x[h * c:(h + 1) * c, :] for h in range(GLA_HEADS)], axis=1)


def _gla_kernel(has_init, q_ref, k_ref, v_ref, g_ref, lora_ref, waf_ref, baf_ref, wab_ref, bab_ref,
                ng_ref, *rest):
    if has_init:
        s0f_ref, s0b_ref, *rest = rest
    (out_ref, sf_ref, sb_ref, laf_ref, lab_ref, oacc_ref, qtf_ref, qtb_ref, saf_ref, sab_ref,
     stf_ref, stb_ref) = rest
    t = q_ref.shape[0]
    c = GLA_CHUNK
    n = t // c
    hc = GLA_HEADS * c

    lora = lora_ref[...].astype(BF16)
    laf_ref[...] = _log_sigmoid(_dot(lora, waf_ref[...]) + baf_ref[...]) * (1.0 / GLA_GATE_NORM)
    lab_ref[...] = _log_sigmoid(_dot(lora, wab_ref[...]) + bab_ref[...]) * (1.0 / GLA_GATE_NORM)

    if has_init:
        stf_ref[...] = s0f_ref[...].T
        stb_ref[...] = s0b_ref[...].T
    else:
        stf_ref[...] = jnp.zeros_like(stf_ref)
        stb_ref[...] = jnp.zeros_like(stb_ref)
    oacc_ref[...] = jnp.zeros_like(oacc_ref)

    r64 = lax.broadcasted_iota(jnp.int32, (c, c), 0)
    c64 = lax.broadcasted_iota(jnp.int32, (c, c), 1)
    tri_f = jnp.where(c64 <= r64, 1.0, 0.0).astype(BF16)
    tri_b = jnp.where(c64 >= r64, 1.0, 0.0).astype(BF16)
    rr = lax.broadcasted_iota(jnp.int32, (hc, hc), 0)
    cc = lax.broadcasted_iota(jnp.int32, (hc, hc), 1)
    same_head = (rr >> 6) == (cc >> 6)
    keep_f = same_head & ((rr & (c - 1)) >= (cc & (c - 1)))
    keep_b = same_head & ((rr & (c - 1)) <= (cc & (c - 1)))
    head_mask = jnp.where(same_head, 1.0, 0.0).astype(BF16)
    norm_g = ng_ref[...]

    def chunk_rows(ci):
        return pl.ds(pl.multiple_of(ci * c, c), c)

    def tile_heads(x):
        x4 = jnp.concatenate([x] * GLA_HEADS, axis=0)
        return jnp.where(same_head, x4, 0.0).astype(BF16)

    def scan_step(i, carry):
        dirs = []
        for u in range(SCAN_UNROLL):
            dirs += [(SCAN_UNROLL * i + u, laf_ref, tri_f, keep_f, c - 1, stf_ref, saf_ref, qtf_ref),
                     (n - 1 - SCAN_UNROLL * i - u, lab_ref, tri_b, keep_b, 0, stb_ref, sab_ref, qtb_ref)]
        cums = []
        for ci, la_ref, tri, _, _, _, _, _ in dirs:
            la_hi, la_lo = _split_hi_lo(la_ref[chunk_rows(ci), :])
            cums.append(_dot(tri, la_hi) + _dot(tri, la_lo))
        ops = []
        for (ci, _, _, _, last_row, _, _, qt_ref), cum in zip(dirs, cums):
            sl = chunk_rows(ci)
            tot = cum[last_row:last_row + 1, :]
            kc = k_ref[sl, :]
            qt = q_ref[sl, :] * (GLA_DK ** -0.5) * jnp.exp(cum)
            qt_ref[sl, :] = qt.astype(BF16)
            v_rows = _heads_to_rows(v_ref[sl, :])
            ops.append((tot, tile_heads(qt), tile_heads(kc * jnp.exp(-cum)),
                        tile_heads(kc * jnp.exp(tot - cum)), v_rows))
        atts = [_dot_nt(q4, k4) for _, q4, k4, _, _ in ops]
        incs = []
        for (_, _, _, keep, _, _, _, _), (_, _, _, kd4, v_rows), att in zip(dirs, ops, atts):
            att = jnp.where(keep, att, 0.0).astype(BF16)
            incs.append((_dot(att, v_rows.astype(BF16)), _dot(v_rows.T.astype(BF16), kd4)))
        for (ci, _, _, _, _, st_ref, snap_ref, _), (tot, _, _, _, _), (o_intra, st_inc) in zip(dirs, ops, incs):
            oacc_ref[ci] += o_intra
            st = st_ref[...]
            snap_ref[ci] = st.astype(BF16)
            st_ref[...] = jnp.exp(tot) * st + st_inc
        return carry

    def tile_heads_bf16(x):
        return jnp.concatenate([x] * GLA_HEADS, axis=0) * head_mask

    def out_step(i, carry):
        chunks = [OUT_UNROLL * i + u for u in range(OUT_UNROLL)]
        inter = []
        for ci in chunks:
            sl = chunk_rows(ci)
            q4 = jnp.concatenate([tile_heads_bf16(qtf_ref[sl, :]), tile_heads_bf16(qtb_ref[sl, :])], axis=1)
            st = jnp.concatenate([saf_ref[ci], sab_ref[ci]], axis=1)
            inter.append(_dot_nt(q4, st))
        for ci, o_inter in zip(chunks, inter):
            sl = chunk_rows(ci)
            on = _rms_norm(oacc_ref[ci] + o_inter, norm_g)
            gate = _silu(_heads_to_rows(g_ref[sl, :]))
            out_ref[sl, :] = _rows_to_heads(on * gate, c)
        return carry

    lax.fori_loop(0, n // SCAN_UNROLL, scan_step, 0)
    lax.fori_loop(0, n // OUT_UNROLL, out_step, 0)
    sf_ref[...] = stf_ref[...].T
    sb_ref[...] = stb_ref[...].T


def _gla(gla_in, lora, waf, baf, wab, bab, norm_g, s0f=None, s0b=None):
    b, t, _ = gla_in.shape
    has_init = s0f is not None
    n = t // GLA_CHUNK
    bmap = lambda i: (i, 0, 0)
    full = lambda i: (0, 0)
    in_specs = [pl.BlockSpec((None, t, GLA_QK), lambda i: (i, 0, 0)),
                pl.BlockSpec((None, t, GLA_QK), lambda i: (i, 0, 1)),
                pl.BlockSpec((None, t, GLA_V), lambda i: (i, 0, 1)),
                pl.BlockSpec((None, t, GLA_V), lambda i: (i, 0, 2)),
                pl.BlockSpec((None, t, 2 * GLA_LORA), bmap),
                pl.BlockSpec((2 * GLA_LORA, GLA_QK), full),
                pl.BlockSpec((1, GLA_QK), full),
                pl.BlockSpec((2 * GLA_LORA, GLA_QK), full),
                pl.BlockSpec((1, GLA_QK), full),
                pl.BlockSpec((1, GLA_DV), full)]
    args = [gla_in, gla_in, gla_in, gla_in, lora, waf, baf, wab, bab, norm_g]
    if has_init:
        in_specs += [pl.BlockSpec((None, GLA_QK, GLA_DV), bmap)] * 2
        args += [s0f, s0b]
    return pl.pallas_call(
        functools.partial(_gla_kernel, has_init),
        out_shape=(jax.ShapeDtypeStruct((b, t, GLA_V), F32),
                   jax.ShapeDtypeStruct((b, GLA_QK, GLA_DV), F32),
                   jax.ShapeDtypeStruct((b, GLA_QK, GLA_DV), F32)),
        grid=(b,),
        in_specs=in_specs,
        out_specs=(pl.BlockSpec((None, t, GLA_V), bmap),
                   pl.BlockSpec((None, GLA_QK, GLA_DV), bmap),
                   pl.BlockSpec((None, GLA_QK, GLA_DV), bmap)),
        scratch_shapes=[pltpu.VMEM((t, GLA_QK), F32),
                        pltpu.VMEM((t, GLA_QK), F32),
                        pltpu.VMEM((n, GLA_HEADS * GLA_CHUNK, GLA_DV), F32),
                        pltpu.VMEM((t, GLA_QK), BF16),
                        pltpu.VMEM((t, GLA_QK), BF16),
                        pltpu.VMEM((n, GLA_DV, GLA_QK), BF16),
                        pltpu.VMEM((n, GLA_DV, GLA_QK), BF16),
                        pltpu.VMEM((GLA_DV, GLA_QK), F32),
                        pltpu.VMEM((GLA_DV, GLA_QK), F32)],
        compiler_params=pltpu.CompilerParams(dimension_semantics=("arbitrary",),
                                             vmem_limit_bytes=VMEM_LIMIT),
        name="gla",
    )(*args)


def _dup_groups(x):
    lo = lax.broadcasted_iota(jnp.int32, x.shape, 1) < SWA_HEAD_DIM
    xr = pltpu.roll(x, SWA_HEAD_DIM, axis=1)
    return jnp.where(lo, x, xr), jnp.where(lo, xr, x)


def _pairs_attention(qps, sinks, k_dups, vt_dups, mask):
    nq = qps[0].shape[0]
    lo = lax.broadcasted_iota(jnp.int32, (nq, LANES), 1) < SWA_HEAD_DIM
    even = lax.broadcasted_iota(jnp.int32, (1, 2 * nq), 1) < nq
    scores = []
    for qp, k_dup in zip(qps, k_dups):
        q2 = jnp.concatenate([jnp.where(lo, qp, 0.0), jnp.where(lo, 0.0, qp)], axis=0).astype(BF16)
        scores.append(_dot_nt(k_dup, q2))
    probs = []
    for s, (sink_even, sink_odd) in zip(scores, sinks):
        if mask is not None:
            s = jnp.where(mask, s, NEG_INF)
        sink = jnp.where(even, sink_even, sink_odd)
        m = jnp.maximum(jnp.max(s, axis=0, keepdims=True), sink)
        p = jnp.exp(s - m)
        denom = jnp.sum(p, axis=0, keepdims=True) + jnp.exp(sink - m)
        probs.append((p.astype(BF16), 1.0 / denom))
    outs = []
    for (p, rdenom), vt_dup in zip(probs, vt_dups):
        o = _dot(vt_dup, p) * rdenom
        outs.append(jnp.concatenate([o[:SWA_HEAD_DIM, :nq], o[SWA_HEAD_DIM:, nq:]], axis=0).T)
    return outs


def _attn_ctx_kernel(sink_ref, q_ref, k_ref, v_ref, o_ref):
    kd = [x.astype(BF16) for x in _dup_groups(k_ref[...])]
    vt = [x.T.astype(BF16) for x in _dup_groups(v_ref[...])]
    scale = SWA_HEAD_DIM ** -0.5
    pairs = range(SWA_HEADS // 2)
    outs = _pairs_attention([q_ref[:, pr * LANES:(pr + 1) * LANES] * scale for pr in pairs],
                            [(sink_ref[2 * pr], sink_ref[2 * pr + 1]) for pr in pairs],
                            [kd[pr // 2] for pr in pairs], [vt[pr // 2] for pr in pairs], None)
    for pr in pairs:
        o_ref[:, pr * LANES:(pr + 1) * LANES] = outs[pr]


def _attn_ctx(sink, q, k, v):
    b, t, _ = q.shape
    bmap = lambda i: (i, 0, 0)
    return pl.pallas_call(
        _attn_ctx_kernel,
        out_shape=jax.ShapeDtypeStruct((b, t, SWA_Q), F32),
        grid=(b,),
        in_specs=[pl.BlockSpec(memory_space=pltpu.SMEM),
                  pl.BlockSpec((None, t, SWA_Q), bmap),
                  pl.BlockSpec((None, t, SWA_KV), bmap),
                  pl.BlockSpec((None, t, SWA_KV), bmap)],
        out_specs=pl.BlockSpec((None, t, SWA_Q), bmap),
        compiler_params=pltpu.CompilerParams(dimension_semantics=("arbitrary",),
                                             vmem_limit_bytes=VMEM_LIMIT),
        name="attn_ctx",
    )(sink, q, k, v)


def _rope(x, cos, sin_lo, sin_hi):
    return x * cos + pltpu.roll(x, LANES - 16, axis=1) * sin_lo + pltpu.roll(x, 16, axis=1) * sin_hi


def _attn_lat_kernel(sink_ref, q_ref, k_ref, v_ref, kc_ref, vc_ref, cos_ref, sl_ref, sh_ref,
                     o_ref, kw_ref, vw_ref):
    t = q_ref.shape[0]
    ab = ATTN_BLOCK
    nb = t // ab
    scale = SWA_HEAD_DIM ** -0.5

    k_rot = _dup_groups(_rope(k_ref[...], cos_ref[...], sl_ref[...], sh_ref[...]))
    v_dup = _dup_groups(v_ref[...])
    zeros = jnp.zeros((ab, LANES), BF16)
    for grp in range(SWA_KV_HEADS):
        kw_ref[grp, 0:ab, :] = zeros
        kw_ref[grp, ab:ab + t, :] = k_rot[grp].astype(BF16)
        kw_ref[grp, ab + t:, :] = zeros
        vw_ref[grp, 0] = zeros
        for blk in range(nb):
            vw_ref[grp, blk + 1] = v_dup[grp][blk * ab:(blk + 1) * ab, :].T.astype(BF16)
        vw_ref[grp, nb + 1] = zeros
    kc = [x.astype(BF16) for x in _dup_groups(kc_ref[...])]
    vct = [x.T.astype(BF16) for x in _dup_groups(vc_ref[...])]
    lc = kc_ref.shape[0]

    key = lax.broadcasted_iota(jnp.int32, (lc + 3 * ab, 2 * ab), 0) - lc
    tq = lax.broadcasted_iota(jnp.int32, (lc + 3 * ab, 2 * ab), 1) & (ab - 1)
    band = (key < 0) | (jnp.abs(tq + ab - key) <= ab)

    def block(nq, carry):
        row0 = pl.multiple_of(nq * ab, ab)
        s_abs = key + (nq - 1) * ab
        mask = band & ((key < 0) | ((s_abs >= 0) & (s_abs < t)))
        cos = cos_ref[pl.ds(row0, ab), :]
        s_lo = sl_ref[pl.ds(row0, ab), :]
        s_hi = sh_ref[pl.ds(row0, ab), :]
        k_all = [jnp.concatenate([kc[grp], kw_ref[grp, pl.ds(row0, 3 * ab), :]], axis=0)
                 for grp in range(SWA_KV_HEADS)]
        vt_all = [jnp.concatenate([vct[grp], vw_ref[grp, nq], vw_ref[grp, nq + 1], vw_ref[grp, nq + 2]],
                                  axis=1) for grp in range(SWA_KV_HEADS)]
        pairs = range(SWA_HEADS // 2)
        qps = [_rope(q_ref[pl.ds(row0, ab), pr * LANES:(pr + 1) * LANES], cos, s_lo, s_hi) * scale
               for pr in pairs]
        outs = _pairs_attention(qps, [(sink_ref[2 * pr], sink_ref[2 * pr + 1]) for pr in pairs],
                                [k_all[pr // 2] for pr in pairs], [vt_all[pr // 2] for pr in pairs], mask)
        for pr in pairs:
            o_ref[pl.ds(row0, ab), pr * LANES:(pr + 1) * LANES] = outs[pr]
        return carry

    lax.fori_loop(0, nb, block, 0)


def _attn_lat(sink, q, k, v, kc, vc, cos, sin_lo, sin_hi):
    b, t, _ = q.shape
    lc = kc.shape[1]
    bmap = lambda i: (i, 0, 0)
    full = lambda i: (0, 0)
    return pl.pallas_call(
        _attn_lat_kernel,
        out_shape=jax.ShapeDtypeStruct((b, t, SWA_Q), F32),
        grid=(b,),
        in_specs=[pl.BlockSpec(memory_space=pltpu.SMEM),
                  pl.BlockSpec((None, t, SWA_Q), bmap),
                  pl.BlockSpec((None, t, SWA_KV), bmap),
                  pl.BlockSpec((None, t, SWA_KV), bmap),
                  pl.BlockSpec((None, lc, SWA_KV), bmap),
                  pl.BlockSpec((None, lc, SWA_KV), bmap),
                  pl.BlockSpec((t, LANES), full),
                  pl.BlockSpec((t, LANES), full),
                  pl.BlockSpec((t, LANES), full)],
        out_specs=pl.BlockSpec((None, t, SWA_Q), bmap),
        scratch_shapes=[pltpu.VMEM((SWA_KV_HEADS, t + 2 * ATTN_BLOCK, LANES), BF16),
                        pltpu.VMEM((SWA_KV_HEADS, t // ATTN_BLOCK + 2, LANES, ATTN_BLOCK), BF16)],
        compiler_params=pltpu.CompilerParams(dimension_semantics=("arbitrary",),
                                             vmem_limit_bytes=VMEM_LIMIT),
        name="attn_lat",
    )(sink, q, k, v, kc, vc, cos, sin_lo, sin_hi)


def _rope_tables(t):
    half = SWA_HEAD_DIM // 2
    quarter = half // 2
    pos = jnp.arange(t)
    row = (pos // GRID_W).astype(F32)
    col = (pos % GRID_W).astype(F32)
    inv_freq = ROPE_BASE ** (-jnp.arange(quarter, dtype=F32) / quarter)
    lane = jnp.arange(LANES)
    d = lane % SWA_HEAD_DIM
    freq = inv_freq[d % quarter]
    use_row = (d < half)
    ang = jnp.where(use_row[None, :], row[:, None], col[:, None]) * freq[None, :]
    cos = jnp.cos(ang)
    sin = jnp.sin(ang)
    lower = (d % half) < quarter
    return cos, jnp.where(lower[None, :], -sin, 0.0), jnp.where(lower[None, :], 0.0, sin)


def _route(sel, scores):
    n = sel.shape[1]
    gsz = N_EXPERTS // N_EXPERT_GROUPS

    def first_max(x, idx, size):
        m = jnp.max(x, axis=0, keepdims=True)
        first = jnp.min(jnp.where(x == m, idx, float(size)), axis=0, keepdims=True)
        return m, idx == first

    i8 = lax.broadcasted_iota(jnp.int32, (gsz, n), 0).astype(F32)
    rows = []
    for g in range(N_EXPERT_GROUPS):
        slab = sel[g * gsz:(g + 1) * gsz, :]
        m1, hit = first_max(slab, i8, gsz)
        m2 = jnp.max(jnp.where(hit, NEG_INF, slab), axis=0, keepdims=True)
        rows.append(m1 + m2)
    gscore = jnp.concatenate(rows, axis=0)
    gsel = jnp.zeros((N_EXPERT_GROUPS, n), F32)
    for _ in range(TOPK_GROUPS):
        _, hit = first_max(gscore, i8, N_EXPERT_GROUPS)
        gsel = jnp.where(hit, 1.0, gsel)
        gscore = jnp.where(hit, NEG_INF, gscore)
    emask = jnp.concatenate(
        [jnp.broadcast_to(gsel[g:g + 1, :], (gsz, n)) for g in range(N_EXPERT_GROUPS)], axis=0)
    cand = jnp.where(emask > 0.5, sel, NEG_INF)
    ie = lax.broadcasted_iota(jnp.int32, (N_EXPERTS, n), 0).astype(F32)
    w = jnp.zeros((N_EXPERTS, n), F32)
    chosen = jnp.zeros((N_EXPERTS, n), F32)
    hits = []
    for _ in range(TOP_K):
        _, hit = first_max(cand, ie, N_EXPERTS)
        hits.append(hit)
        w = jnp.where(hit, scores, w)
        chosen = jnp.where(hit, 1.0, chosen)
        cand = jnp.where(hit, NEG_INF, cand)
    gates = w / jnp.sum(w, axis=0, keepdims=True) * ROUTED_SCALE

    s_idx = lax.broadcasted_iota(jnp.int32, (n, n), 0)
    t_idx = lax.broadcasted_iota(jnp.int32, (n, n), 1)
    before = jnp.where(s_idx < t_idx, 1.0, 0.0).astype(BF16)
    rank = _dot(chosen.astype(BF16), before)
    count = jnp.sum(chosen, axis=1, keepdims=True)
    padded = jnp.floor((count + (SORT_ALIGN - 1)) * (1.0 / SORT_ALIGN)) * SORT_ALIGN
    padded = jnp.broadcast_to(padded, (N_EXPERTS, LANES))
    e_row = lax.broadcasted_iota(jnp.int32, (N_EXPERTS, N_EXPERTS), 0)
    e_col = lax.broadcasted_iota(jnp.int32, (N_EXPERTS, N_EXPERTS), 1)
    below = jnp.where(e_col < e_row, 1.0, 0.0).astype(BF16)
    start = _dot(below, padded.astype(BF16))
    row = start[:, 0:1] + rank
    pos = jnp.concatenate([jnp.sum(jnp.where(h, row, 0.0), axis=0, keepdims=True) for h in hits], axis=0)
    wts = jnp.concatenate([jnp.sum(jnp.where(h, gates, 0.0), axis=0, keepdims=True) for h in hits], axis=0)
    return pos, wts, padded, start


def _outproj_kernel(gla_ref, att_ref, x_ref, wo_ref, g1_ref, sh_ref, sc_ref, ng_ref, rw_ref, rwh_ref,
                    rb_ref, x1_ref, xm_ref, pos_ref, wts_ref, cnt_ref, start_ref):
    y = (_dot(gla_ref[...].astype(BF16), wo_ref[0:GLA_V, :])
         + _dot(att_ref[...].astype(BF16), wo_ref[GLA_V:, :]))
    x1 = x_ref[...] + g1_ref[...] * y
    x1_ref[...] = x1
    xm = _rms_norm(x1, ng_ref[...]) * (1.0 + sc_ref[...]) + sh_ref[...]
    xm_hi, xm_lo = _split_hi_lo(xm)
    xm_ref[...] = xm_hi
    lg = _dot(xm_hi, rw_ref[...])
    logits = lg[:, :N_EXPERTS] + lg[:, N_EXPERTS:] + _dot(xm_lo, rwh_ref[...])
    tm = logits.shape[0]
    lt = jnp.concatenate([logits, jnp.zeros((tm, LANES - N_EXPERTS), F32)], axis=1).T[:N_EXPERTS, :]
    scores = _sigmoid(lt)
    pos_ref[...], wts_ref[...], cnt_ref[...], start_ref[...] = _route(scores + rb_ref[...], scores)


def _outproj(gla_out, att_out, x, w_out, g1, sh2, sc2, norm_g, rw_cat, rw_hi, rbias, *, tm):
    b, t, d = x.shape
    nmod = g1.shape[0]
    mod_map = (lambda i, j: (i, 0, 0)) if nmod > 1 else (lambda i, j: (0, 0, 0))
    row = lambda i, j: (i, j, 0)
    full = lambda i, j: (0, 0)
    tile = lambda i, j: (i, j, 0, 0)
    nt = t // tm
    return pl.pallas_call(
        _outproj_kernel,
        out_shape=(jax.ShapeDtypeStruct((b, t, d), F32),
                   jax.ShapeDtypeStruct((b, t, d), BF16),
                   jax.ShapeDtypeStruct((b, nt, TOP_K, tm), F32),
                   jax.ShapeDtypeStruct((b, nt, TOP_K, tm), F32),
                   jax.ShapeDtypeStruct((b, nt, N_EXPERTS, LANES), F32),
                   jax.ShapeDtypeStruct((b, nt, N_EXPERTS, LANES), F32)),
        grid=(b, t // tm),
        in_specs=[pl.BlockSpec((None, tm, GLA_V), row),
                  pl.BlockSpec((None, tm, SWA_Q), row),
                  pl.BlockSpec((None, tm, d), row),
                  pl.BlockSpec((d, d), full),
                  pl.BlockSpec((None, 1, d), mod_map),
                  pl.BlockSpec((None, 1, d), mod_map),
                  pl.BlockSpec((None, 1, d), mod_map),
                  pl.BlockSpec((1, d), full),
                  pl.BlockSpec((d, 2 * N_EXPERTS), full),
                  pl.BlockSpec((d, N_EXPERTS), full),
                  pl.BlockSpec((N_EXPERTS, 1), full)],
        out_specs=(pl.BlockSpec((None, tm, d), row),
                   pl.BlockSpec((None, tm, d), row),
                   pl.BlockSpec((None, None, TOP_K, tm), tile),
                   pl.BlockSpec((None, None, TOP_K, tm), tile),
                   pl.BlockSpec((None, None, N_EXPERTS, LANES), tile),
                   pl.BlockSpec((None, None, N_EXPERTS, LANES), tile)),
        compiler_params=pltpu.CompilerParams(dimension_semantics=("arbitrary", "arbitrary"),
                                             vmem_limit_bytes=VMEM_LIMIT),
        name="outproj",
    )(gla_out, att_out, x, w_out, g1, sh2, sc2, norm_g, rw_cat, rw_hi, rbias)


MOE_TILE = 256
SORT_ALIGN = 16
SORT_ROWS = 3072
ROW_TILE = 512
GATHER_SLOTS = 3
FFN_CHAINS = 4
COMBINE_CHUNK = 1024
ALWAYS_ROWS = 2560
COMBINE_TAIL = 512


def _moe_sort_kernel(tiles_a, used_ref, xa_ref, xb_ref, pos_ref, xs_ref):
    i = pl.program_id(0)
    x = jnp.where(i < tiles_a, xa_ref[...], xb_ref[...])
    pos = pos_ref[...]
    tm = x.shape[0]
    used = used_ref[i]

    def fill(blk):
        rows = (lax.broadcasted_iota(jnp.int32, (tm, tm), 0) + blk * tm).astype(F32)
        onehot = jnp.zeros((tm, tm), F32)
        for k in range(TOP_K):
            onehot = jnp.where(rows == pos[k:k + 1, :], 1.0, onehot)
        xs_ref[blk * tm:(blk + 1) * tm, :] = _dot(onehot.astype(BF16), x).astype(BF16)

    for blk in range(SORT_ROWS // tm):
        if (blk + 1) * tm <= ALWAYS_ROWS:
            fill(blk)
        else:
            pl.when(blk * tm < used)(functools.partial(fill, blk))

            @pl.when(blk * tm >= used)
            def _():
                xs_ref[blk * tm:(blk + 1) * tm, :] = jnp.zeros((tm, D_MODEL), BF16)


def _moe_sort(xm_a, xm_b, pos, used):
    d = xm_a.shape[1]
    nt, _, tm = pos.shape
    tiles_a = xm_a.shape[0] // tm
    grid_spec = pltpu.PrefetchScalarGridSpec(
        num_scalar_prefetch=1,
        grid=(nt,),
        in_specs=[pl.BlockSpec((tm, d), lambda i, u: (jnp.minimum(i, tiles_a - 1), 0)),
                  pl.BlockSpec((tm, d), lambda i, u: (jnp.maximum(i - tiles_a, 0), 0)),
                  pl.BlockSpec((None, TOP_K, tm), lambda i, u: (i, 0, 0))],
        out_specs=pl.BlockSpec((SORT_ROWS, d), lambda i, u: (i, 0)))
    return pl.pallas_call(
        functools.partial(_moe_sort_kernel, tiles_a),
        out_shape=jax.ShapeDtypeStruct((nt * SORT_ROWS, d), BF16),
        grid_spec=grid_spec,
        compiler_params=pltpu.CompilerParams(dimension_semantics=("arbitrary",),
                                             vmem_limit_bytes=VMEM_LIMIT),
        name="moe_sort",
    )(used, xm_a, xm_b, pos)


def _moe_row_tiles(n_tokens):
    rows = n_tokens * TOP_K + (n_tokens // MOE_TILE) * N_EXPERTS * (SORT_ALIGN - 1) + N_EXPERTS * (ROW_TILE - 1)
    return -(-rows // ROW_TILE) + GATHER_SLOTS - 1


PLAN_CHUNK = 1280


def _int_dot_r(a, onehot):
    hi = jnp.floor(a * (1.0 / 256.0))
    return _dot(hi.astype(BF16), onehot) * 256.0 + _dot((a - hi * 256.0).astype(BF16), onehot)


def _int_dot_l(onehot, b):
    hi = jnp.floor(b * (1.0 / 256.0))
    return _dot(onehot, hi.astype(BF16)) * 256.0 + _dot(onehot, (b - hi * 256.0).astype(BF16))


def _moe_plan_kernel(cnt_ref, start_ref, src_ref, te_ref, nu_ref, back_ref):
    nt, ne = cnt_ref.shape
    gpt = SORT_ROWS // SORT_ALIGN
    gpr = ROW_TILE // SORT_ALIGN
    gc = cnt_ref[...] * (1.0 / SORT_ALIGN)
    ls = start_ref[...] * (1.0 / SORT_ALIGN)

    def transpose(x):
        x = jnp.concatenate([x, jnp.zeros((nt, LANES - ne), F32)], axis=1)
        x = jnp.concatenate([x, jnp.zeros((LANES - nt, LANES), F32)], axis=0)
        return x.T[:ne, :nt]

    def tri(n, keep):
        return jnp.where(keep(lax.broadcasted_iota(jnp.int32, (n, n), 0),
                              lax.broadcasted_iota(jnp.int32, (n, n), 1)), 1.0, 0.0).astype(BF16)

    gc_t = transpose(gc)
    ls_t = transpose(ls)
    tot_c = jnp.broadcast_to(jnp.sum(gc_t, axis=1, keepdims=True), (ne, LANES))
    ptot_c = jnp.floor((tot_c + (gpr - 1)) * (1.0 / gpr)) * gpr
    gend_c = _int_dot_l(tri(ne, lambda r, c: c <= r), ptot_c)
    gstart_c = gend_c - ptot_c
    n_used = gend_c[ne - 1:ne, :] * (1.0 / gpr)
    nu_ref[...] = n_used.astype(jnp.int32)
    tot_r = jnp.sum(gc, axis=0, keepdims=True)
    ptot_r = jnp.floor((tot_r + (gpr - 1)) * (1.0 / gpr)) * gpr
    gstart_r = _int_dot_r(jnp.broadcast_to(ptot_r, (8, ne)), tri(ne, lambda r, c: r < c))
    cumex = _dot(tri(nt, lambda r, c: c < r), gc.astype(BF16))
    cumex_t = _dot(gc_t.astype(BF16), tri(nt, lambda r, c: r < c))
    tile_base = lax.broadcasted_iota(jnp.int32, (nt, ne), 0).astype(F32) * gpt + ls
    table = jnp.concatenate([cumex + gc, cumex, tile_base, gstart_r, jnp.broadcast_to(tot_r, (8, ne))], axis=0)

    e_iota = lax.broadcasted_iota(jnp.int32, (ne, PLAN_CHUNK), 0).astype(F32)
    for ch in range(src_ref.shape[1] // PLAN_CHUNK):
        g = (lax.broadcasted_iota(jnp.int32, (1, PLAN_CHUNK), 1) + ch * PLAN_CHUNK).astype(F32)
        eg = jnp.sum(jnp.where(gend_c[:, 0:1] <= g, 1.0, 0.0), axis=0, keepdims=True)
        picked = _int_dot_r(table, jnp.where(e_iota == eg, 1.0, 0.0).astype(BF16))
        cum_g, cumex_g, base_g = picked[0:nt], picked[nt:2 * nt], picked[2 * nt:3 * nt]
        u = g - picked[3 * nt:3 * nt + 1]
        in_tile = (cumex_g <= u) & (u < cum_g)
        src = jnp.sum(jnp.where(in_tile, base_g - cumex_g, 0.0), axis=0, keepdims=True) + u
        src = jnp.where(u < picked[3 * nt + 8:3 * nt + 9], src, gpt - 1.0)
        src_ref[:, ch * PLAN_CHUNK:(ch + 1) * PLAN_CHUNK] = src.astype(jnp.int32)

    r = lax.broadcasted_iota(jnp.int32, (1, te_ref.shape[1]), 1).astype(F32)
    r = jnp.minimum(r, n_used[:, 0:1] - 1.0)
    te_ref[...] = jnp.sum(jnp.where(gend_c[:, 0:1] <= r * gpr, 1.0, 0.0), axis=0,
                          keepdims=True).astype(jnp.int32)

    lg = lax.broadcasted_iota(jnp.int32, (ne, back_ref.shape[1]), 1).astype(F32)
    for t in range(nt):
        first = ls_t[:, t:t + 1]
        inside = (first <= lg) & (lg < first + gc_t[:, t:t + 1])
        shift = gstart_c[:, 0:1] + cumex_t[:, t:t + 1] - first
        val = jnp.sum(jnp.where(inside, shift + lg, 0.0), axis=0, keepdims=True)
        back_ref[t:t + 1, :] = val.astype(jnp.int32)


def _moe_plan(cnt, start):
    nt, ne = cnt.shape
    row_tiles = _moe_row_tiles(nt * MOE_TILE)
    gpt = SORT_ROWS // SORT_ALIGN
    gpr = ROW_TILE // SORT_ALIGN
    n_src = -(-(row_tiles * gpr) // PLAN_CHUNK) * PLAN_CHUNK
    n_te = -(-row_tiles // LANES) * LANES
    n_back = -(-gpt // LANES) * LANES
    src, te, nu, back = pl.pallas_call(
        _moe_plan_kernel,
        out_shape=(jax.ShapeDtypeStruct((1, n_src), jnp.int32),
                   jax.ShapeDtypeStruct((1, n_te), jnp.int32),
                   jax.ShapeDtypeStruct((1, LANES), jnp.int32),
                   jax.ShapeDtypeStruct((nt, n_back), jnp.int32)),
        compiler_params=pltpu.CompilerParams(vmem_limit_bytes=VMEM_LIMIT),
        name="moe_plan",
    )(cnt, start)
    return nu[0, :1], te[0, :row_tiles], src[0, :row_tiles * gpr], back[:, :gpt]


def _moe_experts_kernel(nu_ref, te_ref, src_ref, xs_hbm, wg_ref, wu_ref, wd_ref, ys_ref, xbuf, sem, wgu_s, wd_s):
    r = pl.program_id(0)
    n_used = nu_ref[0]
    gpr = ROW_TILE // SORT_ALIGN
    slot = lax.rem(r, GATHER_SLOTS)

    def gather(tile, to_slot):
        for j in range(gpr):
            row = pl.multiple_of(src_ref[tile * gpr + j] * SORT_ALIGN, SORT_ALIGN)
            pltpu.make_async_copy(xs_hbm.at[pl.ds(row, SORT_ALIGN), :],
                                  xbuf.at[to_slot, j * SORT_ALIGN:(j + 1) * SORT_ALIGN, :], sem.at[to_slot]).start()

    def drain(of_slot):
        for j in range(gpr):
            pltpu.make_async_copy(xs_hbm.at[0:SORT_ALIGN, :],
                                  xbuf.at[of_slot, j * SORT_ALIGN:(j + 1) * SORT_ALIGN, :], sem.at[of_slot]).wait()

    @pl.when(r == 0)
    def _():
        gather(0, 0)
        gather(1, 1)

    @pl.when((r == 0) | (te_ref[r] != te_ref[jnp.maximum(r - 1, 0)]))
    def _():
        wgu_s[:, :EXPERT_FF] = wg_ref[...].astype(BF16)
        wgu_s[:, EXPERT_FF:] = wu_ref[...].astype(BF16)
        wd_s[...] = wd_ref[...].astype(BF16)

    @pl.when((r >= n_used) & (r < n_used + 2))
    def _():
        drain(slot)

    @pl.when(r < n_used)
    def _():
        drain(slot)
        gather(r + 2, lax.rem(r + 2, GATHER_SLOTS))
        part = ROW_TILE // FFN_CHAINS
        xs = [xbuf[slot, c * part:(c + 1) * part, :] for c in range(FFN_CHAINS)]
        abs_ = [_dot(x, wgu_s[...]) for x in xs]
        hs = [(_silu(ab[:, :EXPERT_FF]) * ab[:, EXPERT_FF:]).astype(BF16) for ab in abs_]
        ys = [_dot(h, wd_s[...]).astype(BF16) for h in hs]
        for c in range(FFN_CHAINS):
            ys_ref[c * part:(c + 1) * part, :] = ys[c]


def _moe_experts(n_used, tile_expert, src, xs, wg, wu, wd):
    d = xs.shape[-1]
    row_tiles = tile_expert.shape[0]
    out_map = lambda r, nu, te, sr: (jnp.minimum(r, nu[0] - 1), 0)
    w_map = lambda r, nu, te, sr: (te[r], 0, 0)
    grid_spec = pltpu.PrefetchScalarGridSpec(
        num_scalar_prefetch=3,
        grid=(row_tiles,),
        in_specs=[pl.BlockSpec(memory_space=pl.ANY),
                  pl.BlockSpec((None, d, EXPERT_FF), w_map),
                  pl.BlockSpec((None, d, EXPERT_FF), w_map),
                  pl.BlockSpec((None, EXPERT_FF, d), w_map)],
        out_specs=pl.BlockSpec((ROW_TILE, d), out_map),
        scratch_shapes=[pltpu.VMEM((GATHER_SLOTS, ROW_TILE, d), BF16),
                        pltpu.SemaphoreType.DMA((GATHER_SLOTS,)),
                        pltpu.VMEM((d, 2 * EXPERT_FF), BF16),
                        pltpu.VMEM((EXPERT_FF, d), BF16)])
    return pl.pallas_call(
        _moe_experts_kernel,
        out_shape=jax.ShapeDtypeStruct((row_tiles * ROW_TILE, d), BF16),
        grid_spec=grid_spec,
        compiler_params=pltpu.CompilerParams(dimension_semantics=("arbitrary",),
                                             vmem_limit_bytes=VMEM_LIMIT),
        name="moe_experts",
    )(n_used, tile_expert, src, xs, wg, wu, wd)


def _moe_combine_kernel(back_ref, used_ref, ys_hbm, pos_ref, wts_ref, xm_ref, x1_ref, g2_ref, fg_ref,
                        swg_ref, swu_ref, swd_ref, o_ref, buf, sem, acc_ref):
    i = pl.program_id(0)
    gpt = SORT_ROWS // SORT_ALIGN
    slot = lax.rem(i, 2)
    always = ALWAYS_ROWS
    tail = range(always, SORT_ROWS, COMBINE_TAIL)

    def copies(tile, of_slot, g0, g1, start):
        for g in range(g0, g1):
            row = pl.multiple_of(back_ref[tile * gpt + g] * SORT_ALIGN, SORT_ALIGN) if start else 0
            cp = pltpu.make_async_copy(ys_hbm.at[pl.ds(row, SORT_ALIGN), :],
                                       buf.at[of_slot, g * SORT_ALIGN:(g + 1) * SORT_ALIGN, :], sem.at[of_slot])
            if start:
                cp.start()
            else:
                cp.wait()

    def transfer(tile, of_slot, start):
        copies(tile, of_slot, 0, always // SORT_ALIGN, start)
        for c0 in tail:
            pl.when(c0 < used_ref[tile])(functools.partial(
                copies, tile, of_slot, c0 // SORT_ALIGN, (c0 + COMBINE_TAIL) // SORT_ALIGN, start))

    @pl.when(i == 0)
    def _():
        transfer(0, 0, True)

    @pl.when(i + 1 < pl.num_programs(0))
    def _():
        transfer(i + 1, 1 - slot, True)

    x = xm_ref[...]
    tm = x.shape[0]
    shared = _dot((_silu(_dot(x, swg_ref[...])) * _dot(x, swu_ref[...])).astype(BF16), swd_ref[...])
    pad = jnp.zeros((LANES - TOP_K, tm), F32)
    pos_t = jnp.concatenate([pos_ref[...], pad], axis=0).T
    wts_t = jnp.concatenate([wts_ref[...], pad], axis=0).T
    transfer(i, slot, False)

    pos_b = [jnp.broadcast_to(pos_t[:, k:k + 1], (tm, LANES)) for k in range(TOP_K)]
    wts_b = [jnp.broadcast_to(wts_t[:, k:k + 1], (tm, LANES)) for k in range(TOP_K)]

    def apply(c0, width):
        reps = width // LANES
        rows = (lax.broadcasted_iota(jnp.int32, (tm, width), 1) + c0).astype(F32)
        comb = jnp.zeros((tm, width), F32)
        for k in range(TOP_K):
            comb = jnp.where(rows == jnp.concatenate([pos_b[k]] * reps, axis=1),
                             jnp.concatenate([wts_b[k]] * reps, axis=1), comb)
        return _dot(comb.astype(BF16), buf[slot, c0:c0 + width, :])

    routed = shared
    for c0 in range(0, always, COMBINE_CHUNK):
        routed = routed + apply(c0, min(COMBINE_CHUNK, always - c0))
    acc_ref[...] = routed
    for c0 in tail:
        @pl.when(c0 < used_ref[i])
        def _(c0=c0):
            acc_ref[...] += apply(c0, COMBINE_TAIL)
    y = x1_ref[...] + g2_ref[...] * acc_ref[...]
    o_ref[...] = _rms_norm(y, fg_ref[...])


def _moe_combine(back, used, ys, pos, wts, xm, x1, g2, final_g, swg, swu, swd, *, tiles_per_mod):
    n, d = xm.shape
    tm = pos.shape[-1]
    nt = n // tm
    gpt = SORT_ROWS // SORT_ALIGN
    row = lambda i, bk, us: (i, 0)
    full = lambda i, bk, us: (0, 0)
    tile = lambda i, bk, us: (i, 0, 0)
    mod_map = lambda i, bk, us: (i // tiles_per_mod, 0, 0)
    grid_spec = pltpu.PrefetchScalarGridSpec(
        num_scalar_prefetch=2,
        grid=(nt,),
        in_specs=[pl.BlockSpec(memory_space=pl.ANY),
                  pl.BlockSpec((None, TOP_K, tm), tile),
                  pl.BlockSpec((None, TOP_K, tm), tile),
                  pl.BlockSpec((tm, d), row),
                  pl.BlockSpec((tm, d), row),
                  pl.BlockSpec((None, 1, d), mod_map),
                  pl.BlockSpec((1, d), full),
                  pl.BlockSpec((d, SHARED_FF), full),
                  pl.BlockSpec((d, SHARED_FF), full),
                  pl.BlockSpec((SHARED_FF, d), full)],
        out_specs=pl.BlockSpec((tm, d), row),
        scratch_shapes=[pltpu.VMEM((2, SORT_ROWS, d), BF16),
                        pltpu.SemaphoreType.DMA((2,)),
                        pltpu.VMEM((tm, d), F32)])
    return pl.pallas_call(
        _moe_combine_kernel,
        out_shape=jax.ShapeDtypeStruct((n, d), F32),
        grid_spec=grid_spec,
        compiler_params=pltpu.CompilerParams(dimension_semantics=("arbitrary",),
                                             vmem_limit_bytes=VMEM_LIMIT),
        name="moe_combine",
    )(back, used, ys, pos.reshape(nt, TOP_K, tm), wts.reshape(nt, TOP_K, tm), xm, x1, g2, final_g, swg, swu, swd)


def _mix(x, mods, p, attn_fn, s0=None):
    sh1, sc1, g1, sh2, sc2, _ = mods
    gla_in, lora, q_s, k_s, v_s = _inproj(x, p["norm_attn_g"], sh1, sc1, p["w_gla"], p["w_lora"],
                                          p["w_swa"], tm=256)
    if s0 is None:
        gla_out, s_f, s_b = _gla(gla_in, lora, p["waf"], p["baf"], p["wab"], p["bab"], p["gla_norm_g"])
    else:
        gla_out, s_f, s_b = _gla(gla_in, lora, p["waf"], p["baf"], p["wab"], p["bab"], p["gla_norm_g"],
                                 s0[0], s0[1])
    att_out = attn_fn(q_s, k_s, v_s)
    routed = _outproj(gla_out, att_out, x, p["w_out"], g1, sh2, sc2, p["norm_ffn_g"],
                      p["rw_cat"], p["rw_hi"], p["rbias"], tm=MOE_TILE)
    return routed, k_s, v_s, s_f, s_b


def _moe(streams, p):
    d = D_MODEL
    (ra, _), (rb, _) = streams
    n_tiles = [r[1].shape[0] * r[1].shape[1] // MOE_TILE for r, _ in streams]
    pos_all = jnp.concatenate([r[2].reshape(-1, TOP_K, MOE_TILE) for r, _ in streams], axis=0)
    cnt_all = jnp.concatenate([r[4][..., 0].reshape(-1, N_EXPERTS) for r, _ in streams], axis=0)
    start_all = jnp.concatenate([r[5][..., 0].reshape(-1, N_EXPERTS) for r, _ in streams], axis=0)
    used = (start_all[:, -1] + cnt_all[:, -1]).astype(jnp.int32)
    xs = _moe_sort(ra[1].reshape(-1, d), rb[1].reshape(-1, d), pos_all, used)
    n_used, tile_expert, src, back = _moe_plan(cnt_all, start_all)
    ys = _moe_experts(n_used, tile_expert, src, xs, p["wg"], p["wu"], p["wd"])
    outs = []
    tile0 = 0
    for ((x1, xm, pos, wts, cnt, start), g2), nt in zip(streams, n_tiles):
        b, t, _ = x1.shape
        tiles_per_mod = (t // MOE_TILE) if g2.shape[0] > 1 else nt
        y = _moe_combine(back[tile0:tile0 + nt].reshape(-1), used[tile0:tile0 + nt], ys, pos, wts,
                         xm.reshape(-1, d), x1.reshape(-1, d), g2, p["final_norm_g"],
                         p["swg"], p["swu"], p["swd"], tiles_per_mod=tiles_per_mod)
        outs.append(y.reshape(b, t, d))
        tile0 += nt
    return outs


def kernel(x_prompt, x_sample, c, cache_swa_k, cache_swa_v, state_gla_fwd, state_gla_bwd, c_ctx, w_ada, b_ada, norm_attn_g, norm_ffn_g, w_in, gla_wa_f, gla_ba_f, gla_wa_b, gla_ba_b, gla_norm_g, swa_sink, w_out, router_w, router_bias, exp_w_gate, exp_w_up, exp_w_down, sh_w_gate, sh_w_up, sh_w_down, final_norm_g):
    l = 0
    d = D_MODEL
    nb_ctx, t_ctx, _ = x_prompt.shape
    nb_lat, t_lat, _ = x_sample.shape

    pad = jnp.zeros((8 - 1 - nb_lat, d), F32)
    cond8 = jnp.concatenate([c_ctx[None, :], c, pad], axis=0)
    mod = _adaln(cond8, w_ada[l], b_ada[l][None, :])
    mods_ctx = [mod[0:1, i * d:(i + 1) * d][:, None, :] for i in range(6)]
    mods_lat = [mod[1:1 + nb_lat, i * d:(i + 1) * d][:, None, :] for i in range(6)]

    zeros_lora = jnp.zeros((GLA_LORA, GLA_QK), F32)
    rw = router_w[l]
    rw_hi = rw.astype(BF16)
    rw_lo = (rw - rw_hi.astype(F32)).astype(BF16)
    w_in_b = w_in[l].astype(BF16)
    p = {
        "norm_attn_g": norm_attn_g[l][None, :],
        "norm_ffn_g": norm_ffn_g[l][None, :],
        "final_norm_g": final_norm_g[None, :],
        "w_gla": w_in_b[:, :2 * GLA_QK + 2 * GLA_V],
        "w_lora": w_in_b[:, 2 * GLA_QK + 2 * GLA_V:2 * GLA_QK + 2 * GLA_V + 2 * GLA_LORA],
        "w_swa": w_in_b[:, 2 * GLA_QK + 2 * GLA_V + 2 * GLA_LORA:],
        "waf": jnp.concatenate([gla_wa_f[l], zeros_lora], axis=0).astype(BF16),
        "wab": jnp.concatenate([zeros_lora, gla_wa_b[l]], axis=0).astype(BF16),
        "baf": gla_ba_f[l][None, :],
        "bab": gla_ba_b[l][None, :],
        "gla_norm_g": gla_norm_g[l][None, :],
        "w_out": w_out[l].astype(BF16),
        "rw_cat": jnp.concatenate([rw_hi, rw_lo], axis=1),
        "rw_hi": rw_hi,
        "rbias": router_bias[l][:, None],
        "wg": exp_w_gate[l], "wu": exp_w_up[l], "wd": exp_w_down[l],
        "swg": sh_w_gate[l].astype(BF16), "swu": sh_w_up[l].astype(BF16),
        "swd": sh_w_down[l].astype(BF16),
    }
    sink = swa_sink[l]

    routed_ctx, k_c, v_c, s_f, s_b = _mix(x_prompt, mods_ctx, p, functools.partial(_attn_ctx, sink))

    cos, sin_lo, sin_hi = _rope_tables(t_lat)
    kc = cache_swa_k[:, l].reshape(nb_lat, -1, SWA_KV)
    vc = cache_swa_v[:, l].reshape(nb_lat, -1, SWA_KV)
    lat_attn = lambda q, k, v: _attn_lat(sink, q, k, v, kc, vc, cos, sin_lo, sin_hi)
    s0 = (state_gla_fwd[:, l].reshape(nb_lat, GLA_QK, GLA_DV),
          state_gla_bwd[:, l].reshape(nb_lat, GLA_QK, GLA_DV))
    routed_lat, _, _, _, _ = _mix(x_sample, mods_lat, p, lat_attn, s0)
    y_prompt, y_sample = _moe([(routed_ctx, mods_ctx[5]), (routed_lat, mods_lat[5])], p)

    new_k = k_c.reshape(nb_ctx, 1, t_ctx, SWA_KV_HEADS, SWA_HEAD_DIM)
    new_v = v_c.reshape(nb_ctx, 1, t_ctx, SWA_KV_HEADS, SWA_HEAD_DIM)
    new_sf = s_f.reshape(nb_ctx, 1, GLA_HEADS, GLA_DK, GLA_DV)
    new_sb = s_b.reshape(nb_ctx, 1, GLA_HEADS, GLA_DK, GLA_DV)
    return (y_prompt, y_sample, new_k, new_v, new_sf, new_sb)
```

```python
import functools

import jax
import jax.numpy as jnp
from jax import lax
from jax.experimental import pallas as pl
from jax.experimental.pallas import tpu as pltpu

F32 = jnp.float32
BF16 = jnp.bfloat16

D_MODEL = 1024
GLA_HEADS = 4
GLA_DK = 64
GLA_DV = 128
GLA_LORA = 16
GLA_GATE_NORM = 16.0
GLA_CHUNK = 64
GLA_QK = GLA_HEADS * GLA_DK
GLA_V = GLA_HEADS * GLA_DV
SWA_HEAD_DIM = 64
SWA_HEADS = 8
SWA_KV_HEADS = 2
SWA_Q = SWA_HEADS * SWA_HEAD_DIM
SWA_KV = SWA_KV_HEADS * SWA_HEAD_DIM
ATTN_BLOCK = 128
GRID_W = 64
ROPE_BASE = 10000.0
N_EXPERTS = 64
TOP_K = 8
N_EXPERT_GROUPS = 8
TOPK_GROUPS = 4
EXPERT_FF = 128
SHARED_FF = 256
ROUTED_SCALE = 2.5
EPS = 1e-6

LANES = 128
VMEM_LIMIT = 56 * 1024 * 1024

NEG_INF = float("-inf")


def _dot(a, b):
    return jnp.dot(a, b, preferred_element_type=F32)


def _dot_nt(a, b):
    return lax.dot_general(a, b, (((1,), (1,)), ((), ())), preferred_element_type=F32)


def _split_hi_lo(x):
    hi = x.astype(BF16)
    lo = (x - hi.astype(F32)).astype(BF16)
    return hi, lo


def _sigmoid(x):
    return 1.0 / (1.0 + jnp.exp(-x))


def _silu(x):
    return x * _sigmoid(x)


def _rms_norm(x, g):
    ms = jnp.mean(x * x, axis=-1, keepdims=True)
    return x * lax.rsqrt(ms + EPS) * g


def _adaln_kernel(c_ref, w_ref, b_ref, o_ref):
    a_hi, a_lo = _split_hi_lo(_silu(c_ref[...]))
    w_hi, w_lo = _split_hi_lo(w_ref[...])
    o_ref[...] = _dot(a_hi, w_hi) + _dot(a_lo, w_hi) + _dot(a_hi, w_lo) + b_ref[...]


def _adaln(cond8, w_ada, b_ada):
    n = w_ada.shape[1]
    tn = 1536
    return pl.pallas_call(
        _adaln_kernel,
        out_shape=jax.ShapeDtypeStruct((8, n), F32),
        grid=(n // tn,),
        in_specs=[pl.BlockSpec((8, D_MODEL), lambda j: (0, 0)),
                  pl.BlockSpec((D_MODEL, tn), lambda j: (0, j)),
                  pl.BlockSpec((1, tn), lambda j: (0, j))],
        out_specs=pl.BlockSpec((8, tn), lambda j: (0, j)),
        compiler_params=pltpu.CompilerParams(dimension_semantics=("arbitrary",),
                                             vmem_limit_bytes=VMEM_LIMIT),
        name="adaln",
    )(cond8, w_ada, b_ada)


def _inproj_kernel(x_ref, g_ref, sh_ref, sc_ref, wg_ref, wl_ref, ws_ref,
                   gla_ref, lora_ref, q_ref, k_ref, v_ref):
    h = _rms_norm(x_ref[...], g_ref[...]) * (1.0 + sc_ref[...]) + sh_ref[...]
    hb = h.astype(BF16)
    gla_ref[...] = _dot(hb, wg_ref[...])
    lora_ref[...] = _dot(hb, wl_ref[...])
    s = _dot(hb, ws_ref[...])
    q_ref[...] = s[:, :SWA_Q]
    k_ref[...] = s[:, SWA_Q:SWA_Q + SWA_KV]
    v_ref[...] = s[:, SWA_Q + SWA_KV:]


def _inproj(x, g, sh, sc, w_gla, w_lora, w_swa, *, tm):
    b, t, d = x.shape
    nmod = sh.shape[0]
    mod_map = (lambda i, j: (i, 0, 0)) if nmod > 1 else (lambda i, j: (0, 0, 0))
    row = lambda i, j: (i, j, 0)
    full = lambda i, j: (0, 0)
    n_gla = w_gla.shape[1]
    n_lora = w_lora.shape[1]
    return pl.pallas_call(
        _inproj_kernel,
        out_shape=(jax.ShapeDtypeStruct((b, t, n_gla), F32),
                   jax.ShapeDtypeStruct((b, t, n_lora), F32),
                   jax.ShapeDtypeStruct((b, t, SWA_Q), F32),
                   jax.ShapeDtypeStruct((b, t, SWA_KV), F32),
                   jax.ShapeDtypeStruct((b, t, SWA_KV), F32)),
        grid=(b, t // tm),
        in_specs=[pl.BlockSpec((None, tm, d), row),
                  pl.BlockSpec((1, d), full),
                  pl.BlockSpec((None, 1, d), mod_map),
                  pl.BlockSpec((None, 1, d), mod_map),
                  pl.BlockSpec((d, n_gla), full),
                  pl.BlockSpec((d, n_lora), full),
                  pl.BlockSpec((d, w_swa.shape[1]), full)],
        out_specs=(pl.BlockSpec((None, tm, n_gla), row),
                   pl.BlockSpec((None, tm, n_lora), row),
                   pl.BlockSpec((None, tm, SWA_Q), row),
                   pl.BlockSpec((None, tm, SWA_KV), row),
                   pl.BlockSpec((None, tm, SWA_KV), row)),
        compiler_params=pltpu.CompilerParams(dimension_semantics=("arbitrary", "arbitrary"),
                                             vmem_limit_bytes=VMEM_LIMIT),
        name="inproj",
    )(x, g, sh, sc, w_gla, w_lora, w_swa)


SCAN_UNROLL = 2
OUT_UNROLL = 4


def _log_sigmoid(x):
    return jnp.minimum(x, 0.0) - jnp.log(1.0 + jnp.exp(-jnp.abs(x)))


def _heads_to_rows(x):
    return jnp.concatenate([x[:, h * LANES:(h + 1) * LANES] for h in range(GLA_HEADS)], axis=0)


def _rows_to_heads(x, c):
    return jnp.concatenate([x[h * c:(h + 1) * c, :] for h in range(GLA_HEADS)], axis=1)


def _gla_kernel(has_init, q_ref, k_ref, v_ref, g_ref, lora_ref, waf_ref, baf_ref, wab_ref, bab_ref,
                ng_ref, *rest):
    if has_init:
        s0f_ref, s0b_ref, *rest = rest
    (out_ref, sf_ref, sb_ref, laf_ref, lab_ref, oacc_ref, qtf_ref, qtb_ref, saf_ref, sab_ref,
     stf_ref, stb_ref) = rest
    t = q_ref.shape[0]
    c = GLA_CHUNK
    n = t // c
    hc = GLA_HEADS * c

    lora = lora_ref[...].astype(BF16)
    laf_ref[...] = _log_sigmoid(_dot(lora, waf_ref[...]) + baf_ref[...]) * (1.0 / GLA_GATE_NORM)
    lab_ref[...] = _log_sigmoid(_dot(lora, wab_ref[...]) + bab_ref[...]) * (1.0 / GLA_GATE_NORM)

    if has_init:
        stf_ref[...] = s0f_ref[...].T
        stb_ref[...] = s0b_ref[...].T
    else:
        stf_ref[...] = jnp.zeros_like(stf_ref)
        stb_ref[...] = jnp.zeros_like(stb_ref)
    oacc_ref[...] = jnp.zeros_like(oacc_ref)

    r64 = lax.broadcasted_iota(jnp.int32, (c, c), 0)
    c64 = lax.broadcasted_iota(jnp.int32, (c, c), 1)
    tri_f = jnp.where(c64 <= r64, 1.0, 0.0).astype(BF16)
    tri_b = jnp.where(c64 >= r64, 1.0, 0.0).astype(BF16)
    rr = lax.broadcasted_iota(jnp.int32, (hc, hc), 0)
    cc = lax.broadcasted_iota(jnp.int32, (hc, hc), 1)
    same_head = (rr >> 6) == (cc >> 6)
    keep_f = same_head & ((rr & (c - 1)) >= (cc & (c - 1)))
    keep_b = same_head & ((rr & (c - 1)) <= (cc & (c - 1)))
    head_mask = jnp.where(same_head, 1.0, 0.0).astype(BF16)
    norm_g = ng_ref[...]

    def chunk_rows(ci):
        return pl.ds(pl.multiple_of(ci * c, c), c)

    def tile_heads(x):
        x4 = jnp.concatenate([x] * GLA_HEADS, axis=0)
        return jnp.where(same_head, x4, 0.0).astype(BF16)

    def scan_step(i, carry):
        dirs = []
        for u in range(SCAN_UNROLL):
            dirs += [(SCAN_UNROLL * i + u, laf_ref, tri_f, keep_f, c - 1, stf_ref, saf_ref, qtf_ref),
                     (n - 1 - SCAN_UNROLL * i - u, lab_ref, tri_b, keep_b, 0, stb_ref, sab_ref, qtb_ref)]
        cums = []
        for ci, la_ref, tri, _, _, _, _, _ in dirs:
            la_hi, la_lo = _split_hi_lo(la_ref[chunk_rows(ci), :])
            cums.append(_dot(tri, la_hi) + _dot(tri, la_lo))
        ops = []
        for (ci, _, _, _, last_row, _, _, qt_ref), cum in zip(dirs, cums):
            sl = chunk_rows(ci)
            tot = cum[last_row:last_row + 1, :]
            kc = k_ref[sl, :]
            qt = q_ref[sl, :] * (GLA_DK ** -0.5) * jnp.exp(cum)
            qt_ref[sl, :] = qt.astype(BF16)
            v_rows = _heads_to_rows(v_ref[sl, :])
            ops.append((tot, tile_heads(qt), tile_heads(kc * jnp.exp(-cum)),
                        tile_heads(kc * jnp.exp(tot - cum)), v_rows))
        atts = [_dot_nt(q4, k4) for _, q4, k4, _, _ in ops]
        incs = []
        for (_, _, _, keep, _, _, _, _), (_, _, _, kd4, v_rows), att in zip(dirs, ops, atts):
            att = jnp.where(keep, att, 0.0).astype(BF16)
            incs.append((_dot(att, v_rows.astype(BF16)), _dot(v_rows.T.astype(BF16), kd4)))
        for (ci, _, _, _, _, st_ref, snap_ref, _), (tot, _, _, _, _), (o_intra, st_inc) in zip(dirs, ops, incs):
            oacc_ref[ci] += o_intra
            st = st_ref[...]
            snap_ref[ci] = st.astype(BF16)
            st_ref[...] = jnp.exp(tot) * st + st_inc
        return carry

    def tile_heads_bf16(x):
        return jnp.concatenate([x] * GLA_HEADS, axis=0) * head_mask

    def out_step(i, carry):
        chunks = [OUT_UNROLL * i + u for u in range(OUT_UNROLL)]
        inter = []
        for ci in chunks:
            sl = chunk_rows(ci)
            q4 = jnp.concatenate([tile_heads_bf16(qtf_ref[sl, :]), tile_heads_bf16(qtb_ref[sl, :])], axis=1)
            st = jnp.concatenate([saf_ref[ci], sab_ref[ci]], axis=1)
            inter.append(_dot_nt(q4, st))
        for ci, o_inter in zip(chunks, inter):
            sl = chunk_rows(ci)
            on = _rms_norm(oacc_ref[ci] + o_inter, norm_g)
            gate = _silu(_heads_to_rows(g_ref[sl, :]))
            out_ref[sl, :] = _rows_to_heads(on * gate, c)
        return carry

    lax.fori_loop(0, n // SCAN_UNROLL, scan_step, 0)
    lax.fori_loop(0, n // OUT_UNROLL, out_step, 0)
    sf_ref[...] = stf_ref[...].T
    sb_ref[...] = stb_ref[...].T


def _gla(gla_in, lora, waf, baf, wab, bab, norm_g, s0f=None, s0b=None):
    b, t, _ = gla_in.shape
    has_init = s0f is not None
    n = t // GLA_CHUNK
    bmap = lambda i: (i, 0, 0)
    full = lambda i: (0, 0)
    in_specs = [pl.BlockSpec((None, t, GLA_QK), lambda i: (i, 0, 0)),
                pl.BlockSpec((None, t, GLA_QK), lambda i: (i, 0, 1)),
                pl.BlockSpec((None, t, GLA_V), lambda i: (i, 0, 1)),
                pl.BlockSpec((None, t, GLA_V), lambda i: (i, 0, 2)),
                pl.BlockSpec((None, t, 2 * GLA_LORA), bmap),
                pl.BlockSpec((2 * GLA_LORA, GLA_QK), full),
                pl.BlockSpec((1, GLA_QK), full),
                pl.BlockSpec((2 * GLA_LORA, GLA_QK), full),
                pl.BlockSpec((1, GLA_QK), full),
                pl.BlockSpec((1, GLA_DV), full)]
    args = [gla_in, gla_in, gla_in, gla_in, lora, waf, baf, wab, bab, norm_g]
    if has_init:
        in_specs += [pl.BlockSpec((None, GLA_QK, GLA_DV), bmap)] * 2
        args += [s0f, s0b]
    return pl.pallas_call(
        functools.partial(_gla_kernel, has_init),
        out_shape=(jax.ShapeDtypeStruct((b, t, GLA_V), F32),
                   jax.ShapeDtypeStruct((b, GLA_QK, GLA_DV), F32),
                   jax.ShapeDtypeStruct((b, GLA_QK, GLA_DV), F32)),
        grid=(b,),
        in_specs=in_specs,
        out_specs=(pl.BlockSpec((None, t, GLA_V), bmap),
                   pl.BlockSpec((None, GLA_QK, GLA_DV), bmap),
                   pl.BlockSpec((None, GLA_QK, GLA_DV), bmap)),
        scratch_shapes=[pltpu.VMEM((t, GLA_QK), F32),
                        pltpu.VMEM((t, GLA_QK), F32),
                        pltpu.VMEM((n, GLA_HEADS * GLA_CHUNK, GLA_DV), F32),
                        pltpu.VMEM((t, GLA_QK), BF16),
                        pltpu.VMEM((t, GLA_QK), BF16),
                        pltpu.VMEM((n, GLA_DV, GLA_QK), BF16),
                        pltpu.VMEM((n, GLA_DV, GLA_QK), BF16),
                        pltpu.VMEM((GLA_DV, GLA_QK), F32),
                        pltpu.VMEM((GLA_DV, GLA_QK), F32)],
        compiler_params=pltpu.CompilerParams(dimension_semantics=("arbitrary",),
                                             vmem_limit_bytes=VMEM_LIMIT),
        name="gla",
    )(*args)


def _dup_groups(x):
    lo = lax.broadcasted_iota(jnp.int32, x.shape, 1) < SWA_HEAD_DIM
    xr = pltpu.roll(x, SWA_HEAD_DIM, axis=1)
    return jnp.where(lo, x, xr), jnp.where(lo, xr, x)


def _pairs_attention(qps, sinks, k_dups, vt_dups, mask):
    nq = qps[0].shape[0]
    lo = lax.broadcasted_iota(jnp.int32, (nq, LANES), 1) < SWA_HEAD_DIM
    even = lax.broadcasted_iota(jnp.int32, (1, 2 * nq), 1) < nq
    scores = []
    for qp, k_dup in zip(qps, k_dups):
        q2 = jnp.concatenate([jnp.where(lo, qp, 0.0), jnp.where(lo, 0.0, qp)], axis=0).astype(BF16)
        scores.append(_dot_nt(k_dup, q2))
    probs = []
    for s, (sink_even, sink_odd) in zip(scores, sinks):
        if mask is not None:
            s = jnp.where(mask, s, NEG_INF)
        sink = jnp.where(even, sink_even, sink_odd)
        m = jnp.maximum(jnp.max(s, axis=0, keepdims=True), sink)
        p = jnp.exp(s - m)
        denom = jnp.sum(p, axis=0, keepdims=True) + jnp.exp(sink - m)
        probs.append((p.astype(BF16), 1.0 / denom))
    outs = []
    for (p, rdenom), vt_dup in zip(probs, vt_dups):
        o = _dot(vt_dup, p) * rdenom
        outs.append(jnp.concatenate([o[:SWA_HEAD_DIM, :nq], o[SWA_HEAD_DIM:, nq:]], axis=0).T)
    return outs


def _attn_ctx_kernel(sink_ref, q_ref, k_ref, v_ref, o_ref):
    kd = [x.astype(BF16) for x in _dup_groups(k_ref[...])]
    vt = [x.T.astype(BF16) for x in _dup_groups(v_ref[...])]
    scale = SWA_HEAD_DIM ** -0.5
    pairs = range(SWA_HEADS // 2)
    outs = _pairs_attention([q_ref[:, pr * LANES:(pr + 1) * LANES] * scale for pr in pairs],
                            [(sink_ref[2 * pr], sink_ref[2 * pr + 1]) for pr in pairs],
                            [kd[pr // 2] for pr in pairs], [vt[pr // 2] for pr in pairs], None)
    for pr in pairs:
        o_ref[:, pr * LANES:(pr + 1) * LANES] = outs[pr]


def _attn_ctx(sink, q, k, v):
    b, t, _ = q.shape
    bmap = lambda i: (i, 0, 0)
    return pl.pallas_call(
        _attn_ctx_kernel,
        out_shape=jax.ShapeDtypeStruct((b, t, SWA_Q), F32),
        grid=(b,),
        in_specs=[pl.BlockSpec(memory_space=pltpu.SMEM),
                  pl.BlockSpec((None, t, SWA_Q), bmap),
                  pl.BlockSpec((None, t, SWA_KV), bmap),
                  pl.BlockSpec((None, t, SWA_KV), bmap)],
        out_specs=pl.BlockSpec((None, t, SWA_Q), bmap),
        compiler_params=pltpu.CompilerParams(dimension_semantics=("arbitrary",),
                                             vmem_limit_bytes=VMEM_LIMIT),
        name="attn_ctx",
    )(sink, q, k, v)


def _rope(x, cos, sin_lo, sin_hi):
    return x * cos + pltpu.roll(x, LANES - 16, axis=1) * sin_lo + pltpu.roll(x, 16, axis=1) * sin_hi


def _attn_lat_kernel(sink_ref, q_ref, k_ref, v_ref, kc_ref, vc_ref, cos_ref, sl_ref, sh_ref,
                     o_ref, kw_ref, vw_ref):
    t = q_ref.shape[0]
    ab = ATTN_BLOCK
    nb = t // ab
    scale = SWA_HEAD_DIM ** -0.5

    k_rot = _dup_groups(_rope(k_ref[...], cos_ref[...], sl_ref[...], sh_ref[...]))
    v_dup = _dup_groups(v_ref[...])
    zeros = jnp.zeros((ab, LANES), BF16)
    for grp in range(SWA_KV_HEADS):
        kw_ref[grp, 0:ab, :] = zeros
        kw_ref[grp, ab:ab + t, :] = k_rot[grp].astype(BF16)
        kw_ref[grp, ab + t:, :] = zeros
        vw_ref[grp, 0] = zeros
        for blk in range(nb):
            vw_ref[grp, blk + 1] = v_dup[grp][blk * ab:(blk + 1) * ab, :].T.astype(BF16)
        vw_ref[grp, nb + 1] = zeros
    kc = [x.astype(BF16) for x in _dup_groups(kc_ref[...])]
    vct = [x.T.astype(BF16) for x in _dup_groups(vc_ref[...])]
    lc = kc_ref.shape[0]

    key = lax.broadcasted_iota(jnp.int32, (lc + 3 * ab, 2 * ab), 0) - lc
    tq = lax.broadcasted_iota(jnp.int32, (lc + 3 * ab, 2 * ab), 1) & (ab - 1)
    band = (key < 0) | (jnp.abs(tq + ab - key) <= ab)

    def block(nq, carry):
        row0 = pl.multiple_of(nq * ab, ab)
        s_abs = key + (nq - 1) * ab
        mask = band & ((key < 0) | ((s_abs >= 0) & (s_abs < t)))
        cos = cos_ref[pl.ds(row0, ab), :]
        s_lo = sl_ref[pl.ds(row0, ab), :]
        s_hi = sh_ref[pl.ds(row0, ab), :]
        k_all = [jnp.concatenate([kc[grp], kw_ref[grp, pl.ds(row0, 3 * ab), :]], axis=0)
                 for grp in range(SWA_KV_HEADS)]
        vt_all = [jnp.concatenate([vct[grp], vw_ref[grp, nq], vw_ref[grp, nq + 1], vw_ref[grp, nq + 2]],
                                  axis=1) for grp in range(SWA_KV_HEADS)]
        pairs = range(SWA_HEADS // 2)
        qps = [_rope(q_ref[pl.ds(row0, ab), pr * LANES:(pr + 1) * LANES], cos, s_lo, s_hi) * scale
               for pr in pairs]
        outs = _pairs_attention(qps, [(sink_ref[2 * pr], sink_ref[2 * pr + 1]) for pr in pairs],
                                [k_all[pr // 2] for pr in pairs], [vt_all[pr // 2] for pr in pairs], mask)
        for pr in pairs:
            o_ref[pl.ds(row0, ab), pr * LANES:(pr + 1) * LANES] = outs[pr]
        return carry

    lax.fori_loop(0, nb, block, 0)


def _attn_lat(sink, q, k, v, kc, vc, cos, sin_lo, sin_hi):
    b, t, _ = q.shape
    lc = kc.shape[1]
    bmap = lambda i: (i, 0, 0)
    full = lambda i: (0, 0)
    return pl.pallas_call(
        _attn_lat_kernel,
        out_shape=jax.ShapeDtypeStruct((b, t, SWA_Q), F32),
        grid=(b,),
        in_specs=[pl.BlockSpec(memory_space=pltpu.SMEM),
                  pl.BlockSpec((None, t, SWA_Q), bmap),
                  pl.BlockSpec((None, t, SWA_KV), bmap),
                  pl.BlockSpec((None, t, SWA_KV), bmap),
                  pl.BlockSpec((None, lc, SWA_KV), bmap),
                  pl.BlockSpec((None, lc, SWA_KV), bmap),
                  pl.BlockSpec((t, LANES), full),
                  pl.BlockSpec((t, LANES), full),
                  pl.BlockSpec((t, LANES), full)],
        out_specs=pl.BlockSpec((None, t, SWA_Q), bmap),
        scratch_shapes=[pltpu.VMEM((SWA_KV_HEADS, t + 2 * ATTN_BLOCK, LANES), BF16),
                        pltpu.VMEM((SWA_KV_HEADS, t // ATTN_BLOCK + 2, LANES, ATTN_BLOCK), BF16)],
        compiler_params=pltpu.CompilerParams(dimension_semantics=("arbitrary",),
                                             vmem_limit_bytes=VMEM_LIMIT),
        name="attn_lat",
    )(sink, q, k, v, kc, vc, cos, sin_lo, sin_hi)


def _rope_tables(t):
    half = SWA_HEAD_DIM // 2
    quarter = half // 2
    pos = jnp.arange(t)
    row = (pos // GRID_W).astype(F32)
    col = (pos % GRID_W).astype(F32)
    inv_freq = ROPE_BASE ** (-jnp.arange(quarter, dtype=F32) / quarter)
    lane = jnp.arange(LANES)
    d = lane % SWA_HEAD_DIM
    freq = inv_freq[d % quarter]
    use_row = (d < half)
    ang = jnp.where(use_row[None, :], row[:, None], col[:, None]) * freq[None, :]
    cos = jnp.cos(ang)
    sin = jnp.sin(ang)
    lower = (d % half) < quarter
    return cos, jnp.where(lower[None, :], -sin, 0.0), jnp.where(lower[None, :], 0.0, sin)


def _route(sel, scores):
    n = sel.shape[1]
    gsz = N_EXPERTS // N_EXPERT_GROUPS

    def first_max(x, idx, size):
        m = jnp.max(x, axis=0, keepdims=True)
        first = jnp.min(jnp.where(x == m, idx, float(size)), axis=0, keepdims=True)
        return m, idx == first

    i8 = lax.broadcasted_iota(jnp.int32, (gsz, n), 0).astype(F32)
    rows = []
    for g in range(N_EXPERT_GROUPS):
        slab = sel[g * gsz:(g + 1) * gsz, :]
        m1, hit = first_max(slab, i8, gsz)
        m2 = jnp.max(jnp.where(hit, NEG_INF, slab), axis=0, keepdims=True)
        rows.append(m1 + m2)
    gscore = jnp.concatenate(rows, axis=0)
    gsel = jnp.zeros((N_EXPERT_GROUPS, n), F32)
    for _ in range(TOPK_GROUPS):
        _, hit = first_max(gscore, i8, N_EXPERT_GROUPS)
        gsel = jnp.where(hit, 1.0, gsel)
        gscore = jnp.where(hit, NEG_INF, gscore)
    emask = jnp.concatenate(
        [jnp.broadcast_to(gsel[g:g + 1, :], (gsz, n)) for g in range(N_EXPERT_GROUPS)], axis=0)
    cand = jnp.where(emask > 0.5, sel, NEG_INF)
    ie = lax.broadcasted_iota(jnp.int32, (N_EXPERTS, n), 0).astype(F32)
    w = jnp.zeros((N_EXPERTS, n), F32)
    chosen = jnp.zeros((N_EXPERTS, n), F32)
    hits = []
    for _ in range(TOP_K):
        _, hit = first_max(cand, ie, N_EXPERTS)
        hits.append(hit)
        w = jnp.where(hit, scores, w)
        chosen = jnp.where(hit, 1.0, chosen)
        cand = jnp.where(hit, NEG_INF, cand)
    gates = w / jnp.sum(w, axis=0, keepdims=True) * ROUTED_SCALE

    s_idx = lax.broadcasted_iota(jnp.int32, (n, n), 0)
    t_idx = lax.broadcasted_iota(jnp.int32, (n, n), 1)
    before = jnp.where(s_idx < t_idx, 1.0, 0.0).astype(BF16)
    rank = _dot(chosen.astype(BF16), before)
    count = jnp.sum(chosen, axis=1, keepdims=True)
    padded = jnp.floor((count + (SORT_ALIGN - 1)) * (1.0 / SORT_ALIGN)) * SORT_ALIGN
    padded = jnp.broadcast_to(padded, (N_EXPERTS, LANES))
    e_row = lax.broadcasted_iota(jnp.int32, (N_EXPERTS, N_EXPERTS), 0)
    e_col = lax.broadcasted_iota(jnp.int32, (N_EXPERTS, N_EXPERTS), 1)
    below = jnp.where(e_col < e_row, 1.0, 0.0).astype(BF16)
    start = _dot(below, padded.astype(BF16))
    row = start[:, 0:1] + rank
    pos = jnp.concatenate([jnp.sum(jnp.where(h, row, 0.0), axis=0, keepdims=True) for h in hits], axis=0)
    wts = jnp.concatenate([jnp.sum(jnp.where(h, gates, 0.0), axis=0, keepdims=True) for h in hits], axis=0)
    return pos, wts, padded, start


def _outproj_kernel(gla_ref, att_ref, x_ref, wo_ref, g1_ref, sh_ref, sc_ref, ng_ref, rw_ref, rwh_ref,
                    rb_ref, x1_ref, xm_ref, pos_ref, wts_ref, cnt_ref, start_ref):
    y = (_dot(gla_ref[...].astype(BF16), wo_ref[0:GLA_V, :])
         + _dot(att_ref[...].astype(BF16), wo_ref[GLA_V:, :]))
    x1 = x_ref[...] + g1_ref[...] * y
    x1_ref[...] = x1
    xm = _rms_norm(x1, ng_ref[...]) * (1.0 + sc_ref[...]) + sh_ref[...]
    xm_hi, xm_lo = _split_hi_lo(xm)
    xm_ref[...] = xm_hi
    lg = _dot(xm_hi, rw_ref[...])
    logits = lg[:, :N_EXPERTS] + lg[:, N_EXPERTS:] + _dot(xm_lo, rwh_ref[...])
    tm = logits.shape[0]
    lt = jnp.concatenate([logits, jnp.zeros((tm, LANES - N_EXPERTS), F32)], axis=1).T[:N_EXPERTS, :]
    scores = _sigmoid(lt)
    pos_ref[...], wts_ref[...], cnt_ref[...], start_ref[...] = _route(scores + rb_ref[...], scores)


def _outproj(gla_out, att_out, x, w_out, g1, sh2, sc2, norm_g, rw_cat, rw_hi, rbias, *, tm):
    b, t, d = x.shape
    nmod = g1.shape[0]
    mod_map = (lambda i, j: (i, 0, 0)) if nmod > 1 else (lambda i, j: (0, 0, 0))
    row = lambda i, j: (i, j, 0)
    full = lambda i, j: (0, 0)
    tile = lambda i, j: (i, j, 0, 0)
    nt = t // tm
    return pl.pallas_call(
        _outproj_kernel,
        out_shape=(jax.ShapeDtypeStruct((b, t, d), F32),
                   jax.ShapeDtypeStruct((b, t, d), BF16),
                   jax.ShapeDtypeStruct((b, nt, TOP_K, tm), F32),
                   jax.ShapeDtypeStruct((b, nt, TOP_K, tm), F32),
                   jax.ShapeDtypeStruct((b, nt, N_EXPERTS, LANES), F32),
                   jax.ShapeDtypeStruct((b, nt, N_EXPERTS, LANES), F32)),
        grid=(b, t // tm),
        in_specs=[pl.BlockSpec((None, tm, GLA_V), row),
                  pl.BlockSpec((None, tm, SWA_Q), row),
                  pl.BlockSpec((None, tm, d), row),
                  pl.BlockSpec((d, d), full),
                  pl.BlockSpec((None, 1, d), mod_map),
                  pl.BlockSpec((None, 1, d), mod_map),
                  pl.BlockSpec((None, 1, d), mod_map),
                  pl.BlockSpec((1, d), full),
                  pl.BlockSpec((d, 2 * N_EXPERTS), full),
                  pl.BlockSpec((d, N_EXPERTS), full),
                  pl.BlockSpec((N_EXPERTS, 1), full)],
        out_specs=(pl.BlockSpec((None, tm, d), row),
                   pl.BlockSpec((None, tm, d), row),
                   pl.BlockSpec((None, None, TOP_K, tm), tile),
                   pl.BlockSpec((None, None, TOP_K, tm), tile),
                   pl.BlockSpec((None, None, N_EXPERTS, LANES), tile),
                   pl.BlockSpec((None, None, N_EXPERTS, LANES), tile)),
        compiler_params=pltpu.CompilerParams(dimension_semantics=("arbitrary", "arbitrary"),
                                             vmem_limit_bytes=VMEM_LIMIT),
        name="outproj",
    )(gla_out, att_out, x, w_out, g1, sh2, sc2, norm_g, rw_cat, rw_hi, rbias)


MOE_TILE = 256
SORT_ALIGN = 16
SORT_ROWS = 3072
ROW_TILE = 512
GATHER_SLOTS = 3
FFN_CHAINS = 4
COMBINE_CHUNK = 1024
ALWAYS_ROWS = 2560
COMBINE_TAIL = 512


def _moe_sort_kernel(tiles_a, used_ref, xa_ref, xb_ref, pos_ref, xs_ref):
    i = pl.program_id(0)
    x = jnp.where(i < tiles_a, xa_ref[...], xb_ref[...])
    pos = pos_ref[...]
    tm = x.shape[0]
    used = used_ref[i]

    def fill(blk):
        rows = (lax.broadcasted_iota(jnp.int32, (tm, tm), 0) + blk * tm).astype(F32)
        onehot = jnp.zeros((tm, tm), F32)
        for k in range(TOP_K):
            onehot = jnp.where(rows == pos[k:k + 1, :], 1.0, onehot)
        xs_ref[blk * tm:(blk + 1) * tm, :] = _dot(onehot.astype(BF16), x).astype(BF16)

    for blk in range(SORT_ROWS // tm):
        if (blk + 1) * tm <= ALWAYS_ROWS:
            fill(blk)
        else:
            pl.when(blk * tm < used)(functools.partial(fill, blk))

            @pl.when(blk * tm >= used)
            def _():
                xs_ref[blk * tm:(blk + 1) * tm, :] = jnp.zeros((tm, D_MODEL), BF16)


def _moe_sort(xm_a, xm_b, pos, used):
    d = xm_a.shape[1]
    nt, _, tm = pos.shape
    tiles_a = xm_a.shape[0] // tm
    grid_spec = pltpu.PrefetchScalarGridSpec(
        num_scalar_prefetch=1,
        grid=(nt,),
        in_specs=[pl.BlockSpec((tm, d), lambda i, u: (jnp.minimum(i, tiles_a - 1), 0)),
                  pl.BlockSpec((tm, d), lambda i, u: (jnp.maximum(i - tiles_a, 0), 0)),
                  pl.BlockSpec((None, TOP_K, tm), lambda i, u: (i, 0, 0))],
        out_specs=pl.BlockSpec((SORT_ROWS, d), lambda i, u: (i, 0)))
    return pl.pallas_call(
        functools.partial(_moe_sort_kernel, tiles_a),
        out_shape=jax.ShapeDtypeStruct((nt * SORT_ROWS, d), BF16),
        grid_spec=grid_spec,
        compiler_params=pltpu.CompilerParams(dimension_semantics=("arbitrary",),
                                             vmem_limit_bytes=VMEM_LIMIT),
        name="moe_sort",
    )(used, xm_a, xm_b, pos)


def _moe_row_tiles(n_tokens):
    rows = n_tokens * TOP_K + (n_tokens // MOE_TILE) * N_EXPERTS * (SORT_ALIGN - 1) + N_EXPERTS * (ROW_TILE - 1)
    return -(-rows // ROW_TILE) + GATHER_SLOTS - 1


PLAN_CHUNK = 1280


def _int_dot_r(a, onehot):
    hi = jnp.floor(a * (1.0 / 256.0))
    return _dot(hi.astype(BF16), onehot) * 256.0 + _dot((a - hi * 256.0).astype(BF16), onehot)


def _int_dot_l(onehot, b):
    hi = jnp.floor(b * (1.0 / 256.0))
    return _dot(onehot, hi.astype(BF16)) * 256.0 + _dot(onehot, (b - hi * 256.0).astype(BF16))


def _moe_plan_kernel(cnt_ref, start_ref, src_ref, first_ref, tiles_ref, nu_ref, back_ref):
    nt, ne = cnt_ref.shape
    gpt = SORT_ROWS // SORT_ALIGN
    gpr = ROW_TILE // SORT_ALIGN
    gc = cnt_ref[...] * (1.0 / SORT_ALIGN)
    ls = start_ref[...] * (1.0 / SORT_ALIGN)

    def transpose(x):
        x = jnp.concatenate([x, jnp.zeros((nt, LANES - ne), F32)], axis=1)
        x = jnp.concatenate([x, jnp.zeros((LANES - nt, LANES), F32)], axis=0)
        return x.T[:ne, :nt]

    def tri(n, keep):
        return jnp.where(keep(lax.broadcasted_iota(jnp.int32, (n, n), 0),
                              lax.broadcasted_iota(jnp.int32, (n, n), 1)), 1.0, 0.0).astype(BF16)

    gc_t = transpose(gc)
    ls_t = transpose(ls)
    tot_c = jnp.broadcast_to(jnp.sum(gc_t, axis=1, keepdims=True), (ne, LANES))
    ptot_c = jnp.floor((tot_c + (gpr - 1)) * (1.0 / gpr)) * gpr
    gend_c = _int_dot_l(tri(ne, lambda r, c: c <= r), ptot_c)
    gstart_c = gend_c - ptot_c
    n_used = gend_c[ne - 1:ne, :] * (1.0 / gpr)
    nu_ref[...] = n_used.astype(jnp.int32)
    tot_r = jnp.sum(gc, axis=0, keepdims=True)
    ptot_r = jnp.floor((tot_r + (gpr - 1)) * (1.0 / gpr)) * gpr
    gstart_r = _int_dot_r(jnp.broadcast_to(ptot_r, (8, ne)), tri(ne, lambda r, c: r < c))
    cumex = _dot(tri(nt, lambda r, c: c < r), gc.astype(BF16))
    cumex_t = _dot(gc_t.astype(BF16), tri(nt, lambda r, c: r < c))
    tile_base = lax.broadcasted_iota(jnp.int32, (nt, ne), 0).astype(F32) * gpt + ls
    table = jnp.concatenate([cumex + gc, cumex, tile_base, gstart_r, jnp.broadcast_to(tot_r, (8, ne))], axis=0)

    e_iota = lax.broadcasted_iota(jnp.int32, (ne, PLAN_CHUNK), 0).astype(F32)
    for ch in range(src_ref.shape[1] // PLAN_CHUNK):
        g = (lax.broadcasted_iota(jnp.int32, (1, PLAN_CHUNK), 1) + ch * PLAN_CHUNK).astype(F32)
        eg = jnp.sum(jnp.where(gend_c[:, 0:1] <= g, 1.0, 0.0), axis=0, keepdims=True)
        picked = _int_dot_r(table, jnp.where(e_iota == eg, 1.0, 0.0).astype(BF16))
        cum_g, cumex_g, base_g = picked[0:nt], picked[nt:2 * nt], picked[2 * nt:3 * nt]
        u = g - picked[3 * nt:3 * nt + 1]
        in_tile = (cumex_g <= u) & (u < cum_g)
        src = jnp.sum(jnp.where(in_tile, base_g - cumex_g, 0.0), axis=0, keepdims=True) + u
        src = jnp.where(u < picked[3 * nt + 8:3 * nt + 9], src, gpt - 1.0)
        src_ref[:, ch * PLAN_CHUNK:(ch + 1) * PLAN_CHUNK] = src.astype(jnp.int32)

    first_ref[...] = (gstart_c * (1.0 / gpr)).astype(jnp.int32)
    tiles_ref[...] = (ptot_c * (1.0 / gpr)).astype(jnp.int32)

    lg = lax.broadcasted_iota(jnp.int32, (ne, back_ref.shape[1]), 1).astype(F32)
    for t in range(nt):
        first = ls_t[:, t:t + 1]
        inside = (first <= lg) & (lg < first + gc_t[:, t:t + 1])
        shift = gstart_c[:, 0:1] + cumex_t[:, t:t + 1] - first
        val = jnp.sum(jnp.where(inside, shift + lg, 0.0), axis=0, keepdims=True)
        back_ref[t:t + 1, :] = val.astype(jnp.int32)


def _moe_plan(cnt, start):
    nt, ne = cnt.shape
    row_tiles = _moe_row_tiles(nt * MOE_TILE)
    gpt = SORT_ROWS // SORT_ALIGN
    gpr = ROW_TILE // SORT_ALIGN
    n_src = -(-(row_tiles * gpr) // PLAN_CHUNK) * PLAN_CHUNK
    n_back = -(-gpt // LANES) * LANES
    src, first, tiles, nu, back = pl.pallas_call(
        _moe_plan_kernel,
        out_shape=(jax.ShapeDtypeStruct((1, n_src), jnp.int32),
                   jax.ShapeDtypeStruct((ne, LANES), jnp.int32),
                   jax.ShapeDtypeStruct((ne, LANES), jnp.int32),
                   jax.ShapeDtypeStruct((1, LANES), jnp.int32),
                   jax.ShapeDtypeStruct((nt, n_back), jnp.int32)),
        compiler_params=pltpu.CompilerParams(vmem_limit_bytes=VMEM_LIMIT),
        name="moe_plan",
    )(cnt, start)
    return nu[0, :1], first[:, 0], tiles[:, 0], src[0, :row_tiles * gpr], back[:, :gpt]


def _moe_experts_kernel(nu_ref, first_ref, tiles_ref, src_ref, xs_hbm, wg_ref, wu_ref, wd_ref, ys_hbm,
                        xbuf, ybuf, gsem, osem, wgu_s, wd_s):
    e = pl.program_id(0)
    n_used = nu_ref[0]
    gpr = ROW_TILE // SORT_ALIGN
    part = ROW_TILE // FFN_CHAINS

    def gather(tile, to_slot, j0=0, j1=gpr):
        for j in range(j0, j1):
            row = pl.multiple_of(src_ref[tile * gpr + j] * SORT_ALIGN, SORT_ALIGN)
            pltpu.make_async_copy(xs_hbm.at[pl.ds(row, SORT_ALIGN), :],
                                  xbuf.at[to_slot, j * SORT_ALIGN:(j + 1) * SORT_ALIGN, :], gsem.at[to_slot]).start()

    def drain(of_slot):
        for j in range(gpr):
            pltpu.make_async_copy(xs_hbm.at[0:SORT_ALIGN, :],
                                  xbuf.at[of_slot, j * SORT_ALIGN:(j + 1) * SORT_ALIGN, :], gsem.at[of_slot]).wait()

    def out_copy(tile, of_slot):
        row = pl.multiple_of(tile * ROW_TILE, ROW_TILE)
        return pltpu.make_async_copy(ybuf.at[of_slot], ys_hbm.at[pl.ds(row, ROW_TILE), :], osem.at[of_slot])

    @pl.when(e == 0)
    def _():
        gather(0, 0)
        gather(1, 1)

    wgu_s[:, :EXPERT_FF] = wg_ref[...].astype(BF16)
    wgu_s[:, EXPERT_FF:] = wu_ref[...].astype(BF16)
    wd_s[...] = wd_ref[...].astype(BF16)

    def row_tile(i, carry):
        r = first_ref[e] + i
        slot = lax.rem(r, GATHER_SLOTS)
        oslot = lax.rem(r, 2)
        next_slot = lax.rem(r + 2, GATHER_SLOTS)
        drain(slot)

        @pl.when(r >= 2)
        def _():
            out_copy(r - 2, oslot).wait()

        abs_ = []
        for c in range(FFN_CHAINS):
            abs_.append(_dot(xbuf[slot, c * part:(c + 1) * part, :], wgu_s[...]))
            gather(r + 2, next_slot, c * gpr // FFN_CHAINS, (c + 1) * gpr // FFN_CHAINS)
        hs = [(_silu(ab[:, :EXPERT_FF]) * ab[:, EXPERT_FF:]).astype(BF16) for ab in abs_]
        ys = [_dot(h, wd_s[...]).astype(BF16) for h in hs]
        for c in range(FFN_CHAINS):
            ybuf[oslot, c * part:(c + 1) * part, :] = ys[c]
        out_copy(r, oslot).start()
        return carry

    lax.fori_loop(0, tiles_ref[e], row_tile, 0)

    @pl.when(e == pl.num_programs(0) - 1)
    def _():
        drain(lax.rem(n_used, GATHER_SLOTS))
        drain(lax.rem(n_used + 1, GATHER_SLOTS))
        out_copy(n_used - 1, lax.rem(n_used - 1, 2)).wait()

        @pl.when(n_used >= 2)
        def _():
            out_copy(n_used - 2, lax.rem(n_used, 2)).wait()


def _moe_experts(n_used, first, tiles, src, xs, wg, wu, wd, row_tiles):
    d = xs.shape[-1]
    ne = wg.shape[0]
    w_map = lambda e, nu, fi, ti, sr: (e, 0, 0)
    grid_spec = pltpu.PrefetchScalarGridSpec(
        num_scalar_prefetch=4,
        grid=(ne,),
        in_specs=[pl.BlockSpec(memory_space=pl.ANY),
                  pl.BlockSpec((None, d, EXPERT_FF), w_map),
                  pl.BlockSpec((None, d, EXPERT_FF), w_map),
                  pl.BlockSpec((None, EXPERT_FF, d), w_map)],
        out_specs=pl.BlockSpec(memory_space=pl.ANY),
        scratch_shapes=[pltpu.VMEM((GATHER_SLOTS, ROW_TILE, d), BF16),
                        pltpu.VMEM((2, ROW_TILE, d), BF16),
                        pltpu.SemaphoreType.DMA((GATHER_SLOTS,)),
                        pltpu.SemaphoreType.DMA((2,)),
                        pltpu.VMEM((d, 2 * EXPERT_FF), BF16),
                        pltpu.VMEM((EXPERT_FF, d), BF16)])
    return pl.pallas_call(
        _moe_experts_kernel,
        out_shape=jax.ShapeDtypeStruct((row_tiles * ROW_TILE, d), BF16),
        grid_spec=grid_spec,
        compiler_params=pltpu.CompilerParams(dimension_semantics=("arbitrary",),
                                             vmem_limit_bytes=VMEM_LIMIT),
        name="moe_experts",
    )(n_used, first, tiles, src, xs, wg, wu, wd)


def _moe_combine_kernel(back_ref, used_ref, ys_hbm, pos_ref, wts_ref, xm_ref, x1_ref, g2_ref, fg_ref,
                        swg_ref, swu_ref, swd_ref, o_ref, buf, sem, acc_ref):
    i = pl.program_id(0)
    gpt = SORT_ROWS // SORT_ALIGN
    slot = lax.rem(i, 2)
    always = ALWAYS_ROWS
    tail = range(always, SORT_ROWS, COMBINE_TAIL)

    def copies(tile, of_slot, g0, g1, start):
        for g in range(g0, g1):
            row = pl.multiple_of(back_ref[tile * gpt + g] * SORT_ALIGN, SORT_ALIGN) if start else 0
            cp = pltpu.make_async_copy(ys_hbm.at[pl.ds(row, SORT_ALIGN), :],
                                       buf.at[of_slot, g * SORT_ALIGN:(g + 1) * SORT_ALIGN, :], sem.at[of_slot])
            if start:
                cp.start()
            else:
                cp.wait()

    def transfer(tile, of_slot, start):
        copies(tile, of_slot, 0, always // SORT_ALIGN, start)
        for c0 in tail:
            pl.when(c0 < used_ref[tile])(functools.partial(
                copies, tile, of_slot, c0 // SORT_ALIGN, (c0 + COMBINE_TAIL) // SORT_ALIGN, start))

    @pl.when(i == 0)
    def _():
        transfer(0, 0, True)

    @pl.when(i + 1 < pl.num_programs(0))
    def _():
        transfer(i + 1, 1 - slot, True)

    x = xm_ref[...]
    tm = x.shape[0]
    shared = _dot((_silu(_dot(x, swg_ref[...])) * _dot(x, swu_ref[...])).astype(BF16), swd_ref[...])
    pad = jnp.zeros((LANES - TOP_K, tm), F32)
    pos_t = jnp.concatenate([pos_ref[...], pad], axis=0).T
    wts_t = jnp.concatenate([wts_ref[...], pad], axis=0).T
    transfer(i, slot, False)

    pos_b = [jnp.broadcast_to(pos_t[:, k:k + 1], (tm, LANES)) for k in range(TOP_K)]
    wts_b = [jnp.broadcast_to(wts_t[:, k:k + 1], (tm, LANES)) for k in range(TOP_K)]

    def apply(c0, width):
        reps = width // LANES
        rows = (lax.broadcasted_iota(jnp.int32, (tm, width), 1) + c0).astype(F32)
        comb = jnp.zeros((tm, width), F32)
        for k in range(TOP_K):
            comb = jnp.where(rows == jnp.concatenate([pos_b[k]] * reps, axis=1),
                             jnp.concatenate([wts_b[k]] * reps, axis=1), comb)
        return _dot(comb.astype(BF16), buf[slot, c0:c0 + width, :])

    routed = shared
    for c0 in range(0, always, COMBINE_CHUNK):
        routed = routed + apply(c0, min(COMBINE_CHUNK, always - c0))
    acc_ref[...] = routed
    for c0 in tail:
        @pl.when(c0 < used_ref[i])
        def _(c0=c0):
            acc_ref[...] += apply(c0, COMBINE_TAIL)
    y = x1_ref[...] + g2_ref[...] * acc_ref[...]
    o_ref[...] = _rms_norm(y, fg_ref[...])


def _moe_combine(back, used, ys, pos, wts, xm, x1, g2, final_g, swg, swu, swd, *, tiles_per_mod):
    n, d = xm.shape
    tm = pos.shape[-1]
    nt = n // tm
    gpt = SORT_ROWS // SORT_ALIGN
    row = lambda i, bk, us: (i, 0)
    full = lambda i, bk, us: (0, 0)
    tile = lambda i, bk, us: (i, 0, 0)
    mod_map = lambda i, bk, us: (i // tiles_per_mod, 0, 0)
    grid_spec = pltpu.PrefetchScalarGridSpec(
        num_scalar_prefetch=2,
        grid=(nt,),
        in_specs=[pl.BlockSpec(memory_space=pl.ANY),
                  pl.BlockSpec((None, TOP_K, tm), tile),
                  pl.BlockSpec((None, TOP_K, tm), tile),
                  pl.BlockSpec((tm, d), row),
                  pl.BlockSpec((tm, d), row),
                  pl.BlockSpec((None, 1, d), mod_map),
                  pl.BlockSpec((1, d), full),
                  pl.BlockSpec((d, SHARED_FF), full),
                  pl.BlockSpec((d, SHARED_FF), full),
                  pl.BlockSpec((SHARED_FF, d), full)],
        out_specs=pl.BlockSpec((tm, d), row),
        scratch_shapes=[pltpu.VMEM((2, SORT_ROWS, d), BF16),
                        pltpu.SemaphoreType.DMA((2,)),
                        pltpu.VMEM((tm, d), F32)])
    return pl.pallas_call(
        _moe_combine_kernel,
        out_shape=jax.ShapeDtypeStruct((n, d), F32),
        grid_spec=grid_spec,
        compiler_params=pltpu.CompilerParams(dimension_semantics=("arbitrary",),
                                             vmem_limit_bytes=VMEM_LIMIT),
        name="moe_combine",
    )(back, used, ys, pos.reshape(nt, TOP_K, tm), wts.reshape(nt, TOP_K, tm), xm, x1, g2, final_g, swg, swu, swd)


def _mix(x, mods, p, attn_fn, s0=None):
    sh1, sc1, g1, sh2, sc2, _ = mods
    gla_in, lora, q_s, k_s, v_s = _inproj(x, p["norm_attn_g"], sh1, sc1, p["w_gla"], p["w_lora"],
                                          p["w_swa"], tm=256)
    if s0 is None:
        gla_out, s_f, s_b = _gla(gla_in, lora, p["waf"], p["baf"], p["wab"], p["bab"], p["gla_norm_g"])
    else:
        gla_out, s_f, s_b = _gla(gla_in, lora, p["waf"], p["baf"], p["wab"], p["bab"], p["gla_norm_g"],
                                 s0[0], s0[1])
    att_out = attn_fn(q_s, k_s, v_s)
    routed = _outproj(gla_out, att_out, x, p["w_out"], g1, sh2, sc2, p["norm_ffn_g"],
                      p["rw_cat"], p["rw_hi"], p["rbias"], tm=MOE_TILE)
    return routed, k_s, v_s, s_f, s_b


def _moe(streams, p):
    d = D_MODEL
    (ra, _), (rb, _) = streams
    n_tiles = [r[1].shape[0] * r[1].shape[1] // MOE_TILE for r, _ in streams]
    pos_all = jnp.concatenate([r[2].reshape(-1, TOP_K, MOE_TILE) for r, _ in streams], axis=0)
    cnt_all = jnp.concatenate([r[4][..., 0].reshape(-1, N_EXPERTS) for r, _ in streams], axis=0)
    start_all = jnp.concatenate([r[5][..., 0].reshape(-1, N_EXPERTS) for r, _ in streams], axis=0)
    used = (start_all[:, -1] + cnt_all[:, -1]).astype(jnp.int32)
    xs = _moe_sort(ra[1].reshape(-1, d), rb[1].reshape(-1, d), pos_all, used)
    n_used, first, tiles, src, back = _moe_plan(cnt_all, start_all)
    ys = _moe_experts(n_used, first, tiles, src, xs, p["wg"], p["wu"], p["wd"],
                      _moe_row_tiles(cnt_all.shape[0] * MOE_TILE))
    outs = []
    tile0 = 0
    for ((x1, xm, pos, wts, cnt, start), g2), nt in zip(streams, n_tiles):
        b, t, _ = x1.shape
        tiles_per_mod = (t // MOE_TILE) if g2.shape[0] > 1 else nt
        y = _moe_combine(back[tile0:tile0 + nt].reshape(-1), used[tile0:tile0 + nt], ys, pos, wts,
                         xm.reshape(-1, d), x1.reshape(-1, d), g2, p["final_norm_g"],
                         p["swg"], p["swu"], p["swd"], tiles_per_mod=tiles_per_mod)
        outs.append(y.reshape(b, t, d))
        tile0 += nt
    return outs


def kernel(x_prompt, x_sample, c, cache_swa_k, cache_swa_v, state_gla_fwd, state_gla_bwd, c_ctx, w_ada, b_ada, norm_attn_g, norm_ffn_g, w_in, gla_wa_f, gla_ba_f, gla_wa_b, gla_ba_b, gla_norm_g, swa_sink, w_out, router_w, router_bias, exp_w_gate, exp_w_up, exp_w_down, sh_w_gate, sh_w_up, sh_w_down, final_norm_g):
    l = 0
    d = D_MODEL
    nb_ctx, t_ctx, _ = x_prompt.shape
    nb_lat, t_lat, _ = x_sample.shape

    pad = jnp.zeros((8 - 1 - nb_lat, d), F32)
    cond8 = jnp.concatenate([c_ctx[None, :], c, pad], axis=0)
    mod = _adaln(cond8, w_ada[l], b_ada[l][None, :])
    mods_ctx = [mod[0:1, i * d:(i + 1) * d][:, None, :] for i in range(6)]
    mods_lat = [mod[1:1 + nb_lat, i * d:(i + 1) * d][:, None, :] for i in range(6)]

    zeros_lora = jnp.zeros((GLA_LORA, GLA_QK), F32)
    rw = router_w[l]
    rw_hi = rw.astype(BF16)
    rw_lo = (rw - rw_hi.astype(F32)).astype(BF16)
    w_in_b = w_in[l].astype(BF16)
    p = {
        "norm_attn_g": norm_attn_g[l][None, :],
        "norm_ffn_g": norm_ffn_g[l][None, :],
        "final_norm_g": final_norm_g[None, :],
        "w_gla": w_in_b[:, :2 * GLA_QK + 2 * GLA_V],
        "w_lora": w_in_b[:, 2 * GLA_QK + 2 * GLA_V:2 * GLA_QK + 2 * GLA_V + 2 * GLA_LORA],
        "w_swa": w_in_b[:, 2 * GLA_QK + 2 * GLA_V + 2 * GLA_LORA:],
        "waf": jnp.concatenate([gla_wa_f[l], zeros_lora], axis=0).astype(BF16),
        "wab": jnp.concatenate([zeros_lora, gla_wa_b[l]], axis=0).astype(BF16),
        "baf": gla_ba_f[l][None, :],
        "bab": gla_ba_b[l][None, :],
        "gla_norm_g": gla_norm_g[l][None, :],
        "w_out": w_out[l].astype(BF16),
        "rw_cat": jnp.concatenate([rw_hi, rw_lo], axis=1),
        "rw_hi": rw_hi,
        "rbias": router_bias[l][:, None],
        "wg": exp_w_gate[l], "wu": exp_w_up[l], "wd": exp_w_down[l],
        "swg": sh_w_gate[l].astype(BF16), "swu": sh_w_up[l].astype(BF16),
        "swd": sh_w_down[l].astype(BF16),
    }
    sink = swa_sink[l]

    routed_ctx, k_c, v_c, s_f, s_b = _mix(x_prompt, mods_ctx, p, functools.partial(_attn_ctx, sink))

    cos, sin_lo, sin_hi = _rope_tables(t_lat)
    kc = cache_swa_k[:, l].reshape(nb_lat, -1, SWA_KV)
    vc = cache_swa_v[:, l].reshape(nb_lat, -1, SWA_KV)
    lat_attn = lambda q, k, v: _attn_lat(sink, q, k, v, kc, vc, cos, sin_lo, sin_hi)
    s0 = (state_gla_fwd[:, l].reshape(nb_lat, GLA_QK, GLA_DV),
          state_gla_bwd[:, l].reshape(nb_lat, GLA_QK, GLA_DV))
    routed_lat, _, _, _, _ = _mix(x_sample, mods_lat, p, lat_attn, s0)
    y_prompt, y_sample = _moe([(routed_ctx, mods_ctx[5]), (routed_lat, mods_lat[5])], p)

    new_k = k_c.reshape(nb_ctx, 1, t_ctx, SWA_KV_HEADS, SWA_HEAD_DIM)
    new_v = v_c.reshape(nb_ctx, 1, t_ctx, SWA_KV_HEADS, SWA_HEAD_DIM)
    new_sf = s_f.reshape(nb_ctx, 1, GLA_HEADS, GLA_DK, GLA_DV)
    new_sb = s_b.reshape(nb_ctx, 1, GLA_HEADS, GLA_DK, GLA_DV)
    return (y_prompt, y_sample, new_k, new_v, new_sf, new_sb)
```

```python
import functools

import jax
import jax.numpy as jnp
from jax import lax
from jax.experimental import pallas as pl
from jax.experimental.pallas import tpu as pltpu

F32 = jnp.float32
BF16 = jnp.bfloat16

D_MODEL = 1024
GLA_HEADS = 4
GLA_DK = 64
GLA_DV = 128
GLA_LORA = 16
GLA_GATE_NORM = 16.0
GLA_CHUNK = 64
GLA_QK = GLA_HEADS * GLA_DK
GLA_V = GLA_HEADS * GLA_DV
SWA_HEAD_DIM = 64
SWA_HEADS = 8
SWA_KV_HEADS = 2
SWA_Q = SWA_HEADS * SWA_HEAD_DIM
SWA_KV = SWA_KV_HEADS * SWA_HEAD_DIM
ATTN_BLOCK = 128
GRID_W = 64
ROPE_BASE = 10000.0
N_EXPERTS = 64
TOP_K = 8
N_EXPERT_GROUPS = 8
TOPK_GROUPS = 4
EXPERT_FF = 128
SHARED_FF = 256
ROUTED_SCALE = 2.5
EPS = 1e-6

LANES = 128
VMEM_LIMIT = 56 * 1024 * 1024

NEG_INF = float("-inf")


def _dot(a, b):
    return jnp.dot(a, b, preferred_element_type=F32)


def _dot_nt(a, b):
    return lax.dot_general(a, b, (((1,), (1,)), ((), ())), preferred_element_type=F32)


def _split_hi_lo(x):
    hi = x.astype(BF16)
    lo = (x - hi.astype(F32)).astype(BF16)
    return hi, lo


def _sigmoid(x):
    return 1.0 / (1.0 + jnp.exp(-x))


def _silu(x):
    return x * _sigmoid(x)


def _rms_norm(x, g):
    ms = jnp.mean(x * x, axis=-1, keepdims=True)
    return x * lax.rsqrt(ms + EPS) * g


def _adaln_kernel(c_ref, w_ref, b_ref, o_ref):
    a_hi, a_lo = _split_hi_lo(_silu(c_ref[...]))
    w_hi, w_lo = _split_hi_lo(w_ref[...])
    o_ref[...] = _dot(a_hi, w_hi) + _dot(a_lo, w_hi) + _dot(a_hi, w_lo) + b_ref[...]


def _adaln(cond8, w_ada, b_ada):
    n = w_ada.shape[1]
    tn = 1536
    return pl.pallas_call(
        _adaln_kernel,
        out_shape=jax.ShapeDtypeStruct((8, n), F32),
        grid=(n // tn,),
        in_specs=[pl.BlockSpec((8, D_MODEL), lambda j: (0, 0)),
                  pl.BlockSpec((D_MODEL, tn), lambda j: (0, j)),
                  pl.BlockSpec((1, tn), lambda j: (0, j))],
        out_specs=pl.BlockSpec((8, tn), lambda j: (0, j)),
        compiler_params=pltpu.CompilerParams(dimension_semantics=("arbitrary",),
                                             vmem_limit_bytes=VMEM_LIMIT),
        name="adaln",
    )(cond8, w_ada, b_ada)


def _inproj_kernel(x_ref, g_ref, sh_ref, sc_ref, wg_ref, wl_ref, ws_ref,
                   gla_ref, lora_ref, q_ref, k_ref, v_ref):
    h = _rms_norm(x_ref[...], g_ref[...]) * (1.0 + sc_ref[...]) + sh_ref[...]
    hb = h.astype(BF16)
    gla_ref[...] = _dot(hb, wg_ref[...])
    lora_ref[...] = _dot(hb, wl_ref[...])
    s = _dot(hb, ws_ref[...])
    q_ref[...] = s[:, :SWA_Q]
    k_ref[...] = s[:, SWA_Q:SWA_Q + SWA_KV]
    v_ref[...] = s[:, SWA_Q + SWA_KV:]


def _inproj(x, g, sh, sc, w_gla, w_lora, w_swa, *, tm):
    b, t, d = x.shape
    nmod = sh.shape[0]
    mod_map = (lambda i, j: (i, 0, 0)) if nmod > 1 else (lambda i, j: (0, 0, 0))
    row = lambda i, j: (i, j, 0)
    full = lambda i, j: (0, 0)
    n_gla = w_gla.shape[1]
    n_lora = w_lora.shape[1]
    return pl.pallas_call(
        _inproj_kernel,
        out_shape=(jax.ShapeDtypeStruct((b, t, n_gla), F32),
                   jax.ShapeDtypeStruct((b, t, n_lora), F32),
                   jax.ShapeDtypeStruct((b, t, SWA_Q), F32),
                   jax.ShapeDtypeStruct((b, t, SWA_KV), F32),
                   jax.ShapeDtypeStruct((b, t, SWA_KV), F32)),
        grid=(b, t // tm),
        in_specs=[pl.BlockSpec((None, tm, d), row),
                  pl.BlockSpec((1, d), full),
                  pl.BlockSpec((None, 1, d), mod_map),
                  pl.BlockSpec((None, 1, d), mod_map),
                  pl.BlockSpec((d, n_gla), full),
                  pl.BlockSpec((d, n_lora), full),
                  pl.BlockSpec((d, w_swa.shape[1]), full)],
        out_specs=(pl.BlockSpec((None, tm, n_gla), row),
                   pl.BlockSpec((None, tm, n_lora), row),
                   pl.BlockSpec((None, tm, SWA_Q), row),
                   pl.BlockSpec((None, tm, SWA_KV), row),
                   pl.BlockSpec((None, tm, SWA_KV), row)),
        compiler_params=pltpu.CompilerParams(dimension_semantics=("arbitrary", "arbitrary"),
                                             vmem_limit_bytes=VMEM_LIMIT),
        name="inproj",
    )(x, g, sh, sc, w_gla, w_lora, w_swa)


SCAN_UNROLL = 2
OUT_UNROLL = 4


def _log_sigmoid(x):
    return jnp.minimum(x, 0.0) - jnp.log(1.0 + jnp.exp(-jnp.abs(x)))


def _heads_to_rows(x):
    return jnp.concatenate([x[:, h * LANES:(h + 1) * LANES] for h in range(GLA_HEADS)], axis=0)


def _rows_to_heads(x, c):
    return jnp.concatenate([x[h * c:(h + 1) * c, :] for h in range(GLA_HEADS)], axis=1)


def _gla_kernel(has_init, q_ref, k_ref, v_ref, g_ref, lora_ref, waf_ref, baf_ref, wab_ref, bab_ref,
                ng_ref, *rest):
    if has_init:
        s0f_ref, s0b_ref, *rest = rest
    (out_ref, sf_ref, sb_ref, laf_ref, lab_ref, oacc_ref, qtf_ref, qtb_ref, saf_ref, sab_ref,
     stf_ref, stb_ref) = rest
    t = q_ref.shape[0]
    c = GLA_CHUNK
    n = t // c
    hc = GLA_HEADS * c

    lora = lora_ref[...].astype(BF16)
    laf_ref[...] = _log_sigmoid(_dot(lora, waf_ref[...]) + baf_ref[...]) * (1.0 / GLA_GATE_NORM)
    lab_ref[...] = _log_sigmoid(_dot(lora, wab_ref[...]) + bab_ref[...]) * (1.0 / GLA_GATE_NORM)

    if has_init:
        stf_ref[...] = s0f_ref[...].T
        stb_ref[...] = s0b_ref[...].T
    else:
        stf_ref[...] = jnp.zeros_like(stf_ref)
        stb_ref[...] = jnp.zeros_like(stb_ref)
    oacc_ref[...] = jnp.zeros_like(oacc_ref)

    r64 = lax.broadcasted_iota(jnp.int32, (c, c), 0)
    c64 = lax.broadcasted_iota(jnp.int32, (c, c), 1)
    tri_f = jnp.where(c64 <= r64, 1.0, 0.0).astype(BF16)
    tri_b = jnp.where(c64 >= r64, 1.0, 0.0).astype(BF16)
    rr = lax.broadcasted_iota(jnp.int32, (hc, hc), 0)
    cc = lax.broadcasted_iota(jnp.int32, (hc, hc), 1)
    same_head = (rr >> 6) == (cc >> 6)
    keep_f = same_head & ((rr & (c - 1)) >= (cc & (c - 1)))
    keep_b = same_head & ((rr & (c - 1)) <= (cc & (c - 1)))
    head_mask = jnp.where(same_head, 1.0, 0.0).astype(BF16)
    norm_g = ng_ref[...]

    def chunk_rows(ci):
        return pl.ds(pl.multiple_of(ci * c, c), c)

    def tile_heads(x):
        x4 = jnp.concatenate([x] * GLA_HEADS, axis=0)
        return jnp.where(same_head, x4, 0.0).astype(BF16)

    def scan_step(i, carry):
        dirs = []
        for u in range(SCAN_UNROLL):
            dirs += [(SCAN_UNROLL * i + u, laf_ref, tri_f, keep_f, c - 1, stf_ref, saf_ref, qtf_ref),
                     (n - 1 - SCAN_UNROLL * i - u, lab_ref, tri_b, keep_b, 0, stb_ref, sab_ref, qtb_ref)]
        cums = []
        for ci, la_ref, tri, _, _, _, _, _ in dirs:
            la_hi, la_lo = _split_hi_lo(la_ref[chunk_rows(ci), :])
            cums.append(_dot(tri, la_hi) + _dot(tri, la_lo))
        ops = []
        for (ci, _, _, _, last_row, _, _, qt_ref), cum in zip(dirs, cums):
            sl = chunk_rows(ci)
            tot = cum[last_row:last_row + 1, :]
            kc = k_ref[sl, :]
            qt = q_ref[sl, :] * (GLA_DK ** -0.5) * jnp.exp(cum)
            qt_ref[sl, :] = qt.astype(BF16)
            v_rows = _heads_to_rows(v_ref[sl, :])
            ops.append((tot, tile_heads(qt), tile_heads(kc * jnp.exp(-cum)),
                        tile_heads(kc * jnp.exp(tot - cum)), v_rows))
        atts = [_dot_nt(q4, k4) for _, q4, k4, _, _ in ops]
        incs = []
        for (_, _, _, keep, _, _, _, _), (_, _, _, kd4, v_rows), att in zip(dirs, ops, atts):
            att = jnp.where(keep, att, 0.0).astype(BF16)
            incs.append((_dot(att, v_rows.astype(BF16)), _dot(v_rows.T.astype(BF16), kd4)))
        for (ci, _, _, _, _, st_ref, snap_ref, _), (tot, _, _, _, _), (o_intra, st_inc) in zip(dirs, ops, incs):
            oacc_ref[ci] += o_intra
            st = st_ref[...]
            snap_ref[ci] = st.astype(BF16)
            st_ref[...] = jnp.exp(tot) * st + st_inc
        return carry

    def tile_heads_bf16(x):
        return jnp.concatenate([x] * GLA_HEADS, axis=0) * head_mask

    def out_step(i, carry):
        chunks = [OUT_UNROLL * i + u for u in range(OUT_UNROLL)]
        inter = []
        for ci in chunks:
            sl = chunk_rows(ci)
            q4 = jnp.concatenate([tile_heads_bf16(qtf_ref[sl, :]), tile_heads_bf16(qtb_ref[sl, :])], axis=1)
            st = jnp.concatenate([saf_ref[ci], sab_ref[ci]], axis=1)
            inter.append(_dot_nt(q4, st))
        for ci, o_inter in zip(chunks, inter):
            sl = chunk_rows(ci)
            on = _rms_norm(oacc_ref[ci] + o_inter, norm_g)
            gate = _silu(_heads_to_rows(g_ref[sl, :]))
            out_ref[sl, :] = _rows_to_heads(on * gate, c)
        return carry

    lax.fori_loop(0, n // SCAN_UNROLL, scan_step, 0)
    lax.fori_loop(0, n // OUT_UNROLL, out_step, 0)
    sf_ref[...] = stf_ref[...].T
    sb_ref[...] = stb_ref[...].T


def _gla(gla_in, lora, waf, baf, wab, bab, norm_g, s0f=None, s0b=None):
    b, t, _ = gla_in.shape
    has_init = s0f is not None
    n = t // GLA_CHUNK
    bmap = lambda i: (i, 0, 0)
    full = lambda i: (0, 0)
    in_specs = [pl.BlockSpec((None, t, GLA_QK), lambda i: (i, 0, 0)),
                pl.BlockSpec((None, t, GLA_QK), lambda i: (i, 0, 1)),
                pl.BlockSpec((None, t, GLA_V), lambda i: (i, 0, 1)),
                pl.BlockSpec((None, t, GLA_V), lambda i: (i, 0, 2)),
                pl.BlockSpec((None, t, 2 * GLA_LORA), bmap),
                pl.BlockSpec((2 * GLA_LORA, GLA_QK), full),
                pl.BlockSpec((1, GLA_QK), full),
                pl.BlockSpec((2 * GLA_LORA, GLA_QK), full),
                pl.BlockSpec((1, GLA_QK), full),
                pl.BlockSpec((1, GLA_DV), full)]
    args = [gla_in, gla_in, gla_in, gla_in, lora, waf, baf, wab, bab, norm_g]
    if has_init:
        in_specs += [pl.BlockSpec((None, GLA_QK, GLA_DV), bmap)] * 2
        args += [s0f, s0b]
    return pl.pallas_call(
        functools.partial(_gla_kernel, has_init),
        out_shape=(jax.ShapeDtypeStruct((b, t, GLA_V), F32),
                   jax.ShapeDtypeStruct((b, GLA_QK, GLA_DV), F32),
                   jax.ShapeDtypeStruct((b, GLA_QK, GLA_DV), F32)),
        grid=(b,),
        in_specs=in_specs,
        out_specs=(pl.BlockSpec((None, t, GLA_V), bmap),
                   pl.BlockSpec((None, GLA_QK, GLA_DV), bmap),
                   pl.BlockSpec((None, GLA_QK, GLA_DV), bmap)),
        scratch_shapes=[pltpu.VMEM((t, GLA_QK), F32),
                        pltpu.VMEM((t, GLA_QK), F32),
                        pltpu.VMEM((n, GLA_HEADS * GLA_CHUNK, GLA_DV), F32),
                        pltpu.VMEM((t, GLA_QK), BF16),
                        pltpu.VMEM((t, GLA_QK), BF16),
                        pltpu.VMEM((n, GLA_DV, GLA_QK), BF16),
                        pltpu.VMEM((n, GLA_DV, GLA_QK), BF16),
                        pltpu.VMEM((GLA_DV, GLA_QK), F32),
                        pltpu.VMEM((GLA_DV, GLA_QK), F32)],
        compiler_params=pltpu.CompilerParams(dimension_semantics=("arbitrary",),
                                             vmem_limit_bytes=VMEM_LIMIT),
        name="gla",
    )(*args)


def _dup_groups(x):
    lo = lax.broadcasted_iota(jnp.int32, x.shape, 1) < SWA_HEAD_DIM
    xr = pltpu.roll(x, SWA_HEAD_DIM, axis=1)
    return jnp.where(lo, x, xr), jnp.where(lo, xr, x)


def _pairs_attention(qps, sinks, k_dups, vt_dups, mask):
    nq = qps[0].shape[0]
    lo = lax.broadcasted_iota(jnp.int32, (nq, LANES), 1) < SWA_HEAD_DIM
    even = lax.broadcasted_iota(jnp.int32, (1, 2 * nq), 1) < nq
    scores = []
    for qp, k_dup in zip(qps, k_dups):
        q2 = jnp.concatenate([jnp.where(lo, qp, 0.0), jnp.where(lo, 0.0, qp)], axis=0).astype(BF16)
        scores.append(_dot_nt(k_dup, q2))
    probs = []
    for s, (sink_even, sink_odd) in zip(scores, sinks):
        if mask is not None:
            s = jnp.where(mask, s, NEG_INF)
        sink = jnp.where(even, sink_even, sink_odd)
        m = jnp.maximum(jnp.max(s, axis=0, keepdims=True), sink)
        p = jnp.exp(s - m)
        denom = jnp.sum(p, axis=0, keepdims=True) + jnp.exp(sink - m)
        probs.append((p.astype(BF16), 1.0 / denom))
    outs = []
    for (p, rdenom), vt_dup in zip(probs, vt_dups):
        o = _dot(vt_dup, p) * rdenom
        outs.append(jnp.concatenate([o[:SWA_HEAD_DIM, :nq], o[SWA_HEAD_DIM:, nq:]], axis=0).T)
    return outs


def _attn_ctx_kernel(sink_ref, q_ref, k_ref, v_ref, o_ref):
    kd = [x.astype(BF16) for x in _dup_groups(k_ref[...])]
    vt = [x.T.astype(BF16) for x in _dup_groups(v_ref[...])]
    scale = SWA_HEAD_DIM ** -0.5
    pairs = range(SWA_HEADS // 2)
    outs = _pairs_attention([q_ref[:, pr * LANES:(pr + 1) * LANES] * scale for pr in pairs],
                            [(sink_ref[2 * pr], sink_ref[2 * pr + 1]) for pr in pairs],
                            [kd[pr // 2] for pr in pairs], [vt[pr // 2] for pr in pairs], None)
    for pr in pairs:
        o_ref[:, pr * LANES:(pr + 1) * LANES] = outs[pr]


def _attn_ctx(sink, q, k, v):
    b, t, _ = q.shape
    bmap = lambda i: (i, 0, 0)
    return pl.pallas_call(
        _attn_ctx_kernel,
        out_shape=jax.ShapeDtypeStruct((b, t, SWA_Q), F32),
        grid=(b,),
        in_specs=[pl.BlockSpec(memory_space=pltpu.SMEM),
                  pl.BlockSpec((None, t, SWA_Q), bmap),
                  pl.BlockSpec((None, t, SWA_KV), bmap),
                  pl.BlockSpec((None, t, SWA_KV), bmap)],
        out_specs=pl.BlockSpec((None, t, SWA_Q), bmap),
        compiler_params=pltpu.CompilerParams(dimension_semantics=("arbitrary",),
                                             vmem_limit_bytes=VMEM_LIMIT),
        name="attn_ctx",
    )(sink, q, k, v)


def _rope(x, cos, sin_lo, sin_hi):
    return x * cos + pltpu.roll(x, LANES - 16, axis=1) * sin_lo + pltpu.roll(x, 16, axis=1) * sin_hi


def _attn_lat_kernel(sink_ref, q_ref, k_ref, v_ref, kc_ref, vc_ref, cos_ref, sl_ref, sh_ref,
                     o_ref, kw_ref, vw_ref):
    t = q_ref.shape[0]
    ab = ATTN_BLOCK
    nb = t // ab
    scale = SWA_HEAD_DIM ** -0.5

    k_rot = _dup_groups(_rope(k_ref[...], cos_ref[...], sl_ref[...], sh_ref[...]))
    v_dup = _dup_groups(v_ref[...])
    zeros = jnp.zeros((ab, LANES), BF16)
    for grp in range(SWA_KV_HEADS):
        kw_ref[grp, 0:ab, :] = zeros
        kw_ref[grp, ab:ab + t, :] = k_rot[grp].astype(BF16)
        kw_ref[grp, ab + t:, :] = zeros
        vw_ref[grp, 0] = zeros
        for blk in range(nb):
            vw_ref[grp, blk + 1] = v_dup[grp][blk * ab:(blk + 1) * ab, :].T.astype(BF16)
        vw_ref[grp, nb + 1] = zeros
    kc = [x.astype(BF16) for x in _dup_groups(kc_ref[...])]
    vct = [x.T.astype(BF16) for x in _dup_groups(vc_ref[...])]
    lc = kc_ref.shape[0]

    key = lax.broadcasted_iota(jnp.int32, (lc + 3 * ab, 2 * ab), 0) - lc
    tq = lax.broadcasted_iota(jnp.int32, (lc + 3 * ab, 2 * ab), 1) & (ab - 1)
    band = (key < 0) | (jnp.abs(tq + ab - key) <= ab)

    def block(nq, carry):
        row0 = pl.multiple_of(nq * ab, ab)
        s_abs = key + (nq - 1) * ab
        mask = band & ((key < 0) | ((s_abs >= 0) & (s_abs < t)))
        cos = cos_ref[pl.ds(row0, ab), :]
        s_lo = sl_ref[pl.ds(row0, ab), :]
        s_hi = sh_ref[pl.ds(row0, ab), :]
        k_all = [jnp.concatenate([kc[grp], kw_ref[grp, pl.ds(row0, 3 * ab), :]], axis=0)
                 for grp in range(SWA_KV_HEADS)]
        vt_all = [jnp.concatenate([vct[grp], vw_ref[grp, nq], vw_ref[grp, nq + 1], vw_ref[grp, nq + 2]],
                                  axis=1) for grp in range(SWA_KV_HEADS)]
        pairs = range(SWA_HEADS // 2)
        qps = [_rope(q_ref[pl.ds(row0, ab), pr * LANES:(pr + 1) * LANES], cos, s_lo, s_hi) * scale
               for pr in pairs]
        outs = _pairs_attention(qps, [(sink_ref[2 * pr], sink_ref[2 * pr + 1]) for pr in pairs],
                                [k_all[pr // 2] for pr in pairs], [vt_all[pr // 2] for pr in pairs], mask)
        for pr in pairs:
            o_ref[pl.ds(row0, ab), pr * LANES:(pr + 1) * LANES] = outs[pr]
        return carry

    lax.fori_loop(0, nb, block, 0)


def _attn_lat(sink, q, k, v, kc, vc, cos, sin_lo, sin_hi):
    b, t, _ = q.shape
    lc = kc.shape[1]
    bmap = lambda i: (i, 0, 0)
    full = lambda i: (0, 0)
    return pl.pallas_call(
        _attn_lat_kernel,
        out_shape=jax.ShapeDtypeStruct((b, t, SWA_Q), F32),
        grid=(b,),
        in_specs=[pl.BlockSpec(memory_space=pltpu.SMEM),
                  pl.BlockSpec((None, t, SWA_Q), bmap),
                  pl.BlockSpec((None, t, SWA_KV), bmap),
                  pl.BlockSpec((None, t, SWA_KV), bmap),
                  pl.BlockSpec((None, lc, SWA_KV), bmap),
                  pl.BlockSpec((None, lc, SWA_KV), bmap),
                  pl.BlockSpec((t, LANES), full),
                  pl.BlockSpec((t, LANES), full),
                  pl.BlockSpec((t, LANES), full)],
        out_specs=pl.BlockSpec((None, t, SWA_Q), bmap),
        scratch_shapes=[pltpu.VMEM((SWA_KV_HEADS, t + 2 * ATTN_BLOCK, LANES), BF16),
                        pltpu.VMEM((SWA_KV_HEADS, t // ATTN_BLOCK + 2, LANES, ATTN_BLOCK), BF16)],
        compiler_params=pltpu.CompilerParams(dimension_semantics=("arbitrary",),
                                             vmem_limit_bytes=VMEM_LIMIT),
        name="attn_lat",
    )(sink, q, k, v, kc, vc, cos, sin_lo, sin_hi)


def _rope_tables(t):
    half = SWA_HEAD_DIM // 2
    quarter = half // 2
    pos = jnp.arange(t)
    row = (pos // GRID_W).astype(F32)
    col = (pos % GRID_W).astype(F32)
    inv_freq = ROPE_BASE ** (-jnp.arange(quarter, dtype=F32) / quarter)
    lane = jnp.arange(LANES)
    d = lane % SWA_HEAD_DIM
    freq = inv_freq[d % quarter]
    use_row = (d < half)
    ang = jnp.where(use_row[None, :], row[:, None], col[:, None]) * freq[None, :]
    cos = jnp.cos(ang)
    sin = jnp.sin(ang)
    lower = (d % half) < quarter
    return cos, jnp.where(lower[None, :], -sin, 0.0), jnp.where(lower[None, :], 0.0, sin)


def _route(sel, scores):
    n = sel.shape[1]
    gsz = N_EXPERTS // N_EXPERT_GROUPS

    def first_max(x, idx, size):
        m = jnp.max(x, axis=0, keepdims=True)
        first = jnp.min(jnp.where(x == m, idx, float(size)), axis=0, keepdims=True)
        return m, idx == first

    i8 = lax.broadcasted_iota(jnp.int32, (gsz, n), 0).astype(F32)
    rows = []
    for g in range(N_EXPERT_GROUPS):
        slab = sel[g * gsz:(g + 1) * gsz, :]
        m1, hit = first_max(slab, i8, gsz)
        m2 = jnp.max(jnp.where(hit, NEG_INF, slab), axis=0, keepdims=True)
        rows.append(m1 + m2)
    gscore = jnp.concatenate(rows, axis=0)
    gsel = jnp.zeros((N_EXPERT_GROUPS, n), F32)
    for _ in range(TOPK_GROUPS):
        _, hit = first_max(gscore, i8, N_EXPERT_GROUPS)
        gsel = jnp.where(hit, 1.0, gsel)
        gscore = jnp.where(hit, NEG_INF, gscore)
    emask = jnp.concatenate(
        [jnp.broadcast_to(gsel[g:g + 1, :], (gsz, n)) for g in range(N_EXPERT_GROUPS)], axis=0)
    cand = jnp.where(emask > 0.5, sel, NEG_INF)
    ie = lax.broadcasted_iota(jnp.int32, (N_EXPERTS, n), 0).astype(F32)
    w = jnp.zeros((N_EXPERTS, n), F32)
    chosen = jnp.zeros((N_EXPERTS, n), F32)
    hits = []
    for _ in range(TOP_K):
        _, hit = first_max(cand, ie, N_EXPERTS)
        hits.append(hit)
        w = jnp.where(hit, scores, w)
        chosen = jnp.where(hit, 1.0, chosen)
        cand = jnp.where(hit, NEG_INF, cand)
    gates = w / jnp.sum(w, axis=0, keepdims=True) * ROUTED_SCALE

    s_idx = lax.broadcasted_iota(jnp.int32, (n, n), 0)
    t_idx = lax.broadcasted_iota(jnp.int32, (n, n), 1)
    before = jnp.where(s_idx < t_idx, 1.0, 0.0).astype(BF16)
    rank = _dot(chosen.astype(BF16), before)
    count = jnp.sum(chosen, axis=1, keepdims=True)
    padded = jnp.floor((count + (SORT_ALIGN - 1)) * (1.0 / SORT_ALIGN)) * SORT_ALIGN
    padded = jnp.broadcast_to(padded, (N_EXPERTS, LANES))
    e_row = lax.broadcasted_iota(jnp.int32, (N_EXPERTS, N_EXPERTS), 0)
    e_col = lax.broadcasted_iota(jnp.int32, (N_EXPERTS, N_EXPERTS), 1)
    below = jnp.where(e_col < e_row, 1.0, 0.0).astype(BF16)
    start = _dot(below, padded.astype(BF16))
    row = start[:, 0:1] + rank
    pos = jnp.concatenate([jnp.sum(jnp.where(h, row, 0.0), axis=0, keepdims=True) for h in hits], axis=0)
    wts = jnp.concatenate([jnp.sum(jnp.where(h, gates, 0.0), axis=0, keepdims=True) for h in hits], axis=0)
    return pos, wts, padded, start


def _outproj_kernel(gla_ref, att_ref, x_ref, wo_ref, g1_ref, sh_ref, sc_ref, ng_ref, rw_ref, rwh_ref,
                    rb_ref, x1_ref, xm_ref, pos_ref, wts_ref, cnt_ref, start_ref):
    y = (_dot(gla_ref[...].astype(BF16), wo_ref[0:GLA_V, :])
         + _dot(att_ref[...].astype(BF16), wo_ref[GLA_V:, :]))
    x1 = x_ref[...] + g1_ref[...] * y
    x1_ref[...] = x1
    xm = _rms_norm(x1, ng_ref[...]) * (1.0 + sc_ref[...]) + sh_ref[...]
    xm_hi, xm_lo = _split_hi_lo(xm)
    xm_ref[...] = xm_hi
    lg = _dot(xm_hi, rw_ref[...])
    logits = lg[:, :N_EXPERTS] + lg[:, N_EXPERTS:] + _dot(xm_lo, rwh_ref[...])
    tm = logits.shape[0]
    lt = jnp.concatenate([logits, jnp.zeros((tm, LANES - N_EXPERTS), F32)], axis=1).T[:N_EXPERTS, :]
    scores = _sigmoid(lt)
    pos_ref[...], wts_ref[...], cnt_ref[...], start_ref[...] = _route(scores + rb_ref[...], scores)


def _outproj(gla_out, att_out, x, w_out, g1, sh2, sc2, norm_g, rw_cat, rw_hi, rbias, *, tm):
    b, t, d = x.shape
    nmod = g1.shape[0]
    mod_map = (lambda i, j: (i, 0, 0)) if nmod > 1 else (lambda i, j: (0, 0, 0))
    row = lambda i, j: (i, j, 0)
    full = lambda i, j: (0, 0)
    tile = lambda i, j: (i, j, 0, 0)
    nt = t // tm
    return pl.pallas_call(
        _outproj_kernel,
        out_shape=(jax.ShapeDtypeStruct((b, t, d), F32),
                   jax.ShapeDtypeStruct((b, t, d), BF16),
                   jax.ShapeDtypeStruct((b, nt, TOP_K, tm), F32),
                   jax.ShapeDtypeStruct((b, nt, TOP_K, tm), F32),
                   jax.ShapeDtypeStruct((b, nt, N_EXPERTS, LANES), F32),
                   jax.ShapeDtypeStruct((b, nt, N_EXPERTS, LANES), F32)),
        grid=(b, t // tm),
        in_specs=[pl.BlockSpec((None, tm, GLA_V), row),
                  pl.BlockSpec((None, tm, SWA_Q), row),
                  pl.BlockSpec((None, tm, d), row),
                  pl.BlockSpec((d, d), full),
                  pl.BlockSpec((None, 1, d), mod_map),
                  pl.BlockSpec((None, 1, d), mod_map),
                  pl.BlockSpec((None, 1, d), mod_map),
                  pl.BlockSpec((1, d), full),
                  pl.BlockSpec((d, 2 * N_EXPERTS), full),
                  pl.BlockSpec((d, N_EXPERTS), full),
                  pl.BlockSpec((N_EXPERTS, 1), full)],
        out_specs=(pl.BlockSpec((None, tm, d), row),
                   pl.BlockSpec((None, tm, d), row),
                   pl.BlockSpec((None, None, TOP_K, tm), tile),
                   pl.BlockSpec((None, None, TOP_K, tm), tile),
                   pl.BlockSpec((None, None, N_EXPERTS, LANES), tile),
                   pl.BlockSpec((None, None, N_EXPERTS, LANES), tile)),
        compiler_params=pltpu.CompilerParams(dimension_semantics=("arbitrary", "arbitrary"),
                                             vmem_limit_bytes=VMEM_LIMIT),
        name="outproj",
    )(gla_out, att_out, x, w_out, g1, sh2, sc2, norm_g, rw_cat, rw_hi, rbias)


MOE_TILE = 256
SORT_ALIGN = 16
SORT_ROWS = 3072
ROW_TILE = 512
GATHER_SLOTS = 3
FFN_CHAINS = 4
COMBINE_CHUNK = 1024
ALWAYS_ROWS = 2560
COMBINE_TAIL = 512


def _moe_sort_kernel(tiles_a, used_ref, xa_ref, xb_ref, pos_ref, xs_ref):
    i = pl.program_id(0)
    x = jnp.where(i < tiles_a, xa_ref[...], xb_ref[...])
    pos = pos_ref[...]
    tm = x.shape[0]
    used = used_ref[i]

    def fill(blk):
        rows = (lax.broadcasted_iota(jnp.int32, (tm, tm), 0) + blk * tm).astype(F32)
        onehot = jnp.zeros((tm, tm), F32)
        for k in range(TOP_K):
            onehot = jnp.where(rows == pos[k:k + 1, :], 1.0, onehot)
        xs_ref[blk * tm:(blk + 1) * tm, :] = _dot(onehot.astype(BF16), x).astype(BF16)

    for blk in range(SORT_ROWS // tm):
        if (blk + 1) * tm <= ALWAYS_ROWS:
            fill(blk)
        else:
            pl.when(blk * tm < used)(functools.partial(fill, blk))

            @pl.when(blk * tm >= used)
            def _():
                xs_ref[blk * tm:(blk + 1) * tm, :] = jnp.zeros((tm, D_MODEL), BF16)


def _moe_sort(xm_a, xm_b, pos, used):
    d = xm_a.shape[1]
    nt, _, tm = pos.shape
    tiles_a = xm_a.shape[0] // tm
    grid_spec = pltpu.PrefetchScalarGridSpec(
        num_scalar_prefetch=1,
        grid=(nt,),
        in_specs=[pl.BlockSpec((tm, d), lambda i, u: (jnp.minimum(i, tiles_a - 1), 0)),
                  pl.BlockSpec((tm, d), lambda i, u: (jnp.maximum(i - tiles_a, 0), 0)),
                  pl.BlockSpec((None, TOP_K, tm), lambda i, u: (i, 0, 0))],
        out_specs=pl.BlockSpec((SORT_ROWS, d), lambda i, u: (i, 0)))
    return pl.pallas_call(
        functools.partial(_moe_sort_kernel, tiles_a),
        out_shape=jax.ShapeDtypeStruct((nt * SORT_ROWS, d), BF16),
        grid_spec=grid_spec,
        compiler_params=pltpu.CompilerParams(dimension_semantics=("arbitrary",),
                                             vmem_limit_bytes=VMEM_LIMIT),
        name="moe_sort",
    )(used, xm_a, xm_b, pos)


def _moe_row_tiles(n_tokens):
    rows = n_tokens * TOP_K + (n_tokens // MOE_TILE) * N_EXPERTS * (SORT_ALIGN - 1) + N_EXPERTS * (ROW_TILE - 1)
    return -(-rows // ROW_TILE) + GATHER_SLOTS - 1


PLAN_CHUNK = 1280


def _int_dot_r(a, onehot):
    hi = jnp.floor(a * (1.0 / 256.0))
    return _dot(hi.astype(BF16), onehot) * 256.0 + _dot((a - hi * 256.0).astype(BF16), onehot)


def _int_dot_l(onehot, b):
    hi = jnp.floor(b * (1.0 / 256.0))
    return _dot(onehot, hi.astype(BF16)) * 256.0 + _dot(onehot, (b - hi * 256.0).astype(BF16))


def _moe_plan_kernel(cnt_ref, start_ref, src_ref, first_ref, tiles_ref, nu_ref, back_ref):
    nt, ne = cnt_ref.shape
    gpt = SORT_ROWS // SORT_ALIGN
    gpr = ROW_TILE // SORT_ALIGN
    gc = cnt_ref[...] * (1.0 / SORT_ALIGN)
    ls = start_ref[...] * (1.0 / SORT_ALIGN)

    def transpose(x):
        x = jnp.concatenate([x, jnp.zeros((nt, LANES - ne), F32)], axis=1)
        x = jnp.concatenate([x, jnp.zeros((LANES - nt, LANES), F32)], axis=0)
        return x.T[:ne, :nt]

    def tri(n, keep):
        return jnp.where(keep(lax.broadcasted_iota(jnp.int32, (n, n), 0),
                              lax.broadcasted_iota(jnp.int32, (n, n), 1)), 1.0, 0.0).astype(BF16)

    gc_t = transpose(gc)
    ls_t = transpose(ls)
    tot_c = jnp.broadcast_to(jnp.sum(gc_t, axis=1, keepdims=True), (ne, LANES))
    ptot_c = jnp.floor((tot_c + (gpr - 1)) * (1.0 / gpr)) * gpr
    gend_c = _int_dot_l(tri(ne, lambda r, c: c <= r), ptot_c)
    gstart_c = gend_c - ptot_c
    n_used = gend_c[ne - 1:ne, :] * (1.0 / gpr)
    nu_ref[...] = n_used.astype(jnp.int32)
    tot_r = jnp.sum(gc, axis=0, keepdims=True)
    ptot_r = jnp.floor((tot_r + (gpr - 1)) * (1.0 / gpr)) * gpr
    gstart_r = _int_dot_r(jnp.broadcast_to(ptot_r, (8, ne)), tri(ne, lambda r, c: r < c))
    cumex = _dot(tri(nt, lambda r, c: c < r), gc.astype(BF16))
    cumex_t = _dot(gc_t.astype(BF16), tri(nt, lambda r, c: r < c))
    tile_base = lax.broadcasted_iota(jnp.int32, (nt, ne), 0).astype(F32) * gpt + ls
    table = jnp.concatenate([cumex + gc, cumex, tile_base, gstart_r, jnp.broadcast_to(tot_r, (8, ne))], axis=0)

    e_iota = lax.broadcasted_iota(jnp.int32, (ne, PLAN_CHUNK), 0).astype(F32)
    for ch in range(src_ref.shape[1] // PLAN_CHUNK):
        g = (lax.broadcasted_iota(jnp.int32, (1, PLAN_CHUNK), 1) + ch * PLAN_CHUNK).astype(F32)
        eg = jnp.sum(jnp.where(gend_c[:, 0:1] <= g, 1.0, 0.0), axis=0, keepdims=True)
        picked = _int_dot_r(table, jnp.where(e_iota == eg, 1.0, 0.0).astype(BF16))
        cum_g, cumex_g, base_g = picked[0:nt], picked[nt:2 * nt], picked[2 * nt:3 * nt]
        u = g - picked[3 * nt:3 * nt + 1]
        in_tile = (cumex_g <= u) & (u < cum_g)
        src = jnp.sum(jnp.where(in_tile, base_g - cumex_g, 0.0), axis=0, keepdims=True) + u
        src = jnp.where(u < picked[3 * nt + 8:3 * nt + 9], src, gpt - 1.0)
        src_ref[:, ch * PLAN_CHUNK:(ch + 1) * PLAN_CHUNK] = src.astype(jnp.int32)

    first_ref[...] = (gstart_c * (1.0 / gpr)).astype(jnp.int32)
    tiles_ref[...] = (ptot_c * (1.0 / gpr)).astype(jnp.int32)

    lg = lax.broadcasted_iota(jnp.int32, (ne, back_ref.shape[1]), 1).astype(F32)
    for t in range(nt):
        first = ls_t[:, t:t + 1]
        inside = (first <= lg) & (lg < first + gc_t[:, t:t + 1])
        shift = gstart_c[:, 0:1] + cumex_t[:, t:t + 1] - first
        val = jnp.sum(jnp.where(inside, shift + lg, 0.0), axis=0, keepdims=True)
        back_ref[t:t + 1, :] = val.astype(jnp.int32)


def _moe_plan(cnt, start):
    nt, ne = cnt.shape
    row_tiles = _moe_row_tiles(nt * MOE_TILE)
    gpt = SORT_ROWS // SORT_ALIGN
    gpr = ROW_TILE // SORT_ALIGN
    n_src = -(-(row_tiles * gpr) // PLAN_CHUNK) * PLAN_CHUNK
    n_back = -(-gpt // LANES) * LANES
    src, first, tiles, nu, back = pl.pallas_call(
        _moe_plan_kernel,
        out_shape=(jax.ShapeDtypeStruct((1, n_src), jnp.int32),
                   jax.ShapeDtypeStruct((ne, LANES), jnp.int32),
                   jax.ShapeDtypeStruct((ne, LANES), jnp.int32),
                   jax.ShapeDtypeStruct((1, LANES), jnp.int32),
                   jax.ShapeDtypeStruct((nt, n_back), jnp.int32)),
        compiler_params=pltpu.CompilerParams(vmem_limit_bytes=VMEM_LIMIT),
        name="moe_plan",
    )(cnt, start)
    return nu[0, :1], first[:, 0], tiles[:, 0], src[0, :row_tiles * gpr], back[:, :gpt]


def _moe_experts_kernel(nu_ref, first_ref, tiles_ref, src_ref, xs_hbm, wg_ref, wu_ref, wd_ref, ys_hbm,
                        xbuf, ybuf, gsem, osem, wgu_s, wd_s):
    e = pl.program_id(0)
    n_used = nu_ref[0]
    gpr = ROW_TILE // SORT_ALIGN
    part = ROW_TILE // FFN_CHAINS

    def gather(tile, to_slot, j0=0, j1=gpr):
        for j in range(j0, j1):
            row = pl.multiple_of(src_ref[tile * gpr + j] * SORT_ALIGN, SORT_ALIGN)
            pltpu.make_async_copy(xs_hbm.at[pl.ds(row, SORT_ALIGN), :],
                                  xbuf.at[to_slot, j * SORT_ALIGN:(j + 1) * SORT_ALIGN, :],
                                  gsem.at[to_slot]).start(priority=j % 2)

    def drain(of_slot):
        for j in range(gpr):
            pltpu.make_async_copy(xs_hbm.at[0:SORT_ALIGN, :],
                                  xbuf.at[of_slot, j * SORT_ALIGN:(j + 1) * SORT_ALIGN, :], gsem.at[of_slot]).wait()

    def out_copy(tile, of_slot):
        row = pl.multiple_of(tile * ROW_TILE, ROW_TILE)
        return pltpu.make_async_copy(ybuf.at[of_slot], ys_hbm.at[pl.ds(row, ROW_TILE), :], osem.at[of_slot])

    @pl.when(e == 0)
    def _():
        gather(0, 0)
        gather(1, 1)

    wgu_s[:, :EXPERT_FF] = wg_ref[...].astype(BF16)
    wgu_s[:, EXPERT_FF:] = wu_ref[...].astype(BF16)
    wd_s[...] = wd_ref[...].astype(BF16)

    def row_tile(i, carry):
        r = first_ref[e] + i
        slot = lax.rem(r, GATHER_SLOTS)
        oslot = lax.rem(r, 2)
        next_slot = lax.rem(r + 2, GATHER_SLOTS)
        drain(slot)

        @pl.when(r >= 2)
        def _():
            out_copy(r - 2, oslot).wait()

        abs_ = []
        for c in range(FFN_CHAINS):
            abs_.append(_dot(xbuf[slot, c * part:(c + 1) * part, :], wgu_s[...]))
            gather(r + 2, next_slot, c * gpr // FFN_CHAINS, (c + 1) * gpr // FFN_CHAINS)
        hs = [(_silu(ab[:, :EXPERT_FF]) * ab[:, EXPERT_FF:]).astype(BF16) for ab in abs_]
        ys = [_dot(h, wd_s[...]).astype(BF16) for h in hs]
        for c in range(FFN_CHAINS):
            ybuf[oslot, c * part:(c + 1) * part, :] = ys[c]
        out_copy(r, oslot).start()
        return carry

    lax.fori_loop(0, tiles_ref[e], row_tile, 0)

    @pl.when(e == pl.num_programs(0) - 1)
    def _():
        drain(lax.rem(n_used, GATHER_SLOTS))
        drain(lax.rem(n_used + 1, GATHER_SLOTS))
        out_copy(n_used - 1, lax.rem(n_used - 1, 2)).wait()

        @pl.when(n_used >= 2)
        def _():
            out_copy(n_used - 2, lax.rem(n_used, 2)).wait()


def _moe_experts(n_used, first, tiles, src, xs, wg, wu, wd, row_tiles):
    d = xs.shape[-1]
    ne = wg.shape[0]
    w_map = lambda e, nu, fi, ti, sr: (e, 0, 0)
    grid_spec = pltpu.PrefetchScalarGridSpec(
        num_scalar_prefetch=4,
        grid=(ne,),
        in_specs=[pl.BlockSpec(memory_space=pl.ANY),
                  pl.BlockSpec((None, d, EXPERT_FF), w_map),
                  pl.BlockSpec((None, d, EXPERT_FF), w_map),
                  pl.BlockSpec((None, EXPERT_FF, d), w_map)],
        out_specs=pl.BlockSpec(memory_space=pl.ANY),
        scratch_shapes=[pltpu.VMEM((GATHER_SLOTS, ROW_TILE, d), BF16),
                        pltpu.VMEM((2, ROW_TILE, d), BF16),
                        pltpu.SemaphoreType.DMA((GATHER_SLOTS,)),
                        pltpu.SemaphoreType.DMA((2,)),
                        pltpu.VMEM((d, 2 * EXPERT_FF), BF16),
                        pltpu.VMEM((EXPERT_FF, d), BF16)])
    return pl.pallas_call(
        _moe_experts_kernel,
        out_shape=jax.ShapeDtypeStruct((row_tiles * ROW_TILE, d), BF16),
        grid_spec=grid_spec,
        compiler_params=pltpu.CompilerParams(dimension_semantics=("arbitrary",),
                                             vmem_limit_bytes=VMEM_LIMIT),
        name="moe_experts",
    )(n_used, first, tiles, src, xs, wg, wu, wd)


def _moe_combine_kernel(back_ref, used_ref, ys_hbm, pos_ref, wts_ref, xm_ref, x1_ref, g2_ref, fg_ref,
                        swg_ref, swu_ref, swd_ref, o_ref, buf, sem, acc_ref):
    i = pl.program_id(0)
    gpt = SORT_ROWS // SORT_ALIGN
    slot = lax.rem(i, 2)
    always = ALWAYS_ROWS
    tail = range(always, SORT_ROWS, COMBINE_TAIL)

    def copies(tile, of_slot, g0, g1, start):
        for g in range(g0, g1):
            row = pl.multiple_of(back_ref[tile * gpt + g] * SORT_ALIGN, SORT_ALIGN) if start else 0
            cp = pltpu.make_async_copy(ys_hbm.at[pl.ds(row, SORT_ALIGN), :],
                                       buf.at[of_slot, g * SORT_ALIGN:(g + 1) * SORT_ALIGN, :], sem.at[of_slot])
            if start:
                cp.start(priority=g % 2)
            else:
                cp.wait()

    def transfer(tile, of_slot, start):
        copies(tile, of_slot, 0, always // SORT_ALIGN, start)
        for c0 in tail:
            pl.when(c0 < used_ref[tile])(functools.partial(
                copies, tile, of_slot, c0 // SORT_ALIGN, (c0 + COMBINE_TAIL) // SORT_ALIGN, start))

    @pl.when(i == 0)
    def _():
        transfer(0, 0, True)

    @pl.when(i + 1 < pl.num_programs(0))
    def _():
        transfer(i + 1, 1 - slot, True)

    x = xm_ref[...]
    tm = x.shape[0]
    shared = _dot((_silu(_dot(x, swg_ref[...])) * _dot(x, swu_ref[...])).astype(BF16), swd_ref[...])
    pad = jnp.zeros((LANES - TOP_K, tm), F32)
    pos_t = jnp.concatenate([pos_ref[...], pad], axis=0).T
    wts_t = jnp.concatenate([wts_ref[...], pad], axis=0).T
    transfer(i, slot, False)

    pos_b = [jnp.broadcast_to(pos_t[:, k:k + 1], (tm, LANES)) for k in range(TOP_K)]
    wts_b = [jnp.broadcast_to(wts_t[:, k:k + 1], (tm, LANES)) for k in range(TOP_K)]

    def apply(c0, width):
        reps = width // LANES
        rows = (lax.broadcasted_iota(jnp.int32, (tm, width), 1) + c0).astype(F32)
        comb = jnp.zeros((tm, width), F32)
        for k in range(TOP_K):
            comb = jnp.where(rows == jnp.concatenate([pos_b[k]] * reps, axis=1),
                             jnp.concatenate([wts_b[k]] * reps, axis=1), comb)
        return _dot(comb.astype(BF16), buf[slot, c0:c0 + width, :])

    routed = shared
    for c0 in range(0, always, COMBINE_CHUNK):
        routed = routed + apply(c0, min(COMBINE_CHUNK, always - c0))
    acc_ref[...] = routed
    for c0 in tail:
        @pl.when(c0 < used_ref[i])
        def _(c0=c0):
            acc_ref[...] += apply(c0, COMBINE_TAIL)
    y = x1_ref[...] + g2_ref[...] * acc_ref[...]
    o_ref[...] = _rms_norm(y, fg_ref[...])


def _moe_combine(back, used, ys, pos, wts, xm, x1, g2, final_g, swg, swu, swd, *, tiles_per_mod):
    n, d = xm.shape
    tm = pos.shape[-1]
    nt = n // tm
    gpt = SORT_ROWS // SORT_ALIGN
    row = lambda i, bk, us: (i, 0)
    full = lambda i, bk, us: (0, 0)
    tile = lambda i, bk, us: (i, 0, 0)
    mod_map = lambda i, bk, us: (i // tiles_per_mod, 0, 0)
    grid_spec = pltpu.PrefetchScalarGridSpec(
        num_scalar_prefetch=2,
        grid=(nt,),
        in_specs=[pl.BlockSpec(memory_space=pl.ANY),
                  pl.BlockSpec((None, TOP_K, tm), tile),
                  pl.BlockSpec((None, TOP_K, tm), tile),
                  pl.BlockSpec((tm, d), row),
                  pl.BlockSpec((tm, d), row),
                  pl.BlockSpec((None, 1, d), mod_map),
                  pl.BlockSpec((1, d), full),
                  pl.BlockSpec((d, SHARED_FF), full),
                  pl.BlockSpec((d, SHARED_FF), full),
                  pl.BlockSpec((SHARED_FF, d), full)],
        out_specs=pl.BlockSpec((tm, d), row),
        scratch_shapes=[pltpu.VMEM((2, SORT_ROWS, d), BF16),
                        pltpu.SemaphoreType.DMA((2,)),
                        pltpu.VMEM((tm, d), F32)])
    return pl.pallas_call(
        _moe_combine_kernel,
        out_shape=jax.ShapeDtypeStruct((n, d), F32),
        grid_spec=grid_spec,
        compiler_params=pltpu.CompilerParams(dimension_semantics=("arbitrary",),
                                             vmem_limit_bytes=VMEM_LIMIT),
        name="moe_combine",
    )(back, used, ys, pos.reshape(nt, TOP_K, tm), wts.reshape(nt, TOP_K, tm), xm, x1, g2, final_g, swg, swu, swd)


def _mix(x, mods, p, attn_fn, s0=None):
    sh1, sc1, g1, sh2, sc2, _ = mods
    gla_in, lora, q_s, k_s, v_s = _inproj(x, p["norm_attn_g"], sh1, sc1, p["w_gla"], p["w_lora"],
                                          p["w_swa"], tm=256)
    if s0 is None:
        gla_out, s_f, s_b = _gla(gla_in, lora, p["waf"], p["baf"], p["wab"], p["bab"], p["gla_norm_g"])
    else:
        gla_out, s_f, s_b = _gla(gla_in, lora, p["waf"], p["baf"], p["wab"], p["bab"], p["gla_norm_g"],
                                 s0[0], s0[1])
    att_out = attn_fn(q_s, k_s, v_s)
    routed = _outproj(gla_out, att_out, x, p["w_out"], g1, sh2, sc2, p["norm_ffn_g"],
                      p["rw_cat"], p["rw_hi"], p["rbias"], tm=MOE_TILE)
    return routed, k_s, v_s, s_f, s_b


def _moe(streams, p):
    d = D_MODEL
    (ra, _), (rb, _) = streams
    n_tiles = [r[1].shape[0] * r[1].shape[1] // MOE_TILE for r, _ in streams]
    pos_all = jnp.concatenate([r[2].reshape(-1, TOP_K, MOE_TILE) for r, _ in streams], axis=0)
    cnt_all = jnp.concatenate([r[4][..., 0].reshape(-1, N_EXPERTS) for r, _ in streams], axis=0)
    start_all = jnp.concatenate([r[5][..., 0].reshape(-1, N_EXPERTS) for r, _ in streams], axis=0)
    used = (start_all[:, -1] + cnt_all[:, -1]).astype(jnp.int32)
    xs = _moe_sort(ra[1].reshape(-1, d), rb[1].reshape(-1, d), pos_all, used)
    n_used, first, tiles, src, back = _moe_plan(cnt_all, start_all)
    ys = _moe_experts(n_used, first, tiles, src, xs, p["wg"], p["wu"], p["wd"],
                      _moe_row_tiles(cnt_all.shape[0] * MOE_TILE))
    outs = []
    tile0 = 0
    for ((x1, xm, pos, wts, cnt, start), g2), nt in zip(streams, n_tiles):
        b, t, _ = x1.shape
        tiles_per_mod = (t // MOE_TILE) if g2.shape[0] > 1 else nt
        y = _moe_combine(back[tile0:tile0 + nt].reshape(-1), used[tile0:tile0 + nt], ys, pos, wts,
                         xm.reshape(-1, d), x1.reshape(-1, d), g2, p["final_norm_g"],
                         p["swg"], p["swu"], p["swd"], tiles_per_mod=tiles_per_mod)
        outs.append(y.reshape(b, t, d))
        tile0 += nt
    return outs


def kernel(x_prompt, x_sample, c, cache_swa_k, cache_swa_v, state_gla_fwd, state_gla_bwd, c_ctx, w_ada, b_ada, norm_attn_g, norm_ffn_g, w_in, gla_wa_f, gla_ba_f, gla_wa_b, gla_ba_b, gla_norm_g, swa_sink, w_out, router_w, router_bias, exp_w_gate, exp_w_up, exp_w_down, sh_w_gate, sh_w_up, sh_w_down, final_norm_g):
    l = 0
    d = D_MODEL
    nb_ctx, t_ctx, _ = x_prompt.shape
    nb_lat, t_lat, _ = x_sample.shape

    pad = jnp.zeros((8 - 1 - nb_lat, d), F32)
    cond8 = jnp.concatenate([c_ctx[None, :], c, pad], axis=0)
    mod = _adaln(cond8, w_ada[l], b_ada[l][None, :])
    mods_ctx = [mod[0:1, i * d:(i + 1) * d][:, None, :] for i in range(6)]
    mods_lat = [mod[1:1 + nb_lat, i * d:(i + 1) * d][:, None, :] for i in range(6)]

    zeros_lora = jnp.zeros((GLA_LORA, GLA_QK), F32)
    rw = router_w[l]
    rw_hi = rw.astype(BF16)
    rw_lo = (rw - rw_hi.astype(F32)).astype(BF16)
    w_in_b = w_in[l].astype(BF16)
    p = {
        "norm_attn_g": norm_attn_g[l][None, :],
        "norm_ffn_g": norm_ffn_g[l][None, :],
        "final_norm_g": final_norm_g[None, :],
        "w_gla": w_in_b[:, :2 * GLA_QK + 2 * GLA_V],
        "w_lora": w_in_b[:, 2 * GLA_QK + 2 * GLA_V:2 * GLA_QK + 2 * GLA_V + 2 * GLA_LORA],
        "w_swa": w_in_b[:, 2 * GLA_QK + 2 * GLA_V + 2 * GLA_LORA:],
        "waf": jnp.concatenate([gla_wa_f[l], zeros_lora], axis=0).astype(BF16),
        "wab": jnp.concatenate([zeros_lora, gla_wa_b[l]], axis=0).astype(BF16),
        "baf": gla_ba_f[l][None, :],
        "bab": gla_ba_b[l][None, :],
        "gla_norm_g": gla_norm_g[l][None, :],
        "w_out": w_out[l].astype(BF16),
        "rw_cat": jnp.concatenate([rw_hi, rw_lo], axis=1),
        "rw_hi": rw_hi,
        "rbias": router_bias[l][:, None],
        "wg": exp_w_gate[l], "wu": exp_w_up[l], "wd": exp_w_down[l],
        "swg": sh_w_gate[l].astype(BF16), "swu": sh_w_up[l].astype(BF16),
        "swd": sh_w_down[l].astype(BF16),
    }
    sink = swa_sink[l]

    routed_ctx, k_c, v_c, s_f, s_b = _mix(x_prompt, mods_ctx, p, functools.partial(_attn_ctx, sink))

    cos, sin_lo, sin_hi = _rope_tables(t_lat)
    kc = cache_swa_k[:, l].reshape(nb_lat, -1, SWA_KV)
    vc = cache_swa_v[:, l].reshape(nb_lat, -1, SWA_KV)
    lat_attn = lambda q, k, v: _attn_lat(sink, q, k, v, kc, vc, cos, sin_lo, sin_hi)
    s0 = (state_gla_fwd[:, l].reshape(nb_lat, GLA_QK, GLA_DV),
          state_gla_bwd[:, l].reshape(nb_lat, GLA_QK, GLA_DV))
    routed_lat, _, _, _, _ = _mix(x_sample, mods_lat, p, lat_attn, s0)
    y_prompt, y_sample = _moe([(routed_ctx, mods_ctx[5]), (routed_lat, mods_lat[5])], p)

    new_k = k_c.reshape(nb_ctx, 1, t_ctx, SWA_KV_HEADS, SWA_HEAD_DIM)
    new_v = v_c.reshape(nb_ctx, 1, t_ctx, SWA_KV_HEADS, SWA_HEAD_DIM)
    new_sf = s_f.reshape(nb_ctx, 1, GLA_HEADS, GLA_DK, GLA_DV)
    new_sb = s_b.reshape(nb_ctx, 1, GLA_HEADS, GLA_DK, GLA_DV)
    return (y_prompt, y_sample, new_k, new_v, new_sf, new_sb)
```

```python
import functools

import jax
import jax.numpy as jnp
from jax import lax
from jax.experimental import pallas as pl
from jax.experimental.pallas import tpu as pltpu

F32 = jnp.float32
BF16 = jnp.bfloat16

D_MODEL = 1024
GLA_HEADS = 4
GLA_DK = 64
GLA_DV = 128
GLA_LORA = 16
GLA_GATE_NORM = 16.0
GLA_CHUNK = 64
GLA_QK = GLA_HEADS * GLA_DK
GLA_V = GLA_HEADS * GLA_DV
SWA_HEAD_DIM = 64
SWA_HEADS = 8
SWA_KV_HEADS = 2
SWA_Q = SWA_HEADS * SWA_HEAD_DIM
SWA_KV = SWA_KV_HEADS * SWA_HEAD_DIM
ATTN_BLOCK = 128
GRID_W = 64
ROPE_BASE = 10000.0
N_EXPERTS = 64
TOP_K = 8
N_EXPERT_GROUPS = 8
TOPK_GROUPS = 4
EXPERT_FF = 128
SHARED_FF = 256
ROUTED_SCALE = 2.5
EPS = 1e-6

LANES = 128
VMEM_LIMIT = 56 * 1024 * 1024

NEG_INF = float("-inf")


def _dot(a, b):
    return jnp.dot(a, b, preferred_element_type=F32)


def _dot_nt(a, b):
    return lax.dot_general(a, b, (((1,), (1,)), ((), ())), preferred_element_type=F32)


def _split_hi_lo(x):
    hi = x.astype(BF16)
    lo = (x - hi.astype(F32)).astype(BF16)
    return hi, lo


def _sigmoid(x):
    return 1.0 / (1.0 + jnp.exp(-x))


def _silu(x):
    return x * _sigmoid(x)


def _rms_norm(x, g):
    ms = jnp.mean(x * x, axis=-1, keepdims=True)
    return x * lax.rsqrt(ms + EPS) * g


def _adaln_kernel(c_ref, w_ref, b_ref, o_ref):
    a_hi, a_lo = _split_hi_lo(_silu(c_ref[...]))
    w_hi, w_lo = _split_hi_lo(w_ref[...])
    o_ref[...] = _dot(a_hi, w_hi) + _dot(a_lo, w_hi) + _dot(a_hi, w_lo) + b_ref[...]


def _adaln(cond8, w_ada, b_ada):
    n = w_ada.shape[1]
    tn = 1536
    return pl.pallas_call(
        _adaln_kernel,
        out_shape=jax.ShapeDtypeStruct((8, n), F32),
        grid=(n // tn,),
        in_specs=[pl.BlockSpec((8, D_MODEL), lambda j: (0, 0)),
                  pl.BlockSpec((D_MODEL, tn), lambda j: (0, j)),
                  pl.BlockSpec((1, tn), lambda j: (0, j))],
        out_specs=pl.BlockSpec((8, tn), lambda j: (0, j)),
        compiler_params=pltpu.CompilerParams(dimension_semantics=("arbitrary",),
                                             vmem_limit_bytes=VMEM_LIMIT),
        name="adaln",
    )(cond8, w_ada, b_ada)


def _inproj_kernel(x_ref, g_ref, sh_ref, sc_ref, wg_ref, wl_ref, ws_ref,
                   gla_ref, lora_ref, q_ref, k_ref, v_ref):
    h = _rms_norm(x_ref[...], g_ref[...]) * (1.0 + sc_ref[...]) + sh_ref[...]
    hb = h.astype(BF16)
    gla_ref[...] = _dot(hb, wg_ref[...])
    lora_ref[...] = _dot(hb, wl_ref[...])
    s = _dot(hb, ws_ref[...])
    q_ref[...] = s[:, :SWA_Q]
    k_ref[...] = s[:, SWA_Q:SWA_Q + SWA_KV]
    v_ref[...] = s[:, SWA_Q + SWA_KV:]


def _inproj(x, g, sh, sc, w_gla, w_lora, w_swa, *, tm):
    b, t, d = x.shape
    nmod = sh.shape[0]
    mod_map = (lambda i, j: (i, 0, 0)) if nmod > 1 else (lambda i, j: (0, 0, 0))
    row = lambda i, j: (i, j, 0)
    full = lambda i, j: (0, 0)
    n_gla = w_gla.shape[1]
    n_lora = w_lora.shape[1]
    return pl.pallas_call(
        _inproj_kernel,
        out_shape=(jax.ShapeDtypeStruct((b, t, n_gla), F32),
                   jax.ShapeDtypeStruct((b, t, n_lora), F32),
                   jax.ShapeDtypeStruct((b, t, SWA_Q), F32),
                   jax.ShapeDtypeStruct((b, t, SWA_KV), F32),
                   jax.ShapeDtypeStruct((b, t, SWA_KV), F32)),
        grid=(b, t // tm),
        in_specs=[pl.BlockSpec((None, tm, d), row),
                  pl.BlockSpec((1, d), full),
                  pl.BlockSpec((None, 1, d), mod_map),
                  pl.BlockSpec((None, 1, d), mod_map),
                  pl.BlockSpec((d, n_gla), full),
                  pl.BlockSpec((d, n_lora), full),
                  pl.BlockSpec((d, w_swa.shape[1]), full)],
        out_specs=(pl.BlockSpec((None, tm, n_gla), row),
                   pl.BlockSpec((None, tm, n_lora), row),
                   pl.BlockSpec((None, tm, SWA_Q), row),
                   pl.BlockSpec((None, tm, SWA_KV), row),
                   pl.BlockSpec((None, tm, SWA_KV), row)),
        compiler_params=pltpu.CompilerParams(dimension_semantics=("arbitrary", "arbitrary"),
                                             vmem_limit_bytes=VMEM_LIMIT),
        name="inproj",
    )(x, g, sh, sc, w_gla, w_lora, w_swa)


SCAN_UNROLL = 2
OUT_UNROLL = 4


def _log_sigmoid(x):
    return jnp.minimum(x, 0.0) - jnp.log(1.0 + jnp.exp(-jnp.abs(x)))


def _heads_to_rows(x):
    return jnp.concatenate([x[:, h * LANES:(h + 1) * LANES] for h in range(GLA_HEADS)], axis=0)


def _rows_to_heads(x, c):
    return jnp.concatenate([x[h * c:(h + 1) * c, :] for h in range(GLA_HEADS)], axis=1)


def _gla_kernel(has_init, q_ref, k_ref, v_ref, g_ref, lora_ref, waf_ref, baf_ref, wab_ref, bab_ref,
                ng_ref, *rest):
    if has_init:
        s0f_ref, s0b_ref, *rest = rest
    (out_ref, sf_ref, sb_ref, laf_ref, lab_ref, oacc_ref, qtf_ref, qtb_ref, saf_ref, sab_ref,
     stf_ref, stb_ref) = rest
    t = q_ref.shape[0]
    c = GLA_CHUNK
    n = t // c
    hc = GLA_HEADS * c

    lora = lora_ref[...].astype(BF16)
    laf_ref[...] = _log_sigmoid(_dot(lora, waf_ref[...]) + baf_ref[...]) * (1.0 / GLA_GATE_NORM)
    lab_ref[...] = _log_sigmoid(_dot(lora, wab_ref[...]) + bab_ref[...]) * (1.0 / GLA_GATE_NORM)

    if has_init:
        stf_ref[...] = s0f_ref[...].T
        stb_ref[...] = s0b_ref[...].T
    else:
        stf_ref[...] = jnp.zeros_like(stf_ref)
        stb_ref[...] = jnp.zeros_like(stb_ref)
    oacc_ref[...] = jnp.zeros_like(oacc_ref)

    r64 = lax.broadcasted_iota(jnp.int32, (c, c), 0)
    c64 = lax.broadcasted_iota(jnp.int32, (c, c), 1)
    tri_f = jnp.where(c64 <= r64, 1.0, 0.0).astype(BF16)
    tri_b = jnp.where(c64 >= r64, 1.0, 0.0).astype(BF16)
    rr = lax.broadcasted_iota(jnp.int32, (hc, hc), 0)
    cc = lax.broadcasted_iota(jnp.int32, (hc, hc), 1)
    same_head = (rr >> 6) == (cc >> 6)
    keep_f = same_head & ((rr & (c - 1)) >= (cc & (c - 1)))
    keep_b = same_head & ((rr & (c - 1)) <= (cc & (c - 1)))
    head_mask = jnp.where(same_head, 1.0, 0.0).astype(BF16)
    norm_g = ng_ref[...]

    def chunk_rows(ci):
        return pl.ds(pl.multiple_of(ci * c, c), c)

    def tile_heads(x):
        x4 = jnp.concatenate([x] * GLA_HEADS, axis=0)
        return jnp.where(same_head, x4, 0.0).astype(BF16)

    def scan_step(i, carry):
        dirs = []
        for u in range(SCAN_UNROLL):
            dirs += [(SCAN_UNROLL * i + u, laf_ref, tri_f, keep_f, c - 1, stf_ref, saf_ref, qtf_ref),
                     (n - 1 - SCAN_UNROLL * i - u, lab_ref, tri_b, keep_b, 0, stb_ref, sab_ref, qtb_ref)]
        cums = []
        for ci, la_ref, tri, _, _, _, _, _ in dirs:
            la_hi, la_lo = _split_hi_lo(la_ref[chunk_rows(ci), :])
            cums.append(_dot(tri, la_hi) + _dot(tri, la_lo))
        ops = []
        for (ci, _, _, _, last_row, _, _, qt_ref), cum in zip(dirs, cums):
            sl = chunk_rows(ci)
            tot = cum[last_row:last_row + 1, :]
            kc = k_ref[sl, :]
            qt = q_ref[sl, :] * (GLA_DK ** -0.5) * jnp.exp(cum)
            qt_ref[sl, :] = qt.astype(BF16)
            v_rows = _heads_to_rows(v_ref[sl, :])
            ops.append((tot, tile_heads(qt), tile_heads(kc * jnp.exp(-cum)),
                        tile_heads(kc * jnp.exp(tot - cum)), v_rows))
        atts = [_dot_nt(q4, k4) for _, q4, k4, _, _ in ops]
        incs = []
        for (_, _, _, keep, _, _, _, _), (_, _, _, kd4, v_rows), att in zip(dirs, ops, atts):
            att = jnp.where(keep, att, 0.0).astype(BF16)
            incs.append((_dot(att, v_rows.astype(BF16)), _dot(v_rows.T.astype(BF16), kd4)))
        for (ci, _, _, _, _, st_ref, snap_ref, _), (tot, _, _, _, _), (o_intra, st_inc) in zip(dirs, ops, incs):
            oacc_ref[ci] += o_intra
            st = st_ref[...]
            snap_ref[ci] = st.astype(BF16)
            st_ref[...] = jnp.exp(tot) * st + st_inc
        return carry

    def tile_heads_bf16(x):
        return jnp.concatenate([x] * GLA_HEADS, axis=0) * head_mask

    def out_step(i, carry):
        chunks = [OUT_UNROLL * i + u for u in range(OUT_UNROLL)]
        inter = []
        for ci in chunks:
            sl = chunk_rows(ci)
            q4 = jnp.concatenate([tile_heads_bf16(qtf_ref[sl, :]), tile_heads_bf16(qtb_ref[sl, :])], axis=1)
            st = jnp.concatenate([saf_ref[ci], sab_ref[ci]], axis=1)
            inter.append(_dot_nt(q4, st))
        for ci, o_inter in zip(chunks, inter):
            sl = chunk_rows(ci)
            on = _rms_norm(oacc_ref[ci] + o_inter, norm_g)
            gate = _silu(_heads_to_rows(g_ref[sl, :]))
            out_ref[sl, :] = _rows_to_heads(on * gate, c)
        return carry

    lax.fori_loop(0, n // SCAN_UNROLL, scan_step, 0)
    lax.fori_loop(0, n // OUT_UNROLL, out_step, 0)
    sf_ref[...] = stf_ref[...].T
    sb_ref[...] = stb_ref[...].T


def _gla(gla_in, lora, waf, baf, wab, bab, norm_g, s0f=None, s0b=None):
    b, t, _ = gla_in.shape
    has_init = s0f is not None
    n = t // GLA_CHUNK
    bmap = lambda i: (i, 0, 0)
    full = lambda i: (0, 0)
    in_specs = [pl.BlockSpec((None, t, GLA_QK), lambda i: (i, 0, 0)),
                pl.BlockSpec((None, t, GLA_QK), lambda i: (i, 0, 1)),
                pl.BlockSpec((None, t, GLA_V), lambda i: (i, 0, 1)),
                pl.BlockSpec((None, t, GLA_V), lambda i: (i, 0, 2)),
                pl.BlockSpec((None, t, 2 * GLA_LORA), bmap),
                pl.BlockSpec((2 * GLA_LORA, GLA_QK), full),
                pl.BlockSpec((1, GLA_QK), full),
                pl.BlockSpec((2 * GLA_LORA, GLA_QK), full),
                pl.BlockSpec((1, GLA_QK), full),
                pl.BlockSpec((1, GLA_DV), full)]
    args = [gla_in, gla_in, gla_in, gla_in, lora, waf, baf, wab, bab, norm_g]
    if has_init:
        in_specs += [pl.BlockSpec((None, GLA_QK, GLA_DV), bmap)] * 2
        args += [s0f, s0b]
    return pl.pallas_call(
        functools.partial(_gla_kernel, has_init),
        out_shape=(jax.ShapeDtypeStruct((b, t, GLA_V), F32),
                   jax.ShapeDtypeStruct((b, GLA_QK, GLA_DV), F32),
                   jax.ShapeDtypeStruct((b, GLA_QK, GLA_DV), F32)),
        grid=(b,),
        in_specs=in_specs,
        out_specs=(pl.BlockSpec((None, t, GLA_V), bmap),
                   pl.BlockSpec((None, GLA_QK, GLA_DV), bmap),
                   pl.BlockSpec((None, GLA_QK, GLA_DV), bmap)),
        scratch_shapes=[pltpu.VMEM((t, GLA_QK), F32),
                        pltpu.VMEM((t, GLA_QK), F32),
                        pltpu.VMEM((n, GLA_HEADS * GLA_CHUNK, GLA_DV), F32),
                        pltpu.VMEM((t, GLA_QK), BF16),
                        pltpu.VMEM((t, GLA_QK), BF16),
                        pltpu.VMEM((n, GLA_DV, GLA_QK), BF16),
                        pltpu.VMEM((n, GLA_DV, GLA_QK), BF16),
                        pltpu.VMEM((GLA_DV, GLA_QK), F32),
                        pltpu.VMEM((GLA_DV, GLA_QK), F32)],
        compiler_params=pltpu.CompilerParams(dimension_semantics=("arbitrary",),
                                             vmem_limit_bytes=VMEM_LIMIT),
        name="gla",
    )(*args)


def _dup_groups(x):
    lo = lax.broadcasted_iota(jnp.int32, x.shape, 1) < SWA_HEAD_DIM
    xr = pltpu.roll(x, SWA_HEAD_DIM, axis=1)
    return jnp.where(lo, x, xr), jnp.where(lo, xr, x)


def _pairs_attention(qps, sinks, k_dups, vt_dups, mask):
    nq = qps[0].shape[0]
    lo = lax.broadcasted_iota(jnp.int32, (nq, LANES), 1) < SWA_HEAD_DIM
    even = lax.broadcasted_iota(jnp.int32, (1, 2 * nq), 1) < nq
    scores = []
    for qp, k_dup in zip(qps, k_dups):
        q2 = jnp.concatenate([jnp.where(lo, qp, 0.0), jnp.where(lo, 0.0, qp)], axis=0).astype(BF16)
        scores.append(_dot_nt(k_dup, q2))
    probs = []
    for s, (sink_even, sink_odd) in zip(scores, sinks):
        if mask is not None:
            s = jnp.where(mask, s, NEG_INF)
        sink = jnp.where(even, sink_even, sink_odd)
        m = jnp.maximum(jnp.max(s, axis=0, keepdims=True), sink)
        p = jnp.exp(s - m)
        denom = jnp.sum(p, axis=0, keepdims=True) + jnp.exp(sink - m)
        probs.append((p.astype(BF16), 1.0 / denom))
    outs = []
    for (p, rdenom), vt_dup in zip(probs, vt_dups):
        o = _dot(vt_dup, p) * rdenom
        outs.append(jnp.concatenate([o[:SWA_HEAD_DIM, :nq], o[SWA_HEAD_DIM:, nq:]], axis=0).T)
    return outs


def _attn_ctx_kernel(sink_ref, q_ref, k_ref, v_ref, o_ref):
    kd = [x.astype(BF16) for x in _dup_groups(k_ref[...])]
    vt = [x.T.astype(BF16) for x in _dup_groups(v_ref[...])]
    scale = SWA_HEAD_DIM ** -0.5
    pairs = range(SWA_HEADS // 2)
    outs = _pairs_attention([q_ref[:, pr * LANES:(pr + 1) * LANES] * scale for pr in pairs],
                            [(sink_ref[2 * pr], sink_ref[2 * pr + 1]) for pr in pairs],
                            [kd[pr // 2] for pr in pairs], [vt[pr // 2] for pr in pairs], None)
    for pr in pairs:
        o_ref[:, pr * LANES:(pr + 1) * LANES] = outs[pr]


def _attn_ctx(sink, q, k, v):
    b, t, _ = q.shape
    bmap = lambda i: (i, 0, 0)
    return pl.pallas_call(
        _attn_ctx_kernel,
        out_shape=jax.ShapeDtypeStruct((b, t, SWA_Q), F32),
        grid=(b,),
        in_specs=[pl.BlockSpec(memory_space=pltpu.SMEM),
                  pl.BlockSpec((None, t, SWA_Q), bmap),
                  pl.BlockSpec((None, t, SWA_KV), bmap),
                  pl.BlockSpec((None, t, SWA_KV), bmap)],
        out_specs=pl.BlockSpec((None, t, SWA_Q), bmap),
        compiler_params=pltpu.CompilerParams(dimension_semantics=("arbitrary",),
                                             vmem_limit_bytes=VMEM_LIMIT),
        name="attn_ctx",
    )(sink, q, k, v)


def _rope(x, cos, sin_lo, sin_hi):
    return x * cos + pltpu.roll(x, LANES - 16, axis=1) * sin_lo + pltpu.roll(x, 16, axis=1) * sin_hi


def _attn_lat_kernel(sink_ref, q_ref, k_ref, v_ref, kc_ref, vc_ref, cos_ref, sl_ref, sh_ref,
                     o_ref, kw_ref, vw_ref):
    t = q_ref.shape[0]
    ab = ATTN_BLOCK
    nb = t // ab
    scale = SWA_HEAD_DIM ** -0.5

    k_rot = _dup_groups(_rope(k_ref[...], cos_ref[...], sl_ref[...], sh_ref[...]))
    v_dup = _dup_groups(v_ref[...])
    zeros = jnp.zeros((ab, LANES), BF16)
    for grp in range(SWA_KV_HEADS):
        kw_ref[grp, 0:ab, :] = zeros
        kw_ref[grp, ab:ab + t, :] = k_rot[grp].astype(BF16)
        kw_ref[grp, ab + t:, :] = zeros
        vw_ref[grp, 0] = zeros
        for blk in range(nb):
            vw_ref[grp, blk + 1] = v_dup[grp][blk * ab:(blk + 1) * ab, :].T.astype(BF16)
        vw_ref[grp, nb + 1] = zeros
    kc = [x.astype(BF16) for x in _dup_groups(kc_ref[...])]
    vct = [x.T.astype(BF16) for x in _dup_groups(vc_ref[...])]
    lc = kc_ref.shape[0]

    key = lax.broadcasted_iota(jnp.int32, (lc + 3 * ab, 2 * ab), 0) - lc
    tq = lax.broadcasted_iota(jnp.int32, (lc + 3 * ab, 2 * ab), 1) & (ab - 1)
    band = (key < 0) | (jnp.abs(tq + ab - key) <= ab)

    def block(nq, carry):
        row0 = pl.multiple_of(nq * ab, ab)
        s_abs = key + (nq - 1) * ab
        mask = band & ((key < 0) | ((s_abs >= 0) & (s_abs < t)))
        cos = cos_ref[pl.ds(row0, ab), :]
        s_lo = sl_ref[pl.ds(row0, ab), :]
        s_hi = sh_ref[pl.ds(row0, ab), :]
        k_all = [jnp.concatenate([kc[grp], kw_ref[grp, pl.ds(row0, 3 * ab), :]], axis=0)
                 for grp in range(SWA_KV_HEADS)]
        vt_all = [jnp.concatenate([vct[grp], vw_ref[grp, nq], vw_ref[grp, nq + 1], vw_ref[grp, nq + 2]],
                                  axis=1) for grp in range(SWA_KV_HEADS)]
        pairs = range(SWA_HEADS // 2)
        qps = [_rope(q_ref[pl.ds(row0, ab), pr * LANES:(pr + 1) * LANES], cos, s_lo, s_hi) * scale
               for pr in pairs]
        outs = _pairs_attention(qps, [(sink_ref[2 * pr], sink_ref[2 * pr + 1]) for pr in pairs],
                                [k_all[pr // 2] for pr in pairs], [vt_all[pr // 2] for pr in pairs], mask)
        for pr in pairs:
            o_ref[pl.ds(row0, ab), pr * LANES:(pr + 1) * LANES] = outs[pr]
        return carry

    lax.fori_loop(0, nb, block, 0)


def _attn_lat(sink, q, k, v, kc, vc, cos, sin_lo, sin_hi):
    b, t, _ = q.shape
    lc = kc.shape[1]
    bmap = lambda i: (i, 0, 0)
    full = lambda i: (0, 0)
    return pl.pallas_call(
        _attn_lat_kernel,
        out_shape=jax.ShapeDtypeStruct((b, t, SWA_Q), F32),
        grid=(b,),
        in_specs=[pl.BlockSpec(memory_space=pltpu.SMEM),
                  pl.BlockSpec((None, t, SWA_Q), bmap),
                  pl.BlockSpec((None, t, SWA_KV), bmap),
                  pl.BlockSpec((None, t, SWA_KV), bmap),
                  pl.BlockSpec((None, lc, SWA_KV), bmap),
                  pl.BlockSpec((None, lc, SWA_KV), bmap),
                  pl.BlockSpec((t, LANES), full),
                  pl.BlockSpec((t, LANES), full),
                  pl.BlockSpec((t, LANES), full)],
        out_specs=pl.BlockSpec((None, t, SWA_Q), bmap),
        scratch_shapes=[pltpu.VMEM((SWA_KV_HEADS, t + 2 * ATTN_BLOCK, LANES), BF16),
                        pltpu.VMEM((SWA_KV_HEADS, t // ATTN_BLOCK + 2, LANES, ATTN_BLOCK), BF16)],
        compiler_params=pltpu.CompilerParams(dimension_semantics=("arbitrary",),
                                             vmem_limit_bytes=VMEM_LIMIT),
        name="attn_lat",
    )(sink, q, k, v, kc, vc, cos, sin_lo, sin_hi)


def _rope_tables(t):
    half = SWA_HEAD_DIM // 2
    quarter = half // 2
    pos = jnp.arange(t)
    row = (pos // GRID_W).astype(F32)
    col = (pos % GRID_W).astype(F32)
    inv_freq = ROPE_BASE ** (-jnp.arange(quarter, dtype=F32) / quarter)
    lane = jnp.arange(LANES)
    d = lane % SWA_HEAD_DIM
    freq = inv_freq[d % quarter]
    use_row = (d < half)
    ang = jnp.where(use_row[None, :], row[:, None], col[:, None]) * freq[None, :]
    cos = jnp.cos(ang)
    sin = jnp.sin(ang)
    lower = (d % half) < quarter
    return cos, jnp.where(lower[None, :], -sin, 0.0), jnp.where(lower[None, :], 0.0, sin)


def _route(sel, scores):
    n = sel.shape[1]
    gsz = N_EXPERTS // N_EXPERT_GROUPS

    def first_max(x, idx, size):
        m = jnp.max(x, axis=0, keepdims=True)
        first = jnp.min(jnp.where(x == m, idx, float(size)), axis=0, keepdims=True)
        return m, idx == first

    i8 = lax.broadcasted_iota(jnp.int32, (gsz, n), 0).astype(F32)
    rows = []
    for g in range(N_EXPERT_GROUPS):
        slab = sel[g * gsz:(g + 1) * gsz, :]
        m1, hit = first_max(slab, i8, gsz)
        m2 = jnp.max(jnp.where(hit, NEG_INF, slab), axis=0, keepdims=True)
        rows.append(m1 + m2)
    gscore = jnp.concatenate(rows, axis=0)
    gsel = jnp.zeros((N_EXPERT_GROUPS, n), F32)
    for _ in range(TOPK_GROUPS):
        _, hit = first_max(gscore, i8, N_EXPERT_GROUPS)
        gsel = jnp.where(hit, 1.0, gsel)
        gscore = jnp.where(hit, NEG_INF, gscore)
    emask = jnp.concatenate(
        [jnp.broadcast_to(gsel[g:g + 1, :], (gsz, n)) for g in range(N_EXPERT_GROUPS)], axis=0)
    cand = jnp.where(emask > 0.5, sel, NEG_INF)
    ie = lax.broadcasted_iota(jnp.int32, (N_EXPERTS, n), 0).astype(F32)
    w = jnp.zeros((N_EXPERTS, n), F32)
    chosen = jnp.zeros((N_EXPERTS, n), F32)
    hits = []
    for _ in range(TOP_K):
        _, hit = first_max(cand, ie, N_EXPERTS)
        hits.append(hit)
        w = jnp.where(hit, scores, w)
        chosen = jnp.where(hit, 1.0, chosen)
        cand = jnp.where(hit, NEG_INF, cand)
    gates = w / jnp.sum(w, axis=0, keepdims=True) * ROUTED_SCALE

    s_idx = lax.broadcasted_iota(jnp.int32, (n, n), 0)
    t_idx = lax.broadcasted_iota(jnp.int32, (n, n), 1)
    before = jnp.where(s_idx < t_idx, 1.0, 0.0).astype(BF16)
    rank = _dot(chosen.astype(BF16), before)
    count = jnp.sum(chosen, axis=1, keepdims=True)
    padded = jnp.floor((count + (SORT_ALIGN - 1)) * (1.0 / SORT_ALIGN)) * SORT_ALIGN
    padded = jnp.broadcast_to(padded, (N_EXPERTS, LANES))
    e_row = lax.broadcasted_iota(jnp.int32, (N_EXPERTS, N_EXPERTS), 0)
    e_col = lax.broadcasted_iota(jnp.int32, (N_EXPERTS, N_EXPERTS), 1)
    below = jnp.where(e_col < e_row, 1.0, 0.0).astype(BF16)
    start = _dot(below, padded.astype(BF16))
    row = start[:, 0:1] + rank
    pos = jnp.concatenate([jnp.sum(jnp.where(h, row, 0.0), axis=0, keepdims=True) for h in hits], axis=0)
    wts = jnp.concatenate([jnp.sum(jnp.where(h, gates, 0.0), axis=0, keepdims=True) for h in hits], axis=0)
    return pos, wts, padded, start


def _outproj_kernel(gla_ref, att_ref, x_ref, wo_ref, g1_ref, sh_ref, sc_ref, ng_ref, rw_ref, rwh_ref,
                    rb_ref, x1_ref, xm_ref, pos_ref, wts_ref, cnt_ref, start_ref):
    y = (_dot(gla_ref[...].astype(BF16), wo_ref[0:GLA_V, :])
         + _dot(att_ref[...].astype(BF16), wo_ref[GLA_V:, :]))
    x1 = x_ref[...] + g1_ref[...] * y
    x1_ref[...] = x1
    xm = _rms_norm(x1, ng_ref[...]) * (1.0 + sc_ref[...]) + sh_ref[...]
    xm_hi, xm_lo = _split_hi_lo(xm)
    xm_ref[...] = xm_hi
    lg = _dot(xm_hi, rw_ref[...])
    logits = lg[:, :N_EXPERTS] + lg[:, N_EXPERTS:] + _dot(xm_lo, rwh_ref[...])
    tm = logits.shape[0]
    lt = jnp.concatenate([logits, jnp.zeros((tm, LANES - N_EXPERTS), F32)], axis=1).T[:N_EXPERTS, :]
    scores = _sigmoid(lt)
    pos_ref[...], wts_ref[...], cnt_ref[...], start_ref[...] = _route(scores + rb_ref[...], scores)


def _outproj(gla_out, att_out, x, w_out, g1, sh2, sc2, norm_g, rw_cat, rw_hi, rbias, *, tm):
    b, t, d = x.shape
    nmod = g1.shape[0]
    mod_map = (lambda i, j: (i, 0, 0)) if nmod > 1 else (lambda i, j: (0, 0, 0))
    row = lambda i, j: (i, j, 0)
    full = lambda i, j: (0, 0)
    tile = lambda i, j: (i, j, 0, 0)
    nt = t // tm
    return pl.pallas_call(
        _outproj_kernel,
        out_shape=(jax.ShapeDtypeStruct((b, t, d), F32),
                   jax.ShapeDtypeStruct((b, t, d), BF16),
                   jax.ShapeDtypeStruct((b, nt, TOP_K, tm), F32),
                   jax.ShapeDtypeStruct((b, nt, TOP_K, tm), F32),
                   jax.ShapeDtypeStruct((b, nt, N_EXPERTS, LANES), F32),
                   jax.ShapeDtypeStruct((b, nt, N_EXPERTS, LANES), F32)),
        grid=(b, t // tm),
        in_specs=[pl.BlockSpec((None, tm, GLA_V), row),
                  pl.BlockSpec((None, tm, SWA_Q), row),
                  pl.BlockSpec((None, tm, d), row),
                  pl.BlockSpec((d, d), full),
                  pl.BlockSpec((None, 1, d), mod_map),
                  pl.BlockSpec((None, 1, d), mod_map),
                  pl.BlockSpec((None, 1, d), mod_map),
                  pl.BlockSpec((1, d), full),
                  pl.BlockSpec((d, 2 * N_EXPERTS), full),
                  pl.BlockSpec((d, N_EXPERTS), full),
                  pl.BlockSpec((N_EXPERTS, 1), full)],
        out_specs=(pl.BlockSpec((None, tm, d), row),
                   pl.BlockSpec((None, tm, d), row),
                   pl.BlockSpec((None, None, TOP_K, tm), tile),
                   pl.BlockSpec((None, None, TOP_K, tm), tile),
                   pl.BlockSpec((None, None, N_EXPERTS, LANES), tile),
                   pl.BlockSpec((None, None, N_EXPERTS, LANES), tile)),
        compiler_params=pltpu.CompilerParams(dimension_semantics=("arbitrary", "arbitrary"),
                                             vmem_limit_bytes=VMEM_LIMIT),
        name="outproj",
    )(gla_out, att_out, x, w_out, g1, sh2, sc2, norm_g, rw_cat, rw_hi, rbias)


MOE_TILE = 256
SORT_ALIGN = 16
SORT_ROWS = 3072
ROW_TILE = 512
GATHER_SLOTS = 3
FFN_CHAINS = 2
COMBINE_CHUNK = 1024
ALWAYS_ROWS = 2560
COMBINE_TAIL = 512


def _moe_sort_kernel(tiles_a, used_ref, xa_ref, xb_ref, pos_ref, xs_ref):
    i = pl.program_id(0)
    x = jnp.where(i < tiles_a, xa_ref[...], xb_ref[...])
    pos = pos_ref[...]
    tm = x.shape[0]
    used = used_ref[i]

    def fill(blk):
        rows = (lax.broadcasted_iota(jnp.int32, (tm, tm), 0) + blk * tm).astype(F32)
        onehot = jnp.zeros((tm, tm), F32)
        for k in range(TOP_K):
            onehot = jnp.where(rows == pos[k:k + 1, :], 1.0, onehot)
        xs_ref[blk * tm:(blk + 1) * tm, :] = _dot(onehot.astype(BF16), x).astype(BF16)

    for blk in range(SORT_ROWS // tm):
        if (blk + 1) * tm <= ALWAYS_ROWS:
            fill(blk)
        else:
            pl.when(blk * tm < used)(functools.partial(fill, blk))

            @pl.when(blk * tm >= used)
            def _():
                xs_ref[blk * tm:(blk + 1) * tm, :] = jnp.zeros((tm, D_MODEL), BF16)


def _moe_sort(xm_a, xm_b, pos, used):
    d = xm_a.shape[1]
    nt, _, tm = pos.shape
    tiles_a = xm_a.shape[0] // tm
    grid_spec = pltpu.PrefetchScalarGridSpec(
        num_scalar_prefetch=1,
        grid=(nt,),
        in_specs=[pl.BlockSpec((tm, d), lambda i, u: (jnp.minimum(i, tiles_a - 1), 0)),
                  pl.BlockSpec((tm, d), lambda i, u: (jnp.maximum(i - tiles_a, 0), 0)),
                  pl.BlockSpec((None, TOP_K, tm), lambda i, u: (i, 0, 0))],
        out_specs=pl.BlockSpec((SORT_ROWS, d), lambda i, u: (i, 0)))
    return pl.pallas_call(
        functools.partial(_moe_sort_kernel, tiles_a),
        out_shape=jax.ShapeDtypeStruct((nt * SORT_ROWS, d), BF16),
        grid_spec=grid_spec,
        compiler_params=pltpu.CompilerParams(dimension_semantics=("arbitrary",),
                                             vmem_limit_bytes=VMEM_LIMIT),
        name="moe_sort",
    )(used, xm_a, xm_b, pos)


def _moe_row_tiles(n_tokens):
    rows = n_tokens * TOP_K + (n_tokens // MOE_TILE) * N_EXPERTS * (SORT_ALIGN - 1) + N_EXPERTS * (ROW_TILE - 1)
    return -(-rows // ROW_TILE) + GATHER_SLOTS - 1


PLAN_CHUNK = 1280


def _int_dot_r(a, onehot):
    hi = jnp.floor(a * (1.0 / 256.0))
    return _dot(hi.astype(BF16), onehot) * 256.0 + _dot((a - hi * 256.0).astype(BF16), onehot)


def _int_dot_l(onehot, b):
    hi = jnp.floor(b * (1.0 / 256.0))
    return _dot(onehot, hi.astype(BF16)) * 256.0 + _dot(onehot, (b - hi * 256.0).astype(BF16))


def _moe_plan_kernel(cnt_ref, start_ref, src_ref, first_ref, tiles_ref, nu_ref, back_ref):
    nt, ne = cnt_ref.shape
    gpt = SORT_ROWS // SORT_ALIGN
    gpr = ROW_TILE // SORT_ALIGN
    gc = cnt_ref[...] * (1.0 / SORT_ALIGN)
    ls = start_ref[...] * (1.0 / SORT_ALIGN)

    def transpose(x):
        x = jnp.concatenate([x, jnp.zeros((nt, LANES - ne), F32)], axis=1)
        x = jnp.concatenate([x, jnp.zeros((LANES - nt, LANES), F32)], axis=0)
        return x.T[:ne, :nt]

    def tri(n, keep):
        return jnp.where(keep(lax.broadcasted_iota(jnp.int32, (n, n), 0),
                              lax.broadcasted_iota(jnp.int32, (n, n), 1)), 1.0, 0.0).astype(BF16)

    gc_t = transpose(gc)
    ls_t = transpose(ls)
    tot_c = jnp.broadcast_to(jnp.sum(gc_t, axis=1, keepdims=True), (ne, LANES))
    ptot_c = jnp.floor((tot_c + (gpr - 1)) * (1.0 / gpr)) * gpr
    gend_c = _int_dot_l(tri(ne, lambda r, c: c <= r), ptot_c)
    gstart_c = gend_c - ptot_c
    n_used = gend_c[ne - 1:ne, :] * (1.0 / gpr)
    nu_ref[...] = n_used.astype(jnp.int32)
    tot_r = jnp.sum(gc, axis=0, keepdims=True)
    ptot_r = jnp.floor((tot_r + (gpr - 1)) * (1.0 / gpr)) * gpr
    gstart_r = _int_dot_r(jnp.broadcast_to(ptot_r, (8, ne)), tri(ne, lambda r, c: r < c))
    cumex = _dot(tri(nt, lambda r, c: c < r), gc.astype(BF16))
    cumex_t = _dot(gc_t.astype(BF16), tri(nt, lambda r, c: r < c))
    tile_base = lax.broadcasted_iota(jnp.int32, (nt, ne), 0).astype(F32) * gpt + ls
    table = jnp.concatenate([cumex + gc, cumex, tile_base, gstart_r, jnp.broadcast_to(tot_r, (8, ne))], axis=0)

    e_iota = lax.broadcasted_iota(jnp.int32, (ne, PLAN_CHUNK), 0).astype(F32)
    for ch in range(src_ref.shape[1] // PLAN_CHUNK):
        g = (lax.broadcasted_iota(jnp.int32, (1, PLAN_CHUNK), 1) + ch * PLAN_CHUNK).astype(F32)
        eg = jnp.sum(jnp.where(gend_c[:, 0:1] <= g, 1.0, 0.0), axis=0, keepdims=True)
        picked = _int_dot_r(table, jnp.where(e_iota == eg, 1.0, 0.0).astype(BF16))
        cum_g, cumex_g, base_g = picked[0:nt], picked[nt:2 * nt], picked[2 * nt:3 * nt]
        u = g - picked[3 * nt:3 * nt + 1]
        in_tile = (cumex_g <= u) & (u < cum_g)
        src = jnp.sum(jnp.where(in_tile, base_g - cumex_g, 0.0), axis=0, keepdims=True) + u
        src = jnp.where(u < picked[3 * nt + 8:3 * nt + 9], src, gpt - 1.0)
        src_ref[:, ch * PLAN_CHUNK:(ch + 1) * PLAN_CHUNK] = src.astype(jnp.int32)

    first_ref[...] = (gstart_c * (1.0 / gpr)).astype(jnp.int32)
    tiles_ref[...] = (ptot_c * (1.0 / gpr)).astype(jnp.int32)

    lg = lax.broadcasted_iota(jnp.int32, (ne, back_ref.shape[1]), 1).astype(F32)
    for t in range(nt):
        first = ls_t[:, t:t + 1]
        inside = (first <= lg) & (lg < first + gc_t[:, t:t + 1])
        shift = gstart_c[:, 0:1] + cumex_t[:, t:t + 1] - first
        val = jnp.sum(jnp.where(inside, shift + lg, 0.0), axis=0, keepdims=True)
        back_ref[t:t + 1, :] = val.astype(jnp.int32)


def _moe_plan(cnt, start):
    nt, ne = cnt.shape
    row_tiles = _moe_row_tiles(nt * MOE_TILE)
    gpt = SORT_ROWS // SORT_ALIGN
    gpr = ROW_TILE // SORT_ALIGN
    n_src = -(-(row_tiles * gpr) // PLAN_CHUNK) * PLAN_CHUNK
    n_back = -(-gpt // LANES) * LANES
    src, first, tiles, nu, back = pl.pallas_call(
        _moe_plan_kernel,
        out_shape=(jax.ShapeDtypeStruct((1, n_src), jnp.int32),
                   jax.ShapeDtypeStruct((ne, LANES), jnp.int32),
                   jax.ShapeDtypeStruct((ne, LANES), jnp.int32),
                   jax.ShapeDtypeStruct((1, LANES), jnp.int32),
                   jax.ShapeDtypeStruct((nt, n_back), jnp.int32)),
        compiler_params=pltpu.CompilerParams(vmem_limit_bytes=VMEM_LIMIT),
        name="moe_plan",
    )(cnt, start)
    return nu[0, :1], first[:, 0], tiles[:, 0], src[0, :row_tiles * gpr], back[:, :gpt]


def _moe_experts_kernel(nu_ref, first_ref, tiles_ref, src_ref, xs_hbm, wg_ref, wu_ref, wd_ref, ys_hbm,
                        xbuf, ybuf, gsem, osem, wgu_s, wd_s):
    e = pl.program_id(0)
    n_used = nu_ref[0]
    gpr = ROW_TILE // SORT_ALIGN
    part = ROW_TILE // FFN_CHAINS

    def gather(tile, to_slot, j0=0, j1=gpr):
        for j in range(j0, j1):
            row = pl.multiple_of(src_ref[tile * gpr + j] * SORT_ALIGN, SORT_ALIGN)
            pltpu.make_async_copy(xs_hbm.at[pl.ds(row, SORT_ALIGN), :],
                                  xbuf.at[to_slot, j * SORT_ALIGN:(j + 1) * SORT_ALIGN, :],
                                  gsem.at[to_slot]).start(priority=j % 2)

    def drain(of_slot):
        for j in range(gpr):
            pltpu.make_async_copy(xs_hbm.at[0:SORT_ALIGN, :],
                                  xbuf.at[of_slot, j * SORT_ALIGN:(j + 1) * SORT_ALIGN, :], gsem.at[of_slot]).wait()

    def out_copy(tile, of_slot):
        row = pl.multiple_of(tile * ROW_TILE, ROW_TILE)
        return pltpu.make_async_copy(ybuf.at[of_slot], ys_hbm.at[pl.ds(row, ROW_TILE), :], osem.at[of_slot])

    @pl.when(e == 0)
    def _():
        gather(0, 0)
        gather(1, 1)

    wgu_s[:, :EXPERT_FF] = wg_ref[...].astype(BF16)
    wgu_s[:, EXPERT_FF:] = wu_ref[...].astype(BF16)
    wd_s[...] = wd_ref[...].astype(BF16)

    def row_tile(i, carry):
        r = first_ref[e] + i
        slot = lax.rem(r, GATHER_SLOTS)
        oslot = lax.rem(r, 2)
        next_slot = lax.rem(r + 2, GATHER_SLOTS)
        drain(slot)

        @pl.when(r >= 2)
        def _():
            out_copy(r - 2, oslot).wait()

        abs_ = []
        for c in range(FFN_CHAINS):
            abs_.append(_dot(xbuf[slot, c * part:(c + 1) * part, :], wgu_s[...]))
            gather(r + 2, next_slot, c * gpr // FFN_CHAINS, (c + 1) * gpr // FFN_CHAINS)
        hs = [(_silu(ab[:, :EXPERT_FF]) * ab[:, EXPERT_FF:]).astype(BF16) for ab in abs_]
        ys = [_dot(h, wd_s[...]).astype(BF16) for h in hs]
        for c in range(FFN_CHAINS):
            ybuf[oslot, c * part:(c + 1) * part, :] = ys[c]
        out_copy(r, oslot).start()
        return carry

    lax.fori_loop(0, tiles_ref[e], row_tile, 0)

    @pl.when(e == pl.num_programs(0) - 1)
    def _():
        drain(lax.rem(n_used, GATHER_SLOTS))
        drain(lax.rem(n_used + 1, GATHER_SLOTS))
        out_copy(n_used - 1, lax.rem(n_used - 1, 2)).wait()

        @pl.when(n_used >= 2)
        def _():
            out_copy(n_used - 2, lax.rem(n_used, 2)).wait()


def _moe_experts(n_used, first, tiles, src, xs, wg, wu, wd, row_tiles):
    d = xs.shape[-1]
    ne = wg.shape[0]
    w_map = lambda e, nu, fi, ti, sr: (e, 0, 0)
    grid_spec = pltpu.PrefetchScalarGridSpec(
        num_scalar_prefetch=4,
        grid=(ne,),
        in_specs=[pl.BlockSpec(memory_space=pl.ANY),
                  pl.BlockSpec((None, d, EXPERT_FF), w_map),
                  pl.BlockSpec((None, d, EXPERT_FF), w_map),
                  pl.BlockSpec((None, EXPERT_FF, d), w_map)],
        out_specs=pl.BlockSpec(memory_space=pl.ANY),
        scratch_shapes=[pltpu.VMEM((GATHER_SLOTS, ROW_TILE, d), BF16),
                        pltpu.VMEM((2, ROW_TILE, d), BF16),
                        pltpu.SemaphoreType.DMA((GATHER_SLOTS,)),
                        pltpu.SemaphoreType.DMA((2,)),
                        pltpu.VMEM((d, 2 * EXPERT_FF), BF16),
                        pltpu.VMEM((EXPERT_FF, d), BF16)])
    return pl.pallas_call(
        _moe_experts_kernel,
        out_shape=jax.ShapeDtypeStruct((row_tiles * ROW_TILE, d), BF16),
        grid_spec=grid_spec,
        compiler_params=pltpu.CompilerParams(dimension_semantics=("arbitrary",),
                                             vmem_limit_bytes=VMEM_LIMIT),
        name="moe_experts",
    )(n_used, first, tiles, src, xs, wg, wu, wd)


def _moe_combine_kernel(back_ref, used_ref, ys_hbm, pos_ref, wts_ref, xm_ref, x1_ref, g2_ref, fg_ref,
                        swg_ref, swu_ref, swd_ref, o_ref, buf, sem, acc_ref):
    i = pl.program_id(0)
    gpt = SORT_ROWS // SORT_ALIGN
    slot = lax.rem(i, 2)
    always = ALWAYS_ROWS
    tail = range(always, SORT_ROWS, COMBINE_TAIL)

    def copies(tile, of_slot, g0, g1, start):
        for g in range(g0, g1):
            row = pl.multiple_of(back_ref[tile * gpt + g] * SORT_ALIGN, SORT_ALIGN) if start else 0
            cp = pltpu.make_async_copy(ys_hbm.at[pl.ds(row, SORT_ALIGN), :],
                                       buf.at[of_slot, g * SORT_ALIGN:(g + 1) * SORT_ALIGN, :], sem.at[of_slot])
            if start:
                cp.start(priority=g % 2)
            else:
                cp.wait()

    def transfer(tile, of_slot, start):
        copies(tile, of_slot, 0, always // SORT_ALIGN, start)
        for c0 in tail:
            pl.when(c0 < used_ref[tile])(functools.partial(
                copies, tile, of_slot, c0 // SORT_ALIGN, (c0 + COMBINE_TAIL) // SORT_ALIGN, start))

    @pl.when(i == 0)
    def _():
        transfer(0, 0, True)

    @pl.when(i + 1 < pl.num_programs(0))
    def _():
        transfer(i + 1, 1 - slot, True)

    x = xm_ref[...]
    tm = x.shape[0]
    shared = _dot((_silu(_dot(x, swg_ref[...])) * _dot(x, swu_ref[...])).astype(BF16), swd_ref[...])
    pad = jnp.zeros((LANES - TOP_K, tm), F32)
    pos_t = jnp.concatenate([pos_ref[...], pad], axis=0).T
    wts_t = jnp.concatenate([wts_ref[...], pad], axis=0).T
    transfer(i, slot, False)

    pos_b = [jnp.broadcast_to(pos_t[:, k:k + 1], (tm, LANES)) for k in range(TOP_K)]
    wts_b = [jnp.broadcast_to(wts_t[:, k:k + 1], (tm, LANES)) for k in range(TOP_K)]

    def apply(c0, width):
        reps = width // LANES
        rows = (lax.broadcasted_iota(jnp.int32, (tm, width), 1) + c0).astype(F32)
        comb = jnp.zeros((tm, width), F32)
        for k in range(TOP_K):
            comb = jnp.where(rows == jnp.concatenate([pos_b[k]] * reps, axis=1),
                             jnp.concatenate([wts_b[k]] * reps, axis=1), comb)
        return _dot(comb.astype(BF16), buf[slot, c0:c0 + width, :])

    routed = shared
    for c0 in range(0, always, COMBINE_CHUNK):
        routed = routed + apply(c0, min(COMBINE_CHUNK, always - c0))
    acc_ref[...] = routed
    for c0 in tail:
        @pl.when(c0 < used_ref[i])
        def _(c0=c0):
            acc_ref[...] += apply(c0, COMBINE_TAIL)
    y = x1_ref[...] + g2_ref[...] * acc_ref[...]
    o_ref[...] = _rms_norm(y, fg_ref[...])


def _moe_combine(back, used, ys, pos, wts, xm, x1, g2, final_g, swg, swu, swd, *, tiles_per_mod):
    n, d = xm.shape
    tm = pos.shape[-1]
    nt = n // tm
    gpt = SORT_ROWS // SORT_ALIGN
    row = lambda i, bk, us: (i, 0)
    full = lambda i, bk, us: (0, 0)
    tile = lambda i, bk, us: (i, 0, 0)
    mod_map = lambda i, bk, us: (i // tiles_per_mod, 0, 0)
    grid_spec = pltpu.PrefetchScalarGridSpec(
        num_scalar_prefetch=2,
        grid=(nt,),
        in_specs=[pl.BlockSpec(memory_space=pl.ANY),
                  pl.BlockSpec((None, TOP_K, tm), tile),
                  pl.BlockSpec((None, TOP_K, tm), tile),
                  pl.BlockSpec((tm, d), row),
                  pl.BlockSpec((tm, d), row),
                  pl.BlockSpec((None, 1, d), mod_map),
                  pl.BlockSpec((1, d), full),
                  pl.BlockSpec((d, SHARED_FF), full),
                  pl.BlockSpec((d, SHARED_FF), full),
                  pl.BlockSpec((SHARED_FF, d), full)],
        out_specs=pl.BlockSpec((tm, d), row),
        scratch_shapes=[pltpu.VMEM((2, SORT_ROWS, d), BF16),
                        pltpu.SemaphoreType.DMA((2,)),
                        pltpu.VMEM((tm, d), F32)])
    return pl.pallas_call(
        _moe_combine_kernel,
        out_shape=jax.ShapeDtypeStruct((n, d), F32),
        grid_spec=grid_spec,
        compiler_params=pltpu.CompilerParams(dimension_semantics=("arbitrary",),
                                             vmem_limit_bytes=VMEM_LIMIT),
        name="moe_combine",
    )(back, used, ys, pos.reshape(nt, TOP_K, tm), wts.reshape(nt, TOP_K, tm), xm, x1, g2, final_g, swg, swu, swd)


def _mix(x, mods, p, attn_fn, s0=None):
    sh1, sc1, g1, sh2, sc2, _ = mods
    gla_in, lora, q_s, k_s, v_s = _inproj(x, p["norm_attn_g"], sh1, sc1, p["w_gla"], p["w_lora"],
                                          p["w_swa"], tm=256)
    if s0 is None:
        gla_out, s_f, s_b = _gla(gla_in, lora, p["waf"], p["baf"], p["wab"], p["bab"], p["gla_norm_g"])
    else:
        gla_out, s_f, s_b = _gla(gla_in, lora, p["waf"], p["baf"], p["wab"], p["bab"], p["gla_norm_g"],
                                 s0[0], s0[1])
    att_out = attn_fn(q_s, k_s, v_s)
    routed = _outproj(gla_out, att_out, x, p["w_out"], g1, sh2, sc2, p["norm_ffn_g"],
                      p["rw_cat"], p["rw_hi"], p["rbias"], tm=MOE_TILE)
    return routed, k_s, v_s, s_f, s_b


def _moe(streams, p):
    d = D_MODEL
    (ra, _), (rb, _) = streams
    n_tiles = [r[1].shape[0] * r[1].shape[1] // MOE_TILE for r, _ in streams]
    pos_all = jnp.concatenate([r[2].reshape(-1, TOP_K, MOE_TILE) for r, _ in streams], axis=0)
    cnt_all = jnp.concatenate([r[4][..., 0].reshape(-1, N_EXPERTS) for r, _ in streams], axis=0)
    start_all = jnp.concatenate([r[5][..., 0].reshape(-1, N_EXPERTS) for r, _ in streams], axis=0)
    used = (start_all[:, -1] + cnt_all[:, -1]).astype(jnp.int32)
    xs = _moe_sort(ra[1].reshape(-1, d), rb[1].reshape(-1, d), pos_all, used)
    n_used, first, tiles, src, back = _moe_plan(cnt_all, start_all)
    ys = _moe_experts(n_used, first, tiles, src, xs, p["wg"], p["wu"], p["wd"],
                      _moe_row_tiles(cnt_all.shape[0] * MOE_TILE))
    outs = []
    tile0 = 0
    for ((x1, xm, pos, wts, cnt, start), g2), nt in zip(streams, n_tiles):
        b, t, _ = x1.shape
        tiles_per_mod = (t // MOE_TILE) if g2.shape[0] > 1 else nt
        y = _moe_combine(back[tile0:tile0 + nt].reshape(-1), used[tile0:tile0 + nt], ys, pos, wts,
                         xm.reshape(-1, d), x1.reshape(-1, d), g2, p["final_norm_g"],
                         p["swg"], p["swu"], p["swd"], tiles_per_mod=tiles_per_mod)
        outs.append(y.reshape(b, t, d))
        tile0 += nt
    return outs


def kernel(x_prompt, x_sample, c, cache_swa_k, cache_swa_v, state_gla_fwd, state_gla_bwd, c_ctx, w_ada, b_ada, norm_attn_g, norm_ffn_g, w_in, gla_wa_f, gla_ba_f, gla_wa_b, gla_ba_b, gla_norm_g, swa_sink, w_out, router_w, router_bias, exp_w_gate, exp_w_up, exp_w_down, sh_w_gate, sh_w_up, sh_w_down, final_norm_g):
    l = 0
    d = D_MODEL
    nb_ctx, t_ctx, _ = x_prompt.shape
    nb_lat, t_lat, _ = x_sample.shape

    pad = jnp.zeros((8 - 1 - nb_lat, d), F32)
    cond8 = jnp.concatenate([c_ctx[None, :], c, pad], axis=0)
    mod = _adaln(cond8, w_ada[l], b_ada[l][None, :])
    mods_ctx = [mod[0:1, i * d:(i + 1) * d][:, None, :] for i in range(6)]
    mods_lat = [mod[1:1 + nb_lat, i * d:(i + 1) * d][:, None, :] for i in range(6)]

    zeros_lora = jnp.zeros((GLA_LORA, GLA_QK), F32)
    rw = router_w[l]
    rw_hi = rw.astype(BF16)
    rw_lo = (rw - rw_hi.astype(F32)).astype(BF16)
    w_in_b = w_in[l].astype(BF16)
    p = {
        "norm_attn_g": norm_attn_g[l][None, :],
        "norm_ffn_g": norm_ffn_g[l][None, :],
        "final_norm_g": final_norm_g[None, :],
        "w_gla": w_in_b[:, :2 * GLA_QK + 2 * GLA_V],
        "w_lora": w_in_b[:, 2 * GLA_QK + 2 * GLA_V:2 * GLA_QK + 2 * GLA_V + 2 * GLA_LORA],
        "w_swa": w_in_b[:, 2 * GLA_QK + 2 * GLA_V + 2 * GLA_LORA:],
        "waf": jnp.concatenate([gla_wa_f[l], zeros_lora], axis=0).astype(BF16),
        "wab": jnp.concatenate([zeros_lora, gla_wa_b[l]], axis=0).astype(BF16),
        "baf": gla_ba_f[l][None, :],
        "bab": gla_ba_b[l][None, :],
        "gla_norm_g": gla_norm_g[l][None, :],
        "w_out": w_out[l].astype(BF16),
        "rw_cat": jnp.concatenate([rw_hi, rw_lo], axis=1),
        "rw_hi": rw_hi,
        "rbias": router_bias[l][:, None],
        "wg": exp_w_gate[l], "wu": exp_w_up[l], "wd": exp_w_down[l],
        "swg": sh_w_gate[l].astype(BF16), "swu": sh_w_up[l].astype(BF16),
        "swd": sh_w_down[l].astype(BF16),
    }
    sink = swa_sink[l]

    routed_ctx, k_c, v_c, s_f, s_b = _mix(x_prompt, mods_ctx, p, functools.partial(_attn_ctx, sink))

    cos, sin_lo, sin_hi = _rope_tables(t_lat)
    kc = cache_swa_k[:, l].reshape(nb_lat, -1, SWA_KV)
    vc = cache_swa_v[:, l].reshape(nb_lat, -1, SWA_KV)
    lat_attn = lambda q, k, v: _attn_lat(sink, q, k, v, kc, vc, cos, sin_lo, sin_hi)
    s0 = (state_gla_fwd[:, l].reshape(nb_lat, GLA_QK, GLA_DV),
          state_gla_bwd[:, l].reshape(nb_lat, GLA_QK, GLA_DV))
    routed_lat, _, _, _, _ = _mix(x_sample, mods_lat, p, lat_attn, s0)
    y_prompt, y_sample = _moe([(routed_ctx, mods_ctx[5]), (routed_lat, mods_lat[5])], p)

    new_k = k_c.reshape(nb_ctx, 1, t_ctx, SWA_KV_HEADS, SWA_HEAD_DIM)
    new_v = v_c.reshape(nb_ctx, 1, t_ctx, SWA_KV_HEADS, SWA_HEAD_DIM)
    new_sf = s_f.reshape(nb_ctx, 1, GLA_HEADS, GLA_DK, GLA_DV)
    new_sb = s_b.reshape(nb_ctx, 1, GLA_HEADS, GLA_DK, GLA_DV)
    return (y_prompt, y_sample, new_k, new_v, new_sf, new_sb)
```

```python
import functools

import jax
import jax.numpy as jnp
from jax import lax
from jax.experimental import pallas as pl
from jax.experimental.pallas import tpu as pltpu

F32 = jnp.float32
BF16 = jnp.bfloat16

D_MODEL = 1024
GLA_HEADS = 4
GLA_DK = 64
GLA_DV = 128
GLA_LORA = 16
GLA_GATE_NORM = 16.0
GLA_CHUNK = 64
GLA_QK = GLA_HEADS * GLA_DK
GLA_V = GLA_HEADS * GLA_DV
SWA_HEAD_DIM = 64
SWA_HEADS = 8
SWA_KV_HEADS = 2
SWA_Q = SWA_HEADS * SWA_HEAD_DIM
SWA_KV = SWA_KV_HEADS * SWA_HEAD_DIM
ATTN_BLOCK = 128
GRID_W = 64
ROPE_BASE = 10000.0
N_EXPERTS = 64
TOP_K = 8
N_EXPERT_GROUPS = 8
TOPK_GROUPS = 4
EXPERT_FF = 128
SHARED_FF = 256
ROUTED_SCALE = 2.5
EPS = 1e-6

LANES = 128
VMEM_LIMIT = 56 * 1024 * 1024

NEG_INF = float("-inf")


def _dot(a, b):
    return jnp.dot(a, b, preferred_element_type=F32)


def _dot_nt(a, b):
    return lax.dot_general(a, b, (((1,), (1,)), ((), ())), preferred_element_type=F32)


def _split_hi_lo(x):
    hi = x.astype(BF16)
    lo = (x - hi.astype(F32)).astype(BF16)
    return hi, lo


def _sigmoid(x):
    return 1.0 / (1.0 + jnp.exp(-x))


def _silu(x):
    return x * _sigmoid(x)


def _rms_norm(x, g):
    ms = jnp.mean(x * x, axis=-1, keepdims=True)
    return x * lax.rsqrt(ms + EPS) * g


def _adaln_kernel(c_ref, w_ref, b_ref, o_ref):
    a_hi, a_lo = _split_hi_lo(_silu(c_ref[...]))
    w_hi, w_lo = _split_hi_lo(w_ref[...])
    o_ref[...] = _dot(a_hi, w_hi) + _dot(a_lo, w_hi) + _dot(a_hi, w_lo) + b_ref[...]


def _adaln(cond8, w_ada, b_ada):
    n = w_ada.shape[1]
    tn = 1536
    return pl.pallas_call(
        _adaln_kernel,
        out_shape=jax.ShapeDtypeStruct((8, n), F32),
        grid=(n // tn,),
        in_specs=[pl.BlockSpec((8, D_MODEL), lambda j: (0, 0)),
                  pl.BlockSpec((D_MODEL, tn), lambda j: (0, j)),
                  pl.BlockSpec((1, tn), lambda j: (0, j))],
        out_specs=pl.BlockSpec((8, tn), lambda j: (0, j)),
        compiler_params=pltpu.CompilerParams(dimension_semantics=("arbitrary",),
                                             vmem_limit_bytes=VMEM_LIMIT),
        name="adaln",
    )(cond8, w_ada, b_ada)


def _inproj_kernel(x_ref, g_ref, sh_ref, sc_ref, wg_ref, wl_ref, ws_ref,
                   gla_ref, lora_ref, q_ref, k_ref, v_ref):
    h = _rms_norm(x_ref[...], g_ref[...]) * (1.0 + sc_ref[...]) + sh_ref[...]
    hb = h.astype(BF16)
    gla_ref[...] = _dot(hb, wg_ref[...])
    lora_ref[...] = _dot(hb, wl_ref[...])
    s = _dot(hb, ws_ref[...])
    q_ref[...] = s[:, :SWA_Q]
    k_ref[...] = s[:, SWA_Q:SWA_Q + SWA_KV]
    v_ref[...] = s[:, SWA_Q + SWA_KV:]


INPROJ_TILE = 512


def _inproj(x, g, sh, sc, w_gla, w_lora, w_swa, *, tm):
    b, t, d = x.shape
    nmod = sh.shape[0]
    mod_map = (lambda i, j: (i, 0, 0)) if nmod > 1 else (lambda i, j: (0, 0, 0))
    row = lambda i, j: (i, j, 0)
    full = lambda i, j: (0, 0)
    n_gla = w_gla.shape[1]
    n_lora = w_lora.shape[1]
    return pl.pallas_call(
        _inproj_kernel,
        out_shape=(jax.ShapeDtypeStruct((b, t, n_gla), F32),
                   jax.ShapeDtypeStruct((b, t, n_lora), F32),
                   jax.ShapeDtypeStruct((b, t, SWA_Q), F32),
                   jax.ShapeDtypeStruct((b, t, SWA_KV), F32),
                   jax.ShapeDtypeStruct((b, t, SWA_KV), F32)),
        grid=(b, t // tm),
        in_specs=[pl.BlockSpec((None, tm, d), row),
                  pl.BlockSpec((1, d), full),
                  pl.BlockSpec((None, 1, d), mod_map),
                  pl.BlockSpec((None, 1, d), mod_map),
                  pl.BlockSpec((d, n_gla), full),
                  pl.BlockSpec((d, n_lora), full),
                  pl.BlockSpec((d, w_swa.shape[1]), full)],
        out_specs=(pl.BlockSpec((None, tm, n_gla), row),
                   pl.BlockSpec((None, tm, n_lora), row),
                   pl.BlockSpec((None, tm, SWA_Q), row),
                   pl.BlockSpec((None, tm, SWA_KV), row),
                   pl.BlockSpec((None, tm, SWA_KV), row)),
        compiler_params=pltpu.CompilerParams(dimension_semantics=("arbitrary", "arbitrary"),
                                             vmem_limit_bytes=VMEM_LIMIT),
        name="inproj",
    )(x, g, sh, sc, w_gla, w_lora, w_swa)


SCAN_UNROLL = 2
OUT_UNROLL = 4


def _log_sigmoid(x):
    return jnp.minimum(x, 0.0) - jnp.log(1.0 + jnp.exp(-jnp.abs(x)))


def _heads_to_rows(x):
    return jnp.concatenate([x[:, h * LANES:(h + 1) * LANES] for h in range(GLA_HEADS)], axis=0)


def _rows_to_heads(x, c):
    return jnp.concatenate([x[h * c:(h + 1) * c, :] for h in range(GLA_HEADS)], axis=1)


def _gla_kernel(has_init, q_ref, k_ref, v_ref, g_ref, lora_ref, waf_ref, baf_ref, wab_ref, bab_ref,
                ng_ref, *rest):
    if has_init:
        s0f_ref, s0b_ref, *rest = rest
    (out_ref, sf_ref, sb_ref, laf_ref, lab_ref, oacc_ref, qtf_ref, qtb_ref, saf_ref, sab_ref,
     stf_ref, stb_ref) = rest
    t = q_ref.shape[0]
    c = GLA_CHUNK
    n = t // c
    hc = GLA_HEADS * c

    lora = lora_ref[...].astype(BF16)
    laf_ref[...] = _log_sigmoid(_dot(lora, waf_ref[...]) + baf_ref[...]) * (1.0 / GLA_GATE_NORM)
    lab_ref[...] = _log_sigmoid(_dot(lora, wab_ref[...]) + bab_ref[...]) * (1.0 / GLA_GATE_NORM)

    if has_init:
        stf_ref[...] = s0f_ref[...].T
        stb_ref[...] = s0b_ref[...].T
    else:
        stf_ref[...] = jnp.zeros_like(stf_ref)
        stb_ref[...] = jnp.zeros_like(stb_ref)
    oacc_ref[...] = jnp.zeros_like(oacc_ref)

    r64 = lax.broadcasted_iota(jnp.int32, (c, c), 0)
    c64 = lax.broadcasted_iota(jnp.int32, (c, c), 1)
    tri_f = jnp.where(c64 <= r64, 1.0, 0.0).astype(BF16)
    tri_b = jnp.where(c64 >= r64, 1.0, 0.0).astype(BF16)
    rr = lax.broadcasted_iota(jnp.int32, (hc, hc), 0)
    cc = lax.broadcasted_iota(jnp.int32, (hc, hc), 1)
    same_head = (rr >> 6) == (cc >> 6)
    keep_f = same_head & ((rr & (c - 1)) >= (cc & (c - 1)))
    keep_b = same_head & ((rr & (c - 1)) <= (cc & (c - 1)))
    head_mask = jnp.where(same_head, 1.0, 0.0).astype(BF16)
    norm_g = ng_ref[...]

    def chunk_rows(ci):
        return pl.ds(pl.multiple_of(ci * c, c), c)

    def tile_heads(x):
        x4 = jnp.concatenate([x] * GLA_HEADS, axis=0)
        return jnp.where(same_head, x4, 0.0).astype(BF16)

    def scan_step(i, carry):
        dirs = []
        for u in range(SCAN_UNROLL):
            dirs += [(SCAN_UNROLL * i + u, laf_ref, tri_f, keep_f, c - 1, stf_ref, saf_ref, qtf_ref),
                     (n - 1 - SCAN_UNROLL * i - u, lab_ref, tri_b, keep_b, 0, stb_ref, sab_ref, qtb_ref)]
        cums = []
        for ci, la_ref, tri, _, _, _, _, _ in dirs:
            la_hi, la_lo = _split_hi_lo(la_ref[chunk_rows(ci), :])
            cums.append(_dot(tri, la_hi) + _dot(tri, la_lo))
        ops = []
        for (ci, _, _, _, last_row, _, _, qt_ref), cum in zip(dirs, cums):
            sl = chunk_rows(ci)
            tot = cum[last_row:last_row + 1, :]
            kc = k_ref[sl, :]
            qt = q_ref[sl, :] * (GLA_DK ** -0.5) * jnp.exp(cum)
            qt_ref[sl, :] = qt.astype(BF16)
            v_rows = _heads_to_rows(v_ref[sl, :])
            ops.append((tot, tile_heads(qt), tile_heads(kc * jnp.exp(-cum)),
                        tile_heads(kc * jnp.exp(tot - cum)), v_rows))
        atts = [_dot_nt(q4, k4) for _, q4, k4, _, _ in ops]
        incs = []
        for (_, _, _, keep, _, _, _, _), (_, _, _, kd4, v_rows), att in zip(dirs, ops, atts):
            att = jnp.where(keep, att, 0.0).astype(BF16)
            incs.append((_dot(att, v_rows.astype(BF16)), _dot(v_rows.T.astype(BF16), kd4)))
        for (ci, _, _, _, _, st_ref, snap_ref, _), (tot, _, _, _, _), (o_intra, st_inc) in zip(dirs, ops, incs):
            oacc_ref[ci] += o_intra
            st = st_ref[...]
            snap_ref[ci] = st.astype(BF16)
            st_ref[...] = jnp.exp(tot) * st + st_inc
        return carry

    def tile_heads_bf16(x):
        return jnp.concatenate([x] * GLA_HEADS, axis=0) * head_mask

    def out_step(i, carry):
        chunks = [OUT_UNROLL * i + u for u in range(OUT_UNROLL)]
        inter = []
        for ci in chunks:
            sl = chunk_rows(ci)
            q4 = jnp.concatenate([tile_heads_bf16(qtf_ref[sl, :]), tile_heads_bf16(qtb_ref[sl, :])], axis=1)
            st = jnp.concatenate([saf_ref[ci], sab_ref[ci]], axis=1)
            inter.append(_dot_nt(q4, st))
        for ci, o_inter in zip(chunks, inter):
            sl = chunk_rows(ci)
            on = _rms_norm(oacc_ref[ci] + o_inter, norm_g)
            gate = _silu(_heads_to_rows(g_ref[sl, :]))
            out_ref[sl, :] = _rows_to_heads(on * gate, c)
        return carry

    lax.fori_loop(0, n // SCAN_UNROLL, scan_step, 0)
    lax.fori_loop(0, n // OUT_UNROLL, out_step, 0)
    sf_ref[...] = stf_ref[...].T
    sb_ref[...] = stb_ref[...].T


def _gla(gla_in, lora, waf, baf, wab, bab, norm_g, s0f=None, s0b=None):
    b, t, _ = gla_in.shape
    has_init = s0f is not None
    n = t // GLA_CHUNK
    bmap = lambda i: (i, 0, 0)
    full = lambda i: (0, 0)
    in_specs = [pl.BlockSpec((None, t, GLA_QK), lambda i: (i, 0, 0)),
                pl.BlockSpec((None, t, GLA_QK), lambda i: (i, 0, 1)),
                pl.BlockSpec((None, t, GLA_V), lambda i: (i, 0, 1)),
                pl.BlockSpec((None, t, GLA_V), lambda i: (i, 0, 2)),
                pl.BlockSpec((None, t, 2 * GLA_LORA), bmap),
                pl.BlockSpec((2 * GLA_LORA, GLA_QK), full),
                pl.BlockSpec((1, GLA_QK), full),
                pl.BlockSpec((2 * GLA_LORA, GLA_QK), full),
                pl.BlockSpec((1, GLA_QK), full),
                pl.BlockSpec((1, GLA_DV), full)]
    args = [gla_in, gla_in, gla_in, gla_in, lora, waf, baf, wab, bab, norm_g]
    if has_init:
        in_specs += [pl.BlockSpec((None, GLA_QK, GLA_DV), bmap)] * 2
        args += [s0f, s0b]
    return pl.pallas_call(
        functools.partial(_gla_kernel, has_init),
        out_shape=(jax.ShapeDtypeStruct((b, t, GLA_V), F32),
                   jax.ShapeDtypeStruct((b, GLA_QK, GLA_DV), F32),
                   jax.ShapeDtypeStruct((b, GLA_QK, GLA_DV), F32)),
        grid=(b,),
        in_specs=in_specs,
        out_specs=(pl.BlockSpec((None, t, GLA_V), bmap),
                   pl.BlockSpec((None, GLA_QK, GLA_DV), bmap),
                   pl.BlockSpec((None, GLA_QK, GLA_DV), bmap)),
        scratch_shapes=[pltpu.VMEM((t, GLA_QK), F32),
                        pltpu.VMEM((t, GLA_QK), F32),
                        pltpu.VMEM((n, GLA_HEADS * GLA_CHUNK, GLA_DV), F32),
                        pltpu.VMEM((t, GLA_QK), BF16),
                        pltpu.VMEM((t, GLA_QK), BF16),
                        pltpu.VMEM((n, GLA_DV, GLA_QK), BF16),
                        pltpu.VMEM((n, GLA_DV, GLA_QK), BF16),
                        pltpu.VMEM((GLA_DV, GLA_QK), F32),
                        pltpu.VMEM((GLA_DV, GLA_QK), F32)],
        compiler_params=pltpu.CompilerParams(dimension_semantics=("arbitrary",),
                                             vmem_limit_bytes=VMEM_LIMIT),
        name="gla",
    )(*args)


def _dup_groups(x):
    lo = lax.broadcasted_iota(jnp.int32, x.shape, 1) < SWA_HEAD_DIM
    xr = pltpu.roll(x, SWA_HEAD_DIM, axis=1)
    return jnp.where(lo, x, xr), jnp.where(lo, xr, x)


def _pairs_attention(qps, sinks, k_dups, vt_dups, mask):
    nq = qps[0].shape[0]
    lo = lax.broadcasted_iota(jnp.int32, (nq, LANES), 1) < SWA_HEAD_DIM
    even = lax.broadcasted_iota(jnp.int32, (1, 2 * nq), 1) < nq
    scores = []
    for qp, k_dup in zip(qps, k_dups):
        q2 = jnp.concatenate([jnp.where(lo, qp, 0.0), jnp.where(lo, 0.0, qp)], axis=0).astype(BF16)
        scores.append(_dot_nt(k_dup, q2))
    probs = []
    for s, (sink_even, sink_odd) in zip(scores, sinks):
        if mask is not None:
            s = jnp.where(mask, s, NEG_INF)
        sink = jnp.where(even, sink_even, sink_odd)
        m = jnp.maximum(jnp.max(s, axis=0, keepdims=True), sink)
        p = jnp.exp(s - m)
        denom = jnp.sum(p, axis=0, keepdims=True) + jnp.exp(sink - m)
        probs.append((p.astype(BF16), 1.0 / denom))
    outs = []
    for (p, rdenom), vt_dup in zip(probs, vt_dups):
        o = _dot(vt_dup, p) * rdenom
        outs.append(jnp.concatenate([o[:SWA_HEAD_DIM, :nq], o[SWA_HEAD_DIM:, nq:]], axis=0).T)
    return outs


CTX_BATCH = 2


def _attn_ctx_kernel(sink_ref, q_ref, k_ref, v_ref, o_ref):
    scale = SWA_HEAD_DIM ** -0.5
    pairs = range(SWA_HEADS // 2)
    items = [(bb, pr) for bb in range(q_ref.shape[0]) for pr in pairs]
    kd = [[x.astype(BF16) for x in _dup_groups(k_ref[bb])] for bb in range(q_ref.shape[0])]
    vt = [[x.T.astype(BF16) for x in _dup_groups(v_ref[bb])] for bb in range(q_ref.shape[0])]
    outs = _pairs_attention([q_ref[bb, :, pr * LANES:(pr + 1) * LANES] * scale for bb, pr in items],
                            [(sink_ref[2 * pr], sink_ref[2 * pr + 1]) for _, pr in items],
                            [kd[bb][pr // 2] for bb, pr in items], [vt[bb][pr // 2] for bb, pr in items], None)
    for (bb, pr), out in zip(items, outs):
        o_ref[bb, :, pr * LANES:(pr + 1) * LANES] = out


def _attn_ctx(sink, q, k, v):
    b, t, _ = q.shape
    bmap = lambda i: (i, 0, 0)
    return pl.pallas_call(
        _attn_ctx_kernel,
        out_shape=jax.ShapeDtypeStruct((b, t, SWA_Q), F32),
        grid=(b // CTX_BATCH,),
        in_specs=[pl.BlockSpec(memory_space=pltpu.SMEM),
                  pl.BlockSpec((CTX_BATCH, t, SWA_Q), bmap),
                  pl.BlockSpec((CTX_BATCH, t, SWA_KV), bmap),
                  pl.BlockSpec((CTX_BATCH, t, SWA_KV), bmap)],
        out_specs=pl.BlockSpec((CTX_BATCH, t, SWA_Q), bmap),
        compiler_params=pltpu.CompilerParams(dimension_semantics=("arbitrary",),
                                             vmem_limit_bytes=VMEM_LIMIT),
        name="attn_ctx",
    )(sink, q, k, v)


def _rope(x, cos, sin_lo, sin_hi):
    return x * cos + pltpu.roll(x, LANES - 16, axis=1) * sin_lo + pltpu.roll(x, 16, axis=1) * sin_hi


def _attn_lat_kernel(sink_ref, q_ref, k_ref, v_ref, kc_ref, vc_ref, cos_ref, sl_ref, sh_ref,
                     o_ref, kw_ref, vw_ref):
    t = q_ref.shape[0]
    ab = ATTN_BLOCK
    nb = t // ab
    scale = SWA_HEAD_DIM ** -0.5

    k_rot = _dup_groups(_rope(k_ref[...], cos_ref[...], sl_ref[...], sh_ref[...]))
    v_dup = _dup_groups(v_ref[...])
    zeros = jnp.zeros((ab, LANES), BF16)
    for grp in range(SWA_KV_HEADS):
        kw_ref[grp, 0:ab, :] = zeros
        kw_ref[grp, ab:ab + t, :] = k_rot[grp].astype(BF16)
        kw_ref[grp, ab + t:, :] = zeros
        vw_ref[grp, 0] = zeros
        for blk in range(nb):
            vw_ref[grp, blk + 1] = v_dup[grp][blk * ab:(blk + 1) * ab, :].T.astype(BF16)
        vw_ref[grp, nb + 1] = zeros
    kc = [x.astype(BF16) for x in _dup_groups(kc_ref[...])]
    vct = [x.T.astype(BF16) for x in _dup_groups(vc_ref[...])]
    lc = kc_ref.shape[0]

    key = lax.broadcasted_iota(jnp.int32, (lc + 3 * ab, 2 * ab), 0) - lc
    tq = lax.broadcasted_iota(jnp.int32, (lc + 3 * ab, 2 * ab), 1) & (ab - 1)
    band = (key < 0) | (jnp.abs(tq + ab - key) <= ab)

    def block(nq, carry):
        row0 = pl.multiple_of(nq * ab, ab)
        s_abs = key + (nq - 1) * ab
        mask = band & ((key < 0) | ((s_abs >= 0) & (s_abs < t)))
        cos = cos_ref[pl.ds(row0, ab), :]
        s_lo = sl_ref[pl.ds(row0, ab), :]
        s_hi = sh_ref[pl.ds(row0, ab), :]
        k_all = [jnp.concatenate([kc[grp], kw_ref[grp, pl.ds(row0, 3 * ab), :]], axis=0)
                 for grp in range(SWA_KV_HEADS)]
        vt_all = [jnp.concatenate([vct[grp], vw_ref[grp, nq], vw_ref[grp, nq + 1], vw_ref[grp, nq + 2]],
                                  axis=1) for grp in range(SWA_KV_HEADS)]
        pairs = range(SWA_HEADS // 2)
        qps = [_rope(q_ref[pl.ds(row0, ab), pr * LANES:(pr + 1) * LANES], cos, s_lo, s_hi) * scale
               for pr in pairs]
        outs = _pairs_attention(qps, [(sink_ref[2 * pr], sink_ref[2 * pr + 1]) for pr in pairs],
                                [k_all[pr // 2] for pr in pairs], [vt_all[pr // 2] for pr in pairs], mask)
        for pr in pairs:
            o_ref[pl.ds(row0, ab), pr * LANES:(pr + 1) * LANES] = outs[pr]
        return carry

    lax.fori_loop(0, nb, block, 0)


def _attn_lat(sink, q, k, v, kc, vc, cos, sin_lo, sin_hi):
    b, t, _ = q.shape
    lc = kc.shape[1]
    bmap = lambda i: (i, 0, 0)
    full = lambda i: (0, 0)
    return pl.pallas_call(
        _attn_lat_kernel,
        out_shape=jax.ShapeDtypeStruct((b, t, SWA_Q), F32),
        grid=(b,),
        in_specs=[pl.BlockSpec(memory_space=pltpu.SMEM),
                  pl.BlockSpec((None, t, SWA_Q), bmap),
                  pl.BlockSpec((None, t, SWA_KV), bmap),
                  pl.BlockSpec((None, t, SWA_KV), bmap),
                  pl.BlockSpec((None, lc, SWA_KV), bmap),
                  pl.BlockSpec((None, lc, SWA_KV), bmap),
                  pl.BlockSpec((t, LANES), full),
                  pl.BlockSpec((t, LANES), full),
                  pl.BlockSpec((t, LANES), full)],
        out_specs=pl.BlockSpec((None, t, SWA_Q), bmap),
        scratch_shapes=[pltpu.VMEM((SWA_KV_HEADS, t + 2 * ATTN_BLOCK, LANES), BF16),
                        pltpu.VMEM((SWA_KV_HEADS, t // ATTN_BLOCK + 2, LANES, ATTN_BLOCK), BF16)],
        compiler_params=pltpu.CompilerParams(dimension_semantics=("arbitrary",),
                                             vmem_limit_bytes=VMEM_LIMIT),
        name="attn_lat",
    )(sink, q, k, v, kc, vc, cos, sin_lo, sin_hi)


def _rope_tables(t):
    half = SWA_HEAD_DIM // 2
    quarter = half // 2
    pos = jnp.arange(t)
    row = (pos // GRID_W).astype(F32)
    col = (pos % GRID_W).astype(F32)
    inv_freq = ROPE_BASE ** (-jnp.arange(quarter, dtype=F32) / quarter)
    lane = jnp.arange(LANES)
    d = lane % SWA_HEAD_DIM
    freq = inv_freq[d % quarter]
    use_row = (d < half)
    ang = jnp.where(use_row[None, :], row[:, None], col[:, None]) * freq[None, :]
    cos = jnp.cos(ang)
    sin = jnp.sin(ang)
    lower = (d % half) < quarter
    return cos, jnp.where(lower[None, :], -sin, 0.0), jnp.where(lower[None, :], 0.0, sin)


def _route(sel, scores):
    n = sel.shape[1]
    gsz = N_EXPERTS // N_EXPERT_GROUPS

    def first_max(x, idx, size):
        m = jnp.max(x, axis=0, keepdims=True)
        first = jnp.min(jnp.where(x == m, idx, float(size)), axis=0, keepdims=True)
        return m, idx == first

    i8 = lax.broadcasted_iota(jnp.int32, (gsz, n), 0).astype(F32)
    rows = []
    for g in range(N_EXPERT_GROUPS):
        slab = sel[g * gsz:(g + 1) * gsz, :]
        m1, hit = first_max(slab, i8, gsz)
        m2 = jnp.max(jnp.where(hit, NEG_INF, slab), axis=0, keepdims=True)
        rows.append(m1 + m2)
    gscore = jnp.concatenate(rows, axis=0)
    gsel = jnp.zeros((N_EXPERT_GROUPS, n), F32)
    for _ in range(TOPK_GROUPS):
        _, hit = first_max(gscore, i8, N_EXPERT_GROUPS)
        gsel = jnp.where(hit, 1.0, gsel)
        gscore = jnp.where(hit, NEG_INF, gscore)
    emask = jnp.concatenate(
        [jnp.broadcast_to(gsel[g:g + 1, :], (gsz, n)) for g in range(N_EXPERT_GROUPS)], axis=0)
    cand = jnp.where(emask > 0.5, sel, NEG_INF)
    ie = lax.broadcasted_iota(jnp.int32, (N_EXPERTS, n), 0).astype(F32)
    w = jnp.zeros((N_EXPERTS, n), F32)
    chosen = jnp.zeros((N_EXPERTS, n), F32)
    hits = []
    for _ in range(TOP_K):
        _, hit = first_max(cand, ie, N_EXPERTS)
        hits.append(hit)
        w = jnp.where(hit, scores, w)
        chosen = jnp.where(hit, 1.0, chosen)
        cand = jnp.where(hit, NEG_INF, cand)
    gates = w / jnp.sum(w, axis=0, keepdims=True) * ROUTED_SCALE

    s_idx = lax.broadcasted_iota(jnp.int32, (n, n), 0)
    t_idx = lax.broadcasted_iota(jnp.int32, (n, n), 1)
    before = jnp.where(s_idx < t_idx, 1.0, 0.0).astype(BF16)
    rank = _dot(chosen.astype(BF16), before)
    count = jnp.sum(chosen, axis=1, keepdims=True)
    padded = jnp.floor((count + (SORT_ALIGN - 1)) * (1.0 / SORT_ALIGN)) * SORT_ALIGN
    padded = jnp.broadcast_to(padded, (N_EXPERTS, LANES))
    e_row = lax.broadcasted_iota(jnp.int32, (N_EXPERTS, N_EXPERTS), 0)
    e_col = lax.broadcasted_iota(jnp.int32, (N_EXPERTS, N_EXPERTS), 1)
    below = jnp.where(e_col < e_row, 1.0, 0.0).astype(BF16)
    start = _dot(below, padded.astype(BF16))
    row = start[:, 0:1] + rank
    pos = jnp.concatenate([jnp.sum(jnp.where(h, row, 0.0), axis=0, keepdims=True) for h in hits], axis=0)
    wts = jnp.concatenate([jnp.sum(jnp.where(h, gates, 0.0), axis=0, keepdims=True) for h in hits], axis=0)
    return pos, wts, padded, start


def _outproj_kernel(gla_ref, att_ref, x_ref, wo_ref, g1_ref, sh_ref, sc_ref, ng_ref, rw_ref, rwh_ref,
                    rb_ref, x1_ref, xm_ref, pos_ref, wts_ref, cnt_ref, start_ref):
    y = (_dot(gla_ref[...].astype(BF16), wo_ref[0:GLA_V, :])
         + _dot(att_ref[...].astype(BF16), wo_ref[GLA_V:, :]))
    x1 = x_ref[...] + g1_ref[...] * y
    x1_ref[...] = x1
    xm = _rms_norm(x1, ng_ref[...]) * (1.0 + sc_ref[...]) + sh_ref[...]
    xm_hi, xm_lo = _split_hi_lo(xm)
    xm_ref[...] = xm_hi
    lg = _dot(xm_hi, rw_ref[...])
    logits = lg[:, :N_EXPERTS] + lg[:, N_EXPERTS:] + _dot(xm_lo, rwh_ref[...])
    tm = logits.shape[0]
    lt = jnp.concatenate([logits, jnp.zeros((tm, LANES - N_EXPERTS), F32)], axis=1).T[:N_EXPERTS, :]
    scores = _sigmoid(lt)
    pos_ref[...], wts_ref[...], cnt_ref[...], start_ref[...] = _route(scores + rb_ref[...], scores)


def _outproj(gla_out, att_out, x, w_out, g1, sh2, sc2, norm_g, rw_cat, rw_hi, rbias, *, tm):
    b, t, d = x.shape
    nmod = g1.shape[0]
    mod_map = (lambda i, j: (i, 0, 0)) if nmod > 1 else (lambda i, j: (0, 0, 0))
    row = lambda i, j: (i, j, 0)
    full = lambda i, j: (0, 0)
    tile = lambda i, j: (i, j, 0, 0)
    nt = t // tm
    return pl.pallas_call(
        _outproj_kernel,
        out_shape=(jax.ShapeDtypeStruct((b, t, d), F32),
                   jax.ShapeDtypeStruct((b, t, d), BF16),
                   jax.ShapeDtypeStruct((b, nt, TOP_K, tm), F32),
                   jax.ShapeDtypeStruct((b, nt, TOP_K, tm), F32),
                   jax.ShapeDtypeStruct((b, nt, N_EXPERTS, LANES), F32),
                   jax.ShapeDtypeStruct((b, nt, N_EXPERTS, LANES), F32)),
        grid=(b, t // tm),
        in_specs=[pl.BlockSpec((None, tm, GLA_V), row),
                  pl.BlockSpec((None, tm, SWA_Q), row),
                  pl.BlockSpec((None, tm, d), row),
                  pl.BlockSpec((d, d), full),
                  pl.BlockSpec((None, 1, d), mod_map),
                  pl.BlockSpec((None, 1, d), mod_map),
                  pl.BlockSpec((None, 1, d), mod_map),
                  pl.BlockSpec((1, d), full),
                  pl.BlockSpec((d, 2 * N_EXPERTS), full),
                  pl.BlockSpec((d, N_EXPERTS), full),
                  pl.BlockSpec((N_EXPERTS, 1), full)],
        out_specs=(pl.BlockSpec((None, tm, d), row),
                   pl.BlockSpec((None, tm, d), row),
                   pl.BlockSpec((None, None, TOP_K, tm), tile),
                   pl.BlockSpec((None, None, TOP_K, tm), tile),
                   pl.BlockSpec((None, None, N_EXPERTS, LANES), tile),
                   pl.BlockSpec((None, None, N_EXPERTS, LANES), tile)),
        compiler_params=pltpu.CompilerParams(dimension_semantics=("arbitrary", "arbitrary"),
                                             vmem_limit_bytes=VMEM_LIMIT),
        name="outproj",
    )(gla_out, att_out, x, w_out, g1, sh2, sc2, norm_g, rw_cat, rw_hi, rbias)


MOE_TILE = 256
SORT_ALIGN = 16
SORT_ROWS = 3072
ROW_TILE = 512
GATHER_SLOTS = 3
FFN_CHAINS = 4
COMBINE_CHUNK = 1024
ALWAYS_ROWS = 2560
COMBINE_TAIL = 512


def _moe_sort_kernel(tiles_a, used_ref, xa_ref, xb_ref, pos_ref, xs_ref):
    i = pl.program_id(0)
    x = jnp.where(i < tiles_a, xa_ref[...], xb_ref[...])
    pos = pos_ref[...]
    tm = x.shape[0]
    used = used_ref[i]

    def fill(blk):
        rows = (lax.broadcasted_iota(jnp.int32, (tm, tm), 0) + blk * tm).astype(F32)
        onehot = jnp.zeros((tm, tm), F32)
        for k in range(TOP_K):
            onehot = jnp.where(rows == pos[k:k + 1, :], 1.0, onehot)
        xs_ref[blk * tm:(blk + 1) * tm, :] = _dot(onehot.astype(BF16), x).astype(BF16)

    for blk in range(SORT_ROWS // tm):
        if (blk + 1) * tm <= ALWAYS_ROWS:
            fill(blk)
        else:
            pl.when(blk * tm < used)(functools.partial(fill, blk))

            @pl.when(blk * tm >= used)
            def _():
                xs_ref[blk * tm:(blk + 1) * tm, :] = jnp.zeros((tm, D_MODEL), BF16)


def _moe_sort(xm_a, xm_b, pos, used):
    d = xm_a.shape[1]
    nt, _, tm = pos.shape
    tiles_a = xm_a.shape[0] // tm
    grid_spec = pltpu.PrefetchScalarGridSpec(
        num_scalar_prefetch=1,
        grid=(nt,),
        in_specs=[pl.BlockSpec((tm, d), lambda i, u: (jnp.minimum(i, tiles_a - 1), 0)),
                  pl.BlockSpec((tm, d), lambda i, u: (jnp.maximum(i - tiles_a, 0), 0)),
                  pl.BlockSpec((None, TOP_K, tm), lambda i, u: (i, 0, 0))],
        out_specs=pl.BlockSpec((SORT_ROWS, d), lambda i, u: (i, 0)))
    return pl.pallas_call(
        functools.partial(_moe_sort_kernel, tiles_a),
        out_shape=jax.ShapeDtypeStruct((nt * SORT_ROWS, d), BF16),
        grid_spec=grid_spec,
        compiler_params=pltpu.CompilerParams(dimension_semantics=("arbitrary",),
                                             vmem_limit_bytes=VMEM_LIMIT),
        name="moe_sort",
    )(used, xm_a, xm_b, pos)


def _moe_row_tiles(n_tokens):
    rows = n_tokens * TOP_K + (n_tokens // MOE_TILE) * N_EXPERTS * (SORT_ALIGN - 1) + N_EXPERTS * (ROW_TILE - 1)
    return -(-rows // ROW_TILE) + GATHER_SLOTS - 1


PLAN_CHUNK = 1280


def _int_dot_r(a, onehot):
    hi = jnp.floor(a * (1.0 / 256.0))
    return _dot(hi.astype(BF16), onehot) * 256.0 + _dot((a - hi * 256.0).astype(BF16), onehot)


def _int_dot_l(onehot, b):
    hi = jnp.floor(b * (1.0 / 256.0))
    return _dot(onehot, hi.astype(BF16)) * 256.0 + _dot(onehot, (b - hi * 256.0).astype(BF16))


def _moe_plan_kernel(cnt_ref, start_ref, src_ref, first_ref, tiles_ref, nu_ref, back_ref):
    nt, ne = cnt_ref.shape
    gpt = SORT_ROWS // SORT_ALIGN
    gpr = ROW_TILE // SORT_ALIGN
    gc = cnt_ref[...] * (1.0 / SORT_ALIGN)
    ls = start_ref[...] * (1.0 / SORT_ALIGN)

    def transpose(x):
        x = jnp.concatenate([x, jnp.zeros((nt, LANES - ne), F32)], axis=1)
        x = jnp.concatenate([x, jnp.zeros((LANES - nt, LANES), F32)], axis=0)
        return x.T[:ne, :nt]

    def tri(n, keep):
        return jnp.where(keep(lax.broadcasted_iota(jnp.int32, (n, n), 0),
                              lax.broadcasted_iota(jnp.int32, (n, n), 1)), 1.0, 0.0).astype(BF16)

    gc_t = transpose(gc)
    ls_t = transpose(ls)
    tot_c = jnp.broadcast_to(jnp.sum(gc_t, axis=1, keepdims=True), (ne, LANES))
    ptot_c = jnp.floor((tot_c + (gpr - 1)) * (1.0 / gpr)) * gpr
    gend_c = _int_dot_l(tri(ne, lambda r, c: c <= r), ptot_c)
    gstart_c = gend_c - ptot_c
    n_used = gend_c[ne - 1:ne, :] * (1.0 / gpr)
    nu_ref[...] = n_used.astype(jnp.int32)
    tot_r = jnp.sum(gc, axis=0, keepdims=True)
    ptot_r = jnp.floor((tot_r + (gpr - 1)) * (1.0 / gpr)) * gpr
    gstart_r = _int_dot_r(jnp.broadcast_to(ptot_r, (8, ne)), tri(ne, lambda r, c: r < c))
    cumex = _dot(tri(nt, lambda r, c: c < r), gc.astype(BF16))
    cumex_t = _dot(gc_t.astype(BF16), tri(nt, lambda r, c: r < c))
    tile_base = lax.broadcasted_iota(jnp.int32, (nt, ne), 0).astype(F32) * gpt + ls
    table = jnp.concatenate([cumex + gc, cumex, tile_base, gstart_r, jnp.broadcast_to(tot_r, (8, ne))], axis=0)

    e_iota = lax.broadcasted_iota(jnp.int32, (ne, PLAN_CHUNK), 0).astype(F32)
    for ch in range(src_ref.shape[1] // PLAN_CHUNK):
        g = (lax.broadcasted_iota(jnp.int32, (1, PLAN_CHUNK), 1) + ch * PLAN_CHUNK).astype(F32)
        eg = jnp.sum(jnp.where(gend_c[:, 0:1] <= g, 1.0, 0.0), axis=0, keepdims=True)
        picked = _int_dot_r(table, jnp.where(e_iota == eg, 1.0, 0.0).astype(BF16))
        cum_g, cumex_g, base_g = picked[0:nt], picked[nt:2 * nt], picked[2 * nt:3 * nt]
        u = g - picked[3 * nt:3 * nt + 1]
        in_tile = (cumex_g <= u) & (u < cum_g)
        src = jnp.sum(jnp.where(in_tile, base_g - cumex_g, 0.0), axis=0, keepdims=True) + u
        src = jnp.where(u < picked[3 * nt + 8:3 * nt + 9], src, gpt - 1.0)
        src_ref[:, ch * PLAN_CHUNK:(ch + 1) * PLAN_CHUNK] = src.astype(jnp.int32)

    first_ref[...] = (gstart_c * (1.0 / gpr)).astype(jnp.int32)
    tiles_ref[...] = (ptot_c * (1.0 / gpr)).astype(jnp.int32)

    lg = lax.broadcasted_iota(jnp.int32, (ne, back_ref.shape[1]), 1).astype(F32)
    for t in range(nt):
        first = ls_t[:, t:t + 1]
        inside = (first <= lg) & (lg < first + gc_t[:, t:t + 1])
        shift = gstart_c[:, 0:1] + cumex_t[:, t:t + 1] - first
        val = jnp.sum(jnp.where(inside, shift + lg, 0.0), axis=0, keepdims=True)
        back_ref[t:t + 1, :] = val.astype(jnp.int32)


def _moe_plan(cnt, start):
    nt, ne = cnt.shape
    row_tiles = _moe_row_tiles(nt * MOE_TILE)
    gpt = SORT_ROWS // SORT_ALIGN
    gpr = ROW_TILE // SORT_ALIGN
    n_src = -(-(row_tiles * gpr) // PLAN_CHUNK) * PLAN_CHUNK
    n_back = -(-gpt // LANES) * LANES
    src, first, tiles, nu, back = pl.pallas_call(
        _moe_plan_kernel,
        out_shape=(jax.ShapeDtypeStruct((1, n_src), jnp.int32),
                   jax.ShapeDtypeStruct((ne, LANES), jnp.int32),
                   jax.ShapeDtypeStruct((ne, LANES), jnp.int32),
                   jax.ShapeDtypeStruct((1, LANES), jnp.int32),
                   jax.ShapeDtypeStruct((nt, n_back), jnp.int32)),
        compiler_params=pltpu.CompilerParams(vmem_limit_bytes=VMEM_LIMIT),
        name="moe_plan",
    )(cnt, start)
    return nu[0, :1], first[:, 0], tiles[:, 0], src[0, :row_tiles * gpr], back[:, :gpt]


def _moe_experts_kernel(nu_ref, first_ref, tiles_ref, src_ref, xs_hbm, wg_ref, wu_ref, wd_ref, ys_hbm,
                        xbuf, ybuf, gsem, osem, wgu_s, wd_s):
    e = pl.program_id(0)
    n_used = nu_ref[0]
    gpr = ROW_TILE // SORT_ALIGN
    part = ROW_TILE // FFN_CHAINS

    def gather(tile, to_slot, j0=0, j1=gpr):
        for j in range(j0, j1):
            row = pl.multiple_of(src_ref[tile * gpr + j] * SORT_ALIGN, SORT_ALIGN)
            pltpu.make_async_copy(xs_hbm.at[pl.ds(row, SORT_ALIGN), :],
                                  xbuf.at[to_slot, j * SORT_ALIGN:(j + 1) * SORT_ALIGN, :],
                                  gsem.at[to_slot]).start(priority=j % 2)

    def drain(of_slot):
        for j in range(gpr):
            pltpu.make_async_copy(xs_hbm.at[0:SORT_ALIGN, :],
                                  xbuf.at[of_slot, j * SORT_ALIGN:(j + 1) * SORT_ALIGN, :], gsem.at[of_slot]).wait()

    def out_copy(tile, of_slot):
        row = pl.multiple_of(tile * ROW_TILE, ROW_TILE)
        return pltpu.make_async_copy(ybuf.at[of_slot], ys_hbm.at[pl.ds(row, ROW_TILE), :], osem.at[of_slot])

    @pl.when(e == 0)
    def _():
        gather(0, 0)
        gather(1, 1)

    wgu_s[:, :EXPERT_FF] = wg_ref[...].astype(BF16)
    wgu_s[:, EXPERT_FF:] = wu_ref[...].astype(BF16)
    wd_s[...] = wd_ref[...].astype(BF16)

    def row_tile(i, carry):
        r = first_ref[e] + i
        slot = lax.rem(r, GATHER_SLOTS)
        oslot = lax.rem(r, 2)
        next_slot = lax.rem(r + 2, GATHER_SLOTS)
        drain(slot)

        @pl.when(r >= 2)
        def _():
            out_copy(r - 2, oslot).wait()

        abs_ = []
        for c in range(FFN_CHAINS):
            abs_.append(_dot(xbuf[slot, c * part:(c + 1) * part, :], wgu_s[...]))
            gather(r + 2, next_slot, c * gpr // FFN_CHAINS, (c + 1) * gpr // FFN_CHAINS)
        hs = [(_silu(ab[:, :EXPERT_FF]) * ab[:, EXPERT_FF:]).astype(BF16) for ab in abs_]
        ys = [_dot(h, wd_s[...]).astype(BF16) for h in hs]
        for c in range(FFN_CHAINS):
            ybuf[oslot, c * part:(c + 1) * part, :] = ys[c]
        out_copy(r, oslot).start()
        return carry

    lax.fori_loop(0, tiles_ref[e], row_tile, 0)

    @pl.when(e == pl.num_programs(0) - 1)
    def _():
        drain(lax.rem(n_used, GATHER_SLOTS))
        drain(lax.rem(n_used + 1, GATHER_SLOTS))
        out_copy(n_used - 1, lax.rem(n_used - 1, 2)).wait()

        @pl.when(n_used >= 2)
        def _():
            out_copy(n_used - 2, lax.rem(n_used, 2)).wait()


def _moe_experts(n_used, first, tiles, src, xs, wg, wu, wd, row_tiles):
    d = xs.shape[-1]
    ne = wg.shape[0]
    w_map = lambda e, nu, fi, ti, sr: (e, 0, 0)
    grid_spec = pltpu.PrefetchScalarGridSpec(
        num_scalar_prefetch=4,
        grid=(ne,),
        in_specs=[pl.BlockSpec(memory_space=pl.ANY),
                  pl.BlockSpec((None, d, EXPERT_FF), w_map),
                  pl.BlockSpec((None, d, EXPERT_FF), w_map),
                  pl.BlockSpec((None, EXPERT_FF, d), w_map)],
        out_specs=pl.BlockSpec(memory_space=pl.ANY),
        scratch_shapes=[pltpu.VMEM((GATHER_SLOTS, ROW_TILE, d), BF16),
                        pltpu.VMEM((2, ROW_TILE, d), BF16),
                        pltpu.SemaphoreType.DMA((GATHER_SLOTS,)),
                        pltpu.SemaphoreType.DMA((2,)),
                        pltpu.VMEM((d, 2 * EXPERT_FF), BF16),
                        pltpu.VMEM((EXPERT_FF, d), BF16)])
    return pl.pallas_call(
        _moe_experts_kernel,
        out_shape=jax.ShapeDtypeStruct((row_tiles * ROW_TILE, d), BF16),
        grid_spec=grid_spec,
        compiler_params=pltpu.CompilerParams(dimension_semantics=("arbitrary",),
                                             vmem_limit_bytes=VMEM_LIMIT),
        name="moe_experts",
    )(n_used, first, tiles, src, xs, wg, wu, wd)


def _moe_combine_kernel(back_ref, used_ref, ys_hbm, pos_ref, wts_ref, xm_ref, x1_ref, g2_ref, fg_ref,
                        swg_ref, swu_ref, swd_ref, o_ref, buf, sem, acc_ref):
    i = pl.program_id(0)
    gpt = SORT_ROWS // SORT_ALIGN
    slot = lax.rem(i, 2)
    always = ALWAYS_ROWS
    tail = range(always, SORT_ROWS, COMBINE_TAIL)

    def copies(tile, of_slot, g0, g1, start):
        for g in range(g0, g1):
            row = pl.multiple_of(back_ref[tile * gpt + g] * SORT_ALIGN, SORT_ALIGN) if start else 0
            cp = pltpu.make_async_copy(ys_hbm.at[pl.ds(row, SORT_ALIGN), :],
                                       buf.at[of_slot, g * SORT_ALIGN:(g + 1) * SORT_ALIGN, :], sem.at[of_slot])
            if start:
                cp.start(priority=g % 2)
            else:
                cp.wait()

    def transfer(tile, of_slot, start):
        copies(tile, of_slot, 0, always // SORT_ALIGN, start)
        for c0 in tail:
            pl.when(c0 < used_ref[tile])(functools.partial(
                copies, tile, of_slot, c0 // SORT_ALIGN, (c0 + COMBINE_TAIL) // SORT_ALIGN, start))

    @pl.when(i == 0)
    def _():
        transfer(0, 0, True)

    @pl.when(i + 1 < pl.num_programs(0))
    def _():
        transfer(i + 1, 1 - slot, True)

    x = xm_ref[...]
    tm = x.shape[0]
    pad = jnp.zeros((LANES - TOP_K, tm), F32)
    pos_t = jnp.concatenate([pos_ref[...], pad], axis=0).T
    wts_t = jnp.concatenate([wts_ref[...], pad], axis=0).T
    pos_b = [jnp.broadcast_to(pos_t[:, k:k + 1], (tm, LANES)) for k in range(TOP_K)]
    wts_b = [jnp.broadcast_to(wts_t[:, k:k + 1], (tm, LANES)) for k in range(TOP_K)]
    shared = _dot((_silu(_dot(x, swg_ref[...])) * _dot(x, swu_ref[...])).astype(BF16), swd_ref[...])
    transfer(i, slot, False)

    def apply(c0, width):
        reps = width // LANES
        rows = (lax.broadcasted_iota(jnp.int32, (tm, width), 1) + c0).astype(F32)
        comb = jnp.zeros((tm, width), F32)
        for k in range(TOP_K):
            comb = jnp.where(rows == jnp.concatenate([pos_b[k]] * reps, axis=1),
                             jnp.concatenate([wts_b[k]] * reps, axis=1), comb)
        return _dot(comb.astype(BF16), buf[slot, c0:c0 + width, :])

    routed = shared
    for c0 in range(0, always, COMBINE_CHUNK):
        routed = routed + apply(c0, min(COMBINE_CHUNK, always - c0))
    acc_ref[...] = routed
    for c0 in tail:
        @pl.when(c0 < used_ref[i])
        def _(c0=c0):
            acc_ref[...] += apply(c0, COMBINE_TAIL)
    y = x1_ref[...] + g2_ref[...] * acc_ref[...]
    o_ref[...] = _rms_norm(y, fg_ref[...])


def _moe_combine(back, used, ys, pos, wts, xm, x1, g2, final_g, swg, swu, swd, *, tiles_per_mod):
    n, d = xm.shape
    tm = pos.shape[-1]
    nt = n // tm
    gpt = SORT_ROWS // SORT_ALIGN
    row = lambda i, bk, us: (i, 0)
    full = lambda i, bk, us: (0, 0)
    tile = lambda i, bk, us: (i, 0, 0)
    mod_map = lambda i, bk, us: (i // tiles_per_mod, 0, 0)
    grid_spec = pltpu.PrefetchScalarGridSpec(
        num_scalar_prefetch=2,
        grid=(nt,),
        in_specs=[pl.BlockSpec(memory_space=pl.ANY),
                  pl.BlockSpec((None, TOP_K, tm), tile),
                  pl.BlockSpec((None, TOP_K, tm), tile),
                  pl.BlockSpec((tm, d), row),
                  pl.BlockSpec((tm, d), row),
                  pl.BlockSpec((None, 1, d), mod_map),
                  pl.BlockSpec((1, d), full),
                  pl.BlockSpec((d, SHARED_FF), full),
                  pl.BlockSpec((d, SHARED_FF), full),
                  pl.BlockSpec((SHARED_FF, d), full)],
        out_specs=pl.BlockSpec((tm, d), row),
        scratch_shapes=[pltpu.VMEM((2, SORT_ROWS, d), BF16),
                        pltpu.SemaphoreType.DMA((2,)),
                        pltpu.VMEM((tm, d), F32)])
    return pl.pallas_call(
        _moe_combine_kernel,
        out_shape=jax.ShapeDtypeStruct((n, d), F32),
        grid_spec=grid_spec,
        compiler_params=pltpu.CompilerParams(dimension_semantics=("arbitrary",),
                                             vmem_limit_bytes=VMEM_LIMIT),
        name="moe_combine",
    )(back, used, ys, pos.reshape(nt, TOP_K, tm), wts.reshape(nt, TOP_K, tm), xm, x1, g2, final_g, swg, swu, swd)


def _mix(x, mods, p, attn_fn, s0=None):
    sh1, sc1, g1, sh2, sc2, _ = mods
    b, t, d = x.shape
    xr = x.reshape(b * t // INPROJ_TILE, INPROJ_TILE, d) if sh1.shape[0] == 1 else x
    proj = _inproj(xr, p["norm_attn_g"], sh1, sc1, p["w_gla"], p["w_lora"], p["w_swa"], tm=INPROJ_TILE)
    gla_in, lora, q_s, k_s, v_s = [a.reshape(b, t, a.shape[-1]) for a in proj]
    if s0 is None:
        gla_out, s_f, s_b = _gla(gla_in, lora, p["waf"], p["baf"], p["wab"], p["bab"], p["gla_norm_g"])
    else:
        gla_out, s_f, s_b = _gla(gla_in, lora, p["waf"], p["baf"], p["wab"], p["bab"], p["gla_norm_g"],
                                 s0[0], s0[1])
    att_out = attn_fn(q_s, k_s, v_s)
    routed = _outproj(gla_out, att_out, x, p["w_out"], g1, sh2, sc2, p["norm_ffn_g"],
                      p["rw_cat"], p["rw_hi"], p["rbias"], tm=MOE_TILE)
    return routed, k_s, v_s, s_f, s_b


def _moe(streams, p):
    d = D_MODEL
    (ra, _), (rb, _) = streams
    n_tiles = [r[1].shape[0] * r[1].shape[1] // MOE_TILE for r, _ in streams]
    pos_all = jnp.concatenate([r[2].reshape(-1, TOP_K, MOE_TILE) for r, _ in streams], axis=0)
    cnt_all = jnp.concatenate([r[4][..., 0].reshape(-1, N_EXPERTS) for r, _ in streams], axis=0)
    start_all = jnp.concatenate([r[5][..., 0].reshape(-1, N_EXPERTS) for r, _ in streams], axis=0)
    used = (start_all[:, -1] + cnt_all[:, -1]).astype(jnp.int32)
    xs = _moe_sort(ra[1].reshape(-1, d), rb[1].reshape(-1, d), pos_all, used)
    n_used, first, tiles, src, back = _moe_plan(cnt_all, start_all)
    ys = _moe_experts(n_used, first, tiles, src, xs, p["wg"], p["wu"], p["wd"],
                      _moe_row_tiles(cnt_all.shape[0] * MOE_TILE))
    outs = []
    tile0 = 0
    for ((x1, xm, pos, wts, cnt, start), g2), nt in zip(streams, n_tiles):
        b, t, _ = x1.shape
        tiles_per_mod = (t // MOE_TILE) if g2.shape[0] > 1 else nt
        y = _moe_combine(back[tile0:tile0 + nt].reshape(-1), used[tile0:tile0 + nt], ys, pos, wts,
                         xm.reshape(-1, d), x1.reshape(-1, d), g2, p["final_norm_g"],
                         p["swg"], p["swu"], p["swd"], tiles_per_mod=tiles_per_mod)
        outs.append(y.reshape(b, t, d))
        tile0 += nt
    return outs


def kernel(x_prompt, x_sample, c, cache_swa_k, cache_swa_v, state_gla_fwd, state_gla_bwd, c_ctx, w_ada, b_ada, norm_attn_g, norm_ffn_g, w_in, gla_wa_f, gla_ba_f, gla_wa_b, gla_ba_b, gla_norm_g, swa_sink, w_out, router_w, router_bias, exp_w_gate, exp_w_up, exp_w_down, sh_w_gate, sh_w_up, sh_w_down, final_norm_g):
    l = 0
    d = D_MODEL
    nb_ctx, t_ctx, _ = x_prompt.shape
    nb_lat, t_lat, _ = x_sample.shape

    pad = jnp.zeros((8 - 1 - nb_lat, d), F32)
    cond8 = jnp.concatenate([c_ctx[None, :], c, pad], axis=0)
    mod = _adaln(cond8, w_ada[l], b_ada[l][None, :])
    mods_ctx = [mod[0:1, i * d:(i + 1) * d][:, None, :] for i in range(6)]
    mods_lat = [mod[1:1 + nb_lat, i * d:(i + 1) * d][:, None, :] for i in range(6)]

    zeros_lora = jnp.zeros((GLA_LORA, GLA_QK), F32)
    rw = router_w[l]
    rw_hi = rw.astype(BF16)
    rw_lo = (rw - rw_hi.astype(F32)).astype(BF16)
    w_in_b = w_in[l].astype(BF16)
    p = {
        "norm_attn_g": norm_attn_g[l][None, :],
        "norm_ffn_g": norm_ffn_g[l][None, :],
        "final_norm_g": final_norm_g[None, :],
        "w_gla": w_in_b[:, :2 * GLA_QK + 2 * GLA_V],
        "w_lora": w_in_b[:, 2 * GLA_QK + 2 * GLA_V:2 * GLA_QK + 2 * GLA_V + 2 * GLA_LORA],
        "w_swa": w_in_b[:, 2 * GLA_QK + 2 * GLA_V + 2 * GLA_LORA:],
        "waf": jnp.concatenate([gla_wa_f[l], zeros_lora], axis=0).astype(BF16),
        "wab": jnp.concatenate([zeros_lora, gla_wa_b[l]], axis=0).astype(BF16),
        "baf": gla_ba_f[l][None, :],
        "bab": gla_ba_b[l][None, :],
        "gla_norm_g": gla_norm_g[l][None, :],
        "w_out": w_out[l].astype(BF16),
        "rw_cat": jnp.concatenate([rw_hi, rw_lo], axis=1),
        "rw_hi": rw_hi,
        "rbias": router_bias[l][:, None],
        "wg": exp_w_gate[l], "wu": exp_w_up[l], "wd": exp_w_down[l],
        "swg": sh_w_gate[l].astype(BF16), "swu": sh_w_up[l].astype(BF16),
        "swd": sh_w_down[l].astype(BF16),
    }
    sink = swa_sink[l]

    routed_ctx, k_c, v_c, s_f, s_b = _mix(x_prompt, mods_ctx, p, functools.partial(_attn_ctx, sink))

    cos, sin_lo, sin_hi = _rope_tables(t_lat)
    kc = cache_swa_k[:, l].reshape(nb_lat, -1, SWA_KV)
    vc = cache_swa_v[:, l].reshape(nb_lat, -1, SWA_KV)
    lat_attn = lambda q, k, v: _attn_lat(sink, q, k, v, kc, vc, cos, sin_lo, sin_hi)
    s0 = (state_gla_fwd[:, l].reshape(nb_lat, GLA_QK, GLA_DV),
          state_gla_bwd[:, l].reshape(nb_lat, GLA_QK, GLA_DV))
    routed_lat, _, _, _, _ = _mix(x_sample, mods_lat, p, lat_attn, s0)
    y_prompt, y_sample = _moe([(routed_ctx, mods_ctx[5]), (routed_lat, mods_lat[5])], p)

    new_k = k_c.reshape(nb_ctx, 1, t_ctx, SWA_KV_HEADS, SWA_HEAD_DIM)
    new_v = v_c.reshape(nb_ctx, 1, t_ctx, SWA_KV_HEADS, SWA_HEAD_DIM)
    new_sf = s_f.reshape(nb_ctx, 1, GLA_HEADS, GLA_DK, GLA_DV)
    new_sb = s_b.reshape(nb_ctx, 1, GLA_HEADS, GLA_DK, GLA_DV)
    return (y_prompt, y_sample, new_k, new_v, new_sf, new_sb)
```

```python
import functools

import jax
import jax.numpy as jnp
from jax import lax
from jax.experimental import pallas as pl
from jax.experimental.pallas import tpu as pltpu

F32 = jnp.float32
BF16 = jnp.bfloat16

D_MODEL = 1024
GLA_HEADS = 4
GLA_DK = 64
GLA_DV = 128
GLA_LORA = 16
GLA_GATE_NORM = 16.0
GLA_CHUNK = 64
GLA_QK = GLA_HEADS * GLA_DK
GLA_V = GLA_HEADS * GLA_DV
SWA_HEAD_DIM = 64
SWA_HEADS = 8
SWA_KV_HEADS = 2
SWA_Q = SWA_HEADS * SWA_HEAD_DIM
SWA_KV = SWA_KV_HEADS * SWA_HEAD_DIM
ATTN_BLOCK = 128
GRID_W = 64
ROPE_BASE = 10000.0
N_EXPERTS = 64
TOP_K = 8
N_EXPERT_GROUPS = 8
TOPK_GROUPS = 4
EXPERT_FF = 128
SHARED_FF = 256
ROUTED_SCALE = 2.5
EPS = 1e-6

LANES = 128
VMEM_LIMIT = 56 * 1024 * 1024

NEG_INF = float("-inf")


def _dot(a, b):
    return jnp.dot(a, b, preferred_element_type=F32)


def _dot_nt(a, b):
    return lax.dot_general(a, b, (((1,), (1,)), ((), ())), preferred_element_type=F32)


def _split_hi_lo(x):
    hi = x.astype(BF16)
    lo = (x - hi.astype(F32)).astype(BF16)
    return hi, lo


def _sigmoid(x):
    return 1.0 / (1.0 + jnp.exp(-x))


def _silu(x):
    return x * _sigmoid(x)


def _rms_norm(x, g):
    ms = jnp.mean(x * x, axis=-1, keepdims=True)
    return x * lax.rsqrt(ms + EPS) * g


def _adaln_kernel(c_ref, w_ref, b_ref, o_ref):
    a_hi, a_lo = _split_hi_lo(_silu(c_ref[...]))
    w_hi, w_lo = _split_hi_lo(w_ref[...])
    o_ref[...] = _dot(a_hi, w_hi) + _dot(a_lo, w_hi) + _dot(a_hi, w_lo) + b_ref[...]


def _adaln(cond8, w_ada, b_ada):
    n = w_ada.shape[1]
    tn = 1536
    return pl.pallas_call(
        _adaln_kernel,
        out_shape=jax.ShapeDtypeStruct((8, n), F32),
        grid=(n // tn,),
        in_specs=[pl.BlockSpec((8, D_MODEL), lambda j: (0, 0)),
                  pl.BlockSpec((D_MODEL, tn), lambda j: (0, j)),
                  pl.BlockSpec((1, tn), lambda j: (0, j))],
        out_specs=pl.BlockSpec((8, tn), lambda j: (0, j)),
        compiler_params=pltpu.CompilerParams(dimension_semantics=("arbitrary",),
                                             vmem_limit_bytes=VMEM_LIMIT),
        name="adaln",
    )(cond8, w_ada, b_ada)


def _inproj_kernel(x_ref, g_ref, sh_ref, sc_ref, wg_ref, wl_ref, ws_ref,
                   gla_ref, lora_ref, q_ref, k_ref, v_ref):
    bb, tb, d = x_ref.shape
    x = x_ref[...].reshape(bb * tb, d)
    h = _rms_norm(x, g_ref[...]) * (1.0 + sc_ref[...]) + sh_ref[...]
    hb = h.astype(BF16)
    gla_ref[...] = _dot(hb, wg_ref[...]).reshape(gla_ref.shape)
    lora_ref[...] = _dot(hb, wl_ref[...]).reshape(lora_ref.shape)
    s = _dot(hb, ws_ref[...])
    q_ref[...] = s[:, :SWA_Q].reshape(q_ref.shape)
    k_ref[...] = s[:, SWA_Q:SWA_Q + SWA_KV].reshape(k_ref.shape)
    v_ref[...] = s[:, SWA_Q + SWA_KV:].reshape(v_ref.shape)


INPROJ_TILE = 512


def _inproj(x, g, sh, sc, w_gla, w_lora, w_swa):
    b, t, d = x.shape
    nmod = sh.shape[0]
    tb = min(t, INPROJ_TILE)
    bb = INPROJ_TILE // tb if nmod == 1 else 1
    mod_map = (lambda i, j: (i, 0, 0)) if nmod > 1 else (lambda i, j: (0, 0, 0))
    row = lambda i, j: (i, j, 0)
    full = lambda i, j: (0, 0)
    n_gla = w_gla.shape[1]
    n_lora = w_lora.shape[1]
    return pl.pallas_call(
        _inproj_kernel,
        out_shape=(jax.ShapeDtypeStruct((b, t, n_gla), F32),
                   jax.ShapeDtypeStruct((b, t, n_lora), F32),
                   jax.ShapeDtypeStruct((b, t, SWA_Q), F32),
                   jax.ShapeDtypeStruct((b, t, SWA_KV), F32),
                   jax.ShapeDtypeStruct((b, t, SWA_KV), F32)),
        grid=(b // bb, t // tb),
        in_specs=[pl.BlockSpec((bb, tb, d), row),
                  pl.BlockSpec((1, d), full),
                  pl.BlockSpec((None, 1, d), mod_map),
                  pl.BlockSpec((None, 1, d), mod_map),
                  pl.BlockSpec((d, n_gla), full),
                  pl.BlockSpec((d, n_lora), full),
                  pl.BlockSpec((d, w_swa.shape[1]), full)],
        out_specs=(pl.BlockSpec((bb, tb, n_gla), row),
                   pl.BlockSpec((bb, tb, n_lora), row),
                   pl.BlockSpec((bb, tb, SWA_Q), row),
                   pl.BlockSpec((bb, tb, SWA_KV), row),
                   pl.BlockSpec((bb, tb, SWA_KV), row)),
        compiler_params=pltpu.CompilerParams(dimension_semantics=("arbitrary", "arbitrary"),
                                             vmem_limit_bytes=VMEM_LIMIT),
        name="inproj",
    )(x, g, sh, sc, w_gla, w_lora, w_swa)


SCAN_UNROLL = 2
OUT_UNROLL = 4


def _log_sigmoid(x):
    return jnp.minimum(x, 0.0) - jnp.log(1.0 + jnp.exp(-jnp.abs(x)))


def _heads_to_rows(x):
    return jnp.concatenate([x[:, h * LANES:(h + 1) * LANES] for h in range(GLA_HEADS)], axis=0)


def _rows_to_heads(x, c):
    return jnp.concatenate([x[h * c:(h + 1) * c, :] for h in range(GLA_HEADS)], axis=1)


def _gla_kernel(has_init, q_ref, k_ref, v_ref, g_ref, lora_ref, waf_ref, baf_ref, wab_ref, bab_ref,
                ng_ref, *rest):
    if has_init:
        s0f_ref, s0b_ref, *rest = rest
    (out_ref, sf_ref, sb_ref, laf_ref, lab_ref, oacc_ref, qtf_ref, qtb_ref, saf_ref, sab_ref,
     stf_ref, stb_ref) = rest
    t = q_ref.shape[0]
    c = GLA_CHUNK
    n = t // c
    hc = GLA_HEADS * c

    lora = lora_ref[...].astype(BF16)
    laf_ref[...] = _log_sigmoid(_dot(lora, waf_ref[...]) + baf_ref[...]) * (1.0 / GLA_GATE_NORM)
    lab_ref[...] = _log_sigmoid(_dot(lora, wab_ref[...]) + bab_ref[...]) * (1.0 / GLA_GATE_NORM)

    if has_init:
        stf_ref[...] = s0f_ref[...].T
        stb_ref[...] = s0b_ref[...].T
    else:
        stf_ref[...] = jnp.zeros_like(stf_ref)
        stb_ref[...] = jnp.zeros_like(stb_ref)
    oacc_ref[...] = jnp.zeros_like(oacc_ref)

    r64 = lax.broadcasted_iota(jnp.int32, (c, c), 0)
    c64 = lax.broadcasted_iota(jnp.int32, (c, c), 1)
    tri_f = jnp.where(c64 <= r64, 1.0, 0.0).astype(BF16)
    tri_b = jnp.where(c64 >= r64, 1.0, 0.0).astype(BF16)
    rr = lax.broadcasted_iota(jnp.int32, (hc, hc), 0)
    cc = lax.broadcasted_iota(jnp.int32, (hc, hc), 1)
    same_head = (rr >> 6) == (cc >> 6)
    keep_f = same_head & ((rr & (c - 1)) >= (cc & (c - 1)))
    keep_b = same_head & ((rr & (c - 1)) <= (cc & (c - 1)))
    head_mask = jnp.where(same_head, 1.0, 0.0).astype(BF16)
    norm_g = ng_ref[...]

    def chunk_rows(ci):
        return pl.ds(pl.multiple_of(ci * c, c), c)

    def tile_heads(x):
        x4 = jnp.concatenate([x] * GLA_HEADS, axis=0)
        return jnp.where(same_head, x4, 0.0).astype(BF16)

    def scan_step(i, carry):
        dirs = []
        for u in range(SCAN_UNROLL):
            dirs += [(SCAN_UNROLL * i + u, laf_ref, tri_f, keep_f, c - 1, stf_ref, saf_ref, qtf_ref),
                     (n - 1 - SCAN_UNROLL * i - u, lab_ref, tri_b, keep_b, 0, stb_ref, sab_ref, qtb_ref)]
        cums = []
        for ci, la_ref, tri, _, _, _, _, _ in dirs:
            la_hi, la_lo = _split_hi_lo(la_ref[chunk_rows(ci), :])
            cums.append(_dot(tri, la_hi) + _dot(tri, la_lo))
        ops = []
        for (ci, _, _, _, last_row, _, _, qt_ref), cum in zip(dirs, cums):
            sl = chunk_rows(ci)
            tot = cum[last_row:last_row + 1, :]
            kc = k_ref[sl, :]
            qt = q_ref[sl, :] * (GLA_DK ** -0.5) * jnp.exp(cum)
            qt_ref[sl, :] = qt.astype(BF16)
            v_rows = _heads_to_rows(v_ref[sl, :])
            ops.append((tot, tile_heads(qt), tile_heads(kc * jnp.exp(-cum)),
                        tile_heads(kc * jnp.exp(tot - cum)), v_rows))
        atts = [_dot_nt(q4, k4) for _, q4, k4, _, _ in ops]
        incs = []
        for (_, _, _, keep, _, _, _, _), (_, _, _, kd4, v_rows), att in zip(dirs, ops, atts):
            att = jnp.where(keep, att, 0.0).astype(BF16)
            incs.append((_dot(att, v_rows.astype(BF16)), _dot(v_rows.T.astype(BF16), kd4)))
        for (ci, _, _, _, _, st_ref, snap_ref, _), (tot, _, _, _, _), (o_intra, st_inc) in zip(dirs, ops, incs):
            oacc_ref[ci] += o_intra
            st = st_ref[...]
            snap_ref[ci] = st.astype(BF16)
            st_ref[...] = jnp.exp(tot) * st + st_inc
        return carry

    def tile_heads_bf16(x):
        return jnp.concatenate([x] * GLA_HEADS, axis=0) * head_mask

    def out_step(i, carry):
        chunks = [OUT_UNROLL * i + u for u in range(OUT_UNROLL)]
        inter = []
        for ci in chunks:
            sl = chunk_rows(ci)
            q4 = jnp.concatenate([tile_heads_bf16(qtf_ref[sl, :]), tile_heads_bf16(qtb_ref[sl, :])], axis=1)
            st = jnp.concatenate([saf_ref[ci], sab_ref[ci]], axis=1)
            inter.append(_dot_nt(q4, st))
        for ci, o_inter in zip(chunks, inter):
            sl = chunk_rows(ci)
            on = _rms_norm(oacc_ref[ci] + o_inter, norm_g)
            gate = _silu(_heads_to_rows(g_ref[sl, :]))
            out_ref[sl, :] = _rows_to_heads(on * gate, c)
        return carry

    lax.fori_loop(0, n // SCAN_UNROLL, scan_step, 0)
    lax.fori_loop(0, n // OUT_UNROLL, out_step, 0)
    sf_ref[...] = stf_ref[...].T
    sb_ref[...] = stb_ref[...].T


def _gla(gla_in, lora, waf, baf, wab, bab, norm_g, s0f=None, s0b=None):
    b, t, _ = gla_in.shape
    has_init = s0f is not None
    n = t // GLA_CHUNK
    bmap = lambda i: (i, 0, 0)
    full = lambda i: (0, 0)
    in_specs = [pl.BlockSpec((None, t, GLA_QK), lambda i: (i, 0, 0)),
                pl.BlockSpec((None, t, GLA_QK), lambda i: (i, 0, 1)),
                pl.BlockSpec((None, t, GLA_V), lambda i: (i, 0, 1)),
                pl.BlockSpec((None, t, GLA_V), lambda i: (i, 0, 2)),
                pl.BlockSpec((None, t, 2 * GLA_LORA), bmap),
                pl.BlockSpec((2 * GLA_LORA, GLA_QK), full),
                pl.BlockSpec((1, GLA_QK), full),
                pl.BlockSpec((2 * GLA_LORA, GLA_QK), full),
                pl.BlockSpec((1, GLA_QK), full),
                pl.BlockSpec((1, GLA_DV), full)]
    args = [gla_in, gla_in, gla_in, gla_in, lora, waf, baf, wab, bab, norm_g]
    if has_init:
        in_specs += [pl.BlockSpec((None, GLA_QK, GLA_DV), bmap)] * 2
        args += [s0f, s0b]
    return pl.pallas_call(
        functools.partial(_gla_kernel, has_init),
        out_shape=(jax.ShapeDtypeStruct((b, t, GLA_V), F32),
                   jax.ShapeDtypeStruct((b, GLA_QK, GLA_DV), F32),
                   jax.ShapeDtypeStruct((b, GLA_QK, GLA_DV), F32)),
        grid=(b,),
        in_specs=in_specs,
        out_specs=(pl.BlockSpec((None, t, GLA_V), bmap),
                   pl.BlockSpec((None, GLA_QK, GLA_DV), bmap),
                   pl.BlockSpec((None, GLA_QK, GLA_DV), bmap)),
        scratch_shapes=[pltpu.VMEM((t, GLA_QK), F32),
                        pltpu.VMEM((t, GLA_QK), F32),
                        pltpu.VMEM((n, GLA_HEADS * GLA_CHUNK, GLA_DV), F32),
                        pltpu.VMEM((t, GLA_QK), BF16),
                        pltpu.VMEM((t, GLA_QK), BF16),
                        pltpu.VMEM((n, GLA_DV, GLA_QK), BF16),
                        pltpu.VMEM((n, GLA_DV, GLA_QK), BF16),
                        pltpu.VMEM((GLA_DV, GLA_QK), F32),
                        pltpu.VMEM((GLA_DV, GLA_QK), F32)],
        compiler_params=pltpu.CompilerParams(dimension_semantics=("arbitrary",),
                                             vmem_limit_bytes=VMEM_LIMIT),
        name="gla",
    )(*args)


def _dup_groups(x):
    lo = lax.broadcasted_iota(jnp.int32, x.shape, 1) < SWA_HEAD_DIM
    xr = pltpu.roll(x, SWA_HEAD_DIM, axis=1)
    return jnp.where(lo, x, xr), jnp.where(lo, xr, x)


def _pairs_attention(qps, sinks, k_dups, vt_dups, mask):
    nq = qps[0].shape[0]
    lo = lax.broadcasted_iota(jnp.int32, (nq, LANES), 1) < SWA_HEAD_DIM
    even = lax.broadcasted_iota(jnp.int32, (1, 2 * nq), 1) < nq
    scores = []
    for qp, k_dup in zip(qps, k_dups):
        q2 = jnp.concatenate([jnp.where(lo, qp, 0.0), jnp.where(lo, 0.0, qp)], axis=0).astype(BF16)
        scores.append(_dot_nt(k_dup, q2))
    probs = []
    for s, (sink_even, sink_odd) in zip(scores, sinks):
        if mask is not None:
            s = jnp.where(mask, s, NEG_INF)
        sink = jnp.where(even, sink_even, sink_odd)
        m = jnp.maximum(jnp.max(s, axis=0, keepdims=True), sink)
        p = jnp.exp(s - m)
        denom = jnp.sum(p, axis=0, keepdims=True) + jnp.exp(sink - m)
        probs.append((p.astype(BF16), 1.0 / denom))
    outs = []
    for (p, rdenom), vt_dup in zip(probs, vt_dups):
        o = _dot(vt_dup, p) * rdenom
        outs.append(jnp.concatenate([o[:SWA_HEAD_DIM, :nq], o[SWA_HEAD_DIM:, nq:]], axis=0).T)
    return outs


CTX_BATCH = 2


def _attn_ctx_kernel(sink_ref, q_ref, k_ref, v_ref, o_ref):
    scale = SWA_HEAD_DIM ** -0.5
    pairs = range(SWA_HEADS // 2)
    items = [(bb, pr) for bb in range(q_ref.shape[0]) for pr in pairs]
    kd = [[x.astype(BF16) for x in _dup_groups(k_ref[bb])] for bb in range(q_ref.shape[0])]
    vt = [[x.T.astype(BF16) for x in _dup_groups(v_ref[bb])] for bb in range(q_ref.shape[0])]
    outs = _pairs_attention([q_ref[bb, :, pr * LANES:(pr + 1) * LANES] * scale for bb, pr in items],
                            [(sink_ref[2 * pr], sink_ref[2 * pr + 1]) for _, pr in items],
                            [kd[bb][pr // 2] for bb, pr in items], [vt[bb][pr // 2] for bb, pr in items], None)
    for (bb, pr), out in zip(items, outs):
        o_ref[bb, :, pr * LANES:(pr + 1) * LANES] = out


def _attn_ctx(sink, q, k, v):
    b, t, _ = q.shape
    bmap = lambda i: (i, 0, 0)
    return pl.pallas_call(
        _attn_ctx_kernel,
        out_shape=jax.ShapeDtypeStruct((b, t, SWA_Q), F32),
        grid=(b // CTX_BATCH,),
        in_specs=[pl.BlockSpec(memory_space=pltpu.SMEM),
                  pl.BlockSpec((CTX_BATCH, t, SWA_Q), bmap),
                  pl.BlockSpec((CTX_BATCH, t, SWA_KV), bmap),
                  pl.BlockSpec((CTX_BATCH, t, SWA_KV), bmap)],
        out_specs=pl.BlockSpec((CTX_BATCH, t, SWA_Q), bmap),
        compiler_params=pltpu.CompilerParams(dimension_semantics=("arbitrary",),
                                             vmem_limit_bytes=VMEM_LIMIT),
        name="attn_ctx",
    )(sink, q, k, v)


def _rope(x, cos, sin_lo, sin_hi):
    return x * cos + pltpu.roll(x, LANES - 16, axis=1) * sin_lo + pltpu.roll(x, 16, axis=1) * sin_hi


def _attn_lat_kernel(sink_ref, q_ref, k_ref, v_ref, kc_ref, vc_ref, cos_ref, sl_ref, sh_ref,
                     o_ref, kw_ref, vw_ref):
    t = q_ref.shape[0]
    ab = ATTN_BLOCK
    nb = t // ab
    scale = SWA_HEAD_DIM ** -0.5

    k_rot = _dup_groups(_rope(k_ref[...], cos_ref[...], sl_ref[...], sh_ref[...]))
    v_dup = _dup_groups(v_ref[...])
    zeros = jnp.zeros((ab, LANES), BF16)
    for grp in range(SWA_KV_HEADS):
        kw_ref[grp, 0:ab, :] = zeros
        kw_ref[grp, ab:ab + t, :] = k_rot[grp].astype(BF16)
        kw_ref[grp, ab + t:, :] = zeros
        vw_ref[grp, 0] = zeros
        for blk in range(nb):
            vw_ref[grp, blk + 1] = v_dup[grp][blk * ab:(blk + 1) * ab, :].T.astype(BF16)
        vw_ref[grp, nb + 1] = zeros
    kc = [x.astype(BF16) for x in _dup_groups(kc_ref[...])]
    vct = [x.T.astype(BF16) for x in _dup_groups(vc_ref[...])]
    lc = kc_ref.shape[0]

    key = lax.broadcasted_iota(jnp.int32, (lc + 3 * ab, 2 * ab), 0) - lc
    tq = lax.broadcasted_iota(jnp.int32, (lc + 3 * ab, 2 * ab), 1) & (ab - 1)
    band = (key < 0) | (jnp.abs(tq + ab - key) <= ab)

    def block(nq, carry):
        row0 = pl.multiple_of(nq * ab, ab)
        s_abs = key + (nq - 1) * ab
        mask = band & ((key < 0) | ((s_abs >= 0) & (s_abs < t)))
        cos = cos_ref[pl.ds(row0, ab), :]
        s_lo = sl_ref[pl.ds(row0, ab), :]
        s_hi = sh_ref[pl.ds(row0, ab), :]
        k_all = [jnp.concatenate([kc[grp], kw_ref[grp, pl.ds(row0, 3 * ab), :]], axis=0)
                 for grp in range(SWA_KV_HEADS)]
        vt_all = [jnp.concatenate([vct[grp], vw_ref[grp, nq], vw_ref[grp, nq + 1], vw_ref[grp, nq + 2]],
                                  axis=1) for grp in range(SWA_KV_HEADS)]
        pairs = range(SWA_HEADS // 2)
        qps = [_rope(q_ref[pl.ds(row0, ab), pr * LANES:(pr + 1) * LANES], cos, s_lo, s_hi) * scale
               for pr in pairs]
        outs = _pairs_attention(qps, [(sink_ref[2 * pr], sink_ref[2 * pr + 1]) for pr in pairs],
                                [k_all[pr // 2] for pr in pairs], [vt_all[pr // 2] for pr in pairs], mask)
        for pr in pairs:
            o_ref[pl.ds(row0, ab), pr * LANES:(pr + 1) * LANES] = outs[pr]
        return carry

    lax.fori_loop(0, nb, block, 0)


def _attn_lat(sink, q, k, v, kc, vc, cos, sin_lo, sin_hi):
    b, t, _ = q.shape
    lc = kc.shape[1]
    bmap = lambda i: (i, 0, 0)
    full = lambda i: (0, 0)
    return pl.pallas_call(
        _attn_lat_kernel,
        out_shape=jax.ShapeDtypeStruct((b, t, SWA_Q), F32),
        grid=(b,),
        in_specs=[pl.BlockSpec(memory_space=pltpu.SMEM),
                  pl.BlockSpec((None, t, SWA_Q), bmap),
                  pl.BlockSpec((None, t, SWA_KV), bmap),
                  pl.BlockSpec((None, t, SWA_KV), bmap),
                  pl.BlockSpec((None, lc, SWA_KV), bmap),
                  pl.BlockSpec((None, lc, SWA_KV), bmap),
                  pl.BlockSpec((t, LANES), full),
                  pl.BlockSpec((t, LANES), full),
                  pl.BlockSpec((t, LANES), full)],
        out_specs=pl.BlockSpec((None, t, SWA_Q), bmap),
        scratch_shapes=[pltpu.VMEM((SWA_KV_HEADS, t + 2 * ATTN_BLOCK, LANES), BF16),
                        pltpu.VMEM((SWA_KV_HEADS, t // ATTN_BLOCK + 2, LANES, ATTN_BLOCK), BF16)],
        compiler_params=pltpu.CompilerParams(dimension_semantics=("arbitrary",),
                                             vmem_limit_bytes=VMEM_LIMIT),
        name="attn_lat",
    )(sink, q, k, v, kc, vc, cos, sin_lo, sin_hi)


def _rope_tables(t):
    half = SWA_HEAD_DIM // 2
    quarter = half // 2
    pos = jnp.arange(t)
    row = (pos // GRID_W).astype(F32)
    col = (pos % GRID_W).astype(F32)
    inv_freq = ROPE_BASE ** (-jnp.arange(quarter, dtype=F32) / quarter)
    lane = jnp.arange(LANES)
    d = lane % SWA_HEAD_DIM
    freq = inv_freq[d % quarter]
    use_row = (d < half)
    ang = jnp.where(use_row[None, :], row[:, None], col[:, None]) * freq[None, :]
    cos = jnp.cos(ang)
    sin = jnp.sin(ang)
    lower = (d % half) < quarter
    return cos, jnp.where(lower[None, :], -sin, 0.0), jnp.where(lower[None, :], 0.0, sin)


def _route(sel, scores):
    n = sel.shape[1]
    gsz = N_EXPERTS // N_EXPERT_GROUPS

    def first_max(x, idx, size):
        m = jnp.max(x, axis=0, keepdims=True)
        first = jnp.min(jnp.where(x == m, idx, float(size)), axis=0, keepdims=True)
        return m, idx == first

    i8 = lax.broadcasted_iota(jnp.int32, (gsz, n), 0).astype(F32)
    rows = []
    for g in range(N_EXPERT_GROUPS):
        slab = sel[g * gsz:(g + 1) * gsz, :]
        m1, hit = first_max(slab, i8, gsz)
        m2 = jnp.max(jnp.where(hit, NEG_INF, slab), axis=0, keepdims=True)
        rows.append(m1 + m2)
    gscore = jnp.concatenate(rows, axis=0)
    gsel = jnp.zeros((N_EXPERT_GROUPS, n), F32)
    for _ in range(TOPK_GROUPS):
        _, hit = first_max(gscore, i8, N_EXPERT_GROUPS)
        gsel = jnp.where(hit, 1.0, gsel)
        gscore = jnp.where(hit, NEG_INF, gscore)
    emask = jnp.concatenate(
        [jnp.broadcast_to(gsel[g:g + 1, :], (gsz, n)) for g in range(N_EXPERT_GROUPS)], axis=0)
    cand = jnp.where(emask > 0.5, sel, NEG_INF)
    ie = lax.broadcasted_iota(jnp.int32, (N_EXPERTS, n), 0).astype(F32)
    w = jnp.zeros((N_EXPERTS, n), F32)
    chosen = jnp.zeros((N_EXPERTS, n), F32)
    hits = []
    for _ in range(TOP_K):
        _, hit = first_max(cand, ie, N_EXPERTS)
        hits.append(hit)
        w = jnp.where(hit, scores, w)
        chosen = jnp.where(hit, 1.0, chosen)
        cand = jnp.where(hit, NEG_INF, cand)
    gates = w / jnp.sum(w, axis=0, keepdims=True) * ROUTED_SCALE

    s_idx = lax.broadcasted_iota(jnp.int32, (n, n), 0)
    t_idx = lax.broadcasted_iota(jnp.int32, (n, n), 1)
    before = jnp.where(s_idx < t_idx, 1.0, 0.0).astype(BF16)
    rank = _dot(chosen.astype(BF16), before)
    count = jnp.sum(chosen, axis=1, keepdims=True)
    padded = jnp.floor((count + (SORT_ALIGN - 1)) * (1.0 / SORT_ALIGN)) * SORT_ALIGN
    padded = jnp.broadcast_to(padded, (N_EXPERTS, LANES))
    e_row = lax.broadcasted_iota(jnp.int32, (N_EXPERTS, N_EXPERTS), 0)
    e_col = lax.broadcasted_iota(jnp.int32, (N_EXPERTS, N_EXPERTS), 1)
    below = jnp.where(e_col < e_row, 1.0, 0.0).astype(BF16)
    start = _dot(below, padded.astype(BF16))
    row = start[:, 0:1] + rank
    pos = jnp.concatenate([jnp.sum(jnp.where(h, row, 0.0), axis=0, keepdims=True) for h in hits], axis=0)
    wts = jnp.concatenate([jnp.sum(jnp.where(h, gates, 0.0), axis=0, keepdims=True) for h in hits], axis=0)
    return pos, wts, padded, start


def _outproj_kernel(gla_ref, att_ref, x_ref, wo_ref, g1_ref, sh_ref, sc_ref, ng_ref, rw_ref, rwh_ref,
                    rb_ref, x1_ref, xm_ref, pos_ref, wts_ref, cnt_ref, start_ref):
    y = (_dot(gla_ref[...].astype(BF16), wo_ref[0:GLA_V, :])
         + _dot(att_ref[...].astype(BF16), wo_ref[GLA_V:, :]))
    x1 = x_ref[...] + g1_ref[...] * y
    x1_ref[...] = x1
    xm = _rms_norm(x1, ng_ref[...]) * (1.0 + sc_ref[...]) + sh_ref[...]
    xm_hi, xm_lo = _split_hi_lo(xm)
    xm_ref[...] = xm_hi
    lg = _dot(xm_hi, rw_ref[...])
    logits = lg[:, :N_EXPERTS] + lg[:, N_EXPERTS:] + _dot(xm_lo, rwh_ref[...])
    tm = logits.shape[0]
    lt = jnp.concatenate([logits, jnp.zeros((tm, LANES - N_EXPERTS), F32)], axis=1).T[:N_EXPERTS, :]
    scores = _sigmoid(lt)
    pos_ref[...], wts_ref[...], cnt_ref[...], start_ref[...] = _route(scores + rb_ref[...], scores)


def _outproj(gla_out, att_out, x, w_out, g1, sh2, sc2, norm_g, rw_cat, rw_hi, rbias, *, tm):
    b, t, d = x.shape
    nmod = g1.shape[0]
    mod_map = (lambda i, j: (i, 0, 0)) if nmod > 1 else (lambda i, j: (0, 0, 0))
    row = lambda i, j: (i, j, 0)
    full = lambda i, j: (0, 0)
    tile = lambda i, j: (i, j, 0, 0)
    nt = t // tm
    return pl.pallas_call(
        _outproj_kernel,
        out_shape=(jax.ShapeDtypeStruct((b, t, d), F32),
                   jax.ShapeDtypeStruct((b, t, d), BF16),
                   jax.ShapeDtypeStruct((b, nt, TOP_K, tm), F32),
                   jax.ShapeDtypeStruct((b, nt, TOP_K, tm), F32),
                   jax.ShapeDtypeStruct((b, nt, N_EXPERTS, LANES), F32),
                   jax.ShapeDtypeStruct((b, nt, N_EXPERTS, LANES), F32)),
        grid=(b, t // tm),
        in_specs=[pl.BlockSpec((None, tm, GLA_V), row),
                  pl.BlockSpec((None, tm, SWA_Q), row),
                  pl.BlockSpec((None, tm, d), row),
                  pl.BlockSpec((d, d), full),
                  pl.BlockSpec((None, 1, d), mod_map),
                  pl.BlockSpec((None, 1, d), mod_map),
                  pl.BlockSpec((None, 1, d), mod_map),
                  pl.BlockSpec((1, d), full),
                  pl.BlockSpec((d, 2 * N_EXPERTS), full),
                  pl.BlockSpec((d, N_EXPERTS), full),
                  pl.BlockSpec((N_EXPERTS, 1), full)],
        out_specs=(pl.BlockSpec((None, tm, d), row),
                   pl.BlockSpec((None, tm, d), row),
                   pl.BlockSpec((None, None, TOP_K, tm), tile),
                   pl.BlockSpec((None, None, TOP_K, tm), tile),
                   pl.BlockSpec((None, None, N_EXPERTS, LANES), tile),
                   pl.BlockSpec((None, None, N_EXPERTS, LANES), tile)),
        compiler_params=pltpu.CompilerParams(dimension_semantics=("arbitrary", "arbitrary"),
                                             vmem_limit_bytes=VMEM_LIMIT),
        name="outproj",
    )(gla_out, att_out, x, w_out, g1, sh2, sc2, norm_g, rw_cat, rw_hi, rbias)


MOE_TILE = 256
SORT_ALIGN = 16
SORT_ROWS = 3072
ROW_TILE = 512
GATHER_SLOTS = 3
FFN_CHAINS = 4
COMBINE_CHUNK = 1024
ALWAYS_ROWS = 2560
COMBINE_TAIL = 512


def _moe_sort_kernel(tiles_a, used_ref, xa_ref, xb_ref, pos_ref, xs_ref):
    i = pl.program_id(0)
    x = jnp.where(i < tiles_a, xa_ref[...], xb_ref[...])
    pos = pos_ref[...]
    tm = x.shape[0]
    used = used_ref[i]

    def fill(blk):
        rows = (lax.broadcasted_iota(jnp.int32, (tm, tm), 0) + blk * tm).astype(F32)
        onehot = jnp.zeros((tm, tm), F32)
        for k in range(TOP_K):
            onehot = jnp.where(rows == pos[k:k + 1, :], 1.0, onehot)
        xs_ref[blk * tm:(blk + 1) * tm, :] = _dot(onehot.astype(BF16), x).astype(BF16)

    for blk in range(SORT_ROWS // tm):
        if (blk + 1) * tm <= ALWAYS_ROWS:
            fill(blk)
        else:
            pl.when(blk * tm < used)(functools.partial(fill, blk))

            @pl.when(blk * tm >= used)
            def _():
                xs_ref[blk * tm:(blk + 1) * tm, :] = jnp.zeros((tm, D_MODEL), BF16)


def _moe_sort(xm_a, xm_b, pos, used):
    d = xm_a.shape[1]
    nt, _, tm = pos.shape
    tiles_a = xm_a.shape[0] // tm
    grid_spec = pltpu.PrefetchScalarGridSpec(
        num_scalar_prefetch=1,
        grid=(nt,),
        in_specs=[pl.BlockSpec((tm, d), lambda i, u: (jnp.minimum(i, tiles_a - 1), 0)),
                  pl.BlockSpec((tm, d), lambda i, u: (jnp.maximum(i - tiles_a, 0), 0)),
                  pl.BlockSpec((None, TOP_K, tm), lambda i, u: (i, 0, 0))],
        out_specs=pl.BlockSpec((SORT_ROWS, d), lambda i, u: (i, 0)))
    return pl.pallas_call(
        functools.partial(_moe_sort_kernel, tiles_a),
        out_shape=jax.ShapeDtypeStruct((nt * SORT_ROWS, d), BF16),
        grid_spec=grid_spec,
        compiler_params=pltpu.CompilerParams(dimension_semantics=("arbitrary",),
                                             vmem_limit_bytes=VMEM_LIMIT),
        name="moe_sort",
    )(used, xm_a, xm_b, pos)


def _moe_row_tiles(n_tokens):
    rows = n_tokens * TOP_K + (n_tokens // MOE_TILE) * N_EXPERTS * (SORT_ALIGN - 1) + N_EXPERTS * (ROW_TILE - 1)
    return -(-rows // ROW_TILE) + GATHER_SLOTS - 1


PLAN_CHUNK = 1280


def _int_dot_r(a, onehot):
    hi = jnp.floor(a * (1.0 / 256.0))
    return _dot(hi.astype(BF16), onehot) * 256.0 + _dot((a - hi * 256.0).astype(BF16), onehot)


def _int_dot_l(onehot, b):
    hi = jnp.floor(b * (1.0 / 256.0))
    return _dot(onehot, hi.astype(BF16)) * 256.0 + _dot(onehot, (b - hi * 256.0).astype(BF16))


def _moe_plan_kernel(cnt_ref, start_ref, src_ref, first_ref, tiles_ref, nu_ref, back_ref):
    nt, ne = cnt_ref.shape
    gpt = SORT_ROWS // SORT_ALIGN
    gpr = ROW_TILE // SORT_ALIGN
    gc = cnt_ref[...] * (1.0 / SORT_ALIGN)
    ls = start_ref[...] * (1.0 / SORT_ALIGN)

    def transpose(x):
        x = jnp.concatenate([x, jnp.zeros((nt, LANES - ne), F32)], axis=1)
        x = jnp.concatenate([x, jnp.zeros((LANES - nt, LANES), F32)], axis=0)
        return x.T[:ne, :nt]

    def tri(n, keep):
        return jnp.where(keep(lax.broadcasted_iota(jnp.int32, (n, n), 0),
                              lax.broadcasted_iota(jnp.int32, (n, n), 1)), 1.0, 0.0).astype(BF16)

    gc_t = transpose(gc)
    ls_t = transpose(ls)
    tot_c = jnp.broadcast_to(jnp.sum(gc_t, axis=1, keepdims=True), (ne, LANES))
    ptot_c = jnp.floor((tot_c + (gpr - 1)) * (1.0 / gpr)) * gpr
    gend_c = _int_dot_l(tri(ne, lambda r, c: c <= r), ptot_c)
    gstart_c = gend_c - ptot_c
    n_used = gend_c[ne - 1:ne, :] * (1.0 / gpr)
    nu_ref[...] = n_used.astype(jnp.int32)
    tot_r = jnp.sum(gc, axis=0, keepdims=True)
    ptot_r = jnp.floor((tot_r + (gpr - 1)) * (1.0 / gpr)) * gpr
    gstart_r = _int_dot_r(jnp.broadcast_to(ptot_r, (8, ne)), tri(ne, lambda r, c: r < c))
    cumex = _dot(tri(nt, lambda r, c: c < r), gc.astype(BF16))
    cumex_t = _dot(gc_t.astype(BF16), tri(nt, lambda r, c: r < c))
    tile_base = lax.broadcasted_iota(jnp.int32, (nt, ne), 0).astype(F32) * gpt + ls
    table = jnp.concatenate([cumex + gc, cumex, tile_base, gstart_r, jnp.broadcast_to(tot_r, (8, ne))], axis=0)

    e_iota = lax.broadcasted_iota(jnp.int32, (ne, PLAN_CHUNK), 0).astype(F32)
    for ch in range(src_ref.shape[1] // PLAN_CHUNK):
        g = (lax.broadcasted_iota(jnp.int32, (1, PLAN_CHUNK), 1) + ch * PLAN_CHUNK).astype(F32)
        eg = jnp.sum(jnp.where(gend_c[:, 0:1] <= g, 1.0, 0.0), axis=0, keepdims=True)
        picked = _int_dot_r(table, jnp.where(e_iota == eg, 1.0, 0.0).astype(BF16))
        cum_g, cumex_g, base_g = picked[0:nt], picked[nt:2 * nt], picked[2 * nt:3 * nt]
        u = g - picked[3 * nt:3 * nt + 1]
        in_tile = (cumex_g <= u) & (u < cum_g)
        src = jnp.sum(jnp.where(in_tile, base_g - cumex_g, 0.0), axis=0, keepdims=True) + u
        src = jnp.where(u < picked[3 * nt + 8:3 * nt + 9], src, gpt - 1.0)
        src_ref[:, ch * PLAN_CHUNK:(ch + 1) * PLAN_CHUNK] = src.astype(jnp.int32)

    first_ref[...] = (gstart_c * (1.0 / gpr)).astype(jnp.int32)
    tiles_ref[...] = (ptot_c * (1.0 / gpr)).astype(jnp.int32)

    lg = lax.broadcasted_iota(jnp.int32, (ne, back_ref.shape[1]), 1).astype(F32)
    for t in range(nt):
        first = ls_t[:, t:t + 1]
        inside = (first <= lg) & (lg < first + gc_t[:, t:t + 1])
        shift = gstart_c[:, 0:1] + cumex_t[:, t:t + 1] - first
        val = jnp.sum(jnp.where(inside, shift + lg, 0.0), axis=0, keepdims=True)
        back_ref[t:t + 1, :] = val.astype(jnp.int32)


def _moe_plan(cnt, start):
    nt, ne = cnt.shape
    row_tiles = _moe_row_tiles(nt * MOE_TILE)
    gpt = SORT_ROWS // SORT_ALIGN
    gpr = ROW_TILE // SORT_ALIGN
    n_src = -(-(row_tiles * gpr) // PLAN_CHUNK) * PLAN_CHUNK
    n_back = -(-gpt // LANES) * LANES
    src, first, tiles, nu, back = pl.pallas_call(
        _moe_plan_kernel,
        out_shape=(jax.ShapeDtypeStruct((1, n_src), jnp.int32),
                   jax.ShapeDtypeStruct((ne, LANES), jnp.int32),
                   jax.ShapeDtypeStruct((ne, LANES), jnp.int32),
                   jax.ShapeDtypeStruct((1, LANES), jnp.int32),
                   jax.ShapeDtypeStruct((nt, n_back), jnp.int32)),
        compiler_params=pltpu.CompilerParams(vmem_limit_bytes=VMEM_LIMIT),
        name="moe_plan",
    )(cnt, start)
    return nu[0, :1], first[:, 0], tiles[:, 0], src[0, :row_tiles * gpr], back[:, :gpt]


def _moe_experts_kernel(nu_ref, first_ref, tiles_ref, src_ref, xs_hbm, wg_ref, wu_ref, wd_ref, ys_hbm,
                        xbuf, ybuf, gsem, osem, wgu_s, wd_s):
    e = pl.program_id(0)
    n_used = nu_ref[0]
    gpr = ROW_TILE // SORT_ALIGN
    part = ROW_TILE // FFN_CHAINS

    def gather(tile, to_slot, j0=0, j1=gpr):
        for j in range(j0, j1):
            row = pl.multiple_of(src_ref[tile * gpr + j] * SORT_ALIGN, SORT_ALIGN)
            pltpu.make_async_copy(xs_hbm.at[pl.ds(row, SORT_ALIGN), :],
                                  xbuf.at[to_slot, j * SORT_ALIGN:(j + 1) * SORT_ALIGN, :],
                                  gsem.at[to_slot]).start(priority=j % 2)

    def drain(of_slot):
        for j in range(gpr):
            pltpu.make_async_copy(xs_hbm.at[0:SORT_ALIGN, :],
                                  xbuf.at[of_slot, j * SORT_ALIGN:(j + 1) * SORT_ALIGN, :], gsem.at[of_slot]).wait()

    def out_copy(tile, of_slot):
        row = pl.multiple_of(tile * ROW_TILE, ROW_TILE)
        return pltpu.make_async_copy(ybuf.at[of_slot], ys_hbm.at[pl.ds(row, ROW_TILE), :], osem.at[of_slot])

    @pl.when(e == 0)
    def _():
        gather(0, 0)
        gather(1, 1)

    wgu_s[:, :EXPERT_FF] = wg_ref[...].astype(BF16)
    wgu_s[:, EXPERT_FF:] = wu_ref[...].astype(BF16)
    wd_s[...] = wd_ref[...].astype(BF16)

    def row_tile(i, carry):
        r = first_ref[e] + i
        slot = lax.rem(r, GATHER_SLOTS)
        oslot = lax.rem(r, 2)
        next_slot = lax.rem(r + 2, GATHER_SLOTS)
        drain(slot)

        @pl.when(r >= 2)
        def _():
            out_copy(r - 2, oslot).wait()

        abs_ = []
        for c in range(FFN_CHAINS):
            abs_.append(_dot(xbuf[slot, c * part:(c + 1) * part, :], wgu_s[...]))
            gather(r + 2, next_slot, c * gpr // FFN_CHAINS, (c + 1) * gpr // FFN_CHAINS)
        hs = [(_silu(ab[:, :EXPERT_FF]) * ab[:, EXPERT_FF:]).astype(BF16) for ab in abs_]
        ys = [_dot(h, wd_s[...]).astype(BF16) for h in hs]
        for c in range(FFN_CHAINS):
            ybuf[oslot, c * part:(c + 1) * part, :] = ys[c]
        out_copy(r, oslot).start()
        return carry

    lax.fori_loop(0, tiles_ref[e], row_tile, 0)

    @pl.when(e == pl.num_programs(0) - 1)
    def _():
        drain(lax.rem(n_used, GATHER_SLOTS))
        drain(lax.rem(n_used + 1, GATHER_SLOTS))
        out_copy(n_used - 1, lax.rem(n_used - 1, 2)).wait()

        @pl.when(n_used >= 2)
        def _():
            out_copy(n_used - 2, lax.rem(n_used, 2)).wait()


def _moe_experts(n_used, first, tiles, src, xs, wg, wu, wd, row_tiles):
    d = xs.shape[-1]
    ne = wg.shape[0]
    w_map = lambda e, nu, fi, ti, sr: (e, 0, 0)
    grid_spec = pltpu.PrefetchScalarGridSpec(
        num_scalar_prefetch=4,
        grid=(ne,),
        in_specs=[pl.BlockSpec(memory_space=pl.ANY),
                  pl.BlockSpec((None, d, EXPERT_FF), w_map),
                  pl.BlockSpec((None, d, EXPERT_FF), w_map),
                  pl.BlockSpec((None, EXPERT_FF, d), w_map)],
        out_specs=pl.BlockSpec(memory_space=pl.ANY),
        scratch_shapes=[pltpu.VMEM((GATHER_SLOTS, ROW_TILE, d), BF16),
                        pltpu.VMEM((2, ROW_TILE, d), BF16),
                        pltpu.SemaphoreType.DMA((GATHER_SLOTS,)),
                        pltpu.SemaphoreType.DMA((2,)),
                        pltpu.VMEM((d, 2 * EXPERT_FF), BF16),
                        pltpu.VMEM((EXPERT_FF, d), BF16)])
    return pl.pallas_call(
        _moe_experts_kernel,
        out_shape=jax.ShapeDtypeStruct((row_tiles * ROW_TILE, d), BF16),
        grid_spec=grid_spec,
        compiler_params=pltpu.CompilerParams(dimension_semantics=("arbitrary",),
                                             vmem_limit_bytes=VMEM_LIMIT),
        name="moe_experts",
    )(n_used, first, tiles, src, xs, wg, wu, wd)


def _moe_combine_kernel(back_ref, used_ref, ys_hbm, pos_ref, wts_ref, xm_ref, x1_ref, g2_ref, fg_ref,
                        swg_ref, swu_ref, swd_ref, o_ref, buf, sem, acc_ref):
    i = pl.program_id(0)
    gpt = SORT_ROWS // SORT_ALIGN
    slot = lax.rem(i, 2)
    always = ALWAYS_ROWS
    tail = range(always, SORT_ROWS, COMBINE_TAIL)

    def copies(tile, of_slot, g0, g1, start):
        for g in range(g0, g1):
            row = pl.multiple_of(back_ref[tile * gpt + g] * SORT_ALIGN, SORT_ALIGN) if start else 0
            cp = pltpu.make_async_copy(ys_hbm.at[pl.ds(row, SORT_ALIGN), :],
                                       buf.at[of_slot, g * SORT_ALIGN:(g + 1) * SORT_ALIGN, :], sem.at[of_slot])
            if start:
                cp.start(priority=g % 2)
            else:
                cp.wait()

    def transfer(tile, of_slot, start):
        copies(tile, of_slot, 0, always // SORT_ALIGN, start)
        for c0 in tail:
            pl.when(c0 < used_ref[tile])(functools.partial(
                copies, tile, of_slot, c0 // SORT_ALIGN, (c0 + COMBINE_TAIL) // SORT_ALIGN, start))

    @pl.when(i == 0)
    def _():
        transfer(0, 0, True)

    @pl.when(i + 1 < pl.num_programs(0))
    def _():
        transfer(i + 1, 1 - slot, True)

    x = xm_ref[...]
    tm = x.shape[0]
    pad = jnp.zeros((LANES - TOP_K, tm), F32)
    pos_t = jnp.concatenate([pos_ref[...], pad], axis=0).T
    wts_t = jnp.concatenate([wts_ref[...], pad], axis=0).T
    pos_b = [jnp.broadcast_to(pos_t[:, k:k + 1], (tm, LANES)) for k in range(TOP_K)]
    wts_b = [jnp.broadcast_to(wts_t[:, k:k + 1], (tm, LANES)) for k in range(TOP_K)]
    shared = _dot((_silu(_dot(x, swg_ref[...])) * _dot(x, swu_ref[...])).astype(BF16), swd_ref[...])
    transfer(i, slot, False)

    def apply(c0, width):
        reps = width // LANES
        rows = (lax.broadcasted_iota(jnp.int32, (tm, width), 1) + c0).astype(F32)
        comb = jnp.zeros((tm, width), F32)
        for k in range(TOP_K):
            comb = jnp.where(rows == jnp.concatenate([pos_b[k]] * reps, axis=1),
                             jnp.concatenate([wts_b[k]] * reps, axis=1), comb)
        return _dot(comb.astype(BF16), buf[slot, c0:c0 + width, :])

    routed = shared
    for c0 in range(0, always, COMBINE_CHUNK):
        routed = routed + apply(c0, min(COMBINE_CHUNK, always - c0))
    acc_ref[...] = routed
    for c0 in tail:
        @pl.when(c0 < used_ref[i])
        def _(c0=c0):
            acc_ref[...] += apply(c0, COMBINE_TAIL)
    y = x1_ref[...] + g2_ref[...] * acc_ref[...]
    o_ref[...] = _rms_norm(y, fg_ref[...])


def _moe_combine(back, used, ys, pos, wts, xm, x1, g2, final_g, swg, swu, swd, *, tiles_per_mod):
    n, d = xm.shape
    tm = pos.shape[-1]
    nt = n // tm
    gpt = SORT_ROWS // SORT_ALIGN
    row = lambda i, bk, us: (i, 0)
    full = lambda i, bk, us: (0, 0)
    tile = lambda i, bk, us: (i, 0, 0)
    mod_map = lambda i, bk, us: (i // tiles_per_mod, 0, 0)
    grid_spec = pltpu.PrefetchScalarGridSpec(
        num_scalar_prefetch=2,
        grid=(nt,),
        in_specs=[pl.BlockSpec(memory_space=pl.ANY),
                  pl.BlockSpec((None, TOP_K, tm), tile),
                  pl.BlockSpec((None, TOP_K, tm), tile),
                  pl.BlockSpec((tm, d), row),
                  pl.BlockSpec((tm, d), row),
                  pl.BlockSpec((None, 1, d), mod_map),
                  pl.BlockSpec((1, d), full),
                  pl.BlockSpec((d, SHARED_FF), full),
                  pl.BlockSpec((d, SHARED_FF), full),
                  pl.BlockSpec((SHARED_FF, d), full)],
        out_specs=pl.BlockSpec((tm, d), row),
        scratch_shapes=[pltpu.VMEM((2, SORT_ROWS, d), BF16),
                        pltpu.SemaphoreType.DMA((2,)),
                        pltpu.VMEM((tm, d), F32)])
    return pl.pallas_call(
        _moe_combine_kernel,
        out_shape=jax.ShapeDtypeStruct((n, d), F32),
        grid_spec=grid_spec,
        compiler_params=pltpu.CompilerParams(dimension_semantics=("arbitrary",),
                                             vmem_limit_bytes=VMEM_LIMIT),
        name="moe_combine",
    )(back, used, ys, pos.reshape(nt, TOP_K, tm), wts.reshape(nt, TOP_K, tm), xm, x1, g2, final_g, swg, swu, swd)


def _mix(x, mods, p, attn_fn, s0=None):
    sh1, sc1, g1, sh2, sc2, _ = mods
    gla_in, lora, q_s, k_s, v_s = _inproj(x, p["norm_attn_g"], sh1, sc1, p["w_gla"], p["w_lora"], p["w_swa"])
    if s0 is None:
        gla_out, s_f, s_b = _gla(gla_in, lora, p["waf"], p["baf"], p["wab"], p["bab"], p["gla_norm_g"])
    else:
        gla_out, s_f, s_b = _gla(gla_in, lora, p["waf"], p["baf"], p["wab"], p["bab"], p["gla_norm_g"],
                                 s0[0], s0[1])
    att_out = attn_fn(q_s, k_s, v_s)
    routed = _outproj(gla_out, att_out, x, p["w_out"], g1, sh2, sc2, p["norm_ffn_g"],
                      p["rw_cat"], p["rw_hi"], p["rbias"], tm=MOE_TILE)
    return routed, k_s, v_s, s_f, s_b


def _moe(streams, p):
    d = D_MODEL
    (ra, _), (rb, _) = streams
    n_tiles = [r[1].shape[0] * r[1].shape[1] // MOE_TILE for r, _ in streams]
    pos_all = jnp.concatenate([r[2].reshape(-1, TOP_K, MOE_TILE) for r, _ in streams], axis=0)
    cnt_all = jnp.concatenate([r[4][..., 0].reshape(-1, N_EXPERTS) for r, _ in streams], axis=0)
    start_all = jnp.concatenate([r[5][..., 0].reshape(-1, N_EXPERTS) for r, _ in streams], axis=0)
    used = (start_all[:, -1] + cnt_all[:, -1]).astype(jnp.int32)
    xs = _moe_sort(ra[1].reshape(-1, d), rb[1].reshape(-1, d), pos_all, used)
    n_used, first, tiles, src, back = _moe_plan(cnt_all, start_all)
    ys = _moe_experts(n_used, first, tiles, src, xs, p["wg"], p["wu"], p["wd"],
                      _moe_row_tiles(cnt_all.shape[0] * MOE_TILE))
    outs = []
    tile0 = 0
    for ((x1, xm, pos, wts, cnt, start), g2), nt in zip(streams, n_tiles):
        b, t, _ = x1.shape
        tiles_per_mod = (t // MOE_TILE) if g2.shape[0] > 1 else nt
        y = _moe_combine(back[tile0:tile0 + nt].reshape(-1), used[tile0:tile0 + nt], ys, pos, wts,
                         xm.reshape(-1, d), x1.reshape(-1, d), g2, p["final_norm_g"],
                         p["swg"], p["swu"], p["swd"], tiles_per_mod=tiles_per_mod)
        outs.append(y.reshape(b, t, d))
        tile0 += nt
    return outs


def kernel(x_prompt, x_sample, c, cache_swa_k, cache_swa_v, state_gla_fwd, state_gla_bwd, c_ctx, w_ada, b_ada, norm_attn_g, norm_ffn_g, w_in, gla_wa_f, gla_ba_f, gla_wa_b, gla_ba_b, gla_norm_g, swa_sink, w_out, router_w, router_bias, exp_w_gate, exp_w_up, exp_w_down, sh_w_gate, sh_w_up, sh_w_down, final_norm_g):
    l = 0
    d = D_MODEL
    nb_ctx, t_ctx, _ = x_prompt.shape
    nb_lat, t_lat, _ = x_sample.shape

    pad = jnp.zeros((8 - 1 - nb_lat, d), F32)
    cond8 = jnp.concatenate([c_ctx[None, :], c, pad], axis=0)
    mod = _adaln(cond8, w_ada[l], b_ada[l][None, :])
    mods_ctx = [mod[0:1, i * d:(i + 1) * d][:, None, :] for i in range(6)]
    mods_lat = [mod[1:1 + nb_lat, i * d:(i + 1) * d][:, None, :] for i in range(6)]

    zeros_lora = jnp.zeros((GLA_LORA, GLA_QK), F32)
    rw = router_w[l]
    rw_hi = rw.astype(BF16)
    rw_lo = (rw - rw_hi.astype(F32)).astype(BF16)
    w_in_b = w_in[l].astype(BF16)
    p = {
        "norm_attn_g": norm_attn_g[l][None, :],
        "norm_ffn_g": norm_ffn_g[l][None, :],
        "final_norm_g": final_norm_g[None, :],
        "w_gla": w_in_b[:, :2 * GLA_QK + 2 * GLA_V],
        "w_lora": w_in_b[:, 2 * GLA_QK + 2 * GLA_V:2 * GLA_QK + 2 * GLA_V + 2 * GLA_LORA],
        "w_swa": w_in_b[:, 2 * GLA_QK + 2 * GLA_V + 2 * GLA_LORA:],
        "waf": jnp.concatenate([gla_wa_f[l], zeros_lora], axis=0).astype(BF16),
        "wab": jnp.concatenate([zeros_lora, gla_wa_b[l]], axis=0).astype(BF16),
        "baf": gla_ba_f[l][None, :],
        "bab": gla_ba_b[l][None, :],
        "gla_norm_g": gla_norm_g[l][None, :],
        "w_out": w_out[l].astype(BF16),
        "rw_cat": jnp.concatenate([rw_hi, rw_lo], axis=1),
        "rw_hi": rw_hi,
        "rbias": router_bias[l][:, None],
        "wg": exp_w_gate[l], "wu": exp_w_up[l], "wd": exp_w_down[l],
        "swg": sh_w_gate[l].astype(BF16), "swu": sh_w_up[l].astype(BF16),
        "swd": sh_w_down[l].astype(BF16),
    }
    sink = swa_sink[l]

    routed_ctx, k_c, v_c, s_f, s_b = _mix(x_prompt, mods_ctx, p, functools.partial(_attn_ctx, sink))

    cos, sin_lo, sin_hi = _rope_tables(t_lat)
    kc = cache_swa_k[:, l].reshape(nb_lat, -1, SWA_KV)
    vc = cache_swa_v[:, l].reshape(nb_lat, -1, SWA_KV)
    lat_attn = lambda q, k, v: _attn_lat(sink, q, k, v, kc, vc, cos, sin_lo, sin_hi)
    s0 = (state_gla_fwd[:, l].reshape(nb_lat, GLA_QK, GLA_DV),
          state_gla_bwd[:, l].reshape(nb_lat, GLA_QK, GLA_DV))
    routed_lat, _, _, _, _ = _mix(x_sample, mods_lat, p, lat_attn, s0)
    y_prompt, y_sample = _moe([(routed_ctx, mods_ctx[5]), (routed_lat, mods_lat[5])], p)

    new_k = k_c.reshape(nb_ctx, 1, t_ctx, SWA_KV_HEADS, SWA_HEAD_DIM)
    new_v = v_c.reshape(nb_ctx, 1, t_ctx, SWA_KV_HEADS, SWA_HEAD_DIM)
    new_sf = s_f.reshape(nb_ctx, 1, GLA_HEADS, GLA_DK, GLA_DV)
    new_sb = s_b.reshape(nb_ctx, 1, GLA_HEADS, GLA_DK, GLA_DV)
    return (y_prompt, y_sample, new_k, new_v, new_sf, new_sb)
```

```python
import functools

import jax
import jax.numpy as jnp
from jax import lax
from jax.experimental import pallas as pl
from jax.experimental.pallas import tpu as pltpu

F32 = jnp.float32
BF16 = jnp.bfloat16

D_MODEL = 1024
GLA_HEADS = 4
GLA_DK = 64
GLA_DV = 128
GLA_LORA = 16
GLA_GATE_NORM = 16.0
GLA_CHUNK = 64
GLA_QK = GLA_HEADS * GLA_DK
GLA_V = GLA_HEADS * GLA_DV
SWA_HEAD_DIM = 64
SWA_HEADS = 8
SWA_KV_HEADS = 2
SWA_Q = SWA_HEADS * SWA_HEAD_DIM
SWA_KV = SWA_KV_HEADS * SWA_HEAD_DIM
ATTN_BLOCK = 128
GRID_W = 64
ROPE_BASE = 10000.0
N_EXPERTS = 64
TOP_K = 8
N_EXPERT_GROUPS = 8
TOPK_GROUPS = 4
EXPERT_FF = 128
SHARED_FF = 256
ROUTED_SCALE = 2.5
EPS = 1e-6

LANES = 128
VMEM_LIMIT = 56 * 1024 * 1024

NEG_INF = float("-inf")


def _dot(a, b):
    return jnp.dot(a, b, preferred_element_type=F32)


def _dot_nt(a, b):
    return lax.dot_general(a, b, (((1,), (1,)), ((), ())), preferred_element_type=F32)


def _split_hi_lo(x):
    hi = x.astype(BF16)
    lo = (x - hi.astype(F32)).astype(BF16)
    return hi, lo


def _sigmoid(x):
    return 1.0 / (1.0 + jnp.exp(-x))


def _silu(x):
    return x * _sigmoid(x)


def _rms_norm(x, g):
    ms = jnp.mean(x * x, axis=-1, keepdims=True)
    return x * lax.rsqrt(ms + EPS) * g


def _adaln_kernel(c_ref, w_ref, b_ref, o_ref):
    a_hi, a_lo = _split_hi_lo(_silu(c_ref[...]))
    w_hi, w_lo = _split_hi_lo(w_ref[...])
    o_ref[...] = _dot(a_hi, w_hi) + _dot(a_lo, w_hi) + _dot(a_hi, w_lo) + b_ref[...]


def _adaln(cond8, w_ada, b_ada):
    n = w_ada.shape[1]
    tn = 1536
    return pl.pallas_call(
        _adaln_kernel,
        out_shape=jax.ShapeDtypeStruct((8, n), F32),
        grid=(n // tn,),
        in_specs=[pl.BlockSpec((8, D_MODEL), lambda j: (0, 0)),
                  pl.BlockSpec((D_MODEL, tn), lambda j: (0, j)),
                  pl.BlockSpec((1, tn), lambda j: (0, j))],
        out_specs=pl.BlockSpec((8, tn), lambda j: (0, j)),
        compiler_params=pltpu.CompilerParams(dimension_semantics=("arbitrary",),
                                             vmem_limit_bytes=VMEM_LIMIT),
        name="adaln",
    )(cond8, w_ada, b_ada)


def _inproj_kernel(x_ref, g_ref, sh_ref, sc_ref, wg_ref, wl_ref, ws_ref,
                   gla_ref, lora_ref, q_ref, k_ref, v_ref):
    bb, tb, d = x_ref.shape
    x = x_ref[...].reshape(bb * tb, d)
    h = _rms_norm(x, g_ref[...]) * (1.0 + sc_ref[...]) + sh_ref[...]
    hb = h.astype(BF16)
    gla_ref[...] = _dot(hb, wg_ref[...]).reshape(gla_ref.shape)
    lora_ref[...] = _dot(hb, wl_ref[...]).reshape(lora_ref.shape)
    s = _dot(hb, ws_ref[...])
    q_ref[...] = s[:, :SWA_Q].reshape(q_ref.shape)
    k_ref[...] = s[:, SWA_Q:SWA_Q + SWA_KV].reshape(k_ref.shape)
    v_ref[...] = s[:, SWA_Q + SWA_KV:].reshape(v_ref.shape)


INPROJ_TILE = 512


def _inproj(x, g, sh, sc, w_gla, w_lora, w_swa):
    b, t, d = x.shape
    nmod = sh.shape[0]
    tb = min(t, INPROJ_TILE)
    bb = INPROJ_TILE // tb if nmod == 1 else 1
    mod_map = (lambda i, j: (i, 0, 0)) if nmod > 1 else (lambda i, j: (0, 0, 0))
    row = lambda i, j: (i, j, 0)
    full = lambda i, j: (0, 0)
    n_gla = w_gla.shape[1]
    n_lora = w_lora.shape[1]
    return pl.pallas_call(
        _inproj_kernel,
        out_shape=(jax.ShapeDtypeStruct((b, t, n_gla), F32),
                   jax.ShapeDtypeStruct((b, t, n_lora), F32),
                   jax.ShapeDtypeStruct((b, t, SWA_Q), F32),
                   jax.ShapeDtypeStruct((b, t, SWA_KV), F32),
                   jax.ShapeDtypeStruct((b, t, SWA_KV), F32)),
        grid=(b // bb, t // tb),
        in_specs=[pl.BlockSpec((bb, tb, d), row),
                  pl.BlockSpec((1, d), full),
                  pl.BlockSpec((None, 1, d), mod_map),
                  pl.BlockSpec((None, 1, d), mod_map),
                  pl.BlockSpec((d, n_gla), full),
                  pl.BlockSpec((d, n_lora), full),
                  pl.BlockSpec((d, w_swa.shape[1]), full)],
        out_specs=(pl.BlockSpec((bb, tb, n_gla), row),
                   pl.BlockSpec((bb, tb, n_lora), row),
                   pl.BlockSpec((bb, tb, SWA_Q), row),
                   pl.BlockSpec((bb, tb, SWA_KV), row),
                   pl.BlockSpec((bb, tb, SWA_KV), row)),
        compiler_params=pltpu.CompilerParams(dimension_semantics=("arbitrary", "arbitrary"),
                                             vmem_limit_bytes=VMEM_LIMIT),
        name="inproj",
    )(x, g, sh, sc, w_gla, w_lora, w_swa)


SCAN_UNROLL = 4
OUT_UNROLL = 4


def _log_sigmoid(x):
    return jnp.minimum(x, 0.0) - jnp.log(1.0 + jnp.exp(-jnp.abs(x)))


def _heads_to_rows(x):
    return jnp.concatenate([x[:, h * LANES:(h + 1) * LANES] for h in range(GLA_HEADS)], axis=0)


def _rows_to_heads(x, c):
    return jnp.concatenate([x[h * c:(h + 1) * c, :] for h in range(GLA_HEADS)], axis=1)


def _gla_kernel(has_init, q_ref, k_ref, v_ref, g_ref, lora_ref, waf_ref, baf_ref, wab_ref, bab_ref,
                ng_ref, *rest):
    if has_init:
        s0f_ref, s0b_ref, *rest = rest
    (out_ref, sf_ref, sb_ref, laf_ref, lab_ref, oacc_ref, qtf_ref, qtb_ref, saf_ref, sab_ref,
     stf_ref, stb_ref) = rest
    t = q_ref.shape[0]
    c = GLA_CHUNK
    n = t // c
    hc = GLA_HEADS * c

    lora = lora_ref[...].astype(BF16)
    laf_ref[...] = _log_sigmoid(_dot(lora, waf_ref[...]) + baf_ref[...]) * (1.0 / GLA_GATE_NORM)
    lab_ref[...] = _log_sigmoid(_dot(lora, wab_ref[...]) + bab_ref[...]) * (1.0 / GLA_GATE_NORM)

    if has_init:
        stf_ref[...] = s0f_ref[...].T
        stb_ref[...] = s0b_ref[...].T
    else:
        stf_ref[...] = jnp.zeros_like(stf_ref)
        stb_ref[...] = jnp.zeros_like(stb_ref)
    oacc_ref[...] = jnp.zeros_like(oacc_ref)

    r64 = lax.broadcasted_iota(jnp.int32, (c, c), 0)
    c64 = lax.broadcasted_iota(jnp.int32, (c, c), 1)
    tri_f = jnp.where(c64 <= r64, 1.0, 0.0).astype(BF16)
    tri_b = jnp.where(c64 >= r64, 1.0, 0.0).astype(BF16)
    rr = lax.broadcasted_iota(jnp.int32, (hc, hc), 0)
    cc = lax.broadcasted_iota(jnp.int32, (hc, hc), 1)
    same_head = (rr >> 6) == (cc >> 6)
    keep_f = same_head & ((rr & (c - 1)) >= (cc & (c - 1)))
    keep_b = same_head & ((rr & (c - 1)) <= (cc & (c - 1)))
    head_mask = jnp.where(same_head, 1.0, 0.0).astype(BF16)
    norm_g = ng_ref[...]

    def chunk_rows(ci):
        return pl.ds(pl.multiple_of(ci * c, c), c)

    def tile_heads(x):
        x4 = jnp.concatenate([x] * GLA_HEADS, axis=0)
        return jnp.where(same_head, x4, 0.0).astype(BF16)

    def scan_step(i, carry):
        dirs = []
        for u in range(SCAN_UNROLL):
            dirs += [(SCAN_UNROLL * i + u, laf_ref, tri_f, keep_f, c - 1, stf_ref, saf_ref, qtf_ref),
                     (n - 1 - SCAN_UNROLL * i - u, lab_ref, tri_b, keep_b, 0, stb_ref, sab_ref, qtb_ref)]
        cums = []
        for ci, la_ref, tri, _, _, _, _, _ in dirs:
            la_hi, la_lo = _split_hi_lo(la_ref[chunk_rows(ci), :])
            cums.append(_dot(tri, la_hi) + _dot(tri, la_lo))
        ops = []
        for (ci, _, _, _, last_row, _, _, qt_ref), cum in zip(dirs, cums):
            sl = chunk_rows(ci)
            tot = cum[last_row:last_row + 1, :]
            kc = k_ref[sl, :]
            qt = q_ref[sl, :] * (GLA_DK ** -0.5) * jnp.exp(cum)
            qt_ref[sl, :] = qt.astype(BF16)
            v_rows = _heads_to_rows(v_ref[sl, :])
            ops.append((tot, tile_heads(qt), tile_heads(kc * jnp.exp(-cum)),
                        tile_heads(kc * jnp.exp(tot - cum)), v_rows))
        atts = [_dot_nt(q4, k4) for _, q4, k4, _, _ in ops]
        incs = []
        for (_, _, _, keep, _, _, _, _), (_, _, _, kd4, v_rows), att in zip(dirs, ops, atts):
            att = jnp.where(keep, att, 0.0).astype(BF16)
            incs.append((_dot(att, v_rows.astype(BF16)), _dot(v_rows.T.astype(BF16), kd4)))
        for (ci, _, _, _, _, st_ref, snap_ref, _), (tot, _, _, _, _), (o_intra, st_inc) in zip(dirs, ops, incs):
            oacc_ref[ci] += o_intra
            st = st_ref[...]
            snap_ref[ci] = st.astype(BF16)
            st_ref[...] = jnp.exp(tot) * st + st_inc
        return carry

    def tile_heads_bf16(x):
        return jnp.concatenate([x] * GLA_HEADS, axis=0) * head_mask

    def out_step(i, carry):
        chunks = [OUT_UNROLL * i + u for u in range(OUT_UNROLL)]
        inter = []
        for ci in chunks:
            sl = chunk_rows(ci)
            q4 = jnp.concatenate([tile_heads_bf16(qtf_ref[sl, :]), tile_heads_bf16(qtb_ref[sl, :])], axis=1)
            st = jnp.concatenate([saf_ref[ci], sab_ref[ci]], axis=1)
            inter.append(_dot_nt(q4, st))
        for ci, o_inter in zip(chunks, inter):
            sl = chunk_rows(ci)
            on = _rms_norm(oacc_ref[ci] + o_inter, norm_g)
            gate = _silu(_heads_to_rows(g_ref[sl, :]))
            out_ref[sl, :] = _rows_to_heads(on * gate, c)
        return carry

    lax.fori_loop(0, n // SCAN_UNROLL, scan_step, 0)
    lax.fori_loop(0, n // OUT_UNROLL, out_step, 0)
    sf_ref[...] = stf_ref[...].T
    sb_ref[...] = stb_ref[...].T


def _gla(gla_in, lora, waf, baf, wab, bab, norm_g, s0f=None, s0b=None):
    b, t, _ = gla_in.shape
    has_init = s0f is not None
    n = t // GLA_CHUNK
    bmap = lambda i: (i, 0, 0)
    full = lambda i: (0, 0)
    in_specs = [pl.BlockSpec((None, t, GLA_QK), lambda i: (i, 0, 0)),
                pl.BlockSpec((None, t, GLA_QK), lambda i: (i, 0, 1)),
                pl.BlockSpec((None, t, GLA_V), lambda i: (i, 0, 1)),
                pl.BlockSpec((None, t, GLA_V), lambda i: (i, 0, 2)),
                pl.BlockSpec((None, t, 2 * GLA_LORA), bmap),
                pl.BlockSpec((2 * GLA_LORA, GLA_QK), full),
                pl.BlockSpec((1, GLA_QK), full),
                pl.BlockSpec((2 * GLA_LORA, GLA_QK), full),
                pl.BlockSpec((1, GLA_QK), full),
                pl.BlockSpec((1, GLA_DV), full)]
    args = [gla_in, gla_in, gla_in, gla_in, lora, waf, baf, wab, bab, norm_g]
    if has_init:
        in_specs += [pl.BlockSpec((None, GLA_QK, GLA_DV), bmap)] * 2
        args += [s0f, s0b]
    return pl.pallas_call(
        functools.partial(_gla_kernel, has_init),
        out_shape=(jax.ShapeDtypeStruct((b, t, GLA_V), F32),
                   jax.ShapeDtypeStruct((b, GLA_QK, GLA_DV), F32),
                   jax.ShapeDtypeStruct((b, GLA_QK, GLA_DV), F32)),
        grid=(b,),
        in_specs=in_specs,
        out_specs=(pl.BlockSpec((None, t, GLA_V), bmap),
                   pl.BlockSpec((None, GLA_QK, GLA_DV), bmap),
                   pl.BlockSpec((None, GLA_QK, GLA_DV), bmap)),
        scratch_shapes=[pltpu.VMEM((t, GLA_QK), F32),
                        pltpu.VMEM((t, GLA_QK), F32),
                        pltpu.VMEM((n, GLA_HEADS * GLA_CHUNK, GLA_DV), F32),
                        pltpu.VMEM((t, GLA_QK), BF16),
                        pltpu.VMEM((t, GLA_QK), BF16),
                        pltpu.VMEM((n, GLA_DV, GLA_QK), BF16),
                        pltpu.VMEM((n, GLA_DV, GLA_QK), BF16),
                        pltpu.VMEM((GLA_DV, GLA_QK), F32),
                        pltpu.VMEM((GLA_DV, GLA_QK), F32)],
        compiler_params=pltpu.CompilerParams(dimension_semantics=("arbitrary",),
                                             vmem_limit_bytes=VMEM_LIMIT),
        name="gla",
    )(*args)


def _dup_groups(x):
    lo = lax.broadcasted_iota(jnp.int32, x.shape, 1) < SWA_HEAD_DIM
    xr = pltpu.roll(x, SWA_HEAD_DIM, axis=1)
    return jnp.where(lo, x, xr), jnp.where(lo, xr, x)


def _pairs_attention(qps, sinks, k_dups, vt_dups, masks):
    nq = qps[0].shape[0]
    lo = lax.broadcasted_iota(jnp.int32, (nq, LANES), 1) < SWA_HEAD_DIM
    even = lax.broadcasted_iota(jnp.int32, (1, 2 * nq), 1) < nq
    scores = []
    for qp, k_dup in zip(qps, k_dups):
        q2 = jnp.concatenate([jnp.where(lo, qp, 0.0), jnp.where(lo, 0.0, qp)], axis=0).astype(BF16)
        scores.append(_dot_nt(k_dup, q2))
    probs = []
    for s, (sink_even, sink_odd), mask in zip(scores, sinks, masks):
        if mask is not None:
            s = jnp.where(mask, s, NEG_INF)
        sink = jnp.where(even, sink_even, sink_odd)
        m = jnp.maximum(jnp.max(s, axis=0, keepdims=True), sink)
        p = jnp.exp(s - m)
        denom = jnp.sum(p, axis=0, keepdims=True) + jnp.exp(sink - m)
        probs.append((p.astype(BF16), 1.0 / denom))
    outs = []
    for (p, rdenom), vt_dup in zip(probs, vt_dups):
        o = _dot(vt_dup, p) * rdenom
        outs.append(jnp.concatenate([o[:SWA_HEAD_DIM, :nq], o[SWA_HEAD_DIM:, nq:]], axis=0).T)
    return outs


CTX_BATCH = 2


def _attn_ctx_kernel(sink_ref, q_ref, k_ref, v_ref, o_ref):
    scale = SWA_HEAD_DIM ** -0.5
    pairs = range(SWA_HEADS // 2)
    items = [(bb, pr) for bb in range(q_ref.shape[0]) for pr in pairs]
    kd = [[x.astype(BF16) for x in _dup_groups(k_ref[bb])] for bb in range(q_ref.shape[0])]
    vt = [[x.T.astype(BF16) for x in _dup_groups(v_ref[bb])] for bb in range(q_ref.shape[0])]
    outs = _pairs_attention([q_ref[bb, :, pr * LANES:(pr + 1) * LANES] * scale for bb, pr in items],
                            [(sink_ref[2 * pr], sink_ref[2 * pr + 1]) for _, pr in items],
                            [kd[bb][pr // 2] for bb, pr in items], [vt[bb][pr // 2] for bb, pr in items],
                            [None] * len(items))
    for (bb, pr), out in zip(items, outs):
        o_ref[bb, :, pr * LANES:(pr + 1) * LANES] = out


def _attn_ctx(sink, q, k, v):
    b, t, _ = q.shape
    bmap = lambda i: (i, 0, 0)
    return pl.pallas_call(
        _attn_ctx_kernel,
        out_shape=jax.ShapeDtypeStruct((b, t, SWA_Q), F32),
        grid=(b // CTX_BATCH,),
        in_specs=[pl.BlockSpec(memory_space=pltpu.SMEM),
                  pl.BlockSpec((CTX_BATCH, t, SWA_Q), bmap),
                  pl.BlockSpec((CTX_BATCH, t, SWA_KV), bmap),
                  pl.BlockSpec((CTX_BATCH, t, SWA_KV), bmap)],
        out_specs=pl.BlockSpec((CTX_BATCH, t, SWA_Q), bmap),
        compiler_params=pltpu.CompilerParams(dimension_semantics=("arbitrary",),
                                             vmem_limit_bytes=VMEM_LIMIT),
        name="attn_ctx",
    )(sink, q, k, v)


LAT_BLOCKS = 2


def _rope(x, cos, sin_lo, sin_hi):
    return x * cos + pltpu.roll(x, LANES - 16, axis=1) * sin_lo + pltpu.roll(x, 16, axis=1) * sin_hi


def _attn_lat_kernel(sink_ref, q_ref, k_ref, v_ref, kc_ref, vc_ref, cos_ref, sl_ref, sh_ref,
                     o_ref, kw_ref, vw_ref):
    t = q_ref.shape[0]
    ab = ATTN_BLOCK
    nb = t // ab
    scale = SWA_HEAD_DIM ** -0.5

    k_rot = _dup_groups(_rope(k_ref[...], cos_ref[...], sl_ref[...], sh_ref[...]))
    v_dup = _dup_groups(v_ref[...])
    zeros = jnp.zeros((ab, LANES), BF16)
    for grp in range(SWA_KV_HEADS):
        kw_ref[grp, 0:ab, :] = zeros
        kw_ref[grp, ab:ab + t, :] = k_rot[grp].astype(BF16)
        kw_ref[grp, ab + t:, :] = zeros
        vw_ref[grp, 0] = zeros
        for blk in range(nb):
            vw_ref[grp, blk + 1] = v_dup[grp][blk * ab:(blk + 1) * ab, :].T.astype(BF16)
        vw_ref[grp, nb + 1] = zeros
    kc = [x.astype(BF16) for x in _dup_groups(kc_ref[...])]
    vct = [x.T.astype(BF16) for x in _dup_groups(vc_ref[...])]
    lc = kc_ref.shape[0]

    key = lax.broadcasted_iota(jnp.int32, (lc + 3 * ab, 2 * ab), 0) - lc
    tq = lax.broadcasted_iota(jnp.int32, (lc + 3 * ab, 2 * ab), 1) & (ab - 1)
    band = (key < 0) | (jnp.abs(tq + ab - key) <= ab)

    def block(it, carry):
        pairs = range(SWA_HEADS // 2)
        qps, sinks, k_dups, vt_dups, masks, places = [], [], [], [], [], []
        for u in range(LAT_BLOCKS):
            nq = it * LAT_BLOCKS + u
            row0 = pl.multiple_of(nq * ab, ab)
            s_abs = key + (nq - 1) * ab
            mask = band & ((key < 0) | ((s_abs >= 0) & (s_abs < t)))
            cos = cos_ref[pl.ds(row0, ab), :]
            s_lo = sl_ref[pl.ds(row0, ab), :]
            s_hi = sh_ref[pl.ds(row0, ab), :]
            k_all = [jnp.concatenate([kc[grp], kw_ref[grp, pl.ds(row0, 3 * ab), :]], axis=0)
                     for grp in range(SWA_KV_HEADS)]
            vt_all = [jnp.concatenate([vct[grp], vw_ref[grp, nq], vw_ref[grp, nq + 1], vw_ref[grp, nq + 2]],
                                      axis=1) for grp in range(SWA_KV_HEADS)]
            for pr in pairs:
                qps.append(_rope(q_ref[pl.ds(row0, ab), pr * LANES:(pr + 1) * LANES], cos, s_lo, s_hi) * scale)
                sinks.append((sink_ref[2 * pr], sink_ref[2 * pr + 1]))
                k_dups.append(k_all[pr // 2])
                vt_dups.append(vt_all[pr // 2])
                masks.append(mask)
                places.append((row0, pr))
        outs = _pairs_attention(qps, sinks, k_dups, vt_dups, masks)
        for (row0, pr), out in zip(places, outs):
            o_ref[pl.ds(row0, ab), pr * LANES:(pr + 1) * LANES] = out
        return carry

    lax.fori_loop(0, nb // LAT_BLOCKS, block, 0)


def _attn_lat(sink, q, k, v, kc, vc, cos, sin_lo, sin_hi):
    b, t, _ = q.shape
    lc = kc.shape[1]
    bmap = lambda i: (i, 0, 0)
    full = lambda i: (0, 0)
    return pl.pallas_call(
        _attn_lat_kernel,
        out_shape=jax.ShapeDtypeStruct((b, t, SWA_Q), F32),
        grid=(b,),
        in_specs=[pl.BlockSpec(memory_space=pltpu.SMEM),
                  pl.BlockSpec((None, t, SWA_Q), bmap),
                  pl.BlockSpec((None, t, SWA_KV), bmap),
                  pl.BlockSpec((None, t, SWA_KV), bmap),
                  pl.BlockSpec((None, lc, SWA_KV), bmap),
                  pl.BlockSpec((None, lc, SWA_KV), bmap),
                  pl.BlockSpec((t, LANES), full),
                  pl.BlockSpec((t, LANES), full),
                  pl.BlockSpec((t, LANES), full)],
        out_specs=pl.BlockSpec((None, t, SWA_Q), bmap),
        scratch_shapes=[pltpu.VMEM((SWA_KV_HEADS, t + 2 * ATTN_BLOCK, LANES), BF16),
                        pltpu.VMEM((SWA_KV_HEADS, t // ATTN_BLOCK + 2, LANES, ATTN_BLOCK), BF16)],
        compiler_params=pltpu.CompilerParams(dimension_semantics=("arbitrary",),
                                             vmem_limit_bytes=VMEM_LIMIT),
        name="attn_lat",
    )(sink, q, k, v, kc, vc, cos, sin_lo, sin_hi)


def _rope_tables(t):
    half = SWA_HEAD_DIM // 2
    quarter = half // 2
    pos = jnp.arange(t)
    row = (pos // GRID_W).astype(F32)
    col = (pos % GRID_W).astype(F32)
    inv_freq = ROPE_BASE ** (-jnp.arange(quarter, dtype=F32) / quarter)
    lane = jnp.arange(LANES)
    d = lane % SWA_HEAD_DIM
    freq = inv_freq[d % quarter]
    use_row = (d < half)
    ang = jnp.where(use_row[None, :], row[:, None], col[:, None]) * freq[None, :]
    cos = jnp.cos(ang)
    sin = jnp.sin(ang)
    lower = (d % half) < quarter
    return cos, jnp.where(lower[None, :], -sin, 0.0), jnp.where(lower[None, :], 0.0, sin)


def _route(sel, scores):
    n = sel.shape[1]
    gsz = N_EXPERTS // N_EXPERT_GROUPS

    def first_max(x, idx, size):
        m = jnp.max(x, axis=0, keepdims=True)
        first = jnp.min(jnp.where(x == m, idx, float(size)), axis=0, keepdims=True)
        return m, idx == first

    i8 = lax.broadcasted_iota(jnp.int32, (gsz, n), 0).astype(F32)
    rows = []
    for g in range(N_EXPERT_GROUPS):
        slab = sel[g * gsz:(g + 1) * gsz, :]
        m1, hit = first_max(slab, i8, gsz)
        m2 = jnp.max(jnp.where(hit, NEG_INF, slab), axis=0, keepdims=True)
        rows.append(m1 + m2)
    gscore = jnp.concatenate(rows, axis=0)
    gsel = jnp.zeros((N_EXPERT_GROUPS, n), F32)
    for _ in range(TOPK_GROUPS):
        _, hit = first_max(gscore, i8, N_EXPERT_GROUPS)
        gsel = jnp.where(hit, 1.0, gsel)
        gscore = jnp.where(hit, NEG_INF, gscore)
    emask = jnp.concatenate(
        [jnp.broadcast_to(gsel[g:g + 1, :], (gsz, n)) for g in range(N_EXPERT_GROUPS)], axis=0)
    cand = jnp.where(emask > 0.5, sel, NEG_INF)
    ie = lax.broadcasted_iota(jnp.int32, (N_EXPERTS, n), 0).astype(F32)
    w = jnp.zeros((N_EXPERTS, n), F32)
    chosen = jnp.zeros((N_EXPERTS, n), F32)
    hits = []
    for _ in range(TOP_K):
        _, hit = first_max(cand, ie, N_EXPERTS)
        hits.append(hit)
        w = jnp.where(hit, scores, w)
        chosen = jnp.where(hit, 1.0, chosen)
        cand = jnp.where(hit, NEG_INF, cand)
    gates = w / jnp.sum(w, axis=0, keepdims=True) * ROUTED_SCALE

    s_idx = lax.broadcasted_iota(jnp.int32, (n, n), 0)
    t_idx = lax.broadcasted_iota(jnp.int32, (n, n), 1)
    before = jnp.where(s_idx < t_idx, 1.0, 0.0).astype(BF16)
    rank = _dot(chosen.astype(BF16), before)
    count = jnp.sum(chosen, axis=1, keepdims=True)
    padded = jnp.floor((count + (SORT_ALIGN - 1)) * (1.0 / SORT_ALIGN)) * SORT_ALIGN
    padded = jnp.broadcast_to(padded, (N_EXPERTS, LANES))
    e_row = lax.broadcasted_iota(jnp.int32, (N_EXPERTS, N_EXPERTS), 0)
    e_col = lax.broadcasted_iota(jnp.int32, (N_EXPERTS, N_EXPERTS), 1)
    below = jnp.where(e_col < e_row, 1.0, 0.0).astype(BF16)
    start = _dot(below, padded.astype(BF16))
    row = start[:, 0:1] + rank
    pos = jnp.concatenate([jnp.sum(jnp.where(h, row, 0.0), axis=0, keepdims=True) for h in hits], axis=0)
    wts = jnp.concatenate([jnp.sum(jnp.where(h, gates, 0.0), axis=0, keepdims=True) for h in hits], axis=0)
    return pos, wts, padded, start


def _outproj_kernel(gla_ref, att_ref, x_ref, wo_ref, g1_ref, sh_ref, sc_ref, ng_ref, rw_ref, rwh_ref,
                    rb_ref, x1_ref, xm_ref, pos_ref, wts_ref, cnt_ref, start_ref):
    y = (_dot(gla_ref[...].astype(BF16), wo_ref[0:GLA_V, :])
         + _dot(att_ref[...].astype(BF16), wo_ref[GLA_V:, :]))
    x1 = x_ref[...] + g1_ref[...] * y
    x1_ref[...] = x1
    xm = _rms_norm(x1, ng_ref[...]) * (1.0 + sc_ref[...]) + sh_ref[...]
    xm_hi, xm_lo = _split_hi_lo(xm)
    xm_ref[...] = xm_hi
    lg = _dot(xm_hi, rw_ref[...])
    logits = lg[:, :N_EXPERTS] + lg[:, N_EXPERTS:] + _dot(xm_lo, rwh_ref[...])
    tm = logits.shape[0]
    lt = jnp.concatenate([logits, jnp.zeros((tm, LANES - N_EXPERTS), F32)], axis=1).T[:N_EXPERTS, :]
    scores = _sigmoid(lt)
    pos_ref[...], wts_ref[...], cnt_ref[...], start_ref[...] = _route(scores + rb_ref[...], scores)


def _outproj(gla_out, att_out, x, w_out, g1, sh2, sc2, norm_g, rw_cat, rw_hi, rbias, *, tm):
    b, t, d = x.shape
    nmod = g1.shape[0]
    mod_map = (lambda i, j: (i, 0, 0)) if nmod > 1 else (lambda i, j: (0, 0, 0))
    row = lambda i, j: (i, j, 0)
    full = lambda i, j: (0, 0)
    tile = lambda i, j: (i, j, 0, 0)
    nt = t // tm
    return pl.pallas_call(
        _outproj_kernel,
        out_shape=(jax.ShapeDtypeStruct((b, t, d), F32),
                   jax.ShapeDtypeStruct((b, t, d), BF16),
                   jax.ShapeDtypeStruct((b, nt, TOP_K, tm), F32),
                   jax.ShapeDtypeStruct((b, nt, TOP_K, tm), F32),
                   jax.ShapeDtypeStruct((b, nt, N_EXPERTS, LANES), F32),
                   jax.ShapeDtypeStruct((b, nt, N_EXPERTS, LANES), F32)),
        grid=(b, t // tm),
        in_specs=[pl.BlockSpec((None, tm, GLA_V), row),
                  pl.BlockSpec((None, tm, SWA_Q), row),
                  pl.BlockSpec((None, tm, d), row),
                  pl.BlockSpec((d, d), full),
                  pl.BlockSpec((None, 1, d), mod_map),
                  pl.BlockSpec((None, 1, d), mod_map),
                  pl.BlockSpec((None, 1, d), mod_map),
                  pl.BlockSpec((1, d), full),
                  pl.BlockSpec((d, 2 * N_EXPERTS), full),
                  pl.BlockSpec((d, N_EXPERTS), full),
                  pl.BlockSpec((N_EXPERTS, 1), full)],
        out_specs=(pl.BlockSpec((None, tm, d), row),
                   pl.BlockSpec((None, tm, d), row),
                   pl.BlockSpec((None, None, TOP_K, tm), tile),
                   pl.BlockSpec((None, None, TOP_K, tm), tile),
                   pl.BlockSpec((None, None, N_EXPERTS, LANES), tile),
                   pl.BlockSpec((None, None, N_EXPERTS, LANES), tile)),
        compiler_params=pltpu.CompilerParams(dimension_semantics=("arbitrary", "arbitrary"),
                                             vmem_limit_bytes=VMEM_LIMIT),
        name="outproj",
    )(gla_out, att_out, x, w_out, g1, sh2, sc2, norm_g, rw_cat, rw_hi, rbias)


MOE_TILE = 256
SORT_ALIGN = 16
SORT_ROWS = 3072
ROW_TILE = 512
GATHER_SLOTS = 3
FFN_CHAINS = 4
COMBINE_CHUNK = 1024
ALWAYS_ROWS = 2560
COMBINE_TAIL = 512


def _moe_sort_kernel(tiles_a, used_ref, xa_ref, xb_ref, pos_ref, xs_ref):
    i = pl.program_id(0)
    x = jnp.where(i < tiles_a, xa_ref[...], xb_ref[...])
    pos = pos_ref[...]
    tm = x.shape[0]
    used = used_ref[i]

    def fill(blk):
        rows = (lax.broadcasted_iota(jnp.int32, (tm, tm), 0) + blk * tm).astype(F32)
        onehot = jnp.zeros((tm, tm), F32)
        for k in range(TOP_K):
            onehot = jnp.where(rows == pos[k:k + 1, :], 1.0, onehot)
        xs_ref[blk * tm:(blk + 1) * tm, :] = _dot(onehot.astype(BF16), x).astype(BF16)

    for blk in range(SORT_ROWS // tm):
        if (blk + 1) * tm <= ALWAYS_ROWS:
            fill(blk)
        else:
            pl.when(blk * tm < used)(functools.partial(fill, blk))

            @pl.when(blk * tm >= used)
            def _():
                xs_ref[blk * tm:(blk + 1) * tm, :] = jnp.zeros((tm, D_MODEL), BF16)


def _moe_sort(xm_a, xm_b, pos, used):
    d = xm_a.shape[1]
    nt, _, tm = pos.shape
    tiles_a = xm_a.shape[0] // tm
    grid_spec = pltpu.PrefetchScalarGridSpec(
        num_scalar_prefetch=1,
        grid=(nt,),
        in_specs=[pl.BlockSpec((tm, d), lambda i, u: (jnp.minimum(i, tiles_a - 1), 0)),
                  pl.BlockSpec((tm, d), lambda i, u: (jnp.maximum(i - tiles_a, 0), 0)),
                  pl.BlockSpec((None, TOP_K, tm), lambda i, u: (i, 0, 0))],
        out_specs=pl.BlockSpec((SORT_ROWS, d), lambda i, u: (i, 0)))
    return pl.pallas_call(
        functools.partial(_moe_sort_kernel, tiles_a),
        out_shape=jax.ShapeDtypeStruct((nt * SORT_ROWS, d), BF16),
        grid_spec=grid_spec,
        compiler_params=pltpu.CompilerParams(dimension_semantics=("arbitrary",),
                                             vmem_limit_bytes=VMEM_LIMIT),
        name="moe_sort",
    )(used, xm_a, xm_b, pos)


def _moe_row_tiles(n_tokens):
    rows = n_tokens * TOP_K + (n_tokens // MOE_TILE) * N_EXPERTS * (SORT_ALIGN - 1) + N_EXPERTS * (ROW_TILE - 1)
    return -(-rows // ROW_TILE) + GATHER_SLOTS - 1


PLAN_CHUNK = 1280


def _int_dot_r(a, onehot):
    hi = jnp.floor(a * (1.0 / 256.0))
    return _dot(hi.astype(BF16), onehot) * 256.0 + _dot((a - hi * 256.0).astype(BF16), onehot)


def _int_dot_l(onehot, b):
    hi = jnp.floor(b * (1.0 / 256.0))
    return _dot(onehot, hi.astype(BF16)) * 256.0 + _dot(onehot, (b - hi * 256.0).astype(BF16))


def _moe_plan_kernel(cnt_ref, start_ref, src_ref, first_ref, tiles_ref, nu_ref, back_ref):
    nt, ne = cnt_ref.shape
    gpt = SORT_ROWS // SORT_ALIGN
    gpr = ROW_TILE // SORT_ALIGN
    gc = cnt_ref[...] * (1.0 / SORT_ALIGN)
    ls = start_ref[...] * (1.0 / SORT_ALIGN)

    def transpose(x):
        x = jnp.concatenate([x, jnp.zeros((nt, LANES - ne), F32)], axis=1)
        x = jnp.concatenate([x, jnp.zeros((LANES - nt, LANES), F32)], axis=0)
        return x.T[:ne, :nt]

    def tri(n, keep):
        return jnp.where(keep(lax.broadcasted_iota(jnp.int32, (n, n), 0),
                              lax.broadcasted_iota(jnp.int32, (n, n), 1)), 1.0, 0.0).astype(BF16)

    gc_t = transpose(gc)
    ls_t = transpose(ls)
    tot_c = jnp.broadcast_to(jnp.sum(gc_t, axis=1, keepdims=True), (ne, LANES))
    ptot_c = jnp.floor((tot_c + (gpr - 1)) * (1.0 / gpr)) * gpr
    gend_c = _int_dot_l(tri(ne, lambda r, c: c <= r), ptot_c)
    gstart_c = gend_c - ptot_c
    n_used = gend_c[ne - 1:ne, :] * (1.0 / gpr)
    nu_ref[...] = n_used.astype(jnp.int32)
    tot_r = jnp.sum(gc, axis=0, keepdims=True)
    ptot_r = jnp.floor((tot_r + (gpr - 1)) * (1.0 / gpr)) * gpr
    gstart_r = _int_dot_r(jnp.broadcast_to(ptot_r, (8, ne)), tri(ne, lambda r, c: r < c))
    cumex = _dot(tri(nt, lambda r, c: c < r), gc.astype(BF16))
    cumex_t = _dot(gc_t.astype(BF16), tri(nt, lambda r, c: r < c))
    tile_base = lax.broadcasted_iota(jnp.int32, (nt, ne), 0).astype(F32) * gpt + ls
    table = jnp.concatenate([cumex + gc, cumex, tile_base, gstart_r, jnp.broadcast_to(tot_r, (8, ne))], axis=0)

    e_iota = lax.broadcasted_iota(jnp.int32, (ne, PLAN_CHUNK), 0).astype(F32)
    for ch in range(src_ref.shape[1] // PLAN_CHUNK):
        g = (lax.broadcasted_iota(jnp.int32, (1, PLAN_CHUNK), 1) + ch * PLAN_CHUNK).astype(F32)
        eg = jnp.sum(jnp.where(gend_c[:, 0:1] <= g, 1.0, 0.0), axis=0, keepdims=True)
        picked = _int_dot_r(table, jnp.where(e_iota == eg, 1.0, 0.0).astype(BF16))
        cum_g, cumex_g, base_g = picked[0:nt], picked[nt:2 * nt], picked[2 * nt:3 * nt]
        u = g - picked[3 * nt:3 * nt + 1]
        in_tile = (cumex_g <= u) & (u < cum_g)
        src = jnp.sum(jnp.where(in_tile, base_g - cumex_g, 0.0), axis=0, keepdims=True) + u
        src = jnp.where(u < picked[3 * nt + 8:3 * nt + 9], src, gpt - 1.0)
        src_ref[:, ch * PLAN_CHUNK:(ch + 1) * PLAN_CHUNK] = src.astype(jnp.int32)

    first_ref[...] = (gstart_c * (1.0 / gpr)).astype(jnp.int32)
    tiles_ref[...] = (ptot_c * (1.0 / gpr)).astype(jnp.int32)

    lg = lax.broadcasted_iota(jnp.int32, (ne, back_ref.shape[1]), 1).astype(F32)
    for t in range(nt):
        first = ls_t[:, t:t + 1]
        inside = (first <= lg) & (lg < first + gc_t[:, t:t + 1])
        shift = gstart_c[:, 0:1] + cumex_t[:, t:t + 1] - first
        val = jnp.sum(jnp.where(inside, shift + lg, 0.0), axis=0, keepdims=True)
        back_ref[t:t + 1, :] = val.astype(jnp.int32)


def _moe_plan(cnt, start):
    nt, ne = cnt.shape
    row_tiles = _moe_row_tiles(nt * MOE_TILE)
    gpt = SORT_ROWS // SORT_ALIGN
    gpr = ROW_TILE // SORT_ALIGN
    n_src = -(-(row_tiles * gpr) // PLAN_CHUNK) * PLAN_CHUNK
    n_back = -(-gpt // LANES) * LANES
    src, first, tiles, nu, back = pl.pallas_call(
        _moe_plan_kernel,
        out_shape=(jax.ShapeDtypeStruct((1, n_src), jnp.int32),
                   jax.ShapeDtypeStruct((ne, LANES), jnp.int32),
                   jax.ShapeDtypeStruct((ne, LANES), jnp.int32),
                   jax.ShapeDtypeStruct((1, LANES), jnp.int32),
                   jax.ShapeDtypeStruct((nt, n_back), jnp.int32)),
        compiler_params=pltpu.CompilerParams(vmem_limit_bytes=VMEM_LIMIT),
        name="moe_plan",
    )(cnt, start)
    return nu[0, :1], first[:, 0], tiles[:, 0], src[0, :row_tiles * gpr], back[:, :gpt]


def _moe_experts_kernel(nu_ref, first_ref, tiles_ref, src_ref, xs_hbm, wg_ref, wu_ref, wd_ref, ys_hbm,
                        xbuf, ybuf, gsem, osem, wgu_s, wd_s):
    e = pl.program_id(0)
    n_used = nu_ref[0]
    gpr = ROW_TILE // SORT_ALIGN
    part = ROW_TILE // FFN_CHAINS

    def gather(tile, to_slot, j0=0, j1=gpr):
        for j in range(j0, j1):
            row = pl.multiple_of(src_ref[tile * gpr + j] * SORT_ALIGN, SORT_ALIGN)
            pltpu.make_async_copy(xs_hbm.at[pl.ds(row, SORT_ALIGN), :],
                                  xbuf.at[to_slot, j * SORT_ALIGN:(j + 1) * SORT_ALIGN, :],
                                  gsem.at[to_slot]).start(priority=j % 2)

    def drain(of_slot):
        for j in range(gpr):
            pltpu.make_async_copy(xs_hbm.at[0:SORT_ALIGN, :],
                                  xbuf.at[of_slot, j * SORT_ALIGN:(j + 1) * SORT_ALIGN, :], gsem.at[of_slot]).wait()

    def out_copy(tile, of_slot):
        row = pl.multiple_of(tile * ROW_TILE, ROW_TILE)
        return pltpu.make_async_copy(ybuf.at[of_slot], ys_hbm.at[pl.ds(row, ROW_TILE), :], osem.at[of_slot])

    @pl.when(e == 0)
    def _():
        gather(0, 0)
        gather(1, 1)

    wgu_s[:, :EXPERT_FF] = wg_ref[...].astype(BF16)
    wgu_s[:, EXPERT_FF:] = wu_ref[...].astype(BF16)
    wd_s[...] = wd_ref[...].astype(BF16)

    def row_tile(i, carry):
        r = first_ref[e] + i
        slot = lax.rem(r, GATHER_SLOTS)
        oslot = lax.rem(r, 2)
        next_slot = lax.rem(r + 2, GATHER_SLOTS)
        drain(slot)

        @pl.when(r >= 2)
        def _():
            out_copy(r - 2, oslot).wait()

        abs_ = []
        for c in range(FFN_CHAINS):
            abs_.append(_dot(xbuf[slot, c * part:(c + 1) * part, :], wgu_s[...]))
            gather(r + 2, next_slot, c * gpr // FFN_CHAINS, (c + 1) * gpr // FFN_CHAINS)
        hs = [(_silu(ab[:, :EXPERT_FF]) * ab[:, EXPERT_FF:]).astype(BF16) for ab in abs_]
        ys = [_dot(h, wd_s[...]).astype(BF16) for h in hs]
        for c in range(FFN_CHAINS):
            ybuf[oslot, c * part:(c + 1) * part, :] = ys[c]
        out_copy(r, oslot).start()
        return carry

    lax.fori_loop(0, tiles_ref[e], row_tile, 0)

    @pl.when(e == pl.num_programs(0) - 1)
    def _():
        drain(lax.rem(n_used, GATHER_SLOTS))
        drain(lax.rem(n_used + 1, GATHER_SLOTS))
        out_copy(n_used - 1, lax.rem(n_used - 1, 2)).wait()

        @pl.when(n_used >= 2)
        def _():
            out_copy(n_used - 2, lax.rem(n_used, 2)).wait()


def _moe_experts(n_used, first, tiles, src, xs, wg, wu, wd, row_tiles):
    d = xs.shape[-1]
    ne = wg.shape[0]
    w_map = lambda e, nu, fi, ti, sr: (e, 0, 0)
    grid_spec = pltpu.PrefetchScalarGridSpec(
        num_scalar_prefetch=4,
        grid=(ne,),
        in_specs=[pl.BlockSpec(memory_space=pl.ANY),
                  pl.BlockSpec((None, d, EXPERT_FF), w_map),
                  pl.BlockSpec((None, d, EXPERT_FF), w_map),
                  pl.BlockSpec((None, EXPERT_FF, d), w_map)],
        out_specs=pl.BlockSpec(memory_space=pl.ANY),
        scratch_shapes=[pltpu.VMEM((GATHER_SLOTS, ROW_TILE, d), BF16),
                        pltpu.VMEM((2, ROW_TILE, d), BF16),
                        pltpu.SemaphoreType.DMA((GATHER_SLOTS,)),
                        pltpu.SemaphoreType.DMA((2,)),
                        pltpu.VMEM((d, 2 * EXPERT_FF), BF16),
                        pltpu.VMEM((EXPERT_FF, d), BF16)])
    return pl.pallas_call(
        _moe_experts_kernel,
        out_shape=jax.ShapeDtypeStruct((row_tiles * ROW_TILE, d), BF16),
        grid_spec=grid_spec,
        compiler_params=pltpu.CompilerParams(dimension_semantics=("arbitrary",),
                                             vmem_limit_bytes=VMEM_LIMIT),
        name="moe_experts",
    )(n_used, first, tiles, src, xs, wg, wu, wd)


def _moe_combine_kernel(back_ref, used_ref, ys_hbm, pos_ref, wts_ref, xm_ref, x1_ref, g2_ref, fg_ref,
                        swg_ref, swu_ref, swd_ref, o_ref, buf, sem, acc_ref):
    i = pl.program_id(0)
    gpt = SORT_ROWS // SORT_ALIGN
    slot = lax.rem(i, 2)
    always = ALWAYS_ROWS
    tail = range(always, SORT_ROWS, COMBINE_TAIL)

    def copies(tile, of_slot, g0, g1, start):
        for g in range(g0, g1):
            row = pl.multiple_of(back_ref[tile * gpt + g] * SORT_ALIGN, SORT_ALIGN) if start else 0
            cp = pltpu.make_async_copy(ys_hbm.at[pl.ds(row, SORT_ALIGN), :],
                                       buf.at[of_slot, g * SORT_ALIGN:(g + 1) * SORT_ALIGN, :], sem.at[of_slot])
            if start:
                cp.start(priority=g % 2)
            else:
                cp.wait()

    def transfer(tile, of_slot, start):
        copies(tile, of_slot, 0, always // SORT_ALIGN, start)
        for c0 in tail:
            pl.when(c0 < used_ref[tile])(functools.partial(
                copies, tile, of_slot, c0 // SORT_ALIGN, (c0 + COMBINE_TAIL) // SORT_ALIGN, start))

    @pl.when(i == 0)
    def _():
        transfer(0, 0, True)

    @pl.when(i + 1 < pl.num_programs(0))
    def _():
        transfer(i + 1, 1 - slot, True)

    x = xm_ref[...]
    tm = x.shape[0]
    pad = jnp.zeros((LANES - TOP_K, tm), F32)
    pos_t = jnp.concatenate([pos_ref[...], pad], axis=0).T
    wts_t = jnp.concatenate([wts_ref[...], pad], axis=0).T
    pos_b = [jnp.broadcast_to(pos_t[:, k:k + 1], (tm, LANES)) for k in range(TOP_K)]
    wts_b = [jnp.broadcast_to(wts_t[:, k:k + 1], (tm, LANES)) for k in range(TOP_K)]
    shared = _dot((_silu(_dot(x, swg_ref[...])) * _dot(x, swu_ref[...])).astype(BF16), swd_ref[...])
    transfer(i, slot, False)

    def apply(c0, width):
        reps = width // LANES
        rows = (lax.broadcasted_iota(jnp.int32, (tm, width), 1) + c0).astype(F32)
        comb = jnp.zeros((tm, width), F32)
        for k in range(TOP_K):
            comb = jnp.where(rows == jnp.concatenate([pos_b[k]] * reps, axis=1),
                             jnp.concatenate([wts_b[k]] * reps, axis=1), comb)
        return _dot(comb.astype(BF16), buf[slot, c0:c0 + width, :])

    routed = shared
    for c0 in range(0, always, COMBINE_CHUNK):
        routed = routed + apply(c0, min(COMBINE_CHUNK, always - c0))
    acc_ref[...] = routed
    for c0 in tail:
        @pl.when(c0 < used_ref[i])
        def _(c0=c0):
            acc_ref[...] += apply(c0, COMBINE_TAIL)
    y = x1_ref[...] + g2_ref[...] * acc_ref[...]
    o_ref[...] = _rms_norm(y, fg_ref[...])


def _moe_combine(back, used, ys, pos, wts, xm, x1, g2, final_g, swg, swu, swd, *, tiles_per_mod):
    n, d = xm.shape
    tm = pos.shape[-1]
    nt = n // tm
    gpt = SORT_ROWS // SORT_ALIGN
    row = lambda i, bk, us: (i, 0)
    full = lambda i, bk, us: (0, 0)
    tile = lambda i, bk, us: (i, 0, 0)
    mod_map = lambda i, bk, us: (i // tiles_per_mod, 0, 0)
    grid_spec = pltpu.PrefetchScalarGridSpec(
        num_scalar_prefetch=2,
        grid=(nt,),
        in_specs=[pl.BlockSpec(memory_space=pl.ANY),
                  pl.BlockSpec((None, TOP_K, tm), tile),
                  pl.BlockSpec((None, TOP_K, tm), tile),
                  pl.BlockSpec((tm, d), row),
                  pl.BlockSpec((tm, d), row),
                  pl.BlockSpec((None, 1, d), mod_map),
                  pl.BlockSpec((1, d), full),
                  pl.BlockSpec((d, SHARED_FF), full),
                  pl.BlockSpec((d, SHARED_FF), full),
                  pl.BlockSpec((SHARED_FF, d), full)],
        out_specs=pl.BlockSpec((tm, d), row),
        scratch_shapes=[pltpu.VMEM((2, SORT_ROWS, d), BF16),
                        pltpu.SemaphoreType.DMA((2,)),
                        pltpu.VMEM((tm, d), F32)])
    return pl.pallas_call(
        _moe_combine_kernel,
        out_shape=jax.ShapeDtypeStruct((n, d), F32),
        grid_spec=grid_spec,
        compiler_params=pltpu.CompilerParams(dimension_semantics=("arbitrary",),
                                             vmem_limit_bytes=VMEM_LIMIT),
        name="moe_combine",
    )(back, used, ys, pos.reshape(nt, TOP_K, tm), wts.reshape(nt, TOP_K, tm), xm, x1, g2, final_g, swg, swu, swd)


def _mix(x, mods, p, attn_fn, s0=None):
    sh1, sc1, g1, sh2, sc2, _ = mods
    gla_in, lora, q_s, k_s, v_s = _inproj(x, p["norm_attn_g"], sh1, sc1, p["w_gla"], p["w_lora"], p["w_swa"])
    if s0 is None:
        gla_out, s_f, s_b = _gla(gla_in, lora, p["waf"], p["baf"], p["wab"], p["bab"], p["gla_norm_g"])
    else:
        gla_out, s_f, s_b = _gla(gla_in, lora, p["waf"], p["baf"], p["wab"], p["bab"], p["gla_norm_g"],
                                 s0[0], s0[1])
    att_out = attn_fn(q_s, k_s, v_s)
    routed = _outproj(gla_out, att_out, x, p["w_out"], g1, sh2, sc2, p["norm_ffn_g"],
                      p["rw_cat"], p["rw_hi"], p["rbias"], tm=MOE_TILE)
    return routed, k_s, v_s, s_f, s_b


def _moe(streams, p):
    d = D_MODEL
    (ra, _), (rb, _) = streams
    n_tiles = [r[1].shape[0] * r[1].shape[1] // MOE_TILE for r, _ in streams]
    pos_all = jnp.concatenate([r[2].reshape(-1, TOP_K, MOE_TILE) for r, _ in streams], axis=0)
    cnt_all = jnp.concatenate([r[4][..., 0].reshape(-1, N_EXPERTS) for r, _ in streams], axis=0)
    start_all = jnp.concatenate([r[5][..., 0].reshape(-1, N_EXPERTS) for r, _ in streams], axis=0)
    used = (start_all[:, -1] + cnt_all[:, -1]).astype(jnp.int32)
    xs = _moe_sort(ra[1].reshape(-1, d), rb[1].reshape(-1, d), pos_all, used)
    n_used, first, tiles, src, back = _moe_plan(cnt_all, start_all)
    ys = _moe_experts(n_used, first, tiles, src, xs, p["wg"], p["wu"], p["wd"],
                      _moe_row_tiles(cnt_all.shape[0] * MOE_TILE))
    outs = []
    tile0 = 0
    for ((x1, xm, pos, wts, cnt, start), g2), nt in zip(streams, n_tiles):
        b, t, _ = x1.shape
        tiles_per_mod = (t // MOE_TILE) if g2.shape[0] > 1 else nt
        y = _moe_combine(back[tile0:tile0 + nt].reshape(-1), used[tile0:tile0 + nt], ys, pos, wts,
                         xm.reshape(-1, d), x1.reshape(-1, d), g2, p["final_norm_g"],
                         p["swg"], p["swu"], p["swd"], tiles_per_mod=tiles_per_mod)
        outs.append(y.reshape(b, t, d))
        tile0 += nt
    return outs


def kernel(x_prompt, x_sample, c, cache_swa_k, cache_swa_v, state_gla_fwd, state_gla_bwd, c_ctx, w_ada, b_ada, norm_attn_g, norm_ffn_g, w_in, gla_wa_f, gla_ba_f, gla_wa_b, gla_ba_b, gla_norm_g, swa_sink, w_out, router_w, router_bias, exp_w_gate, exp_w_up, exp_w_down, sh_w_gate, sh_w_up, sh_w_down, final_norm_g):
    l = 0
    d = D_MODEL
    nb_ctx, t_ctx, _ = x_prompt.shape
    nb_lat, t_lat, _ = x_sample.shape

    pad = jnp.zeros((8 - 1 - nb_lat, d), F32)
    cond8 = jnp.concatenate([c_ctx[None, :], c, pad], axis=0)
    mod = _adaln(cond8, w_ada[l], b_ada[l][None, :])
    mods_ctx = [mod[0:1, i * d:(i + 1) * d][:, None, :] for i in range(6)]
    mods_lat = [mod[1:1 + nb_lat, i * d:(i + 1) * d][:, None, :] for i in range(6)]

    zeros_lora = jnp.zeros((GLA_LORA, GLA_QK), F32)
    rw = router_w[l]
    rw_hi = rw.astype(BF16)
    rw_lo = (rw - rw_hi.astype(F32)).astype(BF16)
    w_in_b = w_in[l].astype(BF16)
    p = {
        "norm_attn_g": norm_attn_g[l][None, :],
        "norm_ffn_g": norm_ffn_g[l][None, :],
        "final_norm_g": final_norm_g[None, :],
        "w_gla": w_in_b[:, :2 * GLA_QK + 2 * GLA_V],
        "w_lora": w_in_b[:, 2 * GLA_QK + 2 * GLA_V:2 * GLA_QK + 2 * GLA_V + 2 * GLA_LORA],
        "w_swa": w_in_b[:, 2 * GLA_QK + 2 * GLA_V + 2 * GLA_LORA:],
        "waf": jnp.concatenate([gla_wa_f[l], zeros_lora], axis=0).astype(BF16),
        "wab": jnp.concatenate([zeros_lora, gla_wa_b[l]], axis=0).astype(BF16),
        "baf": gla_ba_f[l][None, :],
        "bab": gla_ba_b[l][None, :],
        "gla_norm_g": gla_norm_g[l][None, :],
        "w_out": w_out[l].astype(BF16),
        "rw_cat": jnp.concatenate([rw_hi, rw_lo], axis=1),
        "rw_hi": rw_hi,
        "rbias": router_bias[l][:, None],
        "wg": exp_w_gate[l], "wu": exp_w_up[l], "wd": exp_w_down[l],
        "swg": sh_w_gate[l].astype(BF16), "swu": sh_w_up[l].astype(BF16),
        "swd": sh_w_down[l].astype(BF16),
    }
    sink = swa_sink[l]

    routed_ctx, k_c, v_c, s_f, s_b = _mix(x_prompt, mods_ctx, p, functools.partial(_attn_ctx, sink))

    cos, sin_lo, sin_hi = _rope_tables(t_lat)
    kc = cache_swa_k[:, l].reshape(nb_lat, -1, SWA_KV)
    vc = cache_swa_v[:, l].reshape(nb_lat, -1, SWA_KV)
    lat_attn = lambda q, k, v: _attn_lat(sink, q, k, v, kc, vc, cos, sin_lo, sin_hi)
    s0 = (state_gla_fwd[:, l].reshape(nb_lat, GLA_QK, GLA_DV),
          state_gla_bwd[:, l].reshape(nb_lat, GLA_QK, GLA_DV))
    routed_lat, _, _, _, _ = _mix(x_sample, mods_lat, p, lat_attn, s0)
    y_prompt, y_sample = _moe([(routed_ctx, mods_ctx[5]), (routed_lat, mods_lat[5])], p)

    new_k = k_c.reshape(nb_ctx, 1, t_ctx, SWA_KV_HEADS, SWA_HEAD_DIM)
    new_v = v_c.reshape(nb_ctx, 1, t_ctx, SWA_KV_HEADS, SWA_HEAD_DIM)
    new_sf = s_f.reshape(nb_ctx, 1, GLA_HEADS, GLA_DK, GLA_DV)
    new_sb = s_b.reshape(nb_ctx, 1, GLA_HEADS, GLA_DK, GLA_DV)
    return (y_prompt, y_sample, new_k, new_v, new_sf, new_sb)
```

```python
import functools

import jax
import jax.numpy as jnp
from jax import lax
from jax.experimental import pallas as pl
from jax.experimental.pallas import tpu as pltpu

F32 = jnp.float32
BF16 = jnp.bfloat16

D_MODEL = 1024
GLA_HEADS = 4
GLA_DK = 64
GLA_DV = 128
GLA_LORA = 16
GLA_GATE_NORM = 16.0
GLA_CHUNK = 64
GLA_QK = GLA_HEADS * GLA_DK
GLA_V = GLA_HEADS * GLA_DV
SWA_HEAD_DIM = 64
SWA_HEADS = 8
SWA_KV_HEADS = 2
SWA_Q = SWA_HEADS * SWA_HEAD_DIM
SWA_KV = SWA_KV_HEADS * SWA_HEAD_DIM
ATTN_BLOCK = 128
GRID_W = 64
ROPE_BASE = 10000.0
N_EXPERTS = 64
TOP_K = 8
N_EXPERT_GROUPS = 8
TOPK_GROUPS = 4
EXPERT_FF = 128
SHARED_FF = 256
ROUTED_SCALE = 2.5
EPS = 1e-6

LANES = 128
VMEM_LIMIT = 56 * 1024 * 1024

NEG_INF = float("-inf")


def _dot(a, b):
    return jnp.dot(a, b, preferred_element_type=F32)


def _dot_nt(a, b):
    return lax.dot_general(a, b, (((1,), (1,)), ((), ())), preferred_element_type=F32)


def _split_hi_lo(x):
    hi = x.astype(BF16)
    lo = (x - hi.astype(F32)).astype(BF16)
    return hi, lo


def _sigmoid(x):
    return 1.0 / (1.0 + jnp.exp(-x))


def _silu(x):
    return x * _sigmoid(x)


def _rms_norm(x, g):
    ms = jnp.mean(x * x, axis=-1, keepdims=True)
    return x * lax.rsqrt(ms + EPS) * g


def _adaln_kernel(c_ref, w_ref, b_ref, o_ref):
    a_hi, a_lo = _split_hi_lo(_silu(c_ref[...]))
    w_hi, w_lo = _split_hi_lo(w_ref[...])
    o_ref[...] = _dot(a_hi, w_hi) + _dot(a_lo, w_hi) + _dot(a_hi, w_lo) + b_ref[...]


def _adaln(cond8, w_ada, b_ada):
    n = w_ada.shape[1]
    tn = 1536
    return pl.pallas_call(
        _adaln_kernel,
        out_shape=jax.ShapeDtypeStruct((8, n), F32),
        grid=(n // tn,),
        in_specs=[pl.BlockSpec((8, D_MODEL), lambda j: (0, 0)),
                  pl.BlockSpec((D_MODEL, tn), lambda j: (0, j)),
                  pl.BlockSpec((1, tn), lambda j: (0, j))],
        out_specs=pl.BlockSpec((8, tn), lambda j: (0, j)),
        compiler_params=pltpu.CompilerParams(dimension_semantics=("arbitrary",),
                                             vmem_limit_bytes=VMEM_LIMIT),
        name="adaln",
    )(cond8, w_ada, b_ada)


def _inproj_kernel(x_ref, g_ref, sh_ref, sc_ref, wg_ref, wl_ref, ws_ref,
                   gla_ref, lora_ref, q_ref, k_ref, v_ref):
    bb, tb, d = x_ref.shape
    x = x_ref[...].reshape(bb * tb, d)
    h = _rms_norm(x, g_ref[...]) * (1.0 + sc_ref[...]) + sh_ref[...]
    hb = h.astype(BF16)
    gla_ref[...] = _dot(hb, wg_ref[...]).reshape(gla_ref.shape)
    lora_ref[...] = _dot(hb, wl_ref[...]).reshape(lora_ref.shape)
    s = _dot(hb, ws_ref[...])
    q_ref[...] = s[:, :SWA_Q].reshape(q_ref.shape)
    k_ref[...] = s[:, SWA_Q:SWA_Q + SWA_KV].reshape(k_ref.shape)
    v_ref[...] = s[:, SWA_Q + SWA_KV:].reshape(v_ref.shape)


INPROJ_TILE = 512


def _inproj(x, g, sh, sc, w_gla, w_lora, w_swa):
    b, t, d = x.shape
    nmod = sh.shape[0]
    tb = min(t, INPROJ_TILE)
    bb = INPROJ_TILE // tb if nmod == 1 else 1
    mod_map = (lambda i, j: (i, 0, 0)) if nmod > 1 else (lambda i, j: (0, 0, 0))
    row = lambda i, j: (i, j, 0)
    full = lambda i, j: (0, 0)
    n_gla = w_gla.shape[1]
    n_lora = w_lora.shape[1]
    return pl.pallas_call(
        _inproj_kernel,
        out_shape=(jax.ShapeDtypeStruct((b, t, n_gla), F32),
                   jax.ShapeDtypeStruct((b, t, n_lora), F32),
                   jax.ShapeDtypeStruct((b, t, SWA_Q), F32),
                   jax.ShapeDtypeStruct((b, t, SWA_KV), F32),
                   jax.ShapeDtypeStruct((b, t, SWA_KV), F32)),
        grid=(b // bb, t // tb),
        in_specs=[pl.BlockSpec((bb, tb, d), row),
                  pl.BlockSpec((1, d), full),
                  pl.BlockSpec((None, 1, d), mod_map),
                  pl.BlockSpec((None, 1, d), mod_map),
                  pl.BlockSpec((d, n_gla), full),
                  pl.BlockSpec((d, n_lora), full),
                  pl.BlockSpec((d, w_swa.shape[1]), full)],
        out_specs=(pl.BlockSpec((bb, tb, n_gla), row),
                   pl.BlockSpec((bb, tb, n_lora), row),
                   pl.BlockSpec((bb, tb, SWA_Q), row),
                   pl.BlockSpec((bb, tb, SWA_KV), row),
                   pl.BlockSpec((bb, tb, SWA_KV), row)),
        compiler_params=pltpu.CompilerParams(dimension_semantics=("arbitrary", "arbitrary"),
                                             vmem_limit_bytes=VMEM_LIMIT),
        name="inproj",
    )(x, g, sh, sc, w_gla, w_lora, w_swa)


SCAN_UNROLL = 4
OUT_UNROLL = 4


def _log_sigmoid(x):
    return jnp.minimum(x, 0.0) - jnp.log(1.0 + jnp.exp(-jnp.abs(x)))


def _heads_to_rows(x):
    return jnp.concatenate([x[:, h * LANES:(h + 1) * LANES] for h in range(GLA_HEADS)], axis=0)


def _rows_to_heads(x, c):
    return jnp.concatenate([x[h * c:(h + 1) * c, :] for h in range(GLA_HEADS)], axis=1)


def _gla_kernel(has_init, q_ref, k_ref, v_ref, g_ref, lora_ref, waf_ref, baf_ref, wab_ref, bab_ref,
                ng_ref, *rest):
    if has_init:
        s0f_ref, s0b_ref, *rest = rest
    (out_ref, sf_ref, sb_ref, laf_ref, lab_ref, oacc_ref, qtf_ref, qtb_ref, saf_ref, sab_ref,
     stf_ref, stb_ref) = rest
    t = q_ref.shape[0]
    c = GLA_CHUNK
    n = t // c
    hc = GLA_HEADS * c

    lora = lora_ref[...].astype(BF16)
    laf_ref[...] = _log_sigmoid(_dot(lora, waf_ref[...]) + baf_ref[...]) * (1.0 / GLA_GATE_NORM)
    lab_ref[...] = _log_sigmoid(_dot(lora, wab_ref[...]) + bab_ref[...]) * (1.0 / GLA_GATE_NORM)

    if has_init:
        stf_ref[...] = s0f_ref[...].T
        stb_ref[...] = s0b_ref[...].T
    else:
        stf_ref[...] = jnp.zeros_like(stf_ref)
        stb_ref[...] = jnp.zeros_like(stb_ref)
    oacc_ref[...] = jnp.zeros_like(oacc_ref)

    r64 = lax.broadcasted_iota(jnp.int32, (c, c), 0)
    c64 = lax.broadcasted_iota(jnp.int32, (c, c), 1)
    tri_f = jnp.where(c64 <= r64, 1.0, 0.0).astype(BF16)
    tri_b = jnp.where(c64 >= r64, 1.0, 0.0).astype(BF16)
    rr = lax.broadcasted_iota(jnp.int32, (hc, hc), 0)
    cc = lax.broadcasted_iota(jnp.int32, (hc, hc), 1)
    same_head = (rr >> 6) == (cc >> 6)
    keep_f = same_head & ((rr & (c - 1)) >= (cc & (c - 1)))
    keep_b = same_head & ((rr & (c - 1)) <= (cc & (c - 1)))
    head_mask = jnp.where(same_head, 1.0, 0.0).astype(BF16)
    norm_g = ng_ref[...]

    def chunk_rows(ci):
        return pl.ds(pl.multiple_of(ci * c, c), c)

    def tile_heads(x):
        x4 = jnp.concatenate([x] * GLA_HEADS, axis=0)
        return jnp.where(same_head, x4, 0.0).astype(BF16)

    def scan_step(i, carry):
        dirs = []
        for u in range(SCAN_UNROLL):
            dirs += [(SCAN_UNROLL * i + u, laf_ref, tri_f, keep_f, c - 1, stf_ref, saf_ref, qtf_ref),
                     (n - 1 - SCAN_UNROLL * i - u, lab_ref, tri_b, keep_b, 0, stb_ref, sab_ref, qtb_ref)]
        cums = []
        for ci, la_ref, tri, _, _, _, _, _ in dirs:
            la_hi, la_lo = _split_hi_lo(la_ref[chunk_rows(ci), :])
            cums.append(_dot(tri, la_hi) + _dot(tri, la_lo))
        ops = []
        for (ci, _, _, _, last_row, _, _, qt_ref), cum in zip(dirs, cums):
            sl = chunk_rows(ci)
            tot = cum[last_row:last_row + 1, :]
            kc = k_ref[sl, :]
            qt = q_ref[sl, :] * (GLA_DK ** -0.5) * jnp.exp(cum)
            qt_ref[sl, :] = qt.astype(BF16)
            v_rows = _heads_to_rows(v_ref[sl, :])
            ops.append((tot, tile_heads(qt), tile_heads(kc * jnp.exp(-cum)),
                        tile_heads(kc * jnp.exp(tot - cum)), v_rows))
        atts = [_dot_nt(q4, k4) for _, q4, k4, _, _ in ops]
        incs = []
        for (_, _, _, keep, _, _, _, _), (_, _, _, kd4, v_rows), att in zip(dirs, ops, atts):
            att = jnp.where(keep, att, 0.0).astype(BF16)
            incs.append((_dot(att, v_rows.astype(BF16)), _dot(v_rows.T.astype(BF16), kd4)))
        for (ci, _, _, _, _, st_ref, snap_ref, _), (tot, _, _, _, _), (o_intra, st_inc) in zip(dirs, ops, incs):
            oacc_ref[ci] += o_intra
            st = st_ref[...]
            snap_ref[ci] = st.astype(BF16)
            st_ref[...] = jnp.exp(tot) * st + st_inc
        return carry

    def tile_heads_bf16(x):
        return jnp.concatenate([x] * GLA_HEADS, axis=0) * head_mask

    def out_step(i, carry):
        chunks = [OUT_UNROLL * i + u for u in range(OUT_UNROLL)]
        inter = []
        for ci in chunks:
            sl = chunk_rows(ci)
            q4 = jnp.concatenate([tile_heads_bf16(qtf_ref[sl, :]), tile_heads_bf16(qtb_ref[sl, :])], axis=1)
            st = jnp.concatenate([saf_ref[ci], sab_ref[ci]], axis=1)
            inter.append(_dot_nt(q4, st))
        for ci, o_inter in zip(chunks, inter):
            sl = chunk_rows(ci)
            on = _rms_norm(oacc_ref[ci] + o_inter, norm_g)
            gate = _silu(_heads_to_rows(g_ref[sl, :]))
            out_ref[sl, :] = _rows_to_heads(on * gate, c)
        return carry

    lax.fori_loop(0, n // SCAN_UNROLL, scan_step, 0)
    lax.fori_loop(0, n // OUT_UNROLL, out_step, 0)
    sf_ref[...] = stf_ref[...].T
    sb_ref[...] = stb_ref[...].T


def _gla(gla_in, lora, waf, baf, wab, bab, norm_g, s0f=None, s0b=None):
    b, t, _ = gla_in.shape
    has_init = s0f is not None
    n = t // GLA_CHUNK
    bmap = lambda i: (i, 0, 0)
    full = lambda i: (0, 0)
    in_specs = [pl.BlockSpec((None, t, GLA_QK), lambda i: (i, 0, 0)),
                pl.BlockSpec((None, t, GLA_QK), lambda i: (i, 0, 1)),
                pl.BlockSpec((None, t, GLA_V), lambda i: (i, 0, 1)),
                pl.BlockSpec((None, t, GLA_V), lambda i: (i, 0, 2)),
                pl.BlockSpec((None, t, 2 * GLA_LORA), bmap),
                pl.BlockSpec((2 * GLA_LORA, GLA_QK), full),
                pl.BlockSpec((1, GLA_QK), full),
                pl.BlockSpec((2 * GLA_LORA, GLA_QK), full),
                pl.BlockSpec((1, GLA_QK), full),
                pl.BlockSpec((1, GLA_DV), full)]
    args = [gla_in, gla_in, gla_in, gla_in, lora, waf, baf, wab, bab, norm_g]
    if has_init:
        in_specs += [pl.BlockSpec((None, GLA_QK, GLA_DV), bmap)] * 2
        args += [s0f, s0b]
    return pl.pallas_call(
        functools.partial(_gla_kernel, has_init),
        out_shape=(jax.ShapeDtypeStruct((b, t, GLA_V), F32),
                   jax.ShapeDtypeStruct((b, GLA_QK, GLA_DV), F32),
                   jax.ShapeDtypeStruct((b, GLA_QK, GLA_DV), F32)),
        grid=(b,),
        in_specs=in_specs,
        out_specs=(pl.BlockSpec((None, t, GLA_V), bmap),
                   pl.BlockSpec((None, GLA_QK, GLA_DV), bmap),
                   pl.BlockSpec((None, GLA_QK, GLA_DV), bmap)),
        scratch_shapes=[pltpu.VMEM((t, GLA_QK), F32),
                        pltpu.VMEM((t, GLA_QK), F32),
                        pltpu.VMEM((n, GLA_HEADS * GLA_CHUNK, GLA_DV), F32),
                        pltpu.VMEM((t, GLA_QK), BF16),
                        pltpu.VMEM((t, GLA_QK), BF16),
                        pltpu.VMEM((n, GLA_DV, GLA_QK), BF16),
                        pltpu.VMEM((n, GLA_DV, GLA_QK), BF16),
                        pltpu.VMEM((GLA_DV, GLA_QK), F32),
                        pltpu.VMEM((GLA_DV, GLA_QK), F32)],
        compiler_params=pltpu.CompilerParams(dimension_semantics=("arbitrary",),
                                             vmem_limit_bytes=VMEM_LIMIT),
        name="gla",
    )(*args)


def _dup_groups(x):
    lo = lax.broadcasted_iota(jnp.int32, x.shape, 1) < SWA_HEAD_DIM
    xr = pltpu.roll(x, SWA_HEAD_DIM, axis=1)
    return jnp.where(lo, x, xr), jnp.where(lo, xr, x)


def _pairs_attention(qps, sinks, k_dups, vt_dups, masks):
    nq = qps[0].shape[0]
    lo = lax.broadcasted_iota(jnp.int32, (nq, LANES), 1) < SWA_HEAD_DIM
    even = lax.broadcasted_iota(jnp.int32, (1, 2 * nq), 1) < nq
    scores = []
    for qp, k_dup in zip(qps, k_dups):
        q2 = jnp.concatenate([jnp.where(lo, qp, 0.0), jnp.where(lo, 0.0, qp)], axis=0).astype(BF16)
        scores.append(_dot_nt(k_dup, q2))
    probs = []
    for s, (sink_even, sink_odd), mask in zip(scores, sinks, masks):
        if mask is not None:
            s = jnp.where(mask, s, NEG_INF)
        sink = jnp.where(even, sink_even, sink_odd)
        m = jnp.maximum(jnp.max(s, axis=0, keepdims=True), sink)
        p = jnp.exp(s - m)
        denom = jnp.sum(p, axis=0, keepdims=True) + jnp.exp(sink - m)
        probs.append((p.astype(BF16), 1.0 / denom))
    outs = []
    for (p, rdenom), vt_dup in zip(probs, vt_dups):
        o = _dot(vt_dup, p) * rdenom
        outs.append(jnp.concatenate([o[:SWA_HEAD_DIM, :nq], o[SWA_HEAD_DIM:, nq:]], axis=0).T)
    return outs


CTX_BATCH = 4


def _attn_ctx_kernel(sink_ref, q_ref, k_ref, v_ref, o_ref):
    scale = SWA_HEAD_DIM ** -0.5
    pairs = range(SWA_HEADS // 2)
    items = [(bb, pr) for bb in range(q_ref.shape[0]) for pr in pairs]
    kd = [[x.astype(BF16) for x in _dup_groups(k_ref[bb])] for bb in range(q_ref.shape[0])]
    vt = [[x.T.astype(BF16) for x in _dup_groups(v_ref[bb])] for bb in range(q_ref.shape[0])]
    outs = _pairs_attention([q_ref[bb, :, pr * LANES:(pr + 1) * LANES] * scale for bb, pr in items],
                            [(sink_ref[2 * pr], sink_ref[2 * pr + 1]) for _, pr in items],
                            [kd[bb][pr // 2] for bb, pr in items], [vt[bb][pr // 2] for bb, pr in items],
                            [None] * len(items))
    for (bb, pr), out in zip(items, outs):
        o_ref[bb, :, pr * LANES:(pr + 1) * LANES] = out


def _attn_ctx(sink, q, k, v):
    b, t, _ = q.shape
    bmap = lambda i: (i, 0, 0)
    return pl.pallas_call(
        _attn_ctx_kernel,
        out_shape=jax.ShapeDtypeStruct((b, t, SWA_Q), F32),
        grid=(b // CTX_BATCH,),
        in_specs=[pl.BlockSpec(memory_space=pltpu.SMEM),
                  pl.BlockSpec((CTX_BATCH, t, SWA_Q), bmap),
                  pl.BlockSpec((CTX_BATCH, t, SWA_KV), bmap),
                  pl.BlockSpec((CTX_BATCH, t, SWA_KV), bmap)],
        out_specs=pl.BlockSpec((CTX_BATCH, t, SWA_Q), bmap),
        compiler_params=pltpu.CompilerParams(dimension_semantics=("arbitrary",),
                                             vmem_limit_bytes=VMEM_LIMIT),
        name="attn_ctx",
    )(sink, q, k, v)


LAT_BLOCKS = 2


def _rope(x, cos, sin_lo, sin_hi):
    return x * cos + pltpu.roll(x, LANES - 16, axis=1) * sin_lo + pltpu.roll(x, 16, axis=1) * sin_hi


def _attn_lat_kernel(sink_ref, q_ref, k_ref, v_ref, kc_ref, vc_ref, cos_ref, sl_ref, sh_ref,
                     o_ref, kw_ref, vw_ref):
    t = q_ref.shape[0]
    ab = ATTN_BLOCK
    nb = t // ab
    scale = SWA_HEAD_DIM ** -0.5

    k_rot = _dup_groups(_rope(k_ref[...], cos_ref[...], sl_ref[...], sh_ref[...]))
    v_dup = _dup_groups(v_ref[...])
    zeros = jnp.zeros((ab, LANES), BF16)
    for grp in range(SWA_KV_HEADS):
        kw_ref[grp, 0:ab, :] = zeros
        kw_ref[grp, ab:ab + t, :] = k_rot[grp].astype(BF16)
        kw_ref[grp, ab + t:, :] = zeros
        vw_ref[grp, 0] = zeros
        for blk in range(nb):
            vw_ref[grp, blk + 1] = v_dup[grp][blk * ab:(blk + 1) * ab, :].T.astype(BF16)
        vw_ref[grp, nb + 1] = zeros
    kc = [x.astype(BF16) for x in _dup_groups(kc_ref[...])]
    vct = [x.T.astype(BF16) for x in _dup_groups(vc_ref[...])]
    lc = kc_ref.shape[0]

    key = lax.broadcasted_iota(jnp.int32, (lc + 3 * ab, 2 * ab), 0) - lc
    tq = lax.broadcasted_iota(jnp.int32, (lc + 3 * ab, 2 * ab), 1) & (ab - 1)
    band = (key < 0) | (jnp.abs(tq + ab - key) <= ab)

    def block(it, carry):
        pairs = range(SWA_HEADS // 2)
        qps, sinks, k_dups, vt_dups, masks, places = [], [], [], [], [], []
        for u in range(LAT_BLOCKS):
            nq = it * LAT_BLOCKS + u
            row0 = pl.multiple_of(nq * ab, ab)
            s_abs = key + (nq - 1) * ab
            mask = band & ((key < 0) | ((s_abs >= 0) & (s_abs < t)))
            cos = cos_ref[pl.ds(row0, ab), :]
            s_lo = sl_ref[pl.ds(row0, ab), :]
            s_hi = sh_ref[pl.ds(row0, ab), :]
            k_all = [jnp.concatenate([kc[grp], kw_ref[grp, pl.ds(row0, 3 * ab), :]], axis=0)
                     for grp in range(SWA_KV_HEADS)]
            vt_all = [jnp.concatenate([vct[grp], vw_ref[grp, nq], vw_ref[grp, nq + 1], vw_ref[grp, nq + 2]],
                                      axis=1) for grp in range(SWA_KV_HEADS)]
            for pr in pairs:
                qps.append(_rope(q_ref[pl.ds(row0, ab), pr * LANES:(pr + 1) * LANES], cos, s_lo, s_hi) * scale)
                sinks.append((sink_ref[2 * pr], sink_ref[2 * pr + 1]))
                k_dups.append(k_all[pr // 2])
                vt_dups.append(vt_all[pr // 2])
                masks.append(mask)
                places.append((row0, pr))
        outs = _pairs_attention(qps, sinks, k_dups, vt_dups, masks)
        for (row0, pr), out in zip(places, outs):
            o_ref[pl.ds(row0, ab), pr * LANES:(pr + 1) * LANES] = out
        return carry

    lax.fori_loop(0, nb // LAT_BLOCKS, block, 0)


def _attn_lat(sink, q, k, v, kc, vc, cos, sin_lo, sin_hi):
    b, t, _ = q.shape
    lc = kc.shape[1]
    bmap = lambda i: (i, 0, 0)
    full = lambda i: (0, 0)
    return pl.pallas_call(
        _attn_lat_kernel,
        out_shape=jax.ShapeDtypeStruct((b, t, SWA_Q), F32),
        grid=(b,),
        in_specs=[pl.BlockSpec(memory_space=pltpu.SMEM),
                  pl.BlockSpec((None, t, SWA_Q), bmap),
                  pl.BlockSpec((None, t, SWA_KV), bmap),
                  pl.BlockSpec((None, t, SWA_KV), bmap),
                  pl.BlockSpec((None, lc, SWA_KV), bmap),
                  pl.BlockSpec((None, lc, SWA_KV), bmap),
                  pl.BlockSpec((t, LANES), full),
                  pl.BlockSpec((t, LANES), full),
                  pl.BlockSpec((t, LANES), full)],
        out_specs=pl.BlockSpec((None, t, SWA_Q), bmap),
        scratch_shapes=[pltpu.VMEM((SWA_KV_HEADS, t + 2 * ATTN_BLOCK, LANES), BF16),
                        pltpu.VMEM((SWA_KV_HEADS, t // ATTN_BLOCK + 2, LANES, ATTN_BLOCK), BF16)],
        compiler_params=pltpu.CompilerParams(dimension_semantics=("arbitrary",),
                                             vmem_limit_bytes=VMEM_LIMIT),
        name="attn_lat",
    )(sink, q, k, v, kc, vc, cos, sin_lo, sin_hi)


def _rope_tables(t):
    half = SWA_HEAD_DIM // 2
    quarter = half // 2
    pos = jnp.arange(t)
    row = (pos // GRID_W).astype(F32)
    col = (pos % GRID_W).astype(F32)
    inv_freq = ROPE_BASE ** (-jnp.arange(quarter, dtype=F32) / quarter)
    lane = jnp.arange(LANES)
    d = lane % SWA_HEAD_DIM
    freq = inv_freq[d % quarter]
    use_row = (d < half)
    ang = jnp.where(use_row[None, :], row[:, None], col[:, None]) * freq[None, :]
    cos = jnp.cos(ang)
    sin = jnp.sin(ang)
    lower = (d % half) < quarter
    return cos, jnp.where(lower[None, :], -sin, 0.0), jnp.where(lower[None, :], 0.0, sin)


def _route(sel, scores):
    n = sel.shape[1]
    gsz = N_EXPERTS // N_EXPERT_GROUPS

    def first_max(x, idx, size):
        m = jnp.max(x, axis=0, keepdims=True)
        first = jnp.min(jnp.where(x == m, idx, float(size)), axis=0, keepdims=True)
        return m, idx == first

    i8 = lax.broadcasted_iota(jnp.int32, (gsz, n), 0).astype(F32)
    rows = []
    for g in range(N_EXPERT_GROUPS):
        slab = sel[g * gsz:(g + 1) * gsz, :]
        m1, hit = first_max(slab, i8, gsz)
        m2 = jnp.max(jnp.where(hit, NEG_INF, slab), axis=0, keepdims=True)
        rows.append(m1 + m2)
    gscore = jnp.concatenate(rows, axis=0)
    gsel = jnp.zeros((N_EXPERT_GROUPS, n), F32)
    for _ in range(TOPK_GROUPS):
        _, hit = first_max(gscore, i8, N_EXPERT_GROUPS)
        gsel = jnp.where(hit, 1.0, gsel)
        gscore = jnp.where(hit, NEG_INF, gscore)
    emask = jnp.concatenate(
        [jnp.broadcast_to(gsel[g:g + 1, :], (gsz, n)) for g in range(N_EXPERT_GROUPS)], axis=0)
    cand = jnp.where(emask > 0.5, sel, NEG_INF)
    ie = lax.broadcasted_iota(jnp.int32, (N_EXPERTS, n), 0).astype(F32)
    w = jnp.zeros((N_EXPERTS, n), F32)
    chosen = jnp.zeros((N_EXPERTS, n), F32)
    hits = []
    for _ in range(TOP_K):
        _, hit = first_max(cand, ie, N_EXPERTS)
        hits.append(hit)
        w = jnp.where(hit, scores, w)
        chosen = jnp.where(hit, 1.0, chosen)
        cand = jnp.where(hit, NEG_INF, cand)
    gates = w / jnp.sum(w, axis=0, keepdims=True) * ROUTED_SCALE

    s_idx = lax.broadcasted_iota(jnp.int32, (n, n), 0)
    t_idx = lax.broadcasted_iota(jnp.int32, (n, n), 1)
    before = jnp.where(s_idx < t_idx, 1.0, 0.0).astype(BF16)
    rank = _dot(chosen.astype(BF16), before)
    count = jnp.sum(chosen, axis=1, keepdims=True)
    padded = jnp.floor((count + (SORT_ALIGN - 1)) * (1.0 / SORT_ALIGN)) * SORT_ALIGN
    padded = jnp.broadcast_to(padded, (N_EXPERTS, LANES))
    e_row = lax.broadcasted_iota(jnp.int32, (N_EXPERTS, N_EXPERTS), 0)
    e_col = lax.broadcasted_iota(jnp.int32, (N_EXPERTS, N_EXPERTS), 1)
    below = jnp.where(e_col < e_row, 1.0, 0.0).astype(BF16)
    start = _dot(below, padded.astype(BF16))
    row = start[:, 0:1] + rank
    pos = jnp.concatenate([jnp.sum(jnp.where(h, row, 0.0), axis=0, keepdims=True) for h in hits], axis=0)
    wts = jnp.concatenate([jnp.sum(jnp.where(h, gates, 0.0), axis=0, keepdims=True) for h in hits], axis=0)
    return pos, wts, padded, start


def _outproj_kernel(gla_ref, att_ref, x_ref, wo_ref, g1_ref, sh_ref, sc_ref, ng_ref, rw_ref, rwh_ref,
                    rb_ref, x1_ref, xm_ref, pos_ref, wts_ref, cnt_ref, start_ref):
    y = (_dot(gla_ref[...].astype(BF16), wo_ref[0:GLA_V, :])
         + _dot(att_ref[...].astype(BF16), wo_ref[GLA_V:, :]))
    x1 = x_ref[...] + g1_ref[...] * y
    x1_ref[...] = x1
    xm = _rms_norm(x1, ng_ref[...]) * (1.0 + sc_ref[...]) + sh_ref[...]
    xm_hi, xm_lo = _split_hi_lo(xm)
    xm_ref[...] = xm_hi
    lg = _dot(xm_hi, rw_ref[...])
    logits = lg[:, :N_EXPERTS] + lg[:, N_EXPERTS:] + _dot(xm_lo, rwh_ref[...])
    tm = logits.shape[0]
    lt = jnp.concatenate([logits, jnp.zeros((tm, LANES - N_EXPERTS), F32)], axis=1).T[:N_EXPERTS, :]
    scores = _sigmoid(lt)
    pos_ref[...], wts_ref[...], cnt_ref[...], start_ref[...] = _route(scores + rb_ref[...], scores)


def _outproj(gla_out, att_out, x, w_out, g1, sh2, sc2, norm_g, rw_cat, rw_hi, rbias, *, tm):
    b, t, d = x.shape
    nmod = g1.shape[0]
    mod_map = (lambda i, j: (i, 0, 0)) if nmod > 1 else (lambda i, j: (0, 0, 0))
    row = lambda i, j: (i, j, 0)
    full = lambda i, j: (0, 0)
    tile = lambda i, j: (i, j, 0, 0)
    nt = t // tm
    return pl.pallas_call(
        _outproj_kernel,
        out_shape=(jax.ShapeDtypeStruct((b, t, d), F32),
                   jax.ShapeDtypeStruct((b, t, d), BF16),
                   jax.ShapeDtypeStruct((b, nt, TOP_K, tm), F32),
                   jax.ShapeDtypeStruct((b, nt, TOP_K, tm), F32),
                   jax.ShapeDtypeStruct((b, nt, N_EXPERTS, LANES), F32),
                   jax.ShapeDtypeStruct((b, nt, N_EXPERTS, LANES), F32)),
        grid=(b, t // tm),
        in_specs=[pl.BlockSpec((None, tm, GLA_V), row),
                  pl.BlockSpec((None, tm, SWA_Q), row),
                  pl.BlockSpec((None, tm, d), row),
                  pl.BlockSpec((d, d), full),
                  pl.BlockSpec((None, 1, d), mod_map),
                  pl.BlockSpec((None, 1, d), mod_map),
                  pl.BlockSpec((None, 1, d), mod_map),
                  pl.BlockSpec((1, d), full),
                  pl.BlockSpec((d, 2 * N_EXPERTS), full),
                  pl.BlockSpec((d, N_EXPERTS), full),
                  pl.BlockSpec((N_EXPERTS, 1), full)],
        out_specs=(pl.BlockSpec((None, tm, d), row),
                   pl.BlockSpec((None, tm, d), row),
                   pl.BlockSpec((None, None, TOP_K, tm), tile),
                   pl.BlockSpec((None, None, TOP_K, tm), tile),
                   pl.BlockSpec((None, None, N_EXPERTS, LANES), tile),
                   pl.BlockSpec((None, None, N_EXPERTS, LANES), tile)),
        compiler_params=pltpu.CompilerParams(dimension_semantics=("arbitrary", "arbitrary"),
                                             vmem_limit_bytes=VMEM_LIMIT),
        name="outproj",
    )(gla_out, att_out, x, w_out, g1, sh2, sc2, norm_g, rw_cat, rw_hi, rbias)


MOE_TILE = 256
SORT_ALIGN = 16
SORT_ROWS = 3072
ROW_TILE = 1024
GATHER_SLOTS = 3
FFN_CHAINS = 4
COMBINE_CHUNK = 1024
ALWAYS_ROWS = 2560
COMBINE_TAIL = 512


def _moe_sort_kernel(tiles_a, used_ref, xa_ref, xb_ref, pos_ref, xs_ref):
    i = pl.program_id(0)
    x = jnp.where(i < tiles_a, xa_ref[...], xb_ref[...])
    pos = pos_ref[...]
    tm = x.shape[0]
    used = used_ref[i]

    def fill(blk):
        rows = (lax.broadcasted_iota(jnp.int32, (tm, tm), 0) + blk * tm).astype(F32)
        onehot = jnp.zeros((tm, tm), F32)
        for k in range(TOP_K):
            onehot = jnp.where(rows == pos[k:k + 1, :], 1.0, onehot)
        xs_ref[blk * tm:(blk + 1) * tm, :] = _dot(onehot.astype(BF16), x).astype(BF16)

    for blk in range(SORT_ROWS // tm):
        if (blk + 1) * tm <= ALWAYS_ROWS:
            fill(blk)
        else:
            pl.when(blk * tm < used)(functools.partial(fill, blk))

            @pl.when(blk * tm >= used)
            def _():
                xs_ref[blk * tm:(blk + 1) * tm, :] = jnp.zeros((tm, D_MODEL), BF16)


def _moe_sort(xm_a, xm_b, pos, used):
    d = xm_a.shape[1]
    nt, _, tm = pos.shape
    tiles_a = xm_a.shape[0] // tm
    grid_spec = pltpu.PrefetchScalarGridSpec(
        num_scalar_prefetch=1,
        grid=(nt,),
        in_specs=[pl.BlockSpec((tm, d), lambda i, u: (jnp.minimum(i, tiles_a - 1), 0)),
                  pl.BlockSpec((tm, d), lambda i, u: (jnp.maximum(i - tiles_a, 0), 0)),
                  pl.BlockSpec((None, TOP_K, tm), lambda i, u: (i, 0, 0))],
        out_specs=pl.BlockSpec((SORT_ROWS, d), lambda i, u: (i, 0)))
    return pl.pallas_call(
        functools.partial(_moe_sort_kernel, tiles_a),
        out_shape=jax.ShapeDtypeStruct((nt * SORT_ROWS, d), BF16),
        grid_spec=grid_spec,
        compiler_params=pltpu.CompilerParams(dimension_semantics=("arbitrary",),
                                             vmem_limit_bytes=VMEM_LIMIT),
        name="moe_sort",
    )(used, xm_a, xm_b, pos)


def _moe_row_tiles(n_tokens):
    rows = n_tokens * TOP_K + (n_tokens // MOE_TILE) * N_EXPERTS * (SORT_ALIGN - 1) + N_EXPERTS * (ROW_TILE - 1)
    return -(-rows // ROW_TILE) + GATHER_SLOTS - 1


PLAN_CHUNK = 1280


def _int_dot_r(a, onehot):
    hi = jnp.floor(a * (1.0 / 256.0))
    return _dot(hi.astype(BF16), onehot) * 256.0 + _dot((a - hi * 256.0).astype(BF16), onehot)


def _int_dot_l(onehot, b):
    hi = jnp.floor(b * (1.0 / 256.0))
    return _dot(onehot, hi.astype(BF16)) * 256.0 + _dot(onehot, (b - hi * 256.0).astype(BF16))


def _moe_plan_kernel(cnt_ref, start_ref, src_ref, first_ref, tiles_ref, nu_ref, back_ref):
    nt, ne = cnt_ref.shape
    gpt = SORT_ROWS // SORT_ALIGN
    gpr = ROW_TILE // SORT_ALIGN
    gc = cnt_ref[...] * (1.0 / SORT_ALIGN)
    ls = start_ref[...] * (1.0 / SORT_ALIGN)

    def transpose(x):
        x = jnp.concatenate([x, jnp.zeros((nt, LANES - ne), F32)], axis=1)
        x = jnp.concatenate([x, jnp.zeros((LANES - nt, LANES), F32)], axis=0)
        return x.T[:ne, :nt]

    def tri(n, keep):
        return jnp.where(keep(lax.broadcasted_iota(jnp.int32, (n, n), 0),
                              lax.broadcasted_iota(jnp.int32, (n, n), 1)), 1.0, 0.0).astype(BF16)

    gc_t = transpose(gc)
    ls_t = transpose(ls)
    tot_c = jnp.broadcast_to(jnp.sum(gc_t, axis=1, keepdims=True), (ne, LANES))
    ptot_c = jnp.floor((tot_c + (gpr - 1)) * (1.0 / gpr)) * gpr
    gend_c = _int_dot_l(tri(ne, lambda r, c: c <= r), ptot_c)
    gstart_c = gend_c - ptot_c
    n_used = gend_c[ne - 1:ne, :] * (1.0 / gpr)
    nu_ref[...] = n_used.astype(jnp.int32)
    tot_r = jnp.sum(gc, axis=0, keepdims=True)
    ptot_r = jnp.floor((tot_r + (gpr - 1)) * (1.0 / gpr)) * gpr
    gstart_r = _int_dot_r(jnp.broadcast_to(ptot_r, (8, ne)), tri(ne, lambda r, c: r < c))
    cumex = _dot(tri(nt, lambda r, c: c < r), gc.astype(BF16))
    cumex_t = _dot(gc_t.astype(BF16), tri(nt, lambda r, c: r < c))
    tile_base = lax.broadcasted_iota(jnp.int32, (nt, ne), 0).astype(F32) * gpt + ls
    table = jnp.concatenate([cumex + gc, cumex, tile_base, gstart_r, jnp.broadcast_to(tot_r, (8, ne))], axis=0)

    e_iota = lax.broadcasted_iota(jnp.int32, (ne, PLAN_CHUNK), 0).astype(F32)
    for ch in range(src_ref.shape[1] // PLAN_CHUNK):
        g = (lax.broadcasted_iota(jnp.int32, (1, PLAN_CHUNK), 1) + ch * PLAN_CHUNK).astype(F32)
        eg = jnp.sum(jnp.where(gend_c[:, 0:1] <= g, 1.0, 0.0), axis=0, keepdims=True)
        picked = _int_dot_r(table, jnp.where(e_iota == eg, 1.0, 0.0).astype(BF16))
        cum_g, cumex_g, base_g = picked[0:nt], picked[nt:2 * nt], picked[2 * nt:3 * nt]
        u = g - picked[3 * nt:3 * nt + 1]
        in_tile = (cumex_g <= u) & (u < cum_g)
        src = jnp.sum(jnp.where(in_tile, base_g - cumex_g, 0.0), axis=0, keepdims=True) + u
        src = jnp.where(u < picked[3 * nt + 8:3 * nt + 9], src, gpt - 1.0)
        src_ref[:, ch * PLAN_CHUNK:(ch + 1) * PLAN_CHUNK] = src.astype(jnp.int32)

    first_ref[...] = (gstart_c * (1.0 / gpr)).astype(jnp.int32)
    tiles_ref[...] = (ptot_c * (1.0 / gpr)).astype(jnp.int32)

    lg = lax.broadcasted_iota(jnp.int32, (ne, back_ref.shape[1]), 1).astype(F32)
    for t in range(nt):
        first = ls_t[:, t:t + 1]
        inside = (first <= lg) & (lg < first + gc_t[:, t:t + 1])
        shift = gstart_c[:, 0:1] + cumex_t[:, t:t + 1] - first
        val = jnp.sum(jnp.where(inside, shift + lg, 0.0), axis=0, keepdims=True)
        back_ref[t:t + 1, :] = val.astype(jnp.int32)


def _moe_plan(cnt, start):
    nt, ne = cnt.shape
    row_tiles = _moe_row_tiles(nt * MOE_TILE)
    gpt = SORT_ROWS // SORT_ALIGN
    gpr = ROW_TILE // SORT_ALIGN
    n_src = -(-(row_tiles * gpr) // PLAN_CHUNK) * PLAN_CHUNK
    n_back = -(-gpt // LANES) * LANES
    src, first, tiles, nu, back = pl.pallas_call(
        _moe_plan_kernel,
        out_shape=(jax.ShapeDtypeStruct((1, n_src), jnp.int32),
                   jax.ShapeDtypeStruct((ne, LANES), jnp.int32),
                   jax.ShapeDtypeStruct((ne, LANES), jnp.int32),
                   jax.ShapeDtypeStruct((1, LANES), jnp.int32),
                   jax.ShapeDtypeStruct((nt, n_back), jnp.int32)),
        compiler_params=pltpu.CompilerParams(vmem_limit_bytes=VMEM_LIMIT),
        name="moe_plan",
    )(cnt, start)
    return nu[0, :1], first[:, 0], tiles[:, 0], src[0, :row_tiles * gpr], back[:, :gpt]


def _moe_experts_kernel(nu_ref, first_ref, tiles_ref, src_ref, xs_hbm, wg_ref, wu_ref, wd_ref, ys_hbm,
                        xbuf, ybuf, gsem, osem, wgu_s, wd_s):
    e = pl.program_id(0)
    n_used = nu_ref[0]
    gpr = ROW_TILE // SORT_ALIGN
    part = ROW_TILE // FFN_CHAINS

    def gather(tile, to_slot, j0=0, j1=gpr):
        for j in range(j0, j1):
            row = pl.multiple_of(src_ref[tile * gpr + j] * SORT_ALIGN, SORT_ALIGN)
            pltpu.make_async_copy(xs_hbm.at[pl.ds(row, SORT_ALIGN), :],
                                  xbuf.at[to_slot, j * SORT_ALIGN:(j + 1) * SORT_ALIGN, :],
                                  gsem.at[to_slot]).start(priority=j % 2)

    def drain(of_slot):
        for j in range(gpr):
            pltpu.make_async_copy(xs_hbm.at[0:SORT_ALIGN, :],
                                  xbuf.at[of_slot, j * SORT_ALIGN:(j + 1) * SORT_ALIGN, :], gsem.at[of_slot]).wait()

    def out_copy(tile, of_slot):
        row = pl.multiple_of(tile * ROW_TILE, ROW_TILE)
        return pltpu.make_async_copy(ybuf.at[of_slot], ys_hbm.at[pl.ds(row, ROW_TILE), :], osem.at[of_slot])

    @pl.when(e == 0)
    def _():
        gather(0, 0)
        gather(1, 1)

    wgu_s[:, :EXPERT_FF] = wg_ref[...].astype(BF16)
    wgu_s[:, EXPERT_FF:] = wu_ref[...].astype(BF16)
    wd_s[...] = wd_ref[...].astype(BF16)

    def row_tile(i, carry):
        r = first_ref[e] + i
        slot = lax.rem(r, GATHER_SLOTS)
        oslot = lax.rem(r, 2)
        next_slot = lax.rem(r + 2, GATHER_SLOTS)
        drain(slot)

        @pl.when(r >= 2)
        def _():
            out_copy(r - 2, oslot).wait()

        abs_ = []
        for c in range(FFN_CHAINS):
            abs_.append(_dot(xbuf[slot, c * part:(c + 1) * part, :], wgu_s[...]))
            gather(r + 2, next_slot, c * gpr // FFN_CHAINS, (c + 1) * gpr // FFN_CHAINS)
        hs = [(_silu(ab[:, :EXPERT_FF]) * ab[:, EXPERT_FF:]).astype(BF16) for ab in abs_]
        ys = [_dot(h, wd_s[...]).astype(BF16) for h in hs]
        for c in range(FFN_CHAINS):
            ybuf[oslot, c * part:(c + 1) * part, :] = ys[c]
        out_copy(r, oslot).start()
        return carry

    lax.fori_loop(0, tiles_ref[e], row_tile, 0)

    @pl.when(e == pl.num_programs(0) - 1)
    def _():
        drain(lax.rem(n_used, GATHER_SLOTS))
        drain(lax.rem(n_used + 1, GATHER_SLOTS))
        out_copy(n_used - 1, lax.rem(n_used - 1, 2)).wait()

        @pl.when(n_used >= 2)
        def _():
            out_copy(n_used - 2, lax.rem(n_used, 2)).wait()


def _moe_experts(n_used, first, tiles, src, xs, wg, wu, wd, row_tiles):
    d = xs.shape[-1]
    ne = wg.shape[0]
    w_map = lambda e, nu, fi, ti, sr: (e, 0, 0)
    grid_spec = pltpu.PrefetchScalarGridSpec(
        num_scalar_prefetch=4,
        grid=(ne,),
        in_specs=[pl.BlockSpec(memory_space=pl.ANY),
                  pl.BlockSpec((None, d, EXPERT_FF), w_map),
                  pl.BlockSpec((None, d, EXPERT_FF), w_map),
                  pl.BlockSpec((None, EXPERT_FF, d), w_map)],
        out_specs=pl.BlockSpec(memory_space=pl.ANY),
        scratch_shapes=[pltpu.VMEM((GATHER_SLOTS, ROW_TILE, d), BF16),
                        pltpu.VMEM((2, ROW_TILE, d), BF16),
                        pltpu.SemaphoreType.DMA((GATHER_SLOTS,)),
                        pltpu.SemaphoreType.DMA((2,)),
                        pltpu.VMEM((d, 2 * EXPERT_FF), BF16),
                        pltpu.VMEM((EXPERT_FF, d), BF16)])
    return pl.pallas_call(
        _moe_experts_kernel,
        out_shape=jax.ShapeDtypeStruct((row_tiles * ROW_TILE, d), BF16),
        grid_spec=grid_spec,
        compiler_params=pltpu.CompilerParams(dimension_semantics=("arbitrary",),
                                             vmem_limit_bytes=VMEM_LIMIT),
        name="moe_experts",
    )(n_used, first, tiles, src, xs, wg, wu, wd)


def _moe_combine_kernel(back_ref, used_ref, ys_hbm, pos_ref, wts_ref, xm_ref, x1_ref, g2_ref, fg_ref,
                        swg_ref, swu_ref, swd_ref, o_ref, buf, sem, acc_ref):
    i = pl.program_id(0)
    gpt = SORT_ROWS // SORT_ALIGN
    slot = lax.rem(i, 2)
    always = ALWAYS_ROWS
    tail = range(always, SORT_ROWS, COMBINE_TAIL)

    def copies(tile, of_slot, g0, g1, start):
        for g in range(g0, g1):
            row = pl.multiple_of(back_ref[tile * gpt + g] * SORT_ALIGN, SORT_ALIGN) if start else 0
            cp = pltpu.make_async_copy(ys_hbm.at[pl.ds(row, SORT_ALIGN), :],
                                       buf.at[of_slot, g * SORT_ALIGN:(g + 1) * SORT_ALIGN, :], sem.at[of_slot])
            if start:
                cp.start(priority=g % 2)
            else:
                cp.wait()

    def transfer(tile, of_slot, start):
        copies(tile, of_slot, 0, always // SORT_ALIGN, start)
        for c0 in tail:
            pl.when(c0 < used_ref[tile])(functools.partial(
                copies, tile, of_slot, c0 // SORT_ALIGN, (c0 + COMBINE_TAIL) // SORT_ALIGN, start))

    @pl.when(i == 0)
    def _():
        transfer(0, 0, True)

    @pl.when(i + 1 < pl.num_programs(0))
    def _():
        transfer(i + 1, 1 - slot, True)

    x = xm_ref[...]
    tm = x.shape[0]
    pad = jnp.zeros((LANES - TOP_K, tm), F32)
    pos_t = jnp.concatenate([pos_ref[...], pad], axis=0).T
    wts_t = jnp.concatenate([wts_ref[...], pad], axis=0).T
    pos_b = [jnp.broadcast_to(pos_t[:, k:k + 1], (tm, LANES)) for k in range(TOP_K)]
    wts_b = [jnp.broadcast_to(wts_t[:, k:k + 1], (tm, LANES)) for k in range(TOP_K)]
    shared = _dot((_silu(_dot(x, swg_ref[...])) * _dot(x, swu_ref[...])).astype(BF16), swd_ref[...])
    transfer(i, slot, False)

    def apply(c0, width):
        reps = width // LANES
        rows = (lax.broadcasted_iota(jnp.int32, (tm, width), 1) + c0).astype(F32)
        comb = jnp.zeros((tm, width), F32)
        for k in range(TOP_K):
            comb = jnp.where(rows == jnp.concatenate([pos_b[k]] * reps, axis=1),
                             jnp.concatenate([wts_b[k]] * reps, axis=1), comb)
        return _dot(comb.astype(BF16), buf[slot, c0:c0 + width, :])

    routed = shared
    for c0 in range(0, always, COMBINE_CHUNK):
        routed = routed + apply(c0, min(COMBINE_CHUNK, always - c0))
    acc_ref[...] = routed
    for c0 in tail:
        @pl.when(c0 < used_ref[i])
        def _(c0=c0):
            acc_ref[...] += apply(c0, COMBINE_TAIL)
    y = x1_ref[...] + g2_ref[...] * acc_ref[...]
    o_ref[...] = _rms_norm(y, fg_ref[...])


def _moe_combine(back, used, ys, pos, wts, xm, x1, g2, final_g, swg, swu, swd, *, tiles_per_mod):
    n, d = xm.shape
    tm = pos.shape[-1]
    nt = n // tm
    gpt = SORT_ROWS // SORT_ALIGN
    row = lambda i, bk, us: (i, 0)
    full = lambda i, bk, us: (0, 0)
    tile = lambda i, bk, us: (i, 0, 0)
    mod_map = lambda i, bk, us: (i // tiles_per_mod, 0, 0)
    grid_spec = pltpu.PrefetchScalarGridSpec(
        num_scalar_prefetch=2,
        grid=(nt,),
        in_specs=[pl.BlockSpec(memory_space=pl.ANY),
                  pl.BlockSpec((None, TOP_K, tm), tile),
                  pl.BlockSpec((None, TOP_K, tm), tile),
                  pl.BlockSpec((tm, d), row),
                  pl.BlockSpec((tm, d), row),
                  pl.BlockSpec((None, 1, d), mod_map),
                  pl.BlockSpec((1, d), full),
                  pl.BlockSpec((d, SHARED_FF), full),
                  pl.BlockSpec((d, SHARED_FF), full),
                  pl.BlockSpec((SHARED_FF, d), full)],
        out_specs=pl.BlockSpec((tm, d), row),
        scratch_shapes=[pltpu.VMEM((2, SORT_ROWS, d), BF16),
                        pltpu.SemaphoreType.DMA((2,)),
                        pltpu.VMEM((tm, d), F32)])
    return pl.pallas_call(
        _moe_combine_kernel,
        out_shape=jax.ShapeDtypeStruct((n, d), F32),
        grid_spec=grid_spec,
        compiler_params=pltpu.CompilerParams(dimension_semantics=("arbitrary",),
                                             vmem_limit_bytes=VMEM_LIMIT),
        name="moe_combine",
    )(back, used, ys, pos.reshape(nt, TOP_K, tm), wts.reshape(nt, TOP_K, tm), xm, x1, g2, final_g, swg, swu, swd)


def _mix(x, mods, p, attn_fn, s0=None):
    sh1, sc1, g1, sh2, sc2, _ = mods
    gla_in, lora, q_s, k_s, v_s = _inproj(x, p["norm_attn_g"], sh1, sc1, p["w_gla"], p["w_lora"], p["w_swa"])
    if s0 is None:
        gla_out, s_f, s_b = _gla(gla_in, lora, p["waf"], p["baf"], p["wab"], p["bab"], p["gla_norm_g"])
    else:
        gla_out, s_f, s_b = _gla(gla_in, lora, p["waf"], p["baf"], p["wab"], p["bab"], p["gla_norm_g"],
                                 s0[0], s0[1])
    att_out = attn_fn(q_s, k_s, v_s)
    routed = _outproj(gla_out, att_out, x, p["w_out"], g1, sh2, sc2, p["norm_ffn_g"],
                      p["rw_cat"], p["rw_hi"], p["rbias"], tm=MOE_TILE)
    return routed, k_s, v_s, s_f, s_b


def _moe(streams, p):
    d = D_MODEL
    (ra, _), (rb, _) = streams
    n_tiles = [r[1].shape[0] * r[1].shape[1] // MOE_TILE for r, _ in streams]
    pos_all = jnp.concatenate([r[2].reshape(-1, TOP_K, MOE_TILE) for r, _ in streams], axis=0)
    cnt_all = jnp.concatenate([r[4][..., 0].reshape(-1, N_EXPERTS) for r, _ in streams], axis=0)
    start_all = jnp.concatenate([r[5][..., 0].reshape(-1, N_EXPERTS) for r, _ in streams], axis=0)
    used = (start_all[:, -1] + cnt_all[:, -1]).astype(jnp.int32)
    xs = _moe_sort(ra[1].reshape(-1, d), rb[1].reshape(-1, d), pos_all, used)
    n_used, first, tiles, src, back = _moe_plan(cnt_all, start_all)
    ys = _moe_experts(n_used, first, tiles, src, xs, p["wg"], p["wu"], p["wd"],
                      _moe_row_tiles(cnt_all.shape[0] * MOE_TILE))
    outs = []
    tile0 = 0
    for ((x1, xm, pos, wts, cnt, start), g2), nt in zip(streams, n_tiles):
        b, t, _ = x1.shape
        tiles_per_mod = (t // MOE_TILE) if g2.shape[0] > 1 else nt
        y = _moe_combine(back[tile0:tile0 + nt].reshape(-1), used[tile0:tile0 + nt], ys, pos, wts,
                         xm.reshape(-1, d), x1.reshape(-1, d), g2, p["final_norm_g"],
                         p["swg"], p["swu"], p["swd"], tiles_per_mod=tiles_per_mod)
        outs.append(y.reshape(b, t, d))
        tile0 += nt
    return outs


def kernel(x_prompt, x_sample, c, cache_swa_k, cache_swa_v, state_gla_fwd, state_gla_bwd, c_ctx, w_ada, b_ada, norm_attn_g, norm_ffn_g, w_in, gla_wa_f, gla_ba_f, gla_wa_b, gla_ba_b, gla_norm_g, swa_sink, w_out, router_w, router_bias, exp_w_gate, exp_w_up, exp_w_down, sh_w_gate, sh_w_up, sh_w_down, final_norm_g):
    l = 0
    d = D_MODEL
    nb_ctx, t_ctx, _ = x_prompt.shape
    nb_lat, t_lat, _ = x_sample.shape

    pad = jnp.zeros((8 - 1 - nb_lat, d), F32)
    cond8 = jnp.concatenate([c_ctx[None, :], c, pad], axis=0)
    mod = _adaln(cond8, w_ada[l], b_ada[l][None, :])
    mods_ctx = [mod[0:1, i * d:(i + 1) * d][:, None, :] for i in range(6)]
    mods_lat = [mod[1:1 + nb_lat, i * d:(i + 1) * d][:, None, :] for i in range(6)]

    zeros_lora = jnp.zeros((GLA_LORA, GLA_QK), F32)
    rw = router_w[l]
    rw_hi = rw.astype(BF16)
    rw_lo = (rw - rw_hi.astype(F32)).astype(BF16)
    w_in_b = w_in[l].astype(BF16)
    p = {
        "norm_attn_g": norm_attn_g[l][None, :],
        "norm_ffn_g": norm_ffn_g[l][None, :],
        "final_norm_g": final_norm_g[None, :],
        "w_gla": w_in_b[:, :2 * GLA_QK + 2 * GLA_V],
        "w_lora": w_in_b[:, 2 * GLA_QK + 2 * GLA_V:2 * GLA_QK + 2 * GLA_V + 2 * GLA_LORA],
        "w_swa": w_in_b[:, 2 * GLA_QK + 2 * GLA_V + 2 * GLA_LORA:],
        "waf": jnp.concatenate([gla_wa_f[l], zeros_lora], axis=0).astype(BF16),
        "wab": jnp.concatenate([zeros_lora, gla_wa_b[l]], axis=0).astype(BF16),
        "baf": gla_ba_f[l][None, :],
        "bab": gla_ba_b[l][None, :],
        "gla_norm_g": gla_norm_g[l][None, :],
        "w_out": w_out[l].astype(BF16),
        "rw_cat": jnp.concatenate([rw_hi, rw_lo], axis=1),
        "rw_hi": rw_hi,
        "rbias": router_bias[l][:, None],
        "wg": exp_w_gate[l], "wu": exp_w_up[l], "wd": exp_w_down[l],
        "swg": sh_w_gate[l].astype(BF16), "swu": sh_w_up[l].astype(BF16),
        "swd": sh_w_down[l].astype(BF16),
    }
    sink = swa_sink[l]

    routed_ctx, k_c, v_c, s_f, s_b = _mix(x_prompt, mods_ctx, p, functools.partial(_attn_ctx, sink))

    cos, sin_lo, sin_hi = _rope_tables(t_lat)
    kc = cache_swa_k[:, l].reshape(nb_lat, -1, SWA_KV)
    vc = cache_swa_v[:, l].reshape(nb_lat, -1, SWA_KV)
    lat_attn = lambda q, k, v: _attn_lat(sink, q, k, v, kc, vc, cos, sin_lo, sin_hi)
    s0 = (state_gla_fwd[:, l].reshape(nb_lat, GLA_QK, GLA_DV),
          state_gla_bwd[:, l].reshape(nb_lat, GLA_QK, GLA_DV))
    routed_lat, _, _, _, _ = _mix(x_sample, mods_lat, p, lat_attn, s0)
    y_prompt, y_sample = _moe([(routed_ctx, mods_ctx[5]), (routed_lat, mods_lat[5])], p)

    new_k = k_c.reshape(nb_ctx, 1, t_ctx, SWA_KV_HEADS, SWA_HEAD_DIM)
    new_v = v_c.reshape(nb_ctx, 1, t_ctx, SWA_KV_HEADS, SWA_HEAD_DIM)
    new_sf = s_f.reshape(nb_ctx, 1, GLA_HEADS, GLA_DK, GLA_DV)
    new_sb = s_b.reshape(nb_ctx, 1, GLA_HEADS, GLA_DK, GLA_DV)
    return (y_prompt, y_sample, new_k, new_v, new_sf, new_sb)
```

```python
import functools

import jax
import jax.numpy as jnp
from jax import lax
from jax.experimental import pallas as pl
from jax.experimental.pallas import tpu as pltpu

F32 = jnp.float32
BF16 = jnp.bfloat16

D_MODEL = 1024
GLA_HEADS = 4
GLA_DK = 64
GLA_DV = 128
GLA_LORA = 16
GLA_GATE_NORM = 16.0
GLA_CHUNK = 64
GLA_QK = GLA_HEADS * GLA_DK
GLA_V = GLA_HEADS * GLA_DV
SWA_HEAD_DIM = 64
SWA_HEADS = 8
SWA_KV_HEADS = 2
SWA_Q = SWA_HEADS * SWA_HEAD_DIM
SWA_KV = SWA_KV_HEADS * SWA_HEAD_DIM
ATTN_BLOCK = 128
GRID_W = 64
ROPE_BASE = 10000.0
N_EXPERTS = 64
TOP_K = 8
N_EXPERT_GROUPS = 8
TOPK_GROUPS = 4
EXPERT_FF = 128
SHARED_FF = 256
ROUTED_SCALE = 2.5
EPS = 1e-6

LANES = 128
VMEM_LIMIT = 56 * 1024 * 1024

NEG_INF = float("-inf")


def _dot(a, b):
    return jnp.dot(a, b, preferred_element_type=F32)


def _dot_nt(a, b):
    return lax.dot_general(a, b, (((1,), (1,)), ((), ())), preferred_element_type=F32)


def _split_hi_lo(x):
    hi = x.astype(BF16)
    lo = (x - hi.astype(F32)).astype(BF16)
    return hi, lo


def _sigmoid(x):
    return 1.0 / (1.0 + jnp.exp(-x))


def _silu(x):
    return x * _sigmoid(x)


def _rms_norm(x, g):
    ms = jnp.mean(x * x, axis=-1, keepdims=True)
    return x * lax.rsqrt(ms + EPS) * g


def _adaln_kernel(c_ref, w_ref, b_ref, o_ref):
    a_hi, a_lo = _split_hi_lo(_silu(c_ref[...]))
    w_hi, w_lo = _split_hi_lo(w_ref[...])
    o_ref[...] = _dot(a_hi, w_hi) + _dot(a_lo, w_hi) + _dot(a_hi, w_lo) + b_ref[...]


def _adaln(cond8, w_ada, b_ada):
    n = w_ada.shape[1]
    tn = 1536
    return pl.pallas_call(
        _adaln_kernel,
        out_shape=jax.ShapeDtypeStruct((8, n), F32),
        grid=(n // tn,),
        in_specs=[pl.BlockSpec((8, D_MODEL), lambda j: (0, 0)),
                  pl.BlockSpec((D_MODEL, tn), lambda j: (0, j)),
                  pl.BlockSpec((1, tn), lambda j: (0, j))],
        out_specs=pl.BlockSpec((8, tn), lambda j: (0, j)),
        compiler_params=pltpu.CompilerParams(dimension_semantics=("arbitrary",),
                                             vmem_limit_bytes=VMEM_LIMIT),
        name="adaln",
    )(cond8, w_ada, b_ada)


def _inproj_kernel(x_ref, g_ref, sh_ref, sc_ref, wg_ref, wl_ref, ws_ref,
                   gla_ref, lora_ref, q_ref, k_ref, v_ref):
    bb, tb, d = x_ref.shape
    x = x_ref[...].reshape(bb * tb, d)
    h = _rms_norm(x, g_ref[...]) * (1.0 + sc_ref[...]) + sh_ref[...]
    hb = h.astype(BF16)
    gla_ref[...] = _dot(hb, wg_ref[...]).reshape(gla_ref.shape)
    lora_ref[...] = _dot(hb, wl_ref[...]).reshape(lora_ref.shape)
    s = _dot(hb, ws_ref[...])
    q_ref[...] = s[:, :SWA_Q].reshape(q_ref.shape)
    k_ref[...] = s[:, SWA_Q:SWA_Q + SWA_KV].reshape(k_ref.shape)
    v_ref[...] = s[:, SWA_Q + SWA_KV:].reshape(v_ref.shape)


INPROJ_TILE = 512


def _inproj(x, g, sh, sc, w_gla, w_lora, w_swa):
    b, t, d = x.shape
    nmod = sh.shape[0]
    tb = min(t, INPROJ_TILE)
    bb = INPROJ_TILE // tb if nmod == 1 else 1
    mod_map = (lambda i, j: (i, 0, 0)) if nmod > 1 else (lambda i, j: (0, 0, 0))
    row = lambda i, j: (i, j, 0)
    full = lambda i, j: (0, 0)
    n_gla = w_gla.shape[1]
    n_lora = w_lora.shape[1]
    return pl.pallas_call(
        _inproj_kernel,
        out_shape=(jax.ShapeDtypeStruct((b, t, n_gla), F32),
                   jax.ShapeDtypeStruct((b, t, n_lora), F32),
                   jax.ShapeDtypeStruct((b, t, SWA_Q), F32),
                   jax.ShapeDtypeStruct((b, t, SWA_KV), F32),
                   jax.ShapeDtypeStruct((b, t, SWA_KV), F32)),
        grid=(b // bb, t // tb),
        in_specs=[pl.BlockSpec((bb, tb, d), row),
                  pl.BlockSpec((1, d), full),
                  pl.BlockSpec((None, 1, d), mod_map),
                  pl.BlockSpec((None, 1, d), mod_map),
                  pl.BlockSpec((d, n_gla), full),
                  pl.BlockSpec((d, n_lora), full),
                  pl.BlockSpec((d, w_swa.shape[1]), full)],
        out_specs=(pl.BlockSpec((bb, tb, n_gla), row),
                   pl.BlockSpec((bb, tb, n_lora), row),
                   pl.BlockSpec((bb, tb, SWA_Q), row),
                   pl.BlockSpec((bb, tb, SWA_KV), row),
                   pl.BlockSpec((bb, tb, SWA_KV), row)),
        compiler_params=pltpu.CompilerParams(dimension_semantics=("arbitrary", "arbitrary"),
                                             vmem_limit_bytes=VMEM_LIMIT),
        name="inproj",
    )(x, g, sh, sc, w_gla, w_lora, w_swa)


SCAN_UNROLL = 4
OUT_UNROLL = 4


def _log_sigmoid(x):
    return jnp.minimum(x, 0.0) - jnp.log(1.0 + jnp.exp(-jnp.abs(x)))


def _heads_to_rows(x):
    return jnp.concatenate([x[:, h * LANES:(h + 1) * LANES] for h in range(GLA_HEADS)], axis=0)


def _rows_to_heads(x, c):
    return jnp.concatenate([x[h * c:(h + 1) * c, :] for h in range(GLA_HEADS)], axis=1)


def _gla_kernel(has_init, q_ref, k_ref, v_ref, g_ref, lora_ref, waf_ref, baf_ref, wab_ref, bab_ref,
                ng_ref, *rest):
    if has_init:
        s0f_ref, s0b_ref, *rest = rest
    (out_ref, sf_ref, sb_ref, laf_ref, lab_ref, oacc_ref, qtf_ref, qtb_ref, saf_ref, sab_ref,
     stf_ref, stb_ref) = rest
    t = q_ref.shape[0]
    c = GLA_CHUNK
    n = t // c
    hc = GLA_HEADS * c

    lora = lora_ref[...].astype(BF16)
    laf_ref[...] = _log_sigmoid(_dot(lora, waf_ref[...]) + baf_ref[...]) * (1.0 / GLA_GATE_NORM)
    lab_ref[...] = _log_sigmoid(_dot(lora, wab_ref[...]) + bab_ref[...]) * (1.0 / GLA_GATE_NORM)

    if has_init:
        stf_ref[...] = s0f_ref[...].T
        stb_ref[...] = s0b_ref[...].T
    else:
        stf_ref[...] = jnp.zeros_like(stf_ref)
        stb_ref[...] = jnp.zeros_like(stb_ref)
    oacc_ref[...] = jnp.zeros_like(oacc_ref)

    r64 = lax.broadcasted_iota(jnp.int32, (c, c), 0)
    c64 = lax.broadcasted_iota(jnp.int32, (c, c), 1)
    tri_f = jnp.where(c64 <= r64, 1.0, 0.0).astype(BF16)
    tri_b = jnp.where(c64 >= r64, 1.0, 0.0).astype(BF16)
    rr = lax.broadcasted_iota(jnp.int32, (hc, hc), 0)
    cc = lax.broadcasted_iota(jnp.int32, (hc, hc), 1)
    same_head = (rr >> 6) == (cc >> 6)
    keep_f = same_head & ((rr & (c - 1)) >= (cc & (c - 1)))
    keep_b = same_head & ((rr & (c - 1)) <= (cc & (c - 1)))
    head_mask = jnp.where(same_head, 1.0, 0.0).astype(BF16)
    norm_g = ng_ref[...]

    def chunk_rows(ci):
        return pl.ds(pl.multiple_of(ci * c, c), c)

    def tile_heads(x):
        x4 = jnp.concatenate([x] * GLA_HEADS, axis=0)
        return jnp.where(same_head, x4, 0.0).astype(BF16)

    def scan_step(i, carry):
        dirs = []
        for u in range(SCAN_UNROLL):
            dirs += [(SCAN_UNROLL * i + u, laf_ref, tri_f, keep_f, c - 1, stf_ref, saf_ref, qtf_ref),
                     (n - 1 - SCAN_UNROLL * i - u, lab_ref, tri_b, keep_b, 0, stb_ref, sab_ref, qtb_ref)]
        cums = []
        for ci, la_ref, tri, _, _, _, _, _ in dirs:
            la_hi, la_lo = _split_hi_lo(la_ref[chunk_rows(ci), :])
            cums.append(_dot(tri, la_hi) + _dot(tri, la_lo))
        ops = []
        for (ci, _, _, _, last_row, _, _, qt_ref), cum in zip(dirs, cums):
            sl = chunk_rows(ci)
            tot = cum[last_row:last_row + 1, :]
            kc = k_ref[sl, :]
            qt = q_ref[sl, :] * (GLA_DK ** -0.5) * jnp.exp(cum)
            qt_ref[sl, :] = qt.astype(BF16)
            v_rows = _heads_to_rows(v_ref[sl, :])
            ops.append((tot, tile_heads(qt), tile_heads(kc * jnp.exp(-cum)),
                        tile_heads(kc * jnp.exp(tot - cum)), v_rows))
        atts = [_dot_nt(q4, k4) for _, q4, k4, _, _ in ops]
        incs = []
        for (_, _, _, keep, _, _, _, _), (_, _, _, kd4, v_rows), att in zip(dirs, ops, atts):
            att = jnp.where(keep, att, 0.0).astype(BF16)
            incs.append((_dot(att, v_rows.astype(BF16)), _dot(v_rows.T.astype(BF16), kd4)))
        for (ci, _, _, _, _, st_ref, snap_ref, _), (tot, _, _, _, _), (o_intra, st_inc) in zip(dirs, ops, incs):
            oacc_ref[ci] += o_intra
            st = st_ref[...]
            snap_ref[ci] = st.astype(BF16)
            st_ref[...] = jnp.exp(tot) * st + st_inc
        return carry

    def tile_heads_bf16(x):
        return jnp.concatenate([x] * GLA_HEADS, axis=0) * head_mask

    def out_step(i, carry):
        chunks = [OUT_UNROLL * i + u for u in range(OUT_UNROLL)]
        inter = []
        for ci in chunks:
            sl = chunk_rows(ci)
            q4 = jnp.concatenate([tile_heads_bf16(qtf_ref[sl, :]), tile_heads_bf16(qtb_ref[sl, :])], axis=1)
            st = jnp.concatenate([saf_ref[ci], sab_ref[ci]], axis=1)
            inter.append(_dot_nt(q4, st))
        for ci, o_inter in zip(chunks, inter):
            sl = chunk_rows(ci)
            on = _rms_norm(oacc_ref[ci] + o_inter, norm_g)
            gate = _silu(_heads_to_rows(g_ref[sl, :]))
            out_ref[sl, :] = _rows_to_heads(on * gate, c)
        return carry

    lax.fori_loop(0, n // SCAN_UNROLL, scan_step, 0)
    lax.fori_loop(0, n // OUT_UNROLL, out_step, 0)
    sf_ref[...] = stf_ref[...].T
    sb_ref[...] = stb_ref[...].T


def _gla(gla_in, lora, waf, baf, wab, bab, norm_g, s0f=None, s0b=None):
    b, t, _ = gla_in.shape
    has_init = s0f is not None
    n = t // GLA_CHUNK
    bmap = lambda i: (i, 0, 0)
    full = lambda i: (0, 0)
    in_specs = [pl.BlockSpec((None, t, GLA_QK), lambda i: (i, 0, 0)),
                pl.BlockSpec((None, t, GLA_QK), lambda i: (i, 0, 1)),
                pl.BlockSpec((None, t, GLA_V), lambda i: (i, 0, 1)),
                pl.BlockSpec((None, t, GLA_V), lambda i: (i, 0, 2)),
                pl.BlockSpec((None, t, 2 * GLA_LORA), bmap),
                pl.BlockSpec((2 * GLA_LORA, GLA_QK), full),
                pl.BlockSpec((1, GLA_QK), full),
                pl.BlockSpec((2 * GLA_LORA, GLA_QK), full),
                pl.BlockSpec((1, GLA_QK), full),
                pl.BlockSpec((1, GLA_DV), full)]
    args = [gla_in, gla_in, gla_in, gla_in, lora, waf, baf, wab, bab, norm_g]
    if has_init:
        in_specs += [pl.BlockSpec((None, GLA_QK, GLA_DV), bmap)] * 2
        args += [s0f, s0b]
    return pl.pallas_call(
        functools.partial(_gla_kernel, has_init),
        out_shape=(jax.ShapeDtypeStruct((b, t, GLA_V), F32),
                   jax.ShapeDtypeStruct((b, GLA_QK, GLA_DV), F32),
                   jax.ShapeDtypeStruct((b, GLA_QK, GLA_DV), F32)),
        grid=(b,),
        in_specs=in_specs,
        out_specs=(pl.BlockSpec((None, t, GLA_V), bmap),
                   pl.BlockSpec((None, GLA_QK, GLA_DV), bmap),
                   pl.BlockSpec((None, GLA_QK, GLA_DV), bmap)),
        scratch_shapes=[pltpu.VMEM((t, GLA_QK), F32),
                        pltpu.VMEM((t, GLA_QK), F32),
                        pltpu.VMEM((n, GLA_HEADS * GLA_CHUNK, GLA_DV), F32),
                        pltpu.VMEM((t, GLA_QK), BF16),
                        pltpu.VMEM((t, GLA_QK), BF16),
                        pltpu.VMEM((n, GLA_DV, GLA_QK), BF16),
                        pltpu.VMEM((n, GLA_DV, GLA_QK), BF16),
                        pltpu.VMEM((GLA_DV, GLA_QK), F32),
                        pltpu.VMEM((GLA_DV, GLA_QK), F32)],
        compiler_params=pltpu.CompilerParams(dimension_semantics=("arbitrary",),
                                             vmem_limit_bytes=VMEM_LIMIT),
        name="gla",
    )(*args)


def _dup_groups(x):
    lo = lax.broadcasted_iota(jnp.int32, x.shape, 1) < SWA_HEAD_DIM
    xr = pltpu.roll(x, SWA_HEAD_DIM, axis=1)
    return jnp.where(lo, x, xr), jnp.where(lo, xr, x)


def _pairs_attention(qps, sinks, k_dups, vt_dups, masks):
    nq = qps[0].shape[0]
    lo = lax.broadcasted_iota(jnp.int32, (nq, LANES), 1) < SWA_HEAD_DIM
    even = lax.broadcasted_iota(jnp.int32, (1, 2 * nq), 1) < nq
    scores = []
    for qp, k_dup in zip(qps, k_dups):
        q2 = jnp.concatenate([jnp.where(lo, qp, 0.0), jnp.where(lo, 0.0, qp)], axis=0).astype(BF16)
        scores.append(_dot_nt(k_dup, q2))
    probs = []
    for s, (sink_even, sink_odd), mask in zip(scores, sinks, masks):
        if mask is not None:
            s = jnp.where(mask, s, NEG_INF)
        sink = jnp.where(even, sink_even, sink_odd)
        m = jnp.maximum(jnp.max(s, axis=0, keepdims=True), sink)
        p = jnp.exp(s - m)
        denom = jnp.sum(p, axis=0, keepdims=True) + jnp.exp(sink - m)
        probs.append((p.astype(BF16), 1.0 / denom))
    outs = []
    for (p, rdenom), vt_dup in zip(probs, vt_dups):
        o = _dot(vt_dup, p) * rdenom
        outs.append(jnp.concatenate([o[:SWA_HEAD_DIM, :nq], o[SWA_HEAD_DIM:, nq:]], axis=0).T)
    return outs


CTX_BATCH = 4


def _attn_ctx_kernel(sink_ref, q_ref, k_ref, v_ref, o_ref):
    scale = SWA_HEAD_DIM ** -0.5
    pairs = range(SWA_HEADS // 2)
    items = [(bb, pr) for bb in range(q_ref.shape[0]) for pr in pairs]
    kd = [[x.astype(BF16) for x in _dup_groups(k_ref[bb])] for bb in range(q_ref.shape[0])]
    vt = [[x.T.astype(BF16) for x in _dup_groups(v_ref[bb])] for bb in range(q_ref.shape[0])]
    outs = _pairs_attention([q_ref[bb, :, pr * LANES:(pr + 1) * LANES] * scale for bb, pr in items],
                            [(sink_ref[2 * pr], sink_ref[2 * pr + 1]) for _, pr in items],
                            [kd[bb][pr // 2] for bb, pr in items], [vt[bb][pr // 2] for bb, pr in items],
                            [None] * len(items))
    for (bb, pr), out in zip(items, outs):
        o_ref[bb, :, pr * LANES:(pr + 1) * LANES] = out


def _attn_ctx(sink, q, k, v):
    b, t, _ = q.shape
    bmap = lambda i: (i, 0, 0)
    return pl.pallas_call(
        _attn_ctx_kernel,
        out_shape=jax.ShapeDtypeStruct((b, t, SWA_Q), F32),
        grid=(b // CTX_BATCH,),
        in_specs=[pl.BlockSpec(memory_space=pltpu.SMEM),
                  pl.BlockSpec((CTX_BATCH, t, SWA_Q), bmap),
                  pl.BlockSpec((CTX_BATCH, t, SWA_KV), bmap),
                  pl.BlockSpec((CTX_BATCH, t, SWA_KV), bmap)],
        out_specs=pl.BlockSpec((CTX_BATCH, t, SWA_Q), bmap),
        compiler_params=pltpu.CompilerParams(dimension_semantics=("arbitrary",),
                                             vmem_limit_bytes=VMEM_LIMIT),
        name="attn_ctx",
    )(sink, q, k, v)


LAT_BLOCKS = 2


def _rope(x, cos, sin_lo, sin_hi):
    return x * cos + pltpu.roll(x, LANES - 16, axis=1) * sin_lo + pltpu.roll(x, 16, axis=1) * sin_hi


def _attn_lat_kernel(sink_ref, q_ref, k_ref, v_ref, kc_ref, vc_ref, cos_ref, sl_ref, sh_ref,
                     o_ref, kw_ref, vw_ref):
    t = q_ref.shape[0]
    ab = ATTN_BLOCK
    nb = t // ab
    scale = SWA_HEAD_DIM ** -0.5

    k_rot = _dup_groups(_rope(k_ref[...], cos_ref[...], sl_ref[...], sh_ref[...]))
    v_dup = _dup_groups(v_ref[...])
    zeros = jnp.zeros((ab, LANES), BF16)
    for grp in range(SWA_KV_HEADS):
        kw_ref[grp, 0:ab, :] = zeros
        kw_ref[grp, ab:ab + t, :] = k_rot[grp].astype(BF16)
        kw_ref[grp, ab + t:, :] = zeros
        vw_ref[grp, 0] = zeros
        for blk in range(nb):
            vw_ref[grp, blk + 1] = v_dup[grp][blk * ab:(blk + 1) * ab, :].T.astype(BF16)
        vw_ref[grp, nb + 1] = zeros
    kc = [x.astype(BF16) for x in _dup_groups(kc_ref[...])]
    vct = [x.T.astype(BF16) for x in _dup_groups(vc_ref[...])]
    lc = kc_ref.shape[0]

    key = lax.broadcasted_iota(jnp.int32, (lc + 3 * ab, 2 * ab), 0) - lc
    tq = lax.broadcasted_iota(jnp.int32, (lc + 3 * ab, 2 * ab), 1) & (ab - 1)
    band = (key < 0) | (jnp.abs(tq + ab - key) <= ab)

    def block(it, carry):
        pairs = range(SWA_HEADS // 2)
        qps, sinks, k_dups, vt_dups, masks, places = [], [], [], [], [], []
        for u in range(LAT_BLOCKS):
            nq = it * LAT_BLOCKS + u
            row0 = pl.multiple_of(nq * ab, ab)
            s_abs = key + (nq - 1) * ab
            mask = band & ((key < 0) | ((s_abs >= 0) & (s_abs < t)))
            cos = cos_ref[pl.ds(row0, ab), :]
            s_lo = sl_ref[pl.ds(row0, ab), :]
            s_hi = sh_ref[pl.ds(row0, ab), :]
            k_all = [jnp.concatenate([kc[grp], kw_ref[grp, pl.ds(row0, 3 * ab), :]], axis=0)
                     for grp in range(SWA_KV_HEADS)]
            vt_all = [jnp.concatenate([vct[grp], vw_ref[grp, nq], vw_ref[grp, nq + 1], vw_ref[grp, nq + 2]],
                                      axis=1) for grp in range(SWA_KV_HEADS)]
            for pr in pairs:
                qps.append(_rope(q_ref[pl.ds(row0, ab), pr * LANES:(pr + 1) * LANES], cos, s_lo, s_hi) * scale)
                sinks.append((sink_ref[2 * pr], sink_ref[2 * pr + 1]))
                k_dups.append(k_all[pr // 2])
                vt_dups.append(vt_all[pr // 2])
                masks.append(mask)
                places.append((row0, pr))
        outs = _pairs_attention(qps, sinks, k_dups, vt_dups, masks)
        for (row0, pr), out in zip(places, outs):
            o_ref[pl.ds(row0, ab), pr * LANES:(pr + 1) * LANES] = out
        return carry

    lax.fori_loop(0, nb // LAT_BLOCKS, block, 0)


def _attn_lat(sink, q, k, v, kc, vc, cos, sin_lo, sin_hi):
    b, t, _ = q.shape
    lc = kc.shape[1]
    bmap = lambda i: (i, 0, 0)
    full = lambda i: (0, 0)
    return pl.pallas_call(
        _attn_lat_kernel,
        out_shape=jax.ShapeDtypeStruct((b, t, SWA_Q), F32),
        grid=(b,),
        in_specs=[pl.BlockSpec(memory_space=pltpu.SMEM),
                  pl.BlockSpec((None, t, SWA_Q), bmap),
                  pl.BlockSpec((None, t, SWA_KV), bmap),
                  pl.BlockSpec((None, t, SWA_KV), bmap),
                  pl.BlockSpec((None, lc, SWA_KV), bmap),
                  pl.BlockSpec((None, lc, SWA_KV), bmap),
                  pl.BlockSpec((t, LANES), full),
                  pl.BlockSpec((t, LANES), full),
                  pl.BlockSpec((t, LANES), full)],
        out_specs=pl.BlockSpec((None, t, SWA_Q), bmap),
        scratch_shapes=[pltpu.VMEM((SWA_KV_HEADS, t + 2 * ATTN_BLOCK, LANES), BF16),
                        pltpu.VMEM((SWA_KV_HEADS, t // ATTN_BLOCK + 2, LANES, ATTN_BLOCK), BF16)],
        compiler_params=pltpu.CompilerParams(dimension_semantics=("arbitrary",),
                                             vmem_limit_bytes=VMEM_LIMIT),
        name="attn_lat",
    )(sink, q, k, v, kc, vc, cos, sin_lo, sin_hi)


def _rope_tables(t):
    half = SWA_HEAD_DIM // 2
    quarter = half // 2
    pos = jnp.arange(t)
    row = (pos // GRID_W).astype(F32)
    col = (pos % GRID_W).astype(F32)
    inv_freq = ROPE_BASE ** (-jnp.arange(quarter, dtype=F32) / quarter)
    lane = jnp.arange(LANES)
    d = lane % SWA_HEAD_DIM
    freq = inv_freq[d % quarter]
    use_row = (d < half)
    ang = jnp.where(use_row[None, :], row[:, None], col[:, None]) * freq[None, :]
    cos = jnp.cos(ang)
    sin = jnp.sin(ang)
    lower = (d % half) < quarter
    return cos, jnp.where(lower[None, :], -sin, 0.0), jnp.where(lower[None, :], 0.0, sin)


def _route(sel, scores):
    n = sel.shape[1]
    gsz = N_EXPERTS // N_EXPERT_GROUPS

    def first_max(x, idx, size):
        m = jnp.max(x, axis=0, keepdims=True)
        first = jnp.min(jnp.where(x == m, idx, float(size)), axis=0, keepdims=True)
        return m, idx == first

    i8 = lax.broadcasted_iota(jnp.int32, (gsz, n), 0).astype(F32)
    rows = []
    for g in range(N_EXPERT_GROUPS):
        slab = sel[g * gsz:(g + 1) * gsz, :]
        m1, hit = first_max(slab, i8, gsz)
        m2 = jnp.max(jnp.where(hit, NEG_INF, slab), axis=0, keepdims=True)
        rows.append(m1 + m2)
    gscore = jnp.concatenate(rows, axis=0)
    gsel = jnp.zeros((N_EXPERT_GROUPS, n), F32)
    for _ in range(TOPK_GROUPS):
        _, hit = first_max(gscore, i8, N_EXPERT_GROUPS)
        gsel = jnp.where(hit, 1.0, gsel)
        gscore = jnp.where(hit, NEG_INF, gscore)
    emask = jnp.concatenate(
        [jnp.broadcast_to(gsel[g:g + 1, :], (gsz, n)) for g in range(N_EXPERT_GROUPS)], axis=0)
    cand = jnp.where(emask > 0.5, sel, NEG_INF)
    ie = lax.broadcasted_iota(jnp.int32, (N_EXPERTS, n), 0).astype(F32)
    w = jnp.zeros((N_EXPERTS, n), F32)
    chosen = jnp.zeros((N_EXPERTS, n), F32)
    hits = []
    for _ in range(TOP_K):
        _, hit = first_max(cand, ie, N_EXPERTS)
        hits.append(hit)
        w = jnp.where(hit, scores, w)
        chosen = jnp.where(hit, 1.0, chosen)
        cand = jnp.where(hit, NEG_INF, cand)
    gates = w / jnp.sum(w, axis=0, keepdims=True) * ROUTED_SCALE

    s_idx = lax.broadcasted_iota(jnp.int32, (n, n), 0)
    t_idx = lax.broadcasted_iota(jnp.int32, (n, n), 1)
    tile_shift = MOE_TILE.bit_length() - 1
    before = jnp.where((s_idx < t_idx) & ((s_idx >> tile_shift) == (t_idx >> tile_shift)), 1.0, 0.0)
    rank = _dot(chosen.astype(BF16), before.astype(BF16))
    e_row = lax.broadcasted_iota(jnp.int32, (N_EXPERTS, N_EXPERTS), 0)
    e_col = lax.broadcasted_iota(jnp.int32, (N_EXPERTS, N_EXPERTS), 1)
    below = jnp.where(e_col < e_row, 1.0, 0.0).astype(BF16)
    lane_tile = lax.broadcasted_iota(jnp.int32, (1, n), 1) >> tile_shift
    sizes, starts = [], []
    first_row = jnp.zeros((N_EXPERTS, n), F32)
    for ti in range(n // MOE_TILE):
        count = jnp.sum(chosen[:, ti * MOE_TILE:(ti + 1) * MOE_TILE], axis=1, keepdims=True)
        padded = jnp.floor((count + (SORT_ALIGN - 1)) * (1.0 / SORT_ALIGN)) * SORT_ALIGN
        padded = jnp.broadcast_to(padded, (N_EXPERTS, LANES))
        start = _dot(below, padded.astype(BF16))
        first_row = jnp.where(lane_tile == ti, start[:, 0:1], first_row)
        sizes.append(padded)
        starts.append(start)
    row = first_row + rank
    pos = jnp.concatenate([jnp.sum(jnp.where(h, row, 0.0), axis=0, keepdims=True) for h in hits], axis=0)
    wts = jnp.concatenate([jnp.sum(jnp.where(h, gates, 0.0), axis=0, keepdims=True) for h in hits], axis=0)
    return pos, wts, sizes, starts


def _outproj_kernel(gla_ref, att_ref, x_ref, wo_ref, g1_ref, sh_ref, sc_ref, ng_ref, rw_ref, rwh_ref,
                    rb_ref, x1_ref, xm_ref, pos_ref, wts_ref, cnt_ref, start_ref):
    bb, tb, d = x_ref.shape
    tm = bb * tb
    y = (_dot(gla_ref[...].reshape(tm, GLA_V).astype(BF16), wo_ref[0:GLA_V, :])
         + _dot(att_ref[...].reshape(tm, SWA_Q).astype(BF16), wo_ref[GLA_V:, :]))
    x1 = x_ref[...].reshape(tm, d) + g1_ref[...] * y
    x1_ref[...] = x1.reshape(bb, tb, d)
    xm = _rms_norm(x1, ng_ref[...]) * (1.0 + sc_ref[...]) + sh_ref[...]
    xm_hi, xm_lo = _split_hi_lo(xm)
    xm_ref[...] = xm_hi.reshape(bb, tb, d)
    lg = _dot(xm_hi, rw_ref[...])
    logits = lg[:, :N_EXPERTS] + lg[:, N_EXPERTS:] + _dot(xm_lo, rwh_ref[...])
    lt = jnp.concatenate([logits, jnp.zeros((tm, LANES - N_EXPERTS), F32)], axis=1).T[:N_EXPERTS, :]
    scores = _sigmoid(lt)
    pos, wts, sizes, starts = _route(scores + rb_ref[...], scores)
    tiles_per_batch = tb // MOE_TILE
    for ti in range(tm // MOE_TILE):
        at = (ti // tiles_per_batch, ti % tiles_per_batch)
        pos_ref[at] = pos[:, ti * MOE_TILE:(ti + 1) * MOE_TILE]
        wts_ref[at] = wts[:, ti * MOE_TILE:(ti + 1) * MOE_TILE]
        cnt_ref[at] = sizes[ti]
        start_ref[at] = starts[ti]


OUTPROJ_TILE = 512


def _outproj(gla_out, att_out, x, w_out, g1, sh2, sc2, norm_g, rw_cat, rw_hi, rbias):
    b, t, d = x.shape
    nmod = g1.shape[0]
    tb = min(t, OUTPROJ_TILE)
    bb = OUTPROJ_TILE // tb if nmod == 1 else 1
    tpb = tb // MOE_TILE
    mod_map = (lambda i, j: (i, 0, 0)) if nmod > 1 else (lambda i, j: (0, 0, 0))
    row = lambda i, j: (i, j, 0)
    full = lambda i, j: (0, 0)
    tile = lambda i, j: (i, j, 0, 0)
    nt = t // MOE_TILE
    return pl.pallas_call(
        _outproj_kernel,
        out_shape=(jax.ShapeDtypeStruct((b, t, d), F32),
                   jax.ShapeDtypeStruct((b, t, d), BF16),
                   jax.ShapeDtypeStruct((b, nt, TOP_K, MOE_TILE), F32),
                   jax.ShapeDtypeStruct((b, nt, TOP_K, MOE_TILE), F32),
                   jax.ShapeDtypeStruct((b, nt, N_EXPERTS, LANES), F32),
                   jax.ShapeDtypeStruct((b, nt, N_EXPERTS, LANES), F32)),
        grid=(b // bb, t // tb),
        in_specs=[pl.BlockSpec((bb, tb, GLA_V), row),
                  pl.BlockSpec((bb, tb, SWA_Q), row),
                  pl.BlockSpec((bb, tb, d), row),
                  pl.BlockSpec((d, d), full),
                  pl.BlockSpec((None, 1, d), mod_map),
                  pl.BlockSpec((None, 1, d), mod_map),
                  pl.BlockSpec((None, 1, d), mod_map),
                  pl.BlockSpec((1, d), full),
                  pl.BlockSpec((d, 2 * N_EXPERTS), full),
                  pl.BlockSpec((d, N_EXPERTS), full),
                  pl.BlockSpec((N_EXPERTS, 1), full)],
        out_specs=(pl.BlockSpec((bb, tb, d), row),
                   pl.BlockSpec((bb, tb, d), row),
                   pl.BlockSpec((bb, tpb, TOP_K, MOE_TILE), tile),
                   pl.BlockSpec((bb, tpb, TOP_K, MOE_TILE), tile),
                   pl.BlockSpec((bb, tpb, N_EXPERTS, LANES), tile),
                   pl.BlockSpec((bb, tpb, N_EXPERTS, LANES), tile)),
        compiler_params=pltpu.CompilerParams(dimension_semantics=("arbitrary", "arbitrary"),
                                             vmem_limit_bytes=VMEM_LIMIT),
        name="outproj",
    )(gla_out, att_out, x, w_out, g1, sh2, sc2, norm_g, rw_cat, rw_hi, rbias)


MOE_TILE = 256
SORT_ALIGN = 16
SORT_ROWS = 3072
ROW_TILE = 512
GATHER_SLOTS = 3
FFN_CHAINS = 4
COMBINE_CHUNK = 1024
ALWAYS_ROWS = 2560
COMBINE_TAIL = 512


def _moe_sort_kernel(tiles_a, used_ref, xa_ref, xb_ref, pos_ref, xs_ref):
    i = pl.program_id(0)
    x = jnp.where(i < tiles_a, xa_ref[...], xb_ref[...])
    pos = pos_ref[...]
    tm = x.shape[0]
    used = used_ref[i]

    def fill(blk):
        rows = (lax.broadcasted_iota(jnp.int32, (tm, tm), 0) + blk * tm).astype(F32)
        onehot = jnp.zeros((tm, tm), F32)
        for k in range(TOP_K):
            onehot = jnp.where(rows == pos[k:k + 1, :], 1.0, onehot)
        xs_ref[blk * tm:(blk + 1) * tm, :] = _dot(onehot.astype(BF16), x).astype(BF16)

    for blk in range(SORT_ROWS // tm):
        if (blk + 1) * tm <= ALWAYS_ROWS:
            fill(blk)
        else:
            pl.when(blk * tm < used)(functools.partial(fill, blk))

            @pl.when(blk * tm >= used)
            def _():
                xs_ref[blk * tm:(blk + 1) * tm, :] = jnp.zeros((tm, D_MODEL), BF16)


def _moe_sort(xm_a, xm_b, pos, used):
    d = xm_a.shape[1]
    nt, _, tm = pos.shape
    tiles_a = xm_a.shape[0] // tm
    grid_spec = pltpu.PrefetchScalarGridSpec(
        num_scalar_prefetch=1,
        grid=(nt,),
        in_specs=[pl.BlockSpec((tm, d), lambda i, u: (jnp.minimum(i, tiles_a - 1), 0)),
                  pl.BlockSpec((tm, d), lambda i, u: (jnp.maximum(i - tiles_a, 0), 0)),
                  pl.BlockSpec((None, TOP_K, tm), lambda i, u: (i, 0, 0))],
        out_specs=pl.BlockSpec((SORT_ROWS, d), lambda i, u: (i, 0)))
    return pl.pallas_call(
        functools.partial(_moe_sort_kernel, tiles_a),
        out_shape=jax.ShapeDtypeStruct((nt * SORT_ROWS, d), BF16),
        grid_spec=grid_spec,
        compiler_params=pltpu.CompilerParams(dimension_semantics=("arbitrary",),
                                             vmem_limit_bytes=VMEM_LIMIT),
        name="moe_sort",
    )(used, xm_a, xm_b, pos)


def _moe_row_tiles(n_tokens):
    rows = n_tokens * TOP_K + (n_tokens // MOE_TILE) * N_EXPERTS * (SORT_ALIGN - 1) + N_EXPERTS * (ROW_TILE - 1)
    return -(-rows // ROW_TILE) + GATHER_SLOTS - 1


PLAN_CHUNK = 1280


def _int_dot_r(a, onehot):
    hi = jnp.floor(a * (1.0 / 256.0))
    return _dot(hi.astype(BF16), onehot) * 256.0 + _dot((a - hi * 256.0).astype(BF16), onehot)


def _int_dot_l(onehot, b):
    hi = jnp.floor(b * (1.0 / 256.0))
    return _dot(onehot, hi.astype(BF16)) * 256.0 + _dot(onehot, (b - hi * 256.0).astype(BF16))


def _moe_plan_kernel(cnt_ref, start_ref, src_ref, first_ref, tiles_ref, nu_ref, back_ref):
    nt, ne = cnt_ref.shape
    gpt = SORT_ROWS // SORT_ALIGN
    gpr = ROW_TILE // SORT_ALIGN
    gc = cnt_ref[...] * (1.0 / SORT_ALIGN)
    ls = start_ref[...] * (1.0 / SORT_ALIGN)

    def transpose(x):
        x = jnp.concatenate([x, jnp.zeros((nt, LANES - ne), F32)], axis=1)
        x = jnp.concatenate([x, jnp.zeros((LANES - nt, LANES), F32)], axis=0)
        return x.T[:ne, :nt]

    def tri(n, keep):
        return jnp.where(keep(lax.broadcasted_iota(jnp.int32, (n, n), 0),
                              lax.broadcasted_iota(jnp.int32, (n, n), 1)), 1.0, 0.0).astype(BF16)

    gc_t = transpose(gc)
    ls_t = transpose(ls)
    tot_c = jnp.broadcast_to(jnp.sum(gc_t, axis=1, keepdims=True), (ne, LANES))
    ptot_c = jnp.floor((tot_c + (gpr - 1)) * (1.0 / gpr)) * gpr
    gend_c = _int_dot_l(tri(ne, lambda r, c: c <= r), ptot_c)
    gstart_c = gend_c - ptot_c
    n_used = gend_c[ne - 1:ne, :] * (1.0 / gpr)
    nu_ref[...] = n_used.astype(jnp.int32)
    tot_r = jnp.sum(gc, axis=0, keepdims=True)
    ptot_r = jnp.floor((tot_r + (gpr - 1)) * (1.0 / gpr)) * gpr
    gstart_r = _int_dot_r(jnp.broadcast_to(ptot_r, (8, ne)), tri(ne, lambda r, c: r < c))
    cumex = _dot(tri(nt, lambda r, c: c < r), gc.astype(BF16))
    cumex_t = _dot(gc_t.astype(BF16), tri(nt, lambda r, c: r < c))
    tile_base = lax.broadcasted_iota(jnp.int32, (nt, ne), 0).astype(F32) * gpt + ls
    table = jnp.concatenate([cumex + gc, cumex, tile_base, gstart_r, jnp.broadcast_to(tot_r, (8, ne))], axis=0)

    e_iota = lax.broadcasted_iota(jnp.int32, (ne, PLAN_CHUNK), 0).astype(F32)
    for ch in range(src_ref.shape[1] // PLAN_CHUNK):
        g = (lax.broadcasted_iota(jnp.int32, (1, PLAN_CHUNK), 1) + ch * PLAN_CHUNK).astype(F32)
        eg = jnp.sum(jnp.where(gend_c[:, 0:1] <= g, 1.0, 0.0), axis=0, keepdims=True)
        picked = _int_dot_r(table, jnp.where(e_iota == eg, 1.0, 0.0).astype(BF16))
        cum_g, cumex_g, base_g = picked[0:nt], picked[nt:2 * nt], picked[2 * nt:3 * nt]
        u = g - picked[3 * nt:3 * nt + 1]
        in_tile = (cumex_g <= u) & (u < cum_g)
        src = jnp.sum(jnp.where(in_tile, base_g - cumex_g, 0.0), axis=0, keepdims=True) + u
        src = jnp.where(u < picked[3 * nt + 8:3 * nt + 9], src, gpt - 1.0)
        src_ref[:, ch * PLAN_CHUNK:(ch + 1) * PLAN_CHUNK] = src.astype(jnp.int32)

    first_ref[...] = (gstart_c * (1.0 / gpr)).astype(jnp.int32)
    tiles_ref[...] = (ptot_c * (1.0 / gpr)).astype(jnp.int32)

    lg = lax.broadcasted_iota(jnp.int32, (ne, back_ref.shape[1]), 1).astype(F32)
    for t in range(nt):
        first = ls_t[:, t:t + 1]
        inside = (first <= lg) & (lg < first + gc_t[:, t:t + 1])
        shift = gstart_c[:, 0:1] + cumex_t[:, t:t + 1] - first
        val = jnp.sum(jnp.where(inside, shift + lg, 0.0), axis=0, keepdims=True)
        back_ref[t:t + 1, :] = val.astype(jnp.int32)


def _moe_plan(cnt, start):
    nt, ne = cnt.shape
    row_tiles = _moe_row_tiles(nt * MOE_TILE)
    gpt = SORT_ROWS // SORT_ALIGN
    gpr = ROW_TILE // SORT_ALIGN
    n_src = -(-(row_tiles * gpr) // PLAN_CHUNK) * PLAN_CHUNK
    n_back = -(-gpt // LANES) * LANES
    src, first, tiles, nu, back = pl.pallas_call(
        _moe_plan_kernel,
        out_shape=(jax.ShapeDtypeStruct((1, n_src), jnp.int32),
                   jax.ShapeDtypeStruct((ne, LANES), jnp.int32),
                   jax.ShapeDtypeStruct((ne, LANES), jnp.int32),
                   jax.ShapeDtypeStruct((1, LANES), jnp.int32),
                   jax.ShapeDtypeStruct((nt, n_back), jnp.int32)),
        compiler_params=pltpu.CompilerParams(vmem_limit_bytes=VMEM_LIMIT),
        name="moe_plan",
    )(cnt, start)
    return nu[0, :1], first[:, 0], tiles[:, 0], src[0, :row_tiles * gpr], back[:, :gpt]


def _moe_experts_kernel(nu_ref, first_ref, tiles_ref, src_ref, xs_hbm, wg_ref, wu_ref, wd_ref, ys_hbm,
                        xbuf, ybuf, gsem, osem, wgu_s, wd_s):
    e = pl.program_id(0)
    n_used = nu_ref[0]
    gpr = ROW_TILE // SORT_ALIGN
    part = ROW_TILE // FFN_CHAINS

    def gather(tile, to_slot, j0=0, j1=gpr):
        for j in range(j0, j1):
            row = pl.multiple_of(src_ref[tile * gpr + j] * SORT_ALIGN, SORT_ALIGN)
            pltpu.make_async_copy(xs_hbm.at[pl.ds(row, SORT_ALIGN), :],
                                  xbuf.at[to_slot, j * SORT_ALIGN:(j + 1) * SORT_ALIGN, :],
                                  gsem.at[to_slot]).start(priority=j % 2)

    def drain(of_slot):
        for j in range(gpr):
            pltpu.make_async_copy(xs_hbm.at[0:SORT_ALIGN, :],
                                  xbuf.at[of_slot, j * SORT_ALIGN:(j + 1) * SORT_ALIGN, :], gsem.at[of_slot]).wait()

    def out_copy(tile, of_slot):
        row = pl.multiple_of(tile * ROW_TILE, ROW_TILE)
        return pltpu.make_async_copy(ybuf.at[of_slot], ys_hbm.at[pl.ds(row, ROW_TILE), :], osem.at[of_slot])

    @pl.when(e == 0)
    def _():
        gather(0, 0)
        gather(1, 1)

    wgu_s[:, :EXPERT_FF] = wg_ref[...].astype(BF16)
    wgu_s[:, EXPERT_FF:] = wu_ref[...].astype(BF16)
    wd_s[...] = wd_ref[...].astype(BF16)

    def row_tile(i, carry):
        r = first_ref[e] + i
        slot = lax.rem(r, GATHER_SLOTS)
        oslot = lax.rem(r, 2)
        next_slot = lax.rem(r + 2, GATHER_SLOTS)
        drain(slot)

        @pl.when(r >= 2)
        def _():
            out_copy(r - 2, oslot).wait()

        abs_ = []
        for c in range(FFN_CHAINS):
            abs_.append(_dot(xbuf[slot, c * part:(c + 1) * part, :], wgu_s[...]))
            gather(r + 2, next_slot, c * gpr // FFN_CHAINS, (c + 1) * gpr // FFN_CHAINS)
        hs = [(_silu(ab[:, :EXPERT_FF]) * ab[:, EXPERT_FF:]).astype(BF16) for ab in abs_]
        ys = [_dot(h, wd_s[...]).astype(BF16) for h in hs]
        for c in range(FFN_CHAINS):
            ybuf[oslot, c * part:(c + 1) * part, :] = ys[c]
        out_copy(r, oslot).start()
        return carry

    lax.fori_loop(0, tiles_ref[e], row_tile, 0)

    @pl.when(e == pl.num_programs(0) - 1)
    def _():
        drain(lax.rem(n_used, GATHER_SLOTS))
        drain(lax.rem(n_used + 1, GATHER_SLOTS))
        out_copy(n_used - 1, lax.rem(n_used - 1, 2)).wait()

        @pl.when(n_used >= 2)
        def _():
            out_copy(n_used - 2, lax.rem(n_used, 2)).wait()


def _moe_experts(n_used, first, tiles, src, xs, wg, wu, wd, row_tiles):
    d = xs.shape[-1]
    ne = wg.shape[0]
    w_map = lambda e, nu, fi, ti, sr: (e, 0, 0)
    grid_spec = pltpu.PrefetchScalarGridSpec(
        num_scalar_prefetch=4,
        grid=(ne,),
        in_specs=[pl.BlockSpec(memory_space=pl.ANY),
                  pl.BlockSpec((None, d, EXPERT_FF), w_map),
                  pl.BlockSpec((None, d, EXPERT_FF), w_map),
                  pl.BlockSpec((None, EXPERT_FF, d), w_map)],
        out_specs=pl.BlockSpec(memory_space=pl.ANY),
        scratch_shapes=[pltpu.VMEM((GATHER_SLOTS, ROW_TILE, d), BF16),
                        pltpu.VMEM((2, ROW_TILE, d), BF16),
                        pltpu.SemaphoreType.DMA((GATHER_SLOTS,)),
                        pltpu.SemaphoreType.DMA((2,)),
                        pltpu.VMEM((d, 2 * EXPERT_FF), BF16),
                        pltpu.VMEM((EXPERT_FF, d), BF16)])
    return pl.pallas_call(
        _moe_experts_kernel,
        out_shape=jax.ShapeDtypeStruct((row_tiles * ROW_TILE, d), BF16),
        grid_spec=grid_spec,
        compiler_params=pltpu.CompilerParams(dimension_semantics=("arbitrary",),
                                             vmem_limit_bytes=VMEM_LIMIT),
        name="moe_experts",
    )(n_used, first, tiles, src, xs, wg, wu, wd)


def _moe_combine_kernel(back_ref, used_ref, ys_hbm, pos_ref, wts_ref, xm_ref, x1_ref, g2_ref, fg_ref,
                        swg_ref, swu_ref, swd_ref, o_ref, buf, sem, acc_ref):
    i = pl.program_id(0)
    gpt = SORT_ROWS // SORT_ALIGN
    slot = lax.rem(i, 2)
    always = ALWAYS_ROWS
    tail = range(always, SORT_ROWS, COMBINE_TAIL)

    def copies(tile, of_slot, g0, g1, start):
        for g in range(g0, g1):
            row = pl.multiple_of(back_ref[tile * gpt + g] * SORT_ALIGN, SORT_ALIGN) if start else 0
            cp = pltpu.make_async_copy(ys_hbm.at[pl.ds(row, SORT_ALIGN), :],
                                       buf.at[of_slot, g * SORT_ALIGN:(g + 1) * SORT_ALIGN, :], sem.at[of_slot])
            if start:
                cp.start(priority=g % 2)
            else:
                cp.wait()

    def transfer(tile, of_slot, start):
        copies(tile, of_slot, 0, always // SORT_ALIGN, start)
        for c0 in tail:
            pl.when(c0 < used_ref[tile])(functools.partial(
                copies, tile, of_slot, c0 // SORT_ALIGN, (c0 + COMBINE_TAIL) // SORT_ALIGN, start))

    @pl.when(i == 0)
    def _():
        transfer(0, 0, True)

    @pl.when(i + 1 < pl.num_programs(0))
    def _():
        transfer(i + 1, 1 - slot, True)

    x = xm_ref[...]
    tm = x.shape[0]
    pad = jnp.zeros((LANES - TOP_K, tm), F32)
    pos_t = jnp.concatenate([pos_ref[...], pad], axis=0).T
    wts_t = jnp.concatenate([wts_ref[...], pad], axis=0).T
    pos_b = [jnp.broadcast_to(pos_t[:, k:k + 1], (tm, LANES)) for k in range(TOP_K)]
    wts_b = [jnp.broadcast_to(wts_t[:, k:k + 1], (tm, LANES)) for k in range(TOP_K)]
    shared = _dot((_silu(_dot(x, swg_ref[...])) * _dot(x, swu_ref[...])).astype(BF16), swd_ref[...])
    transfer(i, slot, False)

    def apply(c0, width):
        reps = width // LANES
        rows = (lax.broadcasted_iota(jnp.int32, (tm, width), 1) + c0).astype(F32)
        comb = jnp.zeros((tm, width), F32)
        for k in range(TOP_K):
            comb = jnp.where(rows == jnp.concatenate([pos_b[k]] * reps, axis=1),
                             jnp.concatenate([wts_b[k]] * reps, axis=1), comb)
        return _dot(comb.astype(BF16), buf[slot, c0:c0 + width, :])

    routed = shared
    for c0 in range(0, always, COMBINE_CHUNK):
        routed = routed + apply(c0, min(COMBINE_CHUNK, always - c0))
    acc_ref[...] = routed
    for c0 in tail:
        @pl.when(c0 < used_ref[i])
        def _(c0=c0):
            acc_ref[...] += apply(c0, COMBINE_TAIL)
    y = x1_ref[...] + g2_ref[...] * acc_ref[...]
    o_ref[...] = _rms_norm(y, fg_ref[...])


def _moe_combine(back, used, ys, pos, wts, xm, x1, g2, final_g, swg, swu, swd, *, tiles_per_mod):
    n, d = xm.shape
    tm = pos.shape[-1]
    nt = n // tm
    gpt = SORT_ROWS // SORT_ALIGN
    row = lambda i, bk, us: (i, 0)
    full = lambda i, bk, us: (0, 0)
    tile = lambda i, bk, us: (i, 0, 0)
    mod_map = lambda i, bk, us: (i // tiles_per_mod, 0, 0)
    grid_spec = pltpu.PrefetchScalarGridSpec(
        num_scalar_prefetch=2,
        grid=(nt,),
        in_specs=[pl.BlockSpec(memory_space=pl.ANY),
                  pl.BlockSpec((None, TOP_K, tm), tile),
                  pl.BlockSpec((None, TOP_K, tm), tile),
                  pl.BlockSpec((tm, d), row),
                  pl.BlockSpec((tm, d), row),
                  pl.BlockSpec((None, 1, d), mod_map),
                  pl.BlockSpec((1, d), full),
                  pl.BlockSpec((d, SHARED_FF), full),
                  pl.BlockSpec((d, SHARED_FF), full),
                  pl.BlockSpec((SHARED_FF, d), full)],
        out_specs=pl.BlockSpec((tm, d), row),
        scratch_shapes=[pltpu.VMEM((2, SORT_ROWS, d), BF16),
                        pltpu.SemaphoreType.DMA((2,)),
                        pltpu.VMEM((tm, d), F32)])
    return pl.pallas_call(
        _moe_combine_kernel,
        out_shape=jax.ShapeDtypeStruct((n, d), F32),
        grid_spec=grid_spec,
        compiler_params=pltpu.CompilerParams(dimension_semantics=("arbitrary",),
                                             vmem_limit_bytes=VMEM_LIMIT),
        name="moe_combine",
    )(back, used, ys, pos.reshape(nt, TOP_K, tm), wts.reshape(nt, TOP_K, tm), xm, x1, g2, final_g, swg, swu, swd)


def _mix(x, mods, p, attn_fn, s0=None):
    sh1, sc1, g1, sh2, sc2, _ = mods
    gla_in, lora, q_s, k_s, v_s = _inproj(x, p["norm_attn_g"], sh1, sc1, p["w_gla"], p["w_lora"], p["w_swa"])
    if s0 is None:
        gla_out, s_f, s_b = _gla(gla_in, lora, p["waf"], p["baf"], p["wab"], p["bab"], p["gla_norm_g"])
    else:
        gla_out, s_f, s_b = _gla(gla_in, lora, p["waf"], p["baf"], p["wab"], p["bab"], p["gla_norm_g"],
                                 s0[0], s0[1])
    att_out = attn_fn(q_s, k_s, v_s)
    routed = _outproj(gla_out, att_out, x, p["w_out"], g1, sh2, sc2, p["norm_ffn_g"],
                      p["rw_cat"], p["rw_hi"], p["rbias"])
    return routed, k_s, v_s, s_f, s_b


def _moe(streams, p):
    d = D_MODEL
    (ra, _), (rb, _) = streams
    n_tiles = [r[1].shape[0] * r[1].shape[1] // MOE_TILE for r, _ in streams]
    pos_all = jnp.concatenate([r[2].reshape(-1, TOP_K, MOE_TILE) for r, _ in streams], axis=0)
    cnt_all = jnp.concatenate([r[4][..., 0].reshape(-1, N_EXPERTS) for r, _ in streams], axis=0)
    start_all = jnp.concatenate([r[5][..., 0].reshape(-1, N_EXPERTS) for r, _ in streams], axis=0)
    used = (start_all[:, -1] + cnt_all[:, -1]).astype(jnp.int32)
    xs = _moe_sort(ra[1].reshape(-1, d), rb[1].reshape(-1, d), pos_all, used)
    n_used, first, tiles, src, back = _moe_plan(cnt_all, start_all)
    ys = _moe_experts(n_used, first, tiles, src, xs, p["wg"], p["wu"], p["wd"],
                      _moe_row_tiles(cnt_all.shape[0] * MOE_TILE))
    outs = []
    tile0 = 0
    for ((x1, xm, pos, wts, cnt, start), g2), nt in zip(streams, n_tiles):
        b, t, _ = x1.shape
        tiles_per_mod = (t // MOE_TILE) if g2.shape[0] > 1 else nt
        y = _moe_combine(back[tile0:tile0 + nt].reshape(-1), used[tile0:tile0 + nt], ys, pos, wts,
                         xm.reshape(-1, d), x1.reshape(-1, d), g2, p["final_norm_g"],
                         p["swg"], p["swu"], p["swd"], tiles_per_mod=tiles_per_mod)
        outs.append(y.reshape(b, t, d))
        tile0 += nt
    return outs


def kernel(x_prompt, x_sample, c, cache_swa_k, cache_swa_v, state_gla_fwd, state_gla_bwd, c_ctx, w_ada, b_ada, norm_attn_g, norm_ffn_g, w_in, gla_wa_f, gla_ba_f, gla_wa_b, gla_ba_b, gla_norm_g, swa_sink, w_out, router_w, router_bias, exp_w_gate, exp_w_up, exp_w_down, sh_w_gate, sh_w_up, sh_w_down, final_norm_g):
    l = 0
    d = D_MODEL
    nb_ctx, t_ctx, _ = x_prompt.shape
    nb_lat, t_lat, _ = x_sample.shape

    pad = jnp.zeros((8 - 1 - nb_lat, d), F32)
    cond8 = jnp.concatenate([c_ctx[None, :], c, pad], axis=0)
    mod = _adaln(cond8, w_ada[l], b_ada[l][None, :])
    mods_ctx = [mod[0:1, i * d:(i + 1) * d][:, None, :] for i in range(6)]
    mods_lat = [mod[1:1 + nb_lat, i * d:(i + 1) * d][:, None, :] for i in range(6)]

    zeros_lora = jnp.zeros((GLA_LORA, GLA_QK), F32)
    rw = router_w[l]
    rw_hi = rw.astype(BF16)
    rw_lo = (rw - rw_hi.astype(F32)).astype(BF16)
    w_in_b = w_in[l].astype(BF16)
    p = {
        "norm_attn_g": norm_attn_g[l][None, :],
        "norm_ffn_g": norm_ffn_g[l][None, :],
        "final_norm_g": final_norm_g[None, :],
        "w_gla": w_in_b[:, :2 * GLA_QK + 2 * GLA_V],
        "w_lora": w_in_b[:, 2 * GLA_QK + 2 * GLA_V:2 * GLA_QK + 2 * GLA_V + 2 * GLA_LORA],
        "w_swa": w_in_b[:, 2 * GLA_QK + 2 * GLA_V + 2 * GLA_LORA:],
        "waf": jnp.concatenate([gla_wa_f[l], zeros_lora], axis=0).astype(BF16),
        "wab": jnp.concatenate([zeros_lora, gla_wa_b[l]], axis=0).astype(BF16),
        "baf": gla_ba_f[l][None, :],
        "bab": gla_ba_b[l][None, :],
        "gla_norm_g": gla_norm_g[l][None, :],
        "w_out": w_out[l].astype(BF16),
        "rw_cat": jnp.concatenate([rw_hi, rw_lo], axis=1),
        "rw_hi": rw_hi,
        "rbias": router_bias[l][:, None],
        "wg": exp_w_gate[l], "wu": exp_w_up[l], "wd": exp_w_down[l],
        "swg": sh_w_gate[l].astype(BF16), "swu": sh_w_up[l].astype(BF16),
        "swd": sh_w_down[l].astype(BF16),
    }
    sink = swa_sink[l]

    routed_ctx, k_c, v_c, s_f, s_b = _mix(x_prompt, mods_ctx, p, functools.partial(_attn_ctx, sink))

    cos, sin_lo, sin_hi = _rope_tables(t_lat)
    kc = cache_swa_k[:, l].reshape(nb_lat, -1, SWA_KV)
    vc = cache_swa_v[:, l].reshape(nb_lat, -1, SWA_KV)
    lat_attn = lambda q, k, v: _attn_lat(sink, q, k, v, kc, vc, cos, sin_lo, sin_hi)
    s0 = (state_gla_fwd[:, l].reshape(nb_lat, GLA_QK, GLA_DV),
          state_gla_bwd[:, l].reshape(nb_lat, GLA_QK, GLA_DV))
    routed_lat, _, _, _, _ = _mix(x_sample, mods_lat, p, lat_attn, s0)
    y_prompt, y_sample = _moe([(routed_ctx, mods_ctx[5]), (routed_lat, mods_lat[5])], p)

    new_k = k_c.reshape(nb_ctx, 1, t_ctx, SWA_KV_HEADS, SWA_HEAD_DIM)
    new_v = v_c.reshape(nb_ctx, 1, t_ctx, SWA_KV_HEADS, SWA_HEAD_DIM)
    new_sf = s_f.reshape(nb_ctx, 1, GLA_HEADS, GLA_DK, GLA_DV)
    new_sb = s_b.reshape(nb_ctx, 1, GLA_HEADS, GLA_DK, GLA_DV)
    return (y_prompt, y_sample, new_k, new_v, new_sf, new_sb)
```

```python
import functools

import jax
import jax.numpy as jnp
from jax import lax
from jax.experimental import pallas as pl
from jax.experimental.pallas import tpu as pltpu

F32 = jnp.float32
BF16 = jnp.bfloat16

D_MODEL = 1024
GLA_HEADS = 4
GLA_DK = 64
GLA_DV = 128
GLA_LORA = 16
GLA_GATE_NORM = 16.0
GLA_CHUNK = 64
GLA_QK = GLA_HEADS * GLA_DK
GLA_V = GLA_HEADS * GLA_DV
SWA_HEAD_DIM = 64
SWA_HEADS = 8
SWA_KV_HEADS = 2
SWA_Q = SWA_HEADS * SWA_HEAD_DIM
SWA_KV = SWA_KV_HEADS * SWA_HEAD_DIM
ATTN_BLOCK = 128
GRID_W = 64
ROPE_BASE = 10000.0
N_EXPERTS = 64
TOP_K = 8
N_EXPERT_GROUPS = 8
TOPK_GROUPS = 4
EXPERT_FF = 128
SHARED_FF = 256
ROUTED_SCALE = 2.5
EPS = 1e-6

LANES = 128
VMEM_LIMIT = 56 * 1024 * 1024

NEG_INF = float("-inf")


def _dot(a, b):
    return jnp.dot(a, b, preferred_element_type=F32)


def _dot_nt(a, b):
    return lax.dot_general(a, b, (((1,), (1,)), ((), ())), preferred_element_type=F32)


def _split_hi_lo(x):
    hi = x.astype(BF16)
    lo = (x - hi.astype(F32)).astype(BF16)
    return hi, lo


def _sigmoid(x):
    return 1.0 / (1.0 + jnp.exp(-x))


def _silu(x):
    return x * _sigmoid(x)


def _rms_norm(x, g):
    ms = jnp.mean(x * x, axis=-1, keepdims=True)
    return x * lax.rsqrt(ms + EPS) * g


def _adaln_kernel(c_ref, w_ref, b_ref, o_ref):
    a_hi, a_lo = _split_hi_lo(_silu(c_ref[...]))
    w_hi, w_lo = _split_hi_lo(w_ref[...])
    o_ref[...] = _dot(a_hi, w_hi) + _dot(a_lo, w_hi) + _dot(a_hi, w_lo) + b_ref[...]


def _adaln(cond8, w_ada, b_ada):
    n = w_ada.shape[1]
    tn = 1536
    return pl.pallas_call(
        _adaln_kernel,
        out_shape=jax.ShapeDtypeStruct((8, n), F32),
        grid=(n // tn,),
        in_specs=[pl.BlockSpec((8, D_MODEL), lambda j: (0, 0)),
                  pl.BlockSpec((D_MODEL, tn), lambda j: (0, j)),
                  pl.BlockSpec((1, tn), lambda j: (0, j))],
        out_specs=pl.BlockSpec((8, tn), lambda j: (0, j)),
        compiler_params=pltpu.CompilerParams(dimension_semantics=("arbitrary",),
                                             vmem_limit_bytes=VMEM_LIMIT),
        name="adaln",
    )(cond8, w_ada, b_ada)


def _inproj_kernel(x_ref, g_ref, sh_ref, sc_ref, wg_ref, wl_ref, ws_ref,
                   gla_ref, lora_ref, q_ref, k_ref, v_ref):
    bb, tb, d = x_ref.shape
    x = x_ref[...].reshape(bb * tb, d)
    h = _rms_norm(x, g_ref[...]) * (1.0 + sc_ref[...]) + sh_ref[...]
    hb = h.astype(BF16)
    gla_ref[...] = _dot(hb, wg_ref[...]).reshape(gla_ref.shape)
    lora_ref[...] = _dot(hb, wl_ref[...]).reshape(lora_ref.shape)
    s = _dot(hb, ws_ref[...])
    q_ref[...] = s[:, :SWA_Q].reshape(q_ref.shape)
    k_ref[...] = s[:, SWA_Q:SWA_Q + SWA_KV].reshape(k_ref.shape)
    v_ref[...] = s[:, SWA_Q + SWA_KV:].reshape(v_ref.shape)


INPROJ_TILE = 512


def _inproj(x, g, sh, sc, w_gla, w_lora, w_swa):
    b, t, d = x.shape
    nmod = sh.shape[0]
    tb = min(t, INPROJ_TILE)
    bb = INPROJ_TILE // tb if nmod == 1 else 1
    mod_map = (lambda i, j: (i, 0, 0)) if nmod > 1 else (lambda i, j: (0, 0, 0))
    row = lambda i, j: (i, j, 0)
    full = lambda i, j: (0, 0)
    n_gla = w_gla.shape[1]
    n_lora = w_lora.shape[1]
    return pl.pallas_call(
        _inproj_kernel,
        out_shape=(jax.ShapeDtypeStruct((b, t, n_gla), F32),
                   jax.ShapeDtypeStruct((b, t, n_lora), F32),
                   jax.ShapeDtypeStruct((b, t, SWA_Q), F32),
                   jax.ShapeDtypeStruct((b, t, SWA_KV), F32),
                   jax.ShapeDtypeStruct((b, t, SWA_KV), F32)),
        grid=(b // bb, t // tb),
        in_specs=[pl.BlockSpec((bb, tb, d), row),
                  pl.BlockSpec((1, d), full),
                  pl.BlockSpec((None, 1, d), mod_map),
                  pl.BlockSpec((None, 1, d), mod_map),
                  pl.BlockSpec((d, n_gla), full),
                  pl.BlockSpec((d, n_lora), full),
                  pl.BlockSpec((d, w_swa.shape[1]), full)],
        out_specs=(pl.BlockSpec((bb, tb, n_gla), row),
                   pl.BlockSpec((bb, tb, n_lora), row),
                   pl.BlockSpec((bb, tb, SWA_Q), row),
                   pl.BlockSpec((bb, tb, SWA_KV), row),
                   pl.BlockSpec((bb, tb, SWA_KV), row)),
        compiler_params=pltpu.CompilerParams(dimension_semantics=("arbitrary", "arbitrary"),
                                             vmem_limit_bytes=VMEM_LIMIT),
        name="inproj",
    )(x, g, sh, sc, w_gla, w_lora, w_swa)


SCAN_UNROLL = 4
OUT_UNROLL = 4


def _log_sigmoid(x):
    return jnp.minimum(x, 0.0) - jnp.log(1.0 + jnp.exp(-jnp.abs(x)))


def _heads_to_rows(x):
    return jnp.concatenate([x[:, h * LANES:(h + 1) * LANES] for h in range(GLA_HEADS)], axis=0)


def _rows_to_heads(x, c):
    return jnp.concatenate([x[h * c:(h + 1) * c, :] for h in range(GLA_HEADS)], axis=1)


def _gla_kernel(has_init, q_ref, k_ref, v_ref, g_ref, lora_ref, waf_ref, baf_ref, wab_ref, bab_ref,
                ng_ref, *rest):
    if has_init:
        s0f_ref, s0b_ref, *rest = rest
    (out_ref, sf_ref, sb_ref, laf_ref, lab_ref, oacc_ref, qtf_ref, qtb_ref, saf_ref, sab_ref,
     stf_ref, stb_ref) = rest
    t = q_ref.shape[0]
    c = GLA_CHUNK
    n = t // c
    hc = GLA_HEADS * c

    lora = lora_ref[...].astype(BF16)
    laf_ref[...] = _log_sigmoid(_dot(lora, waf_ref[...]) + baf_ref[...]) * (1.0 / GLA_GATE_NORM)
    lab_ref[...] = _log_sigmoid(_dot(lora, wab_ref[...]) + bab_ref[...]) * (1.0 / GLA_GATE_NORM)

    if has_init:
        stf_ref[...] = s0f_ref[...].T
        stb_ref[...] = s0b_ref[...].T
    else:
        stf_ref[...] = jnp.zeros_like(stf_ref)
        stb_ref[...] = jnp.zeros_like(stb_ref)
    oacc_ref[...] = jnp.zeros_like(oacc_ref)

    r64 = lax.broadcasted_iota(jnp.int32, (c, c), 0)
    c64 = lax.broadcasted_iota(jnp.int32, (c, c), 1)
    tri_f = jnp.where(c64 <= r64, 1.0, 0.0).astype(BF16)
    tri_b = jnp.where(c64 >= r64, 1.0, 0.0).astype(BF16)
    rr = lax.broadcasted_iota(jnp.int32, (hc, hc), 0)
    cc = lax.broadcasted_iota(jnp.int32, (hc, hc), 1)
    same_head = (rr >> 6) == (cc >> 6)
    keep_f = same_head & ((rr & (c - 1)) >= (cc & (c - 1)))
    keep_b = same_head & ((rr & (c - 1)) <= (cc & (c - 1)))
    head_mask = jnp.where(same_head, 1.0, 0.0).astype(BF16)
    norm_g = ng_ref[...]

    def chunk_rows(ci):
        return pl.ds(pl.multiple_of(ci * c, c), c)

    def tile_heads(x):
        x4 = jnp.concatenate([x] * GLA_HEADS, axis=0)
        return jnp.where(same_head, x4, 0.0).astype(BF16)

    def scan_step(i, carry):
        dirs = []
        for u in range(SCAN_UNROLL):
            dirs += [(SCAN_UNROLL * i + u, laf_ref, tri_f, keep_f, c - 1, stf_ref, saf_ref, qtf_ref),
                     (n - 1 - SCAN_UNROLL * i - u, lab_ref, tri_b, keep_b, 0, stb_ref, sab_ref, qtb_ref)]
        cums = []
        for ci, la_ref, tri, _, _, _, _, _ in dirs:
            la_hi, la_lo = _split_hi_lo(la_ref[chunk_rows(ci), :])
            cums.append(_dot(tri, la_hi) + _dot(tri, la_lo))
        ops = []
        for (ci, _, _, _, last_row, _, _, qt_ref), cum in zip(dirs, cums):
            sl = chunk_rows(ci)
            tot = cum[last_row:last_row + 1, :]
            kc = k_ref[sl, :]
            qt = q_ref[sl, :] * (GLA_DK ** -0.5) * jnp.exp(cum)
            qt_ref[sl, :] = qt.astype(BF16)
            v_rows = _heads_to_rows(v_ref[sl, :])
            ops.append((tot, tile_heads(qt), tile_heads(kc * jnp.exp(-cum)),
                        tile_heads(kc * jnp.exp(tot - cum)), v_rows))
        atts = [_dot_nt(q4, k4) for _, q4, k4, _, _ in ops]
        incs = []
        for (_, _, _, keep, _, _, _, _), (_, _, _, kd4, v_rows), att in zip(dirs, ops, atts):
            att = jnp.where(keep, att, 0.0).astype(BF16)
            incs.append((_dot(att, v_rows.astype(BF16)), _dot(v_rows.T.astype(BF16), kd4)))
        for (ci, _, _, _, _, st_ref, snap_ref, _), (tot, _, _, _, _), (o_intra, st_inc) in zip(dirs, ops, incs):
            oacc_ref[ci] += o_intra
            st = st_ref[...]
            snap_ref[ci] = st.astype(BF16)
            st_ref[...] = jnp.exp(tot) * st + st_inc
        return carry

    def tile_heads_bf16(x):
        return jnp.concatenate([x] * GLA_HEADS, axis=0) * head_mask

    def out_step(i, carry):
        chunks = [OUT_UNROLL * i + u for u in range(OUT_UNROLL)]
        inter = []
        for ci in chunks:
            sl = chunk_rows(ci)
            q4 = jnp.concatenate([tile_heads_bf16(qtf_ref[sl, :]), tile_heads_bf16(qtb_ref[sl, :])], axis=1)
            st = jnp.concatenate([saf_ref[ci], sab_ref[ci]], axis=1)
            inter.append(_dot_nt(q4, st))
        for ci, o_inter in zip(chunks, inter):
            sl = chunk_rows(ci)
            on = _rms_norm(oacc_ref[ci] + o_inter, norm_g)
            gate = _silu(_heads_to_rows(g_ref[sl, :]))
            out_ref[sl, :] = _rows_to_heads(on * gate, c)
        return carry

    lax.fori_loop(0, n // SCAN_UNROLL, scan_step, 0)
    lax.fori_loop(0, n // OUT_UNROLL, out_step, 0)
    sf_ref[...] = stf_ref[...].T
    sb_ref[...] = stb_ref[...].T


def _gla(gla_in, lora, waf, baf, wab, bab, norm_g, s0f=None, s0b=None):
    b, t, _ = gla_in.shape
    has_init = s0f is not None
    n = t // GLA_CHUNK
    bmap = lambda i: (i, 0, 0)
    full = lambda i: (0, 0)
    in_specs = [pl.BlockSpec((None, t, GLA_QK), lambda i: (i, 0, 0)),
                pl.BlockSpec((None, t, GLA_QK), lambda i: (i, 0, 1)),
                pl.BlockSpec((None, t, GLA_V), lambda i: (i, 0, 1)),
                pl.BlockSpec((None, t, GLA_V), lambda i: (i, 0, 2)),
                pl.BlockSpec((None, t, 2 * GLA_LORA), bmap),
                pl.BlockSpec((2 * GLA_LORA, GLA_QK), full),
                pl.BlockSpec((1, GLA_QK), full),
                pl.BlockSpec((2 * GLA_LORA, GLA_QK), full),
                pl.BlockSpec((1, GLA_QK), full),
                pl.BlockSpec((1, GLA_DV), full)]
    args = [gla_in, gla_in, gla_in, gla_in, lora, waf, baf, wab, bab, norm_g]
    if has_init:
        in_specs += [pl.BlockSpec((None, GLA_QK, GLA_DV), bmap)] * 2
        args += [s0f, s0b]
    return pl.pallas_call(
        functools.partial(_gla_kernel, has_init),
        out_shape=(jax.ShapeDtypeStruct((b, t, GLA_V), F32),
                   jax.ShapeDtypeStruct((b, GLA_QK, GLA_DV), F32),
                   jax.ShapeDtypeStruct((b, GLA_QK, GLA_DV), F32)),
        grid=(b,),
        in_specs=in_specs,
        out_specs=(pl.BlockSpec((None, t, GLA_V), bmap),
                   pl.BlockSpec((None, GLA_QK, GLA_DV), bmap),
                   pl.BlockSpec((None, GLA_QK, GLA_DV), bmap)),
        scratch_shapes=[pltpu.VMEM((t, GLA_QK), F32),
                        pltpu.VMEM((t, GLA_QK), F32),
                        pltpu.VMEM((n, GLA_HEADS * GLA_CHUNK, GLA_DV), F32),
                        pltpu.VMEM((t, GLA_QK), BF16),
                        pltpu.VMEM((t, GLA_QK), BF16),
                        pltpu.VMEM((n, GLA_DV, GLA_QK), BF16),
                        pltpu.VMEM((n, GLA_DV, GLA_QK), BF16),
                        pltpu.VMEM((GLA_DV, GLA_QK), F32),
                        pltpu.VMEM((GLA_DV, GLA_QK), F32)],
        compiler_params=pltpu.CompilerParams(dimension_semantics=("arbitrary",),
                                             vmem_limit_bytes=VMEM_LIMIT),
        name="gla",
    )(*args)


def _dup_groups(x):
    lo = lax.broadcasted_iota(jnp.int32, x.shape, 1) < SWA_HEAD_DIM
    xr = pltpu.roll(x, SWA_HEAD_DIM, axis=1)
    return jnp.where(lo, x, xr), jnp.where(lo, xr, x)


def _pairs_attention(qps, sinks, k_dups, vt_dups, masks):
    nq = qps[0].shape[0]
    lo = lax.broadcasted_iota(jnp.int32, (nq, LANES), 1) < SWA_HEAD_DIM
    even = lax.broadcasted_iota(jnp.int32, (1, 2 * nq), 1) < nq
    scores = []
    for qp, k_dup in zip(qps, k_dups):
        q2 = jnp.concatenate([jnp.where(lo, qp, 0.0), jnp.where(lo, 0.0, qp)], axis=0).astype(BF16)
        scores.append(_dot_nt(k_dup, q2))
    probs = []
    for s, (sink_even, sink_odd), mask in zip(scores, sinks, masks):
        if mask is not None:
            s = jnp.where(mask, s, NEG_INF)
        sink = jnp.where(even, sink_even, sink_odd)
        m = jnp.maximum(jnp.max(s, axis=0, keepdims=True), sink)
        p = jnp.exp(s - m)
        denom = jnp.sum(p, axis=0, keepdims=True) + jnp.exp(sink - m)
        probs.append((p.astype(BF16), 1.0 / denom))
    outs = []
    for (p, rdenom), vt_dup in zip(probs, vt_dups):
        o = _dot(vt_dup, p) * rdenom
        outs.append(jnp.concatenate([o[:SWA_HEAD_DIM, :nq], o[SWA_HEAD_DIM:, nq:]], axis=0).T)
    return outs


CTX_BATCH = 4


def _attn_ctx_kernel(sink_ref, q_ref, k_ref, v_ref, o_ref):
    scale = SWA_HEAD_DIM ** -0.5
    pairs = range(SWA_HEADS // 2)
    items = [(bb, pr) for bb in range(q_ref.shape[0]) for pr in pairs]
    kd = [[x.astype(BF16) for x in _dup_groups(k_ref[bb])] for bb in range(q_ref.shape[0])]
    vt = [[x.T.astype(BF16) for x in _dup_groups(v_ref[bb])] for bb in range(q_ref.shape[0])]
    outs = _pairs_attention([q_ref[bb, :, pr * LANES:(pr + 1) * LANES] * scale for bb, pr in items],
                            [(sink_ref[2 * pr], sink_ref[2 * pr + 1]) for _, pr in items],
                            [kd[bb][pr // 2] for bb, pr in items], [vt[bb][pr // 2] for bb, pr in items],
                            [None] * len(items))
    for (bb, pr), out in zip(items, outs):
        o_ref[bb, :, pr * LANES:(pr + 1) * LANES] = out


def _attn_ctx(sink, q, k, v):
    b, t, _ = q.shape
    bmap = lambda i: (i, 0, 0)
    return pl.pallas_call(
        _attn_ctx_kernel,
        out_shape=jax.ShapeDtypeStruct((b, t, SWA_Q), F32),
        grid=(b // CTX_BATCH,),
        in_specs=[pl.BlockSpec(memory_space=pltpu.SMEM),
                  pl.BlockSpec((CTX_BATCH, t, SWA_Q), bmap),
                  pl.BlockSpec((CTX_BATCH, t, SWA_KV), bmap),
                  pl.BlockSpec((CTX_BATCH, t, SWA_KV), bmap)],
        out_specs=pl.BlockSpec((CTX_BATCH, t, SWA_Q), bmap),
        compiler_params=pltpu.CompilerParams(dimension_semantics=("arbitrary",),
                                             vmem_limit_bytes=VMEM_LIMIT),
        name="attn_ctx",
    )(sink, q, k, v)


LAT_BLOCKS = 2


def _rope(x, cos, sin_lo, sin_hi):
    return x * cos + pltpu.roll(x, LANES - 16, axis=1) * sin_lo + pltpu.roll(x, 16, axis=1) * sin_hi


def _attn_lat_kernel(sink_ref, q_ref, k_ref, v_ref, kc_ref, vc_ref, cos_ref, sl_ref, sh_ref,
                     o_ref, kw_ref, vw_ref):
    t = q_ref.shape[0]
    ab = ATTN_BLOCK
    nb = t // ab
    scale = SWA_HEAD_DIM ** -0.5

    k_rot = _dup_groups(_rope(k_ref[...], cos_ref[...], sl_ref[...], sh_ref[...]))
    v_dup = _dup_groups(v_ref[...])
    zeros = jnp.zeros((ab, LANES), BF16)
    for grp in range(SWA_KV_HEADS):
        kw_ref[grp, 0:ab, :] = zeros
        kw_ref[grp, ab:ab + t, :] = k_rot[grp].astype(BF16)
        kw_ref[grp, ab + t:, :] = zeros
        vw_ref[grp, 0] = zeros
        for blk in range(nb):
            vw_ref[grp, blk + 1] = v_dup[grp][blk * ab:(blk + 1) * ab, :].T.astype(BF16)
        vw_ref[grp, nb + 1] = zeros
    kc = [x.astype(BF16) for x in _dup_groups(kc_ref[...])]
    vct = [x.T.astype(BF16) for x in _dup_groups(vc_ref[...])]
    lc = kc_ref.shape[0]

    key = lax.broadcasted_iota(jnp.int32, (lc + 3 * ab, 2 * ab), 0) - lc
    tq = lax.broadcasted_iota(jnp.int32, (lc + 3 * ab, 2 * ab), 1) & (ab - 1)
    band = (key < 0) | (jnp.abs(tq + ab - key) <= ab)

    def block(it, carry):
        pairs = range(SWA_HEADS // 2)
        qps, sinks, k_dups, vt_dups, masks, places = [], [], [], [], [], []
        for u in range(LAT_BLOCKS):
            nq = it * LAT_BLOCKS + u
            row0 = pl.multiple_of(nq * ab, ab)
            s_abs = key + (nq - 1) * ab
            mask = band & ((key < 0) | ((s_abs >= 0) & (s_abs < t)))
            cos = cos_ref[pl.ds(row0, ab), :]
            s_lo = sl_ref[pl.ds(row0, ab), :]
            s_hi = sh_ref[pl.ds(row0, ab), :]
            k_all = [jnp.concatenate([kc[grp], kw_ref[grp, pl.ds(row0, 3 * ab), :]], axis=0)
                     for grp in range(SWA_KV_HEADS)]
            vt_all = [jnp.concatenate([vct[grp], vw_ref[grp, nq], vw_ref[grp, nq + 1], vw_ref[grp, nq + 2]],
                                      axis=1) for grp in range(SWA_KV_HEADS)]
            for pr in pairs:
                qps.append(_rope(q_ref[pl.ds(row0, ab), pr * LANES:(pr + 1) * LANES], cos, s_lo, s_hi) * scale)
                sinks.append((sink_ref[2 * pr], sink_ref[2 * pr + 1]))
                k_dups.append(k_all[pr // 2])
                vt_dups.append(vt_all[pr // 2])
                masks.append(mask)
                places.append((row0, pr))
        outs = _pairs_attention(qps, sinks, k_dups, vt_dups, masks)
        for (row0, pr), out in zip(places, outs):
            o_ref[pl.ds(row0, ab), pr * LANES:(pr + 1) * LANES] = out
        return carry

    lax.fori_loop(0, nb // LAT_BLOCKS, block, 0)


def _attn_lat(sink, q, k, v, kc, vc, cos, sin_lo, sin_hi):
    b, t, _ = q.shape
    lc = kc.shape[1]
    bmap = lambda i: (i, 0, 0)
    full = lambda i: (0, 0)
    return pl.pallas_call(
        _attn_lat_kernel,
        out_shape=jax.ShapeDtypeStruct((b, t, SWA_Q), F32),
        grid=(b,),
        in_specs=[pl.BlockSpec(memory_space=pltpu.SMEM),
                  pl.BlockSpec((None, t, SWA_Q), bmap),
                  pl.BlockSpec((None, t, SWA_KV), bmap),
                  pl.BlockSpec((None, t, SWA_KV), bmap),
                  pl.BlockSpec((None, lc, SWA_KV), bmap),
                  pl.BlockSpec((None, lc, SWA_KV), bmap),
                  pl.BlockSpec((t, LANES), full),
                  pl.BlockSpec((t, LANES), full),
                  pl.BlockSpec((t, LANES), full)],
        out_specs=pl.BlockSpec((None, t, SWA_Q), bmap),
        scratch_shapes=[pltpu.VMEM((SWA_KV_HEADS, t + 2 * ATTN_BLOCK, LANES), BF16),
                        pltpu.VMEM((SWA_KV_HEADS, t // ATTN_BLOCK + 2, LANES, ATTN_BLOCK), BF16)],
        compiler_params=pltpu.CompilerParams(dimension_semantics=("arbitrary",),
                                             vmem_limit_bytes=VMEM_LIMIT),
        name="attn_lat",
    )(sink, q, k, v, kc, vc, cos, sin_lo, sin_hi)


def _rope_tables(t):
    half = SWA_HEAD_DIM // 2
    quarter = half // 2
    pos = jnp.arange(t)
    row = (pos // GRID_W).astype(F32)
    col = (pos % GRID_W).astype(F32)
    inv_freq = ROPE_BASE ** (-jnp.arange(quarter, dtype=F32) / quarter)
    lane = jnp.arange(LANES)
    d = lane % SWA_HEAD_DIM
    freq = inv_freq[d % quarter]
    use_row = (d < half)
    ang = jnp.where(use_row[None, :], row[:, None], col[:, None]) * freq[None, :]
    cos = jnp.cos(ang)
    sin = jnp.sin(ang)
    lower = (d % half) < quarter
    return cos, jnp.where(lower[None, :], -sin, 0.0), jnp.where(lower[None, :], 0.0, sin)


def _route(sel, scores):
    n = sel.shape[1]
    gsz = N_EXPERTS // N_EXPERT_GROUPS

    def first_max(x, idx, size):
        m = jnp.max(x, axis=0, keepdims=True)
        first = jnp.min(jnp.where(x == m, idx, float(size)), axis=0, keepdims=True)
        return m, idx == first

    i8 = lax.broadcasted_iota(jnp.int32, (gsz, n), 0).astype(F32)
    rows = []
    for g in range(N_EXPERT_GROUPS):
        slab = sel[g * gsz:(g + 1) * gsz, :]
        m1, hit = first_max(slab, i8, gsz)
        m2 = jnp.max(jnp.where(hit, NEG_INF, slab), axis=0, keepdims=True)
        rows.append(m1 + m2)
    gscore = jnp.concatenate(rows, axis=0)
    gsel = jnp.zeros((N_EXPERT_GROUPS, n), F32)
    for _ in range(TOPK_GROUPS):
        _, hit = first_max(gscore, i8, N_EXPERT_GROUPS)
        gsel = jnp.where(hit, 1.0, gsel)
        gscore = jnp.where(hit, NEG_INF, gscore)
    emask = jnp.concatenate(
        [jnp.broadcast_to(gsel[g:g + 1, :], (gsz, n)) for g in range(N_EXPERT_GROUPS)], axis=0)
    cand = jnp.where(emask > 0.5, sel, NEG_INF)
    ie = lax.broadcasted_iota(jnp.int32, (N_EXPERTS, n), 0).astype(F32)
    w = jnp.zeros((N_EXPERTS, n), F32)
    chosen = jnp.zeros((N_EXPERTS, n), F32)
    hits = []
    for _ in range(TOP_K):
        _, hit = first_max(cand, ie, N_EXPERTS)
        hits.append(hit)
        w = jnp.where(hit, scores, w)
        chosen = jnp.where(hit, 1.0, chosen)
        cand = jnp.where(hit, NEG_INF, cand)
    gates = w / jnp.sum(w, axis=0, keepdims=True) * ROUTED_SCALE

    s_idx = lax.broadcasted_iota(jnp.int32, (n, n), 0)
    t_idx = lax.broadcasted_iota(jnp.int32, (n, n), 1)
    tile_shift = MOE_TILE.bit_length() - 1
    before = jnp.where((s_idx < t_idx) & ((s_idx >> tile_shift) == (t_idx >> tile_shift)), 1.0, 0.0)
    rank = _dot(chosen.astype(BF16), before.astype(BF16))
    e_row = lax.broadcasted_iota(jnp.int32, (N_EXPERTS, N_EXPERTS), 0)
    e_col = lax.broadcasted_iota(jnp.int32, (N_EXPERTS, N_EXPERTS), 1)
    below = jnp.where(e_col < e_row, 1.0, 0.0).astype(BF16)
    lane_tile = lax.broadcasted_iota(jnp.int32, (1, n), 1) >> tile_shift
    sizes, starts = [], []
    first_row = jnp.zeros((N_EXPERTS, n), F32)
    for ti in range(n // MOE_TILE):
        count = jnp.sum(chosen[:, ti * MOE_TILE:(ti + 1) * MOE_TILE], axis=1, keepdims=True)
        padded = jnp.floor((count + (SORT_ALIGN - 1)) * (1.0 / SORT_ALIGN)) * SORT_ALIGN
        padded = jnp.broadcast_to(padded, (N_EXPERTS, LANES))
        start = _dot(below, padded.astype(BF16))
        first_row = jnp.where(lane_tile == ti, start[:, 0:1], first_row)
        sizes.append(padded)
        starts.append(start)
    row = first_row + rank
    pos = jnp.concatenate([jnp.sum(jnp.where(h, row, 0.0), axis=0, keepdims=True) for h in hits], axis=0)
    wts = jnp.concatenate([jnp.sum(jnp.where(h, gates, 0.0), axis=0, keepdims=True) for h in hits], axis=0)
    return pos, wts, sizes, starts


def _outproj_kernel(gla_ref, att_ref, x_ref, wo_ref, g1_ref, sh_ref, sc_ref, ng_ref, rw_ref, rwh_ref,
                    rb_ref, x1_ref, xm_ref, pos_ref, wts_ref, cnt_ref, start_ref):
    bb, tb, d = x_ref.shape
    tm = bb * tb
    y = (_dot(gla_ref[...].reshape(tm, GLA_V).astype(BF16), wo_ref[0:GLA_V, :])
         + _dot(att_ref[...].reshape(tm, SWA_Q).astype(BF16), wo_ref[GLA_V:, :]))
    x1 = x_ref[...].reshape(tm, d) + g1_ref[...] * y
    x1_ref[...] = x1.reshape(bb, tb, d)
    xm = _rms_norm(x1, ng_ref[...]) * (1.0 + sc_ref[...]) + sh_ref[...]
    xm_hi, xm_lo = _split_hi_lo(xm)
    xm_ref[...] = xm_hi.reshape(bb, tb, d)
    lg = _dot(xm_hi, rw_ref[...])
    logits = lg[:, :N_EXPERTS] + lg[:, N_EXPERTS:] + _dot(xm_lo, rwh_ref[...])
    lt = jnp.concatenate([logits, jnp.zeros((tm, LANES - N_EXPERTS), F32)], axis=1).T[:N_EXPERTS, :]
    scores = _sigmoid(lt)
    pos, wts, sizes, starts = _route(scores + rb_ref[...], scores)
    tiles_per_batch = tb // MOE_TILE
    for ti in range(tm // MOE_TILE):
        at = (ti // tiles_per_batch, ti % tiles_per_batch)
        pos_ref[at] = pos[:, ti * MOE_TILE:(ti + 1) * MOE_TILE]
        wts_ref[at] = wts[:, ti * MOE_TILE:(ti + 1) * MOE_TILE]
        cnt_ref[at] = sizes[ti]
        start_ref[at] = starts[ti]


OUTPROJ_TILE = 512


def _outproj(gla_out, att_out, x, w_out, g1, sh2, sc2, norm_g, rw_cat, rw_hi, rbias):
    b, t, d = x.shape
    nmod = g1.shape[0]
    tb = min(t, OUTPROJ_TILE)
    bb = OUTPROJ_TILE // tb if nmod == 1 else 1
    tpb = tb // MOE_TILE
    mod_map = (lambda i, j: (i, 0, 0)) if nmod > 1 else (lambda i, j: (0, 0, 0))
    row = lambda i, j: (i, j, 0)
    full = lambda i, j: (0, 0)
    tile = lambda i, j: (i, j, 0, 0)
    nt = t // MOE_TILE
    return pl.pallas_call(
        _outproj_kernel,
        out_shape=(jax.ShapeDtypeStruct((b, t, d), F32),
                   jax.ShapeDtypeStruct((b, t, d), BF16),
                   jax.ShapeDtypeStruct((b, nt, TOP_K, MOE_TILE), F32),
                   jax.ShapeDtypeStruct((b, nt, TOP_K, MOE_TILE), F32),
                   jax.ShapeDtypeStruct((b, nt, N_EXPERTS, LANES), F32),
                   jax.ShapeDtypeStruct((b, nt, N_EXPERTS, LANES), F32)),
        grid=(b // bb, t // tb),
        in_specs=[pl.BlockSpec((bb, tb, GLA_V), row),
                  pl.BlockSpec((bb, tb, SWA_Q), row),
                  pl.BlockSpec((bb, tb, d), row),
                  pl.BlockSpec((d, d), full),
                  pl.BlockSpec((None, 1, d), mod_map),
                  pl.BlockSpec((None, 1, d), mod_map),
                  pl.BlockSpec((None, 1, d), mod_map),
                  pl.BlockSpec((1, d), full),
                  pl.BlockSpec((d, 2 * N_EXPERTS), full),
                  pl.BlockSpec((d, N_EXPERTS), full),
                  pl.BlockSpec((N_EXPERTS, 1), full)],
        out_specs=(pl.BlockSpec((bb, tb, d), row),
                   pl.BlockSpec((bb, tb, d), row),
                   pl.BlockSpec((bb, tpb, TOP_K, MOE_TILE), tile),
                   pl.BlockSpec((bb, tpb, TOP_K, MOE_TILE), tile),
                   pl.BlockSpec((bb, tpb, N_EXPERTS, LANES), tile),
                   pl.BlockSpec((bb, tpb, N_EXPERTS, LANES), tile)),
        compiler_params=pltpu.CompilerParams(dimension_semantics=("arbitrary", "arbitrary"),
                                             vmem_limit_bytes=VMEM_LIMIT),
        name="outproj",
    )(gla_out, att_out, x, w_out, g1, sh2, sc2, norm_g, rw_cat, rw_hi, rbias)


MOE_TILE = 256
SORT_ALIGN = 16
SORT_ROWS = 3072
ROW_TILE = 512
GATHER_SLOTS = 3
FFN_CHAINS = 4
COMBINE_CHUNK = 1024
ALWAYS_ROWS = 2560
COMBINE_TAIL = 512


def _moe_sort_kernel(tiles_a, used_ref, xa_ref, xb_ref, pos_ref, xs_ref):
    i = pl.program_id(0)
    x = jnp.where(i < tiles_a, xa_ref[...], xb_ref[...])
    pos = pos_ref[...]
    tm = x.shape[0]
    used = used_ref[i]

    rows = lax.broadcasted_iota(jnp.int32, (tm, tm), 0).astype(F32).astype(BF16)
    one = jnp.ones((tm, tm), BF16)

    def fill(blk):
        local = (pos - float(blk * tm)).astype(BF16)
        onehot = jnp.zeros((tm, tm), BF16)
        for k in range(TOP_K):
            onehot = jnp.where(rows == local[k:k + 1, :], one, onehot)
        xs_ref[blk * tm:(blk + 1) * tm, :] = _dot(onehot, x).astype(BF16)

    for blk in range(SORT_ROWS // tm):
        if (blk + 1) * tm <= ALWAYS_ROWS:
            fill(blk)
        else:
            pl.when(blk * tm < used)(functools.partial(fill, blk))

            @pl.when(blk * tm >= used)
            def _():
                xs_ref[blk * tm:(blk + 1) * tm, :] = jnp.zeros((tm, D_MODEL), BF16)


def _moe_sort(xm_a, xm_b, pos, used):
    d = xm_a.shape[1]
    nt, _, tm = pos.shape
    tiles_a = xm_a.shape[0] // tm
    grid_spec = pltpu.PrefetchScalarGridSpec(
        num_scalar_prefetch=1,
        grid=(nt,),
        in_specs=[pl.BlockSpec((tm, d), lambda i, u: (jnp.minimum(i, tiles_a - 1), 0)),
                  pl.BlockSpec((tm, d), lambda i, u: (jnp.maximum(i - tiles_a, 0), 0)),
                  pl.BlockSpec((None, TOP_K, tm), lambda i, u: (i, 0, 0))],
        out_specs=pl.BlockSpec((SORT_ROWS, d), lambda i, u: (i, 0)))
    return pl.pallas_call(
        functools.partial(_moe_sort_kernel, tiles_a),
        out_shape=jax.ShapeDtypeStruct((nt * SORT_ROWS, d), BF16),
        grid_spec=grid_spec,
        compiler_params=pltpu.CompilerParams(dimension_semantics=("arbitrary",),
                                             vmem_limit_bytes=VMEM_LIMIT),
        name="moe_sort",
    )(used, xm_a, xm_b, pos)


def _moe_row_tiles(n_tokens):
    rows = n_tokens * TOP_K + (n_tokens // MOE_TILE) * N_EXPERTS * (SORT_ALIGN - 1) + N_EXPERTS * (ROW_TILE - 1)
    return -(-rows // ROW_TILE) + GATHER_SLOTS - 1


PLAN_CHUNK = 1280


def _int_dot_r(a, onehot):
    hi = jnp.floor(a * (1.0 / 256.0))
    return _dot(hi.astype(BF16), onehot) * 256.0 + _dot((a - hi * 256.0).astype(BF16), onehot)


def _int_dot_l(onehot, b):
    hi = jnp.floor(b * (1.0 / 256.0))
    return _dot(onehot, hi.astype(BF16)) * 256.0 + _dot(onehot, (b - hi * 256.0).astype(BF16))


def _moe_plan_kernel(cnt_ref, start_ref, src_ref, first_ref, tiles_ref, nu_ref, back_ref):
    nt, ne = cnt_ref.shape
    gpt = SORT_ROWS // SORT_ALIGN
    gpr = ROW_TILE // SORT_ALIGN
    gc = cnt_ref[...] * (1.0 / SORT_ALIGN)
    ls = start_ref[...] * (1.0 / SORT_ALIGN)

    def transpose(x):
        x = jnp.concatenate([x, jnp.zeros((nt, LANES - ne), F32)], axis=1)
        x = jnp.concatenate([x, jnp.zeros((LANES - nt, LANES), F32)], axis=0)
        return x.T[:ne, :nt]

    def tri(n, keep):
        return jnp.where(keep(lax.broadcasted_iota(jnp.int32, (n, n), 0),
                              lax.broadcasted_iota(jnp.int32, (n, n), 1)), 1.0, 0.0).astype(BF16)

    gc_t = transpose(gc)
    ls_t = transpose(ls)
    tot_c = jnp.broadcast_to(jnp.sum(gc_t, axis=1, keepdims=True), (ne, LANES))
    ptot_c = jnp.floor((tot_c + (gpr - 1)) * (1.0 / gpr)) * gpr
    gend_c = _int_dot_l(tri(ne, lambda r, c: c <= r), ptot_c)
    gstart_c = gend_c - ptot_c
    n_used = gend_c[ne - 1:ne, :] * (1.0 / gpr)
    nu_ref[...] = n_used.astype(jnp.int32)
    tot_r = jnp.sum(gc, axis=0, keepdims=True)
    ptot_r = jnp.floor((tot_r + (gpr - 1)) * (1.0 / gpr)) * gpr
    gstart_r = _int_dot_r(jnp.broadcast_to(ptot_r, (8, ne)), tri(ne, lambda r, c: r < c))
    cumex = _dot(tri(nt, lambda r, c: c < r), gc.astype(BF16))
    cumex_t = _dot(gc_t.astype(BF16), tri(nt, lambda r, c: r < c))
    tile_base = lax.broadcasted_iota(jnp.int32, (nt, ne), 0).astype(F32) * gpt + ls
    table = jnp.concatenate([cumex + gc, cumex, tile_base, gstart_r, jnp.broadcast_to(tot_r, (8, ne))], axis=0)

    e_iota = lax.broadcasted_iota(jnp.int32, (ne, PLAN_CHUNK), 0).astype(F32)
    for ch in range(src_ref.shape[1] // PLAN_CHUNK):
        g = (lax.broadcasted_iota(jnp.int32, (1, PLAN_CHUNK), 1) + ch * PLAN_CHUNK).astype(F32)
        eg = jnp.sum(jnp.where(gend_c[:, 0:1] <= g, 1.0, 0.0), axis=0, keepdims=True)
        picked = _int_dot_r(table, jnp.where(e_iota == eg, 1.0, 0.0).astype(BF16))
        cum_g, cumex_g, base_g = picked[0:nt], picked[nt:2 * nt], picked[2 * nt:3 * nt]
        u = g - picked[3 * nt:3 * nt + 1]
        in_tile = (cumex_g <= u) & (u < cum_g)
        src = jnp.sum(jnp.where(in_tile, base_g - cumex_g, 0.0), axis=0, keepdims=True) + u
        src = jnp.where(u < picked[3 * nt + 8:3 * nt + 9], src, gpt - 1.0)
        src_ref[:, ch * PLAN_CHUNK:(ch + 1) * PLAN_CHUNK] = src.astype(jnp.int32)

    first_ref[...] = (gstart_c * (1.0 / gpr)).astype(jnp.int32)
    tiles_ref[...] = (ptot_c * (1.0 / gpr)).astype(jnp.int32)

    lg = lax.broadcasted_iota(jnp.int32, (ne, back_ref.shape[1]), 1).astype(F32)
    for t in range(nt):
        first = ls_t[:, t:t + 1]
        inside = (first <= lg) & (lg < first + gc_t[:, t:t + 1])
        shift = gstart_c[:, 0:1] + cumex_t[:, t:t + 1] - first
        val = jnp.sum(jnp.where(inside, shift + lg, 0.0), axis=0, keepdims=True)
        back_ref[t:t + 1, :] = val.astype(jnp.int32)


def _moe_plan(cnt, start):
    nt, ne = cnt.shape
    row_tiles = _moe_row_tiles(nt * MOE_TILE)
    gpt = SORT_ROWS // SORT_ALIGN
    gpr = ROW_TILE // SORT_ALIGN
    n_src = -(-(row_tiles * gpr) // PLAN_CHUNK) * PLAN_CHUNK
    n_back = -(-gpt // LANES) * LANES
    src, first, tiles, nu, back = pl.pallas_call(
        _moe_plan_kernel,
        out_shape=(jax.ShapeDtypeStruct((1, n_src), jnp.int32),
                   jax.ShapeDtypeStruct((ne, LANES), jnp.int32),
                   jax.ShapeDtypeStruct((ne, LANES), jnp.int32),
                   jax.ShapeDtypeStruct((1, LANES), jnp.int32),
                   jax.ShapeDtypeStruct((nt, n_back), jnp.int32)),
        compiler_params=pltpu.CompilerParams(vmem_limit_bytes=VMEM_LIMIT),
        name="moe_plan",
    )(cnt, start)
    return nu[0, :1], first[:, 0], tiles[:, 0], src[0, :row_tiles * gpr], back[:, :gpt]


def _moe_experts_kernel(nu_ref, first_ref, tiles_ref, src_ref, xs_hbm, wg_ref, wu_ref, wd_ref, ys_hbm,
                        xbuf, ybuf, gsem, osem, wgu_s, wd_s):
    e = pl.program_id(0)
    n_used = nu_ref[0]
    gpr = ROW_TILE // SORT_ALIGN
    part = ROW_TILE // FFN_CHAINS

    def gather(tile, to_slot, j0=0, j1=gpr):
        for j in range(j0, j1):
            row = pl.multiple_of(src_ref[tile * gpr + j] * SORT_ALIGN, SORT_ALIGN)
            pltpu.make_async_copy(xs_hbm.at[pl.ds(row, SORT_ALIGN), :],
                                  xbuf.at[to_slot, j * SORT_ALIGN:(j + 1) * SORT_ALIGN, :],
                                  gsem.at[to_slot]).start(priority=j % 2)

    def drain(of_slot):
        for j in range(gpr):
            pltpu.make_async_copy(xs_hbm.at[0:SORT_ALIGN, :],
                                  xbuf.at[of_slot, j * SORT_ALIGN:(j + 1) * SORT_ALIGN, :], gsem.at[of_slot]).wait()

    def out_copy(tile, of_slot):
        row = pl.multiple_of(tile * ROW_TILE, ROW_TILE)
        return pltpu.make_async_copy(ybuf.at[of_slot], ys_hbm.at[pl.ds(row, ROW_TILE), :], osem.at[of_slot])

    @pl.when(e == 0)
    def _():
        gather(0, 0)
        gather(1, 1)

    wgu_s[:, :EXPERT_FF] = wg_ref[...].astype(BF16)
    wgu_s[:, EXPERT_FF:] = wu_ref[...].astype(BF16)
    wd_s[...] = wd_ref[...].astype(BF16)

    def row_tile(i, carry):
        r = first_ref[e] + i
        slot = lax.rem(r, GATHER_SLOTS)
        oslot = lax.rem(r, 2)
        next_slot = lax.rem(r + 2, GATHER_SLOTS)
        drain(slot)

        @pl.when(r >= 2)
        def _():
            out_copy(r - 2, oslot).wait()

        abs_ = []
        for c in range(FFN_CHAINS):
            abs_.append(_dot(xbuf[slot, c * part:(c + 1) * part, :], wgu_s[...]))
            gather(r + 2, next_slot, c * gpr // FFN_CHAINS, (c + 1) * gpr // FFN_CHAINS)
        hs = [(_silu(ab[:, :EXPERT_FF]) * ab[:, EXPERT_FF:]).astype(BF16) for ab in abs_]
        ys = [_dot(h, wd_s[...]).astype(BF16) for h in hs]
        for c in range(FFN_CHAINS):
            ybuf[oslot, c * part:(c + 1) * part, :] = ys[c]
        out_copy(r, oslot).start()
        return carry

    lax.fori_loop(0, tiles_ref[e], row_tile, 0)

    @pl.when(e == pl.num_programs(0) - 1)
    def _():
        drain(lax.rem(n_used, GATHER_SLOTS))
        drain(lax.rem(n_used + 1, GATHER_SLOTS))
        out_copy(n_used - 1, lax.rem(n_used - 1, 2)).wait()

        @pl.when(n_used >= 2)
        def _():
            out_copy(n_used - 2, lax.rem(n_used, 2)).wait()


def _moe_experts(n_used, first, tiles, src, xs, wg, wu, wd, row_tiles):
    d = xs.shape[-1]
    ne = wg.shape[0]
    w_map = lambda e, nu, fi, ti, sr: (e, 0, 0)
    grid_spec = pltpu.PrefetchScalarGridSpec(
        num_scalar_prefetch=4,
        grid=(ne,),
        in_specs=[pl.BlockSpec(memory_space=pl.ANY),
                  pl.BlockSpec((None, d, EXPERT_FF), w_map),
                  pl.BlockSpec((None, d, EXPERT_FF), w_map),
                  pl.BlockSpec((None, EXPERT_FF, d), w_map)],
        out_specs=pl.BlockSpec(memory_space=pl.ANY),
        scratch_shapes=[pltpu.VMEM((GATHER_SLOTS, ROW_TILE, d), BF16),
                        pltpu.VMEM((2, ROW_TILE, d), BF16),
                        pltpu.SemaphoreType.DMA((GATHER_SLOTS,)),
                        pltpu.SemaphoreType.DMA((2,)),
                        pltpu.VMEM((d, 2 * EXPERT_FF), BF16),
                        pltpu.VMEM((EXPERT_FF, d), BF16)])
    return pl.pallas_call(
        _moe_experts_kernel,
        out_shape=jax.ShapeDtypeStruct((row_tiles * ROW_TILE, d), BF16),
        grid_spec=grid_spec,
        compiler_params=pltpu.CompilerParams(dimension_semantics=("arbitrary",),
                                             vmem_limit_bytes=VMEM_LIMIT),
        name="moe_experts",
    )(n_used, first, tiles, src, xs, wg, wu, wd)


def _moe_combine_kernel(back_ref, used_ref, ys_hbm, pos_ref, wts_ref, xm_ref, x1_ref, g2_ref, fg_ref,
                        swg_ref, swu_ref, swd_ref, o_ref, buf, sem, acc_ref):
    i = pl.program_id(0)
    gpt = SORT_ROWS // SORT_ALIGN
    slot = lax.rem(i, 2)
    always = ALWAYS_ROWS
    tail = range(always, SORT_ROWS, COMBINE_TAIL)

    def copies(tile, of_slot, g0, g1, start):
        for g in range(g0, g1):
            row = pl.multiple_of(back_ref[tile * gpt + g] * SORT_ALIGN, SORT_ALIGN) if start else 0
            cp = pltpu.make_async_copy(ys_hbm.at[pl.ds(row, SORT_ALIGN), :],
                                       buf.at[of_slot, g * SORT_ALIGN:(g + 1) * SORT_ALIGN, :], sem.at[of_slot])
            if start:
                cp.start(priority=g % 2)
            else:
                cp.wait()

    def transfer(tile, of_slot, start):
        copies(tile, of_slot, 0, always // SORT_ALIGN, start)
        for c0 in tail:
            pl.when(c0 < used_ref[tile])(functools.partial(
                copies, tile, of_slot, c0 // SORT_ALIGN, (c0 + COMBINE_TAIL) // SORT_ALIGN, start))

    @pl.when(i == 0)
    def _():
        transfer(0, 0, True)

    @pl.when(i + 1 < pl.num_programs(0))
    def _():
        transfer(i + 1, 1 - slot, True)

    x = xm_ref[...]
    tm = x.shape[0]
    pad = jnp.zeros((LANES - TOP_K, tm), F32)
    pos_t = jnp.concatenate([pos_ref[...], pad], axis=0).T
    wts_t = jnp.concatenate([wts_ref[...], pad], axis=0).T
    blk_b, loc_b, wts_b = [], [], []
    for k in range(TOP_K):
        p = jnp.broadcast_to(pos_t[:, k:k + 1], (tm, LANES))
        blk = jnp.floor(p * (1.0 / tm))
        two = lambda v: jnp.concatenate([v.astype(BF16)] * (tm // LANES), axis=1)
        blk_b.append(two(blk))
        loc_b.append(two(p - blk * tm))
        wts_b.append(two(jnp.broadcast_to(wts_t[:, k:k + 1], (tm, LANES))))
    shared = _dot((_silu(_dot(x, swg_ref[...])) * _dot(x, swu_ref[...])).astype(BF16), swd_ref[...])
    transfer(i, slot, False)
    lane = lax.broadcasted_iota(jnp.int32, (tm, tm), 1).astype(F32).astype(BF16)
    zero = jnp.zeros((tm, tm), BF16)
    nowhere = jnp.full((tm, tm), -1.0, BF16)

    def apply(c0, width):
        blocks = []
        for b0 in range(c0, c0 + width, tm):
            comb = zero
            for k in range(TOP_K):
                loc = jnp.where(blk_b[k] == float(b0 // tm), loc_b[k], nowhere)
                comb = jnp.where(lane == loc, wts_b[k], comb)
            blocks.append(comb)
        return _dot(jnp.concatenate(blocks, axis=1), buf[slot, c0:c0 + width, :])

    routed = shared
    for c0 in range(0, always, COMBINE_CHUNK):
        routed = routed + apply(c0, min(COMBINE_CHUNK, always - c0))
    acc_ref[...] = routed
    for c0 in tail:
        @pl.when(c0 < used_ref[i])
        def _(c0=c0):
            acc_ref[...] += apply(c0, COMBINE_TAIL)
    y = x1_ref[...] + g2_ref[...] * acc_ref[...]
    o_ref[...] = _rms_norm(y, fg_ref[...])


def _moe_combine(back, used, ys, pos, wts, xm, x1, g2, final_g, swg, swu, swd, *, tiles_per_mod):
    n, d = xm.shape
    tm = pos.shape[-1]
    nt = n // tm
    gpt = SORT_ROWS // SORT_ALIGN
    row = lambda i, bk, us: (i, 0)
    full = lambda i, bk, us: (0, 0)
    tile = lambda i, bk, us: (i, 0, 0)
    mod_map = lambda i, bk, us: (i // tiles_per_mod, 0, 0)
    grid_spec = pltpu.PrefetchScalarGridSpec(
        num_scalar_prefetch=2,
        grid=(nt,),
        in_specs=[pl.BlockSpec(memory_space=pl.ANY),
                  pl.BlockSpec((None, TOP_K, tm), tile),
                  pl.BlockSpec((None, TOP_K, tm), tile),
                  pl.BlockSpec((tm, d), row),
                  pl.BlockSpec((tm, d), row),
                  pl.BlockSpec((None, 1, d), mod_map),
                  pl.BlockSpec((1, d), full),
                  pl.BlockSpec((d, SHARED_FF), full),
                  pl.BlockSpec((d, SHARED_FF), full),
                  pl.BlockSpec((SHARED_FF, d), full)],
        out_specs=pl.BlockSpec((tm, d), row),
        scratch_shapes=[pltpu.VMEM((2, SORT_ROWS, d), BF16),
                        pltpu.SemaphoreType.DMA((2,)),
                        pltpu.VMEM((tm, d), F32)])
    return pl.pallas_call(
        _moe_combine_kernel,
        out_shape=jax.ShapeDtypeStruct((n, d), F32),
        grid_spec=grid_spec,
        compiler_params=pltpu.CompilerParams(dimension_semantics=("arbitrary",),
                                             vmem_limit_bytes=VMEM_LIMIT),
        name="moe_combine",
    )(back, used, ys, pos.reshape(nt, TOP_K, tm), wts.reshape(nt, TOP_K, tm), xm, x1, g2, final_g, swg, swu, swd)


def _mix(x, mods, p, attn_fn, s0=None):
    sh1, sc1, g1, sh2, sc2, _ = mods
    gla_in, lora, q_s, k_s, v_s = _inproj(x, p["norm_attn_g"], sh1, sc1, p["w_gla"], p["w_lora"], p["w_swa"])
    if s0 is None:
        gla_out, s_f, s_b = _gla(gla_in, lora, p["waf"], p["baf"], p["wab"], p["bab"], p["gla_norm_g"])
    else:
        gla_out, s_f, s_b = _gla(gla_in, lora, p["waf"], p["baf"], p["wab"], p["bab"], p["gla_norm_g"],
                                 s0[0], s0[1])
    att_out = attn_fn(q_s, k_s, v_s)
    routed = _outproj(gla_out, att_out, x, p["w_out"], g1, sh2, sc2, p["norm_ffn_g"],
                      p["rw_cat"], p["rw_hi"], p["rbias"])
    return routed, k_s, v_s, s_f, s_b


def _moe(streams, p):
    d = D_MODEL
    (ra, _), (rb, _) = streams
    n_tiles = [r[1].shape[0] * r[1].shape[1] // MOE_TILE for r, _ in streams]
    pos_all = jnp.concatenate([r[2].reshape(-1, TOP_K, MOE_TILE) for r, _ in streams], axis=0)
    cnt_all = jnp.concatenate([r[4][..., 0].reshape(-1, N_EXPERTS) for r, _ in streams], axis=0)
    start_all = jnp.concatenate([r[5][..., 0].reshape(-1, N_EXPERTS) for r, _ in streams], axis=0)
    used = (start_all[:, -1] + cnt_all[:, -1]).astype(jnp.int32)
    xs = _moe_sort(ra[1].reshape(-1, d), rb[1].reshape(-1, d), pos_all, used)
    n_used, first, tiles, src, back = _moe_plan(cnt_all, start_all)
    ys = _moe_experts(n_used, first, tiles, src, xs, p["wg"], p["wu"], p["wd"],
                      _moe_row_tiles(cnt_all.shape[0] * MOE_TILE))
    outs = []
    tile0 = 0
    for ((x1, xm, pos, wts, cnt, start), g2), nt in zip(streams, n_tiles):
        b, t, _ = x1.shape
        tiles_per_mod = (t // MOE_TILE) if g2.shape[0] > 1 else nt
        y = _moe_combine(back[tile0:tile0 + nt].reshape(-1), used[tile0:tile0 + nt], ys, pos, wts,
                         xm.reshape(-1, d), x1.reshape(-1, d), g2, p["final_norm_g"],
                         p["swg"], p["swu"], p["swd"], tiles_per_mod=tiles_per_mod)
        outs.append(y.reshape(b, t, d))
        tile0 += nt
    return outs


def kernel(x_prompt, x_sample, c, cache_swa_k, cache_swa_v, state_gla_fwd, state_gla_bwd, c_ctx, w_ada, b_ada, norm_attn_g, norm_ffn_g, w_in, gla_wa_f, gla_ba_f, gla_wa_b, gla_ba_b, gla_norm_g, swa_sink, w_out, router_w, router_bias, exp_w_gate, exp_w_up, exp_w_down, sh_w_gate, sh_w_up, sh_w_down, final_norm_g):
    l = 0
    d = D_MODEL
    nb_ctx, t_ctx, _ = x_prompt.shape
    nb_lat, t_lat, _ = x_sample.shape

    pad = jnp.zeros((8 - 1 - nb_lat, d), F32)
    cond8 = jnp.concatenate([c_ctx[None, :], c, pad], axis=0)
    mod = _adaln(cond8, w_ada[l], b_ada[l][None, :])
    mods_ctx = [mod[0:1, i * d:(i + 1) * d][:, None, :] for i in range(6)]
    mods_lat = [mod[1:1 + nb_lat, i * d:(i + 1) * d][:, None, :] for i in range(6)]

    zeros_lora = jnp.zeros((GLA_LORA, GLA_QK), F32)
    rw = router_w[l]
    rw_hi = rw.astype(BF16)
    rw_lo = (rw - rw_hi.astype(F32)).astype(BF16)
    w_in_b = w_in[l].astype(BF16)
    p = {
        "norm_attn_g": norm_attn_g[l][None, :],
        "norm_ffn_g": norm_ffn_g[l][None, :],
        "final_norm_g": final_norm_g[None, :],
        "w_gla": w_in_b[:, :2 * GLA_QK + 2 * GLA_V],
        "w_lora": w_in_b[:, 2 * GLA_QK + 2 * GLA_V:2 * GLA_QK + 2 * GLA_V + 2 * GLA_LORA],
        "w_swa": w_in_b[:, 2 * GLA_QK + 2 * GLA_V + 2 * GLA_LORA:],
        "waf": jnp.concatenate([gla_wa_f[l], zeros_lora], axis=0).astype(BF16),
        "wab": jnp.concatenate([zeros_lora, gla_wa_b[l]], axis=0).astype(BF16),
        "baf": gla_ba_f[l][None, :],
        "bab": gla_ba_b[l][None, :],
        "gla_norm_g": gla_norm_g[l][None, :],
        "w_out": w_out[l].astype(BF16),
        "rw_cat": jnp.concatenate([rw_hi, rw_lo], axis=1),
        "rw_hi": rw_hi,
        "rbias": router_bias[l][:, None],
        "wg": exp_w_gate[l], "wu": exp_w_up[l], "wd": exp_w_down[l],
        "swg": sh_w_gate[l].astype(BF16), "swu": sh_w_up[l].astype(BF16),
        "swd": sh_w_down[l].astype(BF16),
    }
    sink = swa_sink[l]

    routed_ctx, k_c, v_c, s_f, s_b = _mix(x_prompt, mods_ctx, p, functools.partial(_attn_ctx, sink))

    cos, sin_lo, sin_hi = _rope_tables(t_lat)
    kc = cache_swa_k[:, l].reshape(nb_lat, -1, SWA_KV)
    vc = cache_swa_v[:, l].reshape(nb_lat, -1, SWA_KV)
    lat_attn = lambda q, k, v: _attn_lat(sink, q, k, v, kc, vc, cos, sin_lo, sin_hi)
    s0 = (state_gla_fwd[:, l].reshape(nb_lat, GLA_QK, GLA_DV),
          state_gla_bwd[:, l].reshape(nb_lat, GLA_QK, GLA_DV))
    routed_lat, _, _, _, _ = _mix(x_sample, mods_lat, p, lat_attn, s0)
    y_prompt, y_sample = _moe([(routed_ctx, mods_ctx[5]), (routed_lat, mods_lat[5])], p)

    new_k = k_c.reshape(nb_ctx, 1, t_ctx, SWA_KV_HEADS, SWA_HEAD_DIM)
    new_v = v_c.reshape(nb_ctx, 1, t_ctx, SWA_KV_HEADS, SWA_HEAD_DIM)
    new_sf = s_f.reshape(nb_ctx, 1, GLA_HEADS, GLA_DK, GLA_DV)
    new_sb = s_b.reshape(nb_ctx, 1, GLA_HEADS, GLA_DK, GLA_DV)
    return (y_prompt, y_sample, new_k, new_v, new_sf, new_sb)
```

```python
import functools

import jax
import jax.numpy as jnp
from jax import lax
from jax.experimental import pallas as pl
from jax.experimental.pallas import tpu as pltpu

F32 = jnp.float32
BF16 = jnp.bfloat16

D_MODEL = 1024
GLA_HEADS = 4
GLA_DK = 64
GLA_DV = 128
GLA_LORA = 16
GLA_GATE_NORM = 16.0
GLA_CHUNK = 64
GLA_QK = GLA_HEADS * GLA_DK
GLA_V = GLA_HEADS * GLA_DV
SWA_HEAD_DIM = 64
SWA_HEADS = 8
SWA_KV_HEADS = 2
SWA_Q = SWA_HEADS * SWA_HEAD_DIM
SWA_KV = SWA_KV_HEADS * SWA_HEAD_DIM
ATTN_BLOCK = 128
GRID_W = 64
ROPE_BASE = 10000.0
N_EXPERTS = 64
TOP_K = 8
N_EXPERT_GROUPS = 8
TOPK_GROUPS = 4
EXPERT_FF = 128
SHARED_FF = 256
ROUTED_SCALE = 2.5
EPS = 1e-6

LANES = 128
VMEM_LIMIT = 56 * 1024 * 1024

NEG_INF = float("-inf")


def _dot(a, b):
    return jnp.dot(a, b, preferred_element_type=F32)


def _dot_nt(a, b):
    return lax.dot_general(a, b, (((1,), (1,)), ((), ())), preferred_element_type=F32)


def _split_hi_lo(x):
    hi = x.astype(BF16)
    lo = (x - hi.astype(F32)).astype(BF16)
    return hi, lo


def _sigmoid(x):
    return 1.0 / (1.0 + jnp.exp(-x))


def _silu(x):
    return x * _sigmoid(x)


def _rms_norm(x, g):
    ms = jnp.mean(x * x, axis=-1, keepdims=True)
    return x * lax.rsqrt(ms + EPS) * g


def _adaln_kernel(c_ref, w_ref, b_ref, o_ref):
    a_hi, a_lo = _split_hi_lo(_silu(c_ref[...]))
    w_hi, w_lo = _split_hi_lo(w_ref[...])
    o_ref[...] = _dot(a_hi, w_hi) + _dot(a_lo, w_hi) + _dot(a_hi, w_lo) + b_ref[...]


def _adaln(cond8, w_ada, b_ada):
    n = w_ada.shape[1]
    tn = 1536
    return pl.pallas_call(
        _adaln_kernel,
        out_shape=jax.ShapeDtypeStruct((8, n), F32),
        grid=(n // tn,),
        in_specs=[pl.BlockSpec((8, D_MODEL), lambda j: (0, 0)),
                  pl.BlockSpec((D_MODEL, tn), lambda j: (0, j)),
                  pl.BlockSpec((1, tn), lambda j: (0, j))],
        out_specs=pl.BlockSpec((8, tn), lambda j: (0, j)),
        compiler_params=pltpu.CompilerParams(dimension_semantics=("arbitrary",),
                                             vmem_limit_bytes=VMEM_LIMIT),
        name="adaln",
    )(cond8, w_ada, b_ada)


def _inproj_kernel(x_ref, g_ref, sh_ref, sc_ref, wg_ref, wl_ref, ws_ref,
                   gla_ref, lora_ref, q_ref, k_ref, v_ref):
    bb, tb, d = x_ref.shape
    x = x_ref[...].reshape(bb * tb, d)
    h = _rms_norm(x, g_ref[...]) * (1.0 + sc_ref[...]) + sh_ref[...]
    hb = h.astype(BF16)
    gla_ref[...] = _dot(hb, wg_ref[...]).reshape(gla_ref.shape)
    lora_ref[...] = _dot(hb, wl_ref[...]).reshape(lora_ref.shape)
    s = _dot(hb, ws_ref[...])
    q_ref[...] = s[:, :SWA_Q].reshape(q_ref.shape)
    k_ref[...] = s[:, SWA_Q:SWA_Q + SWA_KV].reshape(k_ref.shape)
    v_ref[...] = s[:, SWA_Q + SWA_KV:].reshape(v_ref.shape)


INPROJ_TILE = 512


def _inproj(x, g, sh, sc, w_gla, w_lora, w_swa):
    b, t, d = x.shape
    nmod = sh.shape[0]
    tb = min(t, INPROJ_TILE)
    bb = INPROJ_TILE // tb if nmod == 1 else 1
    mod_map = (lambda i, j: (i, 0, 0)) if nmod > 1 else (lambda i, j: (0, 0, 0))
    row = lambda i, j: (i, j, 0)
    full = lambda i, j: (0, 0)
    n_gla = w_gla.shape[1]
    n_lora = w_lora.shape[1]
    return pl.pallas_call(
        _inproj_kernel,
        out_shape=(jax.ShapeDtypeStruct((b, t, n_gla), F32),
                   jax.ShapeDtypeStruct((b, t, n_lora), F32),
                   jax.ShapeDtypeStruct((b, t, SWA_Q), F32),
                   jax.ShapeDtypeStruct((b, t, SWA_KV), F32),
                   jax.ShapeDtypeStruct((b, t, SWA_KV), F32)),
        grid=(b // bb, t // tb),
        in_specs=[pl.BlockSpec((bb, tb, d), row),
                  pl.BlockSpec((1, d), full),
                  pl.BlockSpec((None, 1, d), mod_map),
                  pl.BlockSpec((None, 1, d), mod_map),
                  pl.BlockSpec((d, n_gla), full),
                  pl.BlockSpec((d, n_lora), full),
                  pl.BlockSpec((d, w_swa.shape[1]), full)],
        out_specs=(pl.BlockSpec((bb, tb, n_gla), row),
                   pl.BlockSpec((bb, tb, n_lora), row),
                   pl.BlockSpec((bb, tb, SWA_Q), row),
                   pl.BlockSpec((bb, tb, SWA_KV), row),
                   pl.BlockSpec((bb, tb, SWA_KV), row)),
        compiler_params=pltpu.CompilerParams(dimension_semantics=("arbitrary", "arbitrary"),
                                             vmem_limit_bytes=VMEM_LIMIT),
        name="inproj",
    )(x, g, sh, sc, w_gla, w_lora, w_swa)


SCAN_UNROLL = 4
OUT_UNROLL = 4


def _log_sigmoid(x):
    return jnp.minimum(x, 0.0) - jnp.log(1.0 + jnp.exp(-jnp.abs(x)))


def _heads_to_rows(x):
    return jnp.concatenate([x[:, h * LANES:(h + 1) * LANES] for h in range(GLA_HEADS)], axis=0)


def _rows_to_heads(x, c):
    return jnp.concatenate([x[h * c:(h + 1) * c, :] for h in range(GLA_HEADS)], axis=1)


def _gla_kernel(has_init, q_ref, k_ref, v_ref, g_ref, lora_ref, waf_ref, baf_ref, wab_ref, bab_ref,
                ng_ref, *rest):
    if has_init:
        s0f_ref, s0b_ref, *rest = rest
    (out_ref, sf_ref, sb_ref, laf_ref, lab_ref, oacc_ref, qtf_ref, qtb_ref, saf_ref, sab_ref,
     stf_ref, stb_ref) = rest
    t = q_ref.shape[0]
    c = GLA_CHUNK
    n = t // c
    hc = GLA_HEADS * c

    lora = lora_ref[...].astype(BF16)
    laf_ref[...] = _log_sigmoid(_dot(lora, waf_ref[...]) + baf_ref[...]) * (1.0 / GLA_GATE_NORM)
    lab_ref[...] = _log_sigmoid(_dot(lora, wab_ref[...]) + bab_ref[...]) * (1.0 / GLA_GATE_NORM)

    if has_init:
        stf_ref[...] = s0f_ref[...].T
        stb_ref[...] = s0b_ref[...].T
    else:
        stf_ref[...] = jnp.zeros_like(stf_ref)
        stb_ref[...] = jnp.zeros_like(stb_ref)
    oacc_ref[...] = jnp.zeros_like(oacc_ref)

    r64 = lax.broadcasted_iota(jnp.int32, (c, c), 0)
    c64 = lax.broadcasted_iota(jnp.int32, (c, c), 1)
    tri_f = jnp.where(c64 <= r64, 1.0, 0.0).astype(BF16)
    tri_b = jnp.where(c64 >= r64, 1.0, 0.0).astype(BF16)
    rr = lax.broadcasted_iota(jnp.int32, (hc, hc), 0)
    cc = lax.broadcasted_iota(jnp.int32, (hc, hc), 1)
    same_head = (rr >> 6) == (cc >> 6)
    keep_f = same_head & ((rr & (c - 1)) >= (cc & (c - 1)))
    keep_b = same_head & ((rr & (c - 1)) <= (cc & (c - 1)))
    head_mask = jnp.where(same_head, 1.0, 0.0).astype(BF16)
    norm_g = ng_ref[...]

    def chunk_rows(ci):
        return pl.ds(pl.multiple_of(ci * c, c), c)

    def tile_heads(x):
        x4 = jnp.concatenate([x] * GLA_HEADS, axis=0)
        return jnp.where(same_head, x4, 0.0).astype(BF16)

    def scan_step(i, carry):
        dirs = []
        for u in range(SCAN_UNROLL):
            dirs += [(SCAN_UNROLL * i + u, laf_ref, tri_f, keep_f, c - 1, stf_ref, saf_ref, qtf_ref),
                     (n - 1 - SCAN_UNROLL * i - u, lab_ref, tri_b, keep_b, 0, stb_ref, sab_ref, qtb_ref)]
        cums = []
        for ci, la_ref, tri, _, _, _, _, _ in dirs:
            la_hi, la_lo = _split_hi_lo(la_ref[chunk_rows(ci), :])
            cums.append(_dot(tri, la_hi) + _dot(tri, la_lo))
        ops = []
        for (ci, _, _, _, last_row, _, _, qt_ref), cum in zip(dirs, cums):
            sl = chunk_rows(ci)
            tot = cum[last_row:last_row + 1, :]
            kc = k_ref[sl, :]
            qt = q_ref[sl, :] * (GLA_DK ** -0.5) * jnp.exp(cum)
            qt_ref[sl, :] = qt.astype(BF16)
            v_rows = _heads_to_rows(v_ref[sl, :])
            ops.append((tot, tile_heads(qt), tile_heads(kc * jnp.exp(-cum)),
                        tile_heads(kc * jnp.exp(tot - cum)), v_rows))
        atts = [_dot_nt(q4, k4) for _, q4, k4, _, _ in ops]
        incs = []
        for (_, _, _, keep, _, _, _, _), (_, _, _, kd4, v_rows), att in zip(dirs, ops, atts):
            att = jnp.where(keep, att, 0.0).astype(BF16)
            incs.append((_dot(att, v_rows.astype(BF16)), _dot(v_rows.T.astype(BF16), kd4)))
        for (ci, _, _, _, _, st_ref, snap_ref, _), (tot, _, _, _, _), (o_intra, st_inc) in zip(dirs, ops, incs):
            oacc_ref[ci] += o_intra
            st = st_ref[...]
            snap_ref[ci] = st.astype(BF16)
            st_ref[...] = jnp.exp(tot) * st + st_inc
        return carry

    def tile_heads_bf16(x):
        return jnp.concatenate([x] * GLA_HEADS, axis=0) * head_mask

    def out_step(i, carry):
        chunks = [OUT_UNROLL * i + u for u in range(OUT_UNROLL)]
        inter = []
        for ci in chunks:
            sl = chunk_rows(ci)
            q4 = jnp.concatenate([tile_heads_bf16(qtf_ref[sl, :]), tile_heads_bf16(qtb_ref[sl, :])], axis=1)
            st = jnp.concatenate([saf_ref[ci], sab_ref[ci]], axis=1)
            inter.append(_dot_nt(q4, st))
        for ci, o_inter in zip(chunks, inter):
            sl = chunk_rows(ci)
            on = _rms_norm(oacc_ref[ci] + o_inter, norm_g)
            gate = _silu(_heads_to_rows(g_ref[sl, :]))
            out_ref[sl, :] = _rows_to_heads(on * gate, c)
        return carry

    lax.fori_loop(0, n // SCAN_UNROLL, scan_step, 0)
    lax.fori_loop(0, n // OUT_UNROLL, out_step, 0)
    sf_ref[...] = stf_ref[...].T
    sb_ref[...] = stb_ref[...].T


def _gla(gla_in, lora, waf, baf, wab, bab, norm_g, s0f=None, s0b=None):
    b, t, _ = gla_in.shape
    has_init = s0f is not None
    n = t // GLA_CHUNK
    bmap = lambda i: (i, 0, 0)
    full = lambda i: (0, 0)
    in_specs = [pl.BlockSpec((None, t, GLA_QK), lambda i: (i, 0, 0)),
                pl.BlockSpec((None, t, GLA_QK), lambda i: (i, 0, 1)),
                pl.BlockSpec((None, t, GLA_V), lambda i: (i, 0, 1)),
                pl.BlockSpec((None, t, GLA_V), lambda i: (i, 0, 2)),
                pl.BlockSpec((None, t, 2 * GLA_LORA), bmap),
                pl.BlockSpec((2 * GLA_LORA, GLA_QK), full),
                pl.BlockSpec((1, GLA_QK), full),
                pl.BlockSpec((2 * GLA_LORA, GLA_QK), full),
                pl.BlockSpec((1, GLA_QK), full),
                pl.BlockSpec((1, GLA_DV), full)]
    args = [gla_in, gla_in, gla_in, gla_in, lora, waf, baf, wab, bab, norm_g]
    if has_init:
        in_specs += [pl.BlockSpec((None, GLA_QK, GLA_DV), bmap)] * 2
        args += [s0f, s0b]
    return pl.pallas_call(
        functools.partial(_gla_kernel, has_init),
        out_shape=(jax.ShapeDtypeStruct((b, t, GLA_V), F32),
                   jax.ShapeDtypeStruct((b, GLA_QK, GLA_DV), F32),
                   jax.ShapeDtypeStruct((b, GLA_QK, GLA_DV), F32)),
        grid=(b,),
        in_specs=in_specs,
        out_specs=(pl.BlockSpec((None, t, GLA_V), bmap),
                   pl.BlockSpec((None, GLA_QK, GLA_DV), bmap),
                   pl.BlockSpec((None, GLA_QK, GLA_DV), bmap)),
        scratch_shapes=[pltpu.VMEM((t, GLA_QK), F32),
                        pltpu.VMEM((t, GLA_QK), F32),
                        pltpu.VMEM((n, GLA_HEADS * GLA_CHUNK, GLA_DV), F32),
                        pltpu.VMEM((t, GLA_QK), BF16),
                        pltpu.VMEM((t, GLA_QK), BF16),
                        pltpu.VMEM((n, GLA_DV, GLA_QK), BF16),
                        pltpu.VMEM((n, GLA_DV, GLA_QK), BF16),
                        pltpu.VMEM((GLA_DV, GLA_QK), F32),
                        pltpu.VMEM((GLA_DV, GLA_QK), F32)],
        compiler_params=pltpu.CompilerParams(dimension_semantics=("arbitrary",),
                                             vmem_limit_bytes=VMEM_LIMIT),
        name="gla",
    )(*args)


def _dup_groups(x):
    lo = lax.broadcasted_iota(jnp.int32, x.shape, 1) < SWA_HEAD_DIM
    xr = pltpu.roll(x, SWA_HEAD_DIM, axis=1)
    return jnp.where(lo, x, xr), jnp.where(lo, xr, x)


def _pairs_attention(qps, sinks, k_dups, vt_dups, masks):
    nq = qps[0].shape[0]
    lo = lax.broadcasted_iota(jnp.int32, (nq, LANES), 1) < SWA_HEAD_DIM
    even = lax.broadcasted_iota(jnp.int32, (1, 2 * nq), 1) < nq
    scores = []
    for qp, k_dup in zip(qps, k_dups):
        q2 = jnp.concatenate([jnp.where(lo, qp, 0.0), jnp.where(lo, 0.0, qp)], axis=0).astype(BF16)
        scores.append(_dot_nt(k_dup, q2))
    probs = []
    for s, (sink_even, sink_odd), mask in zip(scores, sinks, masks):
        if mask is not None:
            s = jnp.where(mask, s, NEG_INF)
        sink = jnp.where(even, sink_even, sink_odd)
        m = jnp.maximum(jnp.max(s, axis=0, keepdims=True), sink)
        p = jnp.exp(s - m)
        denom = jnp.sum(p, axis=0, keepdims=True) + jnp.exp(sink - m)
        probs.append((p.astype(BF16), 1.0 / denom))
    outs = []
    for (p, rdenom), vt_dup in zip(probs, vt_dups):
        o = _dot(vt_dup, p) * rdenom
        outs.append(jnp.concatenate([o[:SWA_HEAD_DIM, :nq], o[SWA_HEAD_DIM:, nq:]], axis=0).T)
    return outs


CTX_BATCH = 4


def _attn_ctx_kernel(sink_ref, q_ref, k_ref, v_ref, o_ref):
    scale = SWA_HEAD_DIM ** -0.5
    pairs = range(SWA_HEADS // 2)
    items = [(bb, pr) for bb in range(q_ref.shape[0]) for pr in pairs]
    kd = [[x.astype(BF16) for x in _dup_groups(k_ref[bb])] for bb in range(q_ref.shape[0])]
    vt = [[x.T.astype(BF16) for x in _dup_groups(v_ref[bb])] for bb in range(q_ref.shape[0])]
    outs = _pairs_attention([q_ref[bb, :, pr * LANES:(pr + 1) * LANES] * scale for bb, pr in items],
                            [(sink_ref[2 * pr], sink_ref[2 * pr + 1]) for _, pr in items],
                            [kd[bb][pr // 2] for bb, pr in items], [vt[bb][pr // 2] for bb, pr in items],
                            [None] * len(items))
    for (bb, pr), out in zip(items, outs):
        o_ref[bb, :, pr * LANES:(pr + 1) * LANES] = out


def _attn_ctx(sink, q, k, v):
    b, t, _ = q.shape
    bmap = lambda i: (i, 0, 0)
    return pl.pallas_call(
        _attn_ctx_kernel,
        out_shape=jax.ShapeDtypeStruct((b, t, SWA_Q), F32),
        grid=(b // CTX_BATCH,),
        in_specs=[pl.BlockSpec(memory_space=pltpu.SMEM),
                  pl.BlockSpec((CTX_BATCH, t, SWA_Q), bmap),
                  pl.BlockSpec((CTX_BATCH, t, SWA_KV), bmap),
                  pl.BlockSpec((CTX_BATCH, t, SWA_KV), bmap)],
        out_specs=pl.BlockSpec((CTX_BATCH, t, SWA_Q), bmap),
        compiler_params=pltpu.CompilerParams(dimension_semantics=("arbitrary",),
                                             vmem_limit_bytes=VMEM_LIMIT),
        name="attn_ctx",
    )(sink, q, k, v)


LAT_BLOCKS = 2


def _rope(x, cos, sin_lo, sin_hi):
    return x * cos + pltpu.roll(x, LANES - 16, axis=1) * sin_lo + pltpu.roll(x, 16, axis=1) * sin_hi


def _attn_lat_kernel(sink_ref, q_ref, k_ref, v_ref, kc_ref, vc_ref, cos_ref, sl_ref, sh_ref,
                     o_ref, kw_ref, vw_ref):
    t = q_ref.shape[0]
    ab = ATTN_BLOCK
    nb = t // ab
    scale = SWA_HEAD_DIM ** -0.5

    k_rot = _dup_groups(_rope(k_ref[...], cos_ref[...], sl_ref[...], sh_ref[...]))
    v_dup = _dup_groups(v_ref[...])
    zeros = jnp.zeros((ab, LANES), BF16)
    for grp in range(SWA_KV_HEADS):
        kw_ref[grp, 0:ab, :] = zeros
        kw_ref[grp, ab:ab + t, :] = k_rot[grp].astype(BF16)
        kw_ref[grp, ab + t:, :] = zeros
        vw_ref[grp, 0] = zeros
        for blk in range(nb):
            vw_ref[grp, blk + 1] = v_dup[grp][blk * ab:(blk + 1) * ab, :].T.astype(BF16)
        vw_ref[grp, nb + 1] = zeros
    kc = [x.astype(BF16) for x in _dup_groups(kc_ref[...])]
    vct = [x.T.astype(BF16) for x in _dup_groups(vc_ref[...])]
    lc = kc_ref.shape[0]

    key = lax.broadcasted_iota(jnp.int32, (lc + 3 * ab, 2 * ab), 0) - lc
    tq = lax.broadcasted_iota(jnp.int32, (lc + 3 * ab, 2 * ab), 1) & (ab - 1)
    band = (key < 0) | (jnp.abs(tq + ab - key) <= ab)

    def block(it, carry):
        pairs = range(SWA_HEADS // 2)
        qps, sinks, k_dups, vt_dups, masks, places = [], [], [], [], [], []
        for u in range(LAT_BLOCKS):
            nq = it * LAT_BLOCKS + u
            row0 = pl.multiple_of(nq * ab, ab)
            s_abs = key + (nq - 1) * ab
            mask = band & ((key < 0) | ((s_abs >= 0) & (s_abs < t)))
            cos = cos_ref[pl.ds(row0, ab), :]
            s_lo = sl_ref[pl.ds(row0, ab), :]
            s_hi = sh_ref[pl.ds(row0, ab), :]
            k_all = [jnp.concatenate([kc[grp], kw_ref[grp, pl.ds(row0, 3 * ab), :]], axis=0)
                     for grp in range(SWA_KV_HEADS)]
            vt_all = [jnp.concatenate([vct[grp], vw_ref[grp, nq], vw_ref[grp, nq + 1], vw_ref[grp, nq + 2]],
                                      axis=1) for grp in range(SWA_KV_HEADS)]
            for pr in pairs:
                qps.append(_rope(q_ref[pl.ds(row0, ab), pr * LANES:(pr + 1) * LANES], cos, s_lo, s_hi) * scale)
                sinks.append((sink_ref[2 * pr], sink_ref[2 * pr + 1]))
                k_dups.append(k_all[pr // 2])
                vt_dups.append(vt_all[pr // 2])
                masks.append(mask)
                places.append((row0, pr))
        outs = _pairs_attention(qps, sinks, k_dups, vt_dups, masks)
        for (row0, pr), out in zip(places, outs):
            o_ref[pl.ds(row0, ab), pr * LANES:(pr + 1) * LANES] = out
        return carry

    lax.fori_loop(0, nb // LAT_BLOCKS, block, 0)


def _attn_lat(sink, q, k, v, kc, vc, cos, sin_lo, sin_hi):
    b, t, _ = q.shape
    lc = kc.shape[1]
    bmap = lambda i: (i, 0, 0)
    full = lambda i: (0, 0)
    return pl.pallas_call(
        _attn_lat_kernel,
        out_shape=jax.ShapeDtypeStruct((b, t, SWA_Q), F32),
        grid=(b,),
        in_specs=[pl.BlockSpec(memory_space=pltpu.SMEM),
                  pl.BlockSpec((None, t, SWA_Q), bmap),
                  pl.BlockSpec((None, t, SWA_KV), bmap),
                  pl.BlockSpec((None, t, SWA_KV), bmap),
                  pl.BlockSpec((None, lc, SWA_KV), bmap),
                  pl.BlockSpec((None, lc, SWA_KV), bmap),
                  pl.BlockSpec((t, LANES), full),
                  pl.BlockSpec((t, LANES), full),
                  pl.BlockSpec((t, LANES), full)],
        out_specs=pl.BlockSpec((None, t, SWA_Q), bmap),
        scratch_shapes=[pltpu.VMEM((SWA_KV_HEADS, t + 2 * ATTN_BLOCK, LANES), BF16),
                        pltpu.VMEM((SWA_KV_HEADS, t // ATTN_BLOCK + 2, LANES, ATTN_BLOCK), BF16)],
        compiler_params=pltpu.CompilerParams(dimension_semantics=("arbitrary",),
                                             vmem_limit_bytes=VMEM_LIMIT),
        name="attn_lat",
    )(sink, q, k, v, kc, vc, cos, sin_lo, sin_hi)


def _rope_tables(t):
    half = SWA_HEAD_DIM // 2
    quarter = half // 2
    pos = jnp.arange(t)
    row = (pos // GRID_W).astype(F32)
    col = (pos % GRID_W).astype(F32)
    inv_freq = ROPE_BASE ** (-jnp.arange(quarter, dtype=F32) / quarter)
    lane = jnp.arange(LANES)
    d = lane % SWA_HEAD_DIM
    freq = inv_freq[d % quarter]
    use_row = (d < half)
    ang = jnp.where(use_row[None, :], row[:, None], col[:, None]) * freq[None, :]
    cos = jnp.cos(ang)
    sin = jnp.sin(ang)
    lower = (d % half) < quarter
    return cos, jnp.where(lower[None, :], -sin, 0.0), jnp.where(lower[None, :], 0.0, sin)


def _route(sel, scores):
    n = sel.shape[1]
    gsz = N_EXPERTS // N_EXPERT_GROUPS

    def first_max(x, idx, size):
        m = jnp.max(x, axis=0, keepdims=True)
        first = jnp.min(jnp.where(x == m, idx, float(size)), axis=0, keepdims=True)
        return m, idx == first

    i8 = lax.broadcasted_iota(jnp.int32, (gsz, n), 0).astype(F32)
    rows = []
    for g in range(N_EXPERT_GROUPS):
        slab = sel[g * gsz:(g + 1) * gsz, :]
        m1, hit = first_max(slab, i8, gsz)
        m2 = jnp.max(jnp.where(hit, NEG_INF, slab), axis=0, keepdims=True)
        rows.append(m1 + m2)
    gscore = jnp.concatenate(rows, axis=0)
    gsel = jnp.zeros((N_EXPERT_GROUPS, n), F32)
    for _ in range(TOPK_GROUPS):
        _, hit = first_max(gscore, i8, N_EXPERT_GROUPS)
        gsel = jnp.where(hit, 1.0, gsel)
        gscore = jnp.where(hit, NEG_INF, gscore)
    emask = jnp.concatenate(
        [jnp.broadcast_to(gsel[g:g + 1, :], (gsz, n)) for g in range(N_EXPERT_GROUPS)], axis=0)
    cand = jnp.where(emask > 0.5, sel, NEG_INF)
    ie = lax.broadcasted_iota(jnp.int32, (N_EXPERTS, n), 0).astype(F32)
    w = jnp.zeros((N_EXPERTS, n), F32)
    chosen = jnp.zeros((N_EXPERTS, n), F32)
    hits = []
    for _ in range(TOP_K):
        _, hit = first_max(cand, ie, N_EXPERTS)
        hits.append(hit)
        w = jnp.where(hit, scores, w)
        chosen = jnp.where(hit, 1.0, chosen)
        cand = jnp.where(hit, NEG_INF, cand)
    gates = w / jnp.sum(w, axis=0, keepdims=True) * ROUTED_SCALE

    s_idx = lax.broadcasted_iota(jnp.int32, (n, n), 0)
    t_idx = lax.broadcasted_iota(jnp.int32, (n, n), 1)
    tile_shift = MOE_TILE.bit_length() - 1
    before = jnp.where((s_idx < t_idx) & ((s_idx >> tile_shift) == (t_idx >> tile_shift)), 1.0, 0.0)
    rank = _dot(chosen.astype(BF16), before.astype(BF16))
    e_row = lax.broadcasted_iota(jnp.int32, (N_EXPERTS, N_EXPERTS), 0)
    e_col = lax.broadcasted_iota(jnp.int32, (N_EXPERTS, N_EXPERTS), 1)
    below = jnp.where(e_col < e_row, 1.0, 0.0).astype(BF16)
    lane_tile = lax.broadcasted_iota(jnp.int32, (1, n), 1) >> tile_shift
    sizes, starts = [], []
    first_row = jnp.zeros((N_EXPERTS, n), F32)
    for ti in range(n // MOE_TILE):
        count = jnp.sum(chosen[:, ti * MOE_TILE:(ti + 1) * MOE_TILE], axis=1, keepdims=True)
        padded = jnp.floor((count + (SORT_ALIGN - 1)) * (1.0 / SORT_ALIGN)) * SORT_ALIGN
        padded = jnp.broadcast_to(padded, (N_EXPERTS, LANES))
        start = _dot(below, padded.astype(BF16))
        first_row = jnp.where(lane_tile == ti, start[:, 0:1], first_row)
        sizes.append(padded)
        starts.append(start)
    row = first_row + rank
    pos = jnp.concatenate([jnp.sum(jnp.where(h, row, 0.0), axis=0, keepdims=True) for h in hits], axis=0)
    wts = jnp.concatenate([jnp.sum(jnp.where(h, gates, 0.0), axis=0, keepdims=True) for h in hits], axis=0)
    return pos, wts, sizes, starts


def _outproj_kernel(gla_ref, att_ref, x_ref, wo_ref, g1_ref, sh_ref, sc_ref, ng_ref, rw_ref, rwh_ref,
                    rb_ref, x1_ref, xm_ref, pos_ref, wts_ref, cnt_ref, start_ref):
    bb, tb, d = x_ref.shape
    tm = bb * tb
    y = (_dot(gla_ref[...].reshape(tm, GLA_V).astype(BF16), wo_ref[0:GLA_V, :])
         + _dot(att_ref[...].reshape(tm, SWA_Q).astype(BF16), wo_ref[GLA_V:, :]))
    x1 = x_ref[...].reshape(tm, d) + g1_ref[...] * y
    x1_ref[...] = x1.reshape(bb, tb, d)
    xm = _rms_norm(x1, ng_ref[...]) * (1.0 + sc_ref[...]) + sh_ref[...]
    xm_hi, xm_lo = _split_hi_lo(xm)
    xm_ref[...] = xm_hi.reshape(bb, tb, d)
    lg = _dot(xm_hi, rw_ref[...])
    logits = lg[:, :N_EXPERTS] + lg[:, N_EXPERTS:] + _dot(xm_lo, rwh_ref[...])
    lt = jnp.concatenate([logits, jnp.zeros((tm, LANES - N_EXPERTS), F32)], axis=1).T[:N_EXPERTS, :]
    scores = _sigmoid(lt)
    pos, wts, sizes, starts = _route(scores + rb_ref[...], scores)
    tiles_per_batch = tb // MOE_TILE
    for ti in range(tm // MOE_TILE):
        at = (ti // tiles_per_batch, ti % tiles_per_batch)
        pos_ref[at] = pos[:, ti * MOE_TILE:(ti + 1) * MOE_TILE]
        wts_ref[at] = wts[:, ti * MOE_TILE:(ti + 1) * MOE_TILE]
        cnt_ref[at] = sizes[ti]
        start_ref[at] = starts[ti]


OUTPROJ_TILE = 512


def _outproj(gla_out, att_out, x, w_out, g1, sh2, sc2, norm_g, rw_cat, rw_hi, rbias):
    b, t, d = x.shape
    nmod = g1.shape[0]
    tb = min(t, OUTPROJ_TILE)
    bb = OUTPROJ_TILE // tb if nmod == 1 else 1
    tpb = tb // MOE_TILE
    mod_map = (lambda i, j: (i, 0, 0)) if nmod > 1 else (lambda i, j: (0, 0, 0))
    row = lambda i, j: (i, j, 0)
    full = lambda i, j: (0, 0)
    tile = lambda i, j: (i, j, 0, 0)
    nt = t // MOE_TILE
    return pl.pallas_call(
        _outproj_kernel,
        out_shape=(jax.ShapeDtypeStruct((b, t, d), F32),
                   jax.ShapeDtypeStruct((b, t, d), BF16),
                   jax.ShapeDtypeStruct((b, nt, TOP_K, MOE_TILE), F32),
                   jax.ShapeDtypeStruct((b, nt, TOP_K, MOE_TILE), F32),
                   jax.ShapeDtypeStruct((b, nt, N_EXPERTS, LANES), F32),
                   jax.ShapeDtypeStruct((b, nt, N_EXPERTS, LANES), F32)),
        grid=(b // bb, t // tb),
        in_specs=[pl.BlockSpec((bb, tb, GLA_V), row),
                  pl.BlockSpec((bb, tb, SWA_Q), row),
                  pl.BlockSpec((bb, tb, d), row),
                  pl.BlockSpec((d, d), full),
                  pl.BlockSpec((None, 1, d), mod_map),
                  pl.BlockSpec((None, 1, d), mod_map),
                  pl.BlockSpec((None, 1, d), mod_map),
                  pl.BlockSpec((1, d), full),
                  pl.BlockSpec((d, 2 * N_EXPERTS), full),
                  pl.BlockSpec((d, N_EXPERTS), full),
                  pl.BlockSpec((N_EXPERTS, 1), full)],
        out_specs=(pl.BlockSpec((bb, tb, d), row),
                   pl.BlockSpec((bb, tb, d), row),
                   pl.BlockSpec((bb, tpb, TOP_K, MOE_TILE), tile),
                   pl.BlockSpec((bb, tpb, TOP_K, MOE_TILE), tile),
                   pl.BlockSpec((bb, tpb, N_EXPERTS, LANES), tile),
                   pl.BlockSpec((bb, tpb, N_EXPERTS, LANES), tile)),
        compiler_params=pltpu.CompilerParams(dimension_semantics=("arbitrary", "arbitrary"),
                                             vmem_limit_bytes=VMEM_LIMIT),
        name="outproj",
    )(gla_out, att_out, x, w_out, g1, sh2, sc2, norm_g, rw_cat, rw_hi, rbias)


MOE_TILE = 256
SORT_ALIGN = 16
SORT_ROWS = 3072
ROW_TILE = 512
GATHER_SLOTS = 5
FFN_CHAINS = 4
COMBINE_CHUNK = 1024
ALWAYS_ROWS = 2560
COMBINE_TAIL = 512


def _moe_sort_kernel(tiles_a, used_ref, xa_ref, xb_ref, pos_ref, xs_ref):
    i = pl.program_id(0)
    x = jnp.where(i < tiles_a, xa_ref[...], xb_ref[...])
    pos = pos_ref[...]
    tm = x.shape[0]
    used = used_ref[i]

    rows = lax.broadcasted_iota(jnp.int32, (tm, tm), 0).astype(F32).astype(BF16)
    one = jnp.ones((tm, tm), BF16)

    def fill(blk):
        local = (pos - float(blk * tm)).astype(BF16)
        onehot = jnp.zeros((tm, tm), BF16)
        for k in range(TOP_K):
            onehot = jnp.where(rows == local[k:k + 1, :], one, onehot)
        xs_ref[blk * tm:(blk + 1) * tm, :] = _dot(onehot, x).astype(BF16)

    for blk in range(SORT_ROWS // tm):
        if (blk + 1) * tm <= ALWAYS_ROWS:
            fill(blk)
        else:
            pl.when(blk * tm < used)(functools.partial(fill, blk))

            @pl.when(blk * tm >= used)
            def _():
                xs_ref[blk * tm:(blk + 1) * tm, :] = jnp.zeros((tm, D_MODEL), BF16)


def _moe_sort(xm_a, xm_b, pos, used):
    d = xm_a.shape[1]
    nt, _, tm = pos.shape
    tiles_a = xm_a.shape[0] // tm
    grid_spec = pltpu.PrefetchScalarGridSpec(
        num_scalar_prefetch=1,
        grid=(nt,),
        in_specs=[pl.BlockSpec((tm, d), lambda i, u: (jnp.minimum(i, tiles_a - 1), 0)),
                  pl.BlockSpec((tm, d), lambda i, u: (jnp.maximum(i - tiles_a, 0), 0)),
                  pl.BlockSpec((None, TOP_K, tm), lambda i, u: (i, 0, 0))],
        out_specs=pl.BlockSpec((SORT_ROWS, d), lambda i, u: (i, 0)))
    return pl.pallas_call(
        functools.partial(_moe_sort_kernel, tiles_a),
        out_shape=jax.ShapeDtypeStruct((nt * SORT_ROWS, d), BF16),
        grid_spec=grid_spec,
        compiler_params=pltpu.CompilerParams(dimension_semantics=("arbitrary",),
                                             vmem_limit_bytes=VMEM_LIMIT),
        name="moe_sort",
    )(used, xm_a, xm_b, pos)


def _moe_row_tiles(n_tokens):
    rows = n_tokens * TOP_K + (n_tokens // MOE_TILE) * N_EXPERTS * (SORT_ALIGN - 1) + N_EXPERTS * (ROW_TILE - 1)
    return -(-rows // ROW_TILE) + GATHER_SLOTS - 1


PLAN_CHUNK = 1280


def _int_dot_r(a, onehot):
    hi = jnp.floor(a * (1.0 / 256.0))
    return _dot(hi.astype(BF16), onehot) * 256.0 + _dot((a - hi * 256.0).astype(BF16), onehot)


def _int_dot_l(onehot, b):
    hi = jnp.floor(b * (1.0 / 256.0))
    return _dot(onehot, hi.astype(BF16)) * 256.0 + _dot(onehot, (b - hi * 256.0).astype(BF16))


def _moe_plan_kernel(cnt_ref, start_ref, src_ref, first_ref, tiles_ref, nu_ref, back_ref):
    nt, ne = cnt_ref.shape
    gpt = SORT_ROWS // SORT_ALIGN
    gpr = ROW_TILE // SORT_ALIGN
    gc = cnt_ref[...] * (1.0 / SORT_ALIGN)
    ls = start_ref[...] * (1.0 / SORT_ALIGN)

    def transpose(x):
        x = jnp.concatenate([x, jnp.zeros((nt, LANES - ne), F32)], axis=1)
        x = jnp.concatenate([x, jnp.zeros((LANES - nt, LANES), F32)], axis=0)
        return x.T[:ne, :nt]

    def tri(n, keep):
        return jnp.where(keep(lax.broadcasted_iota(jnp.int32, (n, n), 0),
                              lax.broadcasted_iota(jnp.int32, (n, n), 1)), 1.0, 0.0).astype(BF16)

    gc_t = transpose(gc)
    ls_t = transpose(ls)
    tot_c = jnp.broadcast_to(jnp.sum(gc_t, axis=1, keepdims=True), (ne, LANES))
    ptot_c = jnp.floor((tot_c + (gpr - 1)) * (1.0 / gpr)) * gpr
    gend_c = _int_dot_l(tri(ne, lambda r, c: c <= r), ptot_c)
    gstart_c = gend_c - ptot_c
    n_used = gend_c[ne - 1:ne, :] * (1.0 / gpr)
    nu_ref[...] = n_used.astype(jnp.int32)
    tot_r = jnp.sum(gc, axis=0, keepdims=True)
    ptot_r = jnp.floor((tot_r + (gpr - 1)) * (1.0 / gpr)) * gpr
    gstart_r = _int_dot_r(jnp.broadcast_to(ptot_r, (8, ne)), tri(ne, lambda r, c: r < c))
    cumex = _dot(tri(nt, lambda r, c: c < r), gc.astype(BF16))
    cumex_t = _dot(gc_t.astype(BF16), tri(nt, lambda r, c: r < c))
    tile_base = lax.broadcasted_iota(jnp.int32, (nt, ne), 0).astype(F32) * gpt + ls
    table = jnp.concatenate([cumex + gc, cumex, tile_base, gstart_r, jnp.broadcast_to(tot_r, (8, ne))], axis=0)

    e_iota = lax.broadcasted_iota(jnp.int32, (ne, PLAN_CHUNK), 0).astype(F32)
    for ch in range(src_ref.shape[1] // PLAN_CHUNK):
        g = (lax.broadcasted_iota(jnp.int32, (1, PLAN_CHUNK), 1) + ch * PLAN_CHUNK).astype(F32)
        eg = jnp.sum(jnp.where(gend_c[:, 0:1] <= g, 1.0, 0.0), axis=0, keepdims=True)
        picked = _int_dot_r(table, jnp.where(e_iota == eg, 1.0, 0.0).astype(BF16))
        cum_g, cumex_g, base_g = picked[0:nt], picked[nt:2 * nt], picked[2 * nt:3 * nt]
        u = g - picked[3 * nt:3 * nt + 1]
        in_tile = (cumex_g <= u) & (u < cum_g)
        src = jnp.sum(jnp.where(in_tile, base_g - cumex_g, 0.0), axis=0, keepdims=True) + u
        src = jnp.where(u < picked[3 * nt + 8:3 * nt + 9], src, gpt - 1.0)
        src_ref[:, ch * PLAN_CHUNK:(ch + 1) * PLAN_CHUNK] = src.astype(jnp.int32)

    first_ref[...] = (gstart_c * (1.0 / gpr)).astype(jnp.int32)
    tiles_ref[...] = (ptot_c * (1.0 / gpr)).astype(jnp.int32)

    lg = lax.broadcasted_iota(jnp.int32, (ne, back_ref.shape[1]), 1).astype(F32)
    for t in range(nt):
        first = ls_t[:, t:t + 1]
        inside = (first <= lg) & (lg < first + gc_t[:, t:t + 1])
        shift = gstart_c[:, 0:1] + cumex_t[:, t:t + 1] - first
        val = jnp.sum(jnp.where(inside, shift + lg, 0.0), axis=0, keepdims=True)
        back_ref[t:t + 1, :] = val.astype(jnp.int32)


def _moe_plan(cnt, start):
    nt, ne = cnt.shape
    row_tiles = _moe_row_tiles(nt * MOE_TILE)
    gpt = SORT_ROWS // SORT_ALIGN
    gpr = ROW_TILE // SORT_ALIGN
    n_src = -(-(row_tiles * gpr) // PLAN_CHUNK) * PLAN_CHUNK
    n_back = -(-gpt // LANES) * LANES
    src, first, tiles, nu, back = pl.pallas_call(
        _moe_plan_kernel,
        out_shape=(jax.ShapeDtypeStruct((1, n_src), jnp.int32),
                   jax.ShapeDtypeStruct((ne, LANES), jnp.int32),
                   jax.ShapeDtypeStruct((ne, LANES), jnp.int32),
                   jax.ShapeDtypeStruct((1, LANES), jnp.int32),
                   jax.ShapeDtypeStruct((nt, n_back), jnp.int32)),
        compiler_params=pltpu.CompilerParams(vmem_limit_bytes=VMEM_LIMIT),
        name="moe_plan",
    )(cnt, start)
    return nu[0, :1], first[:, 0], tiles[:, 0], src[0, :row_tiles * gpr], back[:, :gpt]


def _moe_experts_kernel(nu_ref, first_ref, tiles_ref, src_ref, xs_hbm, wg_ref, wu_ref, wd_ref, ys_hbm,
                        xbuf, ybuf, gsem, osem, wgu_s, wd_s):
    e = pl.program_id(0)
    n_used = nu_ref[0]
    gpr = ROW_TILE // SORT_ALIGN
    part = ROW_TILE // FFN_CHAINS

    def gather(tile, to_slot, j0=0, j1=gpr):
        for j in range(j0, j1):
            row = pl.multiple_of(src_ref[tile * gpr + j] * SORT_ALIGN, SORT_ALIGN)
            pltpu.make_async_copy(xs_hbm.at[pl.ds(row, SORT_ALIGN), :],
                                  xbuf.at[to_slot, j * SORT_ALIGN:(j + 1) * SORT_ALIGN, :],
                                  gsem.at[to_slot]).start(priority=j % 2)

    def drain(of_slot):
        for j in range(gpr):
            pltpu.make_async_copy(xs_hbm.at[0:SORT_ALIGN, :],
                                  xbuf.at[of_slot, j * SORT_ALIGN:(j + 1) * SORT_ALIGN, :], gsem.at[of_slot]).wait()

    def out_copy(tile, of_slot):
        row = pl.multiple_of(tile * ROW_TILE, ROW_TILE)
        return pltpu.make_async_copy(ybuf.at[of_slot], ys_hbm.at[pl.ds(row, ROW_TILE), :], osem.at[of_slot])

    @pl.when(e == 0)
    def _():
        for ahead in range(GATHER_SLOTS - 1):
            gather(ahead, ahead)

    wgu_s[:, :EXPERT_FF] = wg_ref[...].astype(BF16)
    wgu_s[:, EXPERT_FF:] = wu_ref[...].astype(BF16)
    wd_s[...] = wd_ref[...].astype(BF16)

    def row_tile(i, carry):
        r = first_ref[e] + i
        slot = lax.rem(r, GATHER_SLOTS)
        oslot = lax.rem(r, 2)
        next_slot = lax.rem(r + GATHER_SLOTS - 1, GATHER_SLOTS)
        drain(slot)

        @pl.when(r >= 2)
        def _():
            out_copy(r - 2, oslot).wait()

        abs_ = []
        for c in range(FFN_CHAINS):
            abs_.append(_dot(xbuf[slot, c * part:(c + 1) * part, :], wgu_s[...]))
            gather(r + GATHER_SLOTS - 1, next_slot, c * gpr // FFN_CHAINS, (c + 1) * gpr // FFN_CHAINS)
        hs = [(_silu(ab[:, :EXPERT_FF]) * ab[:, EXPERT_FF:]).astype(BF16) for ab in abs_]
        ys = [_dot(h, wd_s[...]).astype(BF16) for h in hs]
        for c in range(FFN_CHAINS):
            ybuf[oslot, c * part:(c + 1) * part, :] = ys[c]
        out_copy(r, oslot).start()
        return carry

    lax.fori_loop(0, tiles_ref[e], row_tile, 0)

    @pl.when(e == pl.num_programs(0) - 1)
    def _():
        for ahead in range(GATHER_SLOTS - 1):
            drain(lax.rem(n_used + ahead, GATHER_SLOTS))
        out_copy(n_used - 1, lax.rem(n_used - 1, 2)).wait()

        @pl.when(n_used >= 2)
        def _():
            out_copy(n_used - 2, lax.rem(n_used, 2)).wait()


def _moe_experts(n_used, first, tiles, src, xs, wg, wu, wd, row_tiles):
    d = xs.shape[-1]
    ne = wg.shape[0]
    w_map = lambda e, nu, fi, ti, sr: (e, 0, 0)
    grid_spec = pltpu.PrefetchScalarGridSpec(
        num_scalar_prefetch=4,
        grid=(ne,),
        in_specs=[pl.BlockSpec(memory_space=pl.ANY),
                  pl.BlockSpec((None, d, EXPERT_FF), w_map),
                  pl.BlockSpec((None, d, EXPERT_FF), w_map),
                  pl.BlockSpec((None, EXPERT_FF, d), w_map)],
        out_specs=pl.BlockSpec(memory_space=pl.ANY),
        scratch_shapes=[pltpu.VMEM((GATHER_SLOTS, ROW_TILE, d), BF16),
                        pltpu.VMEM((2, ROW_TILE, d), BF16),
                        pltpu.SemaphoreType.DMA((GATHER_SLOTS,)),
                        pltpu.SemaphoreType.DMA((2,)),
                        pltpu.VMEM((d, 2 * EXPERT_FF), BF16),
                        pltpu.VMEM((EXPERT_FF, d), BF16)])
    return pl.pallas_call(
        _moe_experts_kernel,
        out_shape=jax.ShapeDtypeStruct((row_tiles * ROW_TILE, d), BF16),
        grid_spec=grid_spec,
        compiler_params=pltpu.CompilerParams(dimension_semantics=("arbitrary",),
                                             vmem_limit_bytes=VMEM_LIMIT),
        name="moe_experts",
    )(n_used, first, tiles, src, xs, wg, wu, wd)


def _moe_combine_kernel(back_ref, used_ref, ys_hbm, pos_ref, wts_ref, xm_ref, x1_ref, g2_ref, fg_ref,
                        swg_ref, swu_ref, swd_ref, o_ref, buf, sem, acc_ref):
    i = pl.program_id(0)
    gpt = SORT_ROWS // SORT_ALIGN
    slot = lax.rem(i, 2)
    always = ALWAYS_ROWS
    tail = range(always, SORT_ROWS, COMBINE_TAIL)

    def copies(tile, of_slot, g0, g1, start):
        for g in range(g0, g1):
            row = pl.multiple_of(back_ref[tile * gpt + g] * SORT_ALIGN, SORT_ALIGN) if start else 0
            cp = pltpu.make_async_copy(ys_hbm.at[pl.ds(row, SORT_ALIGN), :],
                                       buf.at[of_slot, g * SORT_ALIGN:(g + 1) * SORT_ALIGN, :], sem.at[of_slot])
            if start:
                cp.start(priority=g % 2)
            else:
                cp.wait()

    def transfer(tile, of_slot, start):
        copies(tile, of_slot, 0, always // SORT_ALIGN, start)
        for c0 in tail:
            pl.when(c0 < used_ref[tile])(functools.partial(
                copies, tile, of_slot, c0 // SORT_ALIGN, (c0 + COMBINE_TAIL) // SORT_ALIGN, start))

    @pl.when(i == 0)
    def _():
        transfer(0, 0, True)

    @pl.when(i + 1 < pl.num_programs(0))
    def _():
        transfer(i + 1, 1 - slot, True)

    x = xm_ref[...]
    tm = x.shape[0]
    pad = jnp.zeros((LANES - TOP_K, tm), F32)
    pos_t = jnp.concatenate([pos_ref[...], pad], axis=0).T
    wts_t = jnp.concatenate([wts_ref[...], pad], axis=0).T
    blk_b, loc_b, wts_b = [], [], []
    for k in range(TOP_K):
        p = jnp.broadcast_to(pos_t[:, k:k + 1], (tm, LANES))
        blk = jnp.floor(p * (1.0 / tm))
        two = lambda v: jnp.concatenate([v.astype(BF16)] * (tm // LANES), axis=1)
        blk_b.append(two(blk))
        loc_b.append(two(p - blk * tm))
        wts_b.append(two(jnp.broadcast_to(wts_t[:, k:k + 1], (tm, LANES))))
    shared = _dot((_silu(_dot(x, swg_ref[...])) * _dot(x, swu_ref[...])).astype(BF16), swd_ref[...])
    transfer(i, slot, False)
    lane = lax.broadcasted_iota(jnp.int32, (tm, tm), 1).astype(F32).astype(BF16)
    zero = jnp.zeros((tm, tm), BF16)
    nowhere = jnp.full((tm, tm), -1.0, BF16)

    def apply(c0, width):
        blocks = []
        for b0 in range(c0, c0 + width, tm):
            comb = zero
            for k in range(TOP_K):
                loc = jnp.where(blk_b[k] == float(b0 // tm), loc_b[k], nowhere)
                comb = jnp.where(lane == loc, wts_b[k], comb)
            blocks.append(comb)
        return _dot(jnp.concatenate(blocks, axis=1), buf[slot, c0:c0 + width, :])

    routed = shared
    for c0 in range(0, always, COMBINE_CHUNK):
        routed = routed + apply(c0, min(COMBINE_CHUNK, always - c0))
    acc_ref[...] = routed
    for c0 in tail:
        @pl.when(c0 < used_ref[i])
        def _(c0=c0):
            acc_ref[...] += apply(c0, COMBINE_TAIL)
    y = x1_ref[...] + g2_ref[...] * acc_ref[...]
    o_ref[...] = _rms_norm(y, fg_ref[...])


def _moe_combine(back, used, ys, pos, wts, xm, x1, g2, final_g, swg, swu, swd, *, tiles_per_mod):
    n, d = xm.shape
    tm = pos.shape[-1]
    nt = n // tm
    gpt = SORT_ROWS // SORT_ALIGN
    row = lambda i, bk, us: (i, 0)
    full = lambda i, bk, us: (0, 0)
    tile = lambda i, bk, us: (i, 0, 0)
    mod_map = lambda i, bk, us: (i // tiles_per_mod, 0, 0)
    grid_spec = pltpu.PrefetchScalarGridSpec(
        num_scalar_prefetch=2,
        grid=(nt,),
        in_specs=[pl.BlockSpec(memory_space=pl.ANY),
                  pl.BlockSpec((None, TOP_K, tm), tile),
                  pl.BlockSpec((None, TOP_K, tm), tile),
                  pl.BlockSpec((tm, d), row),
                  pl.BlockSpec((tm, d), row),
                  pl.BlockSpec((None, 1, d), mod_map),
                  pl.BlockSpec((1, d), full),
                  pl.BlockSpec((d, SHARED_FF), full),
                  pl.BlockSpec((d, SHARED_FF), full),
                  pl.BlockSpec((SHARED_FF, d), full)],
        out_specs=pl.BlockSpec((tm, d), row),
        scratch_shapes=[pltpu.VMEM((2, SORT_ROWS, d), BF16),
                        pltpu.SemaphoreType.DMA((2,)),
                        pltpu.VMEM((tm, d), F32)])
    return pl.pallas_call(
        _moe_combine_kernel,
        out_shape=jax.ShapeDtypeStruct((n, d), F32),
        grid_spec=grid_spec,
        compiler_params=pltpu.CompilerParams(dimension_semantics=("arbitrary",),
                                             vmem_limit_bytes=VMEM_LIMIT),
        name="moe_combine",
    )(back, used, ys, pos.reshape(nt, TOP_K, tm), wts.reshape(nt, TOP_K, tm), xm, x1, g2, final_g, swg, swu, swd)


def _mix(x, mods, p, attn_fn, s0=None):
    sh1, sc1, g1, sh2, sc2, _ = mods
    gla_in, lora, q_s, k_s, v_s = _inproj(x, p["norm_attn_g"], sh1, sc1, p["w_gla"], p["w_lora"], p["w_swa"])
    if s0 is None:
        gla_out, s_f, s_b = _gla(gla_in, lora, p["waf"], p["baf"], p["wab"], p["bab"], p["gla_norm_g"])
    else:
        gla_out, s_f, s_b = _gla(gla_in, lora, p["waf"], p["baf"], p["wab"], p["bab"], p["gla_norm_g"],
                                 s0[0], s0[1])
    att_out = attn_fn(q_s, k_s, v_s)
    routed = _outproj(gla_out, att_out, x, p["w_out"], g1, sh2, sc2, p["norm_ffn_g"],
                      p["rw_cat"], p["rw_hi"], p["rbias"])
    return routed, k_s, v_s, s_f, s_b


def _moe(streams, p):
    d = D_MODEL
    (ra, _), (rb, _) = streams
    n_tiles = [r[1].shape[0] * r[1].shape[1] // MOE_TILE for r, _ in streams]
    pos_all = jnp.concatenate([r[2].reshape(-1, TOP_K, MOE_TILE) for r, _ in streams], axis=0)
    cnt_all = jnp.concatenate([r[4][..., 0].reshape(-1, N_EXPERTS) for r, _ in streams], axis=0)
    start_all = jnp.concatenate([r[5][..., 0].reshape(-1, N_EXPERTS) for r, _ in streams], axis=0)
    used = (start_all[:, -1] + cnt_all[:, -1]).astype(jnp.int32)
    xs = _moe_sort(ra[1].reshape(-1, d), rb[1].reshape(-1, d), pos_all, used)
    n_used, first, tiles, src, back = _moe_plan(cnt_all, start_all)
    ys = _moe_experts(n_used, first, tiles, src, xs, p["wg"], p["wu"], p["wd"],
                      _moe_row_tiles(cnt_all.shape[0] * MOE_TILE))
    outs = []
    tile0 = 0
    for ((x1, xm, pos, wts, cnt, start), g2), nt in zip(streams, n_tiles):
        b, t, _ = x1.shape
        tiles_per_mod = (t // MOE_TILE) if g2.shape[0] > 1 else nt
        y = _moe_combine(back[tile0:tile0 + nt].reshape(-1), used[tile0:tile0 + nt], ys, pos, wts,
                         xm.reshape(-1, d), x1.reshape(-1, d), g2, p["final_norm_g"],
                         p["swg"], p["swu"], p["swd"], tiles_per_mod=tiles_per_mod)
        outs.append(y.reshape(b, t, d))
        tile0 += nt
    return outs


def kernel(x_prompt, x_sample, c, cache_swa_k, cache_swa_v, state_gla_fwd, state_gla_bwd, c_ctx, w_ada, b_ada, norm_attn_g, norm_ffn_g, w_in, gla_wa_f, gla_ba_f, gla_wa_b, gla_ba_b, gla_norm_g, swa_sink, w_out, router_w, router_bias, exp_w_gate, exp_w_up, exp_w_down, sh_w_gate, sh_w_up, sh_w_down, final_norm_g):
    l = 0
    d = D_MODEL
    nb_ctx, t_ctx, _ = x_prompt.shape
    nb_lat, t_lat, _ = x_sample.shape

    pad = jnp.zeros((8 - 1 - nb_lat, d), F32)
    cond8 = jnp.concatenate([c_ctx[None, :], c, pad], axis=0)
    mod = _adaln(cond8, w_ada[l], b_ada[l][None, :])
    mods_ctx = [mod[0:1, i * d:(i + 1) * d][:, None, :] for i in range(6)]
    mods_lat = [mod[1:1 + nb_lat, i * d:(i + 1) * d][:, None, :] for i in range(6)]

    zeros_lora = jnp.zeros((GLA_LORA, GLA_QK), F32)
    rw = router_w[l]
    rw_hi = rw.astype(BF16)
    rw_lo = (rw - rw_hi.astype(F32)).astype(BF16)
    w_in_b = w_in[l].astype(BF16)
    p = {
        "norm_attn_g": norm_attn_g[l][None, :],
        "norm_ffn_g": norm_ffn_g[l][None, :],
        "final_norm_g": final_norm_g[None, :],
        "w_gla": w_in_b[:, :2 * GLA_QK + 2 * GLA_V],
        "w_lora": w_in_b[:, 2 * GLA_QK + 2 * GLA_V:2 * GLA_QK + 2 * GLA_V + 2 * GLA_LORA],
        "w_swa": w_in_b[:, 2 * GLA_QK + 2 * GLA_V + 2 * GLA_LORA:],
        "waf": jnp.concatenate([gla_wa_f[l], zeros_lora], axis=0).astype(BF16),
        "wab": jnp.concatenate([zeros_lora, gla_wa_b[l]], axis=0).astype(BF16),
        "baf": gla_ba_f[l][None, :],
        "bab": gla_ba_b[l][None, :],
        "gla_norm_g": gla_norm_g[l][None, :],
        "w_out": w_out[l].astype(BF16),
        "rw_cat": jnp.concatenate([rw_hi, rw_lo], axis=1),
        "rw_hi": rw_hi,
        "rbias": router_bias[l][:, None],
        "wg": exp_w_gate[l], "wu": exp_w_up[l], "wd": exp_w_down[l],
        "swg": sh_w_gate[l].astype(BF16), "swu": sh_w_up[l].astype(BF16),
        "swd": sh_w_down[l].astype(BF16),
    }
    sink = swa_sink[l]

    routed_ctx, k_c, v_c, s_f, s_b = _mix(x_prompt, mods_ctx, p, functools.partial(_attn_ctx, sink))

    cos, sin_lo, sin_hi = _rope_tables(t_lat)
    kc = cache_swa_k[:, l].reshape(nb_lat, -1, SWA_KV)
    vc = cache_swa_v[:, l].reshape(nb_lat, -1, SWA_KV)
    lat_attn = lambda q, k, v: _attn_lat(sink, q, k, v, kc, vc, cos, sin_lo, sin_hi)
    s0 = (state_gla_fwd[:, l].reshape(nb_lat, GLA_QK, GLA_DV),
          state_gla_bwd[:, l].reshape(nb_lat, GLA_QK, GLA_DV))
    routed_lat, _, _, _, _ = _mix(x_sample, mods_lat, p, lat_attn, s0)
    y_prompt, y_sample = _moe([(routed_ctx, mods_ctx[5]), (routed_lat, mods_lat[5])], p)

    new_k = k_c.reshape(nb_ctx, 1, t_ctx, SWA_KV_HEADS, SWA_HEAD_DIM)
    new_v = v_c.reshape(nb_ctx, 1, t_ctx, SWA_KV_HEADS, SWA_HEAD_DIM)
    new_sf = s_f.reshape(nb_ctx, 1, GLA_HEADS, GLA_DK, GLA_DV)
    new_sb = s_b.reshape(nb_ctx, 1, GLA_HEADS, GLA_DK, GLA_DV)
    return (y_prompt, y_sample, new_k, new_v, new_sf, new_sb)
```

```python
import functools

import jax
import jax.numpy as jnp
from jax import lax
from jax.experimental import pallas as pl
from jax.experimental.pallas import tpu as pltpu

F32 = jnp.float32
BF16 = jnp.bfloat16

D_MODEL = 1024
GLA_HEADS = 4
GLA_DK = 64
GLA_DV = 128
GLA_LORA = 16
GLA_GATE_NORM = 16.0
GLA_CHUNK = 64
GLA_QK = GLA_HEADS * GLA_DK
GLA_V = GLA_HEADS * GLA_DV
SWA_HEAD_DIM = 64
SWA_HEADS = 8
SWA_KV_HEADS = 2
SWA_Q = SWA_HEADS * SWA_HEAD_DIM
SWA_KV = SWA_KV_HEADS * SWA_HEAD_DIM
ATTN_BLOCK = 128
GRID_W = 64
ROPE_BASE = 10000.0
N_EXPERTS = 64
TOP_K = 8
N_EXPERT_GROUPS = 8
TOPK_GROUPS = 4
EXPERT_FF = 128
SHARED_FF = 256
ROUTED_SCALE = 2.5
EPS = 1e-6

LANES = 128
VMEM_LIMIT = 56 * 1024 * 1024

NEG_INF = float("-inf")


def _dot(a, b):
    return jnp.dot(a, b, preferred_element_type=F32)


def _dot_nt(a, b):
    return lax.dot_general(a, b, (((1,), (1,)), ((), ())), preferred_element_type=F32)


def _split_hi_lo(x):
    hi = x.astype(BF16)
    lo = (x - hi.astype(F32)).astype(BF16)
    return hi, lo


def _sigmoid(x):
    return 1.0 / (1.0 + jnp.exp(-x))


def _silu(x):
    return x * _sigmoid(x)


def _rms_norm(x, g):
    ms = jnp.mean(x * x, axis=-1, keepdims=True)
    return x * lax.rsqrt(ms + EPS) * g


def _adaln_kernel(c_ref, w_ref, b_ref, o_ref):
    a_hi, a_lo = _split_hi_lo(_silu(c_ref[...]))
    w_hi, w_lo = _split_hi_lo(w_ref[...])
    o_ref[...] = _dot(a_hi, w_hi) + _dot(a_lo, w_hi) + _dot(a_hi, w_lo) + b_ref[...]


def _adaln(cond8, w_ada, b_ada):
    n = w_ada.shape[1]
    tn = 1536
    return pl.pallas_call(
        _adaln_kernel,
        out_shape=jax.ShapeDtypeStruct((8, n), F32),
        grid=(n // tn,),
        in_specs=[pl.BlockSpec((8, D_MODEL), lambda j: (0, 0)),
                  pl.BlockSpec((D_MODEL, tn), lambda j: (0, j)),
                  pl.BlockSpec((1, tn), lambda j: (0, j))],
        out_specs=pl.BlockSpec((8, tn), lambda j: (0, j)),
        compiler_params=pltpu.CompilerParams(dimension_semantics=("arbitrary",),
                                             vmem_limit_bytes=VMEM_LIMIT),
        name="adaln",
    )(cond8, w_ada, b_ada)


def _inproj_kernel(x_ref, g_ref, sh_ref, sc_ref, wg_ref, wl_ref, ws_ref,
                   gla_ref, lora_ref, q_ref, k_ref, v_ref):
    bb, tb, d = x_ref.shape
    x = x_ref[...].reshape(bb * tb, d)
    h = _rms_norm(x, g_ref[...]) * (1.0 + sc_ref[...]) + sh_ref[...]
    hb = h.astype(BF16)
    gla_ref[...] = _dot(hb, wg_ref[...]).reshape(gla_ref.shape)
    lora_ref[...] = _dot(hb, wl_ref[...]).reshape(lora_ref.shape)
    s = _dot(hb, ws_ref[...])
    q_ref[...] = s[:, :SWA_Q].reshape(q_ref.shape)
    k_ref[...] = s[:, SWA_Q:SWA_Q + SWA_KV].reshape(k_ref.shape)
    v_ref[...] = s[:, SWA_Q + SWA_KV:].reshape(v_ref.shape)


INPROJ_TILE = 512


def _inproj(x, g, sh, sc, w_gla, w_lora, w_swa):
    b, t, d = x.shape
    nmod = sh.shape[0]
    tb = min(t, INPROJ_TILE)
    bb = INPROJ_TILE // tb if nmod == 1 else 1
    mod_map = (lambda i, j: (i, 0, 0)) if nmod > 1 else (lambda i, j: (0, 0, 0))
    row = lambda i, j: (i, j, 0)
    full = lambda i, j: (0, 0)
    n_gla = w_gla.shape[1]
    n_lora = w_lora.shape[1]
    return pl.pallas_call(
        _inproj_kernel,
        out_shape=(jax.ShapeDtypeStruct((b, t, n_gla), F32),
                   jax.ShapeDtypeStruct((b, t, n_lora), F32),
                   jax.ShapeDtypeStruct((b, t, SWA_Q), F32),
                   jax.ShapeDtypeStruct((b, t, SWA_KV), F32),
                   jax.ShapeDtypeStruct((b, t, SWA_KV), F32)),
        grid=(b // bb, t // tb),
        in_specs=[pl.BlockSpec((bb, tb, d), row),
                  pl.BlockSpec((1, d), full),
                  pl.BlockSpec((None, 1, d), mod_map),
                  pl.BlockSpec((None, 1, d), mod_map),
                  pl.BlockSpec((d, n_gla), full),
                  pl.BlockSpec((d, n_lora), full),
                  pl.BlockSpec((d, w_swa.shape[1]), full)],
        out_specs=(pl.BlockSpec((bb, tb, n_gla), row),
                   pl.BlockSpec((bb, tb, n_lora), row),
                   pl.BlockSpec((bb, tb, SWA_Q), row),
                   pl.BlockSpec((bb, tb, SWA_KV), row),
                   pl.BlockSpec((bb, tb, SWA_KV), row)),
        compiler_params=pltpu.CompilerParams(dimension_semantics=("arbitrary", "arbitrary"),
                                             vmem_limit_bytes=VMEM_LIMIT),
        name="inproj",
    )(x, g, sh, sc, w_gla, w_lora, w_swa)


SCAN_UNROLL = 4
OUT_UNROLL = 4


def _log_sigmoid(x):
    return jnp.minimum(x, 0.0) - jnp.log(1.0 + jnp.exp(-jnp.abs(x)))


def _heads_to_rows(x):
    return jnp.concatenate([x[:, h * LANES:(h + 1) * LANES] for h in range(GLA_HEADS)], axis=0)


def _rows_to_heads(x, c):
    return jnp.concatenate([x[h * c:(h + 1) * c, :] for h in range(GLA_HEADS)], axis=1)


def _gla_kernel(has_init, q_ref, k_ref, v_ref, g_ref, lora_ref, waf_ref, baf_ref, wab_ref, bab_ref,
                ng_ref, *rest):
    if has_init:
        s0f_ref, s0b_ref, *rest = rest
    (out_ref, sf_ref, sb_ref, laf_ref, lab_ref, oacc_ref, qtf_ref, qtb_ref, saf_ref, sab_ref,
     stf_ref, stb_ref) = rest
    t = q_ref.shape[0]
    c = GLA_CHUNK
    n = t // c
    hc = GLA_HEADS * c

    lora = lora_ref[...].astype(BF16)
    laf_ref[...] = _log_sigmoid(_dot(lora, waf_ref[...]) + baf_ref[...]) * (1.0 / GLA_GATE_NORM)
    lab_ref[...] = _log_sigmoid(_dot(lora, wab_ref[...]) + bab_ref[...]) * (1.0 / GLA_GATE_NORM)

    if has_init:
        stf_ref[...] = s0f_ref[...].T
        stb_ref[...] = s0b_ref[...].T
    else:
        stf_ref[...] = jnp.zeros_like(stf_ref)
        stb_ref[...] = jnp.zeros_like(stb_ref)
    oacc_ref[...] = jnp.zeros_like(oacc_ref)

    r64 = lax.broadcasted_iota(jnp.int32, (c, c), 0)
    c64 = lax.broadcasted_iota(jnp.int32, (c, c), 1)
    tri_f = jnp.where(c64 <= r64, 1.0, 0.0).astype(BF16)
    tri_b = jnp.where(c64 >= r64, 1.0, 0.0).astype(BF16)
    rr = lax.broadcasted_iota(jnp.int32, (hc, hc), 0)
    cc = lax.broadcasted_iota(jnp.int32, (hc, hc), 1)
    same_head = (rr >> 6) == (cc >> 6)
    keep_f = same_head & ((rr & (c - 1)) >= (cc & (c - 1)))
    keep_b = same_head & ((rr & (c - 1)) <= (cc & (c - 1)))
    head_mask = jnp.where(same_head, 1.0, 0.0).astype(BF16)
    norm_g = ng_ref[...]

    def chunk_rows(ci):
        return pl.ds(pl.multiple_of(ci * c, c), c)

    def tile_heads(x):
        x4 = jnp.concatenate([x] * GLA_HEADS, axis=0)
        return jnp.where(same_head, x4, 0.0).astype(BF16)

    def scan_step(i, carry):
        dirs = []
        for u in range(SCAN_UNROLL):
            dirs += [(SCAN_UNROLL * i + u, laf_ref, tri_f, keep_f, c - 1, stf_ref, saf_ref, qtf_ref),
                     (n - 1 - SCAN_UNROLL * i - u, lab_ref, tri_b, keep_b, 0, stb_ref, sab_ref, qtb_ref)]
        cums = []
        for ci, la_ref, tri, _, _, _, _, _ in dirs:
            la_hi, la_lo = _split_hi_lo(la_ref[chunk_rows(ci), :])
            cums.append(_dot(tri, la_hi) + _dot(tri, la_lo))
        ops = []
        for (ci, _, _, _, last_row, _, _, qt_ref), cum in zip(dirs, cums):
            sl = chunk_rows(ci)
            tot = cum[last_row:last_row + 1, :]
            kc = k_ref[sl, :]
            qt = q_ref[sl, :] * (GLA_DK ** -0.5) * jnp.exp(cum)
            qt_ref[sl, :] = qt.astype(BF16)
            v_rows = _heads_to_rows(v_ref[sl, :])
            ops.append((tot, tile_heads(qt), tile_heads(kc * jnp.exp(-cum)),
                        tile_heads(kc * jnp.exp(tot - cum)), v_rows))
        atts = [_dot_nt(q4, k4) for _, q4, k4, _, _ in ops]
        incs = []
        for (_, _, _, keep, _, _, _, _), (_, _, _, kd4, v_rows), att in zip(dirs, ops, atts):
            att = jnp.where(keep, att, 0.0).astype(BF16)
            incs.append((_dot(att, v_rows.astype(BF16)), _dot(v_rows.T.astype(BF16), kd4)))
        for (ci, _, _, _, _, st_ref, snap_ref, _), (tot, _, _, _, _), (o_intra, st_inc) in zip(dirs, ops, incs):
            oacc_ref[ci] += o_intra
            st = st_ref[...]
            snap_ref[ci] = st.astype(BF16)
            st_ref[...] = jnp.exp(tot) * st + st_inc
        return carry

    def tile_heads_bf16(x):
        return jnp.concatenate([x] * GLA_HEADS, axis=0) * head_mask

    def out_step(i, carry):
        chunks = [OUT_UNROLL * i + u for u in range(OUT_UNROLL)]
        inter = []
        for ci in chunks:
            sl = chunk_rows(ci)
            q4 = jnp.concatenate([tile_heads_bf16(qtf_ref[sl, :]), tile_heads_bf16(qtb_ref[sl, :])], axis=1)
            st = jnp.concatenate([saf_ref[ci], sab_ref[ci]], axis=1)
            inter.append(_dot_nt(q4, st))
        for ci, o_inter in zip(chunks, inter):
            sl = chunk_rows(ci)
            on = _rms_norm(oacc_ref[ci] + o_inter, norm_g)
            gate = _silu(_heads_to_rows(g_ref[sl, :]))
            out_ref[sl, :] = _rows_to_heads(on * gate, c)
        return carry

    lax.fori_loop(0, n // SCAN_UNROLL, scan_step, 0)
    lax.fori_loop(0, n // OUT_UNROLL, out_step, 0)
    sf_ref[...] = stf_ref[...].T
    sb_ref[...] = stb_ref[...].T


def _gla(gla_in, lora, waf, baf, wab, bab, norm_g, s0f=None, s0b=None):
    b, t, _ = gla_in.shape
    has_init = s0f is not None
    n = t // GLA_CHUNK
    bmap = lambda i: (i, 0, 0)
    full = lambda i: (0, 0)
    in_specs = [pl.BlockSpec((None, t, GLA_QK), lambda i: (i, 0, 0)),
                pl.BlockSpec((None, t, GLA_QK), lambda i: (i, 0, 1)),
                pl.BlockSpec((None, t, GLA_V), lambda i: (i, 0, 1)),
                pl.BlockSpec((None, t, GLA_V), lambda i: (i, 0, 2)),
                pl.BlockSpec((None, t, 2 * GLA_LORA), bmap),
                pl.BlockSpec((2 * GLA_LORA, GLA_QK), full),
                pl.BlockSpec((1, GLA_QK), full),
                pl.BlockSpec((2 * GLA_LORA, GLA_QK), full),
                pl.BlockSpec((1, GLA_QK), full),
                pl.BlockSpec((1, GLA_DV), full)]
    args = [gla_in, gla_in, gla_in, gla_in, lora, waf, baf, wab, bab, norm_g]
    if has_init:
        in_specs += [pl.BlockSpec((None, GLA_QK, GLA_DV), bmap)] * 2
        args += [s0f, s0b]
    return pl.pallas_call(
        functools.partial(_gla_kernel, has_init),
        out_shape=(jax.ShapeDtypeStruct((b, t, GLA_V), F32),
                   jax.ShapeDtypeStruct((b, GLA_QK, GLA_DV), F32),
                   jax.ShapeDtypeStruct((b, GLA_QK, GLA_DV), F32)),
        grid=(b,),
        in_specs=in_specs,
        out_specs=(pl.BlockSpec((None, t, GLA_V), bmap),
                   pl.BlockSpec((None, GLA_QK, GLA_DV), bmap),
                   pl.BlockSpec((None, GLA_QK, GLA_DV), bmap)),
        scratch_shapes=[pltpu.VMEM((t, GLA_QK), F32),
                        pltpu.VMEM((t, GLA_QK), F32),
                        pltpu.VMEM((n, GLA_HEADS * GLA_CHUNK, GLA_DV), F32),
                        pltpu.VMEM((t, GLA_QK), BF16),
                        pltpu.VMEM((t, GLA_QK), BF16),
                        pltpu.VMEM((n, GLA_DV, GLA_QK), BF16),
                        pltpu.VMEM((n, GLA_DV, GLA_QK), BF16),
                        pltpu.VMEM((GLA_DV, GLA_QK), F32),
                        pltpu.VMEM((GLA_DV, GLA_QK), F32)],
        compiler_params=pltpu.CompilerParams(dimension_semantics=("arbitrary",),
                                             vmem_limit_bytes=VMEM_LIMIT),
        name="gla",
    )(*args)


def _dup_groups(x):
    lo = lax.broadcasted_iota(jnp.int32, x.shape, 1) < SWA_HEAD_DIM
    xr = pltpu.roll(x, SWA_HEAD_DIM, axis=1)
    return jnp.where(lo, x, xr), jnp.where(lo, xr, x)


def _pairs_attention(qps, sinks, k_dups, vt_dups, masks):
    nq = qps[0].shape[0]
    lo = lax.broadcasted_iota(jnp.int32, (nq, LANES), 1) < SWA_HEAD_DIM
    even = lax.broadcasted_iota(jnp.int32, (1, 2 * nq), 1) < nq
    scores = []
    for qp, k_dup in zip(qps, k_dups):
        q2 = jnp.concatenate([jnp.where(lo, qp, 0.0), jnp.where(lo, 0.0, qp)], axis=0).astype(BF16)
        scores.append(_dot_nt(k_dup, q2))
    probs = []
    for s, (sink_even, sink_odd), mask in zip(scores, sinks, masks):
        if mask is not None:
            s = jnp.where(mask, s, NEG_INF)
        sink = jnp.where(even, sink_even, sink_odd)
        m = jnp.maximum(jnp.max(s, axis=0, keepdims=True), sink)
        p = jnp.exp(s - m)
        denom = jnp.sum(p, axis=0, keepdims=True) + jnp.exp(sink - m)
        probs.append((p.astype(BF16), 1.0 / denom))
    outs = []
    for (p, rdenom), vt_dup in zip(probs, vt_dups):
        o = _dot(vt_dup, p) * rdenom
        outs.append(jnp.concatenate([o[:SWA_HEAD_DIM, :nq], o[SWA_HEAD_DIM:, nq:]], axis=0).T)
    return outs


CTX_BATCH = 4


def _attn_ctx_kernel(sink_ref, q_ref, k_ref, v_ref, o_ref):
    scale = SWA_HEAD_DIM ** -0.5
    pairs = range(SWA_HEADS // 2)
    items = [(bb, pr) for bb in range(q_ref.shape[0]) for pr in pairs]
    kd = [[x.astype(BF16) for x in _dup_groups(k_ref[bb])] for bb in range(q_ref.shape[0])]
    vt = [[x.T.astype(BF16) for x in _dup_groups(v_ref[bb])] for bb in range(q_ref.shape[0])]
    outs = _pairs_attention([q_ref[bb, :, pr * LANES:(pr + 1) * LANES] * scale for bb, pr in items],
                            [(sink_ref[2 * pr], sink_ref[2 * pr + 1]) for _, pr in items],
                            [kd[bb][pr // 2] for bb, pr in items], [vt[bb][pr // 2] for bb, pr in items],
                            [None] * len(items))
    for (bb, pr), out in zip(items, outs):
        o_ref[bb, :, pr * LANES:(pr + 1) * LANES] = out


def _attn_ctx(sink, q, k, v):
    b, t, _ = q.shape
    bmap = lambda i: (i, 0, 0)
    return pl.pallas_call(
        _attn_ctx_kernel,
        out_shape=jax.ShapeDtypeStruct((b, t, SWA_Q), F32),
        grid=(b // CTX_BATCH,),
        in_specs=[pl.BlockSpec(memory_space=pltpu.SMEM),
                  pl.BlockSpec((CTX_BATCH, t, SWA_Q), bmap),
                  pl.BlockSpec((CTX_BATCH, t, SWA_KV), bmap),
                  pl.BlockSpec((CTX_BATCH, t, SWA_KV), bmap)],
        out_specs=pl.BlockSpec((CTX_BATCH, t, SWA_Q), bmap),
        compiler_params=pltpu.CompilerParams(dimension_semantics=("arbitrary",),
                                             vmem_limit_bytes=VMEM_LIMIT),
        name="attn_ctx",
    )(sink, q, k, v)


LAT_BLOCKS = 2


def _rope(x, cos, sin_lo, sin_hi):
    return x * cos + pltpu.roll(x, LANES - 16, axis=1) * sin_lo + pltpu.roll(x, 16, axis=1) * sin_hi


def _attn_lat_kernel(sink_ref, q_ref, k_ref, v_ref, kc_ref, vc_ref, cos_ref, sl_ref, sh_ref,
                     o_ref, kw_ref, vw_ref):
    t = q_ref.shape[0]
    ab = ATTN_BLOCK
    nb = t // ab
    scale = SWA_HEAD_DIM ** -0.5

    k_rot = _dup_groups(_rope(k_ref[...], cos_ref[...], sl_ref[...], sh_ref[...]))
    v_dup = _dup_groups(v_ref[...])
    zeros = jnp.zeros((ab, LANES), BF16)
    for grp in range(SWA_KV_HEADS):
        kw_ref[grp, 0:ab, :] = zeros
        kw_ref[grp, ab:ab + t, :] = k_rot[grp].astype(BF16)
        kw_ref[grp, ab + t:, :] = zeros
        vw_ref[grp, 0] = zeros
        for blk in range(nb):
            vw_ref[grp, blk + 1] = v_dup[grp][blk * ab:(blk + 1) * ab, :].T.astype(BF16)
        vw_ref[grp, nb + 1] = zeros
    kc = [x.astype(BF16) for x in _dup_groups(kc_ref[...])]
    vct = [x.T.astype(BF16) for x in _dup_groups(vc_ref[...])]
    lc = kc_ref.shape[0]

    key = lax.broadcasted_iota(jnp.int32, (lc + 3 * ab, 2 * ab), 0) - lc
    tq = lax.broadcasted_iota(jnp.int32, (lc + 3 * ab, 2 * ab), 1) & (ab - 1)
    band = (key < 0) | (jnp.abs(tq + ab - key) <= ab)

    def block(it, carry):
        pairs = range(SWA_HEADS // 2)
        qps, sinks, k_dups, vt_dups, masks, places = [], [], [], [], [], []
        for u in range(LAT_BLOCKS):
            nq = it * LAT_BLOCKS + u
            row0 = pl.multiple_of(nq * ab, ab)
            s_abs = key + (nq - 1) * ab
            mask = band & ((key < 0) | ((s_abs >= 0) & (s_abs < t)))
            cos = cos_ref[pl.ds(row0, ab), :]
            s_lo = sl_ref[pl.ds(row0, ab), :]
            s_hi = sh_ref[pl.ds(row0, ab), :]
            k_all = [jnp.concatenate([kc[grp], kw_ref[grp, pl.ds(row0, 3 * ab), :]], axis=0)
                     for grp in range(SWA_KV_HEADS)]
            vt_all = [jnp.concatenate([vct[grp], vw_ref[grp, nq], vw_ref[grp, nq + 1], vw_ref[grp, nq + 2]],
                                      axis=1) for grp in range(SWA_KV_HEADS)]
            for pr in pairs:
                qps.append(_rope(q_ref[pl.ds(row0, ab), pr * LANES:(pr + 1) * LANES], cos, s_lo, s_hi) * scale)
                sinks.append((sink_ref[2 * pr], sink_ref[2 * pr + 1]))
                k_dups.append(k_all[pr // 2])
                vt_dups.append(vt_all[pr // 2])
                masks.append(mask)
                places.append((row0, pr))
        outs = _pairs_attention(qps, sinks, k_dups, vt_dups, masks)
        for (row0, pr), out in zip(places, outs):
            o_ref[pl.ds(row0, ab), pr * LANES:(pr + 1) * LANES] = out
        return carry

    lax.fori_loop(0, nb // LAT_BLOCKS, block, 0)


def _attn_lat(sink, q, k, v, kc, vc, cos, sin_lo, sin_hi):
    b, t, _ = q.shape
    lc = kc.shape[1]
    bmap = lambda i: (i, 0, 0)
    full = lambda i: (0, 0)
    return pl.pallas_call(
        _attn_lat_kernel,
        out_shape=jax.ShapeDtypeStruct((b, t, SWA_Q), F32),
        grid=(b,),
        in_specs=[pl.BlockSpec(memory_space=pltpu.SMEM),
                  pl.BlockSpec((None, t, SWA_Q), bmap),
                  pl.BlockSpec((None, t, SWA_KV), bmap),
                  pl.BlockSpec((None, t, SWA_KV), bmap),
                  pl.BlockSpec((None, lc, SWA_KV), bmap),
                  pl.BlockSpec((None, lc, SWA_KV), bmap),
                  pl.BlockSpec((t, LANES), full),
                  pl.BlockSpec((t, LANES), full),
                  pl.BlockSpec((t, LANES), full)],
        out_specs=pl.BlockSpec((None, t, SWA_Q), bmap),
        scratch_shapes=[pltpu.VMEM((SWA_KV_HEADS, t + 2 * ATTN_BLOCK, LANES), BF16),
                        pltpu.VMEM((SWA_KV_HEADS, t // ATTN_BLOCK + 2, LANES, ATTN_BLOCK), BF16)],
        compiler_params=pltpu.CompilerParams(dimension_semantics=("arbitrary",),
                                             vmem_limit_bytes=VMEM_LIMIT),
        name="attn_lat",
    )(sink, q, k, v, kc, vc, cos, sin_lo, sin_hi)


def _rope_tables(t):
    half = SWA_HEAD_DIM // 2
    quarter = half // 2
    pos = jnp.arange(t)
    row = (pos // GRID_W).astype(F32)
    col = (pos % GRID_W).astype(F32)
    inv_freq = ROPE_BASE ** (-jnp.arange(quarter, dtype=F32) / quarter)
    lane = jnp.arange(LANES)
    d = lane % SWA_HEAD_DIM
    freq = inv_freq[d % quarter]
    use_row = (d < half)
    ang = jnp.where(use_row[None, :], row[:, None], col[:, None]) * freq[None, :]
    cos = jnp.cos(ang)
    sin = jnp.sin(ang)
    lower = (d % half) < quarter
    return cos, jnp.where(lower[None, :], -sin, 0.0), jnp.where(lower[None, :], 0.0, sin)


def _route(sel, scores):
    n = sel.shape[1]
    gsz = N_EXPERTS // N_EXPERT_GROUPS

    def first_max(x, idx, size):
        m = jnp.max(x, axis=0, keepdims=True)
        first = jnp.min(jnp.where(x == m, idx, float(size)), axis=0, keepdims=True)
        return m, idx == first

    i8 = lax.broadcasted_iota(jnp.int32, (gsz, n), 0).astype(F32)
    rows = []
    for g in range(N_EXPERT_GROUPS):
        slab = sel[g * gsz:(g + 1) * gsz, :]
        m1, hit = first_max(slab, i8, gsz)
        m2 = jnp.max(jnp.where(hit, NEG_INF, slab), axis=0, keepdims=True)
        rows.append(m1 + m2)
    gscore = jnp.concatenate(rows, axis=0)
    gsel = jnp.zeros((N_EXPERT_GROUPS, n), F32)
    for _ in range(TOPK_GROUPS):
        _, hit = first_max(gscore, i8, N_EXPERT_GROUPS)
        gsel = jnp.where(hit, 1.0, gsel)
        gscore = jnp.where(hit, NEG_INF, gscore)
    emask = jnp.concatenate(
        [jnp.broadcast_to(gsel[g:g + 1, :], (gsz, n)) for g in range(N_EXPERT_GROUPS)], axis=0)
    cand = jnp.where(emask > 0.5, sel, NEG_INF)
    ie = lax.broadcasted_iota(jnp.int32, (N_EXPERTS, n), 0).astype(F32)
    w = jnp.zeros((N_EXPERTS, n), F32)
    chosen = jnp.zeros((N_EXPERTS, n), F32)
    hits = []
    for _ in range(TOP_K):
        _, hit = first_max(cand, ie, N_EXPERTS)
        hits.append(hit)
        w = jnp.where(hit, scores, w)
        chosen = jnp.where(hit, 1.0, chosen)
        cand = jnp.where(hit, NEG_INF, cand)
    gates = w / jnp.sum(w, axis=0, keepdims=True) * ROUTED_SCALE

    s_idx = lax.broadcasted_iota(jnp.int32, (n, n), 0)
    t_idx = lax.broadcasted_iota(jnp.int32, (n, n), 1)
    tile_shift = MOE_TILE.bit_length() - 1
    before = jnp.where((s_idx < t_idx) & ((s_idx >> tile_shift) == (t_idx >> tile_shift)), 1.0, 0.0)
    rank = _dot(chosen.astype(BF16), before.astype(BF16))
    e_row = lax.broadcasted_iota(jnp.int32, (N_EXPERTS, N_EXPERTS), 0)
    e_col = lax.broadcasted_iota(jnp.int32, (N_EXPERTS, N_EXPERTS), 1)
    below = jnp.where(e_col < e_row, 1.0, 0.0).astype(BF16)
    lane_tile = lax.broadcasted_iota(jnp.int32, (1, n), 1) >> tile_shift
    sizes, starts = [], []
    first_row = jnp.zeros((N_EXPERTS, n), F32)
    for ti in range(n // MOE_TILE):
        count = jnp.sum(chosen[:, ti * MOE_TILE:(ti + 1) * MOE_TILE], axis=1, keepdims=True)
        padded = jnp.floor((count + (SORT_ALIGN - 1)) * (1.0 / SORT_ALIGN)) * SORT_ALIGN
        padded = jnp.broadcast_to(padded, (N_EXPERTS, LANES))
        start = _dot(below, padded.astype(BF16))
        first_row = jnp.where(lane_tile == ti, start[:, 0:1], first_row)
        sizes.append(padded)
        starts.append(start)
    row = first_row + rank
    pos = jnp.concatenate([jnp.sum(jnp.where(h, row, 0.0), axis=0, keepdims=True) for h in hits], axis=0)
    wts = jnp.concatenate([jnp.sum(jnp.where(h, gates, 0.0), axis=0, keepdims=True) for h in hits], axis=0)
    return pos, wts, sizes, starts


def _outproj_kernel(gla_ref, att_ref, x_ref, wo_ref, g1_ref, sh_ref, sc_ref, ng_ref, rw_ref, rwh_ref,
                    rb_ref, x1_ref, xm_ref, pos_ref, wts_ref, cnt_ref, start_ref):
    bb, tb, d = x_ref.shape
    tm = bb * tb
    y = (_dot(gla_ref[...].reshape(tm, GLA_V).astype(BF16), wo_ref[0:GLA_V, :])
         + _dot(att_ref[...].reshape(tm, SWA_Q).astype(BF16), wo_ref[GLA_V:, :]))
    x1 = x_ref[...].reshape(tm, d) + g1_ref[...] * y
    x1_ref[...] = x1.reshape(bb, tb, d)
    xm = _rms_norm(x1, ng_ref[...]) * (1.0 + sc_ref[...]) + sh_ref[...]
    xm_hi, xm_lo = _split_hi_lo(xm)
    xm_ref[...] = xm_hi.reshape(bb, tb, d)
    lg = _dot(xm_hi, rw_ref[...])
    logits = lg[:, :N_EXPERTS] + lg[:, N_EXPERTS:] + _dot(xm_lo, rwh_ref[...])
    lt = jnp.concatenate([logits, jnp.zeros((tm, LANES - N_EXPERTS), F32)], axis=1).T[:N_EXPERTS, :]
    scores = _sigmoid(lt)
    pos, wts, sizes, starts = _route(scores + rb_ref[...], scores)
    tiles_per_batch = tb // MOE_TILE
    for ti in range(tm // MOE_TILE):
        at = (ti // tiles_per_batch, ti % tiles_per_batch)
        pos_ref[at] = pos[:, ti * MOE_TILE:(ti + 1) * MOE_TILE]
        wts_ref[at] = wts[:, ti * MOE_TILE:(ti + 1) * MOE_TILE]
        cnt_ref[at] = sizes[ti]
        start_ref[at] = starts[ti]


OUTPROJ_TILE = 512


def _outproj(gla_out, att_out, x, w_out, g1, sh2, sc2, norm_g, rw_cat, rw_hi, rbias):
    b, t, d = x.shape
    nmod = g1.shape[0]
    tb = min(t, OUTPROJ_TILE)
    bb = OUTPROJ_TILE // tb if nmod == 1 else 1
    tpb = tb // MOE_TILE
    mod_map = (lambda i, j: (i, 0, 0)) if nmod > 1 else (lambda i, j: (0, 0, 0))
    row = lambda i, j: (i, j, 0)
    full = lambda i, j: (0, 0)
    tile = lambda i, j: (i, j, 0, 0)
    nt = t // MOE_TILE
    return pl.pallas_call(
        _outproj_kernel,
        out_shape=(jax.ShapeDtypeStruct((b, t, d), F32),
                   jax.ShapeDtypeStruct((b, t, d), BF16),
                   jax.ShapeDtypeStruct((b, nt, TOP_K, MOE_TILE), F32),
                   jax.ShapeDtypeStruct((b, nt, TOP_K, MOE_TILE), F32),
                   jax.ShapeDtypeStruct((b, nt, N_EXPERTS, LANES), F32),
                   jax.ShapeDtypeStruct((b, nt, N_EXPERTS, LANES), F32)),
        grid=(b // bb, t // tb),
        in_specs=[pl.BlockSpec((bb, tb, GLA_V), row),
                  pl.BlockSpec((bb, tb, SWA_Q), row),
                  pl.BlockSpec((bb, tb, d), row),
                  pl.BlockSpec((d, d), full),
                  pl.BlockSpec((None, 1, d), mod_map),
                  pl.BlockSpec((None, 1, d), mod_map),
                  pl.BlockSpec((None, 1, d), mod_map),
                  pl.BlockSpec((1, d), full),
                  pl.BlockSpec((d, 2 * N_EXPERTS), full),
                  pl.BlockSpec((d, N_EXPERTS), full),
                  pl.BlockSpec((N_EXPERTS, 1), full)],
        out_specs=(pl.BlockSpec((bb, tb, d), row),
                   pl.BlockSpec((bb, tb, d), row),
                   pl.BlockSpec((bb, tpb, TOP_K, MOE_TILE), tile),
                   pl.BlockSpec((bb, tpb, TOP_K, MOE_TILE), tile),
                   pl.BlockSpec((bb, tpb, N_EXPERTS, LANES), tile),
                   pl.BlockSpec((bb, tpb, N_EXPERTS, LANES), tile)),
        compiler_params=pltpu.CompilerParams(dimension_semantics=("arbitrary", "arbitrary"),
                                             vmem_limit_bytes=VMEM_LIMIT),
        name="outproj",
    )(gla_out, att_out, x, w_out, g1, sh2, sc2, norm_g, rw_cat, rw_hi, rbias)


MOE_TILE = 256
SORT_ALIGN = 16
SORT_ROWS = 3072
ROW_TILE = 512
GATHER_SLOTS = 9
FFN_CHAINS = 4
COMBINE_CHUNK = 1024
ALWAYS_ROWS = 2560
COMBINE_TAIL = 512


def _moe_sort_kernel(tiles_a, used_ref, xa_ref, xb_ref, pos_ref, xs_ref):
    i = pl.program_id(0)
    x = jnp.where(i < tiles_a, xa_ref[...], xb_ref[...])
    pos = pos_ref[...]
    tm = x.shape[0]
    used = used_ref[i]

    rows = lax.broadcasted_iota(jnp.int32, (tm, tm), 0).astype(F32).astype(BF16)
    one = jnp.ones((tm, tm), BF16)

    def fill(blk):
        local = (pos - float(blk * tm)).astype(BF16)
        onehot = jnp.zeros((tm, tm), BF16)
        for k in range(TOP_K):
            onehot = jnp.where(rows == local[k:k + 1, :], one, onehot)
        xs_ref[blk * tm:(blk + 1) * tm, :] = _dot(onehot, x).astype(BF16)

    for blk in range(SORT_ROWS // tm):
        if (blk + 1) * tm <= ALWAYS_ROWS:
            fill(blk)
        else:
            pl.when(blk * tm < used)(functools.partial(fill, blk))

            @pl.when(blk * tm >= used)
            def _():
                xs_ref[blk * tm:(blk + 1) * tm, :] = jnp.zeros((tm, D_MODEL), BF16)


def _moe_sort(xm_a, xm_b, pos, used):
    d = xm_a.shape[1]
    nt, _, tm = pos.shape
    tiles_a = xm_a.shape[0] // tm
    grid_spec = pltpu.PrefetchScalarGridSpec(
        num_scalar_prefetch=1,
        grid=(nt,),
        in_specs=[pl.BlockSpec((tm, d), lambda i, u: (jnp.minimum(i, tiles_a - 1), 0)),
                  pl.BlockSpec((tm, d), lambda i, u: (jnp.maximum(i - tiles_a, 0), 0)),
                  pl.BlockSpec((None, TOP_K, tm), lambda i, u: (i, 0, 0))],
        out_specs=pl.BlockSpec((SORT_ROWS, d), lambda i, u: (i, 0)))
    return pl.pallas_call(
        functools.partial(_moe_sort_kernel, tiles_a),
        out_shape=jax.ShapeDtypeStruct((nt * SORT_ROWS, d), BF16),
        grid_spec=grid_spec,
        compiler_params=pltpu.CompilerParams(dimension_semantics=("arbitrary",),
                                             vmem_limit_bytes=VMEM_LIMIT),
        name="moe_sort",
    )(used, xm_a, xm_b, pos)


def _moe_row_tiles(n_tokens):
    rows = n_tokens * TOP_K + (n_tokens // MOE_TILE) * N_EXPERTS * (SORT_ALIGN - 1) + N_EXPERTS * (ROW_TILE - 1)
    return -(-rows // ROW_TILE) + GATHER_SLOTS - 1


PLAN_CHUNK = 1280


def _int_dot_r(a, onehot):
    hi = jnp.floor(a * (1.0 / 256.0))
    return _dot(hi.astype(BF16), onehot) * 256.0 + _dot((a - hi * 256.0).astype(BF16), onehot)


def _int_dot_l(onehot, b):
    hi = jnp.floor(b * (1.0 / 256.0))
    return _dot(onehot, hi.astype(BF16)) * 256.0 + _dot(onehot, (b - hi * 256.0).astype(BF16))


def _moe_plan_kernel(cnt_ref, start_ref, src_ref, first_ref, tiles_ref, nu_ref, back_ref):
    nt, ne = cnt_ref.shape
    gpt = SORT_ROWS // SORT_ALIGN
    gpr = ROW_TILE // SORT_ALIGN
    gc = cnt_ref[...] * (1.0 / SORT_ALIGN)
    ls = start_ref[...] * (1.0 / SORT_ALIGN)

    def transpose(x):
        x = jnp.concatenate([x, jnp.zeros((nt, LANES - ne), F32)], axis=1)
        x = jnp.concatenate([x, jnp.zeros((LANES - nt, LANES), F32)], axis=0)
        return x.T[:ne, :nt]

    def tri(n, keep):
        return jnp.where(keep(lax.broadcasted_iota(jnp.int32, (n, n), 0),
                              lax.broadcasted_iota(jnp.int32, (n, n), 1)), 1.0, 0.0).astype(BF16)

    gc_t = transpose(gc)
    ls_t = transpose(ls)
    tot_c = jnp.broadcast_to(jnp.sum(gc_t, axis=1, keepdims=True), (ne, LANES))
    ptot_c = jnp.floor((tot_c + (gpr - 1)) * (1.0 / gpr)) * gpr
    gend_c = _int_dot_l(tri(ne, lambda r, c: c <= r), ptot_c)
    gstart_c = gend_c - ptot_c
    n_used = gend_c[ne - 1:ne, :] * (1.0 / gpr)
    nu_ref[...] = n_used.astype(jnp.int32)
    tot_r = jnp.sum(gc, axis=0, keepdims=True)
    ptot_r = jnp.floor((tot_r + (gpr - 1)) * (1.0 / gpr)) * gpr
    gstart_r = _int_dot_r(jnp.broadcast_to(ptot_r, (8, ne)), tri(ne, lambda r, c: r < c))
    cumex = _dot(tri(nt, lambda r, c: c < r), gc.astype(BF16))
    cumex_t = _dot(gc_t.astype(BF16), tri(nt, lambda r, c: r < c))
    tile_base = lax.broadcasted_iota(jnp.int32, (nt, ne), 0).astype(F32) * gpt + ls
    table = jnp.concatenate([cumex + gc, cumex, tile_base, gstart_r, jnp.broadcast_to(tot_r, (8, ne))], axis=0)

    e_iota = lax.broadcasted_iota(jnp.int32, (ne, PLAN_CHUNK), 0).astype(F32)
    for ch in range(src_ref.shape[1] // PLAN_CHUNK):
        g = (lax.broadcasted_iota(jnp.int32, (1, PLAN_CHUNK), 1) + ch * PLAN_CHUNK).astype(F32)
        eg = jnp.sum(jnp.where(gend_c[:, 0:1] <= g, 1.0, 0.0), axis=0, keepdims=True)
        picked = _int_dot_r(table, jnp.where(e_iota == eg, 1.0, 0.0).astype(BF16))
        cum_g, cumex_g, base_g = picked[0:nt], picked[nt:2 * nt], picked[2 * nt:3 * nt]
        u = g - picked[3 * nt:3 * nt + 1]
        in_tile = (cumex_g <= u) & (u < cum_g)
        src = jnp.sum(jnp.where(in_tile, base_g - cumex_g, 0.0), axis=0, keepdims=True) + u
        src = jnp.where(u < picked[3 * nt + 8:3 * nt + 9], src, gpt - 1.0)
        src_ref[:, ch * PLAN_CHUNK:(ch + 1) * PLAN_CHUNK] = src.astype(jnp.int32)

    first_ref[...] = (gstart_c * (1.0 / gpr)).astype(jnp.int32)
    tiles_ref[...] = (ptot_c * (1.0 / gpr)).astype(jnp.int32)

    lg = lax.broadcasted_iota(jnp.int32, (ne, back_ref.shape[1]), 1).astype(F32)
    for t in range(nt):
        first = ls_t[:, t:t + 1]
        inside = (first <= lg) & (lg < first + gc_t[:, t:t + 1])
        shift = gstart_c[:, 0:1] + cumex_t[:, t:t + 1] - first
        val = jnp.sum(jnp.where(inside, shift + lg, 0.0), axis=0, keepdims=True)
        back_ref[t:t + 1, :] = val.astype(jnp.int32)


def _moe_plan(cnt, start):
    nt, ne = cnt.shape
    row_tiles = _moe_row_tiles(nt * MOE_TILE)
    gpt = SORT_ROWS // SORT_ALIGN
    gpr = ROW_TILE // SORT_ALIGN
    n_src = -(-(row_tiles * gpr) // PLAN_CHUNK) * PLAN_CHUNK
    n_back = -(-gpt // LANES) * LANES
    src, first, tiles, nu, back = pl.pallas_call(
        _moe_plan_kernel,
        out_shape=(jax.ShapeDtypeStruct((1, n_src), jnp.int32),
                   jax.ShapeDtypeStruct((ne, LANES), jnp.int32),
                   jax.ShapeDtypeStruct((ne, LANES), jnp.int32),
                   jax.ShapeDtypeStruct((1, LANES), jnp.int32),
                   jax.ShapeDtypeStruct((nt, n_back), jnp.int32)),
        compiler_params=pltpu.CompilerParams(vmem_limit_bytes=VMEM_LIMIT),
        name="moe_plan",
    )(cnt, start)
    return nu[0, :1], first[:, 0], tiles[:, 0], src[0, :row_tiles * gpr], back[:, :gpt]


def _moe_experts_kernel(nu_ref, first_ref, tiles_ref, src_ref, xs_hbm, wg_ref, wu_ref, wd_ref, ys_hbm,
                        xbuf, ybuf, gsem, osem, wgu_s, wd_s):
    e = pl.program_id(0)
    n_used = nu_ref[0]
    gpr = ROW_TILE // SORT_ALIGN
    part = ROW_TILE // FFN_CHAINS

    def gather(tile, to_slot, j0=0, j1=gpr):
        for j in range(j0, j1):
            row = pl.multiple_of(src_ref[tile * gpr + j] * SORT_ALIGN, SORT_ALIGN)
            pltpu.make_async_copy(xs_hbm.at[pl.ds(row, SORT_ALIGN), :],
                                  xbuf.at[to_slot, j * SORT_ALIGN:(j + 1) * SORT_ALIGN, :],
                                  gsem.at[to_slot]).start(priority=j % 2)

    def drain(of_slot):
        for j in range(gpr):
            pltpu.make_async_copy(xs_hbm.at[0:SORT_ALIGN, :],
                                  xbuf.at[of_slot, j * SORT_ALIGN:(j + 1) * SORT_ALIGN, :], gsem.at[of_slot]).wait()

    def out_copy(tile, of_slot):
        row = pl.multiple_of(tile * ROW_TILE, ROW_TILE)
        return pltpu.make_async_copy(ybuf.at[of_slot], ys_hbm.at[pl.ds(row, ROW_TILE), :], osem.at[of_slot])

    @pl.when(e == 0)
    def _():
        for ahead in range(GATHER_SLOTS - 1):
            gather(ahead, ahead)

    wgu_s[:, :EXPERT_FF] = wg_ref[...].astype(BF16)
    wgu_s[:, EXPERT_FF:] = wu_ref[...].astype(BF16)
    wd_s[...] = wd_ref[...].astype(BF16)

    def row_tile(i, carry):
        r = first_ref[e] + i
        slot = lax.rem(r, GATHER_SLOTS)
        oslot = lax.rem(r, 2)
        next_slot = lax.rem(r + GATHER_SLOTS - 1, GATHER_SLOTS)
        drain(slot)

        @pl.when(r >= 2)
        def _():
            out_copy(r - 2, oslot).wait()

        abs_ = []
        for c in range(FFN_CHAINS):
            abs_.append(_dot(xbuf[slot, c * part:(c + 1) * part, :], wgu_s[...]))
            gather(r + GATHER_SLOTS - 1, next_slot, c * gpr // FFN_CHAINS, (c + 1) * gpr // FFN_CHAINS)
        hs = [(_silu(ab[:, :EXPERT_FF]) * ab[:, EXPERT_FF:]).astype(BF16) for ab in abs_]
        ys = [_dot(h, wd_s[...]).astype(BF16) for h in hs]
        for c in range(FFN_CHAINS):
            ybuf[oslot, c * part:(c + 1) * part, :] = ys[c]
        out_copy(r, oslot).start()
        return carry

    lax.fori_loop(0, tiles_ref[e], row_tile, 0)

    @pl.when(e == pl.num_programs(0) - 1)
    def _():
        for ahead in range(GATHER_SLOTS - 1):
            drain(lax.rem(n_used + ahead, GATHER_SLOTS))
        out_copy(n_used - 1, lax.rem(n_used - 1, 2)).wait()

        @pl.when(n_used >= 2)
        def _():
            out_copy(n_used - 2, lax.rem(n_used, 2)).wait()


def _moe_experts(n_used, first, tiles, src, xs, wg, wu, wd, row_tiles):
    d = xs.shape[-1]
    ne = wg.shape[0]
    w_map = lambda e, nu, fi, ti, sr: (e, 0, 0)
    grid_spec = pltpu.PrefetchScalarGridSpec(
        num_scalar_prefetch=4,
        grid=(ne,),
        in_specs=[pl.BlockSpec(memory_space=pl.ANY),
                  pl.BlockSpec((None, d, EXPERT_FF), w_map),
                  pl.BlockSpec((None, d, EXPERT_FF), w_map),
                  pl.BlockSpec((None, EXPERT_FF, d), w_map)],
        out_specs=pl.BlockSpec(memory_space=pl.ANY),
        scratch_shapes=[pltpu.VMEM((GATHER_SLOTS, ROW_TILE, d), BF16),
                        pltpu.VMEM((2, ROW_TILE, d), BF16),
                        pltpu.SemaphoreType.DMA((GATHER_SLOTS,)),
                        pltpu.SemaphoreType.DMA((2,)),
                        pltpu.VMEM((d, 2 * EXPERT_FF), BF16),
                        pltpu.VMEM((EXPERT_FF, d), BF16)])
    return pl.pallas_call(
        _moe_experts_kernel,
        out_shape=jax.ShapeDtypeStruct((row_tiles * ROW_TILE, d), BF16),
        grid_spec=grid_spec,
        compiler_params=pltpu.CompilerParams(dimension_semantics=("arbitrary",),
                                             vmem_limit_bytes=VMEM_LIMIT),
        name="moe_experts",
    )(n_used, first, tiles, src, xs, wg, wu, wd)


def _moe_combine_kernel(back_ref, used_ref, ys_hbm, pos_ref, wts_ref, xm_ref, x1_ref, g2_ref, fg_ref,
                        swg_ref, swu_ref, swd_ref, o_ref, buf, sem, acc_ref):
    i = pl.program_id(0)
    gpt = SORT_ROWS // SORT_ALIGN
    slot = lax.rem(i, 2)
    always = ALWAYS_ROWS
    tail = range(always, SORT_ROWS, COMBINE_TAIL)

    def copies(tile, of_slot, g0, g1, start):
        for g in range(g0, g1):
            row = pl.multiple_of(back_ref[tile * gpt + g] * SORT_ALIGN, SORT_ALIGN) if start else 0
            cp = pltpu.make_async_copy(ys_hbm.at[pl.ds(row, SORT_ALIGN), :],
                                       buf.at[of_slot, g * SORT_ALIGN:(g + 1) * SORT_ALIGN, :], sem.at[of_slot])
            if start:
                cp.start(priority=g % 2)
            else:
                cp.wait()

    def transfer(tile, of_slot, start):
        copies(tile, of_slot, 0, always // SORT_ALIGN, start)
        for c0 in tail:
            pl.when(c0 < used_ref[tile])(functools.partial(
                copies, tile, of_slot, c0 // SORT_ALIGN, (c0 + COMBINE_TAIL) // SORT_ALIGN, start))

    @pl.when(i == 0)
    def _():
        transfer(0, 0, True)

    @pl.when(i + 1 < pl.num_programs(0))
    def _():
        transfer(i + 1, 1 - slot, True)

    x = xm_ref[...]
    tm = x.shape[0]
    pad = jnp.zeros((LANES - TOP_K, tm), F32)
    pos_t = jnp.concatenate([pos_ref[...], pad], axis=0).T
    wts_t = jnp.concatenate([wts_ref[...], pad], axis=0).T
    blk_b, loc_b, wts_b = [], [], []
    for k in range(TOP_K):
        p = jnp.broadcast_to(pos_t[:, k:k + 1], (tm, LANES))
        blk = jnp.floor(p * (1.0 / tm))
        two = lambda v: jnp.concatenate([v.astype(BF16)] * (tm // LANES), axis=1)
        blk_b.append(two(blk))
        loc_b.append(two(p - blk * tm))
        wts_b.append(two(jnp.broadcast_to(wts_t[:, k:k + 1], (tm, LANES))))
    shared = _dot((_silu(_dot(x, swg_ref[...])) * _dot(x, swu_ref[...])).astype(BF16), swd_ref[...])
    transfer(i, slot, False)
    lane = lax.broadcasted_iota(jnp.int32, (tm, tm), 1).astype(F32).astype(BF16)
    zero = jnp.zeros((tm, tm), BF16)
    nowhere = jnp.full((tm, tm), -1.0, BF16)

    def apply(c0, width):
        blocks = []
        for b0 in range(c0, c0 + width, tm):
            comb = zero
            for k in range(TOP_K):
                loc = jnp.where(blk_b[k] == float(b0 // tm), loc_b[k], nowhere)
                comb = jnp.where(lane == loc, wts_b[k], comb)
            blocks.append(comb)
        return _dot(jnp.concatenate(blocks, axis=1), buf[slot, c0:c0 + width, :])

    routed = shared
    for c0 in range(0, always, COMBINE_CHUNK):
        routed = routed + apply(c0, min(COMBINE_CHUNK, always - c0))
    acc_ref[...] = routed
    for c0 in tail:
        @pl.when(c0 < used_ref[i])
        def _(c0=c0):
            acc_ref[...] += apply(c0, COMBINE_TAIL)
    y = x1_ref[...] + g2_ref[...] * acc_ref[...]
    o_ref[...] = _rms_norm(y, fg_ref[...])


def _moe_combine(back, used, ys, pos, wts, xm, x1, g2, final_g, swg, swu, swd, *, tiles_per_mod):
    n, d = xm.shape
    tm = pos.shape[-1]
    nt = n // tm
    gpt = SORT_ROWS // SORT_ALIGN
    row = lambda i, bk, us: (i, 0)
    full = lambda i, bk, us: (0, 0)
    tile = lambda i, bk, us: (i, 0, 0)
    mod_map = lambda i, bk, us: (i // tiles_per_mod, 0, 0)
    grid_spec = pltpu.PrefetchScalarGridSpec(
        num_scalar_prefetch=2,
        grid=(nt,),
        in_specs=[pl.BlockSpec(memory_space=pl.ANY),
                  pl.BlockSpec((None, TOP_K, tm), tile),
                  pl.BlockSpec((None, TOP_K, tm), tile),
                  pl.BlockSpec((tm, d), row),
                  pl.BlockSpec((tm, d), row),
                  pl.BlockSpec((None, 1, d), mod_map),
                  pl.BlockSpec((1, d), full),
                  pl.BlockSpec((d, SHARED_FF), full),
                  pl.BlockSpec((d, SHARED_FF), full),
                  pl.BlockSpec((SHARED_FF, d), full)],
        out_specs=pl.BlockSpec((tm, d), row),
        scratch_shapes=[pltpu.VMEM((2, SORT_ROWS, d), BF16),
                        pltpu.SemaphoreType.DMA((2,)),
                        pltpu.VMEM((tm, d), F32)])
    return pl.pallas_call(
        _moe_combine_kernel,
        out_shape=jax.ShapeDtypeStruct((n, d), F32),
        grid_spec=grid_spec,
        compiler_params=pltpu.CompilerParams(dimension_semantics=("arbitrary",),
                                             vmem_limit_bytes=VMEM_LIMIT),
        name="moe_combine",
    )(back, used, ys, pos.reshape(nt, TOP_K, tm), wts.reshape(nt, TOP_K, tm), xm, x1, g2, final_g, swg, swu, swd)


def _mix(x, mods, p, attn_fn, s0=None):
    sh1, sc1, g1, sh2, sc2, _ = mods
    gla_in, lora, q_s, k_s, v_s = _inproj(x, p["norm_attn_g"], sh1, sc1, p["w_gla"], p["w_lora"], p["w_swa"])
    if s0 is None:
        gla_out, s_f, s_b = _gla(gla_in, lora, p["waf"], p["baf"], p["wab"], p["bab"], p["gla_norm_g"])
    else:
        gla_out, s_f, s_b = _gla(gla_in, lora, p["waf"], p["baf"], p["wab"], p["bab"], p["gla_norm_g"],
                                 s0[0], s0[1])
    att_out = attn_fn(q_s, k_s, v_s)
    routed = _outproj(gla_out, att_out, x, p["w_out"], g1, sh2, sc2, p["norm_ffn_g"],
                      p["rw_cat"], p["rw_hi"], p["rbias"])
    return routed, k_s, v_s, s_f, s_b


def _moe(streams, p):
    d = D_MODEL
    (ra, _), (rb, _) = streams
    n_tiles = [r[1].shape[0] * r[1].shape[1] // MOE_TILE for r, _ in streams]
    pos_all = jnp.concatenate([r[2].reshape(-1, TOP_K, MOE_TILE) for r, _ in streams], axis=0)
    cnt_all = jnp.concatenate([r[4][..., 0].reshape(-1, N_EXPERTS) for r, _ in streams], axis=0)
    start_all = jnp.concatenate([r[5][..., 0].reshape(-1, N_EXPERTS) for r, _ in streams], axis=0)
    used = (start_all[:, -1] + cnt_all[:, -1]).astype(jnp.int32)
    xs = _moe_sort(ra[1].reshape(-1, d), rb[1].reshape(-1, d), pos_all, used)
    n_used, first, tiles, src, back = _moe_plan(cnt_all, start_all)
    ys = _moe_experts(n_used, first, tiles, src, xs, p["wg"], p["wu"], p["wd"],
                      _moe_row_tiles(cnt_all.shape[0] * MOE_TILE))
    outs = []
    tile0 = 0
    for ((x1, xm, pos, wts, cnt, start), g2), nt in zip(streams, n_tiles):
        b, t, _ = x1.shape
        tiles_per_mod = (t // MOE_TILE) if g2.shape[0] > 1 else nt
        y = _moe_combine(back[tile0:tile0 + nt].reshape(-1), used[tile0:tile0 + nt], ys, pos, wts,
                         xm.reshape(-1, d), x1.reshape(-1, d), g2, p["final_norm_g"],
                         p["swg"], p["swu"], p["swd"], tiles_per_mod=tiles_per_mod)
        outs.append(y.reshape(b, t, d))
        tile0 += nt
    return outs


def kernel(x_prompt, x_sample, c, cache_swa_k, cache_swa_v, state_gla_fwd, state_gla_bwd, c_ctx, w_ada, b_ada, norm_attn_g, norm_ffn_g, w_in, gla_wa_f, gla_ba_f, gla_wa_b, gla_ba_b, gla_norm_g, swa_sink, w_out, router_w, router_bias, exp_w_gate, exp_w_up, exp_w_down, sh_w_gate, sh_w_up, sh_w_down, final_norm_g):
    l = 0
    d = D_MODEL
    nb_ctx, t_ctx, _ = x_prompt.shape
    nb_lat, t_lat, _ = x_sample.shape

    pad = jnp.zeros((8 - 1 - nb_lat, d), F32)
    cond8 = jnp.concatenate([c_ctx[None, :], c, pad], axis=0)
    mod = _adaln(cond8, w_ada[l], b_ada[l][None, :])
    mods_ctx = [mod[0:1, i * d:(i + 1) * d][:, None, :] for i in range(6)]
    mods_lat = [mod[1:1 + nb_lat, i * d:(i + 1) * d][:, None, :] for i in range(6)]

    zeros_lora = jnp.zeros((GLA_LORA, GLA_QK), F32)
    rw = router_w[l]
    rw_hi = rw.astype(BF16)
    rw_lo = (rw - rw_hi.astype(F32)).astype(BF16)
    w_in_b = w_in[l].astype(BF16)
    p = {
        "norm_attn_g": norm_attn_g[l][None, :],
        "norm_ffn_g": norm_ffn_g[l][None, :],
        "final_norm_g": final_norm_g[None, :],
        "w_gla": w_in_b[:, :2 * GLA_QK + 2 * GLA_V],
        "w_lora": w_in_b[:, 2 * GLA_QK + 2 * GLA_V:2 * GLA_QK + 2 * GLA_V + 2 * GLA_LORA],
        "w_swa": w_in_b[:, 2 * GLA_QK + 2 * GLA_V + 2 * GLA_LORA:],
        "waf": jnp.concatenate([gla_wa_f[l], zeros_lora], axis=0).astype(BF16),
        "wab": jnp.concatenate([zeros_lora, gla_wa_b[l]], axis=0).astype(BF16),
        "baf": gla_ba_f[l][None, :],
        "bab": gla_ba_b[l][None, :],
        "gla_norm_g": gla_norm_g[l][None, :],
        "w_out": w_out[l].astype(BF16),
        "rw_cat": jnp.concatenate([rw_hi, rw_lo], axis=1),
        "rw_hi": rw_hi,
        "rbias": router_bias[l][:, None],
        "wg": exp_w_gate[l], "wu": exp_w_up[l], "wd": exp_w_down[l],
        "swg": sh_w_gate[l].astype(BF16), "swu": sh_w_up[l].astype(BF16),
        "swd": sh_w_down[l].astype(BF16),
    }
    sink = swa_sink[l]

    routed_ctx, k_c, v_c, s_f, s_b = _mix(x_prompt, mods_ctx, p, functools.partial(_attn_ctx, sink))

    cos, sin_lo, sin_hi = _rope_tables(t_lat)
    kc = cache_swa_k[:, l].reshape(nb_lat, -1, SWA_KV)
    vc = cache_swa_v[:, l].reshape(nb_lat, -1, SWA_KV)
    lat_attn = lambda q, k, v: _attn_lat(sink, q, k, v, kc, vc, cos, sin_lo, sin_hi)
    s0 = (state_gla_fwd[:, l].reshape(nb_lat, GLA_QK, GLA_DV),
          state_gla_bwd[:, l].reshape(nb_lat, GLA_QK, GLA_DV))
    routed_lat, _, _, _, _ = _mix(x_sample, mods_lat, p, lat_attn, s0)
    y_prompt, y_sample = _moe([(routed_ctx, mods_ctx[5]), (routed_lat, mods_lat[5])], p)

    new_k = k_c.reshape(nb_ctx, 1, t_ctx, SWA_KV_HEADS, SWA_HEAD_DIM)
    new_v = v_c.reshape(nb_ctx, 1, t_ctx, SWA_KV_HEADS, SWA_HEAD_DIM)
    new_sf = s_f.reshape(nb_ctx, 1, GLA_HEADS, GLA_DK, GLA_DV)
    new_sb = s_b.reshape(nb_ctx, 1, GLA_HEADS, GLA_DK, GLA_DV)
    return (y_prompt, y_sample, new_k, new_v, new_sf, new_sb)
```

```python
import functools

import jax
import jax.numpy as jnp
from jax import lax
from jax.experimental import pallas as pl
from jax.experimental.pallas import tpu as pltpu

F32 = jnp.float32
BF16 = jnp.bfloat16

D_MODEL = 1024
GLA_HEADS = 4
GLA_DK = 64
GLA_DV = 128
GLA_LORA = 16
GLA_GATE_NORM = 16.0
GLA_CHUNK = 64
GLA_QK = GLA_HEADS * GLA_DK
GLA_V = GLA_HEADS * GLA_DV
SWA_HEAD_DIM = 64
SWA_HEADS = 8
SWA_KV_HEADS = 2
SWA_Q = SWA_HEADS * SWA_HEAD_DIM
SWA_KV = SWA_KV_HEADS * SWA_HEAD_DIM
ATTN_BLOCK = 128
GRID_W = 64
ROPE_BASE = 10000.0
N_EXPERTS = 64
TOP_K = 8
N_EXPERT_GROUPS = 8
TOPK_GROUPS = 4
EXPERT_FF = 128
SHARED_FF = 256
ROUTED_SCALE = 2.5
EPS = 1e-6

LANES = 128
VMEM_LIMIT = 56 * 1024 * 1024

NEG_INF = float("-inf")


def _dot(a, b):
    return jnp.dot(a, b, preferred_element_type=F32)


def _dot_nt(a, b):
    return lax.dot_general(a, b, (((1,), (1,)), ((), ())), preferred_element_type=F32)


def _split_hi_lo(x):
    hi = x.astype(BF16)
    lo = (x - hi.astype(F32)).astype(BF16)
    return hi, lo


def _sigmoid(x):
    return 1.0 / (1.0 + jnp.exp(-x))


def _silu(x):
    return x * _sigmoid(x)


def _rms_norm(x, g):
    ms = jnp.mean(x * x, axis=-1, keepdims=True)
    return x * lax.rsqrt(ms + EPS) * g


def _adaln_kernel(c_ref, w_ref, b_ref, o_ref):
    a_hi, a_lo = _split_hi_lo(_silu(c_ref[...]))
    w_hi, w_lo = _split_hi_lo(w_ref[...])
    o_ref[...] = _dot(a_hi, w_hi) + _dot(a_lo, w_hi) + _dot(a_hi, w_lo) + b_ref[...]


def _adaln(cond8, w_ada, b_ada):
    n = w_ada.shape[1]
    tn = 1536
    return pl.pallas_call(
        _adaln_kernel,
        out_shape=jax.ShapeDtypeStruct((8, n), F32),
        grid=(n // tn,),
        in_specs=[pl.BlockSpec((8, D_MODEL), lambda j: (0, 0)),
                  pl.BlockSpec((D_MODEL, tn), lambda j: (0, j)),
                  pl.BlockSpec((1, tn), lambda j: (0, j))],
        out_specs=pl.BlockSpec((8, tn), lambda j: (0, j)),
        compiler_params=pltpu.CompilerParams(dimension_semantics=("arbitrary",),
                                             vmem_limit_bytes=VMEM_LIMIT),
        name="adaln",
    )(cond8, w_ada, b_ada)


def _inproj_kernel(x_ref, g_ref, sh_ref, sc_ref, wg_ref, wl_ref, ws_ref,
                   gla_ref, lora_ref, q_ref, k_ref, v_ref):
    bb, tb, d = x_ref.shape
    x = x_ref[...].reshape(bb * tb, d)
    h = _rms_norm(x, g_ref[...]) * (1.0 + sc_ref[...]) + sh_ref[...]
    hb = h.astype(BF16)
    gla_ref[...] = _dot(hb, wg_ref[...]).reshape(gla_ref.shape)
    lora_ref[...] = _dot(hb, wl_ref[...]).reshape(lora_ref.shape)
    s = _dot(hb, ws_ref[...])
    q_ref[...] = s[:, :SWA_Q].reshape(q_ref.shape)
    k_ref[...] = s[:, SWA_Q:SWA_Q + SWA_KV].reshape(k_ref.shape)
    v_ref[...] = s[:, SWA_Q + SWA_KV:].reshape(v_ref.shape)


INPROJ_TILE = 512


def _inproj(x, g, sh, sc, w_gla, w_lora, w_swa):
    b, t, d = x.shape
    nmod = sh.shape[0]
    tb = min(t, INPROJ_TILE)
    bb = INPROJ_TILE // tb if nmod == 1 else 1
    mod_map = (lambda i, j: (i, 0, 0)) if nmod > 1 else (lambda i, j: (0, 0, 0))
    row = lambda i, j: (i, j, 0)
    full = lambda i, j: (0, 0)
    n_gla = w_gla.shape[1]
    n_lora = w_lora.shape[1]
    return pl.pallas_call(
        _inproj_kernel,
        out_shape=(jax.ShapeDtypeStruct((b, t, n_gla), F32),
                   jax.ShapeDtypeStruct((b, t, n_lora), F32),
                   jax.ShapeDtypeStruct((b, t, SWA_Q), F32),
                   jax.ShapeDtypeStruct((b, t, SWA_KV), F32),
                   jax.ShapeDtypeStruct((b, t, SWA_KV), F32)),
        grid=(b // bb, t // tb),
        in_specs=[pl.BlockSpec((bb, tb, d), row),
                  pl.BlockSpec((1, d), full),
                  pl.BlockSpec((None, 1, d), mod_map),
                  pl.BlockSpec((None, 1, d), mod_map),
                  pl.BlockSpec((d, n_gla), full),
                  pl.BlockSpec((d, n_lora), full),
                  pl.BlockSpec((d, w_swa.shape[1]), full)],
        out_specs=(pl.BlockSpec((bb, tb, n_gla), row),
                   pl.BlockSpec((bb, tb, n_lora), row),
                   pl.BlockSpec((bb, tb, SWA_Q), row),
                   pl.BlockSpec((bb, tb, SWA_KV), row),
                   pl.BlockSpec((bb, tb, SWA_KV), row)),
        compiler_params=pltpu.CompilerParams(dimension_semantics=("arbitrary", "arbitrary"),
                                             vmem_limit_bytes=VMEM_LIMIT),
        name="inproj",
    )(x, g, sh, sc, w_gla, w_lora, w_swa)


SCAN_UNROLL = 4
OUT_UNROLL = 4


def _log_sigmoid(x):
    return jnp.minimum(x, 0.0) - jnp.log(1.0 + jnp.exp(-jnp.abs(x)))


def _heads_to_rows(x):
    return jnp.concatenate([x[:, h * LANES:(h + 1) * LANES] for h in range(GLA_HEADS)], axis=0)


def _rows_to_heads(x, c):
    return jnp.concatenate([x[h * c:(h + 1) * c, :] for h in range(GLA_HEADS)], axis=1)


def _gla_kernel(has_init, q_ref, k_ref, v_ref, g_ref, lora_ref, waf_ref, baf_ref, wab_ref, bab_ref,
                ng_ref, *rest):
    if has_init:
        s0f_ref, s0b_ref, *rest = rest
    (out_ref, sf_ref, sb_ref, laf_ref, lab_ref, oacc_ref, qtf_ref, qtb_ref, saf_ref, sab_ref,
     stf_ref, stb_ref) = rest
    t = q_ref.shape[0]
    c = GLA_CHUNK
    n = t // c
    hc = GLA_HEADS * c

    lora = lora_ref[...].astype(BF16)
    laf_ref[...] = _log_sigmoid(_dot(lora, waf_ref[...]) + baf_ref[...]) * (1.0 / GLA_GATE_NORM)
    lab_ref[...] = _log_sigmoid(_dot(lora, wab_ref[...]) + bab_ref[...]) * (1.0 / GLA_GATE_NORM)

    if has_init:
        stf_ref[...] = s0f_ref[...].T
        stb_ref[...] = s0b_ref[...].T
    else:
        stf_ref[...] = jnp.zeros_like(stf_ref)
        stb_ref[...] = jnp.zeros_like(stb_ref)
    oacc_ref[...] = jnp.zeros_like(oacc_ref)

    r64 = lax.broadcasted_iota(jnp.int32, (c, c), 0)
    c64 = lax.broadcasted_iota(jnp.int32, (c, c), 1)
    tri_f = jnp.where(c64 <= r64, 1.0, 0.0).astype(BF16)
    tri_b = jnp.where(c64 >= r64, 1.0, 0.0).astype(BF16)
    rr = lax.broadcasted_iota(jnp.int32, (hc, hc), 0)
    cc = lax.broadcasted_iota(jnp.int32, (hc, hc), 1)
    same_head = (rr >> 6) == (cc >> 6)
    keep_f = same_head & ((rr & (c - 1)) >= (cc & (c - 1)))
    keep_b = same_head & ((rr & (c - 1)) <= (cc & (c - 1)))
    head_mask = jnp.where(same_head, 1.0, 0.0).astype(BF16)
    norm_g = ng_ref[...]

    def chunk_rows(ci):
        return pl.ds(pl.multiple_of(ci * c, c), c)

    def tile_heads(x):
        x4 = jnp.concatenate([x] * GLA_HEADS, axis=0)
        return jnp.where(same_head, x4, 0.0).astype(BF16)

    def scan_step(i, carry):
        dirs = []
        for u in range(SCAN_UNROLL):
            dirs += [(SCAN_UNROLL * i + u, laf_ref, tri_f, keep_f, c - 1, stf_ref, saf_ref, qtf_ref),
                     (n - 1 - SCAN_UNROLL * i - u, lab_ref, tri_b, keep_b, 0, stb_ref, sab_ref, qtb_ref)]
        cums = []
        for ci, la_ref, tri, _, _, _, _, _ in dirs:
            la_hi, la_lo = _split_hi_lo(la_ref[chunk_rows(ci), :])
            cums.append(_dot(tri, la_hi) + _dot(tri, la_lo))
        ops = []
        for (ci, _, _, _, last_row, _, _, qt_ref), cum in zip(dirs, cums):
            sl = chunk_rows(ci)
            tot = cum[last_row:last_row + 1, :]
            kc = k_ref[sl, :]
            qt = q_ref[sl, :] * (GLA_DK ** -0.5) * jnp.exp(cum)
            qt_ref[sl, :] = qt.astype(BF16)
            v_rows = _heads_to_rows(v_ref[sl, :])
            ops.append((tot, tile_heads(qt), tile_heads(kc * jnp.exp(-cum)),
                        tile_heads(kc * jnp.exp(tot - cum)), v_rows))
        atts = [_dot_nt(q4, k4) for _, q4, k4, _, _ in ops]
        incs = []
        for (_, _, _, keep, _, _, _, _), (_, _, _, kd4, v_rows), att in zip(dirs, ops, atts):
            att = jnp.where(keep, att, 0.0).astype(BF16)
            incs.append((_dot(att, v_rows.astype(BF16)), _dot(v_rows.T.astype(BF16), kd4)))
        for (ci, _, _, _, _, st_ref, snap_ref, _), (tot, _, _, _, _), (o_intra, st_inc) in zip(dirs, ops, incs):
            oacc_ref[ci] += o_intra
            st = st_ref[...]
            snap_ref[ci] = st.astype(BF16)
            st_ref[...] = jnp.exp(tot) * st + st_inc
        return carry

    def tile_heads_bf16(x):
        return jnp.concatenate([x] * GLA_HEADS, axis=0) * head_mask

    def out_step(i, carry):
        chunks = [OUT_UNROLL * i + u for u in range(OUT_UNROLL)]
        inter = []
        for ci in chunks:
            sl = chunk_rows(ci)
            q4 = jnp.concatenate([tile_heads_bf16(qtf_ref[sl, :]), tile_heads_bf16(qtb_ref[sl, :])], axis=1)
            st = jnp.concatenate([saf_ref[ci], sab_ref[ci]], axis=1)
            inter.append(_dot_nt(q4, st))
        for ci, o_inter in zip(chunks, inter):
            sl = chunk_rows(ci)
            on = _rms_norm(oacc_ref[ci] + o_inter, norm_g)
            gate = _silu(_heads_to_rows(g_ref[sl, :]))
            out_ref[sl, :] = _rows_to_heads(on * gate, c)
        return carry

    lax.fori_loop(0, n // SCAN_UNROLL, scan_step, 0)
    lax.fori_loop(0, n // OUT_UNROLL, out_step, 0)
    sf_ref[...] = stf_ref[...].T
    sb_ref[...] = stb_ref[...].T


def _gla(gla_in, lora, waf, baf, wab, bab, norm_g, s0f=None, s0b=None):
    b, t, _ = gla_in.shape
    has_init = s0f is not None
    n = t // GLA_CHUNK
    bmap = lambda i: (i, 0, 0)
    full = lambda i: (0, 0)
    in_specs = [pl.BlockSpec((None, t, GLA_QK), lambda i: (i, 0, 0)),
                pl.BlockSpec((None, t, GLA_QK), lambda i: (i, 0, 1)),
                pl.BlockSpec((None, t, GLA_V), lambda i: (i, 0, 1)),
                pl.BlockSpec((None, t, GLA_V), lambda i: (i, 0, 2)),
                pl.BlockSpec((None, t, 2 * GLA_LORA), bmap),
                pl.BlockSpec((2 * GLA_LORA, GLA_QK), full),
                pl.BlockSpec((1, GLA_QK), full),
                pl.BlockSpec((2 * GLA_LORA, GLA_QK), full),
                pl.BlockSpec((1, GLA_QK), full),
                pl.BlockSpec((1, GLA_DV), full)]
    args = [gla_in, gla_in, gla_in, gla_in, lora, waf, baf, wab, bab, norm_g]
    if has_init:
        in_specs += [pl.BlockSpec((None, GLA_QK, GLA_DV), bmap)] * 2
        args += [s0f, s0b]
    return pl.pallas_call(
        functools.partial(_gla_kernel, has_init),
        out_shape=(jax.ShapeDtypeStruct((b, t, GLA_V), F32),
                   jax.ShapeDtypeStruct((b, GLA_QK, GLA_DV), F32),
                   jax.ShapeDtypeStruct((b, GLA_QK, GLA_DV), F32)),
        grid=(b,),
        in_specs=in_specs,
        out_specs=(pl.BlockSpec((None, t, GLA_V), bmap),
                   pl.BlockSpec((None, GLA_QK, GLA_DV), bmap),
                   pl.BlockSpec((None, GLA_QK, GLA_DV), bmap)),
        scratch_shapes=[pltpu.VMEM((t, GLA_QK), F32),
                        pltpu.VMEM((t, GLA_QK), F32),
                        pltpu.VMEM((n, GLA_HEADS * GLA_CHUNK, GLA_DV), F32),
                        pltpu.VMEM((t, GLA_QK), BF16),
                        pltpu.VMEM((t, GLA_QK), BF16),
                        pltpu.VMEM((n, GLA_DV, GLA_QK), BF16),
                        pltpu.VMEM((n, GLA_DV, GLA_QK), BF16),
                        pltpu.VMEM((GLA_DV, GLA_QK), F32),
                        pltpu.VMEM((GLA_DV, GLA_QK), F32)],
        compiler_params=pltpu.CompilerParams(dimension_semantics=("arbitrary",),
                                             vmem_limit_bytes=VMEM_LIMIT),
        name="gla",
    )(*args)


def _dup_groups(x):
    lo = lax.broadcasted_iota(jnp.int32, x.shape, 1) < SWA_HEAD_DIM
    xr = pltpu.roll(x, SWA_HEAD_DIM, axis=1)
    return jnp.where(lo, x, xr), jnp.where(lo, xr, x)


def _pairs_attention(qps, sinks, k_dups, vt_dups, masks):
    nq = qps[0].shape[0]
    lo = lax.broadcasted_iota(jnp.int32, (nq, LANES), 1) < SWA_HEAD_DIM
    even = lax.broadcasted_iota(jnp.int32, (1, 2 * nq), 1) < nq
    scores = []
    for qp, k_dup in zip(qps, k_dups):
        q2 = jnp.concatenate([jnp.where(lo, qp, 0.0), jnp.where(lo, 0.0, qp)], axis=0).astype(BF16)
        scores.append(_dot_nt(k_dup, q2))
    probs = []
    for s, (sink_even, sink_odd), mask in zip(scores, sinks, masks):
        if mask is not None:
            s = jnp.where(mask, s, NEG_INF)
        sink = jnp.where(even, sink_even, sink_odd)
        m = jnp.maximum(jnp.max(s, axis=0, keepdims=True), sink)
        p = jnp.exp(s - m)
        denom = jnp.sum(p, axis=0, keepdims=True) + jnp.exp(sink - m)
        probs.append((p.astype(BF16), 1.0 / denom))
    outs = []
    for (p, rdenom), vt_dup in zip(probs, vt_dups):
        o = _dot(vt_dup, p) * rdenom
        outs.append(jnp.concatenate([o[:SWA_HEAD_DIM, :nq], o[SWA_HEAD_DIM:, nq:]], axis=0).T)
    return outs


CTX_BATCH = 4


def _attn_ctx_kernel(sink_ref, q_ref, k_ref, v_ref, o_ref):
    scale = SWA_HEAD_DIM ** -0.5
    pairs = range(SWA_HEADS // 2)
    items = [(bb, pr) for bb in range(q_ref.shape[0]) for pr in pairs]
    kd = [[x.astype(BF16) for x in _dup_groups(k_ref[bb])] for bb in range(q_ref.shape[0])]
    vt = [[x.T.astype(BF16) for x in _dup_groups(v_ref[bb])] for bb in range(q_ref.shape[0])]
    outs = _pairs_attention([q_ref[bb, :, pr * LANES:(pr + 1) * LANES] * scale for bb, pr in items],
                            [(sink_ref[2 * pr], sink_ref[2 * pr + 1]) for _, pr in items],
                            [kd[bb][pr // 2] for bb, pr in items], [vt[bb][pr // 2] for bb, pr in items],
                            [None] * len(items))
    for (bb, pr), out in zip(items, outs):
        o_ref[bb, :, pr * LANES:(pr + 1) * LANES] = out


def _attn_ctx(sink, q, k, v):
    b, t, _ = q.shape
    bmap = lambda i: (i, 0, 0)
    return pl.pallas_call(
        _attn_ctx_kernel,
        out_shape=jax.ShapeDtypeStruct((b, t, SWA_Q), F32),
        grid=(b // CTX_BATCH,),
        in_specs=[pl.BlockSpec(memory_space=pltpu.SMEM),
                  pl.BlockSpec((CTX_BATCH, t, SWA_Q), bmap),
                  pl.BlockSpec((CTX_BATCH, t, SWA_KV), bmap),
                  pl.BlockSpec((CTX_BATCH, t, SWA_KV), bmap)],
        out_specs=pl.BlockSpec((CTX_BATCH, t, SWA_Q), bmap),
        compiler_params=pltpu.CompilerParams(dimension_semantics=("arbitrary",),
                                             vmem_limit_bytes=VMEM_LIMIT),
        name="attn_ctx",
    )(sink, q, k, v)


LAT_BLOCKS = 2


def _rope(x, cos, sin_lo, sin_hi):
    return x * cos + pltpu.roll(x, LANES - 16, axis=1) * sin_lo + pltpu.roll(x, 16, axis=1) * sin_hi


def _attn_lat_kernel(sink_ref, q_ref, k_ref, v_ref, kc_ref, vc_ref, cos_ref, sl_ref, sh_ref,
                     o_ref, kw_ref, vw_ref):
    t = q_ref.shape[0]
    ab = ATTN_BLOCK
    nb = t // ab
    scale = SWA_HEAD_DIM ** -0.5

    k_rot = _dup_groups(_rope(k_ref[...], cos_ref[...], sl_ref[...], sh_ref[...]))
    v_dup = _dup_groups(v_ref[...])
    zeros = jnp.zeros((ab, LANES), BF16)
    for grp in range(SWA_KV_HEADS):
        kw_ref[grp, 0:ab, :] = zeros
        kw_ref[grp, ab:ab + t, :] = k_rot[grp].astype(BF16)
        kw_ref[grp, ab + t:, :] = zeros
        vw_ref[grp, 0] = zeros
        for blk in range(nb):
            vw_ref[grp, blk + 1] = v_dup[grp][blk * ab:(blk + 1) * ab, :].T.astype(BF16)
        vw_ref[grp, nb + 1] = zeros
    kc = [x.astype(BF16) for x in _dup_groups(kc_ref[...])]
    vct = [x.T.astype(BF16) for x in _dup_groups(vc_ref[...])]
    lc = kc_ref.shape[0]

    key = lax.broadcasted_iota(jnp.int32, (lc + 3 * ab, 2 * ab), 0) - lc
    tq = lax.broadcasted_iota(jnp.int32, (lc + 3 * ab, 2 * ab), 1) & (ab - 1)
    band = (key < 0) | (jnp.abs(tq + ab - key) <= ab)

    def block(it, carry):
        pairs = range(SWA_HEADS // 2)
        qps, sinks, k_dups, vt_dups, masks, places = [], [], [], [], [], []
        for u in range(LAT_BLOCKS):
            nq = it * LAT_BLOCKS + u
            row0 = pl.multiple_of(nq * ab, ab)
            s_abs = key + (nq - 1) * ab
            mask = band & ((key < 0) | ((s_abs >= 0) & (s_abs < t)))
            cos = cos_ref[pl.ds(row0, ab), :]
            s_lo = sl_ref[pl.ds(row0, ab), :]
            s_hi = sh_ref[pl.ds(row0, ab), :]
            k_all = [jnp.concatenate([kc[grp], kw_ref[grp, pl.ds(row0, 3 * ab), :]], axis=0)
                     for grp in range(SWA_KV_HEADS)]
            vt_all = [jnp.concatenate([vct[grp], vw_ref[grp, nq], vw_ref[grp, nq + 1], vw_ref[grp, nq + 2]],
                                      axis=1) for grp in range(SWA_KV_HEADS)]
            for pr in pairs:
                qps.append(_rope(q_ref[pl.ds(row0, ab), pr * LANES:(pr + 1) * LANES], cos, s_lo, s_hi) * scale)
                sinks.append((sink_ref[2 * pr], sink_ref[2 * pr + 1]))
                k_dups.append(k_all[pr // 2])
                vt_dups.append(vt_all[pr // 2])
                masks.append(mask)
                places.append((row0, pr))
        outs = _pairs_attention(qps, sinks, k_dups, vt_dups, masks)
        for (row0, pr), out in zip(places, outs):
            o_ref[pl.ds(row0, ab), pr * LANES:(pr + 1) * LANES] = out
        return carry

    lax.fori_loop(0, nb // LAT_BLOCKS, block, 0)


def _attn_lat(sink, q, k, v, kc, vc, cos, sin_lo, sin_hi):
    b, t, _ = q.shape
    lc = kc.shape[1]
    bmap = lambda i: (i, 0, 0)
    full = lambda i: (0, 0)
    return pl.pallas_call(
        _attn_lat_kernel,
        out_shape=jax.ShapeDtypeStruct((b, t, SWA_Q), F32),
        grid=(b,),
        in_specs=[pl.BlockSpec(memory_space=pltpu.SMEM),
                  pl.BlockSpec((None, t, SWA_Q), bmap),
                  pl.BlockSpec((None, t, SWA_KV), bmap),
                  pl.BlockSpec((None, t, SWA_KV), bmap),
                  pl.BlockSpec((None, lc, SWA_KV), bmap),
                  pl.BlockSpec((None, lc, SWA_KV), bmap),
                  pl.BlockSpec((t, LANES), full),
                  pl.BlockSpec((t, LANES), full),
                  pl.BlockSpec((t, LANES), full)],
        out_specs=pl.BlockSpec((None, t, SWA_Q), bmap),
        scratch_shapes=[pltpu.VMEM((SWA_KV_HEADS, t + 2 * ATTN_BLOCK, LANES), BF16),
                        pltpu.VMEM((SWA_KV_HEADS, t // ATTN_BLOCK + 2, LANES, ATTN_BLOCK), BF16)],
        compiler_params=pltpu.CompilerParams(dimension_semantics=("arbitrary",),
                                             vmem_limit_bytes=VMEM_LIMIT),
        name="attn_lat",
    )(sink, q, k, v, kc, vc, cos, sin_lo, sin_hi)


def _rope_tables(t):
    half = SWA_HEAD_DIM // 2
    quarter = half // 2
    pos = jnp.arange(t)
    row = (pos // GRID_W).astype(F32)
    col = (pos % GRID_W).astype(F32)
    inv_freq = ROPE_BASE ** (-jnp.arange(quarter, dtype=F32) / quarter)
    lane = jnp.arange(LANES)
    d = lane % SWA_HEAD_DIM
    freq = inv_freq[d % quarter]
    use_row = (d < half)
    ang = jnp.where(use_row[None, :], row[:, None], col[:, None]) * freq[None, :]
    cos = jnp.cos(ang)
    sin = jnp.sin(ang)
    lower = (d % half) < quarter
    return cos, jnp.where(lower[None, :], -sin, 0.0), jnp.where(lower[None, :], 0.0, sin)


def _route(sel, scores):
    n = sel.shape[1]
    gsz = N_EXPERTS // N_EXPERT_GROUPS

    def first_max(x, idx, size):
        m = jnp.max(x, axis=0, keepdims=True)
        first = jnp.min(jnp.where(x == m, idx, float(size)), axis=0, keepdims=True)
        return m, idx == first

    i8 = lax.broadcasted_iota(jnp.int32, (gsz, n), 0).astype(F32)
    rows = []
    for g in range(N_EXPERT_GROUPS):
        slab = sel[g * gsz:(g + 1) * gsz, :]
        m1, hit = first_max(slab, i8, gsz)
        m2 = jnp.max(jnp.where(hit, NEG_INF, slab), axis=0, keepdims=True)
        rows.append(m1 + m2)
    gscore = jnp.concatenate(rows, axis=0)
    gsel = jnp.zeros((N_EXPERT_GROUPS, n), F32)
    for _ in range(TOPK_GROUPS):
        _, hit = first_max(gscore, i8, N_EXPERT_GROUPS)
        gsel = jnp.where(hit, 1.0, gsel)
        gscore = jnp.where(hit, NEG_INF, gscore)
    emask = jnp.concatenate(
        [jnp.broadcast_to(gsel[g:g + 1, :], (gsz, n)) for g in range(N_EXPERT_GROUPS)], axis=0)
    cand = jnp.where(emask > 0.5, sel, NEG_INF)
    ie = lax.broadcasted_iota(jnp.int32, (N_EXPERTS, n), 0).astype(F32)
    w = jnp.zeros((N_EXPERTS, n), F32)
    chosen = jnp.zeros((N_EXPERTS, n), F32)
    hits = []
    for _ in range(TOP_K):
        _, hit = first_max(cand, ie, N_EXPERTS)
        hits.append(hit)
        w = jnp.where(hit, scores, w)
        chosen = jnp.where(hit, 1.0, chosen)
        cand = jnp.where(hit, NEG_INF, cand)
    gates = w / jnp.sum(w, axis=0, keepdims=True) * ROUTED_SCALE

    s_idx = lax.broadcasted_iota(jnp.int32, (n, n), 0)
    t_idx = lax.broadcasted_iota(jnp.int32, (n, n), 1)
    tile_shift = MOE_TILE.bit_length() - 1
    before = jnp.where((s_idx < t_idx) & ((s_idx >> tile_shift) == (t_idx >> tile_shift)), 1.0, 0.0)
    rank = _dot(chosen.astype(BF16), before.astype(BF16))
    e_row = lax.broadcasted_iota(jnp.int32, (N_EXPERTS, N_EXPERTS), 0)
    e_col = lax.broadcasted_iota(jnp.int32, (N_EXPERTS, N_EXPERTS), 1)
    below = jnp.where(e_col < e_row, 1.0, 0.0).astype(BF16)
    lane_tile = lax.broadcasted_iota(jnp.int32, (1, n), 1) >> tile_shift
    sizes, starts = [], []
    first_row = jnp.zeros((N_EXPERTS, n), F32)
    for ti in range(n // MOE_TILE):
        count = jnp.sum(chosen[:, ti * MOE_TILE:(ti + 1) * MOE_TILE], axis=1, keepdims=True)
        padded = jnp.floor((count + (SORT_ALIGN - 1)) * (1.0 / SORT_ALIGN)) * SORT_ALIGN
        padded = jnp.broadcast_to(padded, (N_EXPERTS, LANES))
        start = _dot(below, padded.astype(BF16))
        first_row = jnp.where(lane_tile == ti, start[:, 0:1], first_row)
        sizes.append(padded)
        starts.append(start)
    row = first_row + rank
    pos = jnp.concatenate([jnp.sum(jnp.where(h, row, 0.0), axis=0, keepdims=True) for h in hits], axis=0)
    wts = jnp.concatenate([jnp.sum(jnp.where(h, gates, 0.0), axis=0, keepdims=True) for h in hits], axis=0)
    return pos, wts, sizes, starts


def _outproj_kernel(gla_ref, att_ref, x_ref, wo_ref, g1_ref, sh_ref, sc_ref, ng_ref, rw_ref, rwh_ref,
                    rb_ref, x1_ref, xm_ref, pos_ref, wts_ref, cnt_ref, start_ref):
    bb, tb, d = x_ref.shape
    tm = bb * tb
    y = (_dot(gla_ref[...].reshape(tm, GLA_V).astype(BF16), wo_ref[0:GLA_V, :])
         + _dot(att_ref[...].reshape(tm, SWA_Q).astype(BF16), wo_ref[GLA_V:, :]))
    x1 = x_ref[...].reshape(tm, d) + g1_ref[...] * y
    x1_ref[...] = x1.reshape(bb, tb, d)
    xm = _rms_norm(x1, ng_ref[...]) * (1.0 + sc_ref[...]) + sh_ref[...]
    xm_hi, xm_lo = _split_hi_lo(xm)
    xm_ref[...] = xm_hi.reshape(bb, tb, d)
    lg = _dot(xm_hi, rw_ref[...])
    logits = lg[:, :N_EXPERTS] + lg[:, N_EXPERTS:] + _dot(xm_lo, rwh_ref[...])
    lt = jnp.concatenate([logits, jnp.zeros((tm, LANES - N_EXPERTS), F32)], axis=1).T[:N_EXPERTS, :]
    scores = _sigmoid(lt)
    pos, wts, sizes, starts = _route(scores + rb_ref[...], scores)
    tiles_per_batch = tb // MOE_TILE
    for ti in range(tm // MOE_TILE):
        at = (ti // tiles_per_batch, ti % tiles_per_batch)
        pos_ref[at] = pos[:, ti * MOE_TILE:(ti + 1) * MOE_TILE]
        wts_ref[at] = wts[:, ti * MOE_TILE:(ti + 1) * MOE_TILE]
        cnt_ref[at] = sizes[ti]
        start_ref[at] = starts[ti]


OUTPROJ_TILE = 512


def _outproj(gla_out, att_out, x, w_out, g1, sh2, sc2, norm_g, rw_cat, rw_hi, rbias):
    b, t, d = x.shape
    nmod = g1.shape[0]
    tb = min(t, OUTPROJ_TILE)
    bb = OUTPROJ_TILE // tb if nmod == 1 else 1
    tpb = tb // MOE_TILE
    mod_map = (lambda i, j: (i, 0, 0)) if nmod > 1 else (lambda i, j: (0, 0, 0))
    row = lambda i, j: (i, j, 0)
    full = lambda i, j: (0, 0)
    tile = lambda i, j: (i, j, 0, 0)
    nt = t // MOE_TILE
    return pl.pallas_call(
        _outproj_kernel,
        out_shape=(jax.ShapeDtypeStruct((b, t, d), F32),
                   jax.ShapeDtypeStruct((b, t, d), BF16),
                   jax.ShapeDtypeStruct((b, nt, TOP_K, MOE_TILE), F32),
                   jax.ShapeDtypeStruct((b, nt, TOP_K, MOE_TILE), F32),
                   jax.ShapeDtypeStruct((b, nt, N_EXPERTS, LANES), F32),
                   jax.ShapeDtypeStruct((b, nt, N_EXPERTS, LANES), F32)),
        grid=(b // bb, t // tb),
        in_specs=[pl.BlockSpec((bb, tb, GLA_V), row),
                  pl.BlockSpec((bb, tb, SWA_Q), row),
                  pl.BlockSpec((bb, tb, d), row),
                  pl.BlockSpec((d, d), full),
                  pl.BlockSpec((None, 1, d), mod_map),
                  pl.BlockSpec((None, 1, d), mod_map),
                  pl.BlockSpec((None, 1, d), mod_map),
                  pl.BlockSpec((1, d), full),
                  pl.BlockSpec((d, 2 * N_EXPERTS), full),
                  pl.BlockSpec((d, N_EXPERTS), full),
                  pl.BlockSpec((N_EXPERTS, 1), full)],
        out_specs=(pl.BlockSpec((bb, tb, d), row),
                   pl.BlockSpec((bb, tb, d), row),
                   pl.BlockSpec((bb, tpb, TOP_K, MOE_TILE), tile),
                   pl.BlockSpec((bb, tpb, TOP_K, MOE_TILE), tile),
                   pl.BlockSpec((bb, tpb, N_EXPERTS, LANES), tile),
                   pl.BlockSpec((bb, tpb, N_EXPERTS, LANES), tile)),
        compiler_params=pltpu.CompilerParams(dimension_semantics=("arbitrary", "arbitrary"),
                                             vmem_limit_bytes=VMEM_LIMIT),
        name="outproj",
    )(gla_out, att_out, x, w_out, g1, sh2, sc2, norm_g, rw_cat, rw_hi, rbias)


MOE_TILE = 256
SORT_ALIGN = 16
SORT_ROWS = 3072
ROW_TILE = 512
GATHER_SLOTS = 9
FFN_CHAINS = 4
COMBINE_CHUNK = 1024
ALWAYS_ROWS = 2560
COMBINE_TAIL = 512
COMBINE_SLOTS = 4


def _moe_sort_kernel(tiles_a, used_ref, xa_ref, xb_ref, pos_ref, xs_ref):
    i = pl.program_id(0)
    x = jnp.where(i < tiles_a, xa_ref[...], xb_ref[...])
    pos = pos_ref[...]
    tm = x.shape[0]
    used = used_ref[i]

    rows = lax.broadcasted_iota(jnp.int32, (tm, tm), 0).astype(F32).astype(BF16)
    one = jnp.ones((tm, tm), BF16)

    def fill(blk):
        local = (pos - float(blk * tm)).astype(BF16)
        onehot = jnp.zeros((tm, tm), BF16)
        for k in range(TOP_K):
            onehot = jnp.where(rows == local[k:k + 1, :], one, onehot)
        xs_ref[blk * tm:(blk + 1) * tm, :] = _dot(onehot, x).astype(BF16)

    for blk in range(SORT_ROWS // tm):
        if (blk + 1) * tm <= ALWAYS_ROWS:
            fill(blk)
        else:
            pl.when(blk * tm < used)(functools.partial(fill, blk))

            @pl.when(blk * tm >= used)
            def _():
                xs_ref[blk * tm:(blk + 1) * tm, :] = jnp.zeros((tm, D_MODEL), BF16)


def _moe_sort(xm_a, xm_b, pos, used):
    d = xm_a.shape[1]
    nt, _, tm = pos.shape
    tiles_a = xm_a.shape[0] // tm
    grid_spec = pltpu.PrefetchScalarGridSpec(
        num_scalar_prefetch=1,
        grid=(nt,),
        in_specs=[pl.BlockSpec((tm, d), lambda i, u: (jnp.minimum(i, tiles_a - 1), 0)),
                  pl.BlockSpec((tm, d), lambda i, u: (jnp.maximum(i - tiles_a, 0), 0)),
                  pl.BlockSpec((None, TOP_K, tm), lambda i, u: (i, 0, 0))],
        out_specs=pl.BlockSpec((SORT_ROWS, d), lambda i, u: (i, 0)))
    return pl.pallas_call(
        functools.partial(_moe_sort_kernel, tiles_a),
        out_shape=jax.ShapeDtypeStruct((nt * SORT_ROWS, d), BF16),
        grid_spec=grid_spec,
        compiler_params=pltpu.CompilerParams(dimension_semantics=("arbitrary",),
                                             vmem_limit_bytes=VMEM_LIMIT),
        name="moe_sort",
    )(used, xm_a, xm_b, pos)


def _moe_row_tiles(n_tokens):
    rows = n_tokens * TOP_K + (n_tokens // MOE_TILE) * N_EXPERTS * (SORT_ALIGN - 1) + N_EXPERTS * (ROW_TILE - 1)
    return -(-rows // ROW_TILE) + GATHER_SLOTS - 1


PLAN_CHUNK = 1280


def _int_dot_r(a, onehot):
    hi = jnp.floor(a * (1.0 / 256.0))
    return _dot(hi.astype(BF16), onehot) * 256.0 + _dot((a - hi * 256.0).astype(BF16), onehot)


def _int_dot_l(onehot, b):
    hi = jnp.floor(b * (1.0 / 256.0))
    return _dot(onehot, hi.astype(BF16)) * 256.0 + _dot(onehot, (b - hi * 256.0).astype(BF16))


def _moe_plan_kernel(cnt_ref, start_ref, src_ref, first_ref, tiles_ref, nu_ref, back_ref):
    nt, ne = cnt_ref.shape
    gpt = SORT_ROWS // SORT_ALIGN
    gpr = ROW_TILE // SORT_ALIGN
    gc = cnt_ref[...] * (1.0 / SORT_ALIGN)
    ls = start_ref[...] * (1.0 / SORT_ALIGN)

    def transpose(x):
        x = jnp.concatenate([x, jnp.zeros((nt, LANES - ne), F32)], axis=1)
        x = jnp.concatenate([x, jnp.zeros((LANES - nt, LANES), F32)], axis=0)
        return x.T[:ne, :nt]

    def tri(n, keep):
        return jnp.where(keep(lax.broadcasted_iota(jnp.int32, (n, n), 0),
                              lax.broadcasted_iota(jnp.int32, (n, n), 1)), 1.0, 0.0).astype(BF16)

    gc_t = transpose(gc)
    ls_t = transpose(ls)
    tot_c = jnp.broadcast_to(jnp.sum(gc_t, axis=1, keepdims=True), (ne, LANES))
    ptot_c = jnp.floor((tot_c + (gpr - 1)) * (1.0 / gpr)) * gpr
    gend_c = _int_dot_l(tri(ne, lambda r, c: c <= r), ptot_c)
    gstart_c = gend_c - ptot_c
    n_used = gend_c[ne - 1:ne, :] * (1.0 / gpr)
    nu_ref[...] = n_used.astype(jnp.int32)
    tot_r = jnp.sum(gc, axis=0, keepdims=True)
    ptot_r = jnp.floor((tot_r + (gpr - 1)) * (1.0 / gpr)) * gpr
    gstart_r = _int_dot_r(jnp.broadcast_to(ptot_r, (8, ne)), tri(ne, lambda r, c: r < c))
    cumex = _dot(tri(nt, lambda r, c: c < r), gc.astype(BF16))
    cumex_t = _dot(gc_t.astype(BF16), tri(nt, lambda r, c: r < c))
    tile_base = lax.broadcasted_iota(jnp.int32, (nt, ne), 0).astype(F32) * gpt + ls
    table = jnp.concatenate([cumex + gc, cumex, tile_base, gstart_r, jnp.broadcast_to(tot_r, (8, ne))], axis=0)

    e_iota = lax.broadcasted_iota(jnp.int32, (ne, PLAN_CHUNK), 0).astype(F32)
    for ch in range(src_ref.shape[1] // PLAN_CHUNK):
        g = (lax.broadcasted_iota(jnp.int32, (1, PLAN_CHUNK), 1) + ch * PLAN_CHUNK).astype(F32)
        eg = jnp.sum(jnp.where(gend_c[:, 0:1] <= g, 1.0, 0.0), axis=0, keepdims=True)
        picked = _int_dot_r(table, jnp.where(e_iota == eg, 1.0, 0.0).astype(BF16))
        cum_g, cumex_g, base_g = picked[0:nt], picked[nt:2 * nt], picked[2 * nt:3 * nt]
        u = g - picked[3 * nt:3 * nt + 1]
        in_tile = (cumex_g <= u) & (u < cum_g)
        src = jnp.sum(jnp.where(in_tile, base_g - cumex_g, 0.0), axis=0, keepdims=True) + u
        src = jnp.where(u < picked[3 * nt + 8:3 * nt + 9], src, gpt - 1.0)
        src_ref[:, ch * PLAN_CHUNK:(ch + 1) * PLAN_CHUNK] = src.astype(jnp.int32)

    first_ref[...] = (gstart_c * (1.0 / gpr)).astype(jnp.int32)
    tiles_ref[...] = (ptot_c * (1.0 / gpr)).astype(jnp.int32)

    lg = lax.broadcasted_iota(jnp.int32, (ne, back_ref.shape[1]), 1).astype(F32)
    for t in range(nt):
        first = ls_t[:, t:t + 1]
        inside = (first <= lg) & (lg < first + gc_t[:, t:t + 1])
        shift = gstart_c[:, 0:1] + cumex_t[:, t:t + 1] - first
        val = jnp.sum(jnp.where(inside, shift + lg, 0.0), axis=0, keepdims=True)
        back_ref[t:t + 1, :] = val.astype(jnp.int32)


def _moe_plan(cnt, start):
    nt, ne = cnt.shape
    row_tiles = _moe_row_tiles(nt * MOE_TILE)
    gpt = SORT_ROWS // SORT_ALIGN
    gpr = ROW_TILE // SORT_ALIGN
    n_src = -(-(row_tiles * gpr) // PLAN_CHUNK) * PLAN_CHUNK
    n_back = -(-gpt // LANES) * LANES
    src, first, tiles, nu, back = pl.pallas_call(
        _moe_plan_kernel,
        out_shape=(jax.ShapeDtypeStruct((1, n_src), jnp.int32),
                   jax.ShapeDtypeStruct((ne, LANES), jnp.int32),
                   jax.ShapeDtypeStruct((ne, LANES), jnp.int32),
                   jax.ShapeDtypeStruct((1, LANES), jnp.int32),
                   jax.ShapeDtypeStruct((nt, n_back), jnp.int32)),
        compiler_params=pltpu.CompilerParams(vmem_limit_bytes=VMEM_LIMIT),
        name="moe_plan",
    )(cnt, start)
    return nu[0, :1], first[:, 0], tiles[:, 0], src[0, :row_tiles * gpr], back[:, :gpt]


def _moe_experts_kernel(nu_ref, first_ref, tiles_ref, src_ref, xs_hbm, wg_ref, wu_ref, wd_ref, ys_hbm,
                        xbuf, ybuf, gsem, osem, wgu_s, wd_s):
    e = pl.program_id(0)
    n_used = nu_ref[0]
    gpr = ROW_TILE // SORT_ALIGN
    part = ROW_TILE // FFN_CHAINS

    def gather(tile, to_slot, j0=0, j1=gpr):
        for j in range(j0, j1):
            row = pl.multiple_of(src_ref[tile * gpr + j] * SORT_ALIGN, SORT_ALIGN)
            pltpu.make_async_copy(xs_hbm.at[pl.ds(row, SORT_ALIGN), :],
                                  xbuf.at[to_slot, j * SORT_ALIGN:(j + 1) * SORT_ALIGN, :],
                                  gsem.at[to_slot]).start(priority=j % 2)

    def drain(of_slot):
        for j in range(gpr):
            pltpu.make_async_copy(xs_hbm.at[0:SORT_ALIGN, :],
                                  xbuf.at[of_slot, j * SORT_ALIGN:(j + 1) * SORT_ALIGN, :], gsem.at[of_slot]).wait()

    def out_copy(tile, of_slot):
        row = pl.multiple_of(tile * ROW_TILE, ROW_TILE)
        return pltpu.make_async_copy(ybuf.at[of_slot], ys_hbm.at[pl.ds(row, ROW_TILE), :], osem.at[of_slot])

    @pl.when(e == 0)
    def _():
        for ahead in range(GATHER_SLOTS - 1):
            gather(ahead, ahead)

    wgu_s[:, :EXPERT_FF] = wg_ref[...].astype(BF16)
    wgu_s[:, EXPERT_FF:] = wu_ref[...].astype(BF16)
    wd_s[...] = wd_ref[...].astype(BF16)

    def row_tile(i, carry):
        r = first_ref[e] + i
        slot = lax.rem(r, GATHER_SLOTS)
        oslot = lax.rem(r, 2)
        next_slot = lax.rem(r + GATHER_SLOTS - 1, GATHER_SLOTS)
        drain(slot)

        @pl.when(r >= 2)
        def _():
            out_copy(r - 2, oslot).wait()

        abs_ = []
        for c in range(FFN_CHAINS):
            abs_.append(_dot(xbuf[slot, c * part:(c + 1) * part, :], wgu_s[...]))
            gather(r + GATHER_SLOTS - 1, next_slot, c * gpr // FFN_CHAINS, (c + 1) * gpr // FFN_CHAINS)
        hs = [(_silu(ab[:, :EXPERT_FF]) * ab[:, EXPERT_FF:]).astype(BF16) for ab in abs_]
        ys = [_dot(h, wd_s[...]).astype(BF16) for h in hs]
        for c in range(FFN_CHAINS):
            ybuf[oslot, c * part:(c + 1) * part, :] = ys[c]
        out_copy(r, oslot).start()
        return carry

    lax.fori_loop(0, tiles_ref[e], row_tile, 0)

    @pl.when(e == pl.num_programs(0) - 1)
    def _():
        for ahead in range(GATHER_SLOTS - 1):
            drain(lax.rem(n_used + ahead, GATHER_SLOTS))
        out_copy(n_used - 1, lax.rem(n_used - 1, 2)).wait()

        @pl.when(n_used >= 2)
        def _():
            out_copy(n_used - 2, lax.rem(n_used, 2)).wait()


def _moe_experts(n_used, first, tiles, src, xs, wg, wu, wd, row_tiles):
    d = xs.shape[-1]
    ne = wg.shape[0]
    w_map = lambda e, nu, fi, ti, sr: (e, 0, 0)
    grid_spec = pltpu.PrefetchScalarGridSpec(
        num_scalar_prefetch=4,
        grid=(ne,),
        in_specs=[pl.BlockSpec(memory_space=pl.ANY),
                  pl.BlockSpec((None, d, EXPERT_FF), w_map),
                  pl.BlockSpec((None, d, EXPERT_FF), w_map),
                  pl.BlockSpec((None, EXPERT_FF, d), w_map)],
        out_specs=pl.BlockSpec(memory_space=pl.ANY),
        scratch_shapes=[pltpu.VMEM((GATHER_SLOTS, ROW_TILE, d), BF16),
                        pltpu.VMEM((2, ROW_TILE, d), BF16),
                        pltpu.SemaphoreType.DMA((GATHER_SLOTS,)),
                        pltpu.SemaphoreType.DMA((2,)),
                        pltpu.VMEM((d, 2 * EXPERT_FF), BF16),
                        pltpu.VMEM((EXPERT_FF, d), BF16)])
    return pl.pallas_call(
        _moe_experts_kernel,
        out_shape=jax.ShapeDtypeStruct((row_tiles * ROW_TILE, d), BF16),
        grid_spec=grid_spec,
        compiler_params=pltpu.CompilerParams(dimension_semantics=("arbitrary",),
                                             vmem_limit_bytes=VMEM_LIMIT),
        name="moe_experts",
    )(n_used, first, tiles, src, xs, wg, wu, wd)


def _moe_combine_kernel(back_ref, used_ref, ys_hbm, pos_ref, wts_ref, xm_ref, x1_ref, g2_ref, fg_ref,
                        swg_ref, swu_ref, swd_ref, o_ref, buf, sem, acc_ref):
    i = pl.program_id(0)
    gpt = SORT_ROWS // SORT_ALIGN
    slot = lax.rem(i, COMBINE_SLOTS)
    ahead = COMBINE_SLOTS - 1
    always = ALWAYS_ROWS
    tail = range(always, SORT_ROWS, COMBINE_TAIL)

    def copies(tile, of_slot, g0, g1, start):
        for g in range(g0, g1):
            row = pl.multiple_of(back_ref[tile * gpt + g] * SORT_ALIGN, SORT_ALIGN) if start else 0
            cp = pltpu.make_async_copy(ys_hbm.at[pl.ds(row, SORT_ALIGN), :],
                                       buf.at[of_slot, g * SORT_ALIGN:(g + 1) * SORT_ALIGN, :], sem.at[of_slot])
            if start:
                cp.start(priority=g % 2)
            else:
                cp.wait()

    def transfer(tile, of_slot, start):
        copies(tile, of_slot, 0, always // SORT_ALIGN, start)
        for c0 in tail:
            pl.when(c0 < used_ref[tile])(functools.partial(
                copies, tile, of_slot, c0 // SORT_ALIGN, (c0 + COMBINE_TAIL) // SORT_ALIGN, start))

    @pl.when(i == 0)
    def _():
        for first in range(min(ahead, buf.shape[0])):
            pl.when(first < pl.num_programs(0))(functools.partial(transfer, first, first, True))

    @pl.when(i + ahead < pl.num_programs(0))
    def _():
        transfer(i + ahead, lax.rem(i + ahead, COMBINE_SLOTS), True)

    x = xm_ref[...]
    tm = x.shape[0]
    pad = jnp.zeros((LANES - TOP_K, tm), F32)
    pos_t = jnp.concatenate([pos_ref[...], pad], axis=0).T
    wts_t = jnp.concatenate([wts_ref[...], pad], axis=0).T
    blk_b, loc_b, wts_b = [], [], []
    for k in range(TOP_K):
        p = jnp.broadcast_to(pos_t[:, k:k + 1], (tm, LANES))
        blk = jnp.floor(p * (1.0 / tm))
        two = lambda v: jnp.concatenate([v.astype(BF16)] * (tm // LANES), axis=1)
        blk_b.append(two(blk))
        loc_b.append(two(p - blk * tm))
        wts_b.append(two(jnp.broadcast_to(wts_t[:, k:k + 1], (tm, LANES))))
    shared = _dot((_silu(_dot(x, swg_ref[...])) * _dot(x, swu_ref[...])).astype(BF16), swd_ref[...])
    transfer(i, slot, False)
    lane = lax.broadcasted_iota(jnp.int32, (tm, tm), 1).astype(F32).astype(BF16)
    zero = jnp.zeros((tm, tm), BF16)
    nowhere = jnp.full((tm, tm), -1.0, BF16)

    def apply(c0, width):
        blocks = []
        for b0 in range(c0, c0 + width, tm):
            comb = zero
            for k in range(TOP_K):
                loc = jnp.where(blk_b[k] == float(b0 // tm), loc_b[k], nowhere)
                comb = jnp.where(lane == loc, wts_b[k], comb)
            blocks.append(comb)
        return _dot(jnp.concatenate(blocks, axis=1), buf[slot, c0:c0 + width, :])

    routed = shared
    for c0 in range(0, always, COMBINE_CHUNK):
        routed = routed + apply(c0, min(COMBINE_CHUNK, always - c0))
    acc_ref[...] = routed
    for c0 in tail:
        @pl.when(c0 < used_ref[i])
        def _(c0=c0):
            acc_ref[...] += apply(c0, COMBINE_TAIL)
    y = x1_ref[...] + g2_ref[...] * acc_ref[...]
    o_ref[...] = _rms_norm(y, fg_ref[...])


def _moe_combine(back, used, ys, pos, wts, xm, x1, g2, final_g, swg, swu, swd, *, tiles_per_mod):
    n, d = xm.shape
    tm = pos.shape[-1]
    nt = n // tm
    gpt = SORT_ROWS // SORT_ALIGN
    row = lambda i, bk, us: (i, 0)
    full = lambda i, bk, us: (0, 0)
    tile = lambda i, bk, us: (i, 0, 0)
    mod_map = lambda i, bk, us: (i // tiles_per_mod, 0, 0)
    grid_spec = pltpu.PrefetchScalarGridSpec(
        num_scalar_prefetch=2,
        grid=(nt,),
        in_specs=[pl.BlockSpec(memory_space=pl.ANY),
                  pl.BlockSpec((None, TOP_K, tm), tile),
                  pl.BlockSpec((None, TOP_K, tm), tile),
                  pl.BlockSpec((tm, d), row),
                  pl.BlockSpec((tm, d), row),
                  pl.BlockSpec((None, 1, d), mod_map),
                  pl.BlockSpec((1, d), full),
                  pl.BlockSpec((d, SHARED_FF), full),
                  pl.BlockSpec((d, SHARED_FF), full),
                  pl.BlockSpec((SHARED_FF, d), full)],
        out_specs=pl.BlockSpec((tm, d), row),
        scratch_shapes=[pltpu.VMEM((COMBINE_SLOTS, SORT_ROWS, d), BF16),
                        pltpu.SemaphoreType.DMA((COMBINE_SLOTS,)),
                        pltpu.VMEM((tm, d), F32)])
    return pl.pallas_call(
        _moe_combine_kernel,
        out_shape=jax.ShapeDtypeStruct((n, d), F32),
        grid_spec=grid_spec,
        compiler_params=pltpu.CompilerParams(dimension_semantics=("arbitrary",),
                                             vmem_limit_bytes=VMEM_LIMIT),
        name="moe_combine",
    )(back, used, ys, pos.reshape(nt, TOP_K, tm), wts.reshape(nt, TOP_K, tm), xm, x1, g2, final_g, swg, swu, swd)


def _mix(x, mods, p, attn_fn, s0=None):
    sh1, sc1, g1, sh2, sc2, _ = mods
    gla_in, lora, q_s, k_s, v_s = _inproj(x, p["norm_attn_g"], sh1, sc1, p["w_gla"], p["w_lora"], p["w_swa"])
    if s0 is None:
        gla_out, s_f, s_b = _gla(gla_in, lora, p["waf"], p["baf"], p["wab"], p["bab"], p["gla_norm_g"])
    else:
        gla_out, s_f, s_b = _gla(gla_in, lora, p["waf"], p["baf"], p["wab"], p["bab"], p["gla_norm_g"],
                                 s0[0], s0[1])
    att_out = attn_fn(q_s, k_s, v_s)
    routed = _outproj(gla_out, att_out, x, p["w_out"], g1, sh2, sc2, p["norm_ffn_g"],
                      p["rw_cat"], p["rw_hi"], p["rbias"])
    return routed, k_s, v_s, s_f, s_b


def _moe(streams, p):
    d = D_MODEL
    (ra, _), (rb, _) = streams
    n_tiles = [r[1].shape[0] * r[1].shape[1] // MOE_TILE for r, _ in streams]
    pos_all = jnp.concatenate([r[2].reshape(-1, TOP_K, MOE_TILE) for r, _ in streams], axis=0)
    cnt_all = jnp.concatenate([r[4][..., 0].reshape(-1, N_EXPERTS) for r, _ in streams], axis=0)
    start_all = jnp.concatenate([r[5][..., 0].reshape(-1, N_EXPERTS) for r, _ in streams], axis=0)
    used = (start_all[:, -1] + cnt_all[:, -1]).astype(jnp.int32)
    xs = _moe_sort(ra[1].reshape(-1, d), rb[1].reshape(-1, d), pos_all, used)
    n_used, first, tiles, src, back = _moe_plan(cnt_all, start_all)
    ys = _moe_experts(n_used, first, tiles, src, xs, p["wg"], p["wu"], p["wd"],
                      _moe_row_tiles(cnt_all.shape[0] * MOE_TILE))
    outs = []
    tile0 = 0
    for ((x1, xm, pos, wts, cnt, start), g2), nt in zip(streams, n_tiles):
        b, t, _ = x1.shape
        tiles_per_mod = (t // MOE_TILE) if g2.shape[0] > 1 else nt
        y = _moe_combine(back[tile0:tile0 + nt].reshape(-1), used[tile0:tile0 + nt], ys, pos, wts,
                         xm.reshape(-1, d), x1.reshape(-1, d), g2, p["final_norm_g"],
                         p["swg"], p["swu"], p["swd"], tiles_per_mod=tiles_per_mod)
        outs.append(y.reshape(b, t, d))
        tile0 += nt
    return outs


def kernel(x_prompt, x_sample, c, cache_swa_k, cache_swa_v, state_gla_fwd, state_gla_bwd, c_ctx, w_ada, b_ada, norm_attn_g, norm_ffn_g, w_in, gla_wa_f, gla_ba_f, gla_wa_b, gla_ba_b, gla_norm_g, swa_sink, w_out, router_w, router_bias, exp_w_gate, exp_w_up, exp_w_down, sh_w_gate, sh_w_up, sh_w_down, final_norm_g):
    l = 0
    d = D_MODEL
    nb_ctx, t_ctx, _ = x_prompt.shape
    nb_lat, t_lat, _ = x_sample.shape

    pad = jnp.zeros((8 - 1 - nb_lat, d), F32)
    cond8 = jnp.concatenate([c_ctx[None, :], c, pad], axis=0)
    mod = _adaln(cond8, w_ada[l], b_ada[l][None, :])
    mods_ctx = [mod[0:1, i * d:(i + 1) * d][:, None, :] for i in range(6)]
    mods_lat = [mod[1:1 + nb_lat, i * d:(i + 1) * d][:, None, :] for i in range(6)]

    zeros_lora = jnp.zeros((GLA_LORA, GLA_QK), F32)
    rw = router_w[l]
    rw_hi = rw.astype(BF16)
    rw_lo = (rw - rw_hi.astype(F32)).astype(BF16)
    w_in_b = w_in[l].astype(BF16)
    p = {
        "norm_attn_g": norm_attn_g[l][None, :],
        "norm_ffn_g": norm_ffn_g[l][None, :],
        "final_norm_g": final_norm_g[None, :],
        "w_gla": w_in_b[:, :2 * GLA_QK + 2 * GLA_V],
        "w_lora": w_in_b[:, 2 * GLA_QK + 2 * GLA_V:2 * GLA_QK + 2 * GLA_V + 2 * GLA_LORA],
        "w_swa": w_in_b[:, 2 * GLA_QK + 2 * GLA_V + 2 * GLA_LORA:],
        "waf": jnp.concatenate([gla_wa_f[l], zeros_lora], axis=0).astype(BF16),
        "wab": jnp.concatenate([zeros_lora, gla_wa_b[l]], axis=0).astype(BF16),
        "baf": gla_ba_f[l][None, :],
        "bab": gla_ba_b[l][None, :],
        "gla_norm_g": gla_norm_g[l][None, :],
        "w_out": w_out[l].astype(BF16),
        "rw_cat": jnp.concatenate([rw_hi, rw_lo], axis=1),
        "rw_hi": rw_hi,
        "rbias": router_bias[l][:, None],
        "wg": exp_w_gate[l], "wu": exp_w_up[l], "wd": exp_w_down[l],
        "swg": sh_w_gate[l].astype(BF16), "swu": sh_w_up[l].astype(BF16),
        "swd": sh_w_down[l].astype(BF16),
    }
    sink = swa_sink[l]

    routed_ctx, k_c, v_c, s_f, s_b = _mix(x_prompt, mods_ctx, p, functools.partial(_attn_ctx, sink))

    cos, sin_lo, sin_hi = _rope_tables(t_lat)
    kc = cache_swa_k[:, l].reshape(nb_lat, -1, SWA_KV)
    vc = cache_swa_v[:, l].reshape(nb_lat, -1, SWA_KV)
    lat_attn = lambda q, k, v: _attn_lat(sink, q, k, v, kc, vc, cos, sin_lo, sin_hi)
    s0 = (state_gla_fwd[:, l].reshape(nb_lat, GLA_QK, GLA_DV),
          state_gla_bwd[:, l].reshape(nb_lat, GLA_QK, GLA_DV))
    routed_lat, _, _, _, _ = _mix(x_sample, mods_lat, p, lat_attn, s0)
    y_prompt, y_sample = _moe([(routed_ctx, mods_ctx[5]), (routed_lat, mods_lat[5])], p)

    new_k = k_c.reshape(nb_ctx, 1, t_ctx, SWA_KV_HEADS, SWA_HEAD_DIM)
    new_v = v_c.reshape(nb_ctx, 1, t_ctx, SWA_KV_HEADS, SWA_HEAD_DIM)
    new_sf = s_f.reshape(nb_ctx, 1, GLA_HEADS, GLA_DK, GLA_DV)
    new_sb = s_b.reshape(nb_ctx, 1, GLA_HEADS, GLA_DK, GLA_DV)
    return (y_prompt, y_sample, new_k, new_v, new_sf, new_sb)
```

```python
import functools

import jax
import jax.numpy as jnp
from jax import lax
from jax.experimental import pallas as pl
from jax.experimental.pallas import tpu as pltpu

F32 = jnp.float32
BF16 = jnp.bfloat16

D_MODEL = 1024
GLA_HEADS = 4
GLA_DK = 64
GLA_DV = 128
GLA_LORA = 16
GLA_GATE_NORM = 16.0
GLA_CHUNK = 64
GLA_QK = GLA_HEADS * GLA_DK
GLA_V = GLA_HEADS * GLA_DV
SWA_HEAD_DIM = 64
SWA_HEADS = 8
SWA_KV_HEADS = 2
SWA_Q = SWA_HEADS * SWA_HEAD_DIM
SWA_KV = SWA_KV_HEADS * SWA_HEAD_DIM
ATTN_BLOCK = 128
GRID_W = 64
ROPE_BASE = 10000.0
N_EXPERTS = 64
TOP_K = 8
N_EXPERT_GROUPS = 8
TOPK_GROUPS = 4
EXPERT_FF = 128
SHARED_FF = 256
ROUTED_SCALE = 2.5
EPS = 1e-6

LANES = 128
VMEM_LIMIT = 56 * 1024 * 1024

NEG_INF = float("-inf")


def _dot(a, b):
    return jnp.dot(a, b, preferred_element_type=F32)


def _dot_nt(a, b):
    return lax.dot_general(a, b, (((1,), (1,)), ((), ())), preferred_element_type=F32)


def _split_hi_lo(x):
    hi = x.astype(BF16)
    lo = (x - hi.astype(F32)).astype(BF16)
    return hi, lo


def _sigmoid(x):
    return 1.0 / (1.0 + jnp.exp(-x))


def _silu(x):
    return x * _sigmoid(x)


def _rms_norm(x, g):
    ms = jnp.mean(x * x, axis=-1, keepdims=True)
    return x * lax.rsqrt(ms + EPS) * g


def _adaln_kernel(c_ref, w_ref, b_ref, o_ref):
    a_hi, a_lo = _split_hi_lo(_silu(c_ref[...]))
    w_hi, w_lo = _split_hi_lo(w_ref[...])
    o_ref[...] = _dot(a_hi, w_hi) + _dot(a_lo, w_hi) + _dot(a_hi, w_lo) + b_ref[...]


def _adaln(cond8, w_ada, b_ada):
    n = w_ada.shape[1]
    tn = 1536
    return pl.pallas_call(
        _adaln_kernel,
        out_shape=jax.ShapeDtypeStruct((8, n), F32),
        grid=(n // tn,),
        in_specs=[pl.BlockSpec((8, D_MODEL), lambda j: (0, 0)),
                  pl.BlockSpec((D_MODEL, tn), lambda j: (0, j)),
                  pl.BlockSpec((1, tn), lambda j: (0, j))],
        out_specs=pl.BlockSpec((8, tn), lambda j: (0, j)),
        compiler_params=pltpu.CompilerParams(dimension_semantics=("arbitrary",),
                                             vmem_limit_bytes=VMEM_LIMIT),
        name="adaln",
    )(cond8, w_ada, b_ada)


def _inproj_kernel(x_ref, g_ref, sh_ref, sc_ref, wg_ref, wl_ref, ws_ref,
                   gla_ref, lora_ref, q_ref, k_ref, v_ref):
    bb, tb, d = x_ref.shape
    x = x_ref[...].reshape(bb * tb, d)
    h = _rms_norm(x, g_ref[...]) * (1.0 + sc_ref[...]) + sh_ref[...]
    hb = h.astype(BF16)
    gla_ref[...] = _dot(hb, wg_ref[...]).reshape(gla_ref.shape)
    lora_ref[...] = _dot(hb, wl_ref[...]).reshape(lora_ref.shape)
    s = _dot(hb, ws_ref[...])
    q_ref[...] = s[:, :SWA_Q].reshape(q_ref.shape)
    k_ref[...] = s[:, SWA_Q:SWA_Q + SWA_KV].reshape(k_ref.shape)
    v_ref[...] = s[:, SWA_Q + SWA_KV:].reshape(v_ref.shape)


INPROJ_TILE = 512


def _inproj(x, g, sh, sc, w_gla, w_lora, w_swa):
    b, t, d = x.shape
    nmod = sh.shape[0]
    tb = min(t, INPROJ_TILE)
    bb = INPROJ_TILE // tb if nmod == 1 else 1
    mod_map = (lambda i, j: (i, 0, 0)) if nmod > 1 else (lambda i, j: (0, 0, 0))
    row = lambda i, j: (i, j, 0)
    full = lambda i, j: (0, 0)
    n_gla = w_gla.shape[1]
    n_lora = w_lora.shape[1]
    return pl.pallas_call(
        _inproj_kernel,
        out_shape=(jax.ShapeDtypeStruct((b, t, n_gla), F32),
                   jax.ShapeDtypeStruct((b, t, n_lora), F32),
                   jax.ShapeDtypeStruct((b, t, SWA_Q), F32),
                   jax.ShapeDtypeStruct((b, t, SWA_KV), F32),
                   jax.ShapeDtypeStruct((b, t, SWA_KV), F32)),
        grid=(b // bb, t // tb),
        in_specs=[pl.BlockSpec((bb, tb, d), row),
                  pl.BlockSpec((1, d), full),
                  pl.BlockSpec((None, 1, d), mod_map),
                  pl.BlockSpec((None, 1, d), mod_map),
                  pl.BlockSpec((d, n_gla), full),
                  pl.BlockSpec((d, n_lora), full),
                  pl.BlockSpec((d, w_swa.shape[1]), full)],
        out_specs=(pl.BlockSpec((bb, tb, n_gla), row),
                   pl.BlockSpec((bb, tb, n_lora), row),
                   pl.BlockSpec((bb, tb, SWA_Q), row),
                   pl.BlockSpec((bb, tb, SWA_KV), row),
                   pl.BlockSpec((bb, tb, SWA_KV), row)),
        compiler_params=pltpu.CompilerParams(dimension_semantics=("arbitrary", "arbitrary"),
                                             vmem_limit_bytes=VMEM_LIMIT),
        name="inproj",
    )(x, g, sh, sc, w_gla, w_lora, w_swa)


SCAN_UNROLL = 4
OUT_UNROLL = 4


def _log_sigmoid(x):
    return jnp.minimum(x, 0.0) - jnp.log(1.0 + jnp.exp(-jnp.abs(x)))


def _heads_to_rows(x):
    return jnp.concatenate([x[:, h * LANES:(h + 1) * LANES] for h in range(GLA_HEADS)], axis=0)


def _rows_to_heads(x, c):
    return jnp.concatenate([x[h * c:(h + 1) * c, :] for h in range(GLA_HEADS)], axis=1)


def _gla_kernel(has_init, q_ref, k_ref, v_ref, g_ref, lora_ref, waf_ref, baf_ref, wab_ref, bab_ref,
                ng_ref, *rest):
    if has_init:
        s0f_ref, s0b_ref, *rest = rest
    (out_ref, sf_ref, sb_ref, laf_ref, lab_ref, oacc_ref, qtf_ref, qtb_ref, saf_ref, sab_ref,
     stf_ref, stb_ref) = rest
    t = q_ref.shape[0]
    c = GLA_CHUNK
    n = t // c
    hc = GLA_HEADS * c

    lora = lora_ref[...].astype(BF16)
    laf_ref[...] = _log_sigmoid(_dot(lora, waf_ref[...]) + baf_ref[...]) * (1.0 / GLA_GATE_NORM)
    lab_ref[...] = _log_sigmoid(_dot(lora, wab_ref[...]) + bab_ref[...]) * (1.0 / GLA_GATE_NORM)

    if has_init:
        stf_ref[...] = s0f_ref[...].T
        stb_ref[...] = s0b_ref[...].T
    else:
        stf_ref[...] = jnp.zeros_like(stf_ref)
        stb_ref[...] = jnp.zeros_like(stb_ref)
    oacc_ref[...] = jnp.zeros_like(oacc_ref)

    r64 = lax.broadcasted_iota(jnp.int32, (c, c), 0)
    c64 = lax.broadcasted_iota(jnp.int32, (c, c), 1)
    tri_f = jnp.where(c64 <= r64, 1.0, 0.0).astype(BF16)
    tri_b = jnp.where(c64 >= r64, 1.0, 0.0).astype(BF16)
    rr = lax.broadcasted_iota(jnp.int32, (hc, hc), 0)
    cc = lax.broadcasted_iota(jnp.int32, (hc, hc), 1)
    same_head = (rr >> 6) == (cc >> 6)
    keep_f = same_head & ((rr & (c - 1)) >= (cc & (c - 1)))
    keep_b = same_head & ((rr & (c - 1)) <= (cc & (c - 1)))
    head_mask = jnp.where(same_head, 1.0, 0.0).astype(BF16)
    norm_g = ng_ref[...]

    def chunk_rows(ci):
        return pl.ds(pl.multiple_of(ci * c, c), c)

    def tile_heads(x):
        x4 = jnp.concatenate([x] * GLA_HEADS, axis=0)
        return jnp.where(same_head, x4, 0.0).astype(BF16)

    def scan_step(i, carry):
        dirs = []
        for u in range(SCAN_UNROLL):
            dirs += [(SCAN_UNROLL * i + u, laf_ref, tri_f, keep_f, c - 1, stf_ref, saf_ref, qtf_ref),
                     (n - 1 - SCAN_UNROLL * i - u, lab_ref, tri_b, keep_b, 0, stb_ref, sab_ref, qtb_ref)]
        cums = []
        for ci, la_ref, tri, _, _, _, _, _ in dirs:
            la_hi, la_lo = _split_hi_lo(la_ref[chunk_rows(ci), :])
            cums.append(_dot(tri, la_hi) + _dot(tri, la_lo))
        ops = []
        for (ci, _, _, _, last_row, _, _, qt_ref), cum in zip(dirs, cums):
            sl = chunk_rows(ci)
            tot = cum[last_row:last_row + 1, :]
            kc = k_ref[sl, :]
            qt = q_ref[sl, :] * (GLA_DK ** -0.5) * jnp.exp(cum)
            qt_ref[sl, :] = qt.astype(BF16)
            v_rows = _heads_to_rows(v_ref[sl, :])
            ops.append((tot, tile_heads(qt), tile_heads(kc * jnp.exp(-cum)),
                        tile_heads(kc * jnp.exp(tot - cum)), v_rows))
        atts = [_dot_nt(q4, k4) for _, q4, k4, _, _ in ops]
        incs = []
        for (_, _, _, keep, _, _, _, _), (_, _, _, kd4, v_rows), att in zip(dirs, ops, atts):
            att = jnp.where(keep, att, 0.0).astype(BF16)
            incs.append((_dot(att, v_rows.astype(BF16)), _dot(v_rows.T.astype(BF16), kd4)))
        for (ci, _, _, _, _, st_ref, snap_ref, _), (tot, _, _, _, _), (o_intra, st_inc) in zip(dirs, ops, incs):
            oacc_ref[ci] += o_intra
            st = st_ref[...]
            snap_ref[ci] = st.astype(BF16)
            st_ref[...] = jnp.exp(tot) * st + st_inc
        return carry

    def tile_heads_bf16(x):
        return jnp.concatenate([x] * GLA_HEADS, axis=0) * head_mask

    def out_step(i, carry):
        chunks = [OUT_UNROLL * i + u for u in range(OUT_UNROLL)]
        inter = []
        for ci in chunks:
            sl = chunk_rows(ci)
            q4 = jnp.concatenate([tile_heads_bf16(qtf_ref[sl, :]), tile_heads_bf16(qtb_ref[sl, :])], axis=1)
            st = jnp.concatenate([saf_ref[ci], sab_ref[ci]], axis=1)
            inter.append(_dot_nt(q4, st))
        for ci, o_inter in zip(chunks, inter):
            sl = chunk_rows(ci)
            on = _rms_norm(oacc_ref[ci] + o_inter, norm_g)
            gate = _silu(_heads_to_rows(g_ref[sl, :]))
            out_ref[sl, :] = _rows_to_heads(on * gate, c)
        return carry

    lax.fori_loop(0, n // SCAN_UNROLL, scan_step, 0)
    lax.fori_loop(0, n // OUT_UNROLL, out_step, 0)
    sf_ref[...] = stf_ref[...].T
    sb_ref[...] = stb_ref[...].T


def _gla(gla_in, lora, waf, baf, wab, bab, norm_g, s0f=None, s0b=None):
    b, t, _ = gla_in.shape
    has_init = s0f is not None
    n = t // GLA_CHUNK
    bmap = lambda i: (i, 0, 0)
    full = lambda i: (0, 0)
    in_specs = [pl.BlockSpec((None, t, GLA_QK), lambda i: (i, 0, 0)),
                pl.BlockSpec((None, t, GLA_QK), lambda i: (i, 0, 1)),
                pl.BlockSpec((None, t, GLA_V), lambda i: (i, 0, 1)),
                pl.BlockSpec((None, t, GLA_V), lambda i: (i, 0, 2)),
                pl.BlockSpec((None, t, 2 * GLA_LORA), bmap),
                pl.BlockSpec((2 * GLA_LORA, GLA_QK), full),
                pl.BlockSpec((1, GLA_QK), full),
                pl.BlockSpec((2 * GLA_LORA, GLA_QK), full),
                pl.BlockSpec((1, GLA_QK), full),
                pl.BlockSpec((1, GLA_DV), full)]
    args = [gla_in, gla_in, gla_in, gla_in, lora, waf, baf, wab, bab, norm_g]
    if has_init:
        in_specs += [pl.BlockSpec((None, GLA_QK, GLA_DV), bmap)] * 2
        args += [s0f, s0b]
    return pl.pallas_call(
        functools.partial(_gla_kernel, has_init),
        out_shape=(jax.ShapeDtypeStruct((b, t, GLA_V), F32),
                   jax.ShapeDtypeStruct((b, GLA_QK, GLA_DV), F32),
                   jax.ShapeDtypeStruct((b, GLA_QK, GLA_DV), F32)),
        grid=(b,),
        in_specs=in_specs,
        out_specs=(pl.BlockSpec((None, t, GLA_V), bmap),
                   pl.BlockSpec((None, GLA_QK, GLA_DV), bmap),
                   pl.BlockSpec((None, GLA_QK, GLA_DV), bmap)),
        scratch_shapes=[pltpu.VMEM((t, GLA_QK), F32),
                        pltpu.VMEM((t, GLA_QK), F32),
                        pltpu.VMEM((n, GLA_HEADS * GLA_CHUNK, GLA_DV), F32),
                        pltpu.VMEM((t, GLA_QK), BF16),
                        pltpu.VMEM((t, GLA_QK), BF16),
                        pltpu.VMEM((n, GLA_DV, GLA_QK), BF16),
                        pltpu.VMEM((n, GLA_DV, GLA_QK), BF16),
                        pltpu.VMEM((GLA_DV, GLA_QK), F32),
                        pltpu.VMEM((GLA_DV, GLA_QK), F32)],
        compiler_params=pltpu.CompilerParams(dimension_semantics=("arbitrary",),
                                             vmem_limit_bytes=VMEM_LIMIT),
        name="gla",
    )(*args)


def _dup_groups(x):
    lo = lax.broadcasted_iota(jnp.int32, x.shape, 1) < SWA_HEAD_DIM
    xr = pltpu.roll(x, SWA_HEAD_DIM, axis=1)
    return jnp.where(lo, x, xr), jnp.where(lo, xr, x)


def _pairs_attention(qps, sinks, k_dups, vt_dups, masks):
    nq = qps[0].shape[0]
    lo = lax.broadcasted_iota(jnp.int32, (nq, LANES), 1) < SWA_HEAD_DIM
    even = lax.broadcasted_iota(jnp.int32, (1, 2 * nq), 1) < nq
    scores = []
    for qp, k_dup in zip(qps, k_dups):
        q2 = jnp.concatenate([jnp.where(lo, qp, 0.0), jnp.where(lo, 0.0, qp)], axis=0).astype(BF16)
        scores.append(_dot_nt(k_dup, q2))
    probs = []
    for s, (sink_even, sink_odd), mask in zip(scores, sinks, masks):
        if mask is not None:
            s = jnp.where(mask, s, NEG_INF)
        sink = jnp.where(even, sink_even, sink_odd)
        m = jnp.maximum(jnp.max(s, axis=0, keepdims=True), sink)
        p = jnp.exp(s - m)
        denom = jnp.sum(p, axis=0, keepdims=True) + jnp.exp(sink - m)
        probs.append((p.astype(BF16), 1.0 / denom))
    outs = []
    for (p, rdenom), vt_dup in zip(probs, vt_dups):
        o = _dot(vt_dup, p) * rdenom
        outs.append(jnp.concatenate([o[:SWA_HEAD_DIM, :nq], o[SWA_HEAD_DIM:, nq:]], axis=0).T)
    return outs


CTX_BATCH = 4


def _attn_ctx_kernel(sink_ref, q_ref, k_ref, v_ref, o_ref):
    scale = SWA_HEAD_DIM ** -0.5
    pairs = range(SWA_HEADS // 2)
    items = [(bb, pr) for bb in range(q_ref.shape[0]) for pr in pairs]
    kd = [[x.astype(BF16) for x in _dup_groups(k_ref[bb])] for bb in range(q_ref.shape[0])]
    vt = [[x.T.astype(BF16) for x in _dup_groups(v_ref[bb])] for bb in range(q_ref.shape[0])]
    outs = _pairs_attention([q_ref[bb, :, pr * LANES:(pr + 1) * LANES] * scale for bb, pr in items],
                            [(sink_ref[2 * pr], sink_ref[2 * pr + 1]) for _, pr in items],
                            [kd[bb][pr // 2] for bb, pr in items], [vt[bb][pr // 2] for bb, pr in items],
                            [None] * len(items))
    for (bb, pr), out in zip(items, outs):
        o_ref[bb, :, pr * LANES:(pr + 1) * LANES] = out


def _attn_ctx(sink, q, k, v):
    b, t, _ = q.shape
    bmap = lambda i: (i, 0, 0)
    return pl.pallas_call(
        _attn_ctx_kernel,
        out_shape=jax.ShapeDtypeStruct((b, t, SWA_Q), F32),
        grid=(b // CTX_BATCH,),
        in_specs=[pl.BlockSpec(memory_space=pltpu.SMEM),
                  pl.BlockSpec((CTX_BATCH, t, SWA_Q), bmap),
                  pl.BlockSpec((CTX_BATCH, t, SWA_KV), bmap),
                  pl.BlockSpec((CTX_BATCH, t, SWA_KV), bmap)],
        out_specs=pl.BlockSpec((CTX_BATCH, t, SWA_Q), bmap),
        compiler_params=pltpu.CompilerParams(dimension_semantics=("arbitrary",),
                                             vmem_limit_bytes=VMEM_LIMIT),
        name="attn_ctx",
    )(sink, q, k, v)


LAT_BLOCKS = 2


def _rope(x, cos, sin_lo, sin_hi):
    return x * cos + pltpu.roll(x, LANES - 16, axis=1) * sin_lo + pltpu.roll(x, 16, axis=1) * sin_hi


def _attn_lat_kernel(sink_ref, q_ref, k_ref, v_ref, kc_ref, vc_ref, cos_ref, sl_ref, sh_ref,
                     o_ref, kw_ref, vw_ref):
    t = q_ref.shape[0]
    ab = ATTN_BLOCK
    nb = t // ab
    scale = SWA_HEAD_DIM ** -0.5

    k_rot = _dup_groups(_rope(k_ref[...], cos_ref[...], sl_ref[...], sh_ref[...]))
    v_dup = _dup_groups(v_ref[...])
    zeros = jnp.zeros((ab, LANES), BF16)
    for grp in range(SWA_KV_HEADS):
        kw_ref[grp, 0:ab, :] = zeros
        kw_ref[grp, ab:ab + t, :] = k_rot[grp].astype(BF16)
        kw_ref[grp, ab + t:, :] = zeros
        vw_ref[grp, 0] = zeros
        for blk in range(nb):
            vw_ref[grp, blk + 1] = v_dup[grp][blk * ab:(blk + 1) * ab, :].T.astype(BF16)
        vw_ref[grp, nb + 1] = zeros
    kc = [x.astype(BF16) for x in _dup_groups(kc_ref[...])]
    vct = [x.T.astype(BF16) for x in _dup_groups(vc_ref[...])]
    lc = kc_ref.shape[0]

    key = lax.broadcasted_iota(jnp.int32, (lc + 3 * ab, 2 * ab), 0) - lc
    tq = lax.broadcasted_iota(jnp.int32, (lc + 3 * ab, 2 * ab), 1) & (ab - 1)
    band = (key < 0) | (jnp.abs(tq + ab - key) <= ab)

    def block(it, carry):
        pairs = range(SWA_HEADS // 2)
        qps, sinks, k_dups, vt_dups, masks, places = [], [], [], [], [], []
        for u in range(LAT_BLOCKS):
            nq = it * LAT_BLOCKS + u
            row0 = pl.multiple_of(nq * ab, ab)
            s_abs = key + (nq - 1) * ab
            mask = band & ((key < 0) | ((s_abs >= 0) & (s_abs < t)))
            cos = cos_ref[pl.ds(row0, ab), :]
            s_lo = sl_ref[pl.ds(row0, ab), :]
            s_hi = sh_ref[pl.ds(row0, ab), :]
            k_all = [jnp.concatenate([kc[grp], kw_ref[grp, pl.ds(row0, 3 * ab), :]], axis=0)
                     for grp in range(SWA_KV_HEADS)]
            vt_all = [jnp.concatenate([vct[grp], vw_ref[grp, nq], vw_ref[grp, nq + 1], vw_ref[grp, nq + 2]],
                                      axis=1) for grp in range(SWA_KV_HEADS)]
            for pr in pairs:
                qps.append(_rope(q_ref[pl.ds(row0, ab), pr * LANES:(pr + 1) * LANES], cos, s_lo, s_hi) * scale)
                sinks.append((sink_ref[2 * pr], sink_ref[2 * pr + 1]))
                k_dups.append(k_all[pr // 2])
                vt_dups.append(vt_all[pr // 2])
                masks.append(mask)
                places.append((row0, pr))
        outs = _pairs_attention(qps, sinks, k_dups, vt_dups, masks)
        for (row0, pr), out in zip(places, outs):
            o_ref[pl.ds(row0, ab), pr * LANES:(pr + 1) * LANES] = out
        return carry

    lax.fori_loop(0, nb // LAT_BLOCKS, block, 0)


def _attn_lat(sink, q, k, v, kc, vc, cos, sin_lo, sin_hi):
    b, t, _ = q.shape
    lc = kc.shape[1]
    bmap = lambda i: (i, 0, 0)
    full = lambda i: (0, 0)
    return pl.pallas_call(
        _attn_lat_kernel,
        out_shape=jax.ShapeDtypeStruct((b, t, SWA_Q), F32),
        grid=(b,),
        in_specs=[pl.BlockSpec(memory_space=pltpu.SMEM),
                  pl.BlockSpec((None, t, SWA_Q), bmap),
                  pl.BlockSpec((None, t, SWA_KV), bmap),
                  pl.BlockSpec((None, t, SWA_KV), bmap),
                  pl.BlockSpec((None, lc, SWA_KV), bmap),
                  pl.BlockSpec((None, lc, SWA_KV), bmap),
                  pl.BlockSpec((t, LANES), full),
                  pl.BlockSpec((t, LANES), full),
                  pl.BlockSpec((t, LANES), full)],
        out_specs=pl.BlockSpec((None, t, SWA_Q), bmap),
        scratch_shapes=[pltpu.VMEM((SWA_KV_HEADS, t + 2 * ATTN_BLOCK, LANES), BF16),
                        pltpu.VMEM((SWA_KV_HEADS, t // ATTN_BLOCK + 2, LANES, ATTN_BLOCK), BF16)],
        compiler_params=pltpu.CompilerParams(dimension_semantics=("arbitrary",),
                                             vmem_limit_bytes=VMEM_LIMIT),
        name="attn_lat",
    )(sink, q, k, v, kc, vc, cos, sin_lo, sin_hi)


def _rope_tables(t):
    half = SWA_HEAD_DIM // 2
    quarter = half // 2
    pos = jnp.arange(t)
    row = (pos // GRID_W).astype(F32)
    col = (pos % GRID_W).astype(F32)
    inv_freq = ROPE_BASE ** (-jnp.arange(quarter, dtype=F32) / quarter)
    lane = jnp.arange(LANES)
    d = lane % SWA_HEAD_DIM
    freq = inv_freq[d % quarter]
    use_row = (d < half)
    ang = jnp.where(use_row[None, :], row[:, None], col[:, None]) * freq[None, :]
    cos = jnp.cos(ang)
    sin = jnp.sin(ang)
    lower = (d % half) < quarter
    return cos, jnp.where(lower[None, :], -sin, 0.0), jnp.where(lower[None, :], 0.0, sin)


def _route(sel, scores):
    n = sel.shape[1]
    gsz = N_EXPERTS // N_EXPERT_GROUPS

    def first_max(x, idx, size):
        m = jnp.max(x, axis=0, keepdims=True)
        first = jnp.min(jnp.where(x == m, idx, float(size)), axis=0, keepdims=True)
        return m, idx == first

    i8 = lax.broadcasted_iota(jnp.int32, (gsz, n), 0).astype(F32)
    rows = []
    for g in range(N_EXPERT_GROUPS):
        slab = sel[g * gsz:(g + 1) * gsz, :]
        m1, hit = first_max(slab, i8, gsz)
        m2 = jnp.max(jnp.where(hit, NEG_INF, slab), axis=0, keepdims=True)
        rows.append(m1 + m2)
    gscore = jnp.concatenate(rows, axis=0)
    gsel = jnp.zeros((N_EXPERT_GROUPS, n), F32)
    for _ in range(TOPK_GROUPS):
        _, hit = first_max(gscore, i8, N_EXPERT_GROUPS)
        gsel = jnp.where(hit, 1.0, gsel)
        gscore = jnp.where(hit, NEG_INF, gscore)
    emask = jnp.concatenate(
        [jnp.broadcast_to(gsel[g:g + 1, :], (gsz, n)) for g in range(N_EXPERT_GROUPS)], axis=0)
    cand = jnp.where(emask > 0.5, sel, NEG_INF)
    ie = lax.broadcasted_iota(jnp.int32, (N_EXPERTS, n), 0).astype(F32)
    w = jnp.zeros((N_EXPERTS, n), F32)
    chosen = jnp.zeros((N_EXPERTS, n), F32)
    hits = []
    for _ in range(TOP_K):
        _, hit = first_max(cand, ie, N_EXPERTS)
        hits.append(hit)
        w = jnp.where(hit, scores, w)
        chosen = jnp.where(hit, 1.0, chosen)
        cand = jnp.where(hit, NEG_INF, cand)
    gates = w / jnp.sum(w, axis=0, keepdims=True) * ROUTED_SCALE

    s_idx = lax.broadcasted_iota(jnp.int32, (n, n), 0)
    t_idx = lax.broadcasted_iota(jnp.int32, (n, n), 1)
    tile_shift = MOE_TILE.bit_length() - 1
    before = jnp.where((s_idx < t_idx) & ((s_idx >> tile_shift) == (t_idx >> tile_shift)), 1.0, 0.0)
    rank = _dot(chosen.astype(BF16), before.astype(BF16))
    e_row = lax.broadcasted_iota(jnp.int32, (N_EXPERTS, N_EXPERTS), 0)
    e_col = lax.broadcasted_iota(jnp.int32, (N_EXPERTS, N_EXPERTS), 1)
    below = jnp.where(e_col < e_row, 1.0, 0.0).astype(BF16)
    lane_tile = lax.broadcasted_iota(jnp.int32, (1, n), 1) >> tile_shift
    sizes, starts = [], []
    first_row = jnp.zeros((N_EXPERTS, n), F32)
    for ti in range(n // MOE_TILE):
        count = jnp.sum(chosen[:, ti * MOE_TILE:(ti + 1) * MOE_TILE], axis=1, keepdims=True)
        padded = jnp.floor((count + (SORT_ALIGN - 1)) * (1.0 / SORT_ALIGN)) * SORT_ALIGN
        padded = jnp.broadcast_to(padded, (N_EXPERTS, LANES))
        start = _dot(below, padded.astype(BF16))
        first_row = jnp.where(lane_tile == ti, start[:, 0:1], first_row)
        sizes.append(padded)
        starts.append(start)
    row = first_row + rank
    pos = jnp.concatenate([jnp.sum(jnp.where(h, row, 0.0), axis=0, keepdims=True) for h in hits], axis=0)
    wts = jnp.concatenate([jnp.sum(jnp.where(h, gates, 0.0), axis=0, keepdims=True) for h in hits], axis=0)
    return pos, wts, sizes, starts


def _outproj_kernel(gla_ref, att_ref, x_ref, wo_ref, g1_ref, sh_ref, sc_ref, ng_ref, rw_ref, rwh_ref,
                    rb_ref, x1_ref, xm_ref, pos_ref, wts_ref, cnt_ref, start_ref):
    bb, tb, d = x_ref.shape
    tm = bb * tb
    y = (_dot(gla_ref[...].reshape(tm, GLA_V).astype(BF16), wo_ref[0:GLA_V, :])
         + _dot(att_ref[...].reshape(tm, SWA_Q).astype(BF16), wo_ref[GLA_V:, :]))
    x1 = x_ref[...].reshape(tm, d) + g1_ref[...] * y
    x1_ref[...] = x1.reshape(bb, tb, d)
    xm = _rms_norm(x1, ng_ref[...]) * (1.0 + sc_ref[...]) + sh_ref[...]
    xm_hi, xm_lo = _split_hi_lo(xm)
    xm_ref[...] = xm_hi.reshape(bb, tb, d)
    lg = _dot(xm_hi, rw_ref[...])
    logits = lg[:, :N_EXPERTS] + lg[:, N_EXPERTS:] + _dot(xm_lo, rwh_ref[...])
    lt = jnp.concatenate([logits, jnp.zeros((tm, LANES - N_EXPERTS), F32)], axis=1).T[:N_EXPERTS, :]
    scores = _sigmoid(lt)
    pos, wts, sizes, starts = _route(scores + rb_ref[...], scores)
    tiles_per_batch = tb // MOE_TILE
    for ti in range(tm // MOE_TILE):
        at = (ti // tiles_per_batch, ti % tiles_per_batch)
        pos_ref[at] = pos[:, ti * MOE_TILE:(ti + 1) * MOE_TILE]
        wts_ref[at] = wts[:, ti * MOE_TILE:(ti + 1) * MOE_TILE]
        cnt_ref[at] = sizes[ti]
        start_ref[at] = starts[ti]


OUTPROJ_TILE = 512


def _outproj(gla_out, att_out, x, w_out, g1, sh2, sc2, norm_g, rw_cat, rw_hi, rbias):
    b, t, d = x.shape
    nmod = g1.shape[0]
    tb = min(t, OUTPROJ_TILE)
    bb = OUTPROJ_TILE // tb if nmod == 1 else 1
    tpb = tb // MOE_TILE
    mod_map = (lambda i, j: (i, 0, 0)) if nmod > 1 else (lambda i, j: (0, 0, 0))
    row = lambda i, j: (i, j, 0)
    full = lambda i, j: (0, 0)
    tile = lambda i, j: (i, j, 0, 0)
    nt = t // MOE_TILE
    return pl.pallas_call(
        _outproj_kernel,
        out_shape=(jax.ShapeDtypeStruct((b, t, d), F32),
                   jax.ShapeDtypeStruct((b, t, d), BF16),
                   jax.ShapeDtypeStruct((b, nt, TOP_K, MOE_TILE), F32),
                   jax.ShapeDtypeStruct((b, nt, TOP_K, MOE_TILE), F32),
                   jax.ShapeDtypeStruct((b, nt, N_EXPERTS, LANES), F32),
                   jax.ShapeDtypeStruct((b, nt, N_EXPERTS, LANES), F32)),
        grid=(b // bb, t // tb),
        in_specs=[pl.BlockSpec((bb, tb, GLA_V), row),
                  pl.BlockSpec((bb, tb, SWA_Q), row),
                  pl.BlockSpec((bb, tb, d), row),
                  pl.BlockSpec((d, d), full),
                  pl.BlockSpec((None, 1, d), mod_map),
                  pl.BlockSpec((None, 1, d), mod_map),
                  pl.BlockSpec((None, 1, d), mod_map),
                  pl.BlockSpec((1, d), full),
                  pl.BlockSpec((d, 2 * N_EXPERTS), full),
                  pl.BlockSpec((d, N_EXPERTS), full),
                  pl.BlockSpec((N_EXPERTS, 1), full)],
        out_specs=(pl.BlockSpec((bb, tb, d), row),
                   pl.BlockSpec((bb, tb, d), row),
                   pl.BlockSpec((bb, tpb, TOP_K, MOE_TILE), tile),
                   pl.BlockSpec((bb, tpb, TOP_K, MOE_TILE), tile),
                   pl.BlockSpec((bb, tpb, N_EXPERTS, LANES), tile),
                   pl.BlockSpec((bb, tpb, N_EXPERTS, LANES), tile)),
        compiler_params=pltpu.CompilerParams(dimension_semantics=("arbitrary", "arbitrary"),
                                             vmem_limit_bytes=VMEM_LIMIT),
        name="outproj",
    )(gla_out, att_out, x, w_out, g1, sh2, sc2, norm_g, rw_cat, rw_hi, rbias)


MOE_TILE = 256
SORT_ALIGN = 16
SORT_ROWS = 3072
ROW_TILE = 512
GATHER_SLOTS = 9
FFN_CHAINS = 4
COMBINE_CHUNK = 1024
ALWAYS_ROWS = 2560
COMBINE_TAIL = 512
COMBINE_SLOTS = 2


def _moe_sort_kernel(tiles_a, used_ref, xa_ref, xb_ref, pos_ref, xs_hbm, ybuf, osem):
    i = pl.program_id(0)
    last = pl.num_programs(0) - 1
    slot = lax.rem(i, 2)
    x = jnp.where(i < tiles_a, xa_ref[...], xb_ref[...])
    pos = pos_ref[...]
    tm = x.shape[0]
    used = used_ref[i]

    def out_copies(tile, of_slot, start):
        def one(r0, r1):
            row = pl.multiple_of(tile * SORT_ROWS + r0, tm)
            cp = pltpu.make_async_copy(ybuf.at[of_slot, r0:r1, :], xs_hbm.at[pl.ds(row, r1 - r0), :],
                                       osem.at[of_slot])
            if start:
                cp.start()
            else:
                cp.wait()

        one(0, ALWAYS_ROWS)
        for r0 in range(ALWAYS_ROWS, SORT_ROWS, tm):
            pl.when(r0 < used_ref[tile])(functools.partial(one, r0, r0 + tm))

    @pl.when(i >= 2)
    def _():
        out_copies(i - 2, slot, False)

    rows = lax.broadcasted_iota(jnp.int32, (tm, tm), 0).astype(F32).astype(BF16)
    one_bf = jnp.ones((tm, tm), BF16)

    def fill(blk):
        local = (pos - float(blk * tm)).astype(BF16)
        onehot = jnp.zeros((tm, tm), BF16)
        for k in range(TOP_K):
            onehot = jnp.where(rows == local[k:k + 1, :], one_bf, onehot)
        ybuf[slot, blk * tm:(blk + 1) * tm, :] = _dot(onehot, x).astype(BF16)

    for blk in range(SORT_ROWS // tm):
        if (blk + 1) * tm <= ALWAYS_ROWS:
            fill(blk)
        else:
            pl.when(blk * tm < used)(functools.partial(fill, blk))
    out_copies(i, slot, True)

    @pl.when(i == last)
    def _():
        out_copies(i, slot, False)

        @pl.when(i >= 1)
        def _():
            out_copies(i - 1, 1 - slot, False)


def _moe_sort(xm_a, xm_b, pos, used):
    d = xm_a.shape[1]
    nt, _, tm = pos.shape
    tiles_a = xm_a.shape[0] // tm
    grid_spec = pltpu.PrefetchScalarGridSpec(
        num_scalar_prefetch=1,
        grid=(nt,),
        in_specs=[pl.BlockSpec((tm, d), lambda i, u: (jnp.minimum(i, tiles_a - 1), 0)),
                  pl.BlockSpec((tm, d), lambda i, u: (jnp.maximum(i - tiles_a, 0), 0)),
                  pl.BlockSpec((None, TOP_K, tm), lambda i, u: (i, 0, 0))],
        out_specs=pl.BlockSpec(memory_space=pl.ANY),
        scratch_shapes=[pltpu.VMEM((2, SORT_ROWS, d), BF16),
                        pltpu.SemaphoreType.DMA((2,))])
    return pl.pallas_call(
        functools.partial(_moe_sort_kernel, tiles_a),
        out_shape=jax.ShapeDtypeStruct((nt * SORT_ROWS, d), BF16),
        grid_spec=grid_spec,
        compiler_params=pltpu.CompilerParams(dimension_semantics=("arbitrary",),
                                             vmem_limit_bytes=VMEM_LIMIT),
        name="moe_sort",
    )(used, xm_a, xm_b, pos)


def _moe_row_tiles(n_tokens):
    rows = n_tokens * TOP_K + (n_tokens // MOE_TILE) * N_EXPERTS * (SORT_ALIGN - 1) + N_EXPERTS * (ROW_TILE - 1)
    return -(-rows // ROW_TILE) + GATHER_SLOTS - 1


PLAN_CHUNK = 1280


def _int_dot_r(a, onehot):
    hi = jnp.floor(a * (1.0 / 256.0))
    return _dot(hi.astype(BF16), onehot) * 256.0 + _dot((a - hi * 256.0).astype(BF16), onehot)


def _int_dot_l(onehot, b):
    hi = jnp.floor(b * (1.0 / 256.0))
    return _dot(onehot, hi.astype(BF16)) * 256.0 + _dot(onehot, (b - hi * 256.0).astype(BF16))


def _moe_plan_kernel(cnt_ref, start_ref, src_ref, first_ref, tiles_ref, nu_ref, back_ref):
    nt, ne = cnt_ref.shape
    gpt = SORT_ROWS // SORT_ALIGN
    gpr = ROW_TILE // SORT_ALIGN
    gc = cnt_ref[...] * (1.0 / SORT_ALIGN)
    ls = start_ref[...] * (1.0 / SORT_ALIGN)

    def transpose(x):
        x = jnp.concatenate([x, jnp.zeros((nt, LANES - ne), F32)], axis=1)
        x = jnp.concatenate([x, jnp.zeros((LANES - nt, LANES), F32)], axis=0)
        return x.T[:ne, :nt]

    def tri(n, keep):
        return jnp.where(keep(lax.broadcasted_iota(jnp.int32, (n, n), 0),
                              lax.broadcasted_iota(jnp.int32, (n, n), 1)), 1.0, 0.0).astype(BF16)

    gc_t = transpose(gc)
    ls_t = transpose(ls)
    tot_c = jnp.broadcast_to(jnp.sum(gc_t, axis=1, keepdims=True), (ne, LANES))
    ptot_c = jnp.floor((tot_c + (gpr - 1)) * (1.0 / gpr)) * gpr
    gend_c = _int_dot_l(tri(ne, lambda r, c: c <= r), ptot_c)
    gstart_c = gend_c - ptot_c
    n_used = gend_c[ne - 1:ne, :] * (1.0 / gpr)
    nu_ref[...] = n_used.astype(jnp.int32)
    tot_r = jnp.sum(gc, axis=0, keepdims=True)
    ptot_r = jnp.floor((tot_r + (gpr - 1)) * (1.0 / gpr)) * gpr
    gstart_r = _int_dot_r(jnp.broadcast_to(ptot_r, (8, ne)), tri(ne, lambda r, c: r < c))
    cumex = _dot(tri(nt, lambda r, c: c < r), gc.astype(BF16))
    cumex_t = _dot(gc_t.astype(BF16), tri(nt, lambda r, c: r < c))
    tile_base = lax.broadcasted_iota(jnp.int32, (nt, ne), 0).astype(F32) * gpt + ls
    table = jnp.concatenate([cumex + gc, cumex, tile_base, gstart_r, jnp.broadcast_to(tot_r, (8, ne))], axis=0)

    e_iota = lax.broadcasted_iota(jnp.int32, (ne, PLAN_CHUNK), 0).astype(F32)
    for ch in range(src_ref.shape[1] // PLAN_CHUNK):
        g = (lax.broadcasted_iota(jnp.int32, (1, PLAN_CHUNK), 1) + ch * PLAN_CHUNK).astype(F32)
        eg = jnp.sum(jnp.where(gend_c[:, 0:1] <= g, 1.0, 0.0), axis=0, keepdims=True)
        picked = _int_dot_r(table, jnp.where(e_iota == eg, 1.0, 0.0).astype(BF16))
        cum_g, cumex_g, base_g = picked[0:nt], picked[nt:2 * nt], picked[2 * nt:3 * nt]
        u = g - picked[3 * nt:3 * nt + 1]
        in_tile = (cumex_g <= u) & (u < cum_g)
        src = jnp.sum(jnp.where(in_tile, base_g - cumex_g, 0.0), axis=0, keepdims=True) + u
        src = jnp.where(u < picked[3 * nt + 8:3 * nt + 9], src, 0.0)
        src_ref[:, ch * PLAN_CHUNK:(ch + 1) * PLAN_CHUNK] = src.astype(jnp.int32)

    first_ref[...] = (gstart_c * (1.0 / gpr)).astype(jnp.int32)
    tiles_ref[...] = (ptot_c * (1.0 / gpr)).astype(jnp.int32)

    lg = lax.broadcasted_iota(jnp.int32, (ne, back_ref.shape[1]), 1).astype(F32)
    for t in range(nt):
        first = ls_t[:, t:t + 1]
        inside = (first <= lg) & (lg < first + gc_t[:, t:t + 1])
        shift = gstart_c[:, 0:1] + cumex_t[:, t:t + 1] - first
        val = jnp.sum(jnp.where(inside, shift + lg, 0.0), axis=0, keepdims=True)
        back_ref[t:t + 1, :] = val.astype(jnp.int32)


def _moe_plan(cnt, start):
    nt, ne = cnt.shape
    row_tiles = _moe_row_tiles(nt * MOE_TILE)
    gpt = SORT_ROWS // SORT_ALIGN
    gpr = ROW_TILE // SORT_ALIGN
    n_src = -(-(row_tiles * gpr) // PLAN_CHUNK) * PLAN_CHUNK
    n_back = -(-gpt // LANES) * LANES
    src, first, tiles, nu, back = pl.pallas_call(
        _moe_plan_kernel,
        out_shape=(jax.ShapeDtypeStruct((1, n_src), jnp.int32),
                   jax.ShapeDtypeStruct((ne, LANES), jnp.int32),
                   jax.ShapeDtypeStruct((ne, LANES), jnp.int32),
                   jax.ShapeDtypeStruct((1, LANES), jnp.int32),
                   jax.ShapeDtypeStruct((nt, n_back), jnp.int32)),
        compiler_params=pltpu.CompilerParams(vmem_limit_bytes=VMEM_LIMIT),
        name="moe_plan",
    )(cnt, start)
    return nu[0, :1], first[:, 0], tiles[:, 0], src[0, :row_tiles * gpr], back[:, :gpt]


def _moe_experts_kernel(nu_ref, first_ref, tiles_ref, src_ref, xs_hbm, wg_ref, wu_ref, wd_ref, ys_hbm,
                        xbuf, ybuf, gsem, osem, wgu_s, wd_s):
    e = pl.program_id(0)
    n_used = nu_ref[0]
    gpr = ROW_TILE // SORT_ALIGN
    part = ROW_TILE // FFN_CHAINS

    def gather(tile, to_slot, j0=0, j1=gpr):
        for j in range(j0, j1):
            row = pl.multiple_of(src_ref[tile * gpr + j] * SORT_ALIGN, SORT_ALIGN)
            pltpu.make_async_copy(xs_hbm.at[pl.ds(row, SORT_ALIGN), :],
                                  xbuf.at[to_slot, j * SORT_ALIGN:(j + 1) * SORT_ALIGN, :],
                                  gsem.at[to_slot]).start(priority=j % 2)

    def drain(of_slot):
        for j in range(gpr):
            pltpu.make_async_copy(xs_hbm.at[0:SORT_ALIGN, :],
                                  xbuf.at[of_slot, j * SORT_ALIGN:(j + 1) * SORT_ALIGN, :], gsem.at[of_slot]).wait()

    def out_copy(tile, of_slot):
        row = pl.multiple_of(tile * ROW_TILE, ROW_TILE)
        return pltpu.make_async_copy(ybuf.at[of_slot], ys_hbm.at[pl.ds(row, ROW_TILE), :], osem.at[of_slot])

    @pl.when(e == 0)
    def _():
        for ahead in range(GATHER_SLOTS - 1):
            gather(ahead, ahead)

    wgu_s[:, :EXPERT_FF] = wg_ref[...].astype(BF16)
    wgu_s[:, EXPERT_FF:] = wu_ref[...].astype(BF16)
    wd_s[...] = wd_ref[...].astype(BF16)

    def row_tile(i, carry):
        r = first_ref[e] + i
        slot = lax.rem(r, GATHER_SLOTS)
        oslot = lax.rem(r, 2)
        next_slot = lax.rem(r + GATHER_SLOTS - 1, GATHER_SLOTS)
        drain(slot)

        @pl.when(r >= 2)
        def _():
            out_copy(r - 2, oslot).wait()

        abs_ = []
        for c in range(FFN_CHAINS):
            abs_.append(_dot(xbuf[slot, c * part:(c + 1) * part, :], wgu_s[...]))
            gather(r + GATHER_SLOTS - 1, next_slot, c * gpr // FFN_CHAINS, (c + 1) * gpr // FFN_CHAINS)
        hs = [(_silu(ab[:, :EXPERT_FF]) * ab[:, EXPERT_FF:]).astype(BF16) for ab in abs_]
        ys = [_dot(h, wd_s[...]).astype(BF16) for h in hs]
        for c in range(FFN_CHAINS):
            ybuf[oslot, c * part:(c + 1) * part, :] = ys[c]
        out_copy(r, oslot).start()
        return carry

    lax.fori_loop(0, tiles_ref[e], row_tile, 0)

    @pl.when(e == pl.num_programs(0) - 1)
    def _():
        for ahead in range(GATHER_SLOTS - 1):
            drain(lax.rem(n_used + ahead, GATHER_SLOTS))
        out_copy(n_used - 1, lax.rem(n_used - 1, 2)).wait()

        @pl.when(n_used >= 2)
        def _():
            out_copy(n_used - 2, lax.rem(n_used, 2)).wait()


def _moe_experts(n_used, first, tiles, src, xs, wg, wu, wd, row_tiles):
    d = xs.shape[-1]
    ne = wg.shape[0]
    w_map = lambda e, nu, fi, ti, sr: (e, 0, 0)
    grid_spec = pltpu.PrefetchScalarGridSpec(
        num_scalar_prefetch=4,
        grid=(ne,),
        in_specs=[pl.BlockSpec(memory_space=pl.ANY),
                  pl.BlockSpec((None, d, EXPERT_FF), w_map),
                  pl.BlockSpec((None, d, EXPERT_FF), w_map),
                  pl.BlockSpec((None, EXPERT_FF, d), w_map)],
        out_specs=pl.BlockSpec(memory_space=pl.ANY),
        scratch_shapes=[pltpu.VMEM((GATHER_SLOTS, ROW_TILE, d), BF16),
                        pltpu.VMEM((2, ROW_TILE, d), BF16),
                        pltpu.SemaphoreType.DMA((GATHER_SLOTS,)),
                        pltpu.SemaphoreType.DMA((2,)),
                        pltpu.VMEM((d, 2 * EXPERT_FF), BF16),
                        pltpu.VMEM((EXPERT_FF, d), BF16)])
    return pl.pallas_call(
        _moe_experts_kernel,
        out_shape=jax.ShapeDtypeStruct((row_tiles * ROW_TILE, d), BF16),
        grid_spec=grid_spec,
        compiler_params=pltpu.CompilerParams(dimension_semantics=("arbitrary",),
                                             vmem_limit_bytes=VMEM_LIMIT),
        name="moe_experts",
    )(n_used, first, tiles, src, xs, wg, wu, wd)


def _moe_combine_kernel(back_ref, used_ref, ys_hbm, pos_ref, wts_ref, xm_ref, x1_ref, g2_ref, fg_ref,
                        swg_ref, swu_ref, swd_ref, o_ref, buf, sem, acc_ref):
    i = pl.program_id(0)
    gpt = SORT_ROWS // SORT_ALIGN
    slot = lax.rem(i, COMBINE_SLOTS)
    ahead = COMBINE_SLOTS - 1
    always = ALWAYS_ROWS
    tail = range(always, SORT_ROWS, COMBINE_TAIL)

    def copies(tile, of_slot, g0, g1, start):
        for g in range(g0, g1):
            row = pl.multiple_of(back_ref[tile * gpt + g] * SORT_ALIGN, SORT_ALIGN) if start else 0
            cp = pltpu.make_async_copy(ys_hbm.at[pl.ds(row, SORT_ALIGN), :],
                                       buf.at[of_slot, g * SORT_ALIGN:(g + 1) * SORT_ALIGN, :], sem.at[of_slot])
            if start:
                cp.start(priority=g % 2)
            else:
                cp.wait()

    def transfer(tile, of_slot, start):
        copies(tile, of_slot, 0, always // SORT_ALIGN, start)
        for c0 in tail:
            pl.when(c0 < used_ref[tile])(functools.partial(
                copies, tile, of_slot, c0 // SORT_ALIGN, (c0 + COMBINE_TAIL) // SORT_ALIGN, start))

    @pl.when(i == 0)
    def _():
        for first in range(min(ahead, buf.shape[0])):
            pl.when(first < pl.num_programs(0))(functools.partial(transfer, first, first, True))

    @pl.when(i + ahead < pl.num_programs(0))
    def _():
        transfer(i + ahead, lax.rem(i + ahead, COMBINE_SLOTS), True)

    x = xm_ref[...]
    tm = x.shape[0]
    pad = jnp.zeros((LANES - TOP_K, tm), F32)
    pos_t = jnp.concatenate([pos_ref[...], pad], axis=0).T
    wts_t = jnp.concatenate([wts_ref[...], pad], axis=0).T
    blk_b, loc_b, wts_b = [], [], []
    for k in range(TOP_K):
        p = jnp.broadcast_to(pos_t[:, k:k + 1], (tm, LANES))
        blk = jnp.floor(p * (1.0 / tm))
        two = lambda v: jnp.concatenate([v.astype(BF16)] * (tm // LANES), axis=1)
        blk_b.append(two(blk))
        loc_b.append(two(p - blk * tm))
        wts_b.append(two(jnp.broadcast_to(wts_t[:, k:k + 1], (tm, LANES))))
    shared = _dot((_silu(_dot(x, swg_ref[...])) * _dot(x, swu_ref[...])).astype(BF16), swd_ref[...])
    transfer(i, slot, False)
    lane = lax.broadcasted_iota(jnp.int32, (tm, tm), 1).astype(F32).astype(BF16)
    zero = jnp.zeros((tm, tm), BF16)
    nowhere = jnp.full((tm, tm), -1.0, BF16)

    def apply(c0, width):
        blocks = []
        for b0 in range(c0, c0 + width, tm):
            comb = zero
            for k in range(TOP_K):
                loc = jnp.where(blk_b[k] == float(b0 // tm), loc_b[k], nowhere)
                comb = jnp.where(lane == loc, wts_b[k], comb)
            blocks.append(comb)
        return _dot(jnp.concatenate(blocks, axis=1), buf[slot, c0:c0 + width, :])

    routed = shared
    for c0 in range(0, always, COMBINE_CHUNK):
        routed = routed + apply(c0, min(COMBINE_CHUNK, always - c0))
    acc_ref[...] = routed
    for c0 in tail:
        @pl.when(c0 < used_ref[i])
        def _(c0=c0):
            acc_ref[...] += apply(c0, COMBINE_TAIL)
    y = x1_ref[...] + g2_ref[...] * acc_ref[...]
    o_ref[...] = _rms_norm(y, fg_ref[...])


def _moe_combine(back, used, ys, pos, wts, xm, x1, g2, final_g, swg, swu, swd, *, tiles_per_mod):
    n, d = xm.shape
    tm = pos.shape[-1]
    nt = n // tm
    gpt = SORT_ROWS // SORT_ALIGN
    row = lambda i, bk, us: (i, 0)
    full = lambda i, bk, us: (0, 0)
    tile = lambda i, bk, us: (i, 0, 0)
    mod_map = lambda i, bk, us: (i // tiles_per_mod, 0, 0)
    grid_spec = pltpu.PrefetchScalarGridSpec(
        num_scalar_prefetch=2,
        grid=(nt,),
        in_specs=[pl.BlockSpec(memory_space=pl.ANY),
                  pl.BlockSpec((None, TOP_K, tm), tile),
                  pl.BlockSpec((None, TOP_K, tm), tile),
                  pl.BlockSpec((tm, d), row),
                  pl.BlockSpec((tm, d), row),
                  pl.BlockSpec((None, 1, d), mod_map),
                  pl.BlockSpec((1, d), full),
                  pl.BlockSpec((d, SHARED_FF), full),
                  pl.BlockSpec((d, SHARED_FF), full),
                  pl.BlockSpec((SHARED_FF, d), full)],
        out_specs=pl.BlockSpec((tm, d), row),
        scratch_shapes=[pltpu.VMEM((COMBINE_SLOTS, SORT_ROWS, d), BF16),
                        pltpu.SemaphoreType.DMA((COMBINE_SLOTS,)),
                        pltpu.VMEM((tm, d), F32)])
    return pl.pallas_call(
        _moe_combine_kernel,
        out_shape=jax.ShapeDtypeStruct((n, d), F32),
        grid_spec=grid_spec,
        compiler_params=pltpu.CompilerParams(dimension_semantics=("arbitrary",),
                                             vmem_limit_bytes=VMEM_LIMIT),
        name="moe_combine",
    )(back, used, ys, pos.reshape(nt, TOP_K, tm), wts.reshape(nt, TOP_K, tm), xm, x1, g2, final_g, swg, swu, swd)


def _mix(x, mods, p, attn_fn, s0=None):
    sh1, sc1, g1, sh2, sc2, _ = mods
    gla_in, lora, q_s, k_s, v_s = _inproj(x, p["norm_attn_g"], sh1, sc1, p["w_gla"], p["w_lora"], p["w_swa"])
    if s0 is None:
        gla_out, s_f, s_b = _gla(gla_in, lora, p["waf"], p["baf"], p["wab"], p["bab"], p["gla_norm_g"])
    else:
        gla_out, s_f, s_b = _gla(gla_in, lora, p["waf"], p["baf"], p["wab"], p["bab"], p["gla_norm_g"],
                                 s0[0], s0[1])
    att_out = attn_fn(q_s, k_s, v_s)
    routed = _outproj(gla_out, att_out, x, p["w_out"], g1, sh2, sc2, p["norm_ffn_g"],
                      p["rw_cat"], p["rw_hi"], p["rbias"])
    return routed, k_s, v_s, s_f, s_b


def _moe(streams, p):
    d = D_MODEL
    (ra, _), (rb, _) = streams
    n_tiles = [r[1].shape[0] * r[1].shape[1] // MOE_TILE for r, _ in streams]
    pos_all = jnp.concatenate([r[2].reshape(-1, TOP_K, MOE_TILE) for r, _ in streams], axis=0)
    cnt_all = jnp.concatenate([r[4][..., 0].reshape(-1, N_EXPERTS) for r, _ in streams], axis=0)
    start_all = jnp.concatenate([r[5][..., 0].reshape(-1, N_EXPERTS) for r, _ in streams], axis=0)
    used = (start_all[:, -1] + cnt_all[:, -1]).astype(jnp.int32)
    xs = _moe_sort(ra[1].reshape(-1, d), rb[1].reshape(-1, d), pos_all, used)
    n_used, first, tiles, src, back = _moe_plan(cnt_all, start_all)
    ys = _moe_experts(n_used, first, tiles, src, xs, p["wg"], p["wu"], p["wd"],
                      _moe_row_tiles(cnt_all.shape[0] * MOE_TILE))
    outs = []
    tile0 = 0
    for ((x1, xm, pos, wts, cnt, start), g2), nt in zip(streams, n_tiles):
        b, t, _ = x1.shape
        tiles_per_mod = (t // MOE_TILE) if g2.shape[0] > 1 else nt
        y = _moe_combine(back[tile0:tile0 + nt].reshape(-1), used[tile0:tile0 + nt], ys, pos, wts,
                         xm.reshape(-1, d), x1.reshape(-1, d), g2, p["final_norm_g"],
                         p["swg"], p["swu"], p["swd"], tiles_per_mod=tiles_per_mod)
        outs.append(y.reshape(b, t, d))
        tile0 += nt
    return outs


def kernel(x_prompt, x_sample, c, cache_swa_k, cache_swa_v, state_gla_fwd, state_gla_bwd, c_ctx, w_ada, b_ada, norm_attn_g, norm_ffn_g, w_in, gla_wa_f, gla_ba_f, gla_wa_b, gla_ba_b, gla_norm_g, swa_sink, w_out, router_w, router_bias, exp_w_gate, exp_w_up, exp_w_down, sh_w_gate, sh_w_up, sh_w_down, final_norm_g):
    l = 0
    d = D_MODEL
    nb_ctx, t_ctx, _ = x_prompt.shape
    nb_lat, t_lat, _ = x_sample.shape

    pad = jnp.zeros((8 - 1 - nb_lat, d), F32)
    cond8 = jnp.concatenate([c_ctx[None, :], c, pad], axis=0)
    mod = _adaln(cond8, w_ada[l], b_ada[l][None, :])
    mods_ctx = [mod[0:1, i * d:(i + 1) * d][:, None, :] for i in range(6)]
    mods_lat = [mod[1:1 + nb_lat, i * d:(i + 1) * d][:, None, :] for i in range(6)]

    zeros_lora = jnp.zeros((GLA_LORA, GLA_QK), F32)
    rw = router_w[l]
    rw_hi = rw.astype(BF16)
    rw_lo = (rw - rw_hi.astype(F32)).astype(BF16)
    w_in_b = w_in[l].astype(BF16)
    p = {
        "norm_attn_g": norm_attn_g[l][None, :],
        "norm_ffn_g": norm_ffn_g[l][None, :],
        "final_norm_g": final_norm_g[None, :],
        "w_gla": w_in_b[:, :2 * GLA_QK + 2 * GLA_V],
        "w_lora": w_in_b[:, 2 * GLA_QK + 2 * GLA_V:2 * GLA_QK + 2 * GLA_V + 2 * GLA_LORA],
        "w_swa": w_in_b[:, 2 * GLA_QK + 2 * GLA_V + 2 * GLA_LORA:],
        "waf": jnp.concatenate([gla_wa_f[l], zeros_lora], axis=0).astype(BF16),
        "wab": jnp.concatenate([zeros_lora, gla_wa_b[l]], axis=0).astype(BF16),
        "baf": gla_ba_f[l][None, :],
        "bab": gla_ba_b[l][None, :],
        "gla_norm_g": gla_norm_g[l][None, :],
        "w_out": w_out[l].astype(BF16),
        "rw_cat": jnp.concatenate([rw_hi, rw_lo], axis=1),
        "rw_hi": rw_hi,
        "rbias": router_bias[l][:, None],
        "wg": exp_w_gate[l], "wu": exp_w_up[l], "wd": exp_w_down[l],
        "swg": sh_w_gate[l].astype(BF16), "swu": sh_w_up[l].astype(BF16),
        "swd": sh_w_down[l].astype(BF16),
    }
    sink = swa_sink[l]

    routed_ctx, k_c, v_c, s_f, s_b = _mix(x_prompt, mods_ctx, p, functools.partial(_attn_ctx, sink))

    cos, sin_lo, sin_hi = _rope_tables(t_lat)
    kc = cache_swa_k[:, l].reshape(nb_lat, -1, SWA_KV)
    vc = cache_swa_v[:, l].reshape(nb_lat, -1, SWA_KV)
    lat_attn = lambda q, k, v: _attn_lat(sink, q, k, v, kc, vc, cos, sin_lo, sin_hi)
    s0 = (state_gla_fwd[:, l].reshape(nb_lat, GLA_QK, GLA_DV),
          state_gla_bwd[:, l].reshape(nb_lat, GLA_QK, GLA_DV))
    routed_lat, _, _, _, _ = _mix(x_sample, mods_lat, p, lat_attn, s0)
    y_prompt, y_sample = _moe([(routed_ctx, mods_ctx[5]), (routed_lat, mods_lat[5])], p)

    new_k = k_c.reshape(nb_ctx, 1, t_ctx, SWA_KV_HEADS, SWA_HEAD_DIM)
    new_v = v_c.reshape(nb_ctx, 1, t_ctx, SWA_KV_HEADS, SWA_HEAD_DIM)
    new_sf = s_f.reshape(nb_ctx, 1, GLA_HEADS, GLA_DK, GLA_DV)
    new_sb = s_b.reshape(nb_ctx, 1, GLA_HEADS, GLA_DK, GLA_DV)
    return (y_prompt, y_sample, new_k, new_v, new_sf, new_sb)
```

```python
import functools

import jax
import jax.numpy as jnp
from jax import lax
from jax.experimental import pallas as pl
from jax.experimental.pallas import tpu as pltpu

F32 = jnp.float32
BF16 = jnp.bfloat16

D_MODEL = 1024
GLA_HEADS = 4
GLA_DK = 64
GLA_DV = 128
GLA_LORA = 16
GLA_GATE_NORM = 16.0
GLA_CHUNK = 64
GLA_QK = GLA_HEADS * GLA_DK
GLA_V = GLA_HEADS * GLA_DV
SWA_HEAD_DIM = 64
SWA_HEADS = 8
SWA_KV_HEADS = 2
SWA_Q = SWA_HEADS * SWA_HEAD_DIM
SWA_KV = SWA_KV_HEADS * SWA_HEAD_DIM
ATTN_BLOCK = 128
GRID_W = 64
ROPE_BASE = 10000.0
N_EXPERTS = 64
TOP_K = 8
N_EXPERT_GROUPS = 8
TOPK_GROUPS = 4
EXPERT_FF = 128
SHARED_FF = 256
ROUTED_SCALE = 2.5
EPS = 1e-6

LANES = 128
VMEM_LIMIT = 56 * 1024 * 1024

NEG_INF = float("-inf")


def _dot(a, b):
    return jnp.dot(a, b, preferred_element_type=F32)


def _dot_nt(a, b):
    return lax.dot_general(a, b, (((1,), (1,)), ((), ())), preferred_element_type=F32)


def _split_hi_lo(x):
    hi = x.astype(BF16)
    lo = (x - hi.astype(F32)).astype(BF16)
    return hi, lo


def _sigmoid(x):
    return 1.0 / (1.0 + jnp.exp(-x))


def _silu(x):
    return x * _sigmoid(x)


def _rms_norm(x, g):
    ms = jnp.mean(x * x, axis=-1, keepdims=True)
    return x * lax.rsqrt(ms + EPS) * g


def _adaln_kernel(c_ref, w_ref, b_ref, o_ref):
    a_hi, a_lo = _split_hi_lo(_silu(c_ref[...]))
    w_hi, w_lo = _split_hi_lo(w_ref[...])
    o_ref[...] = _dot(a_hi, w_hi) + _dot(a_lo, w_hi) + _dot(a_hi, w_lo) + b_ref[...]


def _adaln(cond8, w_ada, b_ada):
    n = w_ada.shape[1]
    tn = 1536
    return pl.pallas_call(
        _adaln_kernel,
        out_shape=jax.ShapeDtypeStruct((8, n), F32),
        grid=(n // tn,),
        in_specs=[pl.BlockSpec((8, D_MODEL), lambda j: (0, 0)),
                  pl.BlockSpec((D_MODEL, tn), lambda j: (0, j)),
                  pl.BlockSpec((1, tn), lambda j: (0, j))],
        out_specs=pl.BlockSpec((8, tn), lambda j: (0, j)),
        compiler_params=pltpu.CompilerParams(dimension_semantics=("arbitrary",),
                                             vmem_limit_bytes=VMEM_LIMIT),
        name="adaln",
    )(cond8, w_ada, b_ada)


def _inproj_kernel(x_ref, g_ref, sh_ref, sc_ref, wg_ref, wl_ref, ws_ref,
                   gla_ref, lora_ref, q_ref, k_ref, v_ref):
    bb, tb, d = x_ref.shape
    x = x_ref[...].reshape(bb * tb, d)
    h = _rms_norm(x, g_ref[...]) * (1.0 + sc_ref[...]) + sh_ref[...]
    hb = h.astype(BF16)
    gla_ref[...] = _dot(hb, wg_ref[...]).reshape(gla_ref.shape)
    lora_ref[...] = _dot(hb, wl_ref[...]).reshape(lora_ref.shape)
    s = _dot(hb, ws_ref[...])
    q_ref[...] = s[:, :SWA_Q].reshape(q_ref.shape)
    k_ref[...] = s[:, SWA_Q:SWA_Q + SWA_KV].reshape(k_ref.shape)
    v_ref[...] = s[:, SWA_Q + SWA_KV:].reshape(v_ref.shape)


INPROJ_TILE = 1024


def _inproj(x, g, sh, sc, w_gla, w_lora, w_swa):
    b, t, d = x.shape
    nmod = sh.shape[0]
    tb = min(t, INPROJ_TILE)
    bb = INPROJ_TILE // tb if nmod == 1 else 1
    mod_map = (lambda i, j: (i, 0, 0)) if nmod > 1 else (lambda i, j: (0, 0, 0))
    row = lambda i, j: (i, j, 0)
    full = lambda i, j: (0, 0)
    n_gla = w_gla.shape[1]
    n_lora = w_lora.shape[1]
    return pl.pallas_call(
        _inproj_kernel,
        out_shape=(jax.ShapeDtypeStruct((b, t, n_gla), F32),
                   jax.ShapeDtypeStruct((b, t, n_lora), F32),
                   jax.ShapeDtypeStruct((b, t, SWA_Q), F32),
                   jax.ShapeDtypeStruct((b, t, SWA_KV), F32),
                   jax.ShapeDtypeStruct((b, t, SWA_KV), F32)),
        grid=(b // bb, t // tb),
        in_specs=[pl.BlockSpec((bb, tb, d), row),
                  pl.BlockSpec((1, d), full),
                  pl.BlockSpec((None, 1, d), mod_map),
                  pl.BlockSpec((None, 1, d), mod_map),
                  pl.BlockSpec((d, n_gla), full, pipeline_mode=pl.Buffered(1)),
                  pl.BlockSpec((d, n_lora), full, pipeline_mode=pl.Buffered(1)),
                  pl.BlockSpec((d, w_swa.shape[1]), full, pipeline_mode=pl.Buffered(1))],
        out_specs=(pl.BlockSpec((bb, tb, n_gla), row),
                   pl.BlockSpec((bb, tb, n_lora), row),
                   pl.BlockSpec((bb, tb, SWA_Q), row),
                   pl.BlockSpec((bb, tb, SWA_KV), row),
                   pl.BlockSpec((bb, tb, SWA_KV), row)),
        compiler_params=pltpu.CompilerParams(dimension_semantics=("arbitrary", "arbitrary"),
                                             vmem_limit_bytes=VMEM_LIMIT),
        name="inproj",
    )(x, g, sh, sc, w_gla, w_lora, w_swa)


SCAN_UNROLL = 4
OUT_UNROLL = 4


def _log_sigmoid(x):
    return jnp.minimum(x, 0.0) - jnp.log(1.0 + jnp.exp(-jnp.abs(x)))


def _heads_to_rows(x):
    return jnp.concatenate([x[:, h * LANES:(h + 1) * LANES] for h in range(GLA_HEADS)], axis=0)


def _rows_to_heads(x, c):
    return jnp.concatenate([x[h * c:(h + 1) * c, :] for h in range(GLA_HEADS)], axis=1)


def _gla_kernel(has_init, q_ref, k_ref, v_ref, g_ref, lora_ref, waf_ref, baf_ref, wab_ref, bab_ref,
                ng_ref, *rest):
    if has_init:
        s0f_ref, s0b_ref, *rest = rest
    (out_ref, sf_ref, sb_ref, laf_ref, lab_ref, oacc_ref, qtf_ref, qtb_ref, saf_ref, sab_ref,
     stf_ref, stb_ref) = rest
    t = q_ref.shape[0]
    c = GLA_CHUNK
    n = t // c
    hc = GLA_HEADS * c

    lora = lora_ref[...].astype(BF16)
    laf_ref[...] = _log_sigmoid(_dot(lora, waf_ref[...]) + baf_ref[...]) * (1.0 / GLA_GATE_NORM)
    lab_ref[...] = _log_sigmoid(_dot(lora, wab_ref[...]) + bab_ref[...]) * (1.0 / GLA_GATE_NORM)

    if has_init:
        stf_ref[...] = s0f_ref[...].T
        stb_ref[...] = s0b_ref[...].T
    else:
        stf_ref[...] = jnp.zeros_like(stf_ref)
        stb_ref[...] = jnp.zeros_like(stb_ref)
    oacc_ref[...] = jnp.zeros_like(oacc_ref)

    r64 = lax.broadcasted_iota(jnp.int32, (c, c), 0)
    c64 = lax.broadcasted_iota(jnp.int32, (c, c), 1)
    tri_f = jnp.where(c64 <= r64, 1.0, 0.0).astype(BF16)
    tri_b = jnp.where(c64 >= r64, 1.0, 0.0).astype(BF16)
    rr = lax.broadcasted_iota(jnp.int32, (hc, hc), 0)
    cc = lax.broadcasted_iota(jnp.int32, (hc, hc), 1)
    same_head = (rr >> 6) == (cc >> 6)
    keep_f = same_head & ((rr & (c - 1)) >= (cc & (c - 1)))
    keep_b = same_head & ((rr & (c - 1)) <= (cc & (c - 1)))
    head_mask = jnp.where(same_head, 1.0, 0.0).astype(BF16)
    norm_g = ng_ref[...]

    def chunk_rows(ci):
        return pl.ds(pl.multiple_of(ci * c, c), c)

    def tile_heads(x):
        x4 = jnp.concatenate([x] * GLA_HEADS, axis=0)
        return jnp.where(same_head, x4, 0.0).astype(BF16)

    def scan_step(i, carry):
        dirs = []
        for u in range(SCAN_UNROLL):
            dirs += [(SCAN_UNROLL * i + u, laf_ref, tri_f, keep_f, c - 1, stf_ref, saf_ref, qtf_ref),
                     (n - 1 - SCAN_UNROLL * i - u, lab_ref, tri_b, keep_b, 0, stb_ref, sab_ref, qtb_ref)]
        cums = []
        for ci, la_ref, tri, _, _, _, _, _ in dirs:
            la_hi, la_lo = _split_hi_lo(la_ref[chunk_rows(ci), :])
            cums.append(_dot(tri, la_hi) + _dot(tri, la_lo))
        ops = []
        for (ci, _, _, _, last_row, _, _, qt_ref), cum in zip(dirs, cums):
            sl = chunk_rows(ci)
            tot = cum[last_row:last_row + 1, :]
            kc = k_ref[sl, :]
            qt = q_ref[sl, :] * (GLA_DK ** -0.5) * jnp.exp(cum)
            qt_ref[sl, :] = qt.astype(BF16)
            v_rows = _heads_to_rows(v_ref[sl, :])
            ops.append((tot, tile_heads(qt), tile_heads(kc * jnp.exp(-cum)),
                        tile_heads(kc * jnp.exp(tot - cum)), v_rows))
        atts = [_dot_nt(q4, k4) for _, q4, k4, _, _ in ops]
        incs = []
        for (_, _, _, keep, _, _, _, _), (_, _, _, kd4, v_rows), att in zip(dirs, ops, atts):
            att = jnp.where(keep, att, 0.0).astype(BF16)
            incs.append((_dot(att, v_rows.astype(BF16)), _dot(v_rows.T.astype(BF16), kd4)))
        for (ci, _, _, _, _, st_ref, snap_ref, _), (tot, _, _, _, _), (o_intra, st_inc) in zip(dirs, ops, incs):
            oacc_ref[ci] += o_intra
            st = st_ref[...]
            snap_ref[ci] = st.astype(BF16)
            st_ref[...] = jnp.exp(tot) * st + st_inc
        return carry

    def tile_heads_bf16(x):
        return jnp.concatenate([x] * GLA_HEADS, axis=0) * head_mask

    def out_step(i, carry):
        chunks = [OUT_UNROLL * i + u for u in range(OUT_UNROLL)]
        inter = []
        for ci in chunks:
            sl = chunk_rows(ci)
            q4 = jnp.concatenate([tile_heads_bf16(qtf_ref[sl, :]), tile_heads_bf16(qtb_ref[sl, :])], axis=1)
            st = jnp.concatenate([saf_ref[ci], sab_ref[ci]], axis=1)
            inter.append(_dot_nt(q4, st))
        for ci, o_inter in zip(chunks, inter):
            sl = chunk_rows(ci)
            on = _rms_norm(oacc_ref[ci] + o_inter, norm_g)
            gate = _silu(_heads_to_rows(g_ref[sl, :]))
            out_ref[sl, :] = _rows_to_heads(on * gate, c)
        return carry

    lax.fori_loop(0, n // SCAN_UNROLL, scan_step, 0)
    lax.fori_loop(0, n // OUT_UNROLL, out_step, 0)
    sf_ref[...] = stf_ref[...].T
    sb_ref[...] = stb_ref[...].T


def _gla(gla_in, lora, waf, baf, wab, bab, norm_g, s0f=None, s0b=None):
    b, t, _ = gla_in.shape
    has_init = s0f is not None
    n = t // GLA_CHUNK
    bmap = lambda i: (i, 0, 0)
    full = lambda i: (0, 0)
    in_specs = [pl.BlockSpec((None, t, GLA_QK), lambda i: (i, 0, 0)),
                pl.BlockSpec((None, t, GLA_QK), lambda i: (i, 0, 1)),
                pl.BlockSpec((None, t, GLA_V), lambda i: (i, 0, 1)),
                pl.BlockSpec((None, t, GLA_V), lambda i: (i, 0, 2)),
                pl.BlockSpec((None, t, 2 * GLA_LORA), bmap),
                pl.BlockSpec((2 * GLA_LORA, GLA_QK), full),
                pl.BlockSpec((1, GLA_QK), full),
                pl.BlockSpec((2 * GLA_LORA, GLA_QK), full),
                pl.BlockSpec((1, GLA_QK), full),
                pl.BlockSpec((1, GLA_DV), full)]
    args = [gla_in, gla_in, gla_in, gla_in, lora, waf, baf, wab, bab, norm_g]
    if has_init:
        in_specs += [pl.BlockSpec((None, GLA_QK, GLA_DV), bmap)] * 2
        args += [s0f, s0b]
    return pl.pallas_call(
        functools.partial(_gla_kernel, has_init),
        out_shape=(jax.ShapeDtypeStruct((b, t, GLA_V), F32),
                   jax.ShapeDtypeStruct((b, GLA_QK, GLA_DV), F32),
                   jax.ShapeDtypeStruct((b, GLA_QK, GLA_DV), F32)),
        grid=(b,),
        in_specs=in_specs,
        out_specs=(pl.BlockSpec((None, t, GLA_V), bmap),
                   pl.BlockSpec((None, GLA_QK, GLA_DV), bmap),
                   pl.BlockSpec((None, GLA_QK, GLA_DV), bmap)),
        scratch_shapes=[pltpu.VMEM((t, GLA_QK), F32),
                        pltpu.VMEM((t, GLA_QK), F32),
                        pltpu.VMEM((n, GLA_HEADS * GLA_CHUNK, GLA_DV), F32),
                        pltpu.VMEM((t, GLA_QK), BF16),
                        pltpu.VMEM((t, GLA_QK), BF16),
                        pltpu.VMEM((n, GLA_DV, GLA_QK), BF16),
                        pltpu.VMEM((n, GLA_DV, GLA_QK), BF16),
                        pltpu.VMEM((GLA_DV, GLA_QK), F32),
                        pltpu.VMEM((GLA_DV, GLA_QK), F32)],
        compiler_params=pltpu.CompilerParams(dimension_semantics=("arbitrary",),
                                             vmem_limit_bytes=VMEM_LIMIT),
        name="gla",
    )(*args)


def _dup_groups(x):
    lo = lax.broadcasted_iota(jnp.int32, x.shape, 1) < SWA_HEAD_DIM
    xr = pltpu.roll(x, SWA_HEAD_DIM, axis=1)
    return jnp.where(lo, x, xr), jnp.where(lo, xr, x)


def _pairs_attention(qps, sinks, k_dups, vt_dups, masks):
    nq = qps[0].shape[0]
    lo = lax.broadcasted_iota(jnp.int32, (nq, LANES), 1) < SWA_HEAD_DIM
    even = lax.broadcasted_iota(jnp.int32, (1, 2 * nq), 1) < nq
    scores = []
    for qp, k_dup in zip(qps, k_dups):
        q2 = jnp.concatenate([jnp.where(lo, qp, 0.0), jnp.where(lo, 0.0, qp)], axis=0).astype(BF16)
        scores.append(_dot_nt(k_dup, q2))
    probs = []
    for s, (sink_even, sink_odd), mask in zip(scores, sinks, masks):
        if mask is not None:
            s = jnp.where(mask, s, NEG_INF)
        sink = jnp.where(even, sink_even, sink_odd)
        m = jnp.maximum(jnp.max(s, axis=0, keepdims=True), sink)
        p = jnp.exp(s - m)
        denom = jnp.sum(p, axis=0, keepdims=True) + jnp.exp(sink - m)
        probs.append((p.astype(BF16), 1.0 / denom))
    outs = []
    for (p, rdenom), vt_dup in zip(probs, vt_dups):
        o = _dot(vt_dup, p) * rdenom
        outs.append(jnp.concatenate([o[:SWA_HEAD_DIM, :nq], o[SWA_HEAD_DIM:, nq:]], axis=0).T)
    return outs


CTX_BATCH = 4


def _attn_ctx_kernel(sink_ref, q_ref, k_ref, v_ref, o_ref):
    scale = SWA_HEAD_DIM ** -0.5
    pairs = range(SWA_HEADS // 2)
    items = [(bb, pr) for bb in range(q_ref.shape[0]) for pr in pairs]
    kd = [[x.astype(BF16) for x in _dup_groups(k_ref[bb])] for bb in range(q_ref.shape[0])]
    vt = [[x.T.astype(BF16) for x in _dup_groups(v_ref[bb])] for bb in range(q_ref.shape[0])]
    outs = _pairs_attention([q_ref[bb, :, pr * LANES:(pr + 1) * LANES] * scale for bb, pr in items],
                            [(sink_ref[2 * pr], sink_ref[2 * pr + 1]) for _, pr in items],
                            [kd[bb][pr // 2] for bb, pr in items], [vt[bb][pr // 2] for bb, pr in items],
                            [None] * len(items))
    for (bb, pr), out in zip(items, outs):
        o_ref[bb, :, pr * LANES:(pr + 1) * LANES] = out


def _attn_ctx(sink, q, k, v):
    b, t, _ = q.shape
    bmap = lambda i: (i, 0, 0)
    return pl.pallas_call(
        _attn_ctx_kernel,
        out_shape=jax.ShapeDtypeStruct((b, t, SWA_Q), F32),
        grid=(b // CTX_BATCH,),
        in_specs=[pl.BlockSpec(memory_space=pltpu.SMEM),
                  pl.BlockSpec((CTX_BATCH, t, SWA_Q), bmap),
                  pl.BlockSpec((CTX_BATCH, t, SWA_KV), bmap),
                  pl.BlockSpec((CTX_BATCH, t, SWA_KV), bmap)],
        out_specs=pl.BlockSpec((CTX_BATCH, t, SWA_Q), bmap),
        compiler_params=pltpu.CompilerParams(dimension_semantics=("arbitrary",),
                                             vmem_limit_bytes=VMEM_LIMIT),
        name="attn_ctx",
    )(sink, q, k, v)


LAT_BLOCKS = 2


def _rope(x, cos, sin_lo, sin_hi):
    return x * cos + pltpu.roll(x, LANES - 16, axis=1) * sin_lo + pltpu.roll(x, 16, axis=1) * sin_hi


def _attn_lat_kernel(sink_ref, q_ref, k_ref, v_ref, kc_ref, vc_ref, cos_ref, sl_ref, sh_ref,
                     o_ref, kw_ref, vw_ref):
    t = q_ref.shape[0]
    ab = ATTN_BLOCK
    nb = t // ab
    scale = SWA_HEAD_DIM ** -0.5

    k_rot = _dup_groups(_rope(k_ref[...], cos_ref[...], sl_ref[...], sh_ref[...]))
    v_dup = _dup_groups(v_ref[...])
    zeros = jnp.zeros((ab, LANES), BF16)
    for grp in range(SWA_KV_HEADS):
        kw_ref[grp, 0:ab, :] = zeros
        kw_ref[grp, ab:ab + t, :] = k_rot[grp].astype(BF16)
        kw_ref[grp, ab + t:, :] = zeros
        vw_ref[grp, 0] = zeros
        for blk in range(nb):
            vw_ref[grp, blk + 1] = v_dup[grp][blk * ab:(blk + 1) * ab, :].T.astype(BF16)
        vw_ref[grp, nb + 1] = zeros
    kc = [x.astype(BF16) for x in _dup_groups(kc_ref[...])]
    vct = [x.T.astype(BF16) for x in _dup_groups(vc_ref[...])]
    lc = kc_ref.shape[0]

    key = lax.broadcasted_iota(jnp.int32, (lc + 3 * ab, 2 * ab), 0) - lc
    tq = lax.broadcasted_iota(jnp.int32, (lc + 3 * ab, 2 * ab), 1) & (ab - 1)
    band = (key < 0) | (jnp.abs(tq + ab - key) <= ab)

    def block(it, carry):
        pairs = range(SWA_HEADS // 2)
        qps, sinks, k_dups, vt_dups, masks, places = [], [], [], [], [], []
        for u in range(LAT_BLOCKS):
            nq = it * LAT_BLOCKS + u
            row0 = pl.multiple_of(nq * ab, ab)
            s_abs = key + (nq - 1) * ab
            mask = band & ((key < 0) | ((s_abs >= 0) & (s_abs < t)))
            cos = cos_ref[pl.ds(row0, ab), :]
            s_lo = sl_ref[pl.ds(row0, ab), :]
            s_hi = sh_ref[pl.ds(row0, ab), :]
            k_all = [jnp.concatenate([kc[grp], kw_ref[grp, pl.ds(row0, 3 * ab), :]], axis=0)
                     for grp in range(SWA_KV_HEADS)]
            vt_all = [jnp.concatenate([vct[grp], vw_ref[grp, nq], vw_ref[grp, nq + 1], vw_ref[grp, nq + 2]],
                                      axis=1) for grp in range(SWA_KV_HEADS)]
            for pr in pairs:
                qps.append(_rope(q_ref[pl.ds(row0, ab), pr * LANES:(pr + 1) * LANES], cos, s_lo, s_hi) * scale)
                sinks.append((sink_ref[2 * pr], sink_ref[2 * pr + 1]))
                k_dups.append(k_all[pr // 2])
                vt_dups.append(vt_all[pr // 2])
                masks.append(mask)
                places.append((row0, pr))
        outs = _pairs_attention(qps, sinks, k_dups, vt_dups, masks)
        for (row0, pr), out in zip(places, outs):
            o_ref[pl.ds(row0, ab), pr * LANES:(pr + 1) * LANES] = out
        return carry

    lax.fori_loop(0, nb // LAT_BLOCKS, block, 0)


def _attn_lat(sink, q, k, v, kc, vc, cos, sin_lo, sin_hi):
    b, t, _ = q.shape
    lc = kc.shape[1]
    bmap = lambda i: (i, 0, 0)
    full = lambda i: (0, 0)
    return pl.pallas_call(
        _attn_lat_kernel,
        out_shape=jax.ShapeDtypeStruct((b, t, SWA_Q), F32),
        grid=(b,),
        in_specs=[pl.BlockSpec(memory_space=pltpu.SMEM),
                  pl.BlockSpec((None, t, SWA_Q), bmap),
                  pl.BlockSpec((None, t, SWA_KV), bmap),
                  pl.BlockSpec((None, t, SWA_KV), bmap),
                  pl.BlockSpec((None, lc, SWA_KV), bmap),
                  pl.BlockSpec((None, lc, SWA_KV), bmap),
                  pl.BlockSpec((t, LANES), full),
                  pl.BlockSpec((t, LANES), full),
                  pl.BlockSpec((t, LANES), full)],
        out_specs=pl.BlockSpec((None, t, SWA_Q), bmap),
        scratch_shapes=[pltpu.VMEM((SWA_KV_HEADS, t + 2 * ATTN_BLOCK, LANES), BF16),
                        pltpu.VMEM((SWA_KV_HEADS, t // ATTN_BLOCK + 2, LANES, ATTN_BLOCK), BF16)],
        compiler_params=pltpu.CompilerParams(dimension_semantics=("arbitrary",),
                                             vmem_limit_bytes=VMEM_LIMIT),
        name="attn_lat",
    )(sink, q, k, v, kc, vc, cos, sin_lo, sin_hi)


def _rope_tables(t):
    half = SWA_HEAD_DIM // 2
    quarter = half // 2
    pos = jnp.arange(t)
    row = (pos // GRID_W).astype(F32)
    col = (pos % GRID_W).astype(F32)
    inv_freq = ROPE_BASE ** (-jnp.arange(quarter, dtype=F32) / quarter)
    lane = jnp.arange(LANES)
    d = lane % SWA_HEAD_DIM
    freq = inv_freq[d % quarter]
    use_row = (d < half)
    ang = jnp.where(use_row[None, :], row[:, None], col[:, None]) * freq[None, :]
    cos = jnp.cos(ang)
    sin = jnp.sin(ang)
    lower = (d % half) < quarter
    return cos, jnp.where(lower[None, :], -sin, 0.0), jnp.where(lower[None, :], 0.0, sin)


def _route(sel, scores):
    n = sel.shape[1]
    gsz = N_EXPERTS // N_EXPERT_GROUPS

    def first_max(x, idx, size):
        m = jnp.max(x, axis=0, keepdims=True)
        first = jnp.min(jnp.where(x == m, idx, float(size)), axis=0, keepdims=True)
        return m, idx == first

    i8 = lax.broadcasted_iota(jnp.int32, (gsz, n), 0).astype(F32)
    rows = []
    for g in range(N_EXPERT_GROUPS):
        slab = sel[g * gsz:(g + 1) * gsz, :]
        m1, hit = first_max(slab, i8, gsz)
        m2 = jnp.max(jnp.where(hit, NEG_INF, slab), axis=0, keepdims=True)
        rows.append(m1 + m2)
    gscore = jnp.concatenate(rows, axis=0)
    gsel = jnp.zeros((N_EXPERT_GROUPS, n), F32)
    for _ in range(TOPK_GROUPS):
        _, hit = first_max(gscore, i8, N_EXPERT_GROUPS)
        gsel = jnp.where(hit, 1.0, gsel)
        gscore = jnp.where(hit, NEG_INF, gscore)
    emask = jnp.concatenate(
        [jnp.broadcast_to(gsel[g:g + 1, :], (gsz, n)) for g in range(N_EXPERT_GROUPS)], axis=0)
    cand = jnp.where(emask > 0.5, sel, NEG_INF)
    ie = lax.broadcasted_iota(jnp.int32, (N_EXPERTS, n), 0).astype(F32)
    w = jnp.zeros((N_EXPERTS, n), F32)
    chosen = jnp.zeros((N_EXPERTS, n), F32)
    hits = []
    for _ in range(TOP_K):
        _, hit = first_max(cand, ie, N_EXPERTS)
        hits.append(hit)
        w = jnp.where(hit, scores, w)
        chosen = jnp.where(hit, 1.0, chosen)
        cand = jnp.where(hit, NEG_INF, cand)
    gates = w / jnp.sum(w, axis=0, keepdims=True) * ROUTED_SCALE

    s_idx = lax.broadcasted_iota(jnp.int32, (n, n), 0)
    t_idx = lax.broadcasted_iota(jnp.int32, (n, n), 1)
    tile_shift = MOE_TILE.bit_length() - 1
    before = jnp.where((s_idx < t_idx) & ((s_idx >> tile_shift) == (t_idx >> tile_shift)), 1.0, 0.0)
    rank = _dot(chosen.astype(BF16), before.astype(BF16))
    e_row = lax.broadcasted_iota(jnp.int32, (N_EXPERTS, N_EXPERTS), 0)
    e_col = lax.broadcasted_iota(jnp.int32, (N_EXPERTS, N_EXPERTS), 1)
    below = jnp.where(e_col < e_row, 1.0, 0.0).astype(BF16)
    lane_tile = lax.broadcasted_iota(jnp.int32, (1, n), 1) >> tile_shift
    sizes, starts = [], []
    first_row = jnp.zeros((N_EXPERTS, n), F32)
    for ti in range(n // MOE_TILE):
        count = jnp.sum(chosen[:, ti * MOE_TILE:(ti + 1) * MOE_TILE], axis=1, keepdims=True)
        padded = jnp.floor((count + (SORT_ALIGN - 1)) * (1.0 / SORT_ALIGN)) * SORT_ALIGN
        padded = jnp.broadcast_to(padded, (N_EXPERTS, LANES))
        start = _dot(below, padded.astype(BF16))
        first_row = jnp.where(lane_tile == ti, start[:, 0:1], first_row)
        sizes.append(padded)
        starts.append(start)
    row = first_row + rank
    pos = jnp.concatenate([jnp.sum(jnp.where(h, row, 0.0), axis=0, keepdims=True) for h in hits], axis=0)
    wts = jnp.concatenate([jnp.sum(jnp.where(h, gates, 0.0), axis=0, keepdims=True) for h in hits], axis=0)
    return pos, wts, sizes, starts


def _outproj_kernel(gla_ref, att_ref, x_ref, wo_ref, g1_ref, sh_ref, sc_ref, ng_ref, rw_ref, rwh_ref,
                    rb_ref, x1_ref, xm_ref, pos_ref, wts_ref, cnt_ref, start_ref):
    bb, tb, d = x_ref.shape
    tm = bb * tb
    y = (_dot(gla_ref[...].reshape(tm, GLA_V).astype(BF16), wo_ref[0:GLA_V, :])
         + _dot(att_ref[...].reshape(tm, SWA_Q).astype(BF16), wo_ref[GLA_V:, :]))
    x1 = x_ref[...].reshape(tm, d) + g1_ref[...] * y
    x1_ref[...] = x1.reshape(bb, tb, d)
    xm = _rms_norm(x1, ng_ref[...]) * (1.0 + sc_ref[...]) + sh_ref[...]
    xm_hi, xm_lo = _split_hi_lo(xm)
    xm_ref[...] = xm_hi.reshape(bb, tb, d)
    lg = _dot(xm_hi, rw_ref[...])
    logits = lg[:, :N_EXPERTS] + lg[:, N_EXPERTS:] + _dot(xm_lo, rwh_ref[...])
    lt = jnp.concatenate([logits, jnp.zeros((tm, LANES - N_EXPERTS), F32)], axis=1).T[:N_EXPERTS, :]
    scores = _sigmoid(lt)
    pos, wts, sizes, starts = _route(scores + rb_ref[...], scores)
    tiles_per_batch = tb // MOE_TILE
    for ti in range(tm // MOE_TILE):
        at = (ti // tiles_per_batch, ti % tiles_per_batch)
        pos_ref[at] = pos[:, ti * MOE_TILE:(ti + 1) * MOE_TILE]
        wts_ref[at] = wts[:, ti * MOE_TILE:(ti + 1) * MOE_TILE]
        cnt_ref[at] = sizes[ti]
        start_ref[at] = starts[ti]


OUTPROJ_TILE = 1024


def _outproj(gla_out, att_out, x, w_out, g1, sh2, sc2, norm_g, rw_cat, rw_hi, rbias):
    b, t, d = x.shape
    nmod = g1.shape[0]
    tb = min(t, OUTPROJ_TILE)
    bb = OUTPROJ_TILE // tb if nmod == 1 else 1
    tpb = tb // MOE_TILE
    mod_map = (lambda i, j: (i, 0, 0)) if nmod > 1 else (lambda i, j: (0, 0, 0))
    row = lambda i, j: (i, j, 0)
    full = lambda i, j: (0, 0)
    tile = lambda i, j: (i, j, 0, 0)
    nt = t // MOE_TILE
    return pl.pallas_call(
        _outproj_kernel,
        out_shape=(jax.ShapeDtypeStruct((b, t, d), F32),
                   jax.ShapeDtypeStruct((b, t, d), BF16),
                   jax.ShapeDtypeStruct((b, nt, TOP_K, MOE_TILE), F32),
                   jax.ShapeDtypeStruct((b, nt, TOP_K, MOE_TILE), F32),
                   jax.ShapeDtypeStruct((b, nt, N_EXPERTS, LANES), F32),
                   jax.ShapeDtypeStruct((b, nt, N_EXPERTS, LANES), F32)),
        grid=(b // bb, t // tb),
        in_specs=[pl.BlockSpec((bb, tb, GLA_V), row),
                  pl.BlockSpec((bb, tb, SWA_Q), row),
                  pl.BlockSpec((bb, tb, d), row),
                  pl.BlockSpec((d, d), full, pipeline_mode=pl.Buffered(1)),
                  pl.BlockSpec((None, 1, d), mod_map),
                  pl.BlockSpec((None, 1, d), mod_map),
                  pl.BlockSpec((None, 1, d), mod_map),
                  pl.BlockSpec((1, d), full),
                  pl.BlockSpec((d, 2 * N_EXPERTS), full),
                  pl.BlockSpec((d, N_EXPERTS), full),
                  pl.BlockSpec((N_EXPERTS, 1), full)],
        out_specs=(pl.BlockSpec((bb, tb, d), row),
                   pl.BlockSpec((bb, tb, d), row),
                   pl.BlockSpec((bb, tpb, TOP_K, MOE_TILE), tile),
                   pl.BlockSpec((bb, tpb, TOP_K, MOE_TILE), tile),
                   pl.BlockSpec((bb, tpb, N_EXPERTS, LANES), tile),
                   pl.BlockSpec((bb, tpb, N_EXPERTS, LANES), tile)),
        compiler_params=pltpu.CompilerParams(dimension_semantics=("arbitrary", "arbitrary"),
                                             vmem_limit_bytes=VMEM_LIMIT),
        name="outproj",
    )(gla_out, att_out, x, w_out, g1, sh2, sc2, norm_g, rw_cat, rw_hi, rbias)


MOE_TILE = 256
SORT_ALIGN = 16
SORT_ROWS = 3072
ROW_TILE = 512
GATHER_SLOTS = 9
FFN_CHAINS = 4
COMBINE_CHUNK = 1024
ALWAYS_ROWS = 2560
COMBINE_TAIL = 512
COMBINE_SLOTS = 2


def _moe_sort_kernel(tiles_a, used_ref, xa_ref, xb_ref, pos_ref, xs_hbm, ybuf, osem):
    i = pl.program_id(0)
    last = pl.num_programs(0) - 1
    slot = lax.rem(i, 2)
    x = jnp.where(i < tiles_a, xa_ref[...], xb_ref[...])
    pos = pos_ref[...]
    tm = x.shape[0]
    used = used_ref[i]

    def out_copies(tile, of_slot, start):
        def one(r0, r1):
            row = pl.multiple_of(tile * SORT_ROWS + r0, tm)
            cp = pltpu.make_async_copy(ybuf.at[of_slot, r0:r1, :], xs_hbm.at[pl.ds(row, r1 - r0), :],
                                       osem.at[of_slot])
            if start:
                cp.start()
            else:
                cp.wait()

        one(0, ALWAYS_ROWS)
        for r0 in range(ALWAYS_ROWS, SORT_ROWS, tm):
            pl.when(r0 < used_ref[tile])(functools.partial(one, r0, r0 + tm))

    @pl.when(i >= 2)
    def _():
        out_copies(i - 2, slot, False)

    rows = lax.broadcasted_iota(jnp.int32, (tm, tm), 0).astype(F32).astype(BF16)
    one_bf = jnp.ones((tm, tm), BF16)

    def fill(blk):
        local = (pos - float(blk * tm)).astype(BF16)
        onehot = jnp.zeros((tm, tm), BF16)
        for k in range(TOP_K):
            onehot = jnp.where(rows == local[k:k + 1, :], one_bf, onehot)
        ybuf[slot, blk * tm:(blk + 1) * tm, :] = _dot(onehot, x).astype(BF16)

    for blk in range(SORT_ROWS // tm):
        if (blk + 1) * tm <= ALWAYS_ROWS:
            fill(blk)
        else:
            pl.when(blk * tm < used)(functools.partial(fill, blk))
    out_copies(i, slot, True)

    @pl.when(i == last)
    def _():
        out_copies(i, slot, False)

        @pl.when(i >= 1)
        def _():
            out_copies(i - 1, 1 - slot, False)


def _moe_sort(xm_a, xm_b, pos, used):
    d = xm_a.shape[1]
    nt, _, tm = pos.shape
    tiles_a = xm_a.shape[0] // tm
    grid_spec = pltpu.PrefetchScalarGridSpec(
        num_scalar_prefetch=1,
        grid=(nt,),
        in_specs=[pl.BlockSpec((tm, d), lambda i, u: (jnp.minimum(i, tiles_a - 1), 0)),
                  pl.BlockSpec((tm, d), lambda i, u: (jnp.maximum(i - tiles_a, 0), 0)),
                  pl.BlockSpec((None, TOP_K, tm), lambda i, u: (i, 0, 0))],
        out_specs=pl.BlockSpec(memory_space=pl.ANY),
        scratch_shapes=[pltpu.VMEM((2, SORT_ROWS, d), BF16),
                        pltpu.SemaphoreType.DMA((2,))])
    return pl.pallas_call(
        functools.partial(_moe_sort_kernel, tiles_a),
        out_shape=jax.ShapeDtypeStruct((nt * SORT_ROWS, d), BF16),
        grid_spec=grid_spec,
        compiler_params=pltpu.CompilerParams(dimension_semantics=("arbitrary",),
                                             vmem_limit_bytes=VMEM_LIMIT),
        name="moe_sort",
    )(used, xm_a, xm_b, pos)


def _moe_row_tiles(n_tokens):
    rows = n_tokens * TOP_K + (n_tokens // MOE_TILE) * N_EXPERTS * (SORT_ALIGN - 1) + N_EXPERTS * (ROW_TILE - 1)
    return -(-rows // ROW_TILE) + GATHER_SLOTS - 1


PLAN_CHUNK = 1280


def _int_dot_r(a, onehot):
    hi = jnp.floor(a * (1.0 / 256.0))
    return _dot(hi.astype(BF16), onehot) * 256.0 + _dot((a - hi * 256.0).astype(BF16), onehot)


def _int_dot_l(onehot, b):
    hi = jnp.floor(b * (1.0 / 256.0))
    return _dot(onehot, hi.astype(BF16)) * 256.0 + _dot(onehot, (b - hi * 256.0).astype(BF16))


def _moe_plan_kernel(cnt_ref, start_ref, src_ref, first_ref, tiles_ref, nu_ref, back_ref):
    nt, ne = cnt_ref.shape
    gpt = SORT_ROWS // SORT_ALIGN
    gpr = ROW_TILE // SORT_ALIGN
    gc = cnt_ref[...] * (1.0 / SORT_ALIGN)
    ls = start_ref[...] * (1.0 / SORT_ALIGN)

    def transpose(x):
        x = jnp.concatenate([x, jnp.zeros((nt, LANES - ne), F32)], axis=1)
        x = jnp.concatenate([x, jnp.zeros((LANES - nt, LANES), F32)], axis=0)
        return x.T[:ne, :nt]

    def tri(n, keep):
        return jnp.where(keep(lax.broadcasted_iota(jnp.int32, (n, n), 0),
                              lax.broadcasted_iota(jnp.int32, (n, n), 1)), 1.0, 0.0).astype(BF16)

    gc_t = transpose(gc)
    ls_t = transpose(ls)
    tot_c = jnp.broadcast_to(jnp.sum(gc_t, axis=1, keepdims=True), (ne, LANES))
    ptot_c = jnp.floor((tot_c + (gpr - 1)) * (1.0 / gpr)) * gpr
    gend_c = _int_dot_l(tri(ne, lambda r, c: c <= r), ptot_c)
    gstart_c = gend_c - ptot_c
    n_used = gend_c[ne - 1:ne, :] * (1.0 / gpr)
    nu_ref[...] = n_used.astype(jnp.int32)
    tot_r = jnp.sum(gc, axis=0, keepdims=True)
    ptot_r = jnp.floor((tot_r + (gpr - 1)) * (1.0 / gpr)) * gpr
    gstart_r = _int_dot_r(jnp.broadcast_to(ptot_r, (8, ne)), tri(ne, lambda r, c: r < c))
    cumex = _dot(tri(nt, lambda r, c: c < r), gc.astype(BF16))
    cumex_t = _dot(gc_t.astype(BF16), tri(nt, lambda r, c: r < c))
    tile_base = lax.broadcasted_iota(jnp.int32, (nt, ne), 0).astype(F32) * gpt + ls
    table = jnp.concatenate([cumex + gc, cumex, tile_base, gstart_r, jnp.broadcast_to(tot_r, (8, ne))], axis=0)

    e_iota = lax.broadcasted_iota(jnp.int32, (ne, PLAN_CHUNK), 0).astype(F32)
    for ch in range(src_ref.shape[1] // PLAN_CHUNK):
        g = (lax.broadcasted_iota(jnp.int32, (1, PLAN_CHUNK), 1) + ch * PLAN_CHUNK).astype(F32)
        eg = jnp.sum(jnp.where(gend_c[:, 0:1] <= g, 1.0, 0.0), axis=0, keepdims=True)
        picked = _int_dot_r(table, jnp.where(e_iota == eg, 1.0, 0.0).astype(BF16))
        cum_g, cumex_g, base_g = picked[0:nt], picked[nt:2 * nt], picked[2 * nt:3 * nt]
        u = g - picked[3 * nt:3 * nt + 1]
        in_tile = (cumex_g <= u) & (u < cum_g)
        src = jnp.sum(jnp.where(in_tile, base_g - cumex_g, 0.0), axis=0, keepdims=True) + u
        src = jnp.where(u < picked[3 * nt + 8:3 * nt + 9], src, 0.0)
        src_ref[:, ch * PLAN_CHUNK:(ch + 1) * PLAN_CHUNK] = src.astype(jnp.int32)

    first_ref[...] = (gstart_c * (1.0 / gpr)).astype(jnp.int32)
    tiles_ref[...] = (ptot_c * (1.0 / gpr)).astype(jnp.int32)

    lg = lax.broadcasted_iota(jnp.int32, (ne, back_ref.shape[1]), 1).astype(F32)
    for t in range(nt):
        first = ls_t[:, t:t + 1]
        inside = (first <= lg) & (lg < first + gc_t[:, t:t + 1])
        shift = gstart_c[:, 0:1] + cumex_t[:, t:t + 1] - first
        val = jnp.sum(jnp.where(inside, shift + lg, 0.0), axis=0, keepdims=True)
        back_ref[t:t + 1, :] = val.astype(jnp.int32)


def _moe_plan(cnt, start):
    nt, ne = cnt.shape
    row_tiles = _moe_row_tiles(nt * MOE_TILE)
    gpt = SORT_ROWS // SORT_ALIGN
    gpr = ROW_TILE // SORT_ALIGN
    n_src = -(-(row_tiles * gpr) // PLAN_CHUNK) * PLAN_CHUNK
    n_back = -(-gpt // LANES) * LANES
    src, first, tiles, nu, back = pl.pallas_call(
        _moe_plan_kernel,
        out_shape=(jax.ShapeDtypeStruct((1, n_src), jnp.int32),
                   jax.ShapeDtypeStruct((ne, LANES), jnp.int32),
                   jax.ShapeDtypeStruct((ne, LANES), jnp.int32),
                   jax.ShapeDtypeStruct((1, LANES), jnp.int32),
                   jax.ShapeDtypeStruct((nt, n_back), jnp.int32)),
        compiler_params=pltpu.CompilerParams(vmem_limit_bytes=VMEM_LIMIT),
        name="moe_plan",
    )(cnt, start)
    return nu[0, :1], first[:, 0], tiles[:, 0], src[0, :row_tiles * gpr], back[:, :gpt]


def _moe_experts_kernel(nu_ref, first_ref, tiles_ref, src_ref, xs_hbm, wg_ref, wu_ref, wd_ref, ys_hbm,
                        xbuf, ybuf, gsem, osem, wgu_s, wd_s):
    e = pl.program_id(0)
    n_used = nu_ref[0]
    gpr = ROW_TILE // SORT_ALIGN
    part = ROW_TILE // FFN_CHAINS

    def gather(tile, to_slot, j0=0, j1=gpr):
        for j in range(j0, j1):
            row = pl.multiple_of(src_ref[tile * gpr + j] * SORT_ALIGN, SORT_ALIGN)
            pltpu.make_async_copy(xs_hbm.at[pl.ds(row, SORT_ALIGN), :],
                                  xbuf.at[to_slot, j * SORT_ALIGN:(j + 1) * SORT_ALIGN, :],
                                  gsem.at[to_slot]).start(priority=j % 2)

    def drain(of_slot):
        for j in range(gpr):
            pltpu.make_async_copy(xs_hbm.at[0:SORT_ALIGN, :],
                                  xbuf.at[of_slot, j * SORT_ALIGN:(j + 1) * SORT_ALIGN, :], gsem.at[of_slot]).wait()

    def out_copy(tile, of_slot):
        row = pl.multiple_of(tile * ROW_TILE, ROW_TILE)
        return pltpu.make_async_copy(ybuf.at[of_slot], ys_hbm.at[pl.ds(row, ROW_TILE), :], osem.at[of_slot])

    @pl.when(e == 0)
    def _():
        for ahead in range(GATHER_SLOTS - 1):
            gather(ahead, ahead)

    wgu_s[:, :EXPERT_FF] = wg_ref[...].astype(BF16)
    wgu_s[:, EXPERT_FF:] = wu_ref[...].astype(BF16)
    wd_s[...] = wd_ref[...].astype(BF16)

    def row_tile(i, carry):
        r = first_ref[e] + i
        slot = lax.rem(r, GATHER_SLOTS)
        oslot = lax.rem(r, 2)
        next_slot = lax.rem(r + GATHER_SLOTS - 1, GATHER_SLOTS)
        drain(slot)

        @pl.when(r >= 2)
        def _():
            out_copy(r - 2, oslot).wait()

        abs_ = []
        for c in range(FFN_CHAINS):
            abs_.append(_dot(xbuf[slot, c * part:(c + 1) * part, :], wgu_s[...]))
            gather(r + GATHER_SLOTS - 1, next_slot, c * gpr // FFN_CHAINS, (c + 1) * gpr // FFN_CHAINS)
        hs = [(_silu(ab[:, :EXPERT_FF]) * ab[:, EXPERT_FF:]).astype(BF16) for ab in abs_]
        ys = [_dot(h, wd_s[...]).astype(BF16) for h in hs]
        for c in range(FFN_CHAINS):
            ybuf[oslot, c * part:(c + 1) * part, :] = ys[c]
        out_copy(r, oslot).start()
        return carry

    lax.fori_loop(0, tiles_ref[e], row_tile, 0)

    @pl.when(e == pl.num_programs(0) - 1)
    def _():
        for ahead in range(GATHER_SLOTS - 1):
            drain(lax.rem(n_used + ahead, GATHER_SLOTS))
        out_copy(n_used - 1, lax.rem(n_used - 1, 2)).wait()

        @pl.when(n_used >= 2)
        def _():
            out_copy(n_used - 2, lax.rem(n_used, 2)).wait()


def _moe_experts(n_used, first, tiles, src, xs, wg, wu, wd, row_tiles):
    d = xs.shape[-1]
    ne = wg.shape[0]
    w_map = lambda e, nu, fi, ti, sr: (e, 0, 0)
    grid_spec = pltpu.PrefetchScalarGridSpec(
        num_scalar_prefetch=4,
        grid=(ne,),
        in_specs=[pl.BlockSpec(memory_space=pl.ANY),
                  pl.BlockSpec((None, d, EXPERT_FF), w_map),
                  pl.BlockSpec((None, d, EXPERT_FF), w_map),
                  pl.BlockSpec((None, EXPERT_FF, d), w_map)],
        out_specs=pl.BlockSpec(memory_space=pl.ANY),
        scratch_shapes=[pltpu.VMEM((GATHER_SLOTS, ROW_TILE, d), BF16),
                        pltpu.VMEM((2, ROW_TILE, d), BF16),
                        pltpu.SemaphoreType.DMA((GATHER_SLOTS,)),
                        pltpu.SemaphoreType.DMA((2,)),
                        pltpu.VMEM((d, 2 * EXPERT_FF), BF16),
                        pltpu.VMEM((EXPERT_FF, d), BF16)])
    return pl.pallas_call(
        _moe_experts_kernel,
        out_shape=jax.ShapeDtypeStruct((row_tiles * ROW_TILE, d), BF16),
        grid_spec=grid_spec,
        compiler_params=pltpu.CompilerParams(dimension_semantics=("arbitrary",),
                                             vmem_limit_bytes=VMEM_LIMIT),
        name="moe_experts",
    )(n_used, first, tiles, src, xs, wg, wu, wd)


def _moe_combine_kernel(back_ref, used_ref, ys_hbm, pos_ref, wts_ref, xm_ref, x1_ref, g2_ref, fg_ref,
                        swg_ref, swu_ref, swd_ref, o_ref, buf, sem, acc_ref):
    i = pl.program_id(0)
    gpt = SORT_ROWS // SORT_ALIGN
    slot = lax.rem(i, COMBINE_SLOTS)
    ahead = COMBINE_SLOTS - 1
    always = ALWAYS_ROWS
    tail = range(always, SORT_ROWS, COMBINE_TAIL)

    def copies(tile, of_slot, g0, g1, start):
        for g in range(g0, g1):
            row = pl.multiple_of(back_ref[tile * gpt + g] * SORT_ALIGN, SORT_ALIGN) if start else 0
            cp = pltpu.make_async_copy(ys_hbm.at[pl.ds(row, SORT_ALIGN), :],
                                       buf.at[of_slot, g * SORT_ALIGN:(g + 1) * SORT_ALIGN, :], sem.at[of_slot])
            if start:
                cp.start(priority=g % 2)
            else:
                cp.wait()

    def transfer(tile, of_slot, start):
        copies(tile, of_slot, 0, always // SORT_ALIGN, start)
        for c0 in tail:
            pl.when(c0 < used_ref[tile])(functools.partial(
                copies, tile, of_slot, c0 // SORT_ALIGN, (c0 + COMBINE_TAIL) // SORT_ALIGN, start))

    @pl.when(i == 0)
    def _():
        for first in range(min(ahead, buf.shape[0])):
            pl.when(first < pl.num_programs(0))(functools.partial(transfer, first, first, True))

    @pl.when(i + ahead < pl.num_programs(0))
    def _():
        transfer(i + ahead, lax.rem(i + ahead, COMBINE_SLOTS), True)

    x = xm_ref[...]
    tm = x.shape[0]
    pad = jnp.zeros((LANES - TOP_K, tm), F32)
    pos_t = jnp.concatenate([pos_ref[...], pad], axis=0).T
    wts_t = jnp.concatenate([wts_ref[...], pad], axis=0).T
    blk_b, loc_b, wts_b = [], [], []
    for k in range(TOP_K):
        p = jnp.broadcast_to(pos_t[:, k:k + 1], (tm, LANES))
        blk = jnp.floor(p * (1.0 / tm))
        two = lambda v: jnp.concatenate([v.astype(BF16)] * (tm // LANES), axis=1)
        blk_b.append(two(blk))
        loc_b.append(two(p - blk * tm))
        wts_b.append(two(jnp.broadcast_to(wts_t[:, k:k + 1], (tm, LANES))))
    shared = _dot((_silu(_dot(x, swg_ref[...])) * _dot(x, swu_ref[...])).astype(BF16), swd_ref[...])
    transfer(i, slot, False)
    lane = lax.broadcasted_iota(jnp.int32, (tm, tm), 1).astype(F32).astype(BF16)
    zero = jnp.zeros((tm, tm), BF16)
    nowhere = jnp.full((tm, tm), -1.0, BF16)

    def apply(c0, width):
        blocks = []
        for b0 in range(c0, c0 + width, tm):
            comb = zero
            for k in range(TOP_K):
                loc = jnp.where(blk_b[k] == float(b0 // tm), loc_b[k], nowhere)
                comb = jnp.where(lane == loc, wts_b[k], comb)
            blocks.append(comb)
        return _dot(jnp.concatenate(blocks, axis=1), buf[slot, c0:c0 + width, :])

    routed = shared
    for c0 in range(0, always, COMBINE_CHUNK):
        routed = routed + apply(c0, min(COMBINE_CHUNK, always - c0))
    acc_ref[...] = routed
    for c0 in tail:
        @pl.when(c0 < used_ref[i])
        def _(c0=c0):
            acc_ref[...] += apply(c0, COMBINE_TAIL)
    y = x1_ref[...] + g2_ref[...] * acc_ref[...]
    o_ref[...] = _rms_norm(y, fg_ref[...])


def _moe_combine(back, used, ys, pos, wts, xm, x1, g2, final_g, swg, swu, swd, *, tiles_per_mod):
    n, d = xm.shape
    tm = pos.shape[-1]
    nt = n // tm
    gpt = SORT_ROWS // SORT_ALIGN
    row = lambda i, bk, us: (i, 0)
    full = lambda i, bk, us: (0, 0)
    tile = lambda i, bk, us: (i, 0, 0)
    mod_map = lambda i, bk, us: (i // tiles_per_mod, 0, 0)
    grid_spec = pltpu.PrefetchScalarGridSpec(
        num_scalar_prefetch=2,
        grid=(nt,),
        in_specs=[pl.BlockSpec(memory_space=pl.ANY),
                  pl.BlockSpec((None, TOP_K, tm), tile),
                  pl.BlockSpec((None, TOP_K, tm), tile),
                  pl.BlockSpec((tm, d), row),
                  pl.BlockSpec((tm, d), row),
                  pl.BlockSpec((None, 1, d), mod_map),
                  pl.BlockSpec((1, d), full),
                  pl.BlockSpec((d, SHARED_FF), full),
                  pl.BlockSpec((d, SHARED_FF), full),
                  pl.BlockSpec((SHARED_FF, d), full)],
        out_specs=pl.BlockSpec((tm, d), row),
        scratch_shapes=[pltpu.VMEM((COMBINE_SLOTS, SORT_ROWS, d), BF16),
                        pltpu.SemaphoreType.DMA((COMBINE_SLOTS,)),
                        pltpu.VMEM((tm, d), F32)])
    return pl.pallas_call(
        _moe_combine_kernel,
        out_shape=jax.ShapeDtypeStruct((n, d), F32),
        grid_spec=grid_spec,
        compiler_params=pltpu.CompilerParams(dimension_semantics=("arbitrary",),
                                             vmem_limit_bytes=VMEM_LIMIT),
        name="moe_combine",
    )(back, used, ys, pos.reshape(nt, TOP_K, tm), wts.reshape(nt, TOP_K, tm), xm, x1, g2, final_g, swg, swu, swd)


def _mix(x, mods, p, attn_fn, s0=None):
    sh1, sc1, g1, sh2, sc2, _ = mods
    gla_in, lora, q_s, k_s, v_s = _inproj(x, p["norm_attn_g"], sh1, sc1, p["w_gla"], p["w_lora"], p["w_swa"])
    if s0 is None:
        gla_out, s_f, s_b = _gla(gla_in, lora, p["waf"], p["baf"], p["wab"], p["bab"], p["gla_norm_g"])
    else:
        gla_out, s_f, s_b = _gla(gla_in, lora, p["waf"], p["baf"], p["wab"], p["bab"], p["gla_norm_g"],
                                 s0[0], s0[1])
    att_out = attn_fn(q_s, k_s, v_s)
    routed = _outproj(gla_out, att_out, x, p["w_out"], g1, sh2, sc2, p["norm_ffn_g"],
                      p["rw_cat"], p["rw_hi"], p["rbias"])
    return routed, k_s, v_s, s_f, s_b


def _moe(streams, p):
    d = D_MODEL
    (ra, _), (rb, _) = streams
    n_tiles = [r[1].shape[0] * r[1].shape[1] // MOE_TILE for r, _ in streams]
    pos_all = jnp.concatenate([r[2].reshape(-1, TOP_K, MOE_TILE) for r, _ in streams], axis=0)
    cnt_all = jnp.concatenate([r[4][..., 0].reshape(-1, N_EXPERTS) for r, _ in streams], axis=0)
    start_all = jnp.concatenate([r[5][..., 0].reshape(-1, N_EXPERTS) for r, _ in streams], axis=0)
    used = (start_all[:, -1] + cnt_all[:, -1]).astype(jnp.int32)
    xs = _moe_sort(ra[1].reshape(-1, d), rb[1].reshape(-1, d), pos_all, used)
    n_used, first, tiles, src, back = _moe_plan(cnt_all, start_all)
    ys = _moe_experts(n_used, first, tiles, src, xs, p["wg"], p["wu"], p["wd"],
                      _moe_row_tiles(cnt_all.shape[0] * MOE_TILE))
    outs = []
    tile0 = 0
    for ((x1, xm, pos, wts, cnt, start), g2), nt in zip(streams, n_tiles):
        b, t, _ = x1.shape
        tiles_per_mod = (t // MOE_TILE) if g2.shape[0] > 1 else nt
        y = _moe_combine(back[tile0:tile0 + nt].reshape(-1), used[tile0:tile0 + nt], ys, pos, wts,
                         xm.reshape(-1, d), x1.reshape(-1, d), g2, p["final_norm_g"],
                         p["swg"], p["swu"], p["swd"], tiles_per_mod=tiles_per_mod)
        outs.append(y.reshape(b, t, d))
        tile0 += nt
    return outs


def kernel(x_prompt, x_sample, c, cache_swa_k, cache_swa_v, state_gla_fwd, state_gla_bwd, c_ctx, w_ada, b_ada, norm_attn_g, norm_ffn_g, w_in, gla_wa_f, gla_ba_f, gla_wa_b, gla_ba_b, gla_norm_g, swa_sink, w_out, router_w, router_bias, exp_w_gate, exp_w_up, exp_w_down, sh_w_gate, sh_w_up, sh_w_down, final_norm_g):
    l = 0
    d = D_MODEL
    nb_ctx, t_ctx, _ = x_prompt.shape
    nb_lat, t_lat, _ = x_sample.shape

    pad = jnp.zeros((8 - 1 - nb_lat, d), F32)
    cond8 = jnp.concatenate([c_ctx[None, :], c, pad], axis=0)
    mod = _adaln(cond8, w_ada[l], b_ada[l][None, :])
    mods_ctx = [mod[0:1, i * d:(i + 1) * d][:, None, :] for i in range(6)]
    mods_lat = [mod[1:1 + nb_lat, i * d:(i + 1) * d][:, None, :] for i in range(6)]

    zeros_lora = jnp.zeros((GLA_LORA, GLA_QK), F32)
    rw = router_w[l]
    rw_hi = rw.astype(BF16)
    rw_lo = (rw - rw_hi.astype(F32)).astype(BF16)
    w_in_b = w_in[l].astype(BF16)
    p = {
        "norm_attn_g": norm_attn_g[l][None, :],
        "norm_ffn_g": norm_ffn_g[l][None, :],
        "final_norm_g": final_norm_g[None, :],
        "w_gla": w_in_b[:, :2 * GLA_QK + 2 * GLA_V],
        "w_lora": w_in_b[:, 2 * GLA_QK + 2 * GLA_V:2 * GLA_QK + 2 * GLA_V + 2 * GLA_LORA],
        "w_swa": w_in_b[:, 2 * GLA_QK + 2 * GLA_V + 2 * GLA_LORA:],
        "waf": jnp.concatenate([gla_wa_f[l], zeros_lora], axis=0).astype(BF16),
        "wab": jnp.concatenate([zeros_lora, gla_wa_b[l]], axis=0).astype(BF16),
        "baf": gla_ba_f[l][None, :],
        "bab": gla_ba_b[l][None, :],
        "gla_norm_g": gla_norm_g[l][None, :],
        "w_out": w_out[l].astype(BF16),
        "rw_cat": jnp.concatenate([rw_hi, rw_lo], axis=1),
        "rw_hi": rw_hi,
        "rbias": router_bias[l][:, None],
        "wg": exp_w_gate[l], "wu": exp_w_up[l], "wd": exp_w_down[l],
        "swg": sh_w_gate[l].astype(BF16), "swu": sh_w_up[l].astype(BF16),
        "swd": sh_w_down[l].astype(BF16),
    }
    sink = swa_sink[l]

    routed_ctx, k_c, v_c, s_f, s_b = _mix(x_prompt, mods_ctx, p, functools.partial(_attn_ctx, sink))

    cos, sin_lo, sin_hi = _rope_tables(t_lat)
    kc = cache_swa_k[:, l].reshape(nb_lat, -1, SWA_KV)
    vc = cache_swa_v[:, l].reshape(nb_lat, -1, SWA_KV)
    lat_attn = lambda q, k, v: _attn_lat(sink, q, k, v, kc, vc, cos, sin_lo, sin_hi)
    s0 = (state_gla_fwd[:, l].reshape(nb_lat, GLA_QK, GLA_DV),
          state_gla_bwd[:, l].reshape(nb_lat, GLA_QK, GLA_DV))
    routed_lat, _, _, _, _ = _mix(x_sample, mods_lat, p, lat_attn, s0)
    y_prompt, y_sample = _moe([(routed_ctx, mods_ctx[5]), (routed_lat, mods_lat[5])], p)

    new_k = k_c.reshape(nb_ctx, 1, t_ctx, SWA_KV_HEADS, SWA_HEAD_DIM)
    new_v = v_c.reshape(nb_ctx, 1, t_ctx, SWA_KV_HEADS, SWA_HEAD_DIM)
    new_sf = s_f.reshape(nb_ctx, 1, GLA_HEADS, GLA_DK, GLA_DV)
    new_sb = s_b.reshape(nb_ctx, 1, GLA_HEADS, GLA_DK, GLA_DV)
    return (y_prompt, y_sample, new_k, new_v, new_sf, new_sb)
```

```python
import functools

import jax
import jax.numpy as jnp
from jax import lax
from jax.experimental import pallas as pl
from jax.experimental.pallas import tpu as pltpu

F32 = jnp.float32
BF16 = jnp.bfloat16

D_MODEL = 1024
GLA_HEADS = 4
GLA_DK = 64
GLA_DV = 128
GLA_LORA = 16
GLA_GATE_NORM = 16.0
GLA_CHUNK = 64
GLA_QK = GLA_HEADS * GLA_DK
GLA_V = GLA_HEADS * GLA_DV
SWA_HEAD_DIM = 64
SWA_HEADS = 8
SWA_KV_HEADS = 2
SWA_Q = SWA_HEADS * SWA_HEAD_DIM
SWA_KV = SWA_KV_HEADS * SWA_HEAD_DIM
ATTN_BLOCK = 128
GRID_W = 64
ROPE_BASE = 10000.0
N_EXPERTS = 64
TOP_K = 8
N_EXPERT_GROUPS = 8
TOPK_GROUPS = 4
EXPERT_FF = 128
SHARED_FF = 256
ROUTED_SCALE = 2.5
EPS = 1e-6

LANES = 128
VMEM_LIMIT = 56 * 1024 * 1024

NEG_INF = float("-inf")


def _dot(a, b):
    return jnp.dot(a, b, preferred_element_type=F32)


def _dot_nt(a, b):
    return lax.dot_general(a, b, (((1,), (1,)), ((), ())), preferred_element_type=F32)


def _split_hi_lo(x):
    hi = x.astype(BF16)
    lo = (x - hi.astype(F32)).astype(BF16)
    return hi, lo


def _sigmoid(x):
    return 1.0 / (1.0 + jnp.exp(-x))


def _silu(x):
    return x * _sigmoid(x)


def _rms_norm(x, g):
    ms = jnp.mean(x * x, axis=-1, keepdims=True)
    return x * lax.rsqrt(ms + EPS) * g


def _adaln_kernel(c_ref, w_ref, b_ref, o_ref):
    a_hi, a_lo = _split_hi_lo(_silu(c_ref[...]))
    w_hi, w_lo = _split_hi_lo(w_ref[...])
    o_ref[...] = _dot(a_hi, w_hi) + _dot(a_lo, w_hi) + _dot(a_hi, w_lo) + b_ref[...]


def _adaln(cond8, w_ada, b_ada):
    n = w_ada.shape[1]
    tn = 1536
    return pl.pallas_call(
        _adaln_kernel,
        out_shape=jax.ShapeDtypeStruct((8, n), F32),
        grid=(n // tn,),
        in_specs=[pl.BlockSpec((8, D_MODEL), lambda j: (0, 0)),
                  pl.BlockSpec((D_MODEL, tn), lambda j: (0, j)),
                  pl.BlockSpec((1, tn), lambda j: (0, j))],
        out_specs=pl.BlockSpec((8, tn), lambda j: (0, j)),
        compiler_params=pltpu.CompilerParams(dimension_semantics=("arbitrary",),
                                             vmem_limit_bytes=VMEM_LIMIT),
        name="adaln",
    )(cond8, w_ada, b_ada)


def _inproj_kernel(x_ref, g_ref, sh_ref, sc_ref, wg_ref, wl_ref, ws_ref,
                   gla_ref, lora_ref, q_ref, k_ref, v_ref):
    bb, tb, d = x_ref.shape
    x = x_ref[...].reshape(bb * tb, d)
    h = _rms_norm(x, g_ref[...]) * (1.0 + sc_ref[...]) + sh_ref[...]
    hb = h.astype(BF16)
    gla_ref[...] = _dot(hb, wg_ref[...]).reshape(gla_ref.shape)
    lora_ref[...] = _dot(hb, wl_ref[...]).reshape(lora_ref.shape)
    s = _dot(hb, ws_ref[...])
    q_ref[...] = s[:, :SWA_Q].reshape(q_ref.shape)
    k_ref[...] = s[:, SWA_Q:SWA_Q + SWA_KV].reshape(k_ref.shape)
    v_ref[...] = s[:, SWA_Q + SWA_KV:].reshape(v_ref.shape)


INPROJ_TILE = 1024


def _inproj(x, g, sh, sc, w_gla, w_lora, w_swa):
    b, t, d = x.shape
    nmod = sh.shape[0]
    tb = min(t, INPROJ_TILE)
    bb = INPROJ_TILE // tb if nmod == 1 else 1
    mod_map = (lambda i, j: (i, 0, 0)) if nmod > 1 else (lambda i, j: (0, 0, 0))
    row = lambda i, j: (i, j, 0)
    full = lambda i, j: (0, 0)
    n_gla = w_gla.shape[1]
    n_lora = w_lora.shape[1]
    return pl.pallas_call(
        _inproj_kernel,
        out_shape=(jax.ShapeDtypeStruct((b, t, n_gla), F32),
                   jax.ShapeDtypeStruct((b, t, n_lora), F32),
                   jax.ShapeDtypeStruct((b, t, SWA_Q), F32),
                   jax.ShapeDtypeStruct((b, t, SWA_KV), F32),
                   jax.ShapeDtypeStruct((b, t, SWA_KV), F32)),
        grid=(b // bb, t // tb),
        in_specs=[pl.BlockSpec((bb, tb, d), row),
                  pl.BlockSpec((1, d), full),
                  pl.BlockSpec((None, 1, d), mod_map),
                  pl.BlockSpec((None, 1, d), mod_map),
                  pl.BlockSpec((d, n_gla), full, pipeline_mode=pl.Buffered(1)),
                  pl.BlockSpec((d, n_lora), full, pipeline_mode=pl.Buffered(1)),
                  pl.BlockSpec((d, w_swa.shape[1]), full, pipeline_mode=pl.Buffered(1))],
        out_specs=(pl.BlockSpec((bb, tb, n_gla), row),
                   pl.BlockSpec((bb, tb, n_lora), row),
                   pl.BlockSpec((bb, tb, SWA_Q), row),
                   pl.BlockSpec((bb, tb, SWA_KV), row),
                   pl.BlockSpec((bb, tb, SWA_KV), row)),
        compiler_params=pltpu.CompilerParams(dimension_semantics=("arbitrary", "arbitrary"),
                                             vmem_limit_bytes=VMEM_LIMIT),
        name="inproj",
    )(x, g, sh, sc, w_gla, w_lora, w_swa)


SCAN_UNROLL = 4
OUT_UNROLL = 4


def _log_sigmoid(x):
    return jnp.minimum(x, 0.0) - jnp.log(1.0 + jnp.exp(-jnp.abs(x)))


def _heads_to_rows(x):
    return jnp.concatenate([x[:, h * LANES:(h + 1) * LANES] for h in range(GLA_HEADS)], axis=0)


def _rows_to_heads(x, c):
    return jnp.concatenate([x[h * c:(h + 1) * c, :] for h in range(GLA_HEADS)], axis=1)


def _gla_kernel(has_init, q_ref, k_ref, v_ref, g_ref, lora_ref, waf_ref, baf_ref, wab_ref, bab_ref,
                ng_ref, *rest):
    if has_init:
        s0f_ref, s0b_ref, *rest = rest
    (out_ref, sf_ref, sb_ref, laf_ref, lab_ref, oacc_ref, qtf_ref, qtb_ref, saf_ref, sab_ref,
     stf_ref, stb_ref) = rest
    t = q_ref.shape[0]
    c = GLA_CHUNK
    n = t // c
    hc = GLA_HEADS * c

    lora = lora_ref[...].astype(BF16)
    laf_ref[...] = _log_sigmoid(_dot(lora, waf_ref[...]) + baf_ref[...]) * (1.0 / GLA_GATE_NORM)
    lab_ref[...] = _log_sigmoid(_dot(lora, wab_ref[...]) + bab_ref[...]) * (1.0 / GLA_GATE_NORM)

    if has_init:
        stf_ref[...] = s0f_ref[...].T
        stb_ref[...] = s0b_ref[...].T
    else:
        stf_ref[...] = jnp.zeros_like(stf_ref)
        stb_ref[...] = jnp.zeros_like(stb_ref)
    oacc_ref[...] = jnp.zeros_like(oacc_ref)

    r64 = lax.broadcasted_iota(jnp.int32, (c, c), 0)
    c64 = lax.broadcasted_iota(jnp.int32, (c, c), 1)
    tri_f = jnp.where(c64 <= r64, 1.0, 0.0).astype(BF16)
    tri_b = jnp.where(c64 >= r64, 1.0, 0.0).astype(BF16)
    rr = lax.broadcasted_iota(jnp.int32, (hc, hc), 0)
    cc = lax.broadcasted_iota(jnp.int32, (hc, hc), 1)
    same_head = (rr >> 6) == (cc >> 6)
    keep_f = same_head & ((rr & (c - 1)) >= (cc & (c - 1)))
    keep_b = same_head & ((rr & (c - 1)) <= (cc & (c - 1)))
    head_mask = jnp.where(same_head, 1.0, 0.0).astype(BF16)
    norm_g = ng_ref[...]

    def chunk_rows(ci):
        return pl.ds(pl.multiple_of(ci * c, c), c)

    def tile_heads(x):
        x4 = jnp.concatenate([x] * GLA_HEADS, axis=0)
        return jnp.where(same_head, x4, 0.0).astype(BF16)

    def scan_step(i, carry):
        dirs = []
        for u in range(SCAN_UNROLL):
            dirs += [(SCAN_UNROLL * i + u, laf_ref, tri_f, keep_f, c - 1, stf_ref, saf_ref, qtf_ref),
                     (n - 1 - SCAN_UNROLL * i - u, lab_ref, tri_b, keep_b, 0, stb_ref, sab_ref, qtb_ref)]
        cums = []
        for ci, la_ref, tri, _, _, _, _, _ in dirs:
            la_hi, la_lo = _split_hi_lo(la_ref[chunk_rows(ci), :])
            cums.append(_dot(tri, la_hi) + _dot(tri, la_lo))
        ops = []
        for (ci, _, _, _, last_row, _, _, qt_ref), cum in zip(dirs, cums):
            sl = chunk_rows(ci)
            tot = cum[last_row:last_row + 1, :]
            kc = k_ref[sl, :]
            qt = q_ref[sl, :] * (GLA_DK ** -0.5) * jnp.exp(cum)
            qt_ref[sl, :] = qt.astype(BF16)
            v_rows = _heads_to_rows(v_ref[sl, :])
            ops.append((tot, tile_heads(qt), tile_heads(kc * jnp.exp(-cum)),
                        tile_heads(kc * jnp.exp(tot - cum)), v_rows))
        atts = [_dot_nt(q4, k4) for _, q4, k4, _, _ in ops]
        incs = []
        for (_, _, _, keep, _, _, _, _), (_, _, _, kd4, v_rows), att in zip(dirs, ops, atts):
            att = jnp.where(keep, att, 0.0).astype(BF16)
            incs.append((_dot(att, v_rows.astype(BF16)), _dot(v_rows.T.astype(BF16), kd4)))
        for (ci, _, _, _, _, st_ref, snap_ref, _), (tot, _, _, _, _), (o_intra, st_inc) in zip(dirs, ops, incs):
            oacc_ref[ci] += o_intra
            st = st_ref[...]
            snap_ref[ci] = st.astype(BF16)
            st_ref[...] = jnp.exp(tot) * st + st_inc
        return carry

    def tile_heads_bf16(x):
        return jnp.concatenate([x] * GLA_HEADS, axis=0) * head_mask

    def out_step(i, carry):
        chunks = [OUT_UNROLL * i + u for u in range(OUT_UNROLL)]
        inter = []
        for ci in chunks:
            sl = chunk_rows(ci)
            q4 = jnp.concatenate([tile_heads_bf16(qtf_ref[sl, :]), tile_heads_bf16(qtb_ref[sl, :])], axis=1)
            st = jnp.concatenate([saf_ref[ci], sab_ref[ci]], axis=1)
            inter.append(_dot_nt(q4, st))
        for ci, o_inter in zip(chunks, inter):
            sl = chunk_rows(ci)
            on = _rms_norm(oacc_ref[ci] + o_inter, norm_g)
            gate = _silu(_heads_to_rows(g_ref[sl, :]))
            out_ref[sl, :] = _rows_to_heads(on * gate, c)
        return carry

    lax.fori_loop(0, n // SCAN_UNROLL, scan_step, 0)
    lax.fori_loop(0, n // OUT_UNROLL, out_step, 0)
    sf_ref[...] = stf_ref[...].T
    sb_ref[...] = stb_ref[...].T


def _gla(gla_in, lora, waf, baf, wab, bab, norm_g, s0f=None, s0b=None):
    b, t, _ = gla_in.shape
    has_init = s0f is not None
    n = t // GLA_CHUNK
    bmap = lambda i: (i, 0, 0)
    full = lambda i: (0, 0)
    in_specs = [pl.BlockSpec((None, t, GLA_QK), lambda i: (i, 0, 0)),
                pl.BlockSpec((None, t, GLA_QK), lambda i: (i, 0, 1)),
                pl.BlockSpec((None, t, GLA_V), lambda i: (i, 0, 1)),
                pl.BlockSpec((None, t, GLA_V), lambda i: (i, 0, 2)),
                pl.BlockSpec((None, t, 2 * GLA_LORA), bmap),
                pl.BlockSpec((2 * GLA_LORA, GLA_QK), full),
                pl.BlockSpec((1, GLA_QK), full),
                pl.BlockSpec((2 * GLA_LORA, GLA_QK), full),
                pl.BlockSpec((1, GLA_QK), full),
                pl.BlockSpec((1, GLA_DV), full)]
    args = [gla_in, gla_in, gla_in, gla_in, lora, waf, baf, wab, bab, norm_g]
    if has_init:
        in_specs += [pl.BlockSpec((None, GLA_QK, GLA_DV), bmap)] * 2
        args += [s0f, s0b]
    return pl.pallas_call(
        functools.partial(_gla_kernel, has_init),
        out_shape=(jax.ShapeDtypeStruct((b, t, GLA_V), F32),
                   jax.ShapeDtypeStruct((b, GLA_QK, GLA_DV), F32),
                   jax.ShapeDtypeStruct((b, GLA_QK, GLA_DV), F32)),
        grid=(b,),
        in_specs=in_specs,
        out_specs=(pl.BlockSpec((None, t, GLA_V), bmap),
                   pl.BlockSpec((None, GLA_QK, GLA_DV), bmap),
                   pl.BlockSpec((None, GLA_QK, GLA_DV), bmap)),
        scratch_shapes=[pltpu.VMEM((t, GLA_QK), F32),
                        pltpu.VMEM((t, GLA_QK), F32),
                        pltpu.VMEM((n, GLA_HEADS * GLA_CHUNK, GLA_DV), F32),
                        pltpu.VMEM((t, GLA_QK), BF16),
                        pltpu.VMEM((t, GLA_QK), BF16),
                        pltpu.VMEM((n, GLA_DV, GLA_QK), BF16),
                        pltpu.VMEM((n, GLA_DV, GLA_QK), BF16),
                        pltpu.VMEM((GLA_DV, GLA_QK), F32),
                        pltpu.VMEM((GLA_DV, GLA_QK), F32)],
        compiler_params=pltpu.CompilerParams(dimension_semantics=("arbitrary",),
                                             vmem_limit_bytes=VMEM_LIMIT),
        name="gla",
    )(*args)


def _dup_groups(x):
    lo = lax.broadcasted_iota(jnp.int32, x.shape, 1) < SWA_HEAD_DIM
    xr = pltpu.roll(x, SWA_HEAD_DIM, axis=1)
    return jnp.where(lo, x, xr), jnp.where(lo, xr, x)


def _pairs_attention(qps, sinks, k_dups, vt_dups, masks):
    nq = qps[0].shape[0]
    lo = lax.broadcasted_iota(jnp.int32, (nq, LANES), 1) < SWA_HEAD_DIM
    even = lax.broadcasted_iota(jnp.int32, (1, 2 * nq), 1) < nq
    scores = []
    for qp, k_dup in zip(qps, k_dups):
        q2 = jnp.concatenate([jnp.where(lo, qp, 0.0), jnp.where(lo, 0.0, qp)], axis=0).astype(BF16)
        scores.append(_dot_nt(k_dup, q2))
    probs = []
    for s, (sink_even, sink_odd), mask in zip(scores, sinks, masks):
        if mask is not None:
            s = jnp.where(mask, s, NEG_INF)
        sink = jnp.where(even, sink_even, sink_odd)
        m = jnp.maximum(jnp.max(s, axis=0, keepdims=True), sink)
        p = jnp.exp(s - m)
        denom = jnp.sum(p, axis=0, keepdims=True) + jnp.exp(sink - m)
        probs.append((p.astype(BF16), 1.0 / denom))
    outs = []
    for (p, rdenom), vt_dup in zip(probs, vt_dups):
        o = _dot(vt_dup, p) * rdenom
        outs.append(jnp.concatenate([o[:SWA_HEAD_DIM, :nq], o[SWA_HEAD_DIM:, nq:]], axis=0).T)
    return outs


CTX_BATCH = 4


def _attn_ctx_kernel(sink_ref, q_ref, k_ref, v_ref, o_ref):
    scale = SWA_HEAD_DIM ** -0.5
    pairs = range(SWA_HEADS // 2)
    items = [(bb, pr) for bb in range(q_ref.shape[0]) for pr in pairs]
    kd = [[x.astype(BF16) for x in _dup_groups(k_ref[bb])] for bb in range(q_ref.shape[0])]
    vt = [[x.T.astype(BF16) for x in _dup_groups(v_ref[bb])] for bb in range(q_ref.shape[0])]
    outs = _pairs_attention([q_ref[bb, :, pr * LANES:(pr + 1) * LANES] * scale for bb, pr in items],
                            [(sink_ref[2 * pr], sink_ref[2 * pr + 1]) for _, pr in items],
                            [kd[bb][pr // 2] for bb, pr in items], [vt[bb][pr // 2] for bb, pr in items],
                            [None] * len(items))
    for (bb, pr), out in zip(items, outs):
        o_ref[bb, :, pr * LANES:(pr + 1) * LANES] = out


def _attn_ctx(sink, q, k, v):
    b, t, _ = q.shape
    bmap = lambda i: (i, 0, 0)
    return pl.pallas_call(
        _attn_ctx_kernel,
        out_shape=jax.ShapeDtypeStruct((b, t, SWA_Q), F32),
        grid=(b // CTX_BATCH,),
        in_specs=[pl.BlockSpec(memory_space=pltpu.SMEM),
                  pl.BlockSpec((CTX_BATCH, t, SWA_Q), bmap),
                  pl.BlockSpec((CTX_BATCH, t, SWA_KV), bmap),
                  pl.BlockSpec((CTX_BATCH, t, SWA_KV), bmap)],
        out_specs=pl.BlockSpec((CTX_BATCH, t, SWA_Q), bmap),
        compiler_params=pltpu.CompilerParams(dimension_semantics=("arbitrary",),
                                             vmem_limit_bytes=VMEM_LIMIT),
        name="attn_ctx",
    )(sink, q, k, v)


LAT_BLOCKS = 2


def _rope(x, cos, sin_lo, sin_hi):
    return x * cos + pltpu.roll(x, LANES - 16, axis=1) * sin_lo + pltpu.roll(x, 16, axis=1) * sin_hi


def _attn_lat_kernel(sink_ref, q_ref, k_ref, v_ref, kc_ref, vc_ref, cos_ref, sl_ref, sh_ref,
                     o_ref, kw_ref, vw_ref):
    t = q_ref.shape[0]
    ab = ATTN_BLOCK
    nb = t // ab
    scale = SWA_HEAD_DIM ** -0.5

    k_rot = _dup_groups(_rope(k_ref[...], cos_ref[...], sl_ref[...], sh_ref[...]))
    v_dup = _dup_groups(v_ref[...])
    zeros = jnp.zeros((ab, LANES), BF16)
    for grp in range(SWA_KV_HEADS):
        kw_ref[grp, 0:ab, :] = zeros
        kw_ref[grp, ab:ab + t, :] = k_rot[grp].astype(BF16)
        kw_ref[grp, ab + t:, :] = zeros
        vw_ref[grp, 0] = zeros
        for blk in range(nb):
            vw_ref[grp, blk + 1] = v_dup[grp][blk * ab:(blk + 1) * ab, :].T.astype(BF16)
        vw_ref[grp, nb + 1] = zeros
    kc = [x.astype(BF16) for x in _dup_groups(kc_ref[...])]
    vct = [x.T.astype(BF16) for x in _dup_groups(vc_ref[...])]
    lc = kc_ref.shape[0]

    key = lax.broadcasted_iota(jnp.int32, (lc + 3 * ab, 2 * ab), 0) - lc
    tq = lax.broadcasted_iota(jnp.int32, (lc + 3 * ab, 2 * ab), 1) & (ab - 1)
    band = (key < 0) | (jnp.abs(tq + ab - key) <= ab)

    def block(it, carry):
        pairs = range(SWA_HEADS // 2)
        qps, sinks, k_dups, vt_dups, masks, places = [], [], [], [], [], []
        for u in range(LAT_BLOCKS):
            nq = it * LAT_BLOCKS + u
            row0 = pl.multiple_of(nq * ab, ab)
            s_abs = key + (nq - 1) * ab
            mask = band & ((key < 0) | ((s_abs >= 0) & (s_abs < t)))
            cos = cos_ref[pl.ds(row0, ab), :]
            s_lo = sl_ref[pl.ds(row0, ab), :]
            s_hi = sh_ref[pl.ds(row0, ab), :]
            k_all = [jnp.concatenate([kc[grp], kw_ref[grp, pl.ds(row0, 3 * ab), :]], axis=0)
                     for grp in range(SWA_KV_HEADS)]
            vt_all = [jnp.concatenate([vct[grp], vw_ref[grp, nq], vw_ref[grp, nq + 1], vw_ref[grp, nq + 2]],
                                      axis=1) for grp in range(SWA_KV_HEADS)]
            for pr in pairs:
                qps.append(_rope(q_ref[pl.ds(row0, ab), pr * LANES:(pr + 1) * LANES], cos, s_lo, s_hi) * scale)
                sinks.append((sink_ref[2 * pr], sink_ref[2 * pr + 1]))
                k_dups.append(k_all[pr // 2])
                vt_dups.append(vt_all[pr // 2])
                masks.append(mask)
                places.append((row0, pr))
        outs = _pairs_attention(qps, sinks, k_dups, vt_dups, masks)
        for (row0, pr), out in zip(places, outs):
            o_ref[pl.ds(row0, ab), pr * LANES:(pr + 1) * LANES] = out
        return carry

    lax.fori_loop(0, nb // LAT_BLOCKS, block, 0)


def _attn_lat(sink, q, k, v, kc, vc, cos, sin_lo, sin_hi):
    b, t, _ = q.shape
    lc = kc.shape[1]
    bmap = lambda i: (i, 0, 0)
    full = lambda i: (0, 0)
    return pl.pallas_call(
        _attn_lat_kernel,
        out_shape=jax.ShapeDtypeStruct((b, t, SWA_Q), F32),
        grid=(b,),
        in_specs=[pl.BlockSpec(memory_space=pltpu.SMEM),
                  pl.BlockSpec((None, t, SWA_Q), bmap),
                  pl.BlockSpec((None, t, SWA_KV), bmap),
                  pl.BlockSpec((None, t, SWA_KV), bmap),
                  pl.BlockSpec((None, lc, SWA_KV), bmap),
                  pl.BlockSpec((None, lc, SWA_KV), bmap),
                  pl.BlockSpec((t, LANES), full),
                  pl.BlockSpec((t, LANES), full),
                  pl.BlockSpec((t, LANES), full)],
        out_specs=pl.BlockSpec((None, t, SWA_Q), bmap),
        scratch_shapes=[pltpu.VMEM((SWA_KV_HEADS, t + 2 * ATTN_BLOCK, LANES), BF16),
                        pltpu.VMEM((SWA_KV_HEADS, t // ATTN_BLOCK + 2, LANES, ATTN_BLOCK), BF16)],
        compiler_params=pltpu.CompilerParams(dimension_semantics=("arbitrary",),
                                             vmem_limit_bytes=VMEM_LIMIT),
        name="attn_lat",
    )(sink, q, k, v, kc, vc, cos, sin_lo, sin_hi)


def _rope_tables(t):
    half = SWA_HEAD_DIM // 2
    quarter = half // 2
    rows = t // GRID_W
    inv_freq = ROPE_BASE ** (-jnp.arange(quarter, dtype=F32) / quarter)
    reps = LANES // quarter
    ang_row = jnp.tile(jnp.arange(rows).astype(F32)[:, None] * inv_freq[None, :], (1, reps))
    ang_col = jnp.tile(jnp.arange(GRID_W).astype(F32)[:, None] * inv_freq[None, :], (1, reps))
    d = jnp.arange(LANES) % SWA_HEAD_DIM
    use_row = (d < half)[None, :]
    lower = ((d % half) < quarter)[None, :]

    def expand(f):
        by_row = jnp.repeat(f(ang_row), GRID_W, axis=0)
        by_col = jnp.tile(f(ang_col), (rows, 1))
        return jnp.where(use_row, by_row, by_col)

    cos = expand(jnp.cos)
    sin = expand(jnp.sin)
    return cos, jnp.where(lower, -sin, 0.0), jnp.where(lower, 0.0, sin)


def _route(sel, scores):
    n = sel.shape[1]
    gsz = N_EXPERTS // N_EXPERT_GROUPS

    def first_max(x, idx, size):
        m = jnp.max(x, axis=0, keepdims=True)
        first = jnp.min(jnp.where(x == m, idx, float(size)), axis=0, keepdims=True)
        return m, idx == first

    i8 = lax.broadcasted_iota(jnp.int32, (gsz, n), 0).astype(F32)
    rows = []
    for g in range(N_EXPERT_GROUPS):
        slab = sel[g * gsz:(g + 1) * gsz, :]
        m1, hit = first_max(slab, i8, gsz)
        m2 = jnp.max(jnp.where(hit, NEG_INF, slab), axis=0, keepdims=True)
        rows.append(m1 + m2)
    gscore = jnp.concatenate(rows, axis=0)
    gsel = jnp.zeros((N_EXPERT_GROUPS, n), F32)
    for _ in range(TOPK_GROUPS):
        _, hit = first_max(gscore, i8, N_EXPERT_GROUPS)
        gsel = jnp.where(hit, 1.0, gsel)
        gscore = jnp.where(hit, NEG_INF, gscore)
    emask = jnp.concatenate(
        [jnp.broadcast_to(gsel[g:g + 1, :], (gsz, n)) for g in range(N_EXPERT_GROUPS)], axis=0)
    cand = jnp.where(emask > 0.5, sel, NEG_INF)
    ie = lax.broadcasted_iota(jnp.int32, (N_EXPERTS, n), 0).astype(F32)
    w = jnp.zeros((N_EXPERTS, n), F32)
    chosen = jnp.zeros((N_EXPERTS, n), F32)
    hits = []
    for _ in range(TOP_K):
        _, hit = first_max(cand, ie, N_EXPERTS)
        hits.append(hit)
        w = jnp.where(hit, scores, w)
        chosen = jnp.where(hit, 1.0, chosen)
        cand = jnp.where(hit, NEG_INF, cand)
    gates = w / jnp.sum(w, axis=0, keepdims=True) * ROUTED_SCALE

    s_idx = lax.broadcasted_iota(jnp.int32, (n, n), 0)
    t_idx = lax.broadcasted_iota(jnp.int32, (n, n), 1)
    tile_shift = MOE_TILE.bit_length() - 1
    before = jnp.where((s_idx < t_idx) & ((s_idx >> tile_shift) == (t_idx >> tile_shift)), 1.0, 0.0)
    rank = _dot(chosen.astype(BF16), before.astype(BF16))
    e_row = lax.broadcasted_iota(jnp.int32, (N_EXPERTS, N_EXPERTS), 0)
    e_col = lax.broadcasted_iota(jnp.int32, (N_EXPERTS, N_EXPERTS), 1)
    below = jnp.where(e_col < e_row, 1.0, 0.0).astype(BF16)
    lane_tile = lax.broadcasted_iota(jnp.int32, (1, n), 1) >> tile_shift

    def as_row(col):
        return jnp.concatenate([col, jnp.zeros((LANES - N_EXPERTS, LANES), F32)], axis=0).T[0:1, :]

    sizes, starts = [], []
    first_row = jnp.zeros((N_EXPERTS, n), F32)
    for ti in range(n // MOE_TILE):
        count = jnp.sum(chosen[:, ti * MOE_TILE:(ti + 1) * MOE_TILE], axis=1, keepdims=True)
        padded = jnp.floor((count + (SORT_ALIGN - 1)) * (1.0 / SORT_ALIGN)) * SORT_ALIGN
        padded = jnp.broadcast_to(padded, (N_EXPERTS, LANES))
        start = _dot(below, padded.astype(BF16))
        first_row = jnp.where(lane_tile == ti, start[:, 0:1], first_row)
        sizes.append(as_row(padded))
        starts.append(as_row(start))
    row = first_row + rank
    pos = jnp.concatenate([jnp.sum(jnp.where(h, row, 0.0), axis=0, keepdims=True) for h in hits], axis=0)
    wts = jnp.concatenate([jnp.sum(jnp.where(h, gates, 0.0), axis=0, keepdims=True) for h in hits], axis=0)
    return pos, wts, sizes, starts


def _outproj_kernel(gla_ref, att_ref, x_ref, wo_ref, g1_ref, sh_ref, sc_ref, ng_ref, rw_ref, rwh_ref,
                    rb_ref, x1_ref, xm_ref, pos_ref, wts_ref, cnt_ref, start_ref):
    bb, tb, d = x_ref.shape
    tm = bb * tb
    y = (_dot(gla_ref[...].reshape(tm, GLA_V).astype(BF16), wo_ref[0:GLA_V, :])
         + _dot(att_ref[...].reshape(tm, SWA_Q).astype(BF16), wo_ref[GLA_V:, :]))
    x1 = x_ref[...].reshape(tm, d) + g1_ref[...] * y
    x1_ref[...] = x1.reshape(bb, tb, d)
    xm = _rms_norm(x1, ng_ref[...]) * (1.0 + sc_ref[...]) + sh_ref[...]
    xm_hi, xm_lo = _split_hi_lo(xm)
    xm_ref[...] = xm_hi.reshape(bb, tb, d)
    lg = _dot(xm_hi, rw_ref[...])
    logits = lg[:, :N_EXPERTS] + lg[:, N_EXPERTS:] + _dot(xm_lo, rwh_ref[...])
    lt = jnp.concatenate([logits, jnp.zeros((tm, LANES - N_EXPERTS), F32)], axis=1).T[:N_EXPERTS, :]
    scores = _sigmoid(lt)
    pos, wts, sizes, starts = _route(scores + rb_ref[...], scores)
    tiles_per_batch = tb // MOE_TILE
    for ti in range(tm // MOE_TILE):
        at = (ti // tiles_per_batch, ti % tiles_per_batch)
        pos_ref[at] = pos[:, ti * MOE_TILE:(ti + 1) * MOE_TILE]
        wts_ref[at] = wts[:, ti * MOE_TILE:(ti + 1) * MOE_TILE]
        cnt_ref[at] = sizes[ti]
        start_ref[at] = starts[ti]


OUTPROJ_TILE = 1024


def _outproj(gla_out, att_out, x, w_out, g1, sh2, sc2, norm_g, rw_cat, rw_hi, rbias):
    b, t, d = x.shape
    nmod = g1.shape[0]
    tb = min(t, OUTPROJ_TILE)
    bb = OUTPROJ_TILE // tb if nmod == 1 else 1
    tpb = tb // MOE_TILE
    mod_map = (lambda i, j: (i, 0, 0)) if nmod > 1 else (lambda i, j: (0, 0, 0))
    row = lambda i, j: (i, j, 0)
    full = lambda i, j: (0, 0)
    tile = lambda i, j: (i, j, 0, 0)
    nt = t // MOE_TILE
    return pl.pallas_call(
        _outproj_kernel,
        out_shape=(jax.ShapeDtypeStruct((b, t, d), F32),
                   jax.ShapeDtypeStruct((b, t, d), BF16),
                   jax.ShapeDtypeStruct((b, nt, TOP_K, MOE_TILE), F32),
                   jax.ShapeDtypeStruct((b, nt, TOP_K, MOE_TILE), F32),
                   jax.ShapeDtypeStruct((b, nt, 1, LANES), F32),
                   jax.ShapeDtypeStruct((b, nt, 1, LANES), F32)),
        grid=(b // bb, t // tb),
        in_specs=[pl.BlockSpec((bb, tb, GLA_V), row),
                  pl.BlockSpec((bb, tb, SWA_Q), row),
                  pl.BlockSpec((bb, tb, d), row),
                  pl.BlockSpec((d, d), full, pipeline_mode=pl.Buffered(1)),
                  pl.BlockSpec((None, 1, d), mod_map),
                  pl.BlockSpec((None, 1, d), mod_map),
                  pl.BlockSpec((None, 1, d), mod_map),
                  pl.BlockSpec((1, d), full),
                  pl.BlockSpec((d, 2 * N_EXPERTS), full),
                  pl.BlockSpec((d, N_EXPERTS), full),
                  pl.BlockSpec((N_EXPERTS, 1), full)],
        out_specs=(pl.BlockSpec((bb, tb, d), row),
                   pl.BlockSpec((bb, tb, d), row),
                   pl.BlockSpec((bb, tpb, TOP_K, MOE_TILE), tile),
                   pl.BlockSpec((bb, tpb, TOP_K, MOE_TILE), tile),
                   pl.BlockSpec((bb, tpb, 1, LANES), tile),
                   pl.BlockSpec((bb, tpb, 1, LANES), tile)),
        compiler_params=pltpu.CompilerParams(dimension_semantics=("arbitrary", "arbitrary"),
                                             vmem_limit_bytes=VMEM_LIMIT),
        name="outproj",
    )(gla_out, att_out, x, w_out, g1, sh2, sc2, norm_g, rw_cat, rw_hi, rbias)


MOE_TILE = 256
SORT_ALIGN = 16
SORT_ROWS = 3072
ROW_TILE = 512
GATHER_SLOTS = 9
FFN_CHAINS = 4
COMBINE_CHUNK = 1024
ALWAYS_ROWS = 2560
COMBINE_TAIL = 512
COMBINE_SLOTS = 2


def _moe_sort_kernel(tiles_a, used_ref, xa_ref, xb_ref, pos_ref, xs_hbm, ybuf, osem):
    i = pl.program_id(0)
    last = pl.num_programs(0) - 1
    slot = lax.rem(i, 2)
    x = jnp.where(i < tiles_a, xa_ref[...], xb_ref[...])
    pos = pos_ref[...]
    tm = x.shape[0]
    used = used_ref[i]

    def out_copies(tile, of_slot, start):
        def one(r0, r1):
            row = pl.multiple_of(tile * SORT_ROWS + r0, tm)
            cp = pltpu.make_async_copy(ybuf.at[of_slot, r0:r1, :], xs_hbm.at[pl.ds(row, r1 - r0), :],
                                       osem.at[of_slot])
            if start:
                cp.start()
            else:
                cp.wait()

        one(0, ALWAYS_ROWS)
        for r0 in range(ALWAYS_ROWS, SORT_ROWS, tm):
            pl.when(r0 < used_ref[tile])(functools.partial(one, r0, r0 + tm))

    @pl.when(i >= 2)
    def _():
        out_copies(i - 2, slot, False)

    rows = lax.broadcasted_iota(jnp.int32, (tm, tm), 0).astype(F32).astype(BF16)
    one_bf = jnp.ones((tm, tm), BF16)

    def fill(blk):
        local = (pos - float(blk * tm)).astype(BF16)
        onehot = jnp.zeros((tm, tm), BF16)
        for k in range(TOP_K):
            onehot = jnp.where(rows == local[k:k + 1, :], one_bf, onehot)
        ybuf[slot, blk * tm:(blk + 1) * tm, :] = _dot(onehot, x).astype(BF16)

    for blk in range(SORT_ROWS // tm):
        if (blk + 1) * tm <= ALWAYS_ROWS:
            fill(blk)
        else:
            pl.when(blk * tm < used)(functools.partial(fill, blk))
    out_copies(i, slot, True)

    @pl.when(i == last)
    def _():
        out_copies(i, slot, False)

        @pl.when(i >= 1)
        def _():
            out_copies(i - 1, 1 - slot, False)


def _moe_sort(xm_a, xm_b, pos, used):
    d = xm_a.shape[1]
    nt, _, tm = pos.shape
    tiles_a = xm_a.shape[0] // tm
    grid_spec = pltpu.PrefetchScalarGridSpec(
        num_scalar_prefetch=1,
        grid=(nt,),
        in_specs=[pl.BlockSpec((tm, d), lambda i, u: (jnp.minimum(i, tiles_a - 1), 0)),
                  pl.BlockSpec((tm, d), lambda i, u: (jnp.maximum(i - tiles_a, 0), 0)),
                  pl.BlockSpec((None, TOP_K, tm), lambda i, u: (i, 0, 0))],
        out_specs=pl.BlockSpec(memory_space=pl.ANY),
        scratch_shapes=[pltpu.VMEM((2, SORT_ROWS, d), BF16),
                        pltpu.SemaphoreType.DMA((2,))])
    return pl.pallas_call(
        functools.partial(_moe_sort_kernel, tiles_a),
        out_shape=jax.ShapeDtypeStruct((nt * SORT_ROWS, d), BF16),
        grid_spec=grid_spec,
        compiler_params=pltpu.CompilerParams(dimension_semantics=("arbitrary",),
                                             vmem_limit_bytes=VMEM_LIMIT),
        name="moe_sort",
    )(used, xm_a, xm_b, pos)


def _moe_row_tiles(n_tokens):
    rows = n_tokens * TOP_K + (n_tokens // MOE_TILE) * N_EXPERTS * (SORT_ALIGN - 1) + N_EXPERTS * (ROW_TILE - 1)
    return -(-rows // ROW_TILE) + GATHER_SLOTS - 1


PLAN_CHUNK = 1280


def _int_dot_r(a, onehot):
    hi = jnp.floor(a * (1.0 / 256.0))
    return _dot(hi.astype(BF16), onehot) * 256.0 + _dot((a - hi * 256.0).astype(BF16), onehot)


def _int_dot_l(onehot, b):
    hi = jnp.floor(b * (1.0 / 256.0))
    return _dot(onehot, hi.astype(BF16)) * 256.0 + _dot(onehot, (b - hi * 256.0).astype(BF16))


def _moe_plan_kernel(cnt_ref, start_ref, src_ref, first_ref, tiles_ref, nu_ref, back_ref):
    nt, ne = cnt_ref.shape
    gpt = SORT_ROWS // SORT_ALIGN
    gpr = ROW_TILE // SORT_ALIGN
    gc = cnt_ref[...] * (1.0 / SORT_ALIGN)
    ls = start_ref[...] * (1.0 / SORT_ALIGN)

    def transpose(x):
        x = jnp.concatenate([x, jnp.zeros((nt, LANES - ne), F32)], axis=1)
        x = jnp.concatenate([x, jnp.zeros((LANES - nt, LANES), F32)], axis=0)
        return x.T[:ne, :nt]

    def tri(n, keep):
        return jnp.where(keep(lax.broadcasted_iota(jnp.int32, (n, n), 0),
                              lax.broadcasted_iota(jnp.int32, (n, n), 1)), 1.0, 0.0).astype(BF16)

    gc_t = transpose(gc)
    ls_t = transpose(ls)
    tot_c = jnp.broadcast_to(jnp.sum(gc_t, axis=1, keepdims=True), (ne, LANES))
    ptot_c = jnp.floor((tot_c + (gpr - 1)) * (1.0 / gpr)) * gpr
    gend_c = _int_dot_l(tri(ne, lambda r, c: c <= r), ptot_c)
    gstart_c = gend_c - ptot_c
    n_used = gend_c[ne - 1:ne, :] * (1.0 / gpr)
    nu_ref[...] = n_used.astype(jnp.int32)
    tot_r = jnp.sum(gc, axis=0, keepdims=True)
    ptot_r = jnp.floor((tot_r + (gpr - 1)) * (1.0 / gpr)) * gpr
    gstart_r = _int_dot_r(jnp.broadcast_to(ptot_r, (8, ne)), tri(ne, lambda r, c: r < c))
    cumex = _dot(tri(nt, lambda r, c: c < r), gc.astype(BF16))
    cumex_t = _dot(gc_t.astype(BF16), tri(nt, lambda r, c: r < c))
    tile_base = lax.broadcasted_iota(jnp.int32, (nt, ne), 0).astype(F32) * gpt + ls
    table = jnp.concatenate([cumex + gc, cumex, tile_base, gstart_r, jnp.broadcast_to(tot_r, (8, ne))], axis=0)

    e_iota = lax.broadcasted_iota(jnp.int32, (ne, PLAN_CHUNK), 0).astype(F32)
    for ch in range(src_ref.shape[1] // PLAN_CHUNK):
        g = (lax.broadcasted_iota(jnp.int32, (1, PLAN_CHUNK), 1) + ch * PLAN_CHUNK).astype(F32)
        eg = jnp.sum(jnp.where(gend_c[:, 0:1] <= g, 1.0, 0.0), axis=0, keepdims=True)
        picked = _int_dot_r(table, jnp.where(e_iota == eg, 1.0, 0.0).astype(BF16))
        cum_g, cumex_g, base_g = picked[0:nt], picked[nt:2 * nt], picked[2 * nt:3 * nt]
        u = g - picked[3 * nt:3 * nt + 1]
        in_tile = (cumex_g <= u) & (u < cum_g)
        src = jnp.sum(jnp.where(in_tile, base_g - cumex_g, 0.0), axis=0, keepdims=True) + u
        src = jnp.where(u < picked[3 * nt + 8:3 * nt + 9], src, 0.0)
        src_ref[:, ch * PLAN_CHUNK:(ch + 1) * PLAN_CHUNK] = src.astype(jnp.int32)

    first_ref[...] = (gstart_c * (1.0 / gpr)).astype(jnp.int32)
    tiles_ref[...] = (ptot_c * (1.0 / gpr)).astype(jnp.int32)

    lg = lax.broadcasted_iota(jnp.int32, (ne, back_ref.shape[1]), 1).astype(F32)
    for t in range(nt):
        first = ls_t[:, t:t + 1]
        inside = (first <= lg) & (lg < first + gc_t[:, t:t + 1])
        shift = gstart_c[:, 0:1] + cumex_t[:, t:t + 1] - first
        val = jnp.sum(jnp.where(inside, shift + lg, 0.0), axis=0, keepdims=True)
        back_ref[t:t + 1, :] = val.astype(jnp.int32)


def _moe_plan(cnt, start):
    nt, ne = cnt.shape
    row_tiles = _moe_row_tiles(nt * MOE_TILE)
    gpt = SORT_ROWS // SORT_ALIGN
    gpr = ROW_TILE // SORT_ALIGN
    n_src = -(-(row_tiles * gpr) // PLAN_CHUNK) * PLAN_CHUNK
    n_back = -(-gpt // LANES) * LANES
    src, first, tiles, nu, back = pl.pallas_call(
        _moe_plan_kernel,
        out_shape=(jax.ShapeDtypeStruct((1, n_src), jnp.int32),
                   jax.ShapeDtypeStruct((ne, LANES), jnp.int32),
                   jax.ShapeDtypeStruct((ne, LANES), jnp.int32),
                   jax.ShapeDtypeStruct((1, LANES), jnp.int32),
                   jax.ShapeDtypeStruct((nt, n_back), jnp.int32)),
        compiler_params=pltpu.CompilerParams(vmem_limit_bytes=VMEM_LIMIT),
        name="moe_plan",
    )(cnt, start)
    return nu[0, :1], first[:, 0], tiles[:, 0], src[0, :row_tiles * gpr], back[:, :gpt]


def _moe_experts_kernel(nu_ref, first_ref, tiles_ref, src_ref, xs_hbm, wg_ref, wu_ref, wd_ref, ys_hbm,
                        xbuf, ybuf, gsem, osem, wgu_s, wd_s):
    e = pl.program_id(0)
    n_used = nu_ref[0]
    gpr = ROW_TILE // SORT_ALIGN
    part = ROW_TILE // FFN_CHAINS

    def gather(tile, to_slot, j0=0, j1=gpr):
        for j in range(j0, j1):
            row = pl.multiple_of(src_ref[tile * gpr + j] * SORT_ALIGN, SORT_ALIGN)
            pltpu.make_async_copy(xs_hbm.at[pl.ds(row, SORT_ALIGN), :],
                                  xbuf.at[to_slot, j * SORT_ALIGN:(j + 1) * SORT_ALIGN, :],
                                  gsem.at[to_slot]).start(priority=j % 2)

    def drain(of_slot):
        for j in range(gpr):
            pltpu.make_async_copy(xs_hbm.at[0:SORT_ALIGN, :],
                                  xbuf.at[of_slot, j * SORT_ALIGN:(j + 1) * SORT_ALIGN, :], gsem.at[of_slot]).wait()

    def out_copy(tile, of_slot):
        row = pl.multiple_of(tile * ROW_TILE, ROW_TILE)
        return pltpu.make_async_copy(ybuf.at[of_slot], ys_hbm.at[pl.ds(row, ROW_TILE), :], osem.at[of_slot])

    @pl.when(e == 0)
    def _():
        for ahead in range(GATHER_SLOTS - 1):
            gather(ahead, ahead)

    wgu_s[:, :EXPERT_FF] = wg_ref[...].astype(BF16)
    wgu_s[:, EXPERT_FF:] = wu_ref[...].astype(BF16)
    wd_s[...] = wd_ref[...].astype(BF16)

    def row_tile(i, carry):
        r = first_ref[e] + i
        slot = lax.rem(r, GATHER_SLOTS)
        oslot = lax.rem(r, 2)
        next_slot = lax.rem(r + GATHER_SLOTS - 1, GATHER_SLOTS)
        drain(slot)

        @pl.when(r >= 2)
        def _():
            out_copy(r - 2, oslot).wait()

        abs_ = []
        for c in range(FFN_CHAINS):
            abs_.append(_dot(xbuf[slot, c * part:(c + 1) * part, :], wgu_s[...]))
            gather(r + GATHER_SLOTS - 1, next_slot, c * gpr // FFN_CHAINS, (c + 1) * gpr // FFN_CHAINS)
        hs = [(_silu(ab[:, :EXPERT_FF]) * ab[:, EXPERT_FF:]).astype(BF16) for ab in abs_]
        ys = [_dot(h, wd_s[...]).astype(BF16) for h in hs]
        for c in range(FFN_CHAINS):
            ybuf[oslot, c * part:(c + 1) * part, :] = ys[c]
        out_copy(r, oslot).start()
        return carry

    lax.fori_loop(0, tiles_ref[e], row_tile, 0)

    @pl.when(e == pl.num_programs(0) - 1)
    def _():
        for ahead in range(GATHER_SLOTS - 1):
            drain(lax.rem(n_used + ahead, GATHER_SLOTS))
        out_copy(n_used - 1, lax.rem(n_used - 1, 2)).wait()

        @pl.when(n_used >= 2)
        def _():
            out_copy(n_used - 2, lax.rem(n_used, 2)).wait()


def _moe_experts(n_used, first, tiles, src, xs, wg, wu, wd, row_tiles):
    d = xs.shape[-1]
    ne = wg.shape[0]
    w_map = lambda e, nu, fi, ti, sr: (e, 0, 0)
    grid_spec = pltpu.PrefetchScalarGridSpec(
        num_scalar_prefetch=4,
        grid=(ne,),
        in_specs=[pl.BlockSpec(memory_space=pl.ANY),
                  pl.BlockSpec((None, d, EXPERT_FF), w_map),
                  pl.BlockSpec((None, d, EXPERT_FF), w_map),
                  pl.BlockSpec((None, EXPERT_FF, d), w_map)],
        out_specs=pl.BlockSpec(memory_space=pl.ANY),
        scratch_shapes=[pltpu.VMEM((GATHER_SLOTS, ROW_TILE, d), BF16),
                        pltpu.VMEM((2, ROW_TILE, d), BF16),
                        pltpu.SemaphoreType.DMA((GATHER_SLOTS,)),
                        pltpu.SemaphoreType.DMA((2,)),
                        pltpu.VMEM((d, 2 * EXPERT_FF), BF16),
                        pltpu.VMEM((EXPERT_FF, d), BF16)])
    return pl.pallas_call(
        _moe_experts_kernel,
        out_shape=jax.ShapeDtypeStruct((row_tiles * ROW_TILE, d), BF16),
        grid_spec=grid_spec,
        compiler_params=pltpu.CompilerParams(dimension_semantics=("arbitrary",),
                                             vmem_limit_bytes=VMEM_LIMIT),
        name="moe_experts",
    )(n_used, first, tiles, src, xs, wg, wu, wd)


def _moe_combine_kernel(back_ref, used_ref, ys_hbm, pos_ref, wts_ref, xm_ref, x1_ref, g2_ref, fg_ref,
                        swg_ref, swu_ref, swd_ref, o_ref, buf, sem, acc_ref):
    i = pl.program_id(0)
    gpt = SORT_ROWS // SORT_ALIGN
    slot = lax.rem(i, COMBINE_SLOTS)
    ahead = COMBINE_SLOTS - 1
    always = ALWAYS_ROWS
    tail = range(always, SORT_ROWS, COMBINE_TAIL)

    def copies(tile, of_slot, g0, g1, start):
        for g in range(g0, g1):
            row = pl.multiple_of(back_ref[tile * gpt + g] * SORT_ALIGN, SORT_ALIGN) if start else 0
            cp = pltpu.make_async_copy(ys_hbm.at[pl.ds(row, SORT_ALIGN), :],
                                       buf.at[of_slot, g * SORT_ALIGN:(g + 1) * SORT_ALIGN, :], sem.at[of_slot])
            if start:
                cp.start(priority=g % 2)
            else:
                cp.wait()

    def transfer(tile, of_slot, start):
        copies(tile, of_slot, 0, always // SORT_ALIGN, start)
        for c0 in tail:
            pl.when(c0 < used_ref[tile])(functools.partial(
                copies, tile, of_slot, c0 // SORT_ALIGN, (c0 + COMBINE_TAIL) // SORT_ALIGN, start))

    @pl.when(i == 0)
    def _():
        for first in range(min(ahead, buf.shape[0])):
            pl.when(first < pl.num_programs(0))(functools.partial(transfer, first, first, True))

    @pl.when(i + ahead < pl.num_programs(0))
    def _():
        transfer(i + ahead, lax.rem(i + ahead, COMBINE_SLOTS), True)

    x = xm_ref[...]
    tm = x.shape[0]
    pad = jnp.zeros((LANES - TOP_K, tm), F32)
    pos_t = jnp.concatenate([pos_ref[...], pad], axis=0).T
    wts_t = jnp.concatenate([wts_ref[...], pad], axis=0).T
    blk_b, loc_b, wts_b = [], [], []
    for k in range(TOP_K):
        p = jnp.broadcast_to(pos_t[:, k:k + 1], (tm, LANES))
        blk = jnp.floor(p * (1.0 / tm))
        two = lambda v: jnp.concatenate([v.astype(BF16)] * (tm // LANES), axis=1)
        blk_b.append(two(blk))
        loc_b.append(two(p - blk * tm))
        wts_b.append(two(jnp.broadcast_to(wts_t[:, k:k + 1], (tm, LANES))))
    shared = _dot((_silu(_dot(x, swg_ref[...])) * _dot(x, swu_ref[...])).astype(BF16), swd_ref[...])
    transfer(i, slot, False)
    lane = lax.broadcasted_iota(jnp.int32, (tm, tm), 1).astype(F32).astype(BF16)
    zero = jnp.zeros((tm, tm), BF16)
    nowhere = jnp.full((tm, tm), -1.0, BF16)

    def apply(c0, width):
        blocks = []
        for b0 in range(c0, c0 + width, tm):
            comb = zero
            for k in range(TOP_K):
                loc = jnp.where(blk_b[k] == float(b0 // tm), loc_b[k], nowhere)
                comb = jnp.where(lane == loc, wts_b[k], comb)
            blocks.append(comb)
        return _dot(jnp.concatenate(blocks, axis=1), buf[slot, c0:c0 + width, :])

    routed = shared
    for c0 in range(0, always, COMBINE_CHUNK):
        routed = routed + apply(c0, min(COMBINE_CHUNK, always - c0))
    acc_ref[...] = routed
    for c0 in tail:
        @pl.when(c0 < used_ref[i])
        def _(c0=c0):
            acc_ref[...] += apply(c0, COMBINE_TAIL)
    y = x1_ref[...] + g2_ref[...] * acc_ref[...]
    o_ref[...] = _rms_norm(y, fg_ref[...])


def _moe_combine(back, used, ys, pos, wts, xm, x1, g2, final_g, swg, swu, swd, *, tiles_per_mod):
    n, d = xm.shape
    tm = pos.shape[-1]
    nt = n // tm
    gpt = SORT_ROWS // SORT_ALIGN
    row = lambda i, bk, us: (i, 0)
    full = lambda i, bk, us: (0, 0)
    tile = lambda i, bk, us: (i, 0, 0)
    mod_map = lambda i, bk, us: (i // tiles_per_mod, 0, 0)
    grid_spec = pltpu.PrefetchScalarGridSpec(
        num_scalar_prefetch=2,
        grid=(nt,),
        in_specs=[pl.BlockSpec(memory_space=pl.ANY),
                  pl.BlockSpec((None, TOP_K, tm), tile),
                  pl.BlockSpec((None, TOP_K, tm), tile),
                  pl.BlockSpec((tm, d), row),
                  pl.BlockSpec((tm, d), row),
                  pl.BlockSpec((None, 1, d), mod_map),
                  pl.BlockSpec((1, d), full),
                  pl.BlockSpec((d, SHARED_FF), full),
                  pl.BlockSpec((d, SHARED_FF), full),
                  pl.BlockSpec((SHARED_FF, d), full)],
        out_specs=pl.BlockSpec((tm, d), row),
        scratch_shapes=[pltpu.VMEM((COMBINE_SLOTS, SORT_ROWS, d), BF16),
                        pltpu.SemaphoreType.DMA((COMBINE_SLOTS,)),
                        pltpu.VMEM((tm, d), F32)])
    return pl.pallas_call(
        _moe_combine_kernel,
        out_shape=jax.ShapeDtypeStruct((n, d), F32),
        grid_spec=grid_spec,
        compiler_params=pltpu.CompilerParams(dimension_semantics=("arbitrary",),
                                             vmem_limit_bytes=VMEM_LIMIT),
        name="moe_combine",
    )(back, used, ys, pos.reshape(nt, TOP_K, tm), wts.reshape(nt, TOP_K, tm), xm, x1, g2, final_g, swg, swu, swd)


def _mix(x, mods, p, attn_fn, s0=None):
    sh1, sc1, g1, sh2, sc2, _ = mods
    gla_in, lora, q_s, k_s, v_s = _inproj(x, p["norm_attn_g"], sh1, sc1, p["w_gla"], p["w_lora"], p["w_swa"])
    if s0 is None:
        gla_out, s_f, s_b = _gla(gla_in, lora, p["waf"], p["baf"], p["wab"], p["bab"], p["gla_norm_g"])
    else:
        gla_out, s_f, s_b = _gla(gla_in, lora, p["waf"], p["baf"], p["wab"], p["bab"], p["gla_norm_g"],
                                 s0[0], s0[1])
    att_out = attn_fn(q_s, k_s, v_s)
    routed = _outproj(gla_out, att_out, x, p["w_out"], g1, sh2, sc2, p["norm_ffn_g"],
                      p["rw_cat"], p["rw_hi"], p["rbias"])
    return routed, k_s, v_s, s_f, s_b


def _moe(streams, p):
    d = D_MODEL
    (ra, _), (rb, _) = streams
    n_tiles = [r[1].shape[0] * r[1].shape[1] // MOE_TILE for r, _ in streams]
    pos_all = jnp.concatenate([r[2].reshape(-1, TOP_K, MOE_TILE) for r, _ in streams], axis=0)
    cnt_all = jnp.concatenate([r[4].reshape(-1, LANES) for r, _ in streams], axis=0)[:, :N_EXPERTS]
    start_all = jnp.concatenate([r[5].reshape(-1, LANES) for r, _ in streams], axis=0)[:, :N_EXPERTS]
    used = (start_all[:, -1] + cnt_all[:, -1]).astype(jnp.int32)
    xs = _moe_sort(ra[1].reshape(-1, d), rb[1].reshape(-1, d), pos_all, used)
    n_used, first, tiles, src, back = _moe_plan(cnt_all, start_all)
    ys = _moe_experts(n_used, first, tiles, src, xs, p["wg"], p["wu"], p["wd"],
                      _moe_row_tiles(cnt_all.shape[0] * MOE_TILE))
    outs = []
    tile0 = 0
    for ((x1, xm, pos, wts, cnt, start), g2), nt in zip(streams, n_tiles):
        b, t, _ = x1.shape
        tiles_per_mod = (t // MOE_TILE) if g2.shape[0] > 1 else nt
        y = _moe_combine(back[tile0:tile0 + nt].reshape(-1), used[tile0:tile0 + nt], ys, pos, wts,
                         xm.reshape(-1, d), x1.reshape(-1, d), g2, p["final_norm_g"],
                         p["swg"], p["swu"], p["swd"], tiles_per_mod=tiles_per_mod)
        outs.append(y.reshape(b, t, d))
        tile0 += nt
    return outs


def kernel(x_prompt, x_sample, c, cache_swa_k, cache_swa_v, state_gla_fwd, state_gla_bwd, c_ctx, w_ada, b_ada, norm_attn_g, norm_ffn_g, w_in, gla_wa_f, gla_ba_f, gla_wa_b, gla_ba_b, gla_norm_g, swa_sink, w_out, router_w, router_bias, exp_w_gate, exp_w_up, exp_w_down, sh_w_gate, sh_w_up, sh_w_down, final_norm_g):
    l = 0
    d = D_MODEL
    nb_ctx, t_ctx, _ = x_prompt.shape
    nb_lat, t_lat, _ = x_sample.shape

    pad = jnp.zeros((8 - 1 - nb_lat, d), F32)
    cond8 = jnp.concatenate([c_ctx[None, :], c, pad], axis=0)
    mod = _adaln(cond8, w_ada[l], b_ada[l][None, :])
    mods_ctx = [mod[0:1, i * d:(i + 1) * d][:, None, :] for i in range(6)]
    mods_lat = [mod[1:1 + nb_lat, i * d:(i + 1) * d][:, None, :] for i in range(6)]

    zeros_lora = jnp.zeros((GLA_LORA, GLA_QK), F32)
    rw = router_w[l]
    rw_hi = rw.astype(BF16)
    rw_lo = (rw - rw_hi.astype(F32)).astype(BF16)
    n_gla = 2 * GLA_QK + 2 * GLA_V
    p = {
        "norm_attn_g": norm_attn_g[l][None, :],
        "norm_ffn_g": norm_ffn_g[l][None, :],
        "final_norm_g": final_norm_g[None, :],
        "w_gla": w_in[l][:, :n_gla].astype(BF16),
        "w_lora": w_in[l][:, n_gla:n_gla + 2 * GLA_LORA].astype(BF16),
        "w_swa": w_in[l][:, n_gla + 2 * GLA_LORA:].astype(BF16),
        "waf": jnp.concatenate([gla_wa_f[l], zeros_lora], axis=0).astype(BF16),
        "wab": jnp.concatenate([zeros_lora, gla_wa_b[l]], axis=0).astype(BF16),
        "baf": gla_ba_f[l][None, :],
        "bab": gla_ba_b[l][None, :],
        "gla_norm_g": gla_norm_g[l][None, :],
        "w_out": w_out[l].astype(BF16),
        "rw_cat": jnp.concatenate([rw_hi, rw_lo], axis=1),
        "rw_hi": rw_hi,
        "rbias": router_bias[l][:, None],
        "wg": exp_w_gate[l], "wu": exp_w_up[l], "wd": exp_w_down[l],
        "swg": sh_w_gate[l].astype(BF16), "swu": sh_w_up[l].astype(BF16),
        "swd": sh_w_down[l].astype(BF16),
    }
    sink = swa_sink[l]

    routed_ctx, k_c, v_c, s_f, s_b = _mix(x_prompt, mods_ctx, p, functools.partial(_attn_ctx, sink))

    cos, sin_lo, sin_hi = _rope_tables(t_lat)
    kc = cache_swa_k[:, l].reshape(nb_lat, -1, SWA_KV)
    vc = cache_swa_v[:, l].reshape(nb_lat, -1, SWA_KV)
    lat_attn = lambda q, k, v: _attn_lat(sink, q, k, v, kc, vc, cos, sin_lo, sin_hi)
    s0 = (state_gla_fwd[:, l].reshape(nb_lat, GLA_QK, GLA_DV),
          state_gla_bwd[:, l].reshape(nb_lat, GLA_QK, GLA_DV))
    routed_lat, _, _, _, _ = _mix(x_sample, mods_lat, p, lat_attn, s0)
    y_prompt, y_sample = _moe([(routed_ctx, mods_ctx[5]), (routed_lat, mods_lat[5])], p)

    new_k = k_c.reshape(nb_ctx, 1, t_ctx, SWA_KV_HEADS, SWA_HEAD_DIM)
    new_v = v_c.reshape(nb_ctx, 1, t_ctx, SWA_KV_HEADS, SWA_HEAD_DIM)
    new_sf = s_f.reshape(nb_ctx, 1, GLA_HEADS, GLA_DK, GLA_DV)
    new_sb = s_b.reshape(nb_ctx, 1, GLA_HEADS, GLA_DK, GLA_DV)
    return (y_prompt, y_sample, new_k, new_v, new_sf, new_sb)
```

```python
import functools

import jax
import jax.numpy as jnp
from jax import lax
from jax.experimental import pallas as pl
from jax.experimental.pallas import tpu as pltpu

F32 = jnp.float32
BF16 = jnp.bfloat16

D_MODEL = 1024
GLA_HEADS = 4
GLA_DK = 64
GLA_DV = 128
GLA_LORA = 16
GLA_GATE_NORM = 16.0
GLA_CHUNK = 64
GLA_QK = GLA_HEADS * GLA_DK
GLA_V = GLA_HEADS * GLA_DV
SWA_HEAD_DIM = 64
SWA_HEADS = 8
SWA_KV_HEADS = 2
SWA_Q = SWA_HEADS * SWA_HEAD_DIM
SWA_KV = SWA_KV_HEADS * SWA_HEAD_DIM
ATTN_BLOCK = 128
GRID_W = 64
ROPE_BASE = 10000.0
N_EXPERTS = 64
TOP_K = 8
N_EXPERT_GROUPS = 8
TOPK_GROUPS = 4
EXPERT_FF = 128
SHARED_FF = 256
ROUTED_SCALE = 2.5
EPS = 1e-6

LANES = 128
VMEM_LIMIT = 56 * 1024 * 1024

NEG_INF = float("-inf")


def _dot(a, b):
    return jnp.dot(a, b, preferred_element_type=F32)


def _dot_nt(a, b):
    return lax.dot_general(a, b, (((1,), (1,)), ((), ())), preferred_element_type=F32)


def _split_hi_lo(x):
    hi = x.astype(BF16)
    lo = (x - hi.astype(F32)).astype(BF16)
    return hi, lo


def _sigmoid(x):
    return 1.0 / (1.0 + jnp.exp(-x))


def _silu(x):
    return x * _sigmoid(x)


def _rms_norm(x, g):
    ms = jnp.mean(x * x, axis=-1, keepdims=True)
    return x * lax.rsqrt(ms + EPS) * g


def _adaln_kernel(c_ref, w_ref, b_ref, o_ref):
    a_hi, a_lo = _split_hi_lo(_silu(c_ref[...]))
    w_hi, w_lo = _split_hi_lo(w_ref[...])
    o_ref[...] = _dot(a_hi, w_hi) + _dot(a_lo, w_hi) + _dot(a_hi, w_lo) + b_ref[...]


def _adaln(cond8, w_ada, b_ada):
    n = w_ada.shape[1]
    tn = 1536
    return pl.pallas_call(
        _adaln_kernel,
        out_shape=jax.ShapeDtypeStruct((8, n), F32),
        grid=(n // tn,),
        in_specs=[pl.BlockSpec((8, D_MODEL), lambda j: (0, 0)),
                  pl.BlockSpec((D_MODEL, tn), lambda j: (0, j)),
                  pl.BlockSpec((1, tn), lambda j: (0, j))],
        out_specs=pl.BlockSpec((8, tn), lambda j: (0, j)),
        compiler_params=pltpu.CompilerParams(dimension_semantics=("arbitrary",),
                                             vmem_limit_bytes=VMEM_LIMIT),
        name="adaln",
    )(cond8, w_ada, b_ada)


def _inproj_kernel(x_ref, g_ref, sh_ref, sc_ref, wg_ref, wl_ref, ws_ref,
                   gla_ref, lora_ref, q_ref, k_ref, v_ref):
    bb, tb, d = x_ref.shape
    x = x_ref[...].reshape(bb * tb, d)
    h = _rms_norm(x, g_ref[...]) * (1.0 + sc_ref[...]) + sh_ref[...]
    hb = h.astype(BF16)
    gla_ref[...] = _dot(hb, wg_ref[...]).reshape(gla_ref.shape)
    lora_ref[...] = _dot(hb, wl_ref[...]).reshape(lora_ref.shape)
    s = _dot(hb, ws_ref[...])
    q_ref[...] = s[:, :SWA_Q].reshape(q_ref.shape)
    k_ref[...] = s[:, SWA_Q:SWA_Q + SWA_KV].reshape(k_ref.shape)
    v_ref[...] = s[:, SWA_Q + SWA_KV:].reshape(v_ref.shape)


INPROJ_TILE = 1024


def _inproj(x, g, sh, sc, w_gla, w_lora, w_swa):
    b, t, d = x.shape
    nmod = sh.shape[0]
    tb = min(t, INPROJ_TILE)
    bb = INPROJ_TILE // tb if nmod == 1 else 1
    mod_map = (lambda i, j: (i, 0, 0)) if nmod > 1 else (lambda i, j: (0, 0, 0))
    row = lambda i, j: (i, j, 0)
    full = lambda i, j: (0, 0)
    n_gla = w_gla.shape[1]
    n_lora = w_lora.shape[1]
    return pl.pallas_call(
        _inproj_kernel,
        out_shape=(jax.ShapeDtypeStruct((b, t, n_gla), F32),
                   jax.ShapeDtypeStruct((b, t, n_lora), F32),
                   jax.ShapeDtypeStruct((b, t, SWA_Q), F32),
                   jax.ShapeDtypeStruct((b, t, SWA_KV), F32),
                   jax.ShapeDtypeStruct((b, t, SWA_KV), F32)),
        grid=(b // bb, t // tb),
        in_specs=[pl.BlockSpec((bb, tb, d), row),
                  pl.BlockSpec((1, d), full),
                  pl.BlockSpec((None, 1, d), mod_map),
                  pl.BlockSpec((None, 1, d), mod_map),
                  pl.BlockSpec((d, n_gla), full, pipeline_mode=pl.Buffered(1)),
                  pl.BlockSpec((d, n_lora), full, pipeline_mode=pl.Buffered(1)),
                  pl.BlockSpec((d, w_swa.shape[1]), full, pipeline_mode=pl.Buffered(1))],
        out_specs=(pl.BlockSpec((bb, tb, n_gla), row),
                   pl.BlockSpec((bb, tb, n_lora), row),
                   pl.BlockSpec((bb, tb, SWA_Q), row),
                   pl.BlockSpec((bb, tb, SWA_KV), row),
                   pl.BlockSpec((bb, tb, SWA_KV), row)),
        compiler_params=pltpu.CompilerParams(dimension_semantics=("arbitrary", "arbitrary"),
                                             vmem_limit_bytes=VMEM_LIMIT),
        name="inproj",
    )(x, g, sh, sc, w_gla, w_lora, w_swa)


SCAN_UNROLL = 4
OUT_UNROLL = 4


def _log_sigmoid(x):
    return jnp.minimum(x, 0.0) - jnp.log(1.0 + jnp.exp(-jnp.abs(x)))


def _heads_to_rows(x):
    return jnp.concatenate([x[:, h * LANES:(h + 1) * LANES] for h in range(GLA_HEADS)], axis=0)


def _rows_to_heads(x, c):
    return jnp.concatenate([x[h * c:(h + 1) * c, :] for h in range(GLA_HEADS)], axis=1)


def _gla_kernel(has_init, q_ref, k_ref, v_ref, g_ref, lora_ref, waf_ref, baf_ref, wab_ref, bab_ref,
                ng_ref, *rest):
    if has_init:
        s0f_ref, s0b_ref, *rest = rest
    (out_ref, sf_ref, sb_ref, laf_ref, lab_ref, oacc_ref, qtf_ref, qtb_ref, saf_ref, sab_ref,
     stf_ref, stb_ref) = rest
    t = q_ref.shape[0]
    c = GLA_CHUNK
    n = t // c
    hc = GLA_HEADS * c

    lora = lora_ref[...].astype(BF16)
    laf_ref[...] = _log_sigmoid(_dot(lora, waf_ref[...]) + baf_ref[...]) * (1.0 / GLA_GATE_NORM)
    lab_ref[...] = _log_sigmoid(_dot(lora, wab_ref[...]) + bab_ref[...]) * (1.0 / GLA_GATE_NORM)

    if has_init:
        stf_ref[...] = s0f_ref[...].T
        stb_ref[...] = s0b_ref[...].T
    else:
        stf_ref[...] = jnp.zeros_like(stf_ref)
        stb_ref[...] = jnp.zeros_like(stb_ref)
    oacc_ref[...] = jnp.zeros_like(oacc_ref)

    r64 = lax.broadcasted_iota(jnp.int32, (c, c), 0)
    c64 = lax.broadcasted_iota(jnp.int32, (c, c), 1)
    tri_f = jnp.where(c64 <= r64, 1.0, 0.0).astype(BF16)
    tri_b = jnp.where(c64 >= r64, 1.0, 0.0).astype(BF16)
    rr = lax.broadcasted_iota(jnp.int32, (hc, hc), 0)
    cc = lax.broadcasted_iota(jnp.int32, (hc, hc), 1)
    same_head = (rr >> 6) == (cc >> 6)
    keep_f = same_head & ((rr & (c - 1)) >= (cc & (c - 1)))
    keep_b = same_head & ((rr & (c - 1)) <= (cc & (c - 1)))
    head_mask = jnp.where(same_head, 1.0, 0.0).astype(BF16)
    norm_g = ng_ref[...]

    def chunk_rows(ci):
        return pl.ds(pl.multiple_of(ci * c, c), c)

    def tile_heads(x):
        x4 = jnp.concatenate([x] * GLA_HEADS, axis=0)
        return jnp.where(same_head, x4, 0.0).astype(BF16)

    def scan_step(i, carry):
        dirs = []
        for u in range(SCAN_UNROLL):
            dirs += [(SCAN_UNROLL * i + u, laf_ref, tri_f, keep_f, c - 1, stf_ref, saf_ref, qtf_ref),
                     (n - 1 - SCAN_UNROLL * i - u, lab_ref, tri_b, keep_b, 0, stb_ref, sab_ref, qtb_ref)]
        cums = []
        for ci, la_ref, tri, _, _, _, _, _ in dirs:
            la_hi, la_lo = _split_hi_lo(la_ref[chunk_rows(ci), :])
            cums.append(_dot(tri, la_hi) + _dot(tri, la_lo))
        ops = []
        for (ci, _, _, _, last_row, _, _, qt_ref), cum in zip(dirs, cums):
            sl = chunk_rows(ci)
            tot = cum[last_row:last_row + 1, :]
            kc = k_ref[sl, :]
            qt = q_ref[sl, :] * (GLA_DK ** -0.5) * jnp.exp(cum)
            qt_ref[sl, :] = qt.astype(BF16)
            v_rows = _heads_to_rows(v_ref[sl, :])
            ops.append((tot, tile_heads(qt), tile_heads(kc * jnp.exp(-cum)),
                        tile_heads(kc * jnp.exp(tot - cum)), v_rows))
        atts = [_dot_nt(q4, k4) for _, q4, k4, _, _ in ops]
        incs = []
        for (_, _, _, keep, _, _, _, _), (_, _, _, kd4, v_rows), att in zip(dirs, ops, atts):
            att = jnp.where(keep, att, 0.0).astype(BF16)
            incs.append((_dot(att, v_rows.astype(BF16)), _dot(v_rows.T.astype(BF16), kd4)))
        for (ci, _, _, _, _, st_ref, snap_ref, _), (tot, _, _, _, _), (o_intra, st_inc) in zip(dirs, ops, incs):
            oacc_ref[ci] += o_intra
            st = st_ref[...]
            snap_ref[ci] = st.astype(BF16)
            st_ref[...] = jnp.exp(tot) * st + st_inc
        return carry

    def tile_heads_bf16(x):
        return jnp.concatenate([x] * GLA_HEADS, axis=0) * head_mask

    def out_step(i, carry):
        chunks = [OUT_UNROLL * i + u for u in range(OUT_UNROLL)]
        inter = []
        for ci in chunks:
            sl = chunk_rows(ci)
            q4 = jnp.concatenate([tile_heads_bf16(qtf_ref[sl, :]), tile_heads_bf16(qtb_ref[sl, :])], axis=1)
            st = jnp.concatenate([saf_ref[ci], sab_ref[ci]], axis=1)
            inter.append(_dot_nt(q4, st))
        for ci, o_inter in zip(chunks, inter):
            sl = chunk_rows(ci)
            on = _rms_norm(oacc_ref[ci] + o_inter, norm_g)
            gate = _silu(_heads_to_rows(g_ref[sl, :]))
            out_ref[sl, :] = _rows_to_heads(on * gate, c)
        return carry

    lax.fori_loop(0, n // SCAN_UNROLL, scan_step, 0)
    lax.fori_loop(0, n // OUT_UNROLL, out_step, 0)
    sf_ref[...] = stf_ref[...].T
    sb_ref[...] = stb_ref[...].T


def _gla(gla_in, lora, waf, baf, wab, bab, norm_g, s0f=None, s0b=None):
    b, t, _ = gla_in.shape
    has_init = s0f is not None
    n = t // GLA_CHUNK
    bmap = lambda i: (i, 0, 0)
    full = lambda i: (0, 0)
    in_specs = [pl.BlockSpec((None, t, GLA_QK), lambda i: (i, 0, 0)),
                pl.BlockSpec((None, t, GLA_QK), lambda i: (i, 0, 1)),
                pl.BlockSpec((None, t, GLA_V), lambda i: (i, 0, 1)),
                pl.BlockSpec((None, t, GLA_V), lambda i: (i, 0, 2)),
                pl.BlockSpec((None, t, 2 * GLA_LORA), bmap),
                pl.BlockSpec((2 * GLA_LORA, GLA_QK), full),
                pl.BlockSpec((1, GLA_QK), full),
                pl.BlockSpec((2 * GLA_LORA, GLA_QK), full),
                pl.BlockSpec((1, GLA_QK), full),
                pl.BlockSpec((1, GLA_DV), full)]
    args = [gla_in, gla_in, gla_in, gla_in, lora, waf, baf, wab, bab, norm_g]
    if has_init:
        in_specs += [pl.BlockSpec((None, GLA_QK, GLA_DV), bmap)] * 2
        args += [s0f, s0b]
    return pl.pallas_call(
        functools.partial(_gla_kernel, has_init),
        out_shape=(jax.ShapeDtypeStruct((b, t, GLA_V), F32),
                   jax.ShapeDtypeStruct((b, GLA_QK, GLA_DV), F32),
                   jax.ShapeDtypeStruct((b, GLA_QK, GLA_DV), F32)),
        grid=(b,),
        in_specs=in_specs,
        out_specs=(pl.BlockSpec((None, t, GLA_V), bmap),
                   pl.BlockSpec((None, GLA_QK, GLA_DV), bmap),
                   pl.BlockSpec((None, GLA_QK, GLA_DV), bmap)),
        scratch_shapes=[pltpu.VMEM((t, GLA_QK), F32),
                        pltpu.VMEM((t, GLA_QK), F32),
                        pltpu.VMEM((n, GLA_HEADS * GLA_CHUNK, GLA_DV), F32),
                        pltpu.VMEM((t, GLA_QK), BF16),
                        pltpu.VMEM((t, GLA_QK), BF16),
                        pltpu.VMEM((n, GLA_DV, GLA_QK), BF16),
                        pltpu.VMEM((n, GLA_DV, GLA_QK), BF16),
                        pltpu.VMEM((GLA_DV, GLA_QK), F32),
                        pltpu.VMEM((GLA_DV, GLA_QK), F32)],
        compiler_params=pltpu.CompilerParams(dimension_semantics=("arbitrary",),
                                             vmem_limit_bytes=VMEM_LIMIT),
        name="gla",
    )(*args)


def _dup_groups(x):
    lo = lax.broadcasted_iota(jnp.int32, x.shape, 1) < SWA_HEAD_DIM
    xr = pltpu.roll(x, SWA_HEAD_DIM, axis=1)
    return jnp.where(lo, x, xr), jnp.where(lo, xr, x)


def _pairs_attention(qps, sinks, k_dups, vt_dups, masks):
    nq = qps[0].shape[0]
    lo = lax.broadcasted_iota(jnp.int32, (nq, LANES), 1) < SWA_HEAD_DIM
    even = lax.broadcasted_iota(jnp.int32, (1, 2 * nq), 1) < nq
    scores = []
    for qp, k_dup in zip(qps, k_dups):
        q2 = jnp.concatenate([jnp.where(lo, qp, 0.0), jnp.where(lo, 0.0, qp)], axis=0).astype(BF16)
        scores.append(_dot_nt(k_dup, q2))
    probs = []
    for s, (sink_even, sink_odd), mask in zip(scores, sinks, masks):
        if mask is not None:
            s = jnp.where(mask, s, NEG_INF)
        sink = jnp.where(even, sink_even, sink_odd)
        m = jnp.maximum(jnp.max(s, axis=0, keepdims=True), sink)
        p = jnp.exp(s - m)
        denom = jnp.sum(p, axis=0, keepdims=True) + jnp.exp(sink - m)
        probs.append((p.astype(BF16), 1.0 / denom))
    outs = []
    for (p, rdenom), vt_dup in zip(probs, vt_dups):
        o = _dot(vt_dup, p) * rdenom
        outs.append(jnp.concatenate([o[:SWA_HEAD_DIM, :nq], o[SWA_HEAD_DIM:, nq:]], axis=0).T)
    return outs


CTX_BATCH = 4


def _attn_ctx_kernel(sink_ref, q_ref, k_ref, v_ref, o_ref):
    scale = SWA_HEAD_DIM ** -0.5
    pairs = range(SWA_HEADS // 2)
    items = [(bb, pr) for bb in range(q_ref.shape[0]) for pr in pairs]
    kd = [[x.astype(BF16) for x in _dup_groups(k_ref[bb])] for bb in range(q_ref.shape[0])]
    vt = [[x.T.astype(BF16) for x in _dup_groups(v_ref[bb])] for bb in range(q_ref.shape[0])]
    outs = _pairs_attention([q_ref[bb, :, pr * LANES:(pr + 1) * LANES] * scale for bb, pr in items],
                            [(sink_ref[2 * pr], sink_ref[2 * pr + 1]) for _, pr in items],
                            [kd[bb][pr // 2] for bb, pr in items], [vt[bb][pr // 2] for bb, pr in items],
                            [None] * len(items))
    for (bb, pr), out in zip(items, outs):
        o_ref[bb, :, pr * LANES:(pr + 1) * LANES] = out


def _attn_ctx(sink, q, k, v):
    b, t, _ = q.shape
    bmap = lambda i: (i, 0, 0)
    return pl.pallas_call(
        _attn_ctx_kernel,
        out_shape=jax.ShapeDtypeStruct((b, t, SWA_Q), F32),
        grid=(b // CTX_BATCH,),
        in_specs=[pl.BlockSpec(memory_space=pltpu.SMEM),
                  pl.BlockSpec((CTX_BATCH, t, SWA_Q), bmap),
                  pl.BlockSpec((CTX_BATCH, t, SWA_KV), bmap),
                  pl.BlockSpec((CTX_BATCH, t, SWA_KV), bmap)],
        out_specs=pl.BlockSpec((CTX_BATCH, t, SWA_Q), bmap),
        compiler_params=pltpu.CompilerParams(dimension_semantics=("arbitrary",),
                                             vmem_limit_bytes=VMEM_LIMIT),
        name="attn_ctx",
    )(sink, q, k, v)


LAT_BLOCKS = 2


def _rope(x, cos, sin_lo, sin_hi):
    return x * cos + pltpu.roll(x, LANES - 16, axis=1) * sin_lo + pltpu.roll(x, 16, axis=1) * sin_hi


def _attn_lat_kernel(sink_ref, q_ref, k_ref, v_ref, kc_ref, vc_ref, cos_ref, sl_ref, sh_ref,
                     o_ref, kw_ref, vw_ref):
    t = q_ref.shape[0]
    ab = ATTN_BLOCK
    nb = t // ab
    scale = SWA_HEAD_DIM ** -0.5

    k_rot = _dup_groups(_rope(k_ref[...], cos_ref[...], sl_ref[...], sh_ref[...]))
    v_dup = _dup_groups(v_ref[...])
    zeros = jnp.zeros((ab, LANES), BF16)
    for grp in range(SWA_KV_HEADS):
        kw_ref[grp, 0:ab, :] = zeros
        kw_ref[grp, ab:ab + t, :] = k_rot[grp].astype(BF16)
        kw_ref[grp, ab + t:, :] = zeros
        vw_ref[grp, 0] = zeros
        for blk in range(nb):
            vw_ref[grp, blk + 1] = v_dup[grp][blk * ab:(blk + 1) * ab, :].T.astype(BF16)
        vw_ref[grp, nb + 1] = zeros
    kc = [x.astype(BF16) for x in _dup_groups(kc_ref[...])]
    vct = [x.T.astype(BF16) for x in _dup_groups(vc_ref[...])]
    lc = kc_ref.shape[0]

    key = lax.broadcasted_iota(jnp.int32, (lc + 3 * ab, 2 * ab), 0) - lc
    tq = lax.broadcasted_iota(jnp.int32, (lc + 3 * ab, 2 * ab), 1) & (ab - 1)
    band = (key < 0) | (jnp.abs(tq + ab - key) <= ab)

    def block(it, carry):
        pairs = range(SWA_HEADS // 2)
        qps, sinks, k_dups, vt_dups, masks, places = [], [], [], [], [], []
        for u in range(LAT_BLOCKS):
            nq = it * LAT_BLOCKS + u
            row0 = pl.multiple_of(nq * ab, ab)
            s_abs = key + (nq - 1) * ab
            mask = band & ((key < 0) | ((s_abs >= 0) & (s_abs < t)))
            cos = cos_ref[pl.ds(row0, ab), :]
            s_lo = sl_ref[pl.ds(row0, ab), :]
            s_hi = sh_ref[pl.ds(row0, ab), :]
            k_all = [jnp.concatenate([kc[grp], kw_ref[grp, pl.ds(row0, 3 * ab), :]], axis=0)
                     for grp in range(SWA_KV_HEADS)]
            vt_all = [jnp.concatenate([vct[grp], vw_ref[grp, nq], vw_ref[grp, nq + 1], vw_ref[grp, nq + 2]],
                                      axis=1) for grp in range(SWA_KV_HEADS)]
            for pr in pairs:
                qps.append(_rope(q_ref[pl.ds(row0, ab), pr * LANES:(pr + 1) * LANES], cos, s_lo, s_hi) * scale)
                sinks.append((sink_ref[2 * pr], sink_ref[2 * pr + 1]))
                k_dups.append(k_all[pr // 2])
                vt_dups.append(vt_all[pr // 2])
                masks.append(mask)
                places.append((row0, pr))
        outs = _pairs_attention(qps, sinks, k_dups, vt_dups, masks)
        for (row0, pr), out in zip(places, outs):
            o_ref[pl.ds(row0, ab), pr * LANES:(pr + 1) * LANES] = out
        return carry

    lax.fori_loop(0, nb // LAT_BLOCKS, block, 0)


def _attn_lat(sink, q, k, v, kc, vc, cos, sin_lo, sin_hi):
    b, t, _ = q.shape
    lc = kc.shape[1]
    bmap = lambda i: (i, 0, 0)
    full = lambda i: (0, 0)
    return pl.pallas_call(
        _attn_lat_kernel,
        out_shape=jax.ShapeDtypeStruct((b, t, SWA_Q), F32),
        grid=(b,),
        in_specs=[pl.BlockSpec(memory_space=pltpu.SMEM),
                  pl.BlockSpec((None, t, SWA_Q), bmap),
                  pl.BlockSpec((None, t, SWA_KV), bmap),
                  pl.BlockSpec((None, t, SWA_KV), bmap),
                  pl.BlockSpec((None, lc, SWA_KV), bmap),
                  pl.BlockSpec((None, lc, SWA_KV), bmap),
                  pl.BlockSpec((t, LANES), full),
                  pl.BlockSpec((t, LANES), full),
                  pl.BlockSpec((t, LANES), full)],
        out_specs=pl.BlockSpec((None, t, SWA_Q), bmap),
        scratch_shapes=[pltpu.VMEM((SWA_KV_HEADS, t + 2 * ATTN_BLOCK, LANES), BF16),
                        pltpu.VMEM((SWA_KV_HEADS, t // ATTN_BLOCK + 2, LANES, ATTN_BLOCK), BF16)],
        compiler_params=pltpu.CompilerParams(dimension_semantics=("arbitrary",),
                                             vmem_limit_bytes=VMEM_LIMIT),
        name="attn_lat",
    )(sink, q, k, v, kc, vc, cos, sin_lo, sin_hi)


def _rope_tables(t):
    half = SWA_HEAD_DIM // 2
    quarter = half // 2
    rows = t // GRID_W
    inv_freq = ROPE_BASE ** (-jnp.arange(quarter, dtype=F32) / quarter)
    reps = LANES // quarter
    ang_row = jnp.tile(jnp.arange(rows).astype(F32)[:, None] * inv_freq[None, :], (1, reps))
    ang_col = jnp.tile(jnp.arange(GRID_W).astype(F32)[:, None] * inv_freq[None, :], (1, reps))
    d = jnp.arange(LANES) % SWA_HEAD_DIM
    use_row = (d < half)[None, :]
    lower = ((d % half) < quarter)[None, :]

    def expand(f):
        by_row = jnp.repeat(f(ang_row), GRID_W, axis=0)
        by_col = jnp.tile(f(ang_col), (rows, 1))
        return jnp.where(use_row, by_row, by_col)

    cos = expand(jnp.cos)
    sin = expand(jnp.sin)
    return cos, jnp.where(lower, -sin, 0.0), jnp.where(lower, 0.0, sin)


def _route(sel, scores):
    n = sel.shape[1]
    gsz = N_EXPERTS // N_EXPERT_GROUPS

    def first_max(x, idx, size):
        m = jnp.max(x, axis=0, keepdims=True)
        first = jnp.min(jnp.where(x == m, idx, float(size)), axis=0, keepdims=True)
        return m, idx == first

    i8 = lax.broadcasted_iota(jnp.int32, (gsz, n), 0).astype(F32)
    rows = []
    for g in range(N_EXPERT_GROUPS):
        slab = sel[g * gsz:(g + 1) * gsz, :]
        m1, hit = first_max(slab, i8, gsz)
        m2 = jnp.max(jnp.where(hit, NEG_INF, slab), axis=0, keepdims=True)
        rows.append(m1 + m2)
    gscore = jnp.concatenate(rows, axis=0)
    gsel = jnp.zeros((N_EXPERT_GROUPS, n), F32)
    for _ in range(TOPK_GROUPS):
        _, hit = first_max(gscore, i8, N_EXPERT_GROUPS)
        gsel = jnp.where(hit, 1.0, gsel)
        gscore = jnp.where(hit, NEG_INF, gscore)
    emask = jnp.concatenate(
        [jnp.broadcast_to(gsel[g:g + 1, :], (gsz, n)) for g in range(N_EXPERT_GROUPS)], axis=0)
    cand = jnp.where(emask > 0.5, sel, NEG_INF)
    ie = lax.broadcasted_iota(jnp.int32, (N_EXPERTS, n), 0).astype(F32)
    w = jnp.zeros((N_EXPERTS, n), F32)
    chosen = jnp.zeros((N_EXPERTS, n), F32)
    hits = []
    for _ in range(TOP_K):
        _, hit = first_max(cand, ie, N_EXPERTS)
        hits.append(hit)
        w = jnp.where(hit, scores, w)
        chosen = jnp.where(hit, 1.0, chosen)
        cand = jnp.where(hit, NEG_INF, cand)
    gates = w / jnp.sum(w, axis=0, keepdims=True) * ROUTED_SCALE

    s_idx = lax.broadcasted_iota(jnp.int32, (n, n), 0)
    t_idx = lax.broadcasted_iota(jnp.int32, (n, n), 1)
    tile_shift = MOE_TILE.bit_length() - 1
    before = jnp.where((s_idx < t_idx) & ((s_idx >> tile_shift) == (t_idx >> tile_shift)), 1.0, 0.0)
    rank = _dot(chosen.astype(BF16), before.astype(BF16))
    e_row = lax.broadcasted_iota(jnp.int32, (N_EXPERTS, N_EXPERTS), 0)
    e_col = lax.broadcasted_iota(jnp.int32, (N_EXPERTS, N_EXPERTS), 1)
    below = jnp.where(e_col < e_row, 1.0, 0.0).astype(BF16)
    lane_tile = lax.broadcasted_iota(jnp.int32, (1, n), 1) >> tile_shift

    def as_row(col):
        return jnp.concatenate([col, jnp.zeros((LANES - N_EXPERTS, LANES), F32)], axis=0).T[0:1, :]

    sizes, starts = [], []
    first_row = jnp.zeros((N_EXPERTS, n), F32)
    for ti in range(n // MOE_TILE):
        count = jnp.sum(chosen[:, ti * MOE_TILE:(ti + 1) * MOE_TILE], axis=1, keepdims=True)
        padded = jnp.floor((count + (SORT_ALIGN - 1)) * (1.0 / SORT_ALIGN)) * SORT_ALIGN
        padded = jnp.broadcast_to(padded, (N_EXPERTS, LANES))
        start = _dot(below, padded.astype(BF16))
        first_row = jnp.where(lane_tile == ti, start[:, 0:1], first_row)
        sizes.append(as_row(padded))
        starts.append(as_row(start))
    row = first_row + rank
    pos = jnp.concatenate([jnp.sum(jnp.where(h, row, 0.0), axis=0, keepdims=True) for h in hits], axis=0)
    wts = jnp.concatenate([jnp.sum(jnp.where(h, gates, 0.0), axis=0, keepdims=True) for h in hits], axis=0)
    return pos, wts, sizes, starts


def _outproj_kernel(gla_ref, att_ref, x_ref, wo_ref, g1_ref, sh_ref, sc_ref, ng_ref, rw_ref, rwh_ref,
                    rb_ref, x1_ref, xm_ref, pos_ref, wts_ref, cnt_ref, start_ref):
    bb, tb, d = x_ref.shape
    tm = bb * tb
    y = (_dot(gla_ref[...].reshape(tm, GLA_V).astype(BF16), wo_ref[0:GLA_V, :])
         + _dot(att_ref[...].reshape(tm, SWA_Q).astype(BF16), wo_ref[GLA_V:, :]))
    x1 = x_ref[...].reshape(tm, d) + g1_ref[...] * y
    x1_ref[...] = x1.reshape(bb, tb, d)
    xm = _rms_norm(x1, ng_ref[...]) * (1.0 + sc_ref[...]) + sh_ref[...]
    xm_hi, xm_lo = _split_hi_lo(xm)
    xm_ref[...] = xm_hi.reshape(bb, tb, d)
    lg = _dot(xm_hi, rw_ref[...])
    logits = lg[:, :N_EXPERTS] + lg[:, N_EXPERTS:] + _dot(xm_lo, rwh_ref[...])
    lt = jnp.concatenate([logits, jnp.zeros((tm, LANES - N_EXPERTS), F32)], axis=1).T[:N_EXPERTS, :]
    scores = _sigmoid(lt)
    pos, wts, sizes, starts = _route(scores + rb_ref[...], scores)
    tiles_per_batch = tb // MOE_TILE
    for ti in range(tm // MOE_TILE):
        at = (ti // tiles_per_batch, ti % tiles_per_batch)
        pos_ref[at] = pos[:, ti * MOE_TILE:(ti + 1) * MOE_TILE]
        wts_ref[at] = wts[:, ti * MOE_TILE:(ti + 1) * MOE_TILE]
        cnt_ref[at] = sizes[ti]
        start_ref[at] = starts[ti]


OUTPROJ_TILE = 1024


def _outproj(gla_out, att_out, x, w_out, g1, sh2, sc2, norm_g, rw_cat, rw_hi, rbias):
    b, t, d = x.shape
    nmod = g1.shape[0]
    tb = min(t, OUTPROJ_TILE)
    bb = OUTPROJ_TILE // tb if nmod == 1 else 1
    tpb = tb // MOE_TILE
    mod_map = (lambda i, j: (i, 0, 0)) if nmod > 1 else (lambda i, j: (0, 0, 0))
    row = lambda i, j: (i, j, 0)
    full = lambda i, j: (0, 0)
    tile = lambda i, j: (i, j, 0, 0)
    nt = t // MOE_TILE
    return pl.pallas_call(
        _outproj_kernel,
        out_shape=(jax.ShapeDtypeStruct((b, t, d), F32),
                   jax.ShapeDtypeStruct((b, t, d), BF16),
                   jax.ShapeDtypeStruct((b, nt, TOP_K, MOE_TILE), F32),
                   jax.ShapeDtypeStruct((b, nt, TOP_K, MOE_TILE), F32),
                   jax.ShapeDtypeStruct((b, nt, 1, LANES), F32),
                   jax.ShapeDtypeStruct((b, nt, 1, LANES), F32)),
        grid=(b // bb, t // tb),
        in_specs=[pl.BlockSpec((bb, tb, GLA_V), row),
                  pl.BlockSpec((bb, tb, SWA_Q), row),
                  pl.BlockSpec((bb, tb, d), row),
                  pl.BlockSpec((d, d), full, pipeline_mode=pl.Buffered(1)),
                  pl.BlockSpec((None, 1, d), mod_map),
                  pl.BlockSpec((None, 1, d), mod_map),
                  pl.BlockSpec((None, 1, d), mod_map),
                  pl.BlockSpec((1, d), full),
                  pl.BlockSpec((d, 2 * N_EXPERTS), full),
                  pl.BlockSpec((d, N_EXPERTS), full),
                  pl.BlockSpec((N_EXPERTS, 1), full)],
        out_specs=(pl.BlockSpec((bb, tb, d), row),
                   pl.BlockSpec((bb, tb, d), row),
                   pl.BlockSpec((bb, tpb, TOP_K, MOE_TILE), tile),
                   pl.BlockSpec((bb, tpb, TOP_K, MOE_TILE), tile),
                   pl.BlockSpec((bb, tpb, 1, LANES), tile),
                   pl.BlockSpec((bb, tpb, 1, LANES), tile)),
        compiler_params=pltpu.CompilerParams(dimension_semantics=("arbitrary", "arbitrary"),
                                             vmem_limit_bytes=VMEM_LIMIT),
        name="outproj",
    )(gla_out, att_out, x, w_out, g1, sh2, sc2, norm_g, rw_cat, rw_hi, rbias)


MOE_TILE = 256
SORT_ALIGN = 16
SORT_ROWS = 3072
ROW_TILE = 256
GATHER_SLOTS = 17
FFN_CHAINS = 2
COMBINE_CHUNK = 1024
ALWAYS_ROWS = 2560
COMBINE_TAIL = 512
COMBINE_SLOTS = 2


def _moe_sort_kernel(tiles_a, used_ref, xa_ref, xb_ref, pos_ref, xs_hbm, ybuf, osem):
    i = pl.program_id(0)
    last = pl.num_programs(0) - 1
    slot = lax.rem(i, 2)
    x = jnp.where(i < tiles_a, xa_ref[...], xb_ref[...])
    pos = pos_ref[...]
    tm = x.shape[0]
    used = used_ref[i]

    def out_copies(tile, of_slot, start):
        def one(r0, r1):
            row = pl.multiple_of(tile * SORT_ROWS + r0, tm)
            cp = pltpu.make_async_copy(ybuf.at[of_slot, r0:r1, :], xs_hbm.at[pl.ds(row, r1 - r0), :],
                                       osem.at[of_slot])
            if start:
                cp.start()
            else:
                cp.wait()

        one(0, ALWAYS_ROWS)
        for r0 in range(ALWAYS_ROWS, SORT_ROWS, tm):
            pl.when(r0 < used_ref[tile])(functools.partial(one, r0, r0 + tm))

    @pl.when(i >= 2)
    def _():
        out_copies(i - 2, slot, False)

    rows = lax.broadcasted_iota(jnp.int32, (tm, tm), 0).astype(F32).astype(BF16)
    one_bf = jnp.ones((tm, tm), BF16)

    def fill(blk):
        local = (pos - float(blk * tm)).astype(BF16)
        onehot = jnp.zeros((tm, tm), BF16)
        for k in range(TOP_K):
            onehot = jnp.where(rows == local[k:k + 1, :], one_bf, onehot)
        ybuf[slot, blk * tm:(blk + 1) * tm, :] = _dot(onehot, x).astype(BF16)

    for blk in range(SORT_ROWS // tm):
        if (blk + 1) * tm <= ALWAYS_ROWS:
            fill(blk)
        else:
            pl.when(blk * tm < used)(functools.partial(fill, blk))
    out_copies(i, slot, True)

    @pl.when(i == last)
    def _():
        out_copies(i, slot, False)

        @pl.when(i >= 1)
        def _():
            out_copies(i - 1, 1 - slot, False)


def _moe_sort(xm_a, xm_b, pos, used):
    d = xm_a.shape[1]
    nt, _, tm = pos.shape
    tiles_a = xm_a.shape[0] // tm
    grid_spec = pltpu.PrefetchScalarGridSpec(
        num_scalar_prefetch=1,
        grid=(nt,),
        in_specs=[pl.BlockSpec((tm, d), lambda i, u: (jnp.minimum(i, tiles_a - 1), 0)),
                  pl.BlockSpec((tm, d), lambda i, u: (jnp.maximum(i - tiles_a, 0), 0)),
                  pl.BlockSpec((None, TOP_K, tm), lambda i, u: (i, 0, 0))],
        out_specs=pl.BlockSpec(memory_space=pl.ANY),
        scratch_shapes=[pltpu.VMEM((2, SORT_ROWS, d), BF16),
                        pltpu.SemaphoreType.DMA((2,))])
    return pl.pallas_call(
        functools.partial(_moe_sort_kernel, tiles_a),
        out_shape=jax.ShapeDtypeStruct((nt * SORT_ROWS, d), BF16),
        grid_spec=grid_spec,
        compiler_params=pltpu.CompilerParams(dimension_semantics=("arbitrary",),
                                             vmem_limit_bytes=VMEM_LIMIT),
        name="moe_sort",
    )(used, xm_a, xm_b, pos)


def _moe_row_tiles(n_tokens):
    rows = n_tokens * TOP_K + (n_tokens // MOE_TILE) * N_EXPERTS * (SORT_ALIGN - 1) + N_EXPERTS * (ROW_TILE - 1)
    return -(-rows // ROW_TILE) + GATHER_SLOTS - 1


PLAN_CHUNK = 1280


def _int_dot_r(a, onehot):
    hi = jnp.floor(a * (1.0 / 256.0))
    return _dot(hi.astype(BF16), onehot) * 256.0 + _dot((a - hi * 256.0).astype(BF16), onehot)


def _int_dot_l(onehot, b):
    hi = jnp.floor(b * (1.0 / 256.0))
    return _dot(onehot, hi.astype(BF16)) * 256.0 + _dot(onehot, (b - hi * 256.0).astype(BF16))


def _moe_plan_kernel(cnt_ref, start_ref, src_ref, first_ref, tiles_ref, nu_ref, back_ref):
    nt, ne = cnt_ref.shape
    gpt = SORT_ROWS // SORT_ALIGN
    gpr = ROW_TILE // SORT_ALIGN
    gc = cnt_ref[...] * (1.0 / SORT_ALIGN)
    ls = start_ref[...] * (1.0 / SORT_ALIGN)

    def transpose(x):
        x = jnp.concatenate([x, jnp.zeros((nt, LANES - ne), F32)], axis=1)
        x = jnp.concatenate([x, jnp.zeros((LANES - nt, LANES), F32)], axis=0)
        return x.T[:ne, :nt]

    def tri(n, keep):
        return jnp.where(keep(lax.broadcasted_iota(jnp.int32, (n, n), 0),
                              lax.broadcasted_iota(jnp.int32, (n, n), 1)), 1.0, 0.0).astype(BF16)

    gc_t = transpose(gc)
    ls_t = transpose(ls)
    tot_c = jnp.broadcast_to(jnp.sum(gc_t, axis=1, keepdims=True), (ne, LANES))
    ptot_c = jnp.floor((tot_c + (gpr - 1)) * (1.0 / gpr)) * gpr
    gend_c = _int_dot_l(tri(ne, lambda r, c: c <= r), ptot_c)
    gstart_c = gend_c - ptot_c
    n_used = gend_c[ne - 1:ne, :] * (1.0 / gpr)
    nu_ref[...] = n_used.astype(jnp.int32)
    tot_r = jnp.sum(gc, axis=0, keepdims=True)
    ptot_r = jnp.floor((tot_r + (gpr - 1)) * (1.0 / gpr)) * gpr
    gstart_r = _int_dot_r(jnp.broadcast_to(ptot_r, (8, ne)), tri(ne, lambda r, c: r < c))
    cumex = _dot(tri(nt, lambda r, c: c < r), gc.astype(BF16))
    cumex_t = _dot(gc_t.astype(BF16), tri(nt, lambda r, c: r < c))
    tile_base = lax.broadcasted_iota(jnp.int32, (nt, ne), 0).astype(F32) * gpt + ls
    table = jnp.concatenate([cumex + gc, cumex, tile_base, gstart_r, jnp.broadcast_to(tot_r, (8, ne))], axis=0)

    e_iota = lax.broadcasted_iota(jnp.int32, (ne, PLAN_CHUNK), 0).astype(F32)
    for ch in range(src_ref.shape[1] // PLAN_CHUNK):
        g = (lax.broadcasted_iota(jnp.int32, (1, PLAN_CHUNK), 1) + ch * PLAN_CHUNK).astype(F32)
        eg = jnp.sum(jnp.where(gend_c[:, 0:1] <= g, 1.0, 0.0), axis=0, keepdims=True)
        picked = _int_dot_r(table, jnp.where(e_iota == eg, 1.0, 0.0).astype(BF16))
        cum_g, cumex_g, base_g = picked[0:nt], picked[nt:2 * nt], picked[2 * nt:3 * nt]
        u = g - picked[3 * nt:3 * nt + 1]
        in_tile = (cumex_g <= u) & (u < cum_g)
        src = jnp.sum(jnp.where(in_tile, base_g - cumex_g, 0.0), axis=0, keepdims=True) + u
        src = jnp.where(u < picked[3 * nt + 8:3 * nt + 9], src, 0.0)
        src_ref[:, ch * PLAN_CHUNK:(ch + 1) * PLAN_CHUNK] = src.astype(jnp.int32)

    first_ref[...] = (gstart_c * (1.0 / gpr)).astype(jnp.int32)
    tiles_ref[...] = (ptot_c * (1.0 / gpr)).astype(jnp.int32)

    lg = lax.broadcasted_iota(jnp.int32, (ne, back_ref.shape[1]), 1).astype(F32)
    for t in range(nt):
        first = ls_t[:, t:t + 1]
        inside = (first <= lg) & (lg < first + gc_t[:, t:t + 1])
        shift = gstart_c[:, 0:1] + cumex_t[:, t:t + 1] - first
        val = jnp.sum(jnp.where(inside, shift + lg, 0.0), axis=0, keepdims=True)
        back_ref[t:t + 1, :] = val.astype(jnp.int32)


def _moe_plan(cnt, start):
    nt, ne = cnt.shape
    row_tiles = _moe_row_tiles(nt * MOE_TILE)
    gpt = SORT_ROWS // SORT_ALIGN
    gpr = ROW_TILE // SORT_ALIGN
    n_src = -(-(row_tiles * gpr) // PLAN_CHUNK) * PLAN_CHUNK
    n_back = -(-gpt // LANES) * LANES
    src, first, tiles, nu, back = pl.pallas_call(
        _moe_plan_kernel,
        out_shape=(jax.ShapeDtypeStruct((1, n_src), jnp.int32),
                   jax.ShapeDtypeStruct((ne, LANES), jnp.int32),
                   jax.ShapeDtypeStruct((ne, LANES), jnp.int32),
                   jax.ShapeDtypeStruct((1, LANES), jnp.int32),
                   jax.ShapeDtypeStruct((nt, n_back), jnp.int32)),
        compiler_params=pltpu.CompilerParams(vmem_limit_bytes=VMEM_LIMIT),
        name="moe_plan",
    )(cnt, start)
    return nu[0, :1], first[:, 0], tiles[:, 0], src[0, :row_tiles * gpr], back[:, :gpt]


def _moe_experts_kernel(nu_ref, first_ref, tiles_ref, src_ref, xs_hbm, wg_ref, wu_ref, wd_ref, ys_hbm,
                        xbuf, ybuf, gsem, osem, wgu_s, wd_s):
    e = pl.program_id(0)
    n_used = nu_ref[0]
    gpr = ROW_TILE // SORT_ALIGN
    part = ROW_TILE // FFN_CHAINS

    def gather(tile, to_slot, j0=0, j1=gpr):
        for j in range(j0, j1):
            row = pl.multiple_of(src_ref[tile * gpr + j] * SORT_ALIGN, SORT_ALIGN)
            pltpu.make_async_copy(xs_hbm.at[pl.ds(row, SORT_ALIGN), :],
                                  xbuf.at[to_slot, j * SORT_ALIGN:(j + 1) * SORT_ALIGN, :],
                                  gsem.at[to_slot]).start(priority=j % 2)

    def drain(of_slot):
        for j in range(gpr):
            pltpu.make_async_copy(xs_hbm.at[0:SORT_ALIGN, :],
                                  xbuf.at[of_slot, j * SORT_ALIGN:(j + 1) * SORT_ALIGN, :], gsem.at[of_slot]).wait()

    def out_copy(tile, of_slot):
        row = pl.multiple_of(tile * ROW_TILE, ROW_TILE)
        return pltpu.make_async_copy(ybuf.at[of_slot], ys_hbm.at[pl.ds(row, ROW_TILE), :], osem.at[of_slot])

    @pl.when(e == 0)
    def _():
        for ahead in range(GATHER_SLOTS - 1):
            gather(ahead, ahead)

    wgu_s[:, :EXPERT_FF] = wg_ref[...].astype(BF16)
    wgu_s[:, EXPERT_FF:] = wu_ref[...].astype(BF16)
    wd_s[...] = wd_ref[...].astype(BF16)

    def row_tile(i, carry):
        r = first_ref[e] + i
        slot = lax.rem(r, GATHER_SLOTS)
        oslot = lax.rem(r, 2)
        next_slot = lax.rem(r + GATHER_SLOTS - 1, GATHER_SLOTS)
        drain(slot)

        @pl.when(r >= 2)
        def _():
            out_copy(r - 2, oslot).wait()

        abs_ = []
        for c in range(FFN_CHAINS):
            abs_.append(_dot(xbuf[slot, c * part:(c + 1) * part, :], wgu_s[...]))
            gather(r + GATHER_SLOTS - 1, next_slot, c * gpr // FFN_CHAINS, (c + 1) * gpr // FFN_CHAINS)
        hs = [(_silu(ab[:, :EXPERT_FF]) * ab[:, EXPERT_FF:]).astype(BF16) for ab in abs_]
        ys = [_dot(h, wd_s[...]).astype(BF16) for h in hs]
        for c in range(FFN_CHAINS):
            ybuf[oslot, c * part:(c + 1) * part, :] = ys[c]
        out_copy(r, oslot).start()
        return carry

    lax.fori_loop(0, tiles_ref[e], row_tile, 0)

    @pl.when(e == pl.num_programs(0) - 1)
    def _():
        for ahead in range(GATHER_SLOTS - 1):
            drain(lax.rem(n_used + ahead, GATHER_SLOTS))
        out_copy(n_used - 1, lax.rem(n_used - 1, 2)).wait()

        @pl.when(n_used >= 2)
        def _():
            out_copy(n_used - 2, lax.rem(n_used, 2)).wait()


def _moe_experts(n_used, first, tiles, src, xs, wg, wu, wd, row_tiles):
    d = xs.shape[-1]
    ne = wg.shape[0]
    w_map = lambda e, nu, fi, ti, sr: (e, 0, 0)
    grid_spec = pltpu.PrefetchScalarGridSpec(
        num_scalar_prefetch=4,
        grid=(ne,),
        in_specs=[pl.BlockSpec(memory_space=pl.ANY),
                  pl.BlockSpec((None, d, EXPERT_FF), w_map),
                  pl.BlockSpec((None, d, EXPERT_FF), w_map),
                  pl.BlockSpec((None, EXPERT_FF, d), w_map)],
        out_specs=pl.BlockSpec(memory_space=pl.ANY),
        scratch_shapes=[pltpu.VMEM((GATHER_SLOTS, ROW_TILE, d), BF16),
                        pltpu.VMEM((2, ROW_TILE, d), BF16),
                        pltpu.SemaphoreType.DMA((GATHER_SLOTS,)),
                        pltpu.SemaphoreType.DMA((2,)),
                        pltpu.VMEM((d, 2 * EXPERT_FF), BF16),
                        pltpu.VMEM((EXPERT_FF, d), BF16)])
    return pl.pallas_call(
        _moe_experts_kernel,
        out_shape=jax.ShapeDtypeStruct((row_tiles * ROW_TILE, d), BF16),
        grid_spec=grid_spec,
        compiler_params=pltpu.CompilerParams(dimension_semantics=("arbitrary",),
                                             vmem_limit_bytes=VMEM_LIMIT),
        name="moe_experts",
    )(n_used, first, tiles, src, xs, wg, wu, wd)


def _moe_combine_kernel(back_ref, used_ref, ys_hbm, pos_ref, wts_ref, xm_ref, x1_ref, g2_ref, fg_ref,
                        swg_ref, swu_ref, swd_ref, o_ref, buf, sem, acc_ref):
    i = pl.program_id(0)
    gpt = SORT_ROWS // SORT_ALIGN
    slot = lax.rem(i, COMBINE_SLOTS)
    ahead = COMBINE_SLOTS - 1
    always = ALWAYS_ROWS
    tail = range(always, SORT_ROWS, COMBINE_TAIL)

    def copies(tile, of_slot, g0, g1, start):
        for g in range(g0, g1):
            row = pl.multiple_of(back_ref[tile * gpt + g] * SORT_ALIGN, SORT_ALIGN) if start else 0
            cp = pltpu.make_async_copy(ys_hbm.at[pl.ds(row, SORT_ALIGN), :],
                                       buf.at[of_slot, g * SORT_ALIGN:(g + 1) * SORT_ALIGN, :], sem.at[of_slot])
            if start:
                cp.start(priority=g % 2)
            else:
                cp.wait()

    def transfer(tile, of_slot, start):
        copies(tile, of_slot, 0, always // SORT_ALIGN, start)
        for c0 in tail:
            pl.when(c0 < used_ref[tile])(functools.partial(
                copies, tile, of_slot, c0 // SORT_ALIGN, (c0 + COMBINE_TAIL) // SORT_ALIGN, start))

    @pl.when(i == 0)
    def _():
        for first in range(min(ahead, buf.shape[0])):
            pl.when(first < pl.num_programs(0))(functools.partial(transfer, first, first, True))

    @pl.when(i + ahead < pl.num_programs(0))
    def _():
        transfer(i + ahead, lax.rem(i + ahead, COMBINE_SLOTS), True)

    x = xm_ref[...]
    tm = x.shape[0]
    pad = jnp.zeros((LANES - TOP_K, tm), F32)
    pos_t = jnp.concatenate([pos_ref[...], pad], axis=0).T
    wts_t = jnp.concatenate([wts_ref[...], pad], axis=0).T
    blk_b, loc_b, wts_b = [], [], []
    for k in range(TOP_K):
        p = jnp.broadcast_to(pos_t[:, k:k + 1], (tm, LANES))
        blk = jnp.floor(p * (1.0 / tm))
        two = lambda v: jnp.concatenate([v.astype(BF16)] * (tm // LANES), axis=1)
        blk_b.append(two(blk))
        loc_b.append(two(p - blk * tm))
        wts_b.append(two(jnp.broadcast_to(wts_t[:, k:k + 1], (tm, LANES))))
    shared = _dot((_silu(_dot(x, swg_ref[...])) * _dot(x, swu_ref[...])).astype(BF16), swd_ref[...])
    transfer(i, slot, False)
    lane = lax.broadcasted_iota(jnp.int32, (tm, tm), 1).astype(F32).astype(BF16)
    zero = jnp.zeros((tm, tm), BF16)
    nowhere = jnp.full((tm, tm), -1.0, BF16)

    def apply(c0, width):
        blocks = []
        for b0 in range(c0, c0 + width, tm):
            comb = zero
            for k in range(TOP_K):
                loc = jnp.where(blk_b[k] == float(b0 // tm), loc_b[k], nowhere)
                comb = jnp.where(lane == loc, wts_b[k], comb)
            blocks.append(comb)
        return _dot(jnp.concatenate(blocks, axis=1), buf[slot, c0:c0 + width, :])

    routed = shared
    for c0 in range(0, always, COMBINE_CHUNK):
        routed = routed + apply(c0, min(COMBINE_CHUNK, always - c0))
    acc_ref[...] = routed
    for c0 in tail:
        @pl.when(c0 < used_ref[i])
        def _(c0=c0):
            acc_ref[...] += apply(c0, COMBINE_TAIL)
    y = x1_ref[...] + g2_ref[...] * acc_ref[...]
    o_ref[...] = _rms_norm(y, fg_ref[...])


def _moe_combine(back, used, ys, pos, wts, xm, x1, g2, final_g, swg, swu, swd, *, tiles_per_mod):
    n, d = xm.shape
    tm = pos.shape[-1]
    nt = n // tm
    gpt = SORT_ROWS // SORT_ALIGN
    row = lambda i, bk, us: (i, 0)
    full = lambda i, bk, us: (0, 0)
    tile = lambda i, bk, us: (i, 0, 0)
    mod_map = lambda i, bk, us: (i // tiles_per_mod, 0, 0)
    grid_spec = pltpu.PrefetchScalarGridSpec(
        num_scalar_prefetch=2,
        grid=(nt,),
        in_specs=[pl.BlockSpec(memory_space=pl.ANY),
                  pl.BlockSpec((None, TOP_K, tm), tile),
                  pl.BlockSpec((None, TOP_K, tm), tile),
                  pl.BlockSpec((tm, d), row),
                  pl.BlockSpec((tm, d), row),
                  pl.BlockSpec((None, 1, d), mod_map),
                  pl.BlockSpec((1, d), full),
                  pl.BlockSpec((d, SHARED_FF), full),
                  pl.BlockSpec((d, SHARED_FF), full),
                  pl.BlockSpec((SHARED_FF, d), full)],
        out_specs=pl.BlockSpec((tm, d), row),
        scratch_shapes=[pltpu.VMEM((COMBINE_SLOTS, SORT_ROWS, d), BF16),
                        pltpu.SemaphoreType.DMA((COMBINE_SLOTS,)),
                        pltpu.VMEM((tm, d), F32)])
    return pl.pallas_call(
        _moe_combine_kernel,
        out_shape=jax.ShapeDtypeStruct((n, d), F32),
        grid_spec=grid_spec,
        compiler_params=pltpu.CompilerParams(dimension_semantics=("arbitrary",),
                                             vmem_limit_bytes=VMEM_LIMIT),
        name="moe_combine",
    )(back, used, ys, pos.reshape(nt, TOP_K, tm), wts.reshape(nt, TOP_K, tm), xm, x1, g2, final_g, swg, swu, swd)


def _mix(x, mods, p, attn_fn, s0=None):
    sh1, sc1, g1, sh2, sc2, _ = mods
    gla_in, lora, q_s, k_s, v_s = _inproj(x, p["norm_attn_g"], sh1, sc1, p["w_gla"], p["w_lora"], p["w_swa"])
    if s0 is None:
        gla_out, s_f, s_b = _gla(gla_in, lora, p["waf"], p["baf"], p["wab"], p["bab"], p["gla_norm_g"])
    else:
        gla_out, s_f, s_b = _gla(gla_in, lora, p["waf"], p["baf"], p["wab"], p["bab"], p["gla_norm_g"],
                                 s0[0], s0[1])
    att_out = attn_fn(q_s, k_s, v_s)
    routed = _outproj(gla_out, att_out, x, p["w_out"], g1, sh2, sc2, p["norm_ffn_g"],
                      p["rw_cat"], p["rw_hi"], p["rbias"])
    return routed, k_s, v_s, s_f, s_b


def _moe(streams, p):
    d = D_MODEL
    (ra, _), (rb, _) = streams
    n_tiles = [r[1].shape[0] * r[1].shape[1] // MOE_TILE for r, _ in streams]
    pos_all = jnp.concatenate([r[2].reshape(-1, TOP_K, MOE_TILE) for r, _ in streams], axis=0)
    cnt_all = jnp.concatenate([r[4].reshape(-1, LANES) for r, _ in streams], axis=0)[:, :N_EXPERTS]
    start_all = jnp.concatenate([r[5].reshape(-1, LANES) for r, _ in streams], axis=0)[:, :N_EXPERTS]
    used = (start_all[:, -1] + cnt_all[:, -1]).astype(jnp.int32)
    xs = _moe_sort(ra[1].reshape(-1, d), rb[1].reshape(-1, d), pos_all, used)
    n_used, first, tiles, src, back = _moe_plan(cnt_all, start_all)
    ys = _moe_experts(n_used, first, tiles, src, xs, p["wg"], p["wu"], p["wd"],
                      _moe_row_tiles(cnt_all.shape[0] * MOE_TILE))
    outs = []
    tile0 = 0
    for ((x1, xm, pos, wts, cnt, start), g2), nt in zip(streams, n_tiles):
        b, t, _ = x1.shape
        tiles_per_mod = (t // MOE_TILE) if g2.shape[0] > 1 else nt
        y = _moe_combine(back[tile0:tile0 + nt].reshape(-1), used[tile0:tile0 + nt], ys, pos, wts,
                         xm.reshape(-1, d), x1.reshape(-1, d), g2, p["final_norm_g"],
                         p["swg"], p["swu"], p["swd"], tiles_per_mod=tiles_per_mod)
        outs.append(y.reshape(b, t, d))
        tile0 += nt
    return outs


def kernel(x_prompt, x_sample, c, cache_swa_k, cache_swa_v, state_gla_fwd, state_gla_bwd, c_ctx, w_ada, b_ada, norm_attn_g, norm_ffn_g, w_in, gla_wa_f, gla_ba_f, gla_wa_b, gla_ba_b, gla_norm_g, swa_sink, w_out, router_w, router_bias, exp_w_gate, exp_w_up, exp_w_down, sh_w_gate, sh_w_up, sh_w_down, final_norm_g):
    l = 0
    d = D_MODEL
    nb_ctx, t_ctx, _ = x_prompt.shape
    nb_lat, t_lat, _ = x_sample.shape

    pad = jnp.zeros((8 - 1 - nb_lat, d), F32)
    cond8 = jnp.concatenate([c_ctx[None, :], c, pad], axis=0)
    mod = _adaln(cond8, w_ada[l], b_ada[l][None, :])
    mods_ctx = [mod[0:1, i * d:(i + 1) * d][:, None, :] for i in range(6)]
    mods_lat = [mod[1:1 + nb_lat, i * d:(i + 1) * d][:, None, :] for i in range(6)]

    zeros_lora = jnp.zeros((GLA_LORA, GLA_QK), F32)
    rw = router_w[l]
    rw_hi = rw.astype(BF16)
    rw_lo = (rw - rw_hi.astype(F32)).astype(BF16)
    n_gla = 2 * GLA_QK + 2 * GLA_V
    p = {
        "norm_attn_g": norm_attn_g[l][None, :],
        "norm_ffn_g": norm_ffn_g[l][None, :],
        "final_norm_g": final_norm_g[None, :],
        "w_gla": w_in[l][:, :n_gla].astype(BF16),
        "w_lora": w_in[l][:, n_gla:n_gla + 2 * GLA_LORA].astype(BF16),
        "w_swa": w_in[l][:, n_gla + 2 * GLA_LORA:].astype(BF16),
        "waf": jnp.concatenate([gla_wa_f[l], zeros_lora], axis=0).astype(BF16),
        "wab": jnp.concatenate([zeros_lora, gla_wa_b[l]], axis=0).astype(BF16),
        "baf": gla_ba_f[l][None, :],
        "bab": gla_ba_b[l][None, :],
        "gla_norm_g": gla_norm_g[l][None, :],
        "w_out": w_out[l].astype(BF16),
        "rw_cat": jnp.concatenate([rw_hi, rw_lo], axis=1),
        "rw_hi": rw_hi,
        "rbias": router_bias[l][:, None],
        "wg": exp_w_gate[l], "wu": exp_w_up[l], "wd": exp_w_down[l],
        "swg": sh_w_gate[l].astype(BF16), "swu": sh_w_up[l].astype(BF16),
        "swd": sh_w_down[l].astype(BF16),
    }
    sink = swa_sink[l]

    routed_ctx, k_c, v_c, s_f, s_b = _mix(x_prompt, mods_ctx, p, functools.partial(_attn_ctx, sink))

    cos, sin_lo, sin_hi = _rope_tables(t_lat)
    kc = cache_swa_k[:, l].reshape(nb_lat, -1, SWA_KV)
    vc = cache_swa_v[:, l].reshape(nb_lat, -1, SWA_KV)
    lat_attn = lambda q, k, v: _attn_lat(sink, q, k, v, kc, vc, cos, sin_lo, sin_hi)
    s0 = (state_gla_fwd[:, l].reshape(nb_lat, GLA_QK, GLA_DV),
          state_gla_bwd[:, l].reshape(nb_lat, GLA_QK, GLA_DV))
    routed_lat, _, _, _, _ = _mix(x_sample, mods_lat, p, lat_attn, s0)
    y_prompt, y_sample = _moe([(routed_ctx, mods_ctx[5]), (routed_lat, mods_lat[5])], p)

    new_k = k_c.reshape(nb_ctx, 1, t_ctx, SWA_KV_HEADS, SWA_HEAD_DIM)
    new_v = v_c.reshape(nb_ctx, 1, t_ctx, SWA_KV_HEADS, SWA_HEAD_DIM)
    new_sf = s_f.reshape(nb_ctx, 1, GLA_HEADS, GLA_DK, GLA_DV)
    new_sb = s_b.reshape(nb_ctx, 1, GLA_HEADS, GLA_DK, GLA_DV)
    return (y_prompt, y_sample, new_k, new_v, new_sf, new_sb)
```

```python
import functools

import jax
import jax.numpy as jnp
from jax import lax
from jax.experimental import pallas as pl
from jax.experimental.pallas import tpu as pltpu

F32 = jnp.float32
BF16 = jnp.bfloat16

D_MODEL = 1024
GLA_HEADS = 4
GLA_DK = 64
GLA_DV = 128
GLA_LORA = 16
GLA_GATE_NORM = 16.0
GLA_CHUNK = 64
GLA_QK = GLA_HEADS * GLA_DK
GLA_V = GLA_HEADS * GLA_DV
SWA_HEAD_DIM = 64
SWA_HEADS = 8
SWA_KV_HEADS = 2
SWA_Q = SWA_HEADS * SWA_HEAD_DIM
SWA_KV = SWA_KV_HEADS * SWA_HEAD_DIM
ATTN_BLOCK = 128
GRID_W = 64
ROPE_BASE = 10000.0
N_EXPERTS = 64
TOP_K = 8
N_EXPERT_GROUPS = 8
TOPK_GROUPS = 4
EXPERT_FF = 128
SHARED_FF = 256
ROUTED_SCALE = 2.5
EPS = 1e-6

LANES = 128
VMEM_LIMIT = 56 * 1024 * 1024

NEG_INF = float("-inf")


def _dot(a, b):
    return jnp.dot(a, b, preferred_element_type=F32)


def _dot_nt(a, b):
    return lax.dot_general(a, b, (((1,), (1,)), ((), ())), preferred_element_type=F32)


def _split_hi_lo(x):
    hi = x.astype(BF16)
    lo = (x - hi.astype(F32)).astype(BF16)
    return hi, lo


def _sigmoid(x):
    return 1.0 / (1.0 + jnp.exp(-x))


def _silu(x):
    return x * _sigmoid(x)


def _rms_norm(x, g):
    ms = jnp.mean(x * x, axis=-1, keepdims=True)
    return x * lax.rsqrt(ms + EPS) * g


def _adaln_kernel(c_ref, w_ref, b_ref, o_ref):
    a_hi, a_lo = _split_hi_lo(_silu(c_ref[...]))
    w_hi, w_lo = _split_hi_lo(w_ref[...])
    o_ref[...] = _dot(a_hi, w_hi) + _dot(a_lo, w_hi) + _dot(a_hi, w_lo) + b_ref[...]


def _adaln(cond8, w_ada, b_ada):
    n = w_ada.shape[1]
    tn = 1536
    return pl.pallas_call(
        _adaln_kernel,
        out_shape=jax.ShapeDtypeStruct((8, n), F32),
        grid=(n // tn,),
        in_specs=[pl.BlockSpec((8, D_MODEL), lambda j: (0, 0)),
                  pl.BlockSpec((D_MODEL, tn), lambda j: (0, j)),
                  pl.BlockSpec((1, tn), lambda j: (0, j))],
        out_specs=pl.BlockSpec((8, tn), lambda j: (0, j)),
        compiler_params=pltpu.CompilerParams(dimension_semantics=("arbitrary",),
                                             vmem_limit_bytes=VMEM_LIMIT),
        name="adaln",
    )(cond8, w_ada, b_ada)


def _inproj_kernel(x_ref, g_ref, sh_ref, sc_ref, wg_ref, wl_ref, ws_ref,
                   gla_ref, lora_ref, q_ref, k_ref, v_ref):
    bb, tb, d = x_ref.shape
    x = x_ref[...].reshape(bb * tb, d)
    h = _rms_norm(x, g_ref[...]) * (1.0 + sc_ref[...]) + sh_ref[...]
    hb = h.astype(BF16)
    gla_ref[...] = _dot(hb, wg_ref[...]).reshape(gla_ref.shape)
    lora_ref[...] = _dot(hb, wl_ref[...]).reshape(lora_ref.shape)
    s = _dot(hb, ws_ref[...])
    q_ref[...] = s[:, :SWA_Q].reshape(q_ref.shape)
    k_ref[...] = s[:, SWA_Q:SWA_Q + SWA_KV].reshape(k_ref.shape)
    v_ref[...] = s[:, SWA_Q + SWA_KV:].reshape(v_ref.shape)


INPROJ_TILE = 1024


def _inproj(x, g, sh, sc, w_gla, w_lora, w_swa):
    b, t, d = x.shape
    nmod = sh.shape[0]
    tb = min(t, INPROJ_TILE)
    bb = INPROJ_TILE // tb if nmod == 1 else 1
    mod_map = (lambda i, j: (i, 0, 0)) if nmod > 1 else (lambda i, j: (0, 0, 0))
    row = lambda i, j: (i, j, 0)
    full = lambda i, j: (0, 0)
    n_gla = w_gla.shape[1]
    n_lora = w_lora.shape[1]
    return pl.pallas_call(
        _inproj_kernel,
        out_shape=(jax.ShapeDtypeStruct((b, t, n_gla), F32),
                   jax.ShapeDtypeStruct((b, t, n_lora), F32),
                   jax.ShapeDtypeStruct((b, t, SWA_Q), F32),
                   jax.ShapeDtypeStruct((b, t, SWA_KV), F32),
                   jax.ShapeDtypeStruct((b, t, SWA_KV), F32)),
        grid=(b // bb, t // tb),
        in_specs=[pl.BlockSpec((bb, tb, d), row),
                  pl.BlockSpec((1, d), full),
                  pl.BlockSpec((None, 1, d), mod_map),
                  pl.BlockSpec((None, 1, d), mod_map),
                  pl.BlockSpec((d, n_gla), full, pipeline_mode=pl.Buffered(1)),
                  pl.BlockSpec((d, n_lora), full, pipeline_mode=pl.Buffered(1)),
                  pl.BlockSpec((d, w_swa.shape[1]), full, pipeline_mode=pl.Buffered(1))],
        out_specs=(pl.BlockSpec((bb, tb, n_gla), row),
                   pl.BlockSpec((bb, tb, n_lora), row),
                   pl.BlockSpec((bb, tb, SWA_Q), row),
                   pl.BlockSpec((bb, tb, SWA_KV), row),
                   pl.BlockSpec((bb, tb, SWA_KV), row)),
        compiler_params=pltpu.CompilerParams(dimension_semantics=("arbitrary", "arbitrary"),
                                             vmem_limit_bytes=VMEM_LIMIT),
        name="inproj",
    )(x, g, sh, sc, w_gla, w_lora, w_swa)


SCAN_UNROLL = 4
OUT_UNROLL = 4


def _log_sigmoid(x):
    return jnp.minimum(x, 0.0) - jnp.log(1.0 + jnp.exp(-jnp.abs(x)))


def _heads_to_rows(x):
    return jnp.concatenate([x[:, h * LANES:(h + 1) * LANES] for h in range(GLA_HEADS)], axis=0)


def _rows_to_heads(x, c):
    return jnp.concatenate([x[h * c:(h + 1) * c, :] for h in range(GLA_HEADS)], axis=1)


def _gla_kernel(has_init, q_ref, k_ref, v_ref, g_ref, lora_ref, waf_ref, baf_ref, wab_ref, bab_ref,
                ng_ref, *rest):
    if has_init:
        s0f_ref, s0b_ref, *rest = rest
    (out_ref, sf_ref, sb_ref, laf_ref, lab_ref, oacc_ref, qtf_ref, qtb_ref, saf_ref, sab_ref,
     stf_ref, stb_ref) = rest
    t = q_ref.shape[0]
    c = GLA_CHUNK
    n = t // c
    hc = GLA_HEADS * c

    lora = lora_ref[...].astype(BF16)
    laf_ref[...] = _log_sigmoid(_dot(lora, waf_ref[...]) + baf_ref[...]) * (1.0 / GLA_GATE_NORM)
    lab_ref[...] = _log_sigmoid(_dot(lora, wab_ref[...]) + bab_ref[...]) * (1.0 / GLA_GATE_NORM)

    if has_init:
        stf_ref[...] = s0f_ref[...].T
        stb_ref[...] = s0b_ref[...].T
    else:
        stf_ref[...] = jnp.zeros_like(stf_ref)
        stb_ref[...] = jnp.zeros_like(stb_ref)
    oacc_ref[...] = jnp.zeros_like(oacc_ref)

    r64 = lax.broadcasted_iota(jnp.int32, (c, c), 0)
    c64 = lax.broadcasted_iota(jnp.int32, (c, c), 1)
    tri_f = jnp.where(c64 <= r64, 1.0, 0.0).astype(BF16)
    tri_b = jnp.where(c64 >= r64, 1.0, 0.0).astype(BF16)
    rr = lax.broadcasted_iota(jnp.int32, (hc, hc), 0)
    cc = lax.broadcasted_iota(jnp.int32, (hc, hc), 1)
    same_head = (rr >> 6) == (cc >> 6)
    keep_f = same_head & ((rr & (c - 1)) >= (cc & (c - 1)))
    keep_b = same_head & ((rr & (c - 1)) <= (cc & (c - 1)))
    head_mask = jnp.where(same_head, 1.0, 0.0).astype(BF16)
    norm_g = ng_ref[...]

    def chunk_rows(ci):
        return pl.ds(pl.multiple_of(ci * c, c), c)

    def tile_heads(x):
        x4 = jnp.concatenate([x] * GLA_HEADS, axis=0)
        return jnp.where(same_head, x4, 0.0).astype(BF16)

    def scan_step(i, carry):
        dirs = []
        for u in range(SCAN_UNROLL):
            dirs += [(SCAN_UNROLL * i + u, laf_ref, tri_f, keep_f, c - 1, stf_ref, saf_ref, qtf_ref),
                     (n - 1 - SCAN_UNROLL * i - u, lab_ref, tri_b, keep_b, 0, stb_ref, sab_ref, qtb_ref)]
        cums = []
        for ci, la_ref, tri, _, _, _, _, _ in dirs:
            la_hi, la_lo = _split_hi_lo(la_ref[chunk_rows(ci), :])
            cums.append(_dot(tri, la_hi) + _dot(tri, la_lo))
        ops = []
        for (ci, _, _, _, last_row, _, _, qt_ref), cum in zip(dirs, cums):
            sl = chunk_rows(ci)
            tot = cum[last_row:last_row + 1, :]
            kc = k_ref[sl, :]
            qt = q_ref[sl, :] * (GLA_DK ** -0.5) * jnp.exp(cum)
            qt_ref[sl, :] = qt.astype(BF16)
            v_rows = _heads_to_rows(v_ref[sl, :])
            ops.append((tot, tile_heads(qt), tile_heads(kc * jnp.exp(-cum)),
                        tile_heads(kc * jnp.exp(tot - cum)), v_rows))
        atts = [_dot_nt(q4, k4) for _, q4, k4, _, _ in ops]
        incs = []
        for (_, _, _, keep, _, _, _, _), (_, _, _, kd4, v_rows), att in zip(dirs, ops, atts):
            att = jnp.where(keep, att, 0.0).astype(BF16)
            incs.append((_dot(att, v_rows.astype(BF16)), _dot(v_rows.T.astype(BF16), kd4)))
        for (ci, _, _, _, _, st_ref, snap_ref, _), (tot, _, _, _, _), (o_intra, st_inc) in zip(dirs, ops, incs):
            oacc_ref[ci] += o_intra
            st = st_ref[...]
            snap_ref[ci] = st.astype(BF16)
            st_ref[...] = jnp.exp(tot) * st + st_inc
        return carry

    def tile_heads_bf16(x):
        return jnp.concatenate([x] * GLA_HEADS, axis=0) * head_mask

    def out_step(i, carry):
        chunks = [OUT_UNROLL * i + u for u in range(OUT_UNROLL)]
        inter = []
        for ci in chunks:
            sl = chunk_rows(ci)
            q4 = jnp.concatenate([tile_heads_bf16(qtf_ref[sl, :]), tile_heads_bf16(qtb_ref[sl, :])], axis=1)
            st = jnp.concatenate([saf_ref[ci], sab_ref[ci]], axis=1)
            inter.append(_dot_nt(q4, st))
        for ci, o_inter in zip(chunks, inter):
            sl = chunk_rows(ci)
            on = _rms_norm(oacc_ref[ci] + o_inter, norm_g)
            gate = _silu(_heads_to_rows(g_ref[sl, :]))
            out_ref[sl, :] = _rows_to_heads(on * gate, c)
        return carry

    lax.fori_loop(0, n // SCAN_UNROLL, scan_step, 0)
    lax.fori_loop(0, n // OUT_UNROLL, out_step, 0)
    sf_ref[...] = stf_ref[...].T
    sb_ref[...] = stb_ref[...].T


def _gla(gla_in, lora, waf, baf, wab, bab, norm_g, s0f=None, s0b=None):
    b, t, _ = gla_in.shape
    has_init = s0f is not None
    n = t // GLA_CHUNK
    bmap = lambda i: (i, 0, 0)
    full = lambda i: (0, 0)
    in_specs = [pl.BlockSpec((None, t, GLA_QK), lambda i: (i, 0, 0)),
                pl.BlockSpec((None, t, GLA_QK), lambda i: (i, 0, 1)),
                pl.BlockSpec((None, t, GLA_V), lambda i: (i, 0, 1)),
                pl.BlockSpec((None, t, GLA_V), lambda i: (i, 0, 2)),
                pl.BlockSpec((None, t, 2 * GLA_LORA), bmap),
                pl.BlockSpec((2 * GLA_LORA, GLA_QK), full),
                pl.BlockSpec((1, GLA_QK), full),
                pl.BlockSpec((2 * GLA_LORA, GLA_QK), full),
                pl.BlockSpec((1, GLA_QK), full),
                pl.BlockSpec((1, GLA_DV), full)]
    args = [gla_in, gla_in, gla_in, gla_in, lora, waf, baf, wab, bab, norm_g]
    if has_init:
        in_specs += [pl.BlockSpec((None, GLA_QK, GLA_DV), bmap)] * 2
        args += [s0f, s0b]
    return pl.pallas_call(
        functools.partial(_gla_kernel, has_init),
        out_shape=(jax.ShapeDtypeStruct((b, t, GLA_V), F32),
                   jax.ShapeDtypeStruct((b, GLA_QK, GLA_DV), F32),
                   jax.ShapeDtypeStruct((b, GLA_QK, GLA_DV), F32)),
        grid=(b,),
        in_specs=in_specs,
        out_specs=(pl.BlockSpec((None, t, GLA_V), bmap),
                   pl.BlockSpec((None, GLA_QK, GLA_DV), bmap),
                   pl.BlockSpec((None, GLA_QK, GLA_DV), bmap)),
        scratch_shapes=[pltpu.VMEM((t, GLA_QK), F32),
                        pltpu.VMEM((t, GLA_QK), F32),
                        pltpu.VMEM((n, GLA_HEADS * GLA_CHUNK, GLA_DV), F32),
                        pltpu.VMEM((t, GLA_QK), BF16),
                        pltpu.VMEM((t, GLA_QK), BF16),
                        pltpu.VMEM((n, GLA_DV, GLA_QK), BF16),
                        pltpu.VMEM((n, GLA_DV, GLA_QK), BF16),
                        pltpu.VMEM((GLA_DV, GLA_QK), F32),
                        pltpu.VMEM((GLA_DV, GLA_QK), F32)],
        compiler_params=pltpu.CompilerParams(dimension_semantics=("arbitrary",),
                                             vmem_limit_bytes=VMEM_LIMIT),
        name="gla",
    )(*args)


def _dup_groups(x):
    lo = lax.broadcasted_iota(jnp.int32, x.shape, 1) < SWA_HEAD_DIM
    xr = pltpu.roll(x, SWA_HEAD_DIM, axis=1)
    return jnp.where(lo, x, xr), jnp.where(lo, xr, x)


def _pairs_attention(qps, sinks, k_dups, vt_dups, masks):
    nq = qps[0].shape[0]
    lo = lax.broadcasted_iota(jnp.int32, (nq, LANES), 1) < SWA_HEAD_DIM
    even = lax.broadcasted_iota(jnp.int32, (1, 2 * nq), 1) < nq
    scores = []
    for qp, k_dup in zip(qps, k_dups):
        q2 = jnp.concatenate([jnp.where(lo, qp, 0.0), jnp.where(lo, 0.0, qp)], axis=0).astype(BF16)
        scores.append(_dot_nt(k_dup, q2))
    probs = []
    for s, (sink_even, sink_odd), mask in zip(scores, sinks, masks):
        if mask is not None:
            s = jnp.where(mask, s, NEG_INF)
        sink = jnp.where(even, sink_even, sink_odd)
        m = jnp.maximum(jnp.max(s, axis=0, keepdims=True), sink)
        p = jnp.exp(s - m)
        denom = jnp.sum(p, axis=0, keepdims=True) + jnp.exp(sink - m)
        probs.append((p.astype(BF16), 1.0 / denom))
    outs = []
    for (p, rdenom), vt_dup in zip(probs, vt_dups):
        o = _dot(vt_dup, p) * rdenom
        outs.append(jnp.concatenate([o[:SWA_HEAD_DIM, :nq], o[SWA_HEAD_DIM:, nq:]], axis=0).T)
    return outs


CTX_BATCH = 4


def _attn_ctx_kernel(sink_ref, q_ref, k_ref, v_ref, o_ref):
    scale = SWA_HEAD_DIM ** -0.5
    pairs = range(SWA_HEADS // 2)
    items = [(bb, pr) for bb in range(q_ref.shape[0]) for pr in pairs]
    kd = [[x.astype(BF16) for x in _dup_groups(k_ref[bb])] for bb in range(q_ref.shape[0])]
    vt = [[x.T.astype(BF16) for x in _dup_groups(v_ref[bb])] for bb in range(q_ref.shape[0])]
    outs = _pairs_attention([q_ref[bb, :, pr * LANES:(pr + 1) * LANES] * scale for bb, pr in items],
                            [(sink_ref[2 * pr], sink_ref[2 * pr + 1]) for _, pr in items],
                            [kd[bb][pr // 2] for bb, pr in items], [vt[bb][pr // 2] for bb, pr in items],
                            [None] * len(items))
    for (bb, pr), out in zip(items, outs):
        o_ref[bb, :, pr * LANES:(pr + 1) * LANES] = out


def _attn_ctx(sink, q, k, v):
    b, t, _ = q.shape
    bmap = lambda i: (i, 0, 0)
    return pl.pallas_call(
        _attn_ctx_kernel,
        out_shape=jax.ShapeDtypeStruct((b, t, SWA_Q), F32),
        grid=(b // CTX_BATCH,),
        in_specs=[pl.BlockSpec(memory_space=pltpu.SMEM),
                  pl.BlockSpec((CTX_BATCH, t, SWA_Q), bmap),
                  pl.BlockSpec((CTX_BATCH, t, SWA_KV), bmap),
                  pl.BlockSpec((CTX_BATCH, t, SWA_KV), bmap)],
        out_specs=pl.BlockSpec((CTX_BATCH, t, SWA_Q), bmap),
        compiler_params=pltpu.CompilerParams(dimension_semantics=("arbitrary",),
                                             vmem_limit_bytes=VMEM_LIMIT),
        name="attn_ctx",
    )(sink, q, k, v)


LAT_BLOCKS = 2


def _rope(x, cos, sin_lo, sin_hi):
    return x * cos + pltpu.roll(x, LANES - 16, axis=1) * sin_lo + pltpu.roll(x, 16, axis=1) * sin_hi


def _attn_lat_kernel(sink_ref, q_ref, k_ref, v_ref, kc_ref, vc_ref, cos_ref, sl_ref, sh_ref,
                     o_ref, kw_ref, vw_ref):
    t = q_ref.shape[0]
    ab = ATTN_BLOCK
    nb = t // ab
    scale = SWA_HEAD_DIM ** -0.5

    k_rot = _dup_groups(_rope(k_ref[...], cos_ref[...], sl_ref[...], sh_ref[...]))
    v_dup = _dup_groups(v_ref[...])
    zeros = jnp.zeros((ab, LANES), BF16)
    for grp in range(SWA_KV_HEADS):
        kw_ref[grp, 0:ab, :] = zeros
        kw_ref[grp, ab:ab + t, :] = k_rot[grp].astype(BF16)
        kw_ref[grp, ab + t:, :] = zeros
        vw_ref[grp, 0] = zeros
        for blk in range(nb):
            vw_ref[grp, blk + 1] = v_dup[grp][blk * ab:(blk + 1) * ab, :].T.astype(BF16)
        vw_ref[grp, nb + 1] = zeros
    kc = [x.astype(BF16) for x in _dup_groups(kc_ref[...])]
    vct = [x.T.astype(BF16) for x in _dup_groups(vc_ref[...])]
    lc = kc_ref.shape[0]

    key = lax.broadcasted_iota(jnp.int32, (lc + 3 * ab, 2 * ab), 0) - lc
    tq = lax.broadcasted_iota(jnp.int32, (lc + 3 * ab, 2 * ab), 1) & (ab - 1)
    band = (key < 0) | (jnp.abs(tq + ab - key) <= ab)

    def block(it, carry):
        pairs = range(SWA_HEADS // 2)
        qps, sinks, k_dups, vt_dups, masks, places = [], [], [], [], [], []
        for u in range(LAT_BLOCKS):
            nq = it * LAT_BLOCKS + u
            row0 = pl.multiple_of(nq * ab, ab)
            s_abs = key + (nq - 1) * ab
            mask = band & ((key < 0) | ((s_abs >= 0) & (s_abs < t)))
            cos = cos_ref[pl.ds(row0, ab), :]
            s_lo = sl_ref[pl.ds(row0, ab), :]
            s_hi = sh_ref[pl.ds(row0, ab), :]
            k_all = [jnp.concatenate([kc[grp], kw_ref[grp, pl.ds(row0, 3 * ab), :]], axis=0)
                     for grp in range(SWA_KV_HEADS)]
            vt_all = [jnp.concatenate([vct[grp], vw_ref[grp, nq], vw_ref[grp, nq + 1], vw_ref[grp, nq + 2]],
                                      axis=1) for grp in range(SWA_KV_HEADS)]
            for pr in pairs:
                qps.append(_rope(q_ref[pl.ds(row0, ab), pr * LANES:(pr + 1) * LANES], cos, s_lo, s_hi) * scale)
                sinks.append((sink_ref[2 * pr], sink_ref[2 * pr + 1]))
                k_dups.append(k_all[pr // 2])
                vt_dups.append(vt_all[pr // 2])
                masks.append(mask)
                places.append((row0, pr))
        outs = _pairs_attention(qps, sinks, k_dups, vt_dups, masks)
        for (row0, pr), out in zip(places, outs):
            o_ref[pl.ds(row0, ab), pr * LANES:(pr + 1) * LANES] = out
        return carry

    lax.fori_loop(0, nb // LAT_BLOCKS, block, 0)


def _attn_lat(sink, q, k, v, kc, vc, cos, sin_lo, sin_hi):
    b, t, _ = q.shape
    lc = kc.shape[1]
    bmap = lambda i: (i, 0, 0)
    full = lambda i: (0, 0)
    return pl.pallas_call(
        _attn_lat_kernel,
        out_shape=jax.ShapeDtypeStruct((b, t, SWA_Q), F32),
        grid=(b,),
        in_specs=[pl.BlockSpec(memory_space=pltpu.SMEM),
                  pl.BlockSpec((None, t, SWA_Q), bmap),
                  pl.BlockSpec((None, t, SWA_KV), bmap),
                  pl.BlockSpec((None, t, SWA_KV), bmap),
                  pl.BlockSpec((None, lc, SWA_KV), bmap),
                  pl.BlockSpec((None, lc, SWA_KV), bmap),
                  pl.BlockSpec((t, LANES), full),
                  pl.BlockSpec((t, LANES), full),
                  pl.BlockSpec((t, LANES), full)],
        out_specs=pl.BlockSpec((None, t, SWA_Q), bmap),
        scratch_shapes=[pltpu.VMEM((SWA_KV_HEADS, t + 2 * ATTN_BLOCK, LANES), BF16),
                        pltpu.VMEM((SWA_KV_HEADS, t // ATTN_BLOCK + 2, LANES, ATTN_BLOCK), BF16)],
        compiler_params=pltpu.CompilerParams(dimension_semantics=("arbitrary",),
                                             vmem_limit_bytes=VMEM_LIMIT),
        name="attn_lat",
    )(sink, q, k, v, kc, vc, cos, sin_lo, sin_hi)


def _rope_tables(t):
    half = SWA_HEAD_DIM // 2
    quarter = half // 2
    rows = t // GRID_W
    inv_freq = ROPE_BASE ** (-jnp.arange(quarter, dtype=F32) / quarter)
    reps = LANES // quarter
    ang_row = jnp.tile(jnp.arange(rows).astype(F32)[:, None] * inv_freq[None, :], (1, reps))
    ang_col = jnp.tile(jnp.arange(GRID_W).astype(F32)[:, None] * inv_freq[None, :], (1, reps))
    d = jnp.arange(LANES) % SWA_HEAD_DIM
    use_row = (d < half)[None, :]
    lower = ((d % half) < quarter)[None, :]

    def expand(f):
        by_row = jnp.repeat(f(ang_row), GRID_W, axis=0)
        by_col = jnp.tile(f(ang_col), (rows, 1))
        return jnp.where(use_row, by_row, by_col)

    cos = expand(jnp.cos)
    sin = expand(jnp.sin)
    return cos, jnp.where(lower, -sin, 0.0), jnp.where(lower, 0.0, sin)


def _route(sel, scores):
    n = sel.shape[1]
    gsz = N_EXPERTS // N_EXPERT_GROUPS

    def first_max(x, idx, size):
        m = jnp.max(x, axis=0, keepdims=True)
        first = jnp.min(jnp.where(x == m, idx, float(size)), axis=0, keepdims=True)
        return m, idx == first

    i8 = lax.broadcasted_iota(jnp.int32, (gsz, n), 0).astype(F32)
    rows = []
    for g in range(N_EXPERT_GROUPS):
        slab = sel[g * gsz:(g + 1) * gsz, :]
        m1, hit = first_max(slab, i8, gsz)
        m2 = jnp.max(jnp.where(hit, NEG_INF, slab), axis=0, keepdims=True)
        rows.append(m1 + m2)
    gscore = jnp.concatenate(rows, axis=0)
    gsel = jnp.zeros((N_EXPERT_GROUPS, n), F32)
    for _ in range(TOPK_GROUPS):
        _, hit = first_max(gscore, i8, N_EXPERT_GROUPS)
        gsel = jnp.where(hit, 1.0, gsel)
        gscore = jnp.where(hit, NEG_INF, gscore)
    emask = jnp.concatenate(
        [jnp.broadcast_to(gsel[g:g + 1, :], (gsz, n)) for g in range(N_EXPERT_GROUPS)], axis=0)
    cand = jnp.where(emask > 0.5, sel, NEG_INF)
    ie = lax.broadcasted_iota(jnp.int32, (N_EXPERTS, n), 0).astype(F32)
    w = jnp.zeros((N_EXPERTS, n), F32)
    chosen = jnp.zeros((N_EXPERTS, n), F32)
    hits = []
    for _ in range(TOP_K):
        _, hit = first_max(cand, ie, N_EXPERTS)
        hits.append(hit)
        w = jnp.where(hit, scores, w)
        chosen = jnp.where(hit, 1.0, chosen)
        cand = jnp.where(hit, NEG_INF, cand)
    gates = w / jnp.sum(w, axis=0, keepdims=True) * ROUTED_SCALE

    s_idx = lax.broadcasted_iota(jnp.int32, (n, n), 0)
    t_idx = lax.broadcasted_iota(jnp.int32, (n, n), 1)
    tile_shift = MOE_TILE.bit_length() - 1
    before = jnp.where((s_idx < t_idx) & ((s_idx >> tile_shift) == (t_idx >> tile_shift)), 1.0, 0.0)
    rank = _dot(chosen.astype(BF16), before.astype(BF16))
    e_row = lax.broadcasted_iota(jnp.int32, (N_EXPERTS, N_EXPERTS), 0)
    e_col = lax.broadcasted_iota(jnp.int32, (N_EXPERTS, N_EXPERTS), 1)
    below = jnp.where(e_col < e_row, 1.0, 0.0).astype(BF16)
    lane_tile = lax.broadcasted_iota(jnp.int32, (1, n), 1) >> tile_shift

    def as_row(col):
        return jnp.concatenate([col, jnp.zeros((LANES - N_EXPERTS, LANES), F32)], axis=0).T[0:1, :]

    sizes, starts = [], []
    first_row = jnp.zeros((N_EXPERTS, n), F32)
    for ti in range(n // MOE_TILE):
        count = jnp.sum(chosen[:, ti * MOE_TILE:(ti + 1) * MOE_TILE], axis=1, keepdims=True)
        padded = jnp.floor((count + (SORT_ALIGN - 1)) * (1.0 / SORT_ALIGN)) * SORT_ALIGN
        padded = jnp.broadcast_to(padded, (N_EXPERTS, LANES))
        start = _dot(below, padded.astype(BF16))
        first_row = jnp.where(lane_tile == ti, start[:, 0:1], first_row)
        sizes.append(as_row(padded))
        starts.append(as_row(start))
    row = first_row + rank
    pos = jnp.concatenate([jnp.sum(jnp.where(h, row, 0.0), axis=0, keepdims=True) for h in hits], axis=0)
    wts = jnp.concatenate([jnp.sum(jnp.where(h, gates, 0.0), axis=0, keepdims=True) for h in hits], axis=0)
    return pos, wts, sizes, starts


def _outproj_kernel(gla_ref, att_ref, x_ref, wo_ref, g1_ref, sh_ref, sc_ref, ng_ref, rw_ref, rwh_ref,
                    rb_ref, x1_ref, xm_ref, pos_ref, wts_ref, cnt_ref, start_ref):
    bb, tb, d = x_ref.shape
    tm = bb * tb
    y = (_dot(gla_ref[...].reshape(tm, GLA_V).astype(BF16), wo_ref[0:GLA_V, :])
         + _dot(att_ref[...].reshape(tm, SWA_Q).astype(BF16), wo_ref[GLA_V:, :]))
    x1 = x_ref[...].reshape(tm, d) + g1_ref[...] * y
    x1_ref[...] = x1.reshape(bb, tb, d)
    xm = _rms_norm(x1, ng_ref[...]) * (1.0 + sc_ref[...]) + sh_ref[...]
    xm_hi, xm_lo = _split_hi_lo(xm)
    xm_ref[...] = xm_hi.reshape(bb, tb, d)
    lg = _dot(xm_hi, rw_ref[...])
    logits = lg[:, :N_EXPERTS] + lg[:, N_EXPERTS:] + _dot(xm_lo, rwh_ref[...])
    lt = jnp.concatenate([logits, jnp.zeros((tm, LANES - N_EXPERTS), F32)], axis=1).T[:N_EXPERTS, :]
    scores = _sigmoid(lt)
    pos, wts, sizes, starts = _route(scores + rb_ref[...], scores)
    tiles_per_batch = tb // MOE_TILE
    for ti in range(tm // MOE_TILE):
        at = (ti // tiles_per_batch, ti % tiles_per_batch)
        pos_ref[at] = pos[:, ti * MOE_TILE:(ti + 1) * MOE_TILE]
        wts_ref[at] = wts[:, ti * MOE_TILE:(ti + 1) * MOE_TILE]
        cnt_ref[at] = sizes[ti]
        start_ref[at] = starts[ti]


OUTPROJ_TILE = 1024


def _outproj(gla_out, att_out, x, w_out, g1, sh2, sc2, norm_g, rw_cat, rw_hi, rbias):
    b, t, d = x.shape
    nmod = g1.shape[0]
    tb = min(t, OUTPROJ_TILE)
    bb = OUTPROJ_TILE // tb if nmod == 1 else 1
    tpb = tb // MOE_TILE
    mod_map = (lambda i, j: (i, 0, 0)) if nmod > 1 else (lambda i, j: (0, 0, 0))
    row = lambda i, j: (i, j, 0)
    full = lambda i, j: (0, 0)
    tile = lambda i, j: (i, j, 0, 0)
    nt = t // MOE_TILE
    return pl.pallas_call(
        _outproj_kernel,
        out_shape=(jax.ShapeDtypeStruct((b, t, d), F32),
                   jax.ShapeDtypeStruct((b, t, d), BF16),
                   jax.ShapeDtypeStruct((b, nt, TOP_K, MOE_TILE), F32),
                   jax.ShapeDtypeStruct((b, nt, TOP_K, MOE_TILE), F32),
                   jax.ShapeDtypeStruct((b, nt, 1, LANES), F32),
                   jax.ShapeDtypeStruct((b, nt, 1, LANES), F32)),
        grid=(b // bb, t // tb),
        in_specs=[pl.BlockSpec((bb, tb, GLA_V), row),
                  pl.BlockSpec((bb, tb, SWA_Q), row),
                  pl.BlockSpec((bb, tb, d), row),
                  pl.BlockSpec((d, d), full, pipeline_mode=pl.Buffered(1)),
                  pl.BlockSpec((None, 1, d), mod_map),
                  pl.BlockSpec((None, 1, d), mod_map),
                  pl.BlockSpec((None, 1, d), mod_map),
                  pl.BlockSpec((1, d), full),
                  pl.BlockSpec((d, 2 * N_EXPERTS), full),
                  pl.BlockSpec((d, N_EXPERTS), full),
                  pl.BlockSpec((N_EXPERTS, 1), full)],
        out_specs=(pl.BlockSpec((bb, tb, d), row),
                   pl.BlockSpec((bb, tb, d), row),
                   pl.BlockSpec((bb, tpb, TOP_K, MOE_TILE), tile),
                   pl.BlockSpec((bb, tpb, TOP_K, MOE_TILE), tile),
                   pl.BlockSpec((bb, tpb, 1, LANES), tile),
                   pl.BlockSpec((bb, tpb, 1, LANES), tile)),
        compiler_params=pltpu.CompilerParams(dimension_semantics=("arbitrary", "arbitrary"),
                                             vmem_limit_bytes=VMEM_LIMIT),
        name="outproj",
    )(gla_out, att_out, x, w_out, g1, sh2, sc2, norm_g, rw_cat, rw_hi, rbias)


MOE_TILE = 256
SORT_ALIGN = 16
SORT_ROWS = 3072
ROW_TILE = 1024
GATHER_SLOTS = 5
FFN_CHAINS = 8
COMBINE_CHUNK = 1024
ALWAYS_ROWS = 2560
COMBINE_TAIL = 512
COMBINE_SLOTS = 2


def _moe_sort_kernel(tiles_a, used_ref, xa_ref, xb_ref, pos_ref, xs_hbm, ybuf, osem):
    i = pl.program_id(0)
    last = pl.num_programs(0) - 1
    slot = lax.rem(i, 2)
    x = jnp.where(i < tiles_a, xa_ref[...], xb_ref[...])
    pos = pos_ref[...]
    tm = x.shape[0]
    used = used_ref[i]

    def out_copies(tile, of_slot, start):
        def one(r0, r1):
            row = pl.multiple_of(tile * SORT_ROWS + r0, tm)
            cp = pltpu.make_async_copy(ybuf.at[of_slot, r0:r1, :], xs_hbm.at[pl.ds(row, r1 - r0), :],
                                       osem.at[of_slot])
            if start:
                cp.start()
            else:
                cp.wait()

        one(0, ALWAYS_ROWS)
        for r0 in range(ALWAYS_ROWS, SORT_ROWS, tm):
            pl.when(r0 < used_ref[tile])(functools.partial(one, r0, r0 + tm))

    @pl.when(i >= 2)
    def _():
        out_copies(i - 2, slot, False)

    rows = lax.broadcasted_iota(jnp.int32, (tm, tm), 0).astype(F32).astype(BF16)
    one_bf = jnp.ones((tm, tm), BF16)

    def fill(blk):
        local = (pos - float(blk * tm)).astype(BF16)
        onehot = jnp.zeros((tm, tm), BF16)
        for k in range(TOP_K):
            onehot = jnp.where(rows == local[k:k + 1, :], one_bf, onehot)
        ybuf[slot, blk * tm:(blk + 1) * tm, :] = _dot(onehot, x).astype(BF16)

    for blk in range(SORT_ROWS // tm):
        if (blk + 1) * tm <= ALWAYS_ROWS:
            fill(blk)
        else:
            pl.when(blk * tm < used)(functools.partial(fill, blk))
    out_copies(i, slot, True)

    @pl.when(i == last)
    def _():
        out_copies(i, slot, False)

        @pl.when(i >= 1)
        def _():
            out_copies(i - 1, 1 - slot, False)


def _moe_sort(xm_a, xm_b, pos, used):
    d = xm_a.shape[1]
    nt, _, tm = pos.shape
    tiles_a = xm_a.shape[0] // tm
    grid_spec = pltpu.PrefetchScalarGridSpec(
        num_scalar_prefetch=1,
        grid=(nt,),
        in_specs=[pl.BlockSpec((tm, d), lambda i, u: (jnp.minimum(i, tiles_a - 1), 0)),
                  pl.BlockSpec((tm, d), lambda i, u: (jnp.maximum(i - tiles_a, 0), 0)),
                  pl.BlockSpec((None, TOP_K, tm), lambda i, u: (i, 0, 0))],
        out_specs=pl.BlockSpec(memory_space=pl.ANY),
        scratch_shapes=[pltpu.VMEM((2, SORT_ROWS, d), BF16),
                        pltpu.SemaphoreType.DMA((2,))])
    return pl.pallas_call(
        functools.partial(_moe_sort_kernel, tiles_a),
        out_shape=jax.ShapeDtypeStruct((nt * SORT_ROWS, d), BF16),
        grid_spec=grid_spec,
        compiler_params=pltpu.CompilerParams(dimension_semantics=("arbitrary",),
                                             vmem_limit_bytes=VMEM_LIMIT),
        name="moe_sort",
    )(used, xm_a, xm_b, pos)


def _moe_row_tiles(n_tokens):
    rows = n_tokens * TOP_K + (n_tokens // MOE_TILE) * N_EXPERTS * (SORT_ALIGN - 1) + N_EXPERTS * (ROW_TILE - 1)
    return -(-rows // ROW_TILE) + GATHER_SLOTS - 1


PLAN_CHUNK = 1280


def _int_dot_r(a, onehot):
    hi = jnp.floor(a * (1.0 / 256.0))
    return _dot(hi.astype(BF16), onehot) * 256.0 + _dot((a - hi * 256.0).astype(BF16), onehot)


def _int_dot_l(onehot, b):
    hi = jnp.floor(b * (1.0 / 256.0))
    return _dot(onehot, hi.astype(BF16)) * 256.0 + _dot(onehot, (b - hi * 256.0).astype(BF16))


def _moe_plan_kernel(cnt_ref, start_ref, src_ref, first_ref, tiles_ref, nu_ref, back_ref):
    nt, ne = cnt_ref.shape
    gpt = SORT_ROWS // SORT_ALIGN
    gpr = ROW_TILE // SORT_ALIGN
    gc = cnt_ref[...] * (1.0 / SORT_ALIGN)
    ls = start_ref[...] * (1.0 / SORT_ALIGN)

    def transpose(x):
        x = jnp.concatenate([x, jnp.zeros((nt, LANES - ne), F32)], axis=1)
        x = jnp.concatenate([x, jnp.zeros((LANES - nt, LANES), F32)], axis=0)
        return x.T[:ne, :nt]

    def tri(n, keep):
        return jnp.where(keep(lax.broadcasted_iota(jnp.int32, (n, n), 0),
                              lax.broadcasted_iota(jnp.int32, (n, n), 1)), 1.0, 0.0).astype(BF16)

    gc_t = transpose(gc)
    ls_t = transpose(ls)
    tot_c = jnp.broadcast_to(jnp.sum(gc_t, axis=1, keepdims=True), (ne, LANES))
    ptot_c = jnp.floor((tot_c + (gpr - 1)) * (1.0 / gpr)) * gpr
    gend_c = _int_dot_l(tri(ne, lambda r, c: c <= r), ptot_c)
    gstart_c = gend_c - ptot_c
    n_used = gend_c[ne - 1:ne, :] * (1.0 / gpr)
    nu_ref[...] = n_used.astype(jnp.int32)
    tot_r = jnp.sum(gc, axis=0, keepdims=True)
    ptot_r = jnp.floor((tot_r + (gpr - 1)) * (1.0 / gpr)) * gpr
    gstart_r = _int_dot_r(jnp.broadcast_to(ptot_r, (8, ne)), tri(ne, lambda r, c: r < c))
    cumex = _dot(tri(nt, lambda r, c: c < r), gc.astype(BF16))
    cumex_t = _dot(gc_t.astype(BF16), tri(nt, lambda r, c: r < c))
    tile_base = lax.broadcasted_iota(jnp.int32, (nt, ne), 0).astype(F32) * gpt + ls
    table = jnp.concatenate([cumex + gc, cumex, tile_base, gstart_r, jnp.broadcast_to(tot_r, (8, ne))], axis=0)

    e_iota = lax.broadcasted_iota(jnp.int32, (ne, PLAN_CHUNK), 0).astype(F32)
    for ch in range(src_ref.shape[1] // PLAN_CHUNK):
        g = (lax.broadcasted_iota(jnp.int32, (1, PLAN_CHUNK), 1) + ch * PLAN_CHUNK).astype(F32)
        eg = jnp.sum(jnp.where(gend_c[:, 0:1] <= g, 1.0, 0.0), axis=0, keepdims=True)
        picked = _int_dot_r(table, jnp.where(e_iota == eg, 1.0, 0.0).astype(BF16))
        cum_g, cumex_g, base_g = picked[0:nt], picked[nt:2 * nt], picked[2 * nt:3 * nt]
        u = g - picked[3 * nt:3 * nt + 1]
        in_tile = (cumex_g <= u) & (u < cum_g)
        src = jnp.sum(jnp.where(in_tile, base_g - cumex_g, 0.0), axis=0, keepdims=True) + u
        src = jnp.where(u < picked[3 * nt + 8:3 * nt + 9], src, 0.0)
        src_ref[:, ch * PLAN_CHUNK:(ch + 1) * PLAN_CHUNK] = src.astype(jnp.int32)

    first_ref[...] = (gstart_c * (1.0 / gpr)).astype(jnp.int32)
    tiles_ref[...] = (ptot_c * (1.0 / gpr)).astype(jnp.int32)

    lg = lax.broadcasted_iota(jnp.int32, (ne, back_ref.shape[1]), 1).astype(F32)
    for t in range(nt):
        first = ls_t[:, t:t + 1]
        inside = (first <= lg) & (lg < first + gc_t[:, t:t + 1])
        shift = gstart_c[:, 0:1] + cumex_t[:, t:t + 1] - first
        val = jnp.sum(jnp.where(inside, shift + lg, 0.0), axis=0, keepdims=True)
        back_ref[t:t + 1, :] = val.astype(jnp.int32)


def _moe_plan(cnt, start):
    nt, ne = cnt.shape
    row_tiles = _moe_row_tiles(nt * MOE_TILE)
    gpt = SORT_ROWS // SORT_ALIGN
    gpr = ROW_TILE // SORT_ALIGN
    n_src = -(-(row_tiles * gpr) // PLAN_CHUNK) * PLAN_CHUNK
    n_back = -(-gpt // LANES) * LANES
    src, first, tiles, nu, back = pl.pallas_call(
        _moe_plan_kernel,
        out_shape=(jax.ShapeDtypeStruct((1, n_src), jnp.int32),
                   jax.ShapeDtypeStruct((ne, LANES), jnp.int32),
                   jax.ShapeDtypeStruct((ne, LANES), jnp.int32),
                   jax.ShapeDtypeStruct((1, LANES), jnp.int32),
                   jax.ShapeDtypeStruct((nt, n_back), jnp.int32)),
        compiler_params=pltpu.CompilerParams(vmem_limit_bytes=VMEM_LIMIT),
        name="moe_plan",
    )(cnt, start)
    return nu[0, :1], first[:, 0], tiles[:, 0], src[0, :row_tiles * gpr], back[:, :gpt]


def _moe_experts_kernel(nu_ref, first_ref, tiles_ref, src_ref, xs_hbm, wg_ref, wu_ref, wd_ref, ys_hbm,
                        xbuf, ybuf, gsem, osem, wgu_s, wd_s):
    e = pl.program_id(0)
    n_used = nu_ref[0]
    gpr = ROW_TILE // SORT_ALIGN
    part = ROW_TILE // FFN_CHAINS

    def gather(tile, to_slot, j0=0, j1=gpr):
        for j in range(j0, j1):
            row = pl.multiple_of(src_ref[tile * gpr + j] * SORT_ALIGN, SORT_ALIGN)
            pltpu.make_async_copy(xs_hbm.at[pl.ds(row, SORT_ALIGN), :],
                                  xbuf.at[to_slot, j * SORT_ALIGN:(j + 1) * SORT_ALIGN, :],
                                  gsem.at[to_slot]).start(priority=j % 2)

    def drain(of_slot):
        for j in range(gpr):
            pltpu.make_async_copy(xs_hbm.at[0:SORT_ALIGN, :],
                                  xbuf.at[of_slot, j * SORT_ALIGN:(j + 1) * SORT_ALIGN, :], gsem.at[of_slot]).wait()

    def out_copy(tile, of_slot):
        row = pl.multiple_of(tile * ROW_TILE, ROW_TILE)
        return pltpu.make_async_copy(ybuf.at[of_slot], ys_hbm.at[pl.ds(row, ROW_TILE), :], osem.at[of_slot])

    @pl.when(e == 0)
    def _():
        for ahead in range(GATHER_SLOTS - 1):
            gather(ahead, ahead)

    wgu_s[:, :EXPERT_FF] = wg_ref[...].astype(BF16)
    wgu_s[:, EXPERT_FF:] = wu_ref[...].astype(BF16)
    wd_s[...] = wd_ref[...].astype(BF16)

    def row_tile(i, carry):
        r = first_ref[e] + i
        slot = lax.rem(r, GATHER_SLOTS)
        oslot = lax.rem(r, 2)
        next_slot = lax.rem(r + GATHER_SLOTS - 1, GATHER_SLOTS)
        drain(slot)

        @pl.when(r >= 2)
        def _():
            out_copy(r - 2, oslot).wait()

        abs_ = []
        for c in range(FFN_CHAINS):
            abs_.append(_dot(xbuf[slot, c * part:(c + 1) * part, :], wgu_s[...]))
            gather(r + GATHER_SLOTS - 1, next_slot, c * gpr // FFN_CHAINS, (c + 1) * gpr // FFN_CHAINS)
        hs = [(_silu(ab[:, :EXPERT_FF]) * ab[:, EXPERT_FF:]).astype(BF16) for ab in abs_]
        ys = [_dot(h, wd_s[...]).astype(BF16) for h in hs]
        for c in range(FFN_CHAINS):
            ybuf[oslot, c * part:(c + 1) * part, :] = ys[c]
        out_copy(r, oslot).start()
        return carry

    lax.fori_loop(0, tiles_ref[e], row_tile, 0)

    @pl.when(e == pl.num_programs(0) - 1)
    def _():
        for ahead in range(GATHER_SLOTS - 1):
            drain(lax.rem(n_used + ahead, GATHER_SLOTS))
        out_copy(n_used - 1, lax.rem(n_used - 1, 2)).wait()

        @pl.when(n_used >= 2)
        def _():
            out_copy(n_used - 2, lax.rem(n_used, 2)).wait()


def _moe_experts(n_used, first, tiles, src, xs, wg, wu, wd, row_tiles):
    d = xs.shape[-1]
    ne = wg.shape[0]
    w_map = lambda e, nu, fi, ti, sr: (e, 0, 0)
    grid_spec = pltpu.PrefetchScalarGridSpec(
        num_scalar_prefetch=4,
        grid=(ne,),
        in_specs=[pl.BlockSpec(memory_space=pl.ANY),
                  pl.BlockSpec((None, d, EXPERT_FF), w_map),
                  pl.BlockSpec((None, d, EXPERT_FF), w_map),
                  pl.BlockSpec((None, EXPERT_FF, d), w_map)],
        out_specs=pl.BlockSpec(memory_space=pl.ANY),
        scratch_shapes=[pltpu.VMEM((GATHER_SLOTS, ROW_TILE, d), BF16),
                        pltpu.VMEM((2, ROW_TILE, d), BF16),
                        pltpu.SemaphoreType.DMA((GATHER_SLOTS,)),
                        pltpu.SemaphoreType.DMA((2,)),
                        pltpu.VMEM((d, 2 * EXPERT_FF), BF16),
                        pltpu.VMEM((EXPERT_FF, d), BF16)])
    return pl.pallas_call(
        _moe_experts_kernel,
        out_shape=jax.ShapeDtypeStruct((row_tiles * ROW_TILE, d), BF16),
        grid_spec=grid_spec,
        compiler_params=pltpu.CompilerParams(dimension_semantics=("arbitrary",),
                                             vmem_limit_bytes=VMEM_LIMIT),
        name="moe_experts",
    )(n_used, first, tiles, src, xs, wg, wu, wd)


def _moe_combine_kernel(back_ref, used_ref, ys_hbm, pos_ref, wts_ref, xm_ref, x1_ref, g2_ref, fg_ref,
                        swg_ref, swu_ref, swd_ref, o_ref, buf, sem, acc_ref):
    i = pl.program_id(0)
    gpt = SORT_ROWS // SORT_ALIGN
    slot = lax.rem(i, COMBINE_SLOTS)
    ahead = COMBINE_SLOTS - 1
    always = ALWAYS_ROWS
    tail = range(always, SORT_ROWS, COMBINE_TAIL)

    def copies(tile, of_slot, g0, g1, start):
        for g in range(g0, g1):
            row = pl.multiple_of(back_ref[tile * gpt + g] * SORT_ALIGN, SORT_ALIGN) if start else 0
            cp = pltpu.make_async_copy(ys_hbm.at[pl.ds(row, SORT_ALIGN), :],
                                       buf.at[of_slot, g * SORT_ALIGN:(g + 1) * SORT_ALIGN, :], sem.at[of_slot])
            if start:
                cp.start(priority=g % 2)
            else:
                cp.wait()

    def transfer(tile, of_slot, start):
        copies(tile, of_slot, 0, always // SORT_ALIGN, start)
        for c0 in tail:
            pl.when(c0 < used_ref[tile])(functools.partial(
                copies, tile, of_slot, c0 // SORT_ALIGN, (c0 + COMBINE_TAIL) // SORT_ALIGN, start))

    @pl.when(i == 0)
    def _():
        for first in range(min(ahead, buf.shape[0])):
            pl.when(first < pl.num_programs(0))(functools.partial(transfer, first, first, True))

    @pl.when(i + ahead < pl.num_programs(0))
    def _():
        transfer(i + ahead, lax.rem(i + ahead, COMBINE_SLOTS), True)

    x = xm_ref[...]
    tm = x.shape[0]
    pad = jnp.zeros((LANES - TOP_K, tm), F32)
    pos_t = jnp.concatenate([pos_ref[...], pad], axis=0).T
    wts_t = jnp.concatenate([wts_ref[...], pad], axis=0).T
    blk_b, loc_b, wts_b = [], [], []
    for k in range(TOP_K):
        p = jnp.broadcast_to(pos_t[:, k:k + 1], (tm, LANES))
        blk = jnp.floor(p * (1.0 / tm))
        two = lambda v: jnp.concatenate([v.astype(BF16)] * (tm // LANES), axis=1)
        blk_b.append(two(blk))
        loc_b.append(two(p - blk * tm))
        wts_b.append(two(jnp.broadcast_to(wts_t[:, k:k + 1], (tm, LANES))))
    shared = _dot((_silu(_dot(x, swg_ref[...])) * _dot(x, swu_ref[...])).astype(BF16), swd_ref[...])
    transfer(i, slot, False)
    lane = lax.broadcasted_iota(jnp.int32, (tm, tm), 1).astype(F32).astype(BF16)
    zero = jnp.zeros((tm, tm), BF16)
    nowhere = jnp.full((tm, tm), -1.0, BF16)

    def apply(c0, width):
        blocks = []
        for b0 in range(c0, c0 + width, tm):
            comb = zero
            for k in range(TOP_K):
                loc = jnp.where(blk_b[k] == float(b0 // tm), loc_b[k], nowhere)
                comb = jnp.where(lane == loc, wts_b[k], comb)
            blocks.append(comb)
        return _dot(jnp.concatenate(blocks, axis=1), buf[slot, c0:c0 + width, :])

    routed = shared
    for c0 in range(0, always, COMBINE_CHUNK):
        routed = routed + apply(c0, min(COMBINE_CHUNK, always - c0))
    acc_ref[...] = routed
    for c0 in tail:
        @pl.when(c0 < used_ref[i])
        def _(c0=c0):
            acc_ref[...] += apply(c0, COMBINE_TAIL)
    y = x1_ref[...] + g2_ref[...] * acc_ref[...]
    o_ref[...] = _rms_norm(y, fg_ref[...])


def _moe_combine(back, used, ys, pos, wts, xm, x1, g2, final_g, swg, swu, swd, *, tiles_per_mod):
    n, d = xm.shape
    tm = pos.shape[-1]
    nt = n // tm
    gpt = SORT_ROWS // SORT_ALIGN
    row = lambda i, bk, us: (i, 0)
    full = lambda i, bk, us: (0, 0)
    tile = lambda i, bk, us: (i, 0, 0)
    mod_map = lambda i, bk, us: (i // tiles_per_mod, 0, 0)
    grid_spec = pltpu.PrefetchScalarGridSpec(
        num_scalar_prefetch=2,
        grid=(nt,),
        in_specs=[pl.BlockSpec(memory_space=pl.ANY),
                  pl.BlockSpec((None, TOP_K, tm), tile),
                  pl.BlockSpec((None, TOP_K, tm), tile),
                  pl.BlockSpec((tm, d), row),
                  pl.BlockSpec((tm, d), row),
                  pl.BlockSpec((None, 1, d), mod_map),
                  pl.BlockSpec((1, d), full),
                  pl.BlockSpec((d, SHARED_FF), full),
                  pl.BlockSpec((d, SHARED_FF), full),
                  pl.BlockSpec((SHARED_FF, d), full)],
        out_specs=pl.BlockSpec((tm, d), row),
        scratch_shapes=[pltpu.VMEM((COMBINE_SLOTS, SORT_ROWS, d), BF16),
                        pltpu.SemaphoreType.DMA((COMBINE_SLOTS,)),
                        pltpu.VMEM((tm, d), F32)])
    return pl.pallas_call(
        _moe_combine_kernel,
        out_shape=jax.ShapeDtypeStruct((n, d), F32),
        grid_spec=grid_spec,
        compiler_params=pltpu.CompilerParams(dimension_semantics=("arbitrary",),
                                             vmem_limit_bytes=VMEM_LIMIT),
        name="moe_combine",
    )(back, used, ys, pos.reshape(nt, TOP_K, tm), wts.reshape(nt, TOP_K, tm), xm, x1, g2, final_g, swg, swu, swd)


def _mix(x, mods, p, attn_fn, s0=None):
    sh1, sc1, g1, sh2, sc2, _ = mods
    gla_in, lora, q_s, k_s, v_s = _inproj(x, p["norm_attn_g"], sh1, sc1, p["w_gla"], p["w_lora"], p["w_swa"])
    if s0 is None:
        gla_out, s_f, s_b = _gla(gla_in, lora, p["waf"], p["baf"], p["wab"], p["bab"], p["gla_norm_g"])
    else:
        gla_out, s_f, s_b = _gla(gla_in, lora, p["waf"], p["baf"], p["wab"], p["bab"], p["gla_norm_g"],
                                 s0[0], s0[1])
    att_out = attn_fn(q_s, k_s, v_s)
    routed = _outproj(gla_out, att_out, x, p["w_out"], g1, sh2, sc2, p["norm_ffn_g"],
                      p["rw_cat"], p["rw_hi"], p["rbias"])
    return routed, k_s, v_s, s_f, s_b


def _moe(streams, p):
    d = D_MODEL
    (ra, _), (rb, _) = streams
    n_tiles = [r[1].shape[0] * r[1].shape[1] // MOE_TILE for r, _ in streams]
    pos_all = jnp.concatenate([r[2].reshape(-1, TOP_K, MOE_TILE) for r, _ in streams], axis=0)
    cnt_all = jnp.concatenate([r[4].reshape(-1, LANES) for r, _ in streams], axis=0)[:, :N_EXPERTS]
    start_all = jnp.concatenate([r[5].reshape(-1, LANES) for r, _ in streams], axis=0)[:, :N_EXPERTS]
    used = (start_all[:, -1] + cnt_all[:, -1]).astype(jnp.int32)
    xs = _moe_sort(ra[1].reshape(-1, d), rb[1].reshape(-1, d), pos_all, used)
    n_used, first, tiles, src, back = _moe_plan(cnt_all, start_all)
    ys = _moe_experts(n_used, first, tiles, src, xs, p["wg"], p["wu"], p["wd"],
                      _moe_row_tiles(cnt_all.shape[0] * MOE_TILE))
    outs = []
    tile0 = 0
    for ((x1, xm, pos, wts, cnt, start), g2), nt in zip(streams, n_tiles):
        b, t, _ = x1.shape
        tiles_per_mod = (t // MOE_TILE) if g2.shape[0] > 1 else nt
        y = _moe_combine(back[tile0:tile0 + nt].reshape(-1), used[tile0:tile0 + nt], ys, pos, wts,
                         xm.reshape(-1, d), x1.reshape(-1, d), g2, p["final_norm_g"],
                         p["swg"], p["swu"], p["swd"], tiles_per_mod=tiles_per_mod)
        outs.append(y.reshape(b, t, d))
        tile0 += nt
    return outs


def kernel(x_prompt, x_sample, c, cache_swa_k, cache_swa_v, state_gla_fwd, state_gla_bwd, c_ctx, w_ada, b_ada, norm_attn_g, norm_ffn_g, w_in, gla_wa_f, gla_ba_f, gla_wa_b, gla_ba_b, gla_norm_g, swa_sink, w_out, router_w, router_bias, exp_w_gate, exp_w_up, exp_w_down, sh_w_gate, sh_w_up, sh_w_down, final_norm_g):
    l = 0
    d = D_MODEL
    nb_ctx, t_ctx, _ = x_prompt.shape
    nb_lat, t_lat, _ = x_sample.shape

    pad = jnp.zeros((8 - 1 - nb_lat, d), F32)
    cond8 = jnp.concatenate([c_ctx[None, :], c, pad], axis=0)
    mod = _adaln(cond8, w_ada[l], b_ada[l][None, :])
    mods_ctx = [mod[0:1, i * d:(i + 1) * d][:, None, :] for i in range(6)]
    mods_lat = [mod[1:1 + nb_lat, i * d:(i + 1) * d][:, None, :] for i in range(6)]

    zeros_lora = jnp.zeros((GLA_LORA, GLA_QK), F32)
    rw = router_w[l]
    rw_hi = rw.astype(BF16)
    rw_lo = (rw - rw_hi.astype(F32)).astype(BF16)
    n_gla = 2 * GLA_QK + 2 * GLA_V
    p = {
        "norm_attn_g": norm_attn_g[l][None, :],
        "norm_ffn_g": norm_ffn_g[l][None, :],
        "final_norm_g": final_norm_g[None, :],
        "w_gla": w_in[l][:, :n_gla].astype(BF16),
        "w_lora": w_in[l][:, n_gla:n_gla + 2 * GLA_LORA].astype(BF16),
        "w_swa": w_in[l][:, n_gla + 2 * GLA_LORA:].astype(BF16),
        "waf": jnp.concatenate([gla_wa_f[l], zeros_lora], axis=0).astype(BF16),
        "wab": jnp.concatenate([zeros_lora, gla_wa_b[l]], axis=0).astype(BF16),
        "baf": gla_ba_f[l][None, :],
        "bab": gla_ba_b[l][None, :],
        "gla_norm_g": gla_norm_g[l][None, :],
        "w_out": w_out[l].astype(BF16),
        "rw_cat": jnp.concatenate([rw_hi, rw_lo], axis=1),
        "rw_hi": rw_hi,
        "rbias": router_bias[l][:, None],
        "wg": exp_w_gate[l], "wu": exp_w_up[l], "wd": exp_w_down[l],
        "swg": sh_w_gate[l].astype(BF16), "swu": sh_w_up[l].astype(BF16),
        "swd": sh_w_down[l].astype(BF16),
    }
    sink = swa_sink[l]

    routed_ctx, k_c, v_c, s_f, s_b = _mix(x_prompt, mods_ctx, p, functools.partial(_attn_ctx, sink))

    cos, sin_lo, sin_hi = _rope_tables(t_lat)
    kc = cache_swa_k[:, l].reshape(nb_lat, -1, SWA_KV)
    vc = cache_swa_v[:, l].reshape(nb_lat, -1, SWA_KV)
    lat_attn = lambda q, k, v: _attn_lat(sink, q, k, v, kc, vc, cos, sin_lo, sin_hi)
    s0 = (state_gla_fwd[:, l].reshape(nb_lat, GLA_QK, GLA_DV),
          state_gla_bwd[:, l].reshape(nb_lat, GLA_QK, GLA_DV))
    routed_lat, _, _, _, _ = _mix(x_sample, mods_lat, p, lat_attn, s0)
    y_prompt, y_sample = _moe([(routed_ctx, mods_ctx[5]), (routed_lat, mods_lat[5])], p)

    new_k = k_c.reshape(nb_ctx, 1, t_ctx, SWA_KV_HEADS, SWA_HEAD_DIM)
    new_v = v_c.reshape(nb_ctx, 1, t_ctx, SWA_KV_HEADS, SWA_HEAD_DIM)
    new_sf = s_f.reshape(nb_ctx, 1, GLA_HEADS, GLA_DK, GLA_DV)
    new_sb = s_b.reshape(nb_ctx, 1, GLA_HEADS, GLA_DK, GLA_DV)
    return (y_prompt, y_sample, new_k, new_v, new_sf, new_sb)
```

```python
import functools

import jax
import jax.numpy as jnp
from jax import lax
from jax.experimental import pallas as pl
from jax.experimental.pallas import tpu as pltpu

F32 = jnp.float32
BF16 = jnp.bfloat16

D_MODEL = 1024
GLA_HEADS = 4
GLA_DK = 64
GLA_DV = 128
GLA_LORA = 16
GLA_GATE_NORM = 16.0
GLA_CHUNK = 64
GLA_QK = GLA_HEADS * GLA_DK
GLA_V = GLA_HEADS * GLA_DV
SWA_HEAD_DIM = 64
SWA_HEADS = 8
SWA_KV_HEADS = 2
SWA_Q = SWA_HEADS * SWA_HEAD_DIM
SWA_KV = SWA_KV_HEADS * SWA_HEAD_DIM
ATTN_BLOCK = 128
GRID_W = 64
ROPE_BASE = 10000.0
N_EXPERTS = 64
TOP_K = 8
N_EXPERT_GROUPS = 8
TOPK_GROUPS = 4
EXPERT_FF = 128
SHARED_FF = 256
ROUTED_SCALE = 2.5
EPS = 1e-6

LANES = 128
VMEM_LIMIT = 56 * 1024 * 1024

NEG_INF = float("-inf")


def _dot(a, b):
    return jnp.dot(a, b, preferred_element_type=F32)


def _dot_nt(a, b):
    return lax.dot_general(a, b, (((1,), (1,)), ((), ())), preferred_element_type=F32)


def _split_hi_lo(x):
    hi = x.astype(BF16)
    lo = (x - hi.astype(F32)).astype(BF16)
    return hi, lo


def _sigmoid(x):
    return 1.0 / (1.0 + jnp.exp(-x))


def _silu(x):
    return x * _sigmoid(x)


def _rms_norm(x, g):
    ms = jnp.mean(x * x, axis=-1, keepdims=True)
    return x * lax.rsqrt(ms + EPS) * g


def _adaln_kernel(c_ref, w_ref, b_ref, o_ref):
    a_hi, a_lo = _split_hi_lo(_silu(c_ref[...]))
    w_hi, w_lo = _split_hi_lo(w_ref[...])
    o_ref[...] = _dot(a_hi, w_hi) + _dot(a_lo, w_hi) + _dot(a_hi, w_lo) + b_ref[...]


def _adaln(cond8, w_ada, b_ada):
    n = w_ada.shape[1]
    tn = 1536
    return pl.pallas_call(
        _adaln_kernel,
        out_shape=jax.ShapeDtypeStruct((8, n), F32),
        grid=(n // tn,),
        in_specs=[pl.BlockSpec((8, D_MODEL), lambda j: (0, 0)),
                  pl.BlockSpec((D_MODEL, tn), lambda j: (0, j)),
                  pl.BlockSpec((1, tn), lambda j: (0, j))],
        out_specs=pl.BlockSpec((8, tn), lambda j: (0, j)),
        compiler_params=pltpu.CompilerParams(dimension_semantics=("arbitrary",),
                                             vmem_limit_bytes=VMEM_LIMIT),
        name="adaln",
    )(cond8, w_ada, b_ada)


def _inproj_kernel(x_ref, g_ref, sh_ref, sc_ref, wg_ref, wl_ref, ws_ref,
                   gla_ref, lora_ref, q_ref, k_ref, v_ref):
    bb, tb, d = x_ref.shape
    x = x_ref[...].reshape(bb * tb, d)
    h = _rms_norm(x, g_ref[...]) * (1.0 + sc_ref[...]) + sh_ref[...]
    hb = h.astype(BF16)
    gla_ref[...] = _dot(hb, wg_ref[...]).reshape(gla_ref.shape)
    lora_ref[...] = _dot(hb, wl_ref[...]).reshape(lora_ref.shape)
    s = _dot(hb, ws_ref[...])
    q_ref[...] = s[:, :SWA_Q].reshape(q_ref.shape)
    k_ref[...] = s[:, SWA_Q:SWA_Q + SWA_KV].reshape(k_ref.shape)
    v_ref[...] = s[:, SWA_Q + SWA_KV:].reshape(v_ref.shape)


INPROJ_TILE = 1024


def _inproj(x, g, sh, sc, w_gla, w_lora, w_swa):
    b, t, d = x.shape
    nmod = sh.shape[0]
    tb = min(t, INPROJ_TILE)
    bb = INPROJ_TILE // tb if nmod == 1 else 1
    mod_map = (lambda i, j: (i, 0, 0)) if nmod > 1 else (lambda i, j: (0, 0, 0))
    row = lambda i, j: (i, j, 0)
    full = lambda i, j: (0, 0)
    n_gla = w_gla.shape[1]
    n_lora = w_lora.shape[1]
    return pl.pallas_call(
        _inproj_kernel,
        out_shape=(jax.ShapeDtypeStruct((b, t, n_gla), F32),
                   jax.ShapeDtypeStruct((b, t, n_lora), F32),
                   jax.ShapeDtypeStruct((b, t, SWA_Q), F32),
                   jax.ShapeDtypeStruct((b, t, SWA_KV), F32),
                   jax.ShapeDtypeStruct((b, t, SWA_KV), F32)),
        grid=(b // bb, t // tb),
        in_specs=[pl.BlockSpec((bb, tb, d), row),
                  pl.BlockSpec((1, d), full),
                  pl.BlockSpec((None, 1, d), mod_map),
                  pl.BlockSpec((None, 1, d), mod_map),
                  pl.BlockSpec((d, n_gla), full, pipeline_mode=pl.Buffered(1)),
                  pl.BlockSpec((d, n_lora), full, pipeline_mode=pl.Buffered(1)),
                  pl.BlockSpec((d, w_swa.shape[1]), full, pipeline_mode=pl.Buffered(1))],
        out_specs=(pl.BlockSpec((bb, tb, n_gla), row),
                   pl.BlockSpec((bb, tb, n_lora), row),
                   pl.BlockSpec((bb, tb, SWA_Q), row),
                   pl.BlockSpec((bb, tb, SWA_KV), row),
                   pl.BlockSpec((bb, tb, SWA_KV), row)),
        compiler_params=pltpu.CompilerParams(dimension_semantics=("arbitrary", "arbitrary"),
                                             vmem_limit_bytes=VMEM_LIMIT),
        name="inproj",
    )(x, g, sh, sc, w_gla, w_lora, w_swa)


SCAN_UNROLL = 4
OUT_UNROLL = 4


def _log_sigmoid(x):
    return jnp.minimum(x, 0.0) - jnp.log(1.0 + jnp.exp(-jnp.abs(x)))


def _heads_to_rows(x):
    return jnp.concatenate([x[:, h * LANES:(h + 1) * LANES] for h in range(GLA_HEADS)], axis=0)


def _rows_to_heads(x, c):
    return jnp.concatenate([x[h * c:(h + 1) * c, :] for h in range(GLA_HEADS)], axis=1)


def _gla_kernel(has_init, q_ref, k_ref, v_ref, g_ref, lora_ref, waf_ref, baf_ref, wab_ref, bab_ref,
                ng_ref, *rest):
    if has_init:
        s0f_ref, s0b_ref, *rest = rest
    (out_ref, sf_ref, sb_ref, laf_ref, lab_ref, oacc_ref, qtf_ref, qtb_ref, saf_ref, sab_ref,
     stf_ref, stb_ref) = rest
    t = q_ref.shape[0]
    c = GLA_CHUNK
    n = t // c
    hc = GLA_HEADS * c

    lora = lora_ref[...].astype(BF16)
    laf_ref[...] = _log_sigmoid(_dot(lora, waf_ref[...]) + baf_ref[...]) * (1.0 / GLA_GATE_NORM)
    lab_ref[...] = _log_sigmoid(_dot(lora, wab_ref[...]) + bab_ref[...]) * (1.0 / GLA_GATE_NORM)

    if has_init:
        stf_ref[...] = s0f_ref[...].T
        stb_ref[...] = s0b_ref[...].T
    else:
        stf_ref[...] = jnp.zeros_like(stf_ref)
        stb_ref[...] = jnp.zeros_like(stb_ref)
    oacc_ref[...] = jnp.zeros_like(oacc_ref)

    r64 = lax.broadcasted_iota(jnp.int32, (c, c), 0)
    c64 = lax.broadcasted_iota(jnp.int32, (c, c), 1)
    tri_f = jnp.where(c64 <= r64, 1.0, 0.0).astype(BF16)
    tri_b = jnp.where(c64 >= r64, 1.0, 0.0).astype(BF16)
    rr = lax.broadcasted_iota(jnp.int32, (hc, hc), 0)
    cc = lax.broadcasted_iota(jnp.int32, (hc, hc), 1)
    same_head = (rr >> 6) == (cc >> 6)
    keep_f = same_head & ((rr & (c - 1)) >= (cc & (c - 1)))
    keep_b = same_head & ((rr & (c - 1)) <= (cc & (c - 1)))
    head_mask = jnp.where(same_head, 1.0, 0.0).astype(BF16)
    norm_g = ng_ref[...]

    def chunk_rows(ci):
        return pl.ds(pl.multiple_of(ci * c, c), c)

    def tile_heads(x):
        x4 = jnp.concatenate([x] * GLA_HEADS, axis=0)
        return jnp.where(same_head, x4, 0.0).astype(BF16)

    def scan_step(i, carry):
        dirs = []
        for u in range(SCAN_UNROLL):
            dirs += [(SCAN_UNROLL * i + u, laf_ref, tri_f, keep_f, c - 1, stf_ref, saf_ref, qtf_ref),
                     (n - 1 - SCAN_UNROLL * i - u, lab_ref, tri_b, keep_b, 0, stb_ref, sab_ref, qtb_ref)]
        cums = []
        for ci, la_ref, tri, _, _, _, _, _ in dirs:
            la_hi, la_lo = _split_hi_lo(la_ref[chunk_rows(ci), :])
            cums.append(_dot(tri, la_hi) + _dot(tri, la_lo))
        ops = []
        for (ci, _, _, _, last_row, _, _, qt_ref), cum in zip(dirs, cums):
            sl = chunk_rows(ci)
            tot = cum[last_row:last_row + 1, :]
            kc = k_ref[sl, :]
            qt = q_ref[sl, :] * (GLA_DK ** -0.5) * jnp.exp(cum)
            qt_ref[sl, :] = qt.astype(BF16)
            v_rows = _heads_to_rows(v_ref[sl, :])
            ops.append((tot, tile_heads(qt), tile_heads(kc * jnp.exp(-cum)),
                        tile_heads(kc * jnp.exp(tot - cum)), v_rows))
        atts = [_dot_nt(q4, k4) for _, q4, k4, _, _ in ops]
        incs = []
        for (_, _, _, keep, _, _, _, _), (_, _, _, kd4, v_rows), att in zip(dirs, ops, atts):
            att = jnp.where(keep, att, 0.0).astype(BF16)
            incs.append((_dot(att, v_rows.astype(BF16)), _dot(v_rows.T.astype(BF16), kd4)))
        for (ci, _, _, _, _, st_ref, snap_ref, _), (tot, _, _, _, _), (o_intra, st_inc) in zip(dirs, ops, incs):
            oacc_ref[ci] += o_intra
            st = st_ref[...]
            snap_ref[ci] = st.astype(BF16)
            st_ref[...] = jnp.exp(tot) * st + st_inc
        return carry

    def tile_heads_bf16(x):
        return jnp.concatenate([x] * GLA_HEADS, axis=0) * head_mask

    def out_step(i, carry):
        chunks = [OUT_UNROLL * i + u for u in range(OUT_UNROLL)]
        inter = []
        for ci in chunks:
            sl = chunk_rows(ci)
            q4 = jnp.concatenate([tile_heads_bf16(qtf_ref[sl, :]), tile_heads_bf16(qtb_ref[sl, :])], axis=1)
            st = jnp.concatenate([saf_ref[ci], sab_ref[ci]], axis=1)
            inter.append(_dot_nt(q4, st))
        for ci, o_inter in zip(chunks, inter):
            sl = chunk_rows(ci)
            on = _rms_norm(oacc_ref[ci] + o_inter, norm_g)
            gate = _silu(_heads_to_rows(g_ref[sl, :]))
            out_ref[sl, :] = _rows_to_heads(on * gate, c)
        return carry

    lax.fori_loop(0, n // SCAN_UNROLL, scan_step, 0)
    lax.fori_loop(0, n // OUT_UNROLL, out_step, 0)
    sf_ref[...] = stf_ref[...].T
    sb_ref[...] = stb_ref[...].T


def _gla(gla_in, lora, waf, baf, wab, bab, norm_g, s0f=None, s0b=None):
    b, t, _ = gla_in.shape
    has_init = s0f is not None
    n = t // GLA_CHUNK
    bmap = lambda i: (i, 0, 0)
    full = lambda i: (0, 0)
    in_specs = [pl.BlockSpec((None, t, GLA_QK), lambda i: (i, 0, 0)),
                pl.BlockSpec((None, t, GLA_QK), lambda i: (i, 0, 1)),
                pl.BlockSpec((None, t, GLA_V), lambda i: (i, 0, 1)),
                pl.BlockSpec((None, t, GLA_V), lambda i: (i, 0, 2)),
                pl.BlockSpec((None, t, 2 * GLA_LORA), bmap),
                pl.BlockSpec((2 * GLA_LORA, GLA_QK), full),
                pl.BlockSpec((1, GLA_QK), full),
                pl.BlockSpec((2 * GLA_LORA, GLA_QK), full),
                pl.BlockSpec((1, GLA_QK), full),
                pl.BlockSpec((1, GLA_DV), full)]
    args = [gla_in, gla_in, gla_in, gla_in, lora, waf, baf, wab, bab, norm_g]
    if has_init:
        in_specs += [pl.BlockSpec((None, GLA_QK, GLA_DV), bmap)] * 2
        args += [s0f, s0b]
    return pl.pallas_call(
        functools.partial(_gla_kernel, has_init),
        out_shape=(jax.ShapeDtypeStruct((b, t, GLA_V), F32),
                   jax.ShapeDtypeStruct((b, GLA_QK, GLA_DV), F32),
                   jax.ShapeDtypeStruct((b, GLA_QK, GLA_DV), F32)),
        grid=(b,),
        in_specs=in_specs,
        out_specs=(pl.BlockSpec((None, t, GLA_V), bmap),
                   pl.BlockSpec((None, GLA_QK, GLA_DV), bmap),
                   pl.BlockSpec((None, GLA_QK, GLA_DV), bmap)),
        scratch_shapes=[pltpu.VMEM((t, GLA_QK), F32),
                        pltpu.VMEM((t, GLA_QK), F32),
                        pltpu.VMEM((n, GLA_HEADS * GLA_CHUNK, GLA_DV), F32),
                        pltpu.VMEM((t, GLA_QK), BF16),
                        pltpu.VMEM((t, GLA_QK), BF16),
                        pltpu.VMEM((n, GLA_DV, GLA_QK), BF16),
                        pltpu.VMEM((n, GLA_DV, GLA_QK), BF16),
                        pltpu.VMEM((GLA_DV, GLA_QK), F32),
                        pltpu.VMEM((GLA_DV, GLA_QK), F32)],
        compiler_params=pltpu.CompilerParams(dimension_semantics=("arbitrary",),
                                             vmem_limit_bytes=VMEM_LIMIT),
        name="gla",
    )(*args)


def _dup_groups(x):
    lo = lax.broadcasted_iota(jnp.int32, x.shape, 1) < SWA_HEAD_DIM
    xr = pltpu.roll(x, SWA_HEAD_DIM, axis=1)
    return jnp.where(lo, x, xr), jnp.where(lo, xr, x)


def _pairs_attention(qps, sinks, k_dups, vt_dups, masks):
    nq = qps[0].shape[0]
    lo = lax.broadcasted_iota(jnp.int32, (nq, LANES), 1) < SWA_HEAD_DIM
    even = lax.broadcasted_iota(jnp.int32, (1, 2 * nq), 1) < nq
    scores = []
    for qp, k_dup in zip(qps, k_dups):
        q2 = jnp.concatenate([jnp.where(lo, qp, 0.0), jnp.where(lo, 0.0, qp)], axis=0).astype(BF16)
        scores.append(_dot_nt(k_dup, q2))
    probs = []
    for s, (sink_even, sink_odd), mask in zip(scores, sinks, masks):
        if mask is not None:
            s = jnp.where(mask, s, NEG_INF)
        sink = jnp.where(even, sink_even, sink_odd)
        m = jnp.maximum(jnp.max(s, axis=0, keepdims=True), sink)
        p = jnp.exp(s - m)
        denom = jnp.sum(p, axis=0, keepdims=True) + jnp.exp(sink - m)
        probs.append((p.astype(BF16), 1.0 / denom))
    outs = []
    for (p, rdenom), vt_dup in zip(probs, vt_dups):
        o = _dot(vt_dup, p) * rdenom
        outs.append(jnp.concatenate([o[:SWA_HEAD_DIM, :nq], o[SWA_HEAD_DIM:, nq:]], axis=0).T)
    return outs


CTX_BATCH = 4


def _attn_ctx_kernel(sink_ref, q_ref, k_ref, v_ref, o_ref):
    scale = SWA_HEAD_DIM ** -0.5
    pairs = range(SWA_HEADS // 2)
    items = [(bb, pr) for bb in range(q_ref.shape[0]) for pr in pairs]
    kd = [[x.astype(BF16) for x in _dup_groups(k_ref[bb])] for bb in range(q_ref.shape[0])]
    vt = [[x.T.astype(BF16) for x in _dup_groups(v_ref[bb])] for bb in range(q_ref.shape[0])]
    outs = _pairs_attention([q_ref[bb, :, pr * LANES:(pr + 1) * LANES] * scale for bb, pr in items],
                            [(sink_ref[2 * pr], sink_ref[2 * pr + 1]) for _, pr in items],
                            [kd[bb][pr // 2] for bb, pr in items], [vt[bb][pr // 2] for bb, pr in items],
                            [None] * len(items))
    for (bb, pr), out in zip(items, outs):
        o_ref[bb, :, pr * LANES:(pr + 1) * LANES] = out


def _attn_ctx(sink, q, k, v):
    b, t, _ = q.shape
    bmap = lambda i: (i, 0, 0)
    return pl.pallas_call(
        _attn_ctx_kernel,
        out_shape=jax.ShapeDtypeStruct((b, t, SWA_Q), F32),
        grid=(b // CTX_BATCH,),
        in_specs=[pl.BlockSpec(memory_space=pltpu.SMEM),
                  pl.BlockSpec((CTX_BATCH, t, SWA_Q), bmap),
                  pl.BlockSpec((CTX_BATCH, t, SWA_KV), bmap),
                  pl.BlockSpec((CTX_BATCH, t, SWA_KV), bmap)],
        out_specs=pl.BlockSpec((CTX_BATCH, t, SWA_Q), bmap),
        compiler_params=pltpu.CompilerParams(dimension_semantics=("arbitrary",),
                                             vmem_limit_bytes=VMEM_LIMIT),
        name="attn_ctx",
    )(sink, q, k, v)


LAT_BLOCKS = 2


def _rope(x, cos, sin_lo, sin_hi):
    return x * cos + pltpu.roll(x, LANES - 16, axis=1) * sin_lo + pltpu.roll(x, 16, axis=1) * sin_hi


def _attn_lat_kernel(sink_ref, q_ref, k_ref, v_ref, kc_ref, vc_ref, cos_ref, sl_ref, sh_ref,
                     o_ref, kw_ref, vw_ref):
    t = q_ref.shape[0]
    ab = ATTN_BLOCK
    nb = t // ab
    scale = SWA_HEAD_DIM ** -0.5

    k_rot = _dup_groups(_rope(k_ref[...], cos_ref[...], sl_ref[...], sh_ref[...]))
    v_dup = _dup_groups(v_ref[...])
    zeros = jnp.zeros((ab, LANES), BF16)
    for grp in range(SWA_KV_HEADS):
        kw_ref[grp, 0:ab, :] = zeros
        kw_ref[grp, ab:ab + t, :] = k_rot[grp].astype(BF16)
        kw_ref[grp, ab + t:, :] = zeros
        vw_ref[grp, 0] = zeros
        for blk in range(nb):
            vw_ref[grp, blk + 1] = v_dup[grp][blk * ab:(blk + 1) * ab, :].T.astype(BF16)
        vw_ref[grp, nb + 1] = zeros
    kc = [x.astype(BF16) for x in _dup_groups(kc_ref[...])]
    vct = [x.T.astype(BF16) for x in _dup_groups(vc_ref[...])]
    lc = kc_ref.shape[0]

    key = lax.broadcasted_iota(jnp.int32, (lc + 3 * ab, 2 * ab), 0) - lc
    tq = lax.broadcasted_iota(jnp.int32, (lc + 3 * ab, 2 * ab), 1) & (ab - 1)
    band = (key < 0) | (jnp.abs(tq + ab - key) <= ab)

    def block(it, carry):
        pairs = range(SWA_HEADS // 2)
        qps, sinks, k_dups, vt_dups, masks, places = [], [], [], [], [], []
        for u in range(LAT_BLOCKS):
            nq = it * LAT_BLOCKS + u
            row0 = pl.multiple_of(nq * ab, ab)
            s_abs = key + (nq - 1) * ab
            mask = band & ((key < 0) | ((s_abs >= 0) & (s_abs < t)))
            cos = cos_ref[pl.ds(row0, ab), :]
            s_lo = sl_ref[pl.ds(row0, ab), :]
            s_hi = sh_ref[pl.ds(row0, ab), :]
            k_all = [jnp.concatenate([kc[grp], kw_ref[grp, pl.ds(row0, 3 * ab), :]], axis=0)
                     for grp in range(SWA_KV_HEADS)]
            vt_all = [jnp.concatenate([vct[grp], vw_ref[grp, nq], vw_ref[grp, nq + 1], vw_ref[grp, nq + 2]],
                                      axis=1) for grp in range(SWA_KV_HEADS)]
            for pr in pairs:
                qps.append(_rope(q_ref[pl.ds(row0, ab), pr * LANES:(pr + 1) * LANES], cos, s_lo, s_hi) * scale)
                sinks.append((sink_ref[2 * pr], sink_ref[2 * pr + 1]))
                k_dups.append(k_all[pr // 2])
                vt_dups.append(vt_all[pr // 2])
                masks.append(mask)
                places.append((row0, pr))
        outs = _pairs_attention(qps, sinks, k_dups, vt_dups, masks)
        for (row0, pr), out in zip(places, outs):
            o_ref[pl.ds(row0, ab), pr * LANES:(pr + 1) * LANES] = out
        return carry

    lax.fori_loop(0, nb // LAT_BLOCKS, block, 0)


def _attn_lat(sink, q, k, v, kc, vc, cos, sin_lo, sin_hi):
    b, t, _ = q.shape
    lc = kc.shape[1]
    bmap = lambda i: (i, 0, 0)
    full = lambda i: (0, 0)
    return pl.pallas_call(
        _attn_lat_kernel,
        out_shape=jax.ShapeDtypeStruct((b, t, SWA_Q), F32),
        grid=(b,),
        in_specs=[pl.BlockSpec(memory_space=pltpu.SMEM),
                  pl.BlockSpec((None, t, SWA_Q), bmap),
                  pl.BlockSpec((None, t, SWA_KV), bmap),
                  pl.BlockSpec((None, t, SWA_KV), bmap),
                  pl.BlockSpec((None, lc, SWA_KV), bmap),
                  pl.BlockSpec((None, lc, SWA_KV), bmap),
                  pl.BlockSpec((t, LANES), full),
                  pl.BlockSpec((t, LANES), full),
                  pl.BlockSpec((t, LANES), full)],
        out_specs=pl.BlockSpec((None, t, SWA_Q), bmap),
        scratch_shapes=[pltpu.VMEM((SWA_KV_HEADS, t + 2 * ATTN_BLOCK, LANES), BF16),
                        pltpu.VMEM((SWA_KV_HEADS, t // ATTN_BLOCK + 2, LANES, ATTN_BLOCK), BF16)],
        compiler_params=pltpu.CompilerParams(dimension_semantics=("arbitrary",),
                                             vmem_limit_bytes=VMEM_LIMIT),
        name="attn_lat",
    )(sink, q, k, v, kc, vc, cos, sin_lo, sin_hi)


def _rope_tables(t):
    half = SWA_HEAD_DIM // 2
    quarter = half // 2
    rows = t // GRID_W
    inv_freq = ROPE_BASE ** (-jnp.arange(quarter, dtype=F32) / quarter)
    reps = LANES // quarter
    ang_row = jnp.tile(jnp.arange(rows).astype(F32)[:, None] * inv_freq[None, :], (1, reps))
    ang_col = jnp.tile(jnp.arange(GRID_W).astype(F32)[:, None] * inv_freq[None, :], (1, reps))
    d = jnp.arange(LANES) % SWA_HEAD_DIM
    use_row = (d < half)[None, :]
    lower = ((d % half) < quarter)[None, :]

    def expand(f):
        by_row = jnp.repeat(f(ang_row), GRID_W, axis=0)
        by_col = jnp.tile(f(ang_col), (rows, 1))
        return jnp.where(use_row, by_row, by_col)

    cos = expand(jnp.cos)
    sin = expand(jnp.sin)
    return cos, jnp.where(lower, -sin, 0.0), jnp.where(lower, 0.0, sin)


def _route(sel, scores):
    n = sel.shape[1]
    gsz = N_EXPERTS // N_EXPERT_GROUPS

    def first_max(x, idx, size):
        m = jnp.max(x, axis=0, keepdims=True)
        first = jnp.min(jnp.where(x == m, idx, float(size)), axis=0, keepdims=True)
        return m, idx == first

    i8 = lax.broadcasted_iota(jnp.int32, (gsz, n), 0).astype(F32)
    rows = []
    for g in range(N_EXPERT_GROUPS):
        slab = sel[g * gsz:(g + 1) * gsz, :]
        m1, hit = first_max(slab, i8, gsz)
        m2 = jnp.max(jnp.where(hit, NEG_INF, slab), axis=0, keepdims=True)
        rows.append(m1 + m2)
    gscore = jnp.concatenate(rows, axis=0)
    gsel = jnp.zeros((N_EXPERT_GROUPS, n), F32)
    for _ in range(TOPK_GROUPS):
        _, hit = first_max(gscore, i8, N_EXPERT_GROUPS)
        gsel = jnp.where(hit, 1.0, gsel)
        gscore = jnp.where(hit, NEG_INF, gscore)
    emask = jnp.concatenate(
        [jnp.broadcast_to(gsel[g:g + 1, :], (gsz, n)) for g in range(N_EXPERT_GROUPS)], axis=0)
    cand = jnp.where(emask > 0.5, sel, NEG_INF)
    ie = lax.broadcasted_iota(jnp.int32, (N_EXPERTS, n), 0).astype(F32)
    w = jnp.zeros((N_EXPERTS, n), F32)
    chosen = jnp.zeros((N_EXPERTS, n), F32)
    hits = []
    for _ in range(TOP_K):
        _, hit = first_max(cand, ie, N_EXPERTS)
        hits.append(hit)
        w = jnp.where(hit, scores, w)
        chosen = jnp.where(hit, 1.0, chosen)
        cand = jnp.where(hit, NEG_INF, cand)
    gates = w / jnp.sum(w, axis=0, keepdims=True) * ROUTED_SCALE

    s_idx = lax.broadcasted_iota(jnp.int32, (n, n), 0)
    t_idx = lax.broadcasted_iota(jnp.int32, (n, n), 1)
    tile_shift = MOE_TILE.bit_length() - 1
    before = jnp.where((s_idx < t_idx) & ((s_idx >> tile_shift) == (t_idx >> tile_shift)), 1.0, 0.0)
    rank = _dot(chosen.astype(BF16), before.astype(BF16))
    e_row = lax.broadcasted_iota(jnp.int32, (N_EXPERTS, N_EXPERTS), 0)
    e_col = lax.broadcasted_iota(jnp.int32, (N_EXPERTS, N_EXPERTS), 1)
    below = jnp.where(e_col < e_row, 1.0, 0.0).astype(BF16)
    lane_tile = lax.broadcasted_iota(jnp.int32, (1, n), 1) >> tile_shift

    def as_row(col):
        return jnp.concatenate([col, jnp.zeros((LANES - N_EXPERTS, LANES), F32)], axis=0).T[0:1, :]

    sizes, starts = [], []
    first_row = jnp.zeros((N_EXPERTS, n), F32)
    for ti in range(n // MOE_TILE):
        count = jnp.sum(chosen[:, ti * MOE_TILE:(ti + 1) * MOE_TILE], axis=1, keepdims=True)
        padded = jnp.floor((count + (SORT_ALIGN - 1)) * (1.0 / SORT_ALIGN)) * SORT_ALIGN
        padded = jnp.broadcast_to(padded, (N_EXPERTS, LANES))
        start = _dot(below, padded.astype(BF16))
        first_row = jnp.where(lane_tile == ti, start[:, 0:1], first_row)
        sizes.append(as_row(padded))
        starts.append(as_row(start))
    row = first_row + rank
    pos = jnp.concatenate([jnp.sum(jnp.where(h, row, 0.0), axis=0, keepdims=True) for h in hits], axis=0)
    wts = jnp.concatenate([jnp.sum(jnp.where(h, gates, 0.0), axis=0, keepdims=True) for h in hits], axis=0)
    return pos, wts, sizes, starts


def _outproj_kernel(gla_ref, att_ref, x_ref, wo_ref, g1_ref, sh_ref, sc_ref, ng_ref, rw_ref, rwh_ref,
                    rb_ref, x1_ref, xm_ref, pos_ref, wts_ref, cnt_ref, start_ref):
    bb, tb, d = x_ref.shape
    tm = bb * tb
    y = (_dot(gla_ref[...].reshape(tm, GLA_V).astype(BF16), wo_ref[0:GLA_V, :])
         + _dot(att_ref[...].reshape(tm, SWA_Q).astype(BF16), wo_ref[GLA_V:, :]))
    x1 = x_ref[...].reshape(tm, d) + g1_ref[...] * y
    x1_ref[...] = x1.reshape(bb, tb, d)
    xm = _rms_norm(x1, ng_ref[...]) * (1.0 + sc_ref[...]) + sh_ref[...]
    xm_hi, xm_lo = _split_hi_lo(xm)
    xm_ref[...] = xm_hi.reshape(bb, tb, d)
    lg = _dot(xm_hi, rw_ref[...])
    logits = lg[:, :N_EXPERTS] + lg[:, N_EXPERTS:] + _dot(xm_lo, rwh_ref[...])
    lt = jnp.concatenate([logits, jnp.zeros((tm, LANES - N_EXPERTS), F32)], axis=1).T[:N_EXPERTS, :]
    scores = _sigmoid(lt)
    pos, wts, sizes, starts = _route(scores + rb_ref[...], scores)
    tiles_per_batch = tb // MOE_TILE
    for ti in range(tm // MOE_TILE):
        at = (ti // tiles_per_batch, ti % tiles_per_batch)
        pos_ref[at] = pos[:, ti * MOE_TILE:(ti + 1) * MOE_TILE]
        wts_ref[at] = wts[:, ti * MOE_TILE:(ti + 1) * MOE_TILE]
        cnt_ref[at] = sizes[ti]
        start_ref[at] = starts[ti]


OUTPROJ_TILE = 1024


def _outproj(gla_out, att_out, x, w_out, g1, sh2, sc2, norm_g, rw_cat, rw_hi, rbias):
    b, t, d = x.shape
    nmod = g1.shape[0]
    tb = min(t, OUTPROJ_TILE)
    bb = OUTPROJ_TILE // tb if nmod == 1 else 1
    tpb = tb // MOE_TILE
    mod_map = (lambda i, j: (i, 0, 0)) if nmod > 1 else (lambda i, j: (0, 0, 0))
    row = lambda i, j: (i, j, 0)
    full = lambda i, j: (0, 0)
    tile = lambda i, j: (i, j, 0, 0)
    nt = t // MOE_TILE
    return pl.pallas_call(
        _outproj_kernel,
        out_shape=(jax.ShapeDtypeStruct((b, t, d), F32),
                   jax.ShapeDtypeStruct((b, t, d), BF16),
                   jax.ShapeDtypeStruct((b, nt, TOP_K, MOE_TILE), F32),
                   jax.ShapeDtypeStruct((b, nt, TOP_K, MOE_TILE), F32),
                   jax.ShapeDtypeStruct((b, nt, 1, LANES), F32),
                   jax.ShapeDtypeStruct((b, nt, 1, LANES), F32)),
        grid=(b // bb, t // tb),
        in_specs=[pl.BlockSpec((bb, tb, GLA_V), row),
                  pl.BlockSpec((bb, tb, SWA_Q), row),
                  pl.BlockSpec((bb, tb, d), row),
                  pl.BlockSpec((d, d), full, pipeline_mode=pl.Buffered(1)),
                  pl.BlockSpec((None, 1, d), mod_map),
                  pl.BlockSpec((None, 1, d), mod_map),
                  pl.BlockSpec((None, 1, d), mod_map),
                  pl.BlockSpec((1, d), full),
                  pl.BlockSpec((d, 2 * N_EXPERTS), full),
                  pl.BlockSpec((d, N_EXPERTS), full),
                  pl.BlockSpec((N_EXPERTS, 1), full)],
        out_specs=(pl.BlockSpec((bb, tb, d), row),
                   pl.BlockSpec((bb, tb, d), row),
                   pl.BlockSpec((bb, tpb, TOP_K, MOE_TILE), tile),
                   pl.BlockSpec((bb, tpb, TOP_K, MOE_TILE), tile),
                   pl.BlockSpec((bb, tpb, 1, LANES), tile),
                   pl.BlockSpec((bb, tpb, 1, LANES), tile)),
        compiler_params=pltpu.CompilerParams(dimension_semantics=("arbitrary", "arbitrary"),
                                             vmem_limit_bytes=VMEM_LIMIT),
        name="outproj",
    )(gla_out, att_out, x, w_out, g1, sh2, sc2, norm_g, rw_cat, rw_hi, rbias)


MOE_TILE = 256
SORT_ALIGN = 16
SORT_ROWS = 3072
ROW_TILE = 512
GATHER_SLOTS = 9
FFN_CHAINS = 4
COMBINE_CHUNK = 1024
ALWAYS_ROWS = 2560
COMBINE_TAIL = 512


def _moe_sort_kernel(tiles_a, used_ref, xa_ref, xb_ref, pos_ref, xs_hbm, ybuf, osem):
    i = pl.program_id(0)
    last = pl.num_programs(0) - 1
    slot = lax.rem(i, 2)
    x = jnp.where(i < tiles_a, xa_ref[...], xb_ref[...])
    pos = pos_ref[...]
    tm = x.shape[0]
    used = used_ref[i]

    def out_copies(tile, of_slot, start):
        def one(r0, r1):
            row = pl.multiple_of(tile * SORT_ROWS + r0, tm)
            cp = pltpu.make_async_copy(ybuf.at[of_slot, r0:r1, :], xs_hbm.at[pl.ds(row, r1 - r0), :],
                                       osem.at[of_slot])
            if start:
                cp.start()
            else:
                cp.wait()

        one(0, ALWAYS_ROWS)
        for r0 in range(ALWAYS_ROWS, SORT_ROWS, tm):
            pl.when(r0 < used_ref[tile])(functools.partial(one, r0, r0 + tm))

    @pl.when(i >= 2)
    def _():
        out_copies(i - 2, slot, False)

    rows = lax.broadcasted_iota(jnp.int32, (tm, tm), 0).astype(F32).astype(BF16)
    one_bf = jnp.ones((tm, tm), BF16)

    def fill(blk):
        local = (pos - float(blk * tm)).astype(BF16)
        onehot = jnp.zeros((tm, tm), BF16)
        for k in range(TOP_K):
            onehot = jnp.where(rows == local[k:k + 1, :], one_bf, onehot)
        ybuf[slot, blk * tm:(blk + 1) * tm, :] = _dot(onehot, x).astype(BF16)

    for blk in range(SORT_ROWS // tm):
        if (blk + 1) * tm <= ALWAYS_ROWS:
            fill(blk)
        else:
            pl.when(blk * tm < used)(functools.partial(fill, blk))
    out_copies(i, slot, True)

    @pl.when(i == last)
    def _():
        out_copies(i, slot, False)

        @pl.when(i >= 1)
        def _():
            out_copies(i - 1, 1 - slot, False)


def _moe_sort(xm_a, xm_b, pos, used):
    d = xm_a.shape[1]
    nt, _, tm = pos.shape
    tiles_a = xm_a.shape[0] // tm
    grid_spec = pltpu.PrefetchScalarGridSpec(
        num_scalar_prefetch=1,
        grid=(nt,),
        in_specs=[pl.BlockSpec((tm, d), lambda i, u: (jnp.minimum(i, tiles_a - 1), 0)),
                  pl.BlockSpec((tm, d), lambda i, u: (jnp.maximum(i - tiles_a, 0), 0)),
                  pl.BlockSpec((None, TOP_K, tm), lambda i, u: (i, 0, 0))],
        out_specs=pl.BlockSpec(memory_space=pl.ANY),
        scratch_shapes=[pltpu.VMEM((2, SORT_ROWS, d), BF16),
                        pltpu.SemaphoreType.DMA((2,))])
    return pl.pallas_call(
        functools.partial(_moe_sort_kernel, tiles_a),
        out_shape=jax.ShapeDtypeStruct((nt * SORT_ROWS, d), BF16),
        grid_spec=grid_spec,
        compiler_params=pltpu.CompilerParams(dimension_semantics=("arbitrary",),
                                             vmem_limit_bytes=VMEM_LIMIT),
        name="moe_sort",
    )(used, xm_a, xm_b, pos)


def _moe_row_tiles(n_tokens):
    rows = n_tokens * TOP_K + (n_tokens // MOE_TILE) * N_EXPERTS * (SORT_ALIGN - 1) + N_EXPERTS * (ROW_TILE - 1)
    return -(-rows // ROW_TILE) + GATHER_SLOTS - 1


PLAN_CHUNK = 1280


def _int_dot_r(a, onehot):
    hi = jnp.floor(a * (1.0 / 256.0))
    return _dot(hi.astype(BF16), onehot) * 256.0 + _dot((a - hi * 256.0).astype(BF16), onehot)


def _int_dot_l(onehot, b):
    hi = jnp.floor(b * (1.0 / 256.0))
    return _dot(onehot, hi.astype(BF16)) * 256.0 + _dot(onehot, (b - hi * 256.0).astype(BF16))


def _moe_plan_kernel(cnt_ref, start_ref, src_ref, first_ref, tiles_ref, nu_ref, back_ref):
    nt, ne = cnt_ref.shape
    gpt = SORT_ROWS // SORT_ALIGN
    gpr = ROW_TILE // SORT_ALIGN
    gc = cnt_ref[...] * (1.0 / SORT_ALIGN)
    ls = start_ref[...] * (1.0 / SORT_ALIGN)

    def transpose(x):
        x = jnp.concatenate([x, jnp.zeros((nt, LANES - ne), F32)], axis=1)
        x = jnp.concatenate([x, jnp.zeros((LANES - nt, LANES), F32)], axis=0)
        return x.T[:ne, :nt]

    def tri(n, keep):
        return jnp.where(keep(lax.broadcasted_iota(jnp.int32, (n, n), 0),
                              lax.broadcasted_iota(jnp.int32, (n, n), 1)), 1.0, 0.0).astype(BF16)

    gc_t = transpose(gc)
    ls_t = transpose(ls)
    tot_c = jnp.broadcast_to(jnp.sum(gc_t, axis=1, keepdims=True), (ne, LANES))
    ptot_c = jnp.floor((tot_c + (gpr - 1)) * (1.0 / gpr)) * gpr
    gend_c = _int_dot_l(tri(ne, lambda r, c: c <= r), ptot_c)
    gstart_c = gend_c - ptot_c
    n_used = gend_c[ne - 1:ne, :] * (1.0 / gpr)
    nu_ref[...] = n_used.astype(jnp.int32)
    tot_r = jnp.sum(gc, axis=0, keepdims=True)
    ptot_r = jnp.floor((tot_r + (gpr - 1)) * (1.0 / gpr)) * gpr
    gstart_r = _int_dot_r(jnp.broadcast_to(ptot_r, (8, ne)), tri(ne, lambda r, c: r < c))
    cumex = _dot(tri(nt, lambda r, c: c < r), gc.astype(BF16))
    cumex_t = _dot(gc_t.astype(BF16), tri(nt, lambda r, c: r < c))
    tile_base = lax.broadcasted_iota(jnp.int32, (nt, ne), 0).astype(F32) * gpt + ls
    table = jnp.concatenate([cumex + gc, cumex, tile_base, gstart_r, jnp.broadcast_to(tot_r, (8, ne))], axis=0)

    e_iota = lax.broadcasted_iota(jnp.int32, (ne, PLAN_CHUNK), 0).astype(F32)
    for ch in range(src_ref.shape[1] // PLAN_CHUNK):
        g = (lax.broadcasted_iota(jnp.int32, (1, PLAN_CHUNK), 1) + ch * PLAN_CHUNK).astype(F32)
        eg = jnp.sum(jnp.where(gend_c[:, 0:1] <= g, 1.0, 0.0), axis=0, keepdims=True)
        picked = _int_dot_r(table, jnp.where(e_iota == eg, 1.0, 0.0).astype(BF16))
        cum_g, cumex_g, base_g = picked[0:nt], picked[nt:2 * nt], picked[2 * nt:3 * nt]
        u = g - picked[3 * nt:3 * nt + 1]
        in_tile = (cumex_g <= u) & (u < cum_g)
        src = jnp.sum(jnp.where(in_tile, base_g - cumex_g, 0.0), axis=0, keepdims=True) + u
        src = jnp.where(u < picked[3 * nt + 8:3 * nt + 9], src, 0.0)
        src_ref[:, ch * PLAN_CHUNK:(ch + 1) * PLAN_CHUNK] = src.astype(jnp.int32)

    first_ref[...] = (gstart_c * (1.0 / gpr)).astype(jnp.int32)
    tiles_ref[...] = (ptot_c * (1.0 / gpr)).astype(jnp.int32)

    lg = lax.broadcasted_iota(jnp.int32, (ne, back_ref.shape[1]), 1).astype(F32)
    for t in range(nt):
        first = ls_t[:, t:t + 1]
        inside = (first <= lg) & (lg < first + gc_t[:, t:t + 1])
        shift = gstart_c[:, 0:1] + cumex_t[:, t:t + 1] - first
        val = jnp.sum(jnp.where(inside, shift + lg, 0.0), axis=0, keepdims=True)
        back_ref[t:t + 1, :] = val.astype(jnp.int32)


def _moe_plan(cnt, start):
    nt, ne = cnt.shape
    row_tiles = _moe_row_tiles(nt * MOE_TILE)
    gpt = SORT_ROWS // SORT_ALIGN
    gpr = ROW_TILE // SORT_ALIGN
    n_src = -(-(row_tiles * gpr) // PLAN_CHUNK) * PLAN_CHUNK
    n_back = -(-gpt // LANES) * LANES
    src, first, tiles, nu, back = pl.pallas_call(
        _moe_plan_kernel,
        out_shape=(jax.ShapeDtypeStruct((1, n_src), jnp.int32),
                   jax.ShapeDtypeStruct((ne, LANES), jnp.int32),
                   jax.ShapeDtypeStruct((ne, LANES), jnp.int32),
                   jax.ShapeDtypeStruct((1, LANES), jnp.int32),
                   jax.ShapeDtypeStruct((nt, n_back), jnp.int32)),
        compiler_params=pltpu.CompilerParams(vmem_limit_bytes=VMEM_LIMIT),
        name="moe_plan",
    )(cnt, start)
    return nu[0, :1], first[:, 0], tiles[:, 0], src[0, :row_tiles * gpr], back[:, :gpt]


def _moe_experts_kernel(nu_ref, first_ref, tiles_ref, src_ref, xs_hbm, wg_ref, wu_ref, wd_ref, ys_hbm,
                        xbuf, ybuf, gsem, osem, wgu_s, wd_s):
    e = pl.program_id(0)
    n_used = nu_ref[0]
    gpr = ROW_TILE // SORT_ALIGN
    part = ROW_TILE // FFN_CHAINS

    def gather(tile, to_slot, j0=0, j1=gpr):
        for j in range(j0, j1):
            row = pl.multiple_of(src_ref[tile * gpr + j] * SORT_ALIGN, SORT_ALIGN)
            pltpu.make_async_copy(xs_hbm.at[pl.ds(row, SORT_ALIGN), :],
                                  xbuf.at[to_slot, j * SORT_ALIGN:(j + 1) * SORT_ALIGN, :],
                                  gsem.at[to_slot]).start(priority=j % 2)

    def drain(of_slot):
        for j in range(gpr):
            pltpu.make_async_copy(xs_hbm.at[0:SORT_ALIGN, :],
                                  xbuf.at[of_slot, j * SORT_ALIGN:(j + 1) * SORT_ALIGN, :], gsem.at[of_slot]).wait()

    def out_copy(tile, of_slot):
        row = pl.multiple_of(tile * ROW_TILE, ROW_TILE)
        return pltpu.make_async_copy(ybuf.at[of_slot], ys_hbm.at[pl.ds(row, ROW_TILE), :], osem.at[of_slot])

    @pl.when(e == 0)
    def _():
        for ahead in range(GATHER_SLOTS - 1):
            gather(ahead, ahead)

    wgu_s[:, :EXPERT_FF] = wg_ref[...].astype(BF16)
    wgu_s[:, EXPERT_FF:] = wu_ref[...].astype(BF16)
    wd_s[...] = wd_ref[...].astype(BF16)

    def row_tile(i, carry):
        r = first_ref[e] + i
        slot = lax.rem(r, GATHER_SLOTS)
        oslot = lax.rem(r, 2)
        next_slot = lax.rem(r + GATHER_SLOTS - 1, GATHER_SLOTS)
        drain(slot)

        @pl.when(r >= 2)
        def _():
            out_copy(r - 2, oslot).wait()

        abs_ = []
        for c in range(FFN_CHAINS):
            abs_.append(_dot(xbuf[slot, c * part:(c + 1) * part, :], wgu_s[...]))
            gather(r + GATHER_SLOTS - 1, next_slot, c * gpr // FFN_CHAINS, (c + 1) * gpr // FFN_CHAINS)
        hs = [(_silu(ab[:, :EXPERT_FF]) * ab[:, EXPERT_FF:]).astype(BF16) for ab in abs_]
        ys = [_dot(h, wd_s[...]).astype(BF16) for h in hs]
        for c in range(FFN_CHAINS):
            ybuf[oslot, c * part:(c + 1) * part, :] = ys[c]
        out_copy(r, oslot).start()
        return carry

    lax.fori_loop(0, tiles_ref[e], row_tile, 0)

    @pl.when(e == pl.num_programs(0) - 1)
    def _():
        for ahead in range(GATHER_SLOTS - 1):
            drain(lax.rem(n_used + ahead, GATHER_SLOTS))
        out_copy(n_used - 1, lax.rem(n_used - 1, 2)).wait()

        @pl.when(n_used >= 2)
        def _():
            out_copy(n_used - 2, lax.rem(n_used, 2)).wait()


def _moe_experts(n_used, first, tiles, src, xs, wg, wu, wd, row_tiles):
    d = xs.shape[-1]
    ne = wg.shape[0]
    w_map = lambda e, nu, fi, ti, sr: (e, 0, 0)
    grid_spec = pltpu.PrefetchScalarGridSpec(
        num_scalar_prefetch=4,
        grid=(ne,),
        in_specs=[pl.BlockSpec(memory_space=pl.ANY),
                  pl.BlockSpec((None, d, EXPERT_FF), w_map),
                  pl.BlockSpec((None, d, EXPERT_FF), w_map),
                  pl.BlockSpec((None, EXPERT_FF, d), w_map)],
        out_specs=pl.BlockSpec(memory_space=pl.ANY),
        scratch_shapes=[pltpu.VMEM((GATHER_SLOTS, ROW_TILE, d), BF16),
                        pltpu.VMEM((2, ROW_TILE, d), BF16),
                        pltpu.SemaphoreType.DMA((GATHER_SLOTS,)),
                        pltpu.SemaphoreType.DMA((2,)),
                        pltpu.VMEM((d, 2 * EXPERT_FF), BF16),
                        pltpu.VMEM((EXPERT_FF, d), BF16)])
    return pl.pallas_call(
        _moe_experts_kernel,
        out_shape=jax.ShapeDtypeStruct((row_tiles * ROW_TILE, d), BF16),
        grid_spec=grid_spec,
        compiler_params=pltpu.CompilerParams(dimension_semantics=("arbitrary",),
                                             vmem_limit_bytes=VMEM_LIMIT),
        name="moe_experts",
    )(n_used, first, tiles, src, xs, wg, wu, wd)


def _moe_combine_kernel(back_ref, used_ref, ys_hbm, pos_ref, wts_ref, xm_ref, x1_ref, g2_ref, fg_ref,
                        swg_ref, swu_ref, swd_ref, o_ref, buf, sem, acc_ref):
    i = pl.program_id(0)
    gpt = SORT_ROWS // SORT_ALIGN
    slot = lax.rem(i, 2)
    always = ALWAYS_ROWS
    tail = range(always, SORT_ROWS, COMBINE_TAIL)

    def copies(tile, of_slot, g0, g1, start):
        for g in range(g0, g1):
            row = pl.multiple_of(back_ref[tile * gpt + g] * SORT_ALIGN, SORT_ALIGN) if start else 0
            cp = pltpu.make_async_copy(ys_hbm.at[pl.ds(row, SORT_ALIGN), :],
                                       buf.at[of_slot, g * SORT_ALIGN:(g + 1) * SORT_ALIGN, :], sem.at[of_slot])
            if start:
                cp.start(priority=g % 2)
            else:
                cp.wait()

    def transfer_tail(tile, of_slot, start):
        for c0 in tail:
            pl.when(c0 < used_ref[tile])(functools.partial(
                copies, tile, of_slot, c0 // SORT_ALIGN, (c0 + COMBINE_TAIL) // SORT_ALIGN, start))

    def transfer(tile, of_slot, start):
        copies(tile, of_slot, 0, always // SORT_ALIGN, start)
        transfer_tail(tile, of_slot, start)

    @pl.when(i == 0)
    def _():
        transfer(0, 0, True)

    nxt = jnp.minimum(i + 1, pl.num_programs(0) - 1)
    nxt_slot = 1 - slot
    transfer_tail(nxt, nxt_slot, True)
    always_groups = always // SORT_ALIGN
    spread = [always_groups * part // 4 for part in range(5)]

    x = xm_ref[...]
    tm = x.shape[0]
    pad = jnp.zeros((LANES - TOP_K, tm), F32)
    pos_t = jnp.concatenate([pos_ref[...], pad], axis=0).T
    wts_t = jnp.concatenate([wts_ref[...], pad], axis=0).T
    copies(nxt, nxt_slot, spread[0], spread[1], True)
    blk_b, loc_b, wts_b = [], [], []
    for k in range(TOP_K):
        p = jnp.broadcast_to(pos_t[:, k:k + 1], (tm, LANES))
        blk = jnp.floor(p * (1.0 / tm))
        two = lambda v: jnp.concatenate([v.astype(BF16)] * (tm // LANES), axis=1)
        blk_b.append(two(blk))
        loc_b.append(two(p - blk * tm))
        wts_b.append(two(jnp.broadcast_to(wts_t[:, k:k + 1], (tm, LANES))))
    copies(nxt, nxt_slot, spread[1], spread[2], True)
    shared = _dot((_silu(_dot(x, swg_ref[...])) * _dot(x, swu_ref[...])).astype(BF16), swd_ref[...])
    copies(nxt, nxt_slot, spread[2], spread[3], True)
    transfer(i, slot, False)
    lane = lax.broadcasted_iota(jnp.int32, (tm, tm), 1).astype(F32).astype(BF16)
    zero = jnp.zeros((tm, tm), BF16)
    nowhere = jnp.full((tm, tm), -1.0, BF16)

    def apply(c0, width):
        blocks = []
        for b0 in range(c0, c0 + width, tm):
            comb = zero
            for k in range(TOP_K):
                loc = jnp.where(blk_b[k] == float(b0 // tm), loc_b[k], nowhere)
                comb = jnp.where(lane == loc, wts_b[k], comb)
            blocks.append(comb)
        return _dot(jnp.concatenate(blocks, axis=1), buf[slot, c0:c0 + width, :])

    routed = shared
    for c0 in range(0, always, COMBINE_CHUNK):
        routed = routed + apply(c0, min(COMBINE_CHUNK, always - c0))
        if c0 == 0:
            copies(nxt, nxt_slot, spread[3], spread[4], True)
    acc_ref[...] = routed
    for c0 in tail:
        @pl.when(c0 < used_ref[i])
        def _(c0=c0):
            acc_ref[...] += apply(c0, COMBINE_TAIL)
    y = x1_ref[...] + g2_ref[...] * acc_ref[...]
    o_ref[...] = _rms_norm(y, fg_ref[...])

    @pl.when(i == pl.num_programs(0) - 1)
    def _():
        transfer(i, nxt_slot, False)


def _moe_combine(back, used, ys, pos, wts, xm, x1, g2, final_g, swg, swu, swd, *, tiles_per_mod):
    n, d = xm.shape
    tm = pos.shape[-1]
    nt = n // tm
    gpt = SORT_ROWS // SORT_ALIGN
    row = lambda i, bk, us: (i, 0)
    full = lambda i, bk, us: (0, 0)
    tile = lambda i, bk, us: (i, 0, 0)
    mod_map = lambda i, bk, us: (i // tiles_per_mod, 0, 0)
    grid_spec = pltpu.PrefetchScalarGridSpec(
        num_scalar_prefetch=2,
        grid=(nt,),
        in_specs=[pl.BlockSpec(memory_space=pl.ANY),
                  pl.BlockSpec((None, TOP_K, tm), tile),
                  pl.BlockSpec((None, TOP_K, tm), tile),
                  pl.BlockSpec((tm, d), row),
                  pl.BlockSpec((tm, d), row),
                  pl.BlockSpec((None, 1, d), mod_map),
                  pl.BlockSpec((1, d), full),
                  pl.BlockSpec((d, SHARED_FF), full),
                  pl.BlockSpec((d, SHARED_FF), full),
                  pl.BlockSpec((SHARED_FF, d), full)],
        out_specs=pl.BlockSpec((tm, d), row),
        scratch_shapes=[pltpu.VMEM((2, SORT_ROWS, d), BF16),
                        pltpu.SemaphoreType.DMA((2,)),
                        pltpu.VMEM((tm, d), F32)])
    return pl.pallas_call(
        _moe_combine_kernel,
        out_shape=jax.ShapeDtypeStruct((n, d), F32),
        grid_spec=grid_spec,
        compiler_params=pltpu.CompilerParams(dimension_semantics=("arbitrary",),
                                             vmem_limit_bytes=VMEM_LIMIT),
        name="moe_combine",
    )(back, used, ys, pos.reshape(nt, TOP_K, tm), wts.reshape(nt, TOP_K, tm), xm, x1, g2, final_g, swg, swu, swd)


def _mix(x, mods, p, attn_fn, s0=None):
    sh1, sc1, g1, sh2, sc2, _ = mods
    gla_in, lora, q_s, k_s, v_s = _inproj(x, p["norm_attn_g"], sh1, sc1, p["w_gla"], p["w_lora"], p["w_swa"])
    if s0 is None:
        gla_out, s_f, s_b = _gla(gla_in, lora, p["waf"], p["baf"], p["wab"], p["bab"], p["gla_norm_g"])
    else:
        gla_out, s_f, s_b = _gla(gla_in, lora, p["waf"], p["baf"], p["wab"], p["bab"], p["gla_norm_g"],
                                 s0[0], s0[1])
    att_out = attn_fn(q_s, k_s, v_s)
    routed = _outproj(gla_out, att_out, x, p["w_out"], g1, sh2, sc2, p["norm_ffn_g"],
                      p["rw_cat"], p["rw_hi"], p["rbias"])
    return routed, k_s, v_s, s_f, s_b


def _moe(streams, p):
    d = D_MODEL
    (ra, _), (rb, _) = streams
    n_tiles = [r[1].shape[0] * r[1].shape[1] // MOE_TILE for r, _ in streams]
    pos_all = jnp.concatenate([r[2].reshape(-1, TOP_K, MOE_TILE) for r, _ in streams], axis=0)
    cnt_all = jnp.concatenate([r[4].reshape(-1, LANES) for r, _ in streams], axis=0)[:, :N_EXPERTS]
    start_all = jnp.concatenate([r[5].reshape(-1, LANES) for r, _ in streams], axis=0)[:, :N_EXPERTS]
    used = (start_all[:, -1] + cnt_all[:, -1]).astype(jnp.int32)
    xs = _moe_sort(ra[1].reshape(-1, d), rb[1].reshape(-1, d), pos_all, used)
    n_used, first, tiles, src, back = _moe_plan(cnt_all, start_all)
    ys = _moe_experts(n_used, first, tiles, src, xs, p["wg"], p["wu"], p["wd"],
                      _moe_row_tiles(cnt_all.shape[0] * MOE_TILE))
    outs = []
    tile0 = 0
    for ((x1, xm, pos, wts, cnt, start), g2), nt in zip(streams, n_tiles):
        b, t, _ = x1.shape
        tiles_per_mod = (t // MOE_TILE) if g2.shape[0] > 1 else nt
        y = _moe_combine(back[tile0:tile0 + nt].reshape(-1), used[tile0:tile0 + nt], ys, pos, wts,
                         xm.reshape(-1, d), x1.reshape(-1, d), g2, p["final_norm_g"],
                         p["swg"], p["swu"], p["swd"], tiles_per_mod=tiles_per_mod)
        outs.append(y.reshape(b, t, d))
        tile0 += nt
    return outs


def kernel(x_prompt, x_sample, c, cache_swa_k, cache_swa_v, state_gla_fwd, state_gla_bwd, c_ctx, w_ada, b_ada, norm_attn_g, norm_ffn_g, w_in, gla_wa_f, gla_ba_f, gla_wa_b, gla_ba_b, gla_norm_g, swa_sink, w_out, router_w, router_bias, exp_w_gate, exp_w_up, exp_w_down, sh_w_gate, sh_w_up, sh_w_down, final_norm_g):
    l = 0
    d = D_MODEL
    nb_ctx, t_ctx, _ = x_prompt.shape
    nb_lat, t_lat, _ = x_sample.shape

    pad = jnp.zeros((8 - 1 - nb_lat, d), F32)
    cond8 = jnp.concatenate([c_ctx[None, :], c, pad], axis=0)
    mod = _adaln(cond8, w_ada[l], b_ada[l][None, :])
    mods_ctx = [mod[0:1, i * d:(i + 1) * d][:, None, :] for i in range(6)]
    mods_lat = [mod[1:1 + nb_lat, i * d:(i + 1) * d][:, None, :] for i in range(6)]

    zeros_lora = jnp.zeros((GLA_LORA, GLA_QK), F32)
    rw = router_w[l]
    rw_hi = rw.astype(BF16)
    rw_lo = (rw - rw_hi.astype(F32)).astype(BF16)
    n_gla = 2 * GLA_QK + 2 * GLA_V
    p = {
        "norm_attn_g": norm_attn_g[l][None, :],
        "norm_ffn_g": norm_ffn_g[l][None, :],
        "final_norm_g": final_norm_g[None, :],
        "w_gla": w_in[l][:, :n_gla].astype(BF16),
        "w_lora": w_in[l][:, n_gla:n_gla + 2 * GLA_LORA].astype(BF16),
        "w_swa": w_in[l][:, n_gla + 2 * GLA_LORA:].astype(BF16),
        "waf": jnp.concatenate([gla_wa_f[l], zeros_lora], axis=0).astype(BF16),
        "wab": jnp.concatenate([zeros_lora, gla_wa_b[l]], axis=0).astype(BF16),
        "baf": gla_ba_f[l][None, :],
        "bab": gla_ba_b[l][None, :],
        "gla_norm_g": gla_norm_g[l][None, :],
        "w_out": w_out[l].astype(BF16),
        "rw_cat": jnp.concatenate([rw_hi, rw_lo], axis=1),
        "rw_hi": rw_hi,
        "rbias": router_bias[l][:, None],
        "wg": exp_w_gate[l], "wu": exp_w_up[l], "wd": exp_w_down[l],
        "swg": sh_w_gate[l].astype(BF16), "swu": sh_w_up[l].astype(BF16),
        "swd": sh_w_down[l].astype(BF16),
    }
    sink = swa_sink[l]

    routed_ctx, k_c, v_c, s_f, s_b = _mix(x_prompt, mods_ctx, p, functools.partial(_attn_ctx, sink))

    cos, sin_lo, sin_hi = _rope_tables(t_lat)
    kc = cache_swa_k[:, l].reshape(nb_lat, -1, SWA_KV)
    vc = cache_swa_v[:, l].reshape(nb_lat, -1, SWA_KV)
    lat_attn = lambda q, k, v: _attn_lat(sink, q, k, v, kc, vc, cos, sin_lo, sin_hi)
    s0 = (state_gla_fwd[:, l].reshape(nb_lat, GLA_QK, GLA_DV),
          state_gla_bwd[:, l].reshape(nb_lat, GLA_QK, GLA_DV))
    routed_lat, _, _, _, _ = _mix(x_sample, mods_lat, p, lat_attn, s0)
    y_prompt, y_sample = _moe([(routed_ctx, mods_ctx[5]), (routed_lat, mods_lat[5])], p)

    new_k = k_c.reshape(nb_ctx, 1, t_ctx, SWA_KV_HEADS, SWA_HEAD_DIM)
    new_v = v_c.reshape(nb_ctx, 1, t_ctx, SWA_KV_HEADS, SWA_HEAD_DIM)
    new_sf = s_f.reshape(nb_ctx, 1, GLA_HEADS, GLA_DK, GLA_DV)
    new_sb = s_b.reshape(nb_ctx, 1, GLA_HEADS, GLA_DK, GLA_DV)
    return (y_prompt, y_sample, new_k, new_v, new_sf, new_sb)
```

```python
import functools

import jax
import jax.numpy as jnp
from jax import lax
from jax.experimental import pallas as pl
from jax.experimental.pallas import tpu as pltpu

F32 = jnp.float32
BF16 = jnp.bfloat16

D_MODEL = 1024
GLA_HEADS = 4
GLA_DK = 64
GLA_DV = 128
GLA_LORA = 16
GLA_GATE_NORM = 16.0
GLA_CHUNK = 64
GLA_QK = GLA_HEADS * GLA_DK
GLA_V = GLA_HEADS * GLA_DV
SWA_HEAD_DIM = 64
SWA_HEADS = 8
SWA_KV_HEADS = 2
SWA_Q = SWA_HEADS * SWA_HEAD_DIM
SWA_KV = SWA_KV_HEADS * SWA_HEAD_DIM
ATTN_BLOCK = 128
GRID_W = 64
ROPE_BASE = 10000.0
N_EXPERTS = 64
TOP_K = 8
N_EXPERT_GROUPS = 8
TOPK_GROUPS = 4
EXPERT_FF = 128
SHARED_FF = 256
ROUTED_SCALE = 2.5
EPS = 1e-6

LANES = 128
VMEM_LIMIT = 56 * 1024 * 1024

NEG_INF = float("-inf")


def _dot(a, b):
    return jnp.dot(a, b, preferred_element_type=F32)


def _dot_nt(a, b):
    return lax.dot_general(a, b, (((1,), (1,)), ((), ())), preferred_element_type=F32)


def _split_hi_lo(x):
    hi = x.astype(BF16)
    lo = (x - hi.astype(F32)).astype(BF16)
    return hi, lo


def _sigmoid(x):
    return 1.0 / (1.0 + jnp.exp(-x))


def _silu(x):
    return x * _sigmoid(x)


def _rms_norm(x, g):
    ms = jnp.mean(x * x, axis=-1, keepdims=True)
    return x * lax.rsqrt(ms + EPS) * g


def _adaln_kernel(c_ref, w_ref, b_ref, o_ref):
    a_hi, a_lo = _split_hi_lo(_silu(c_ref[...]))
    w_hi, w_lo = _split_hi_lo(w_ref[...])
    o_ref[...] = _dot(a_hi, w_hi) + _dot(a_lo, w_hi) + _dot(a_hi, w_lo) + b_ref[...]


def _adaln(cond8, w_ada, b_ada):
    n = w_ada.shape[1]
    tn = 1536
    return pl.pallas_call(
        _adaln_kernel,
        out_shape=jax.ShapeDtypeStruct((8, n), F32),
        grid=(n // tn,),
        in_specs=[pl.BlockSpec((8, D_MODEL), lambda j: (0, 0)),
                  pl.BlockSpec((D_MODEL, tn), lambda j: (0, j)),
                  pl.BlockSpec((1, tn), lambda j: (0, j))],
        out_specs=pl.BlockSpec((8, tn), lambda j: (0, j)),
        compiler_params=pltpu.CompilerParams(dimension_semantics=("arbitrary",),
                                             vmem_limit_bytes=VMEM_LIMIT),
        name="adaln",
    )(cond8, w_ada, b_ada)


def _inproj_kernel(x_ref, g_ref, sh_ref, sc_ref, wg_ref, wl_ref, ws_ref,
                   gla_ref, lora_ref, q_ref, k_ref, v_ref):
    bb, tb, d = x_ref.shape
    x = x_ref[...].reshape(bb * tb, d)
    h = _rms_norm(x, g_ref[...]) * (1.0 + sc_ref[...]) + sh_ref[...]
    hb = h.astype(BF16)
    gla_ref[...] = _dot(hb, wg_ref[...]).reshape(gla_ref.shape)
    lora_ref[...] = _dot(hb, wl_ref[...]).reshape(lora_ref.shape)
    s = _dot(hb, ws_ref[...])
    q_ref[...] = s[:, :SWA_Q].reshape(q_ref.shape)
    k_ref[...] = s[:, SWA_Q:SWA_Q + SWA_KV].reshape(k_ref.shape)
    v_ref[...] = s[:, SWA_Q + SWA_KV:].reshape(v_ref.shape)


INPROJ_TILE = 1024


def _inproj(x, g, sh, sc, w_gla, w_lora, w_swa):
    b, t, d = x.shape
    nmod = sh.shape[0]
    tb = min(t, INPROJ_TILE)
    bb = INPROJ_TILE // tb if nmod == 1 else 1
    mod_map = (lambda i, j: (i, 0, 0)) if nmod > 1 else (lambda i, j: (0, 0, 0))
    row = lambda i, j: (i, j, 0)
    full = lambda i, j: (0, 0)
    n_gla = w_gla.shape[1]
    n_lora = w_lora.shape[1]
    return pl.pallas_call(
        _inproj_kernel,
        out_shape=(jax.ShapeDtypeStruct((b, t, n_gla), F32),
                   jax.ShapeDtypeStruct((b, t, n_lora), F32),
                   jax.ShapeDtypeStruct((b, t, SWA_Q), F32),
                   jax.ShapeDtypeStruct((b, t, SWA_KV), F32),
                   jax.ShapeDtypeStruct((b, t, SWA_KV), F32)),
        grid=(b // bb, t // tb),
        in_specs=[pl.BlockSpec((bb, tb, d), row),
                  pl.BlockSpec((1, d), full),
                  pl.BlockSpec((None, 1, d), mod_map),
                  pl.BlockSpec((None, 1, d), mod_map),
                  pl.BlockSpec((d, n_gla), full, pipeline_mode=pl.Buffered(1)),
                  pl.BlockSpec((d, n_lora), full, pipeline_mode=pl.Buffered(1)),
                  pl.BlockSpec((d, w_swa.shape[1]), full, pipeline_mode=pl.Buffered(1))],
        out_specs=(pl.BlockSpec((bb, tb, n_gla), row),
                   pl.BlockSpec((bb, tb, n_lora), row),
                   pl.BlockSpec((bb, tb, SWA_Q), row),
                   pl.BlockSpec((bb, tb, SWA_KV), row),
                   pl.BlockSpec((bb, tb, SWA_KV), row)),
        compiler_params=pltpu.CompilerParams(dimension_semantics=("arbitrary", "arbitrary"),
                                             vmem_limit_bytes=VMEM_LIMIT),
        name="inproj",
    )(x, g, sh, sc, w_gla, w_lora, w_swa)


SCAN_UNROLL = 4
OUT_UNROLL = 4


def _log_sigmoid(x):
    return jnp.minimum(x, 0.0) - jnp.log(1.0 + jnp.exp(-jnp.abs(x)))


def _heads_to_rows(x):
    return jnp.concatenate([x[:, h * LANES:(h + 1) * LANES] for h in range(GLA_HEADS)], axis=0)


def _rows_to_heads(x, c):
    return jnp.concatenate([x[h * c:(h + 1) * c, :] for h in range(GLA_HEADS)], axis=1)


def _gla_kernel(has_init, q_ref, k_ref, v_ref, g_ref, lora_ref, waf_ref, baf_ref, wab_ref, bab_ref,
                ng_ref, *rest):
    if has_init:
        s0f_ref, s0b_ref, *rest = rest
    (out_ref, sf_ref, sb_ref, laf_ref, lab_ref, oacc_ref, qtf_ref, qtb_ref, saf_ref, sab_ref,
     stf_ref, stb_ref) = rest
    t = q_ref.shape[0]
    c = GLA_CHUNK
    n = t // c
    hc = GLA_HEADS * c

    lora = lora_ref[...].astype(BF16)
    laf_ref[...] = _log_sigmoid(_dot(lora, waf_ref[...]) + baf_ref[...]) * (1.0 / GLA_GATE_NORM)
    lab_ref[...] = _log_sigmoid(_dot(lora, wab_ref[...]) + bab_ref[...]) * (1.0 / GLA_GATE_NORM)

    if has_init:
        stf_ref[...] = s0f_ref[...].T
        stb_ref[...] = s0b_ref[...].T
    else:
        stf_ref[...] = jnp.zeros_like(stf_ref)
        stb_ref[...] = jnp.zeros_like(stb_ref)
    oacc_ref[...] = jnp.zeros_like(oacc_ref)

    r64 = lax.broadcasted_iota(jnp.int32, (c, c), 0)
    c64 = lax.broadcasted_iota(jnp.int32, (c, c), 1)
    tri_f = jnp.where(c64 <= r64, 1.0, 0.0).astype(BF16)
    tri_b = jnp.where(c64 >= r64, 1.0, 0.0).astype(BF16)
    rr = lax.broadcasted_iota(jnp.int32, (hc, hc), 0)
    cc = lax.broadcasted_iota(jnp.int32, (hc, hc), 1)
    same_head = (rr >> 6) == (cc >> 6)
    keep_f = same_head & ((rr & (c - 1)) >= (cc & (c - 1)))
    keep_b = same_head & ((rr & (c - 1)) <= (cc & (c - 1)))
    head_mask = jnp.where(same_head, 1.0, 0.0).astype(BF16)
    norm_g = ng_ref[...]

    def chunk_rows(ci):
        return pl.ds(pl.multiple_of(ci * c, c), c)

    def tile_heads(x):
        x4 = jnp.concatenate([x] * GLA_HEADS, axis=0)
        return jnp.where(same_head, x4, 0.0).astype(BF16)

    def scan_step(i, carry):
        dirs = []
        for u in range(SCAN_UNROLL):
            dirs += [(SCAN_UNROLL * i + u, laf_ref, tri_f, keep_f, c - 1, stf_ref, saf_ref, qtf_ref),
                     (n - 1 - SCAN_UNROLL * i - u, lab_ref, tri_b, keep_b, 0, stb_ref, sab_ref, qtb_ref)]
        cums = []
        for ci, la_ref, tri, _, _, _, _, _ in dirs:
            la_hi, la_lo = _split_hi_lo(la_ref[chunk_rows(ci), :])
            cums.append(_dot(tri, la_hi) + _dot(tri, la_lo))
        ops = []
        for (ci, _, _, _, last_row, _, _, qt_ref), cum in zip(dirs, cums):
            sl = chunk_rows(ci)
            tot = cum[last_row:last_row + 1, :]
            mid = cum[c // 2:c // 2 + 1, :]
            kc = k_ref[sl, :]
            qt = q_ref[sl, :] * (GLA_DK ** -0.5) * jnp.exp(cum - mid)
            qt_ref[sl, :] = qt.astype(BF16)
            v_rows = _heads_to_rows(v_ref[sl, :])
            ops.append((tot, tile_heads(qt), tile_heads(kc * jnp.exp(mid - cum)),
                        tile_heads(kc * jnp.exp(tot - cum)), v_rows, jnp.exp(mid)))
        atts = [_dot_nt(q4, k4) for _, q4, k4, _, _, _ in ops]
        incs = []
        for (_, _, _, keep, _, _, _, _), (_, _, _, kd4, v_rows, _), att in zip(dirs, ops, atts):
            att = jnp.where(keep, att, 0.0).astype(BF16)
            incs.append((_dot(att, v_rows.astype(BF16)), _dot(v_rows.T.astype(BF16), kd4)))
        for (ci, _, _, _, _, st_ref, snap_ref, _), (tot, _, _, _, _, e_mid), (o_intra, st_inc) in zip(
                dirs, ops, incs):
            oacc_ref[ci] += o_intra
            st = st_ref[...]
            snap_ref[ci] = (st * e_mid).astype(BF16)
            st_ref[...] = jnp.exp(tot) * st + st_inc
        return carry

    def tile_heads_bf16(x):
        return jnp.concatenate([x] * GLA_HEADS, axis=0) * head_mask

    def out_step(i, carry):
        chunks = [OUT_UNROLL * i + u for u in range(OUT_UNROLL)]
        inter = []
        for ci in chunks:
            sl = chunk_rows(ci)
            q4 = jnp.concatenate([tile_heads_bf16(qtf_ref[sl, :]), tile_heads_bf16(qtb_ref[sl, :])], axis=1)
            st = jnp.concatenate([saf_ref[ci], sab_ref[ci]], axis=1)
            inter.append(_dot_nt(q4, st))
        for ci, o_inter in zip(chunks, inter):
            sl = chunk_rows(ci)
            on = _rms_norm(oacc_ref[ci] + o_inter, norm_g)
            gate = _silu(_heads_to_rows(g_ref[sl, :]))
            out_ref[sl, :] = _rows_to_heads(on * gate, c)
        return carry

    lax.fori_loop(0, n // SCAN_UNROLL, scan_step, 0)
    lax.fori_loop(0, n // OUT_UNROLL, out_step, 0)
    sf_ref[...] = stf_ref[...].T
    sb_ref[...] = stb_ref[...].T


def _gla(gla_in, lora, waf, baf, wab, bab, norm_g, s0f=None, s0b=None):
    b, t, _ = gla_in.shape
    has_init = s0f is not None
    n = t // GLA_CHUNK
    bmap = lambda i: (i, 0, 0)
    full = lambda i: (0, 0)
    in_specs = [pl.BlockSpec((None, t, GLA_QK), lambda i: (i, 0, 0)),
                pl.BlockSpec((None, t, GLA_QK), lambda i: (i, 0, 1)),
                pl.BlockSpec((None, t, GLA_V), lambda i: (i, 0, 1)),
                pl.BlockSpec((None, t, GLA_V), lambda i: (i, 0, 2)),
                pl.BlockSpec((None, t, 2 * GLA_LORA), bmap),
                pl.BlockSpec((2 * GLA_LORA, GLA_QK), full),
                pl.BlockSpec((1, GLA_QK), full),
                pl.BlockSpec((2 * GLA_LORA, GLA_QK), full),
                pl.BlockSpec((1, GLA_QK), full),
                pl.BlockSpec((1, GLA_DV), full)]
    args = [gla_in, gla_in, gla_in, gla_in, lora, waf, baf, wab, bab, norm_g]
    if has_init:
        in_specs += [pl.BlockSpec((None, GLA_QK, GLA_DV), bmap)] * 2
        args += [s0f, s0b]
    return pl.pallas_call(
        functools.partial(_gla_kernel, has_init),
        out_shape=(jax.ShapeDtypeStruct((b, t, GLA_V), F32),
                   jax.ShapeDtypeStruct((b, GLA_QK, GLA_DV), F32),
                   jax.ShapeDtypeStruct((b, GLA_QK, GLA_DV), F32)),
        grid=(b,),
        in_specs=in_specs,
        out_specs=(pl.BlockSpec((None, t, GLA_V), bmap),
                   pl.BlockSpec((None, GLA_QK, GLA_DV), bmap),
                   pl.BlockSpec((None, GLA_QK, GLA_DV), bmap)),
        scratch_shapes=[pltpu.VMEM((t, GLA_QK), F32),
                        pltpu.VMEM((t, GLA_QK), F32),
                        pltpu.VMEM((n, GLA_HEADS * GLA_CHUNK, GLA_DV), F32),
                        pltpu.VMEM((t, GLA_QK), BF16),
                        pltpu.VMEM((t, GLA_QK), BF16),
                        pltpu.VMEM((n, GLA_DV, GLA_QK), BF16),
                        pltpu.VMEM((n, GLA_DV, GLA_QK), BF16),
                        pltpu.VMEM((GLA_DV, GLA_QK), F32),
                        pltpu.VMEM((GLA_DV, GLA_QK), F32)],
        compiler_params=pltpu.CompilerParams(dimension_semantics=("arbitrary",),
                                             vmem_limit_bytes=VMEM_LIMIT),
        name="gla",
    )(*args)


def _dup_groups(x):
    lo = lax.broadcasted_iota(jnp.int32, x.shape, 1) < SWA_HEAD_DIM
    xr = pltpu.roll(x, SWA_HEAD_DIM, axis=1)
    return jnp.where(lo, x, xr), jnp.where(lo, xr, x)


def _pairs_attention(qps, sinks, k_dups, vt_dups, masks):
    nq = qps[0].shape[0]
    lo = lax.broadcasted_iota(jnp.int32, (nq, LANES), 1) < SWA_HEAD_DIM
    even = lax.broadcasted_iota(jnp.int32, (1, 2 * nq), 1) < nq
    scores = []
    for qp, k_dup in zip(qps, k_dups):
        q2 = jnp.concatenate([jnp.where(lo, qp, 0.0), jnp.where(lo, 0.0, qp)], axis=0).astype(BF16)
        scores.append(_dot_nt(k_dup, q2))
    probs = []
    for s, (sink_even, sink_odd), mask in zip(scores, sinks, masks):
        if mask is not None:
            s = jnp.where(mask, s, NEG_INF)
        sink = jnp.where(even, sink_even, sink_odd)
        m = jnp.maximum(jnp.max(s, axis=0, keepdims=True), sink)
        p = jnp.exp(s - m)
        denom = jnp.sum(p, axis=0, keepdims=True) + jnp.exp(sink - m)
        probs.append((p.astype(BF16), 1.0 / denom))
    outs = []
    for (p, rdenom), vt_dup in zip(probs, vt_dups):
        o = _dot(vt_dup, p) * rdenom
        outs.append(jnp.concatenate([o[:SWA_HEAD_DIM, :nq], o[SWA_HEAD_DIM:, nq:]], axis=0).T)
    return outs


CTX_BATCH = 4


def _attn_ctx_kernel(sink_ref, q_ref, k_ref, v_ref, o_ref):
    scale = SWA_HEAD_DIM ** -0.5
    pairs = range(SWA_HEADS // 2)
    items = [(bb, pr) for bb in range(q_ref.shape[0]) for pr in pairs]
    kd = [[x.astype(BF16) for x in _dup_groups(k_ref[bb])] for bb in range(q_ref.shape[0])]
    vt = [[x.T.astype(BF16) for x in _dup_groups(v_ref[bb])] for bb in range(q_ref.shape[0])]
    outs = _pairs_attention([q_ref[bb, :, pr * LANES:(pr + 1) * LANES] * scale for bb, pr in items],
                            [(sink_ref[2 * pr], sink_ref[2 * pr + 1]) for _, pr in items],
                            [kd[bb][pr // 2] for bb, pr in items], [vt[bb][pr // 2] for bb, pr in items],
                            [None] * len(items))
    for (bb, pr), out in zip(items, outs):
        o_ref[bb, :, pr * LANES:(pr + 1) * LANES] = out


def _attn_ctx(sink, q, k, v):
    b, t, _ = q.shape
    bmap = lambda i: (i, 0, 0)
    return pl.pallas_call(
        _attn_ctx_kernel,
        out_shape=jax.ShapeDtypeStruct((b, t, SWA_Q), F32),
        grid=(b // CTX_BATCH,),
        in_specs=[pl.BlockSpec(memory_space=pltpu.SMEM),
                  pl.BlockSpec((CTX_BATCH, t, SWA_Q), bmap),
                  pl.BlockSpec((CTX_BATCH, t, SWA_KV), bmap),
                  pl.BlockSpec((CTX_BATCH, t, SWA_KV), bmap)],
        out_specs=pl.BlockSpec((CTX_BATCH, t, SWA_Q), bmap),
        compiler_params=pltpu.CompilerParams(dimension_semantics=("arbitrary",),
                                             vmem_limit_bytes=VMEM_LIMIT),
        name="attn_ctx",
    )(sink, q, k, v)


LAT_BLOCKS = 2


def _rope(x, cos, sin_lo, sin_hi):
    return x * cos + pltpu.roll(x, LANES - 16, axis=1) * sin_lo + pltpu.roll(x, 16, axis=1) * sin_hi


def _attn_lat_kernel(sink_ref, q_ref, k_ref, v_ref, kc_ref, vc_ref, cos_ref, sl_ref, sh_ref,
                     o_ref, kw_ref, vw_ref):
    t = q_ref.shape[0]
    ab = ATTN_BLOCK
    nb = t // ab
    scale = SWA_HEAD_DIM ** -0.5

    k_rot = _dup_groups(_rope(k_ref[...], cos_ref[...], sl_ref[...], sh_ref[...]))
    v_dup = _dup_groups(v_ref[...])
    zeros = jnp.zeros((ab, LANES), BF16)
    for grp in range(SWA_KV_HEADS):
        kw_ref[grp, 0:ab, :] = zeros
        kw_ref[grp, ab:ab + t, :] = k_rot[grp].astype(BF16)
        kw_ref[grp, ab + t:, :] = zeros
        vw_ref[grp, 0] = zeros
        for blk in range(nb):
            vw_ref[grp, blk + 1] = v_dup[grp][blk * ab:(blk + 1) * ab, :].T.astype(BF16)
        vw_ref[grp, nb + 1] = zeros
    kc = [x.astype(BF16) for x in _dup_groups(kc_ref[...])]
    vct = [x.T.astype(BF16) for x in _dup_groups(vc_ref[...])]
    lc = kc_ref.shape[0]

    key = lax.broadcasted_iota(jnp.int32, (lc + 3 * ab, 2 * ab), 0) - lc
    tq = lax.broadcasted_iota(jnp.int32, (lc + 3 * ab, 2 * ab), 1) & (ab - 1)
    band = (key < 0) | (jnp.abs(tq + ab - key) <= ab)

    def block(it, carry):
        pairs = range(SWA_HEADS // 2)
        qps, sinks, k_dups, vt_dups, masks, places = [], [], [], [], [], []
        for u in range(LAT_BLOCKS):
            nq = it * LAT_BLOCKS + u
            row0 = pl.multiple_of(nq * ab, ab)
            s_abs = key + (nq - 1) * ab
            mask = band & ((key < 0) | ((s_abs >= 0) & (s_abs < t)))
            cos = cos_ref[pl.ds(row0, ab), :]
            s_lo = sl_ref[pl.ds(row0, ab), :]
            s_hi = sh_ref[pl.ds(row0, ab), :]
            k_all = [jnp.concatenate([kc[grp], kw_ref[grp, pl.ds(row0, 3 * ab), :]], axis=0)
                     for grp in range(SWA_KV_HEADS)]
            vt_all = [jnp.concatenate([vct[grp], vw_ref[grp, nq], vw_ref[grp, nq + 1], vw_ref[grp, nq + 2]],
                                      axis=1) for grp in range(SWA_KV_HEADS)]
            for pr in pairs:
                qps.append(_rope(q_ref[pl.ds(row0, ab), pr * LANES:(pr + 1) * LANES], cos, s_lo, s_hi) * scale)
                sinks.append((sink_ref[2 * pr], sink_ref[2 * pr + 1]))
                k_dups.append(k_all[pr // 2])
                vt_dups.append(vt_all[pr // 2])
                masks.append(mask)
                places.append((row0, pr))
        outs = _pairs_attention(qps, sinks, k_dups, vt_dups, masks)
        for (row0, pr), out in zip(places, outs):
            o_ref[pl.ds(row0, ab), pr * LANES:(pr + 1) * LANES] = out
        return carry

    lax.fori_loop(0, nb // LAT_BLOCKS, block, 0)


def _attn_lat(sink, q, k, v, kc, vc, cos, sin_lo, sin_hi):
    b, t, _ = q.shape
    lc = kc.shape[1]
    bmap = lambda i: (i, 0, 0)
    full = lambda i: (0, 0)
    return pl.pallas_call(
        _attn_lat_kernel,
        out_shape=jax.ShapeDtypeStruct((b, t, SWA_Q), F32),
        grid=(b,),
        in_specs=[pl.BlockSpec(memory_space=pltpu.SMEM),
                  pl.BlockSpec((None, t, SWA_Q), bmap),
                  pl.BlockSpec((None, t, SWA_KV), bmap),
                  pl.BlockSpec((None, t, SWA_KV), bmap),
                  pl.BlockSpec((None, lc, SWA_KV), bmap),
                  pl.BlockSpec((None, lc, SWA_KV), bmap),
                  pl.BlockSpec((t, LANES), full),
                  pl.BlockSpec((t, LANES), full),
                  pl.BlockSpec((t, LANES), full)],
        out_specs=pl.BlockSpec((None, t, SWA_Q), bmap),
        scratch_shapes=[pltpu.VMEM((SWA_KV_HEADS, t + 2 * ATTN_BLOCK, LANES), BF16),
                        pltpu.VMEM((SWA_KV_HEADS, t // ATTN_BLOCK + 2, LANES, ATTN_BLOCK), BF16)],
        compiler_params=pltpu.CompilerParams(dimension_semantics=("arbitrary",),
                                             vmem_limit_bytes=VMEM_LIMIT),
        name="attn_lat",
    )(sink, q, k, v, kc, vc, cos, sin_lo, sin_hi)


def _rope_tables(t):
    half = SWA_HEAD_DIM // 2
    quarter = half // 2
    rows = t // GRID_W
    inv_freq = ROPE_BASE ** (-jnp.arange(quarter, dtype=F32) / quarter)
    reps = LANES // quarter
    ang_row = jnp.tile(jnp.arange(rows).astype(F32)[:, None] * inv_freq[None, :], (1, reps))
    ang_col = jnp.tile(jnp.arange(GRID_W).astype(F32)[:, None] * inv_freq[None, :], (1, reps))
    d = jnp.arange(LANES) % SWA_HEAD_DIM
    use_row = (d < half)[None, :]
    lower = ((d % half) < quarter)[None, :]

    def expand(f):
        by_row = jnp.repeat(f(ang_row), GRID_W, axis=0)
        by_col = jnp.tile(f(ang_col), (rows, 1))
        return jnp.where(use_row, by_row, by_col)

    cos = expand(jnp.cos)
    sin = expand(jnp.sin)
    return cos, jnp.where(lower, -sin, 0.0), jnp.where(lower, 0.0, sin)


def _route(sel, scores):
    n = sel.shape[1]
    gsz = N_EXPERTS // N_EXPERT_GROUPS

    def first_max(x, idx, size):
        m = jnp.max(x, axis=0, keepdims=True)
        first = jnp.min(jnp.where(x == m, idx, float(size)), axis=0, keepdims=True)
        return m, idx == first

    i8 = lax.broadcasted_iota(jnp.int32, (gsz, n), 0).astype(F32)
    rows = []
    for g in range(N_EXPERT_GROUPS):
        slab = sel[g * gsz:(g + 1) * gsz, :]
        m1, hit = first_max(slab, i8, gsz)
        m2 = jnp.max(jnp.where(hit, NEG_INF, slab), axis=0, keepdims=True)
        rows.append(m1 + m2)
    gscore = jnp.concatenate(rows, axis=0)
    gsel = jnp.zeros((N_EXPERT_GROUPS, n), F32)
    for _ in range(TOPK_GROUPS):
        _, hit = first_max(gscore, i8, N_EXPERT_GROUPS)
        gsel = jnp.where(hit, 1.0, gsel)
        gscore = jnp.where(hit, NEG_INF, gscore)
    emask = jnp.concatenate(
        [jnp.broadcast_to(gsel[g:g + 1, :], (gsz, n)) for g in range(N_EXPERT_GROUPS)], axis=0)
    cand = jnp.where(emask > 0.5, sel, NEG_INF)
    ie = lax.broadcasted_iota(jnp.int32, (N_EXPERTS, n), 0).astype(F32)
    w = jnp.zeros((N_EXPERTS, n), F32)
    chosen = jnp.zeros((N_EXPERTS, n), F32)
    hits = []
    for _ in range(TOP_K):
        _, hit = first_max(cand, ie, N_EXPERTS)
        hits.append(hit)
        w = jnp.where(hit, scores, w)
        chosen = jnp.where(hit, 1.0, chosen)
        cand = jnp.where(hit, NEG_INF, cand)
    gates = w / jnp.sum(w, axis=0, keepdims=True) * ROUTED_SCALE

    s_idx = lax.broadcasted_iota(jnp.int32, (n, n), 0)
    t_idx = lax.broadcasted_iota(jnp.int32, (n, n), 1)
    tile_shift = MOE_TILE.bit_length() - 1
    before = jnp.where((s_idx < t_idx) & ((s_idx >> tile_shift) == (t_idx >> tile_shift)), 1.0, 0.0)
    rank = _dot(chosen.astype(BF16), before.astype(BF16))
    e_row = lax.broadcasted_iota(jnp.int32, (N_EXPERTS, N_EXPERTS), 0)
    e_col = lax.broadcasted_iota(jnp.int32, (N_EXPERTS, N_EXPERTS), 1)
    below = jnp.where(e_col < e_row, 1.0, 0.0).astype(BF16)
    lane_tile = lax.broadcasted_iota(jnp.int32, (1, n), 1) >> tile_shift

    def as_row(col):
        return jnp.concatenate([col, jnp.zeros((LANES - N_EXPERTS, LANES), F32)], axis=0).T[0:1, :]

    sizes, starts = [], []
    first_row = jnp.zeros((N_EXPERTS, n), F32)
    for ti in range(n // MOE_TILE):
        count = jnp.sum(chosen[:, ti * MOE_TILE:(ti + 1) * MOE_TILE], axis=1, keepdims=True)
        padded = jnp.floor((count + (SORT_ALIGN - 1)) * (1.0 / SORT_ALIGN)) * SORT_ALIGN
        padded = jnp.broadcast_to(padded, (N_EXPERTS, LANES))
        start = _dot(below, padded.astype(BF16))
        first_row = jnp.where(lane_tile == ti, start[:, 0:1], first_row)
        sizes.append(as_row(padded))
        starts.append(as_row(start))
    row = first_row + rank
    pos = jnp.concatenate([jnp.sum(jnp.where(h, row, 0.0), axis=0, keepdims=True) for h in hits], axis=0)
    wts = jnp.concatenate([jnp.sum(jnp.where(h, gates, 0.0), axis=0, keepdims=True) for h in hits], axis=0)
    return pos, wts, sizes, starts


def _outproj_kernel(gla_ref, att_ref, x_ref, wo_ref, g1_ref, sh_ref, sc_ref, ng_ref, rw_ref, rwh_ref,
                    rb_ref, x1_ref, xm_ref, pos_ref, wts_ref, cnt_ref, start_ref):
    bb, tb, d = x_ref.shape
    tm = bb * tb
    y = (_dot(gla_ref[...].reshape(tm, GLA_V).astype(BF16), wo_ref[0:GLA_V, :])
         + _dot(att_ref[...].reshape(tm, SWA_Q).astype(BF16), wo_ref[GLA_V:, :]))
    x1 = x_ref[...].reshape(tm, d) + g1_ref[...] * y
    x1_ref[...] = x1.reshape(bb, tb, d)
    xm = _rms_norm(x1, ng_ref[...]) * (1.0 + sc_ref[...]) + sh_ref[...]
    xm_hi, xm_lo = _split_hi_lo(xm)
    xm_ref[...] = xm_hi.reshape(bb, tb, d)
    lg = _dot(xm_hi, rw_ref[...])
    logits = lg[:, :N_EXPERTS] + lg[:, N_EXPERTS:] + _dot(xm_lo, rwh_ref[...])
    lt = jnp.concatenate([logits, jnp.zeros((tm, LANES - N_EXPERTS), F32)], axis=1).T[:N_EXPERTS, :]
    scores = _sigmoid(lt)
    pos, wts, sizes, starts = _route(scores + rb_ref[...], scores)
    tiles_per_batch = tb // MOE_TILE
    for ti in range(tm // MOE_TILE):
        at = (ti // tiles_per_batch, ti % tiles_per_batch)
        pos_ref[at] = pos[:, ti * MOE_TILE:(ti + 1) * MOE_TILE]
        wts_ref[at] = wts[:, ti * MOE_TILE:(ti + 1) * MOE_TILE]
        cnt_ref[at] = sizes[ti]
        start_ref[at] = starts[ti]


OUTPROJ_TILE = 1024


def _outproj(gla_out, att_out, x, w_out, g1, sh2, sc2, norm_g, rw_cat, rw_hi, rbias):
    b, t, d = x.shape
    nmod = g1.shape[0]
    tb = min(t, OUTPROJ_TILE)
    bb = OUTPROJ_TILE // tb if nmod == 1 else 1
    tpb = tb // MOE_TILE
    mod_map = (lambda i, j: (i, 0, 0)) if nmod > 1 else (lambda i, j: (0, 0, 0))
    row = lambda i, j: (i, j, 0)
    full = lambda i, j: (0, 0)
    tile = lambda i, j: (i, j, 0, 0)
    nt = t // MOE_TILE
    return pl.pallas_call(
        _outproj_kernel,
        out_shape=(jax.ShapeDtypeStruct((b, t, d), F32),
                   jax.ShapeDtypeStruct((b, t, d), BF16),
                   jax.ShapeDtypeStruct((b, nt, TOP_K, MOE_TILE), F32),
                   jax.ShapeDtypeStruct((b, nt, TOP_K, MOE_TILE), F32),
                   jax.ShapeDtypeStruct((b, nt, 1, LANES), F32),
                   jax.ShapeDtypeStruct((b, nt, 1, LANES), F32)),
        grid=(b // bb, t // tb),
        in_specs=[pl.BlockSpec((bb, tb, GLA_V), row),
                  pl.BlockSpec((bb, tb, SWA_Q), row),
                  pl.BlockSpec((bb, tb, d), row),
                  pl.BlockSpec((d, d), full, pipeline_mode=pl.Buffered(1)),
                  pl.BlockSpec((None, 1, d), mod_map),
                  pl.BlockSpec((None, 1, d), mod_map),
                  pl.BlockSpec((None, 1, d), mod_map),
                  pl.BlockSpec((1, d), full),
                  pl.BlockSpec((d, 2 * N_EXPERTS), full),
                  pl.BlockSpec((d, N_EXPERTS), full),
                  pl.BlockSpec((N_EXPERTS, 1), full)],
        out_specs=(pl.BlockSpec((bb, tb, d), row),
                   pl.BlockSpec((bb, tb, d), row),
                   pl.BlockSpec((bb, tpb, TOP_K, MOE_TILE), tile),
                   pl.BlockSpec((bb, tpb, TOP_K, MOE_TILE), tile),
                   pl.BlockSpec((bb, tpb, 1, LANES), tile),
                   pl.BlockSpec((bb, tpb, 1, LANES), tile)),
        compiler_params=pltpu.CompilerParams(dimension_semantics=("arbitrary", "arbitrary"),
                                             vmem_limit_bytes=VMEM_LIMIT),
        name="outproj",
    )(gla_out, att_out, x, w_out, g1, sh2, sc2, norm_g, rw_cat, rw_hi, rbias)


MOE_TILE = 256
SORT_ALIGN = 16
SORT_ROWS = 3072
ROW_TILE = 512
GATHER_SLOTS = 9
FFN_CHAINS = 4
COMBINE_CHUNK = 1024
ALWAYS_ROWS = 2560
COMBINE_TAIL = 512


def _moe_sort_kernel(tiles_a, used_ref, xa_ref, xb_ref, pos_ref, xs_hbm, ybuf, osem):
    i = pl.program_id(0)
    last = pl.num_programs(0) - 1
    slot = lax.rem(i, 2)
    x = jnp.where(i < tiles_a, xa_ref[...], xb_ref[...])
    pos = pos_ref[...]
    tm = x.shape[0]
    used = used_ref[i]

    def out_copies(tile, of_slot, start):
        def one(r0, r1):
            row = pl.multiple_of(tile * SORT_ROWS + r0, tm)
            cp = pltpu.make_async_copy(ybuf.at[of_slot, r0:r1, :], xs_hbm.at[pl.ds(row, r1 - r0), :],
                                       osem.at[of_slot])
            if start:
                cp.start()
            else:
                cp.wait()

        one(0, ALWAYS_ROWS)
        for r0 in range(ALWAYS_ROWS, SORT_ROWS, tm):
            pl.when(r0 < used_ref[tile])(functools.partial(one, r0, r0 + tm))

    @pl.when(i >= 2)
    def _():
        out_copies(i - 2, slot, False)

    rows = lax.broadcasted_iota(jnp.int32, (tm, tm), 0).astype(F32).astype(BF16)
    one_bf = jnp.ones((tm, tm), BF16)

    def fill(blk):
        local = (pos - float(blk * tm)).astype(BF16)
        onehot = jnp.zeros((tm, tm), BF16)
        for k in range(TOP_K):
            onehot = jnp.where(rows == local[k:k + 1, :], one_bf, onehot)
        ybuf[slot, blk * tm:(blk + 1) * tm, :] = _dot(onehot, x).astype(BF16)

    for blk in range(SORT_ROWS // tm):
        if (blk + 1) * tm <= ALWAYS_ROWS:
            fill(blk)
        else:
            pl.when(blk * tm < used)(functools.partial(fill, blk))
    out_copies(i, slot, True)

    @pl.when(i == last)
    def _():
        out_copies(i, slot, False)

        @pl.when(i >= 1)
        def _():
            out_copies(i - 1, 1 - slot, False)


def _moe_sort(xm_a, xm_b, pos, used):
    d = xm_a.shape[1]
    nt, _, tm = pos.shape
    tiles_a = xm_a.shape[0] // tm
    grid_spec = pltpu.PrefetchScalarGridSpec(
        num_scalar_prefetch=1,
        grid=(nt,),
        in_specs=[pl.BlockSpec((tm, d), lambda i, u: (jnp.minimum(i, tiles_a - 1), 0)),
                  pl.BlockSpec((tm, d), lambda i, u: (jnp.maximum(i - tiles_a, 0), 0)),
                  pl.BlockSpec((None, TOP_K, tm), lambda i, u: (i, 0, 0))],
        out_specs=pl.BlockSpec(memory_space=pl.ANY),
        scratch_shapes=[pltpu.VMEM((2, SORT_ROWS, d), BF16),
                        pltpu.SemaphoreType.DMA((2,))])
    return pl.pallas_call(
        functools.partial(_moe_sort_kernel, tiles_a),
        out_shape=jax.ShapeDtypeStruct((nt * SORT_ROWS, d), BF16),
        grid_spec=grid_spec,
        compiler_params=pltpu.CompilerParams(dimension_semantics=("arbitrary",),
                                             vmem_limit_bytes=VMEM_LIMIT),
        name="moe_sort",
    )(used, xm_a, xm_b, pos)


def _moe_row_tiles(n_tokens):
    rows = n_tokens * TOP_K + (n_tokens // MOE_TILE) * N_EXPERTS * (SORT_ALIGN - 1) + N_EXPERTS * (ROW_TILE - 1)
    return -(-rows // ROW_TILE) + GATHER_SLOTS - 1


PLAN_CHUNK = 1280


def _int_dot_r(a, onehot):
    hi = jnp.floor(a * (1.0 / 256.0))
    return _dot(hi.astype(BF16), onehot) * 256.0 + _dot((a - hi * 256.0).astype(BF16), onehot)


def _int_dot_l(onehot, b):
    hi = jnp.floor(b * (1.0 / 256.0))
    return _dot(onehot, hi.astype(BF16)) * 256.0 + _dot(onehot, (b - hi * 256.0).astype(BF16))


def _moe_plan_kernel(cnt_ref, start_ref, src_ref, first_ref, tiles_ref, nu_ref, back_ref):
    nt, ne = cnt_ref.shape
    gpt = SORT_ROWS // SORT_ALIGN
    gpr = ROW_TILE // SORT_ALIGN
    gc = cnt_ref[...] * (1.0 / SORT_ALIGN)
    ls = start_ref[...] * (1.0 / SORT_ALIGN)

    def transpose(x):
        x = jnp.concatenate([x, jnp.zeros((nt, LANES - ne), F32)], axis=1)
        x = jnp.concatenate([x, jnp.zeros((LANES - nt, LANES), F32)], axis=0)
        return x.T[:ne, :nt]

    def tri(n, keep):
        return jnp.where(keep(lax.broadcasted_iota(jnp.int32, (n, n), 0),
                              lax.broadcasted_iota(jnp.int32, (n, n), 1)), 1.0, 0.0).astype(BF16)

    gc_t = transpose(gc)
    ls_t = transpose(ls)
    tot_c = jnp.broadcast_to(jnp.sum(gc_t, axis=1, keepdims=True), (ne, LANES))
    ptot_c = jnp.floor((tot_c + (gpr - 1)) * (1.0 / gpr)) * gpr
    gend_c = _int_dot_l(tri(ne, lambda r, c: c <= r), ptot_c)
    gstart_c = gend_c - ptot_c
    n_used = gend_c[ne - 1:ne, :] * (1.0 / gpr)
    nu_ref[...] = n_used.astype(jnp.int32)
    tot_r = jnp.sum(gc, axis=0, keepdims=True)
    ptot_r = jnp.floor((tot_r + (gpr - 1)) * (1.0 / gpr)) * gpr
    gstart_r = _int_dot_r(jnp.broadcast_to(ptot_r, (8, ne)), tri(ne, lambda r, c: r < c))
    cumex = _dot(tri(nt, lambda r, c: c < r), gc.astype(BF16))
    cumex_t = _dot(gc_t.astype(BF16), tri(nt, lambda r, c: r < c))
    tile_base = lax.broadcasted_iota(jnp.int32, (nt, ne), 0).astype(F32) * gpt + ls
    table = jnp.concatenate([cumex + gc, cumex, tile_base, gstart_r, jnp.broadcast_to(tot_r, (8, ne))], axis=0)

    e_iota = lax.broadcasted_iota(jnp.int32, (ne, PLAN_CHUNK), 0).astype(F32)
    for ch in range(src_ref.shape[1] // PLAN_CHUNK):
        g = (lax.broadcasted_iota(jnp.int32, (1, PLAN_CHUNK), 1) + ch * PLAN_CHUNK).astype(F32)
        eg = jnp.sum(jnp.where(gend_c[:, 0:1] <= g, 1.0, 0.0), axis=0, keepdims=True)
        picked = _int_dot_r(table, jnp.where(e_iota == eg, 1.0, 0.0).astype(BF16))
        cum_g, cumex_g, base_g = picked[0:nt], picked[nt:2 * nt], picked[2 * nt:3 * nt]
        u = g - picked[3 * nt:3 * nt + 1]
        in_tile = (cumex_g <= u) & (u < cum_g)
        src = jnp.sum(jnp.where(in_tile, base_g - cumex_g, 0.0), axis=0, keepdims=True) + u
        src = jnp.where(u < picked[3 * nt + 8:3 * nt + 9], src, 0.0)
        src_ref[:, ch * PLAN_CHUNK:(ch + 1) * PLAN_CHUNK] = src.astype(jnp.int32)

    first_ref[...] = (gstart_c * (1.0 / gpr)).astype(jnp.int32)
    tiles_ref[...] = (ptot_c * (1.0 / gpr)).astype(jnp.int32)

    lg = lax.broadcasted_iota(jnp.int32, (ne, back_ref.shape[1]), 1).astype(F32)
    for t in range(nt):
        first = ls_t[:, t:t + 1]
        inside = (first <= lg) & (lg < first + gc_t[:, t:t + 1])
        shift = gstart_c[:, 0:1] + cumex_t[:, t:t + 1] - first
        val = jnp.sum(jnp.where(inside, shift + lg, 0.0), axis=0, keepdims=True)
        back_ref[t:t + 1, :] = val.astype(jnp.int32)


def _moe_plan(cnt, start):
    nt, ne = cnt.shape
    row_tiles = _moe_row_tiles(nt * MOE_TILE)
    gpt = SORT_ROWS // SORT_ALIGN
    gpr = ROW_TILE // SORT_ALIGN
    n_src = -(-(row_tiles * gpr) // PLAN_CHUNK) * PLAN_CHUNK
    n_back = -(-gpt // LANES) * LANES
    src, first, tiles, nu, back = pl.pallas_call(
        _moe_plan_kernel,
        out_shape=(jax.ShapeDtypeStruct((1, n_src), jnp.int32),
                   jax.ShapeDtypeStruct((ne, LANES), jnp.int32),
                   jax.ShapeDtypeStruct((ne, LANES), jnp.int32),
                   jax.ShapeDtypeStruct((1, LANES), jnp.int32),
                   jax.ShapeDtypeStruct((nt, n_back), jnp.int32)),
        compiler_params=pltpu.CompilerParams(vmem_limit_bytes=VMEM_LIMIT),
        name="moe_plan",
    )(cnt, start)
    return nu[0, :1], first[:, 0], tiles[:, 0], src[0, :row_tiles * gpr], back[:, :gpt]


def _moe_experts_kernel(nu_ref, first_ref, tiles_ref, src_ref, xs_hbm, wg_ref, wu_ref, wd_ref, ys_hbm,
                        xbuf, ybuf, gsem, osem, wgu_s, wd_s):
    e = pl.program_id(0)
    n_used = nu_ref[0]
    gpr = ROW_TILE // SORT_ALIGN
    part = ROW_TILE // FFN_CHAINS

    def gather(tile, to_slot, j0=0, j1=gpr):
        for j in range(j0, j1):
            row = pl.multiple_of(src_ref[tile * gpr + j] * SORT_ALIGN, SORT_ALIGN)
            pltpu.make_async_copy(xs_hbm.at[pl.ds(row, SORT_ALIGN), :],
                                  xbuf.at[to_slot, j * SORT_ALIGN:(j + 1) * SORT_ALIGN, :],
                                  gsem.at[to_slot]).start(priority=j % 2)

    def drain(of_slot):
        for j in range(gpr):
            pltpu.make_async_copy(xs_hbm.at[0:SORT_ALIGN, :],
                                  xbuf.at[of_slot, j * SORT_ALIGN:(j + 1) * SORT_ALIGN, :], gsem.at[of_slot]).wait()

    def out_copy(tile, of_slot):
        row = pl.multiple_of(tile * ROW_TILE, ROW_TILE)
        return pltpu.make_async_copy(ybuf.at[of_slot], ys_hbm.at[pl.ds(row, ROW_TILE), :], osem.at[of_slot])

    @pl.when(e == 0)
    def _():
        for ahead in range(GATHER_SLOTS - 1):
            gather(ahead, ahead)

    wgu_s[:, :EXPERT_FF] = wg_ref[...].astype(BF16)
    wgu_s[:, EXPERT_FF:] = wu_ref[...].astype(BF16)
    wd_s[...] = wd_ref[...].astype(BF16)

    def row_tile(i, carry):
        r = first_ref[e] + i
        slot = lax.rem(r, GATHER_SLOTS)
        oslot = lax.rem(r, 2)
        next_slot = lax.rem(r + GATHER_SLOTS - 1, GATHER_SLOTS)
        drain(slot)

        @pl.when(r >= 2)
        def _():
            out_copy(r - 2, oslot).wait()

        abs_ = []
        for c in range(FFN_CHAINS):
            abs_.append(_dot(xbuf[slot, c * part:(c + 1) * part, :], wgu_s[...]))
            gather(r + GATHER_SLOTS - 1, next_slot, c * gpr // FFN_CHAINS, (c + 1) * gpr // FFN_CHAINS)
        hs = [(_silu(ab[:, :EXPERT_FF]) * ab[:, EXPERT_FF:]).astype(BF16) for ab in abs_]
        ys = [_dot(h, wd_s[...]).astype(BF16) for h in hs]
        for c in range(FFN_CHAINS):
            ybuf[oslot, c * part:(c + 1) * part, :] = ys[c]
        out_copy(r, oslot).start()
        return carry

    lax.fori_loop(0, tiles_ref[e], row_tile, 0)

    @pl.when(e == pl.num_programs(0) - 1)
    def _():
        for ahead in range(GATHER_SLOTS - 1):
            drain(lax.rem(n_used + ahead, GATHER_SLOTS))
        out_copy(n_used - 1, lax.rem(n_used - 1, 2)).wait()

        @pl.when(n_used >= 2)
        def _():
            out_copy(n_used - 2, lax.rem(n_used, 2)).wait()


def _moe_experts(n_used, first, tiles, src, xs, wg, wu, wd, row_tiles):
    d = xs.shape[-1]
    ne = wg.shape[0]
    w_map = lambda e, nu, fi, ti, sr: (e, 0, 0)
    grid_spec = pltpu.PrefetchScalarGridSpec(
        num_scalar_prefetch=4,
        grid=(ne,),
        in_specs=[pl.BlockSpec(memory_space=pl.ANY),
                  pl.BlockSpec((None, d, EXPERT_FF), w_map),
                  pl.BlockSpec((None, d, EXPERT_FF), w_map),
                  pl.BlockSpec((None, EXPERT_FF, d), w_map)],
        out_specs=pl.BlockSpec(memory_space=pl.ANY),
        scratch_shapes=[pltpu.VMEM((GATHER_SLOTS, ROW_TILE, d), BF16),
                        pltpu.VMEM((2, ROW_TILE, d), BF16),
                        pltpu.SemaphoreType.DMA((GATHER_SLOTS,)),
                        pltpu.SemaphoreType.DMA((2,)),
                        pltpu.VMEM((d, 2 * EXPERT_FF), BF16),
                        pltpu.VMEM((EXPERT_FF, d), BF16)])
    return pl.pallas_call(
        _moe_experts_kernel,
        out_shape=jax.ShapeDtypeStruct((row_tiles * ROW_TILE, d), BF16),
        grid_spec=grid_spec,
        compiler_params=pltpu.CompilerParams(dimension_semantics=("arbitrary",),
                                             vmem_limit_bytes=VMEM_LIMIT),
        name="moe_experts",
    )(n_used, first, tiles, src, xs, wg, wu, wd)


def _moe_combine_kernel(back_ref, used_ref, ys_hbm, pos_ref, wts_ref, xm_ref, x1_ref, g2_ref, fg_ref,
                        swg_ref, swu_ref, swd_ref, o_ref, buf, sem, acc_ref):
    i = pl.program_id(0)
    gpt = SORT_ROWS // SORT_ALIGN
    slot = lax.rem(i, 2)
    always = ALWAYS_ROWS
    tail = range(always, SORT_ROWS, COMBINE_TAIL)

    def copies(tile, of_slot, g0, g1, start):
        for g in range(g0, g1):
            row = pl.multiple_of(back_ref[tile * gpt + g] * SORT_ALIGN, SORT_ALIGN) if start else 0
            cp = pltpu.make_async_copy(ys_hbm.at[pl.ds(row, SORT_ALIGN), :],
                                       buf.at[of_slot, g * SORT_ALIGN:(g + 1) * SORT_ALIGN, :], sem.at[of_slot])
            if start:
                cp.start(priority=g % 2)
            else:
                cp.wait()

    def transfer_tail(tile, of_slot, start):
        for c0 in tail:
            pl.when(c0 < used_ref[tile])(functools.partial(
                copies, tile, of_slot, c0 // SORT_ALIGN, (c0 + COMBINE_TAIL) // SORT_ALIGN, start))

    def transfer(tile, of_slot, start):
        copies(tile, of_slot, 0, always // SORT_ALIGN, start)
        transfer_tail(tile, of_slot, start)

    @pl.when(i == 0)
    def _():
        transfer(0, 0, True)

    nxt = jnp.minimum(i + 1, pl.num_programs(0) - 1)
    nxt_slot = 1 - slot
    transfer_tail(nxt, nxt_slot, True)
    always_groups = always // SORT_ALIGN
    spread = [always_groups * part // 4 for part in range(5)]

    x = xm_ref[...]
    tm = x.shape[0]
    pad = jnp.zeros((LANES - TOP_K, tm), F32)
    pos_t = jnp.concatenate([pos_ref[...], pad], axis=0).T
    wts_t = jnp.concatenate([wts_ref[...], pad], axis=0).T
    copies(nxt, nxt_slot, spread[0], spread[1], True)
    blk_b, loc_b, wts_b = [], [], []
    for k in range(TOP_K):
        p = jnp.broadcast_to(pos_t[:, k:k + 1], (tm, LANES))
        blk = jnp.floor(p * (1.0 / tm))
        two = lambda v: jnp.concatenate([v.astype(BF16)] * (tm // LANES), axis=1)
        blk_b.append(two(blk))
        loc_b.append(two(p - blk * tm))
        wts_b.append(two(jnp.broadcast_to(wts_t[:, k:k + 1], (tm, LANES))))
    copies(nxt, nxt_slot, spread[1], spread[2], True)
    shared = _dot((_silu(_dot(x, swg_ref[...])) * _dot(x, swu_ref[...])).astype(BF16), swd_ref[...])
    copies(nxt, nxt_slot, spread[2], spread[3], True)
    transfer(i, slot, False)
    lane = lax.broadcasted_iota(jnp.int32, (tm, tm), 1).astype(F32).astype(BF16)
    zero = jnp.zeros((tm, tm), BF16)
    nowhere = jnp.full((tm, tm), -1.0, BF16)

    def apply(c0, width):
        blocks = []
        for b0 in range(c0, c0 + width, tm):
            comb = zero
            for k in range(TOP_K):
                loc = jnp.where(blk_b[k] == float(b0 // tm), loc_b[k], nowhere)
                comb = jnp.where(lane == loc, wts_b[k], comb)
            blocks.append(comb)
        return _dot(jnp.concatenate(blocks, axis=1), buf[slot, c0:c0 + width, :])

    routed = shared
    for c0 in range(0, always, COMBINE_CHUNK):
        routed = routed + apply(c0, min(COMBINE_CHUNK, always - c0))
        if c0 == 0:
            copies(nxt, nxt_slot, spread[3], spread[4], True)
    acc_ref[...] = routed
    for c0 in tail:
        @pl.when(c0 < used_ref[i])
        def _(c0=c0):
            acc_ref[...] += apply(c0, COMBINE_TAIL)
    y = x1_ref[...] + g2_ref[...] * acc_ref[...]
    o_ref[...] = _rms_norm(y, fg_ref[...])

    @pl.when(i == pl.num_programs(0) - 1)
    def _():
        transfer(i, nxt_slot, False)


def _moe_combine(back, used, ys, pos, wts, xm, x1, g2, final_g, swg, swu, swd, *, tiles_per_mod):
    n, d = xm.shape
    tm = pos.shape[-1]
    nt = n // tm
    gpt = SORT_ROWS // SORT_ALIGN
    row = lambda i, bk, us: (i, 0)
    full = lambda i, bk, us: (0, 0)
    tile = lambda i, bk, us: (i, 0, 0)
    mod_map = lambda i, bk, us: (i // tiles_per_mod, 0, 0)
    grid_spec = pltpu.PrefetchScalarGridSpec(
        num_scalar_prefetch=2,
        grid=(nt,),
        in_specs=[pl.BlockSpec(memory_space=pl.ANY),
                  pl.BlockSpec((None, TOP_K, tm), tile),
                  pl.BlockSpec((None, TOP_K, tm), tile),
                  pl.BlockSpec((tm, d), row),
                  pl.BlockSpec((tm, d), row),
                  pl.BlockSpec((None, 1, d), mod_map),
                  pl.BlockSpec((1, d), full),
                  pl.BlockSpec((d, SHARED_FF), full),
                  pl.BlockSpec((d, SHARED_FF), full),
                  pl.BlockSpec((SHARED_FF, d), full)],
        out_specs=pl.BlockSpec((tm, d), row),
        scratch_shapes=[pltpu.VMEM((2, SORT_ROWS, d), BF16),
                        pltpu.SemaphoreType.DMA((2,)),
                        pltpu.VMEM((tm, d), F32)])
    return pl.pallas_call(
        _moe_combine_kernel,
        out_shape=jax.ShapeDtypeStruct((n, d), F32),
        grid_spec=grid_spec,
        compiler_params=pltpu.CompilerParams(dimension_semantics=("arbitrary",),
                                             vmem_limit_bytes=VMEM_LIMIT),
        name="moe_combine",
    )(back, used, ys, pos.reshape(nt, TOP_K, tm), wts.reshape(nt, TOP_K, tm), xm, x1, g2, final_g, swg, swu, swd)


def _mix(x, mods, p, attn_fn, s0=None):
    sh1, sc1, g1, sh2, sc2, _ = mods
    gla_in, lora, q_s, k_s, v_s = _inproj(x, p["norm_attn_g"], sh1, sc1, p["w_gla"], p["w_lora"], p["w_swa"])
    if s0 is None:
        gla_out, s_f, s_b = _gla(gla_in, lora, p["waf"], p["baf"], p["wab"], p["bab"], p["gla_norm_g"])
    else:
        gla_out, s_f, s_b = _gla(gla_in, lora, p["waf"], p["baf"], p["wab"], p["bab"], p["gla_norm_g"],
                                 s0[0], s0[1])
    att_out = attn_fn(q_s, k_s, v_s)
    routed = _outproj(gla_out, att_out, x, p["w_out"], g1, sh2, sc2, p["norm_ffn_g"],
                      p["rw_cat"], p["rw_hi"], p["rbias"])
    return routed, k_s, v_s, s_f, s_b


def _moe(streams, p):
    d = D_MODEL
    (ra, _), (rb, _) = streams
    n_tiles = [r[1].shape[0] * r[1].shape[1] // MOE_TILE for r, _ in streams]
    pos_all = jnp.concatenate([r[2].reshape(-1, TOP_K, MOE_TILE) for r, _ in streams], axis=0)
    cnt_all = jnp.concatenate([r[4].reshape(-1, LANES) for r, _ in streams], axis=0)[:, :N_EXPERTS]
    start_all = jnp.concatenate([r[5].reshape(-1, LANES) for r, _ in streams], axis=0)[:, :N_EXPERTS]
    used = (start_all[:, -1] + cnt_all[:, -1]).astype(jnp.int32)
    xs = _moe_sort(ra[1].reshape(-1, d), rb[1].reshape(-1, d), pos_all, used)
    n_used, first, tiles, src, back = _moe_plan(cnt_all, start_all)
    ys = _moe_experts(n_used, first, tiles, src, xs, p["wg"], p["wu"], p["wd"],
                      _moe_row_tiles(cnt_all.shape[0] * MOE_TILE))
    outs = []
    tile0 = 0
    for ((x1, xm, pos, wts, cnt, start), g2), nt in zip(streams, n_tiles):
        b, t, _ = x1.shape
        tiles_per_mod = (t // MOE_TILE) if g2.shape[0] > 1 else nt
        y = _moe_combine(back[tile0:tile0 + nt].reshape(-1), used[tile0:tile0 + nt], ys, pos, wts,
                         xm.reshape(-1, d), x1.reshape(-1, d), g2, p["final_norm_g"],
                         p["swg"], p["swu"], p["swd"], tiles_per_mod=tiles_per_mod)
        outs.append(y.reshape(b, t, d))
        tile0 += nt
    return outs


def kernel(x_prompt, x_sample, c, cache_swa_k, cache_swa_v, state_gla_fwd, state_gla_bwd, c_ctx, w_ada, b_ada, norm_attn_g, norm_ffn_g, w_in, gla_wa_f, gla_ba_f, gla_wa_b, gla_ba_b, gla_norm_g, swa_sink, w_out, router_w, router_bias, exp_w_gate, exp_w_up, exp_w_down, sh_w_gate, sh_w_up, sh_w_down, final_norm_g):
    l = 0
    d = D_MODEL
    nb_ctx, t_ctx, _ = x_prompt.shape
    nb_lat, t_lat, _ = x_sample.shape

    pad = jnp.zeros((8 - 1 - nb_lat, d), F32)
    cond8 = jnp.concatenate([c_ctx[None, :], c, pad], axis=0)
    mod = _adaln(cond8, w_ada[l], b_ada[l][None, :])
    mods_ctx = [mod[0:1, i * d:(i + 1) * d][:, None, :] for i in range(6)]
    mods_lat = [mod[1:1 + nb_lat, i * d:(i + 1) * d][:, None, :] for i in range(6)]

    zeros_lora = jnp.zeros((GLA_LORA, GLA_QK), F32)
    rw = router_w[l]
    rw_hi = rw.astype(BF16)
    rw_lo = (rw - rw_hi.astype(F32)).astype(BF16)
    n_gla = 2 * GLA_QK + 2 * GLA_V
    p = {
        "norm_attn_g": norm_attn_g[l][None, :],
        "norm_ffn_g": norm_ffn_g[l][None, :],
        "final_norm_g": final_norm_g[None, :],
        "w_gla": w_in[l][:, :n_gla].astype(BF16),
        "w_lora": w_in[l][:, n_gla:n_gla + 2 * GLA_LORA].astype(BF16),
        "w_swa": w_in[l][:, n_gla + 2 * GLA_LORA:].astype(BF16),
        "waf": jnp.concatenate([gla_wa_f[l], zeros_lora], axis=0).astype(BF16),
        "wab": jnp.concatenate([zeros_lora, gla_wa_b[l]], axis=0).astype(BF16),
        "baf": gla_ba_f[l][None, :],
        "bab": gla_ba_b[l][None, :],
        "gla_norm_g": gla_norm_g[l][None, :],
        "w_out": w_out[l].astype(BF16),
        "rw_cat": jnp.concatenate([rw_hi, rw_lo], axis=1),
        "rw_hi": rw_hi,
        "rbias": router_bias[l][:, None],
        "wg": exp_w_gate[l], "wu": exp_w_up[l], "wd": exp_w_down[l],
        "swg": sh_w_gate[l].astype(BF16), "swu": sh_w_up[l].astype(BF16),
        "swd": sh_w_down[l].astype(BF16),
    }
    sink = swa_sink[l]

    routed_ctx, k_c, v_c, s_f, s_b = _mix(x_prompt, mods_ctx, p, functools.partial(_attn_ctx, sink))

    cos, sin_lo, sin_hi = _rope_tables(t_lat)
    kc = cache_swa_k[:, l].reshape(nb_lat, -1, SWA_KV)
    vc = cache_swa_v[:, l].reshape(nb_lat, -1, SWA_KV)
    lat_attn = lambda q, k, v: _attn_lat(sink, q, k, v, kc, vc, cos, sin_lo, sin_hi)
    s0 = (state_gla_fwd[:, l].reshape(nb_lat, GLA_QK, GLA_DV),
          state_gla_bwd[:, l].reshape(nb_lat, GLA_QK, GLA_DV))
    routed_lat, _, _, _, _ = _mix(x_sample, mods_lat, p, lat_attn, s0)
    y_prompt, y_sample = _moe([(routed_ctx, mods_ctx[5]), (routed_lat, mods_lat[5])], p)

    new_k = k_c.reshape(nb_ctx, 1, t_ctx, SWA_KV_HEADS, SWA_HEAD_DIM)
    new_v = v_c.reshape(nb_ctx, 1, t_ctx, SWA_KV_HEADS, SWA_HEAD_DIM)
    new_sf = s_f.reshape(nb_ctx, 1, GLA_HEADS, GLA_DK, GLA_DV)
    new_sb = s_b.reshape(nb_ctx, 1, GLA_HEADS, GLA_DK, GLA_DV)
    return (y_prompt, y_sample, new_k, new_v, new_sf, new_sb)
```

```python
import functools

import jax
import jax.numpy as jnp
from jax import lax
from jax.experimental import pallas as pl
from jax.experimental.pallas import tpu as pltpu

F32 = jnp.float32
BF16 = jnp.bfloat16

D_MODEL = 1024
GLA_HEADS = 4
GLA_DK = 64
GLA_DV = 128
GLA_LORA = 16
GLA_GATE_NORM = 16.0
GLA_CHUNK = 64
GLA_QK = GLA_HEADS * GLA_DK
GLA_V = GLA_HEADS * GLA_DV
SWA_HEAD_DIM = 64
SWA_HEADS = 8
SWA_KV_HEADS = 2
SWA_Q = SWA_HEADS * SWA_HEAD_DIM
SWA_KV = SWA_KV_HEADS * SWA_HEAD_DIM
ATTN_BLOCK = 128
GRID_W = 64
ROPE_BASE = 10000.0
N_EXPERTS = 64
TOP_K = 8
N_EXPERT_GROUPS = 8
TOPK_GROUPS = 4
EXPERT_FF = 128
SHARED_FF = 256
ROUTED_SCALE = 2.5
EPS = 1e-6

LANES = 128
VMEM_LIMIT = 56 * 1024 * 1024

NEG_INF = float("-inf")


def _dot(a, b):
    return jnp.dot(a, b, preferred_element_type=F32)


def _dot_nt(a, b):
    return lax.dot_general(a, b, (((1,), (1,)), ((), ())), preferred_element_type=F32)


def _split_hi_lo(x):
    hi = x.astype(BF16)
    lo = (x - hi.astype(F32)).astype(BF16)
    return hi, lo


def _sigmoid(x):
    return 1.0 / (1.0 + jnp.exp(-x))


def _silu(x):
    return x * _sigmoid(x)


def _rms_norm(x, g):
    ms = jnp.mean(x * x, axis=-1, keepdims=True)
    return x * lax.rsqrt(ms + EPS) * g


def _adaln_kernel(c_ref, w_ref, b_ref, o_ref):
    a_hi, a_lo = _split_hi_lo(_silu(c_ref[...]))
    w_hi, w_lo = _split_hi_lo(w_ref[...])
    o_ref[...] = _dot(a_hi, w_hi) + _dot(a_lo, w_hi) + _dot(a_hi, w_lo) + b_ref[...]


def _adaln(cond8, w_ada, b_ada):
    n = w_ada.shape[1]
    tn = 1536
    return pl.pallas_call(
        _adaln_kernel,
        out_shape=jax.ShapeDtypeStruct((8, n), F32),
        grid=(n // tn,),
        in_specs=[pl.BlockSpec((8, D_MODEL), lambda j: (0, 0)),
                  pl.BlockSpec((D_MODEL, tn), lambda j: (0, j)),
                  pl.BlockSpec((1, tn), lambda j: (0, j))],
        out_specs=pl.BlockSpec((8, tn), lambda j: (0, j)),
        compiler_params=pltpu.CompilerParams(dimension_semantics=("arbitrary",),
                                             vmem_limit_bytes=VMEM_LIMIT),
        name="adaln",
    )(cond8, w_ada, b_ada)


def _inproj_kernel(x_ref, g_ref, sh_ref, sc_ref, wg_ref, wl_ref, ws_ref,
                   gla_ref, lora_ref, q_ref, k_ref, v_ref):
    bb, tb, d = x_ref.shape
    x = x_ref[...].reshape(bb * tb, d)
    h = _rms_norm(x, g_ref[...]) * (1.0 + sc_ref[...]) + sh_ref[...]
    hb = h.astype(BF16)
    gla_ref[...] = _dot(hb, wg_ref[...]).reshape(gla_ref.shape)
    lora_ref[...] = _dot(hb, wl_ref[...]).reshape(lora_ref.shape)
    s = _dot(hb, ws_ref[...])
    q_ref[...] = s[:, :SWA_Q].reshape(q_ref.shape)
    k_ref[...] = s[:, SWA_Q:SWA_Q + SWA_KV].reshape(k_ref.shape)
    v_ref[...] = s[:, SWA_Q + SWA_KV:].reshape(v_ref.shape)


INPROJ_TILE = 1024


def _inproj(x, g, sh, sc, w_gla, w_lora, w_swa):
    b, t, d = x.shape
    nmod = sh.shape[0]
    tb = min(t, INPROJ_TILE)
    bb = INPROJ_TILE // tb if nmod == 1 else 1
    mod_map = (lambda i, j: (i, 0, 0)) if nmod > 1 else (lambda i, j: (0, 0, 0))
    row = lambda i, j: (i, j, 0)
    full = lambda i, j: (0, 0)
    n_gla = w_gla.shape[1]
    n_lora = w_lora.shape[1]
    return pl.pallas_call(
        _inproj_kernel,
        out_shape=(jax.ShapeDtypeStruct((b, t, n_gla), F32),
                   jax.ShapeDtypeStruct((b, t, n_lora), F32),
                   jax.ShapeDtypeStruct((b, t, SWA_Q), F32),
                   jax.ShapeDtypeStruct((b, t, SWA_KV), F32),
                   jax.ShapeDtypeStruct((b, t, SWA_KV), F32)),
        grid=(b // bb, t // tb),
        in_specs=[pl.BlockSpec((bb, tb, d), row),
                  pl.BlockSpec((1, d), full),
                  pl.BlockSpec((None, 1, d), mod_map),
                  pl.BlockSpec((None, 1, d), mod_map),
                  pl.BlockSpec((d, n_gla), full, pipeline_mode=pl.Buffered(1)),
                  pl.BlockSpec((d, n_lora), full, pipeline_mode=pl.Buffered(1)),
                  pl.BlockSpec((d, w_swa.shape[1]), full, pipeline_mode=pl.Buffered(1))],
        out_specs=(pl.BlockSpec((bb, tb, n_gla), row),
                   pl.BlockSpec((bb, tb, n_lora), row),
                   pl.BlockSpec((bb, tb, SWA_Q), row),
                   pl.BlockSpec((bb, tb, SWA_KV), row),
                   pl.BlockSpec((bb, tb, SWA_KV), row)),
        compiler_params=pltpu.CompilerParams(dimension_semantics=("arbitrary", "arbitrary"),
                                             vmem_limit_bytes=VMEM_LIMIT),
        name="inproj",
    )(x, g, sh, sc, w_gla, w_lora, w_swa)


SCAN_UNROLL = 4
OUT_UNROLL = 4


def _log_sigmoid(x):
    return jnp.minimum(x, 0.0) - jnp.log(1.0 + jnp.exp(-jnp.abs(x)))


def _heads_to_rows(x):
    return jnp.concatenate([x[:, h * LANES:(h + 1) * LANES] for h in range(GLA_HEADS)], axis=0)


def _rows_to_heads(x, c):
    return jnp.concatenate([x[h * c:(h + 1) * c, :] for h in range(GLA_HEADS)], axis=1)


def _gla_kernel(has_init, q_ref, k_ref, v_ref, g_ref, lora_ref, waf_ref, baf_ref, wab_ref, bab_ref,
                ng_ref, *rest):
    if has_init:
        s0f_ref, s0b_ref, *rest = rest
    (out_ref, sf_ref, sb_ref, laf_ref, lab_ref, oacc_ref, qtf_ref, qtb_ref, saf_ref, sab_ref,
     stf_ref, stb_ref) = rest
    t = q_ref.shape[0]
    c = GLA_CHUNK
    n = t // c
    hc = GLA_HEADS * c

    lora = lora_ref[...].astype(BF16)
    laf_ref[...] = _log_sigmoid(_dot(lora, waf_ref[...]) + baf_ref[...]) * (1.0 / GLA_GATE_NORM)
    lab_ref[...] = _log_sigmoid(_dot(lora, wab_ref[...]) + bab_ref[...]) * (1.0 / GLA_GATE_NORM)

    if has_init:
        stf_ref[...] = s0f_ref[...].T
        stb_ref[...] = s0b_ref[...].T
    else:
        stf_ref[...] = jnp.zeros_like(stf_ref)
        stb_ref[...] = jnp.zeros_like(stb_ref)
    oacc_ref[...] = jnp.zeros_like(oacc_ref)

    r64 = lax.broadcasted_iota(jnp.int32, (c, c), 0)
    c64 = lax.broadcasted_iota(jnp.int32, (c, c), 1)
    tri_f = jnp.where(c64 <= r64, 1.0, 0.0).astype(BF16)
    tri_b = jnp.where(c64 >= r64, 1.0, 0.0).astype(BF16)
    rr = lax.broadcasted_iota(jnp.int32, (hc, hc), 0)
    cc = lax.broadcasted_iota(jnp.int32, (hc, hc), 1)
    same_head = (rr >> 6) == (cc >> 6)
    keep_f = same_head & ((rr & (c - 1)) >= (cc & (c - 1)))
    keep_b = same_head & ((rr & (c - 1)) <= (cc & (c - 1)))
    head_mask = jnp.where(same_head, 1.0, 0.0).astype(BF16)
    norm_g = ng_ref[...]

    def chunk_rows(ci):
        return pl.ds(pl.multiple_of(ci * c, c), c)

    def tile_heads(x):
        x4 = jnp.concatenate([x] * GLA_HEADS, axis=0)
        return jnp.where(same_head, x4, 0.0).astype(BF16)

    def scan_step(i, carry):
        dirs = []
        for u in range(SCAN_UNROLL):
            dirs += [(SCAN_UNROLL * i + u, laf_ref, tri_f, keep_f, c - 1, stf_ref, saf_ref, qtf_ref),
                     (n - 1 - SCAN_UNROLL * i - u, lab_ref, tri_b, keep_b, 0, stb_ref, sab_ref, qtb_ref)]
        cums = []
        for ci, la_ref, tri, _, _, _, _, _ in dirs:
            la_hi, la_lo = _split_hi_lo(la_ref[chunk_rows(ci), :])
            cums.append(_dot(tri, la_hi) + _dot(tri, la_lo))
        ops = []
        for (ci, _, _, _, last_row, _, _, qt_ref), cum in zip(dirs, cums):
            sl = chunk_rows(ci)
            tot = cum[last_row:last_row + 1, :]
            mid = cum[c // 2:c // 2 + 1, :]
            kc = k_ref[sl, :]
            qt = q_ref[sl, :] * (GLA_DK ** -0.5) * jnp.exp(cum - mid)
            qt_ref[sl, :] = qt.astype(BF16)
            v_rows = _heads_to_rows(v_ref[sl, :])
            ops.append((tot, tile_heads(qt), tile_heads(kc * jnp.exp(mid - cum)),
                        tile_heads(kc * jnp.exp(tot - cum)), v_rows, jnp.exp(mid)))
        atts = [_dot_nt(q4, k4) for _, q4, k4, _, _, _ in ops]
        incs = []
        for (_, _, _, keep, _, _, _, _), (_, _, _, kd4, v_rows, _), att in zip(dirs, ops, atts):
            att = jnp.where(keep, att, 0.0).astype(BF16)
            incs.append((_dot(att, v_rows.astype(BF16)), _dot(v_rows.T.astype(BF16), kd4)))
        for (ci, _, _, _, _, st_ref, snap_ref, _), (tot, _, _, _, _, e_mid), (o_intra, st_inc) in zip(
                dirs, ops, incs):
            oacc_ref[ci] += o_intra
            st = st_ref[...]
            snap_ref[ci] = (st * e_mid).astype(BF16)
            st_ref[...] = jnp.exp(tot) * st + st_inc
        return carry

    def tile_heads_bf16(x):
        return jnp.concatenate([x] * GLA_HEADS, axis=0) * head_mask

    def out_step(i, carry):
        chunks = [OUT_UNROLL * i + u for u in range(OUT_UNROLL)]
        inter = []
        for ci in chunks:
            sl = chunk_rows(ci)
            q4 = jnp.concatenate([tile_heads_bf16(qtf_ref[sl, :]), tile_heads_bf16(qtb_ref[sl, :])], axis=1)
            st = jnp.concatenate([saf_ref[ci], sab_ref[ci]], axis=1)
            inter.append(_dot_nt(q4, st))
        for ci, o_inter in zip(chunks, inter):
            sl = chunk_rows(ci)
            on = _rms_norm(oacc_ref[ci] + o_inter, norm_g)
            gate = _silu(_heads_to_rows(g_ref[sl, :]))
            out_ref[sl, :] = _rows_to_heads(on * gate, c)
        return carry

    lax.fori_loop(0, n // SCAN_UNROLL, scan_step, 0)
    lax.fori_loop(0, n // OUT_UNROLL, out_step, 0)
    sf_ref[...] = stf_ref[...].T
    sb_ref[...] = stb_ref[...].T


def _gla(gla_in, lora, waf, baf, wab, bab, norm_g, s0f=None, s0b=None):
    b, t, _ = gla_in.shape
    has_init = s0f is not None
    n = t // GLA_CHUNK
    bmap = lambda i: (i, 0, 0)
    full = lambda i: (0, 0)
    in_specs = [pl.BlockSpec((None, t, GLA_QK), lambda i: (i, 0, 0)),
                pl.BlockSpec((None, t, GLA_QK), lambda i: (i, 0, 1)),
                pl.BlockSpec((None, t, GLA_V), lambda i: (i, 0, 1)),
                pl.BlockSpec((None, t, GLA_V), lambda i: (i, 0, 2)),
                pl.BlockSpec((None, t, 2 * GLA_LORA), bmap),
                pl.BlockSpec((2 * GLA_LORA, GLA_QK), full),
                pl.BlockSpec((1, GLA_QK), full),
                pl.BlockSpec((2 * GLA_LORA, GLA_QK), full),
                pl.BlockSpec((1, GLA_QK), full),
                pl.BlockSpec((1, GLA_DV), full)]
    args = [gla_in, gla_in, gla_in, gla_in, lora, waf, baf, wab, bab, norm_g]
    if has_init:
        in_specs += [pl.BlockSpec((None, GLA_QK, GLA_DV), bmap)] * 2
        args += [s0f, s0b]
    return pl.pallas_call(
        functools.partial(_gla_kernel, has_init),
        out_shape=(jax.ShapeDtypeStruct((b, t, GLA_V), F32),
                   jax.ShapeDtypeStruct((b, GLA_QK, GLA_DV), F32),
                   jax.ShapeDtypeStruct((b, GLA_QK, GLA_DV), F32)),
        grid=(b,),
        in_specs=in_specs,
        out_specs=(pl.BlockSpec((None, t, GLA_V), bmap),
                   pl.BlockSpec((None, GLA_QK, GLA_DV), bmap),
                   pl.BlockSpec((None, GLA_QK, GLA_DV), bmap)),
        scratch_shapes=[pltpu.VMEM((t, GLA_QK), F32),
                        pltpu.VMEM((t, GLA_QK), F32),
                        pltpu.VMEM((n, GLA_HEADS * GLA_CHUNK, GLA_DV), F32),
                        pltpu.VMEM((t, GLA_QK), BF16),
                        pltpu.VMEM((t, GLA_QK), BF16),
                        pltpu.VMEM((n, GLA_DV, GLA_QK), BF16),
                        pltpu.VMEM((n, GLA_DV, GLA_QK), BF16),
                        pltpu.VMEM((GLA_DV, GLA_QK), F32),
                        pltpu.VMEM((GLA_DV, GLA_QK), F32)],
        compiler_params=pltpu.CompilerParams(dimension_semantics=("arbitrary",),
                                             vmem_limit_bytes=VMEM_LIMIT),
        name="gla",
    )(*args)


def _dup_groups(x):
    lo = lax.broadcasted_iota(jnp.int32, x.shape, 1) < SWA_HEAD_DIM
    xr = pltpu.roll(x, SWA_HEAD_DIM, axis=1)
    return jnp.where(lo, x, xr), jnp.where(lo, xr, x)


def _pairs_attention(qps, sinks, k_dups, vt_dups, masks):
    nq = qps[0].shape[0]
    lo = lax.broadcasted_iota(jnp.int32, (nq, LANES), 1) < SWA_HEAD_DIM
    even = lax.broadcasted_iota(jnp.int32, (1, 2 * nq), 1) < nq
    scores = []
    for qp, k_dup in zip(qps, k_dups):
        q2 = jnp.concatenate([jnp.where(lo, qp, 0.0), jnp.where(lo, 0.0, qp)], axis=0).astype(BF16)
        scores.append(_dot_nt(k_dup, q2))
    probs = []
    for s, (sink_even, sink_odd), mask in zip(scores, sinks, masks):
        if mask is not None:
            s = jnp.where(mask, s, NEG_INF)
        sink = jnp.where(even, sink_even, sink_odd)
        m = jnp.maximum(jnp.max(s, axis=0, keepdims=True), sink)
        p = jnp.exp(s - m)
        denom = jnp.sum(p, axis=0, keepdims=True) + jnp.exp(sink - m)
        probs.append((p.astype(BF16), 1.0 / denom))
    outs = []
    for (p, rdenom), vt_dup in zip(probs, vt_dups):
        o = _dot(vt_dup, p) * rdenom
        outs.append(jnp.concatenate([o[:SWA_HEAD_DIM, :nq], o[SWA_HEAD_DIM:, nq:]], axis=0).T)
    return outs


CTX_BATCH = 4


def _attn_ctx_kernel(sink_ref, q_ref, k_ref, v_ref, o_ref):
    scale = SWA_HEAD_DIM ** -0.5
    pairs = range(SWA_HEADS // 2)
    items = [(bb, pr) for bb in range(q_ref.shape[0]) for pr in pairs]
    kd = [[x.astype(BF16) for x in _dup_groups(k_ref[bb])] for bb in range(q_ref.shape[0])]
    vt = [[x.T.astype(BF16) for x in _dup_groups(v_ref[bb])] for bb in range(q_ref.shape[0])]
    outs = _pairs_attention([q_ref[bb, :, pr * LANES:(pr + 1) * LANES] * scale for bb, pr in items],
                            [(sink_ref[2 * pr], sink_ref[2 * pr + 1]) for _, pr in items],
                            [kd[bb][pr // 2] for bb, pr in items], [vt[bb][pr // 2] for bb, pr in items],
                            [None] * len(items))
    for (bb, pr), out in zip(items, outs):
        o_ref[bb, :, pr * LANES:(pr + 1) * LANES] = out


def _attn_ctx(sink, q, k, v):
    b, t, _ = q.shape
    bmap = lambda i: (i, 0, 0)
    return pl.pallas_call(
        _attn_ctx_kernel,
        out_shape=jax.ShapeDtypeStruct((b, t, SWA_Q), F32),
        grid=(b // CTX_BATCH,),
        in_specs=[pl.BlockSpec(memory_space=pltpu.SMEM),
                  pl.BlockSpec((CTX_BATCH, t, SWA_Q), bmap),
                  pl.BlockSpec((CTX_BATCH, t, SWA_KV), bmap),
                  pl.BlockSpec((CTX_BATCH, t, SWA_KV), bmap)],
        out_specs=pl.BlockSpec((CTX_BATCH, t, SWA_Q), bmap),
        compiler_params=pltpu.CompilerParams(dimension_semantics=("arbitrary",),
                                             vmem_limit_bytes=VMEM_LIMIT),
        name="attn_ctx",
    )(sink, q, k, v)


LAT_BLOCKS = 2


def _rope(x, cos, sin_lo, sin_hi):
    return x * cos + pltpu.roll(x, LANES - 16, axis=1) * sin_lo + pltpu.roll(x, 16, axis=1) * sin_hi


def _attn_lat_kernel(sink_ref, q_ref, k_ref, v_ref, kc_ref, vc_ref, cos_ref, sl_ref, sh_ref,
                     o_ref, kw_ref, vw_ref):
    t = q_ref.shape[0]
    ab = ATTN_BLOCK
    nb = t // ab
    scale = SWA_HEAD_DIM ** -0.5

    k_rot = _dup_groups(_rope(k_ref[...], cos_ref[...], sl_ref[...], sh_ref[...]))
    v_dup = _dup_groups(v_ref[...])
    zeros = jnp.zeros((ab, LANES), BF16)
    for grp in range(SWA_KV_HEADS):
        kw_ref[grp, 0:ab, :] = zeros
        kw_ref[grp, ab:ab + t, :] = k_rot[grp].astype(BF16)
        kw_ref[grp, ab + t:, :] = zeros
        vw_ref[grp, 0] = zeros
        for blk in range(nb):
            vw_ref[grp, blk + 1] = v_dup[grp][blk * ab:(blk + 1) * ab, :].T.astype(BF16)
        vw_ref[grp, nb + 1] = zeros
    kc = [x.astype(BF16) for x in _dup_groups(kc_ref[...])]
    vct = [x.T.astype(BF16) for x in _dup_groups(vc_ref[...])]
    lc = kc_ref.shape[0]

    key = lax.broadcasted_iota(jnp.int32, (lc + 3 * ab, 2 * ab), 0) - lc
    tq = lax.broadcasted_iota(jnp.int32, (lc + 3 * ab, 2 * ab), 1) & (ab - 1)
    band = (key < 0) | (jnp.abs(tq + ab - key) <= ab)

    def block(it, carry):
        pairs = range(SWA_HEADS // 2)
        qps, sinks, k_dups, vt_dups, masks, places = [], [], [], [], [], []
        for u in range(LAT_BLOCKS):
            nq = it * LAT_BLOCKS + u
            row0 = pl.multiple_of(nq * ab, ab)
            s_abs = key + (nq - 1) * ab
            mask = band & ((key < 0) | ((s_abs >= 0) & (s_abs < t)))
            cos = cos_ref[pl.ds(row0, ab), :]
            s_lo = sl_ref[pl.ds(row0, ab), :]
            s_hi = sh_ref[pl.ds(row0, ab), :]
            k_all = [jnp.concatenate([kc[grp], kw_ref[grp, pl.ds(row0, 3 * ab), :]], axis=0)
                     for grp in range(SWA_KV_HEADS)]
            vt_all = [jnp.concatenate([vct[grp], vw_ref[grp, nq], vw_ref[grp, nq + 1], vw_ref[grp, nq + 2]],
                                      axis=1) for grp in range(SWA_KV_HEADS)]
            for pr in pairs:
                qps.append(_rope(q_ref[pl.ds(row0, ab), pr * LANES:(pr + 1) * LANES], cos, s_lo, s_hi) * scale)
                sinks.append((sink_ref[2 * pr], sink_ref[2 * pr + 1]))
                k_dups.append(k_all[pr // 2])
                vt_dups.append(vt_all[pr // 2])
                masks.append(mask)
                places.append((row0, pr))
        outs = _pairs_attention(qps, sinks, k_dups, vt_dups, masks)
        for (row0, pr), out in zip(places, outs):
            o_ref[pl.ds(row0, ab), pr * LANES:(pr + 1) * LANES] = out
        return carry

    lax.fori_loop(0, nb // LAT_BLOCKS, block, 0)


def _attn_lat(sink, q, k, v, kc, vc, cos, sin_lo, sin_hi):
    b, t, _ = q.shape
    lc = kc.shape[1]
    bmap = lambda i: (i, 0, 0)
    full = lambda i: (0, 0)
    return pl.pallas_call(
        _attn_lat_kernel,
        out_shape=jax.ShapeDtypeStruct((b, t, SWA_Q), F32),
        grid=(b,),
        in_specs=[pl.BlockSpec(memory_space=pltpu.SMEM),
                  pl.BlockSpec((None, t, SWA_Q), bmap),
                  pl.BlockSpec((None, t, SWA_KV), bmap),
                  pl.BlockSpec((None, t, SWA_KV), bmap),
                  pl.BlockSpec((None, lc, SWA_KV), bmap),
                  pl.BlockSpec((None, lc, SWA_KV), bmap),
                  pl.BlockSpec((t, LANES), full),
                  pl.BlockSpec((t, LANES), full),
                  pl.BlockSpec((t, LANES), full)],
        out_specs=pl.BlockSpec((None, t, SWA_Q), bmap),
        scratch_shapes=[pltpu.VMEM((SWA_KV_HEADS, t + 2 * ATTN_BLOCK, LANES), BF16),
                        pltpu.VMEM((SWA_KV_HEADS, t // ATTN_BLOCK + 2, LANES, ATTN_BLOCK), BF16)],
        compiler_params=pltpu.CompilerParams(dimension_semantics=("arbitrary",),
                                             vmem_limit_bytes=VMEM_LIMIT),
        name="attn_lat",
    )(sink, q, k, v, kc, vc, cos, sin_lo, sin_hi)


def _rope_tables(t):
    half = SWA_HEAD_DIM // 2
    quarter = half // 2
    rows = t // GRID_W
    inv_freq = ROPE_BASE ** (-jnp.arange(quarter, dtype=F32) / quarter)
    reps = LANES // quarter
    ang_row = jnp.tile(jnp.arange(rows).astype(F32)[:, None] * inv_freq[None, :], (1, reps))
    ang_col = jnp.tile(jnp.arange(GRID_W).astype(F32)[:, None] * inv_freq[None, :], (1, reps))
    d = jnp.arange(LANES) % SWA_HEAD_DIM
    use_row = (d < half)[None, :]
    lower = ((d % half) < quarter)[None, :]

    def expand(f):
        by_row = jnp.repeat(f(ang_row), GRID_W, axis=0)
        by_col = jnp.tile(f(ang_col), (rows, 1))
        return jnp.where(use_row, by_row, by_col)

    cos = expand(jnp.cos)
    sin = expand(jnp.sin)
    return cos, jnp.where(lower, -sin, 0.0), jnp.where(lower, 0.0, sin)


def _route(sel, scores):
    n = sel.shape[1]
    gsz = N_EXPERTS // N_EXPERT_GROUPS

    def first_max(x, idx, size):
        m = jnp.max(x, axis=0, keepdims=True)
        first = jnp.min(jnp.where(x == m, idx, float(size)), axis=0, keepdims=True)
        return m, idx == first

    i8 = lax.broadcasted_iota(jnp.int32, (gsz, n), 0).astype(F32)
    rows = []
    for g in range(N_EXPERT_GROUPS):
        slab = sel[g * gsz:(g + 1) * gsz, :]
        m1, hit = first_max(slab, i8, gsz)
        m2 = jnp.max(jnp.where(hit, NEG_INF, slab), axis=0, keepdims=True)
        rows.append(m1 + m2)
    gscore = jnp.concatenate(rows, axis=0)
    gsel = jnp.zeros((N_EXPERT_GROUPS, n), F32)
    for _ in range(TOPK_GROUPS):
        _, hit = first_max(gscore, i8, N_EXPERT_GROUPS)
        gsel = jnp.where(hit, 1.0, gsel)
        gscore = jnp.where(hit, NEG_INF, gscore)
    emask = jnp.concatenate(
        [jnp.broadcast_to(gsel[g:g + 1, :], (gsz, n)) for g in range(N_EXPERT_GROUPS)], axis=0)
    cand = jnp.where(emask > 0.5, sel, NEG_INF)
    ie = lax.broadcasted_iota(jnp.int32, (N_EXPERTS, n), 0).astype(F32)
    w = jnp.zeros((N_EXPERTS, n), F32)
    chosen = jnp.zeros((N_EXPERTS, n), F32)
    hits = []
    for _ in range(TOP_K):
        _, hit = first_max(cand, ie, N_EXPERTS)
        hits.append(hit)
        w = jnp.where(hit, scores, w)
        chosen = jnp.where(hit, 1.0, chosen)
        cand = jnp.where(hit, NEG_INF, cand)
    gates = w / jnp.sum(w, axis=0, keepdims=True) * ROUTED_SCALE

    s_idx = lax.broadcasted_iota(jnp.int32, (MOE_TILE, MOE_TILE), 0)
    t_idx = lax.broadcasted_iota(jnp.int32, (MOE_TILE, MOE_TILE), 1)
    tile_shift = MOE_TILE.bit_length() - 1
    before = jnp.where(s_idx < t_idx, 1.0, 0.0).astype(BF16)
    chosen_b = chosen.astype(BF16)
    rank = jnp.concatenate([_dot(chosen_b[:, ti * MOE_TILE:(ti + 1) * MOE_TILE], before)
                            for ti in range(n // MOE_TILE)], axis=1)
    e_row = lax.broadcasted_iota(jnp.int32, (N_EXPERTS, N_EXPERTS), 0)
    e_col = lax.broadcasted_iota(jnp.int32, (N_EXPERTS, N_EXPERTS), 1)
    below = jnp.where(e_col < e_row, 1.0, 0.0).astype(BF16)
    lane_tile = lax.broadcasted_iota(jnp.int32, (1, n), 1) >> tile_shift

    def as_row(col):
        return jnp.concatenate([col, jnp.zeros((LANES - N_EXPERTS, LANES), F32)], axis=0).T[0:1, :]

    sizes, starts = [], []
    first_row = jnp.zeros((N_EXPERTS, n), F32)
    for ti in range(n // MOE_TILE):
        count = jnp.sum(chosen[:, ti * MOE_TILE:(ti + 1) * MOE_TILE], axis=1, keepdims=True)
        padded = jnp.floor((count + (SORT_ALIGN - 1)) * (1.0 / SORT_ALIGN)) * SORT_ALIGN
        padded = jnp.broadcast_to(padded, (N_EXPERTS, LANES))
        start = _dot(below, padded.astype(BF16))
        first_row = jnp.where(lane_tile == ti, start[:, 0:1], first_row)
        sizes.append(as_row(padded))
        starts.append(as_row(start))
    row = first_row + rank
    pos = jnp.concatenate([jnp.sum(jnp.where(h, row, 0.0), axis=0, keepdims=True) for h in hits], axis=0)
    wts = jnp.concatenate([jnp.sum(jnp.where(h, gates, 0.0), axis=0, keepdims=True) for h in hits], axis=0)
    return pos, wts, sizes, starts


def _outproj_kernel(gla_ref, att_ref, x_ref, wo_ref, g1_ref, sh_ref, sc_ref, ng_ref, rw_ref, rwh_ref,
                    rb_ref, x1_ref, xm_ref, pos_ref, wts_ref, cnt_ref, start_ref):
    bb, tb, d = x_ref.shape
    tm = bb * tb
    y = (_dot(gla_ref[...].reshape(tm, GLA_V).astype(BF16), wo_ref[0:GLA_V, :])
         + _dot(att_ref[...].reshape(tm, SWA_Q).astype(BF16), wo_ref[GLA_V:, :]))
    x1 = x_ref[...].reshape(tm, d) + g1_ref[...] * y
    x1_ref[...] = x1.reshape(bb, tb, d)
    xm = _rms_norm(x1, ng_ref[...]) * (1.0 + sc_ref[...]) + sh_ref[...]
    xm_hi, xm_lo = _split_hi_lo(xm)
    xm_ref[...] = xm_hi.reshape(bb, tb, d)
    lg = _dot(xm_hi, rw_ref[...])
    logits = lg[:, :N_EXPERTS] + lg[:, N_EXPERTS:] + _dot(xm_lo, rwh_ref[...])
    lt = jnp.concatenate([logits, jnp.zeros((tm, LANES - N_EXPERTS), F32)], axis=1).T[:N_EXPERTS, :]
    scores = _sigmoid(lt)
    pos, wts, sizes, starts = _route(scores + rb_ref[...], scores)
    tiles_per_batch = tb // MOE_TILE
    for ti in range(tm // MOE_TILE):
        at = (ti // tiles_per_batch, ti % tiles_per_batch)
        pos_ref[at] = pos[:, ti * MOE_TILE:(ti + 1) * MOE_TILE]
        wts_ref[at] = wts[:, ti * MOE_TILE:(ti + 1) * MOE_TILE]
        cnt_ref[at] = sizes[ti]
        start_ref[at] = starts[ti]


OUTPROJ_TILE = 1024


def _outproj(gla_out, att_out, x, w_out, g1, sh2, sc2, norm_g, rw_cat, rw_hi, rbias):
    b, t, d = x.shape
    nmod = g1.shape[0]
    tb = min(t, OUTPROJ_TILE)
    bb = OUTPROJ_TILE // tb if nmod == 1 else 1
    tpb = tb // MOE_TILE
    mod_map = (lambda i, j: (i, 0, 0)) if nmod > 1 else (lambda i, j: (0, 0, 0))
    row = lambda i, j: (i, j, 0)
    full = lambda i, j: (0, 0)
    tile = lambda i, j: (i, j, 0, 0)
    nt = t // MOE_TILE
    return pl.pallas_call(
        _outproj_kernel,
        out_shape=(jax.ShapeDtypeStruct((b, t, d), F32),
                   jax.ShapeDtypeStruct((b, t, d), BF16),
                   jax.ShapeDtypeStruct((b, nt, TOP_K, MOE_TILE), F32),
                   jax.ShapeDtypeStruct((b, nt, TOP_K, MOE_TILE), F32),
                   jax.ShapeDtypeStruct((b, nt, 1, LANES), F32),
                   jax.ShapeDtypeStruct((b, nt, 1, LANES), F32)),
        grid=(b // bb, t // tb),
        in_specs=[pl.BlockSpec((bb, tb, GLA_V), row),
                  pl.BlockSpec((bb, tb, SWA_Q), row),
                  pl.BlockSpec((bb, tb, d), row),
                  pl.BlockSpec((d, d), full, pipeline_mode=pl.Buffered(1)),
                  pl.BlockSpec((None, 1, d), mod_map),
                  pl.BlockSpec((None, 1, d), mod_map),
                  pl.BlockSpec((None, 1, d), mod_map),
                  pl.BlockSpec((1, d), full),
                  pl.BlockSpec((d, 2 * N_EXPERTS), full),
                  pl.BlockSpec((d, N_EXPERTS), full),
                  pl.BlockSpec((N_EXPERTS, 1), full)],
        out_specs=(pl.BlockSpec((bb, tb, d), row),
                   pl.BlockSpec((bb, tb, d), row),
                   pl.BlockSpec((bb, tpb, TOP_K, MOE_TILE), tile),
                   pl.BlockSpec((bb, tpb, TOP_K, MOE_TILE), tile),
                   pl.BlockSpec((bb, tpb, 1, LANES), tile),
                   pl.BlockSpec((bb, tpb, 1, LANES), tile)),
        compiler_params=pltpu.CompilerParams(dimension_semantics=("arbitrary", "arbitrary"),
                                             vmem_limit_bytes=VMEM_LIMIT),
        name="outproj",
    )(gla_out, att_out, x, w_out, g1, sh2, sc2, norm_g, rw_cat, rw_hi, rbias)


MOE_TILE = 256
SORT_ALIGN = 16
SORT_ROWS = 3072
ROW_TILE = 512
GATHER_SLOTS = 9
FFN_CHAINS = 4
COMBINE_CHUNK = 1024
ALWAYS_ROWS = 2560
COMBINE_TAIL = 512


def _moe_sort_kernel(tiles_a, used_ref, xa_ref, xb_ref, pos_ref, xs_hbm, ybuf, osem):
    i = pl.program_id(0)
    last = pl.num_programs(0) - 1
    slot = lax.rem(i, 2)
    x = jnp.where(i < tiles_a, xa_ref[...], xb_ref[...])
    pos = pos_ref[...]
    tm = x.shape[0]
    used = used_ref[i]

    def out_copies(tile, of_slot, start):
        def one(r0, r1):
            row = pl.multiple_of(tile * SORT_ROWS + r0, tm)
            cp = pltpu.make_async_copy(ybuf.at[of_slot, r0:r1, :], xs_hbm.at[pl.ds(row, r1 - r0), :],
                                       osem.at[of_slot])
            if start:
                cp.start()
            else:
                cp.wait()

        one(0, ALWAYS_ROWS)
        for r0 in range(ALWAYS_ROWS, SORT_ROWS, tm):
            pl.when(r0 < used_ref[tile])(functools.partial(one, r0, r0 + tm))

    @pl.when(i >= 2)
    def _():
        out_copies(i - 2, slot, False)

    rows = lax.broadcasted_iota(jnp.int32, (tm, tm), 0).astype(F32).astype(BF16)
    one_bf = jnp.ones((tm, tm), BF16)

    def fill(blk):
        local = (pos - float(blk * tm)).astype(BF16)
        onehot = jnp.zeros((tm, tm), BF16)
        for k in range(TOP_K):
            onehot = jnp.where(rows == local[k:k + 1, :], one_bf, onehot)
        ybuf[slot, blk * tm:(blk + 1) * tm, :] = _dot(onehot, x).astype(BF16)

    for blk in range(SORT_ROWS // tm):
        if (blk + 1) * tm <= ALWAYS_ROWS:
            fill(blk)
        else:
            pl.when(blk * tm < used)(functools.partial(fill, blk))
    out_copies(i, slot, True)

    @pl.when(i == last)
    def _():
        out_copies(i, slot, False)

        @pl.when(i >= 1)
        def _():
            out_copies(i - 1, 1 - slot, False)


def _moe_sort(xm_a, xm_b, pos, used):
    d = xm_a.shape[1]
    nt, _, tm = pos.shape
    tiles_a = xm_a.shape[0] // tm
    grid_spec = pltpu.PrefetchScalarGridSpec(
        num_scalar_prefetch=1,
        grid=(nt,),
        in_specs=[pl.BlockSpec((tm, d), lambda i, u: (jnp.minimum(i, tiles_a - 1), 0)),
                  pl.BlockSpec((tm, d), lambda i, u: (jnp.maximum(i - tiles_a, 0), 0)),
                  pl.BlockSpec((None, TOP_K, tm), lambda i, u: (i, 0, 0))],
        out_specs=pl.BlockSpec(memory_space=pl.ANY),
        scratch_shapes=[pltpu.VMEM((2, SORT_ROWS, d), BF16),
                        pltpu.SemaphoreType.DMA((2,))])
    return pl.pallas_call(
        functools.partial(_moe_sort_kernel, tiles_a),
        out_shape=jax.ShapeDtypeStruct((nt * SORT_ROWS, d), BF16),
        grid_spec=grid_spec,
        compiler_params=pltpu.CompilerParams(dimension_semantics=("arbitrary",),
                                             vmem_limit_bytes=VMEM_LIMIT),
        name="moe_sort",
    )(used, xm_a, xm_b, pos)


def _moe_row_tiles(n_tokens):
    rows = n_tokens * TOP_K + (n_tokens // MOE_TILE) * N_EXPERTS * (SORT_ALIGN - 1) + N_EXPERTS * (ROW_TILE - 1)
    return -(-rows // ROW_TILE) + GATHER_SLOTS - 1


PLAN_CHUNK = 1280


def _int_dot_r(a, onehot):
    hi = jnp.floor(a * (1.0 / 256.0))
    return _dot(hi.astype(BF16), onehot) * 256.0 + _dot((a - hi * 256.0).astype(BF16), onehot)


def _int_dot_l(onehot, b):
    hi = jnp.floor(b * (1.0 / 256.0))
    return _dot(onehot, hi.astype(BF16)) * 256.0 + _dot(onehot, (b - hi * 256.0).astype(BF16))


def _moe_plan_kernel(cnt_ref, start_ref, src_ref, first_ref, tiles_ref, nu_ref, back_ref):
    nt, ne = cnt_ref.shape
    gpt = SORT_ROWS // SORT_ALIGN
    gpr = ROW_TILE // SORT_ALIGN
    gc = cnt_ref[...] * (1.0 / SORT_ALIGN)
    ls = start_ref[...] * (1.0 / SORT_ALIGN)

    def transpose(x):
        x = jnp.concatenate([x, jnp.zeros((nt, LANES - ne), F32)], axis=1)
        x = jnp.concatenate([x, jnp.zeros((LANES - nt, LANES), F32)], axis=0)
        return x.T[:ne, :nt]

    def tri(n, keep):
        return jnp.where(keep(lax.broadcasted_iota(jnp.int32, (n, n), 0),
                              lax.broadcasted_iota(jnp.int32, (n, n), 1)), 1.0, 0.0).astype(BF16)

    gc_t = transpose(gc)
    ls_t = transpose(ls)
    tot_c = jnp.broadcast_to(jnp.sum(gc_t, axis=1, keepdims=True), (ne, LANES))
    ptot_c = jnp.floor((tot_c + (gpr - 1)) * (1.0 / gpr)) * gpr
    gend_c = _int_dot_l(tri(ne, lambda r, c: c <= r), ptot_c)
    gstart_c = gend_c - ptot_c
    n_used = gend_c[ne - 1:ne, :] * (1.0 / gpr)
    nu_ref[...] = n_used.astype(jnp.int32)
    tot_r = jnp.sum(gc, axis=0, keepdims=True)
    ptot_r = jnp.floor((tot_r + (gpr - 1)) * (1.0 / gpr)) * gpr
    gstart_r = _int_dot_r(jnp.broadcast_to(ptot_r, (8, ne)), tri(ne, lambda r, c: r < c))
    cumex = _dot(tri(nt, lambda r, c: c < r), gc.astype(BF16))
    cumex_t = _dot(gc_t.astype(BF16), tri(nt, lambda r, c: r < c))
    tile_base = lax.broadcasted_iota(jnp.int32, (nt, ne), 0).astype(F32) * gpt + ls
    table = jnp.concatenate([cumex + gc, cumex, tile_base, gstart_r, jnp.broadcast_to(tot_r, (8, ne))], axis=0)

    e_iota = lax.broadcasted_iota(jnp.int32, (ne, PLAN_CHUNK), 0).astype(F32)
    for ch in range(src_ref.shape[1] // PLAN_CHUNK):
        g = (lax.broadcasted_iota(jnp.int32, (1, PLAN_CHUNK), 1) + ch * PLAN_CHUNK).astype(F32)
        eg = jnp.sum(jnp.where(gend_c[:, 0:1] <= g, 1.0, 0.0), axis=0, keepdims=True)
        picked = _int_dot_r(table, jnp.where(e_iota == eg, 1.0, 0.0).astype(BF16))
        cum_g, cumex_g, base_g = picked[0:nt], picked[nt:2 * nt], picked[2 * nt:3 * nt]
        u = g - picked[3 * nt:3 * nt + 1]
        in_tile = (cumex_g <= u) & (u < cum_g)
        src = jnp.sum(jnp.where(in_tile, base_g - cumex_g, 0.0), axis=0, keepdims=True) + u
        src = jnp.where(u < picked[3 * nt + 8:3 * nt + 9], src, 0.0)
        src_ref[:, ch * PLAN_CHUNK:(ch + 1) * PLAN_CHUNK] = src.astype(jnp.int32)

    first_ref[...] = (gstart_c * (1.0 / gpr)).astype(jnp.int32)
    tiles_ref[...] = (ptot_c * (1.0 / gpr)).astype(jnp.int32)

    lg = lax.broadcasted_iota(jnp.int32, (ne, back_ref.shape[1]), 1).astype(F32)
    for t in range(nt):
        first = ls_t[:, t:t + 1]
        inside = (first <= lg) & (lg < first + gc_t[:, t:t + 1])
        shift = gstart_c[:, 0:1] + cumex_t[:, t:t + 1] - first
        val = jnp.sum(jnp.where(inside, shift + lg, 0.0), axis=0, keepdims=True)
        back_ref[t:t + 1, :] = val.astype(jnp.int32)


def _moe_plan(cnt, start):
    nt, ne = cnt.shape
    row_tiles = _moe_row_tiles(nt * MOE_TILE)
    gpt = SORT_ROWS // SORT_ALIGN
    gpr = ROW_TILE // SORT_ALIGN
    n_src = -(-(row_tiles * gpr) // PLAN_CHUNK) * PLAN_CHUNK
    n_back = -(-gpt // LANES) * LANES
    src, first, tiles, nu, back = pl.pallas_call(
        _moe_plan_kernel,
        out_shape=(jax.ShapeDtypeStruct((1, n_src), jnp.int32),
                   jax.ShapeDtypeStruct((ne, LANES), jnp.int32),
                   jax.ShapeDtypeStruct((ne, LANES), jnp.int32),
                   jax.ShapeDtypeStruct((1, LANES), jnp.int32),
                   jax.ShapeDtypeStruct((nt, n_back), jnp.int32)),
        compiler_params=pltpu.CompilerParams(vmem_limit_bytes=VMEM_LIMIT),
        name="moe_plan",
    )(cnt, start)
    return nu[0, :1], first[:, 0], tiles[:, 0], src[0, :row_tiles * gpr], back[:, :gpt]


def _moe_experts_kernel(nu_ref, first_ref, tiles_ref, src_ref, xs_hbm, wg_ref, wu_ref, wd_ref, ys_hbm,
                        xbuf, ybuf, gsem, osem, wgu_s, wd_s):
    e = pl.program_id(0)
    n_used = nu_ref[0]
    gpr = ROW_TILE // SORT_ALIGN
    part = ROW_TILE // FFN_CHAINS

    def gather(tile, to_slot, j0=0, j1=gpr):
        for j in range(j0, j1):
            row = pl.multiple_of(src_ref[tile * gpr + j] * SORT_ALIGN, SORT_ALIGN)
            pltpu.make_async_copy(xs_hbm.at[pl.ds(row, SORT_ALIGN), :],
                                  xbuf.at[to_slot, j * SORT_ALIGN:(j + 1) * SORT_ALIGN, :],
                                  gsem.at[to_slot]).start(priority=j % 2)

    def drain(of_slot):
        for j in range(gpr):
            pltpu.make_async_copy(xs_hbm.at[0:SORT_ALIGN, :],
                                  xbuf.at[of_slot, j * SORT_ALIGN:(j + 1) * SORT_ALIGN, :], gsem.at[of_slot]).wait()

    def out_copy(tile, of_slot):
        row = pl.multiple_of(tile * ROW_TILE, ROW_TILE)
        return pltpu.make_async_copy(ybuf.at[of_slot], ys_hbm.at[pl.ds(row, ROW_TILE), :], osem.at[of_slot])

    @pl.when(e == 0)
    def _():
        for ahead in range(GATHER_SLOTS - 1):
            gather(ahead, ahead)

    wgu_s[:, :EXPERT_FF] = wg_ref[...].astype(BF16)
    wgu_s[:, EXPERT_FF:] = wu_ref[...].astype(BF16)
    wd_s[...] = wd_ref[...].astype(BF16)

    def row_tile(i, carry):
        r = first_ref[e] + i
        slot = lax.rem(r, GATHER_SLOTS)
        oslot = lax.rem(r, 2)
        next_slot = lax.rem(r + GATHER_SLOTS - 1, GATHER_SLOTS)
        drain(slot)

        @pl.when(r >= 2)
        def _():
            out_copy(r - 2, oslot).wait()

        abs_ = []
        for c in range(FFN_CHAINS):
            abs_.append(_dot(xbuf[slot, c * part:(c + 1) * part, :], wgu_s[...]))
            gather(r + GATHER_SLOTS - 1, next_slot, c * gpr // FFN_CHAINS, (c + 1) * gpr // FFN_CHAINS)
        hs = [(_silu(ab[:, :EXPERT_FF]) * ab[:, EXPERT_FF:]).astype(BF16) for ab in abs_]
        ys = [_dot(h, wd_s[...]).astype(BF16) for h in hs]
        for c in range(FFN_CHAINS):
            ybuf[oslot, c * part:(c + 1) * part, :] = ys[c]
        out_copy(r, oslot).start()
        return carry

    lax.fori_loop(0, tiles_ref[e], row_tile, 0)

    @pl.when(e == pl.num_programs(0) - 1)
    def _():
        for ahead in range(GATHER_SLOTS - 1):
            drain(lax.rem(n_used + ahead, GATHER_SLOTS))
        out_copy(n_used - 1, lax.rem(n_used - 1, 2)).wait()

        @pl.when(n_used >= 2)
        def _():
            out_copy(n_used - 2, lax.rem(n_used, 2)).wait()


def _moe_experts(n_used, first, tiles, src, xs, wg, wu, wd, row_tiles):
    d = xs.shape[-1]
    ne = wg.shape[0]
    w_map = lambda e, nu, fi, ti, sr: (e, 0, 0)
    grid_spec = pltpu.PrefetchScalarGridSpec(
        num_scalar_prefetch=4,
        grid=(ne,),
        in_specs=[pl.BlockSpec(memory_space=pl.ANY),
                  pl.BlockSpec((None, d, EXPERT_FF), w_map),
                  pl.BlockSpec((None, d, EXPERT_FF), w_map),
                  pl.BlockSpec((None, EXPERT_FF, d), w_map)],
        out_specs=pl.BlockSpec(memory_space=pl.ANY),
        scratch_shapes=[pltpu.VMEM((GATHER_SLOTS, ROW_TILE, d), BF16),
                        pltpu.VMEM((2, ROW_TILE, d), BF16),
                        pltpu.SemaphoreType.DMA((GATHER_SLOTS,)),
                        pltpu.SemaphoreType.DMA((2,)),
                        pltpu.VMEM((d, 2 * EXPERT_FF), BF16),
                        pltpu.VMEM((EXPERT_FF, d), BF16)])
    return pl.pallas_call(
        _moe_experts_kernel,
        out_shape=jax.ShapeDtypeStruct((row_tiles * ROW_TILE, d), BF16),
        grid_spec=grid_spec,
        compiler_params=pltpu.CompilerParams(dimension_semantics=("arbitrary",),
                                             vmem_limit_bytes=VMEM_LIMIT),
        name="moe_experts",
    )(n_used, first, tiles, src, xs, wg, wu, wd)


def _moe_combine_kernel(back_ref, used_ref, ys_hbm, pos_ref, wts_ref, xm_ref, x1_ref, g2_ref, fg_ref,
                        swg_ref, swu_ref, swd_ref, o_ref, buf, sem, acc_ref):
    i = pl.program_id(0)
    gpt = SORT_ROWS // SORT_ALIGN
    slot = lax.rem(i, 2)
    always = ALWAYS_ROWS
    tail = range(always, SORT_ROWS, COMBINE_TAIL)

    def copies(tile, of_slot, g0, g1, start):
        for g in range(g0, g1):
            row = pl.multiple_of(back_ref[tile * gpt + g] * SORT_ALIGN, SORT_ALIGN) if start else 0
            cp = pltpu.make_async_copy(ys_hbm.at[pl.ds(row, SORT_ALIGN), :],
                                       buf.at[of_slot, g * SORT_ALIGN:(g + 1) * SORT_ALIGN, :], sem.at[of_slot])
            if start:
                cp.start(priority=g % 2)
            else:
                cp.wait()

    def transfer_tail(tile, of_slot, start):
        for c0 in tail:
            pl.when(c0 < used_ref[tile])(functools.partial(
                copies, tile, of_slot, c0 // SORT_ALIGN, (c0 + COMBINE_TAIL) // SORT_ALIGN, start))

    def transfer(tile, of_slot, start):
        copies(tile, of_slot, 0, always // SORT_ALIGN, start)
        transfer_tail(tile, of_slot, start)

    @pl.when(i == 0)
    def _():
        transfer(0, 0, True)

    nxt = jnp.minimum(i + 1, pl.num_programs(0) - 1)
    nxt_slot = 1 - slot
    transfer_tail(nxt, nxt_slot, True)
    always_groups = always // SORT_ALIGN
    spread = [always_groups * part // 4 for part in range(5)]

    x = xm_ref[...]
    tm = x.shape[0]
    pad = jnp.zeros((LANES - TOP_K, tm), F32)
    pos_t = jnp.concatenate([pos_ref[...], pad], axis=0).T
    wts_t = jnp.concatenate([wts_ref[...], pad], axis=0).T
    copies(nxt, nxt_slot, spread[0], spread[1], True)
    blk_b, loc_b, wts_b = [], [], []
    for k in range(TOP_K):
        p = jnp.broadcast_to(pos_t[:, k:k + 1], (tm, LANES))
        blk = jnp.floor(p * (1.0 / tm))
        two = lambda v: jnp.concatenate([v.astype(BF16)] * (tm // LANES), axis=1)
        blk_b.append(two(blk))
        loc_b.append(two(p - blk * tm))
        wts_b.append(two(jnp.broadcast_to(wts_t[:, k:k + 1], (tm, LANES))))
    copies(nxt, nxt_slot, spread[1], spread[2], True)
    shared = _dot((_silu(_dot(x, swg_ref[...])) * _dot(x, swu_ref[...])).astype(BF16), swd_ref[...])
    copies(nxt, nxt_slot, spread[2], spread[3], True)
    transfer(i, slot, False)
    lane = lax.broadcasted_iota(jnp.int32, (tm, tm), 1).astype(F32).astype(BF16)
    zero = jnp.zeros((tm, tm), BF16)
    nowhere = jnp.full((tm, tm), -1.0, BF16)

    def apply(c0, width):
        blocks = []
        for b0 in range(c0, c0 + width, tm):
            comb = zero
            for k in range(TOP_K):
                loc = jnp.where(blk_b[k] == float(b0 // tm), loc_b[k], nowhere)
                comb = jnp.where(lane == loc, wts_b[k], comb)
            blocks.append(comb)
        return _dot(jnp.concatenate(blocks, axis=1), buf[slot, c0:c0 + width, :])

    routed = shared
    for c0 in range(0, always, COMBINE_CHUNK):
        routed = routed + apply(c0, min(COMBINE_CHUNK, always - c0))
        if c0 == 0:
            copies(nxt, nxt_slot, spread[3], spread[4], True)
    acc_ref[...] = routed
    for c0 in tail:
        @pl.when(c0 < used_ref[i])
        def _(c0=c0):
            acc_ref[...] += apply(c0, COMBINE_TAIL)
    y = x1_ref[...] + g2_ref[...] * acc_ref[...]
    o_ref[...] = _rms_norm(y, fg_ref[...])

    @pl.when(i == pl.num_programs(0) - 1)
    def _():
        transfer(i, nxt_slot, False)


def _moe_combine(back, used, ys, pos, wts, xm, x1, g2, final_g, swg, swu, swd, *, tiles_per_mod):
    n, d = xm.shape
    tm = pos.shape[-1]
    nt = n // tm
    gpt = SORT_ROWS // SORT_ALIGN
    row = lambda i, bk, us: (i, 0)
    full = lambda i, bk, us: (0, 0)
    tile = lambda i, bk, us: (i, 0, 0)
    mod_map = lambda i, bk, us: (i // tiles_per_mod, 0, 0)
    grid_spec = pltpu.PrefetchScalarGridSpec(
        num_scalar_prefetch=2,
        grid=(nt,),
        in_specs=[pl.BlockSpec(memory_space=pl.ANY),
                  pl.BlockSpec((None, TOP_K, tm), tile),
                  pl.BlockSpec((None, TOP_K, tm), tile),
                  pl.BlockSpec((tm, d), row),
                  pl.BlockSpec((tm, d), row),
                  pl.BlockSpec((None, 1, d), mod_map),
                  pl.BlockSpec((1, d), full),
                  pl.BlockSpec((d, SHARED_FF), full),
                  pl.BlockSpec((d, SHARED_FF), full),
                  pl.BlockSpec((SHARED_FF, d), full)],
        out_specs=pl.BlockSpec((tm, d), row),
        scratch_shapes=[pltpu.VMEM((2, SORT_ROWS, d), BF16),
                        pltpu.SemaphoreType.DMA((2,)),
                        pltpu.VMEM((tm, d), F32)])
    return pl.pallas_call(
        _moe_combine_kernel,
        out_shape=jax.ShapeDtypeStruct((n, d), F32),
        grid_spec=grid_spec,
        compiler_params=pltpu.CompilerParams(dimension_semantics=("arbitrary",),
                                             vmem_limit_bytes=VMEM_LIMIT),
        name="moe_combine",
    )(back, used, ys, pos.reshape(nt, TOP_K, tm), wts.reshape(nt, TOP_K, tm), xm, x1, g2, final_g, swg, swu, swd)


def _mix(x, mods, p, attn_fn, s0=None):
    sh1, sc1, g1, sh2, sc2, _ = mods
    gla_in, lora, q_s, k_s, v_s = _inproj(x, p["norm_attn_g"], sh1, sc1, p["w_gla"], p["w_lora"], p["w_swa"])
    if s0 is None:
        gla_out, s_f, s_b = _gla(gla_in, lora, p["waf"], p["baf"], p["wab"], p["bab"], p["gla_norm_g"])
    else:
        gla_out, s_f, s_b = _gla(gla_in, lora, p["waf"], p["baf"], p["wab"], p["bab"], p["gla_norm_g"],
                                 s0[0], s0[1])
    att_out = attn_fn(q_s, k_s, v_s)
    routed = _outproj(gla_out, att_out, x, p["w_out"], g1, sh2, sc2, p["norm_ffn_g"],
                      p["rw_cat"], p["rw_hi"], p["rbias"])
    return routed, k_s, v_s, s_f, s_b


def _moe(streams, p):
    d = D_MODEL
    (ra, _), (rb, _) = streams
    n_tiles = [r[1].shape[0] * r[1].shape[1] // MOE_TILE for r, _ in streams]
    pos_all = jnp.concatenate([r[2].reshape(-1, TOP_K, MOE_TILE) for r, _ in streams], axis=0)
    cnt_all = jnp.concatenate([r[4].reshape(-1, LANES) for r, _ in streams], axis=0)[:, :N_EXPERTS]
    start_all = jnp.concatenate([r[5].reshape(-1, LANES) for r, _ in streams], axis=0)[:, :N_EXPERTS]
    used = (start_all[:, -1] + cnt_all[:, -1]).astype(jnp.int32)
    xs = _moe_sort(ra[1].reshape(-1, d), rb[1].reshape(-1, d), pos_all, used)
    n_used, first, tiles, src, back = _moe_plan(cnt_all, start_all)
    ys = _moe_experts(n_used, first, tiles, src, xs, p["wg"], p["wu"], p["wd"],
                      _moe_row_tiles(cnt_all.shape[0] * MOE_TILE))
    outs = []
    tile0 = 0
    for ((x1, xm, pos, wts, cnt, start), g2), nt in zip(streams, n_tiles):
        b, t, _ = x1.shape
        tiles_per_mod = (t // MOE_TILE) if g2.shape[0] > 1 else nt
        y = _moe_combine(back[tile0:tile0 + nt].reshape(-1), used[tile0:tile0 + nt], ys, pos, wts,
                         xm.reshape(-1, d), x1.reshape(-1, d), g2, p["final_norm_g"],
                         p["swg"], p["swu"], p["swd"], tiles_per_mod=tiles_per_mod)
        outs.append(y.reshape(b, t, d))
        tile0 += nt
    return outs


def kernel(x_prompt, x_sample, c, cache_swa_k, cache_swa_v, state_gla_fwd, state_gla_bwd, c_ctx, w_ada, b_ada, norm_attn_g, norm_ffn_g, w_in, gla_wa_f, gla_ba_f, gla_wa_b, gla_ba_b, gla_norm_g, swa_sink, w_out, router_w, router_bias, exp_w_gate, exp_w_up, exp_w_down, sh_w_gate, sh_w_up, sh_w_down, final_norm_g):
    l = 0
    d = D_MODEL
    nb_ctx, t_ctx, _ = x_prompt.shape
    nb_lat, t_lat, _ = x_sample.shape

    pad = jnp.zeros((8 - 1 - nb_lat, d), F32)
    cond8 = jnp.concatenate([c_ctx[None, :], c, pad], axis=0)
    mod = _adaln(cond8, w_ada[l], b_ada[l][None, :])
    mods_ctx = [mod[0:1, i * d:(i + 1) * d][:, None, :] for i in range(6)]
    mods_lat = [mod[1:1 + nb_lat, i * d:(i + 1) * d][:, None, :] for i in range(6)]

    zeros_lora = jnp.zeros((GLA_LORA, GLA_QK), F32)
    rw = router_w[l]
    rw_hi = rw.astype(BF16)
    rw_lo = (rw - rw_hi.astype(F32)).astype(BF16)
    n_gla = 2 * GLA_QK + 2 * GLA_V
    p = {
        "norm_attn_g": norm_attn_g[l][None, :],
        "norm_ffn_g": norm_ffn_g[l][None, :],
        "final_norm_g": final_norm_g[None, :],
        "w_gla": w_in[l][:, :n_gla].astype(BF16),
        "w_lora": w_in[l][:, n_gla:n_gla + 2 * GLA_LORA].astype(BF16),
        "w_swa": w_in[l][:, n_gla + 2 * GLA_LORA:].astype(BF16),
        "waf": jnp.concatenate([gla_wa_f[l], zeros_lora], axis=0).astype(BF16),
        "wab": jnp.concatenate([zeros_lora, gla_wa_b[l]], axis=0).astype(BF16),
        "baf": gla_ba_f[l][None, :],
        "bab": gla_ba_b[l][None, :],
        "gla_norm_g": gla_norm_g[l][None, :],
        "w_out": w_out[l].astype(BF16),
        "rw_cat": jnp.concatenate([rw_hi, rw_lo], axis=1),
        "rw_hi": rw_hi,
        "rbias": router_bias[l][:, None],
        "wg": exp_w_gate[l], "wu": exp_w_up[l], "wd": exp_w_down[l],
        "swg": sh_w_gate[l].astype(BF16), "swu": sh_w_up[l].astype(BF16),
        "swd": sh_w_down[l].astype(BF16),
    }
    sink = swa_sink[l]

    routed_ctx, k_c, v_c, s_f, s_b = _mix(x_prompt, mods_ctx, p, functools.partial(_attn_ctx, sink))

    cos, sin_lo, sin_hi = _rope_tables(t_lat)
    kc = cache_swa_k[:, l].reshape(nb_lat, -1, SWA_KV)
    vc = cache_swa_v[:, l].reshape(nb_lat, -1, SWA_KV)
    lat_attn = lambda q, k, v: _attn_lat(sink, q, k, v, kc, vc, cos, sin_lo, sin_hi)
    s0 = (state_gla_fwd[:, l].reshape(nb_lat, GLA_QK, GLA_DV),
          state_gla_bwd[:, l].reshape(nb_lat, GLA_QK, GLA_DV))
    routed_lat, _, _, _, _ = _mix(x_sample, mods_lat, p, lat_attn, s0)
    y_prompt, y_sample = _moe([(routed_ctx, mods_ctx[5]), (routed_lat, mods_lat[5])], p)

    new_k = k_c.reshape(nb_ctx, 1, t_ctx, SWA_KV_HEADS, SWA_HEAD_DIM)
    new_v = v_c.reshape(nb_ctx, 1, t_ctx, SWA_KV_HEADS, SWA_HEAD_DIM)
    new_sf = s_f.reshape(nb_ctx, 1, GLA_HEADS, GLA_DK, GLA_DV)
    new_sb = s_b.reshape(nb_ctx, 1, GLA_HEADS, GLA_DK, GLA_DV)
    return (y_prompt, y_sample, new_k, new_v, new_sf, new_sb)
```
